```python
import math
import jax, jax.numpy as jnp
from jax import lax
import numpy as np

D_MODEL = 1024
BATCH = 4
SEQ = 4096
DEPTH = 1

ATTN_GROUPS = ((128, 1), (512, 4), (2048, 16))
N_ATTN_GROUPS = len(ATTN_GROUPS)
ATTN_HEADS = 8
ATTN_HEAD_DIM = D_MODEL // 16
ATTN_GROUP_WIDTH = ATTN_HEADS * ATTN_HEAD_DIM
ATTN_QKV_WIDTH = N_ATTN_GROUPS * ATTN_GROUP_WIDTH
RET_HEADS = 4
RET_KEY_DIM = D_MODEL // 8
RET_VALUE_DIM = 2 * RET_KEY_DIM
RET_QK_WIDTH = RET_HEADS * RET_KEY_DIM
RET_V_WIDTH = RET_HEADS * RET_VALUE_DIM
RET_CHUNK = 128
N_BRANCHES = 2
IN_SPLITS = (ATTN_QKV_WIDTH, ATTN_QKV_WIDTH, ATTN_QKV_WIDTH,
             RET_QK_WIDTH, RET_QK_WIDTH, RET_V_WIDTH, RET_V_WIDTH,
             D_MODEL, D_MODEL)
IN_WIDTH = sum(IN_SPLITS)
N_EXPERT_GROUPS = 4
EXPERTS_PER_GROUP = 8
N_EXPERTS = N_EXPERT_GROUPS * EXPERTS_PER_GROUP
TOP_K_IN_GROUP = 2
EXPERT_FF = D_MODEL // 2
EPS = 1e-6

kernel_name = "hybrid_dilated_attn_retention_hmoe_block"


def rms_norm(x, g):
    xf = x.astype(jnp.float32)
    y = xf * lax.rsqrt(jnp.mean(xf * xf, axis=-1, keepdims=True) + EPS)
    return (y * g.astype(jnp.float32)).astype(x.dtype)


def alibi_slopes(n):
    return jnp.exp2(-8.0 * jnp.arange(1, n + 1, dtype=jnp.float32) / n)


def dilated_window_attention(q, k, v, slopes, window, dilation):
    B, S, H, Dh = q.shape
    n_back = window // dilation
    blk = n_back
    L = S // dilation
    nb = -(-L // blk)
    Lp = nb * blk

    def to_blocks(t):
        t = t.reshape(B, L, dilation, H, Dh)
        t = jnp.pad(t, ((0, 0), (0, Lp - L), (0, 0), (0, 0), (0, 0)))
        t = t.reshape(B, nb, blk, dilation, H, Dh)
        return t.transpose(0, 3, 4, 1, 2, 5)

    def with_prev(t):
        prev = jnp.pad(t, ((0, 0), (0, 0), (0, 0), (1, 0), (0, 0), (0, 0)))[:, :, :, :-1]
        return jnp.concatenate([prev, t], axis=4)

    qb, kb, vb = to_blocks(q), to_blocks(k), to_blocks(v)
    kk, vv = with_prev(kb), with_prev(vb)
    s = jnp.einsum('brhnqc,brhnkc->brhnqk', qb, kk,
                   preferred_element_type=jnp.float32) * (Dh ** -0.5)
    qi = jnp.arange(blk)[:, None]
    kj = jnp.arange(2 * blk)[None, :]
    rel = qi + blk - kj
    blk_idx = jnp.arange(nb)[:, None, None]
    valid = (rel >= 0) & (rel <= n_back) & ((blk_idx > 0) | (kj >= blk))
    bias = -slopes[:, None, None, None] * (rel * dilation).astype(jnp.float32)
    s = jnp.where(valid, s + bias, -jnp.inf)
    m = jnp.max(s, axis=-1, keepdims=True)
    p = jnp.exp(s - m)
    den = jnp.sum(p, axis=-1, keepdims=True)
    o = jnp.einsum('brhnqk,brhnkc->brhnqc', p, vv.astype(jnp.float32)) / den
    lse = (m + jnp.log(den))[..., 0]
    o = o.transpose(0, 3, 4, 1, 2, 5).reshape(B, Lp, dilation, H, Dh)[:, :L]
    lse = lse.transpose(0, 3, 4, 1, 2).reshape(B, Lp, dilation, H)[:, :L]
    return o.reshape(B, S, H, Dh), lse.reshape(B, S, H)


def retention_chunkwise(q, k, v):
    B, S, H, Dk = q.shape
    Dv = v.shape[-1]
    C = RET_CHUNK
    nc = S // C
    log_g = jnp.log1p(-jnp.exp2(-5.0 - jnp.arange(H, dtype=jnp.float32)))
    pos = jnp.arange(C, dtype=jnp.float32)
    diff = pos[:, None] - pos[None, :]
    decay_in = jnp.where(diff >= 0, jnp.exp(log_g[:, None, None] * jnp.maximum(diff, 0.0)), 0.0)
    xi = jnp.exp(log_g[:, None] * (pos + 1.0))[..., None]
    zeta = jnp.exp(log_g[:, None] * (C - 1.0 - pos))[..., None]
    g_chunk = jnp.exp(log_g * C)[:, None, None]

    def chunks(t):
        return t.astype(jnp.float32).reshape(B, nc, C, H, t.shape[-1]).transpose(1, 0, 3, 2, 4)

    qc, kc, vc = chunks(q), chunks(k) * (Dk ** -0.5), chunks(v)

    def step(state, inp):
        qi, ki, vi = inp
        att = jnp.einsum('bhqd,bhkd->bhqk', qi, ki) * decay_in
        inner = jnp.einsum('bhqk,bhkv->bhqv', att, vi)
        cross = jnp.einsum('bhqd,bhdv->bhqv', qi, state) * xi
        state = g_chunk * state + jnp.einsum('bhkd,bhkv->bhdv', ki * zeta, vi)
        return state, inner + cross

    state0 = jnp.zeros((B, H, Dk, Dv), jnp.float32)
    _, y = lax.scan(step, state0, (qc, kc, vc))
    return y.transpose(1, 0, 3, 2, 4).reshape(B, S, H, Dv)


def hier_moe(h, w_gr, b_gr, w_er, b_er, w_gate, w_up, w_down):
    N = h.shape[0]
    g_prob = jax.nn.softmax((h @ w_gr + b_gr).astype(jnp.float32), axis=-1)
    g_val, g_idx = lax.top_k(g_prob, 1)
    g_onehot = jax.nn.one_hot(g_idx[:, 0], N_EXPERT_GROUPS, dtype=jnp.float32)
    e_logit = (h @ w_er + b_er).astype(jnp.float32).reshape(N, N_EXPERT_GROUPS, EXPERTS_PER_GROUP)
    e_sel = jnp.einsum('nge,ng->ne', e_logit, g_onehot)
    e_val, e_idx = lax.top_k(e_sel, TOP_K_IN_GROUP)
    e_w = jax.nn.softmax(e_val, axis=-1)
    e_gate = jnp.sum(e_w[..., None] * jax.nn.one_hot(e_idx, EXPERTS_PER_GROUP, dtype=jnp.float32), axis=1)
    gate = (g_val[:, 0:1] * g_onehot)[:, :, None] * e_gate[:, None, :]
    wg = w_gate.reshape(N_EXPERT_GROUPS, EXPERTS_PER_GROUP, D_MODEL, EXPERT_FF)
    wu = w_up.reshape(N_EXPERT_GROUPS, EXPERTS_PER_GROUP, D_MODEL, EXPERT_FF)
    wd = w_down.reshape(N_EXPERT_GROUPS, EXPERTS_PER_GROUP, EXPERT_FF, D_MODEL)
    y = jnp.zeros((N, D_MODEL), jnp.float32)
    for grp in range(N_EXPERT_GROUPS):
        a = jnp.einsum('nd,edf->enf', h, wg[grp])
        u = jnp.einsum('nd,edf->enf', h, wu[grp])
        hid = jax.nn.silu(a) * u * gate[:, grp].T[:, :, None]
        y = y + jnp.einsum('enf,efd->nd', hid, wd[grp])
    return y


def setup_inputs(seed: int = 0) -> dict:
    key = jax.random.key(seed)
    ks = jax.random.split(key, 16)
    f32 = jnp.float32
    nrm = lambda k, shape, fan_in: jax.random.normal(k, shape, f32) * (fan_in ** -0.5)
    gain = lambda k, shape: 1.0 + 0.01 * jax.random.normal(k, shape, f32)
    return {
        "x": jax.random.normal(ks[0], (BATCH, SEQ, D_MODEL), f32),
        "g_mix": gain(ks[1], (DEPTH, D_MODEL)),
        "w_in": nrm(ks[2], (DEPTH, D_MODEL, IN_WIDTH), D_MODEL),
        "w_attn_branch": nrm(ks[3], (DEPTH, ATTN_GROUP_WIDTH, D_MODEL), ATTN_GROUP_WIDTH),
        "w_ret_branch": nrm(ks[4], (DEPTH, RET_V_WIDTH, D_MODEL), RET_V_WIDTH),
        "w_out": nrm(ks[5], (DEPTH, D_MODEL, D_MODEL), D_MODEL),
        "g_ffn": gain(ks[6], (DEPTH, D_MODEL)),
        "w_group_router": nrm(ks[7], (DEPTH, D_MODEL, N_EXPERT_GROUPS), D_MODEL),
        "b_group_router": 0.01 * jax.random.normal(ks[8], (DEPTH, N_EXPERT_GROUPS), f32),
        "w_expert_router": nrm(ks[9], (DEPTH, D_MODEL, N_EXPERTS), D_MODEL),
        "b_expert_router": 0.01 * jax.random.normal(ks[10], (DEPTH, N_EXPERTS), f32),
        "w_gate": nrm(ks[11], (DEPTH, N_EXPERTS, D_MODEL, EXPERT_FF), D_MODEL),
        "w_up": nrm(ks[12], (DEPTH, N_EXPERTS, D_MODEL, EXPERT_FF), D_MODEL),
        "w_down": nrm(ks[13], (DEPTH, N_EXPERTS, EXPERT_FF, D_MODEL), EXPERT_FF),
        "g_final": gain(ks[14], (D_MODEL,)),
    }


def reference(x, g_mix, w_in, w_attn_branch, w_ret_branch, w_out, g_ffn,
              w_group_router, b_group_router, w_expert_router, b_expert_router,
              w_gate, w_up, w_down, g_final):
    B, S, D = x.shape
    slopes = alibi_slopes(ATTN_HEADS)
    split_at = np.cumsum(np.array(IN_SPLITS))[:-1].tolist()
    for layer in range(DEPTH):
        h = rms_norm(x, g_mix[layer])
        proj = h @ w_in[layer]
        qa, ka, va, qr, kr, vr, gr_swish, gate_a, gate_r = jnp.split(proj, split_at, axis=-1)
        qa = qa.reshape(B, S, N_ATTN_GROUPS, ATTN_HEADS, ATTN_HEAD_DIM)
        ka = ka.reshape(B, S, N_ATTN_GROUPS, ATTN_HEADS, ATTN_HEAD_DIM)
        va = va.reshape(B, S, N_ATTN_GROUPS, ATTN_HEADS, ATTN_HEAD_DIM)
        outs, lses = [], []
        for gi, (window, dilation) in enumerate(ATTN_GROUPS):
            o, l = dilated_window_attention(qa[:, :, gi], ka[:, :, gi], va[:, :, gi],
                                            slopes, window, dilation)
            outs.append(o)
            lses.append(l)
        mix_w = jax.nn.softmax(jnp.stack(lses, axis=0), axis=0)
        o_attn = jnp.einsum('gbsh,gbshc->bshc', mix_w, jnp.stack(outs, axis=0))
        o_attn = o_attn.reshape(B, S, ATTN_GROUP_WIDTH).astype(x.dtype)

        yr = retention_chunkwise(qr.reshape(B, S, RET_HEADS, RET_KEY_DIM),
                                 kr.reshape(B, S, RET_HEADS, RET_KEY_DIM),
                                 vr.reshape(B, S, RET_HEADS, RET_VALUE_DIM))
        mu = jnp.mean(yr, axis=-1, keepdims=True)
        var = jnp.mean(jnp.square(yr - mu), axis=-1, keepdims=True)
        yr = ((yr - mu) * lax.rsqrt(var + EPS)).reshape(B, S, RET_V_WIDTH)
        o_ret = (jax.nn.silu(gr_swish.astype(jnp.float32)) * yr).astype(x.dtype)

        merged = (jax.nn.sigmoid(gate_a) * (o_attn @ w_attn_branch[layer])
                  + jax.nn.sigmoid(gate_r) * (o_ret @ w_ret_branch[layer]))
        x = x + merged @ w_out[layer]

        h2 = rms_norm(x, g_ffn[layer]).reshape(B * S, D)
        y = hier_moe(h2, w_group_router[layer], b_group_router[layer],
                     w_expert_router[layer], b_expert_router[layer],
                     w_gate[layer], w_up[layer], w_down[layer])
        x = x + y.reshape(B, S, D).astype(x.dtype)
    return rms_norm(x, g_final)
```

```python
import functools

import numpy as np
import jax
import jax.numpy as jnp
from jax import lax
from jax.experimental import pallas as pl
from jax.experimental.pallas import tpu as pltpu

F32 = jnp.float32
BF16 = jnp.bfloat16

D_MODEL = 1024
ATTN_GROUPS = ((128, 1), (512, 4), (2048, 16))
N_GROUPS = len(ATTN_GROUPS)
ATTN_HEADS = 8
HEAD_DIM = 64
GROUP_W = ATTN_HEADS * HEAD_DIM
QKV_W = N_GROUPS * GROUP_W
RET_HEADS = 4
RET_DK = 128
RET_DV = 256
RET_CHUNK = 128
N_EXPERT_GROUPS = 4
EXPERTS_PER_GROUP = 8
N_EXPERTS = N_EXPERT_GROUPS * EXPERTS_PER_GROUP
EXPERT_FF = 512
EPS = 1e-6

LANES = 128
BLK = 128
SPAN = 2048
NEG = -1e30

COL_GATE_A = 0
COL_GATE_R = 1024
COL_QA = 2048
COL_KA = COL_QA + QKV_W
COL_VA = COL_KA + QKV_W
COL_QR = COL_VA + QKV_W
COL_KR = COL_QR + RET_HEADS * RET_DK
COL_VR = COL_KR + RET_HEADS * RET_DK
COL_GR = COL_VR + RET_HEADS * RET_DV
IN_WIDTH = COL_GR + RET_HEADS * RET_DV

PROJ_TM = 512
PROJ_TN = IN_WIDTH // 2
MXU_N = 256
MIX_TM = 512
EXP_TM = 256
FIN_TM = 512
VMEM_LIMIT = 56 * 1024 * 1024


def _cparams(sem):
    return pltpu.CompilerParams(dimension_semantics=sem, vmem_limit_bytes=VMEM_LIMIT)


def _proj_kernel(x_ref, g_ref, w_ref, o_ref):
    x = x_ref[...]
    ms = jnp.mean(x * x, axis=-1, keepdims=True)
    h = (x * lax.rsqrt(ms + EPS) * g_ref[...]).astype(BF16)
    for c in range(PROJ_TN // MXU_N):
        sl = slice(c * MXU_N, (c + 1) * MXU_N)
        o_ref[:, sl] = jnp.dot(h, w_ref[:, sl], preferred_element_type=F32).astype(o_ref.dtype)


def _proj(x2, g_mix, w_in_bf16, interpret):
    n = x2.shape[0]
    return pl.pallas_call(
        _proj_kernel,
        grid=(IN_WIDTH // PROJ_TN, n // PROJ_TM),
        in_specs=[
            pl.BlockSpec((PROJ_TM, D_MODEL), lambda j, i: (i, 0)),
            pl.BlockSpec((1, D_MODEL), lambda j, i: (0, 0)),
            pl.BlockSpec((D_MODEL, PROJ_TN), lambda j, i: (0, j)),
        ],
        out_specs=pl.BlockSpec((PROJ_TM, PROJ_TN), lambda j, i: (i, j)),
        out_shape=jax.ShapeDtypeStruct((n, IN_WIDTH), BF16),
        compiler_params=_cparams(("arbitrary", "arbitrary")),
        interpret=interpret,
        name="proj",
    )(x2, g_mix, w_in_bf16)


def _attn_unit(q2, kk, vv, bias_a, bias_b):
    lane = lax.broadcasted_iota(jnp.int32, (BLK, LANES), 1)
    left = lane < HEAD_DIM
    zero = jnp.zeros_like(q2)
    nt = (((1,), (1,)), ((), ()))
    outs = []
    for qh, bias in ((jnp.where(left, q2, zero), bias_a), (jnp.where(left, zero, q2), bias_b)):
        s = lax.dot_general(qh, kk, nt, preferred_element_type=F32) + bias
        m = jnp.max(s, axis=-1, keepdims=True)
        p = jnp.exp(s - m)
        den = jnp.sum(p, axis=-1, keepdims=True)
        o = jnp.dot(p.astype(BF16), vv, preferred_element_type=F32)
        outs.append((o, m, den))
    (oa, ma, da), (ob, mb, db) = outs
    return jnp.where(left, oa, ob), jnp.where(left, ma, mb), jnp.where(left, da, db)


def _attn_kernel(q1_ref, q2_ref, q3_ref, k1_ref, k2_ref, k3_ref, v1_ref, v2_ref, v3_ref,
                 bias_ref, o_ref, qf_ref, kvf_ref, acc_ref, *, seq):
    s_id = pl.program_id(2)
    conv_rows = 256

    @pl.when(s_id == 0)
    def _():
        for slot, ref in enumerate((k2_ref, v2_ref, k3_ref, v3_ref)):
            def body(i, c, slot=slot, ref=ref):
                r0 = pl.multiple_of(i * conv_rows, conv_rows)
                kvf_ref[slot, pl.ds(r0, conv_rows), :] = ref[pl.ds(r0, conv_rows), :].astype(F32)
                return c
            lax.fori_loop(0, seq // conv_rows, body, 0)

    for slot, ref in enumerate((q2_ref, q3_ref)):
        def body(i, c, slot=slot, ref=ref):
            r0 = pl.multiple_of(i * conv_rows, conv_rows)
            qf_ref[slot, pl.ds(r0, conv_rows), :] = ref[pl.ds(r0, conv_rows), :].astype(F32) * 0.125
            return c
        lax.fori_loop(0, SPAN // conv_rows, body, 0)

    def dilated_group(gi, d):
        slot = gi - 1
        n_blk = SPAN // (BLK * d)
        for m in range(n_blk):
            first = jnp.where(jnp.logical_and(s_id == 0, m == 0), 1, 0) if m == 0 else 0

            def body(r, c, m=m, first=first):
                loc = BLK * m * d + r
                cur = s_id * SPAN + loc
                prev = jnp.maximum(cur - BLK * d, r)
                q2 = qf_ref[slot, pl.ds(loc, BLK, stride=d), :].astype(BF16)
                kk = jnp.concatenate([kvf_ref[2 * slot, pl.ds(prev, BLK, stride=d), :],
                                      kvf_ref[2 * slot, pl.ds(cur, BLK, stride=d), :]], axis=0).astype(BF16)
                vv = jnp.concatenate([kvf_ref[2 * slot + 1, pl.ds(prev, BLK, stride=d), :],
                                      kvf_ref[2 * slot + 1, pl.ds(cur, BLK, stride=d), :]], axis=0).astype(BF16)
                num, mx, den = _attn_unit(q2, kk, vv, bias_ref[gi, first, 0], bias_ref[gi, first, 1])
                acc_ref[3 * slot + 0, pl.ds(loc, BLK, stride=d), :] = num
                acc_ref[3 * slot + 1, pl.ds(loc, BLK, stride=d), :] = mx
                acc_ref[3 * slot + 2, pl.ds(loc, BLK, stride=d), :] = den
                return c
            lax.fori_loop(0, d, body, 0)

    dilated_group(2, ATTN_GROUPS[2][1])
    dilated_group(1, ATTN_GROUPS[1][1])

    def body(m, c):
        loc = pl.multiple_of(m * BLK, BLK)
        cur = pl.multiple_of(s_id * SPAN + loc, BLK)
        prev = pl.multiple_of(jnp.maximum(cur - BLK, 0), BLK)
        first = jnp.where(cur == 0, 1, 0)
        q2 = q1_ref[pl.ds(loc, BLK), :] * 0.125
        kk = jnp.concatenate([k1_ref[pl.ds(prev, BLK), :], k1_ref[pl.ds(cur, BLK), :]], axis=0)
        vv = jnp.concatenate([v1_ref[pl.ds(prev, BLK), :], v1_ref[pl.ds(cur, BLK), :]], axis=0)
        n1, m1, d1 = _attn_unit(q2, kk, vv, bias_ref[0, first, 0], bias_ref[0, first, 1])
        n2, m2, d2 = (acc_ref[j, pl.ds(loc, BLK), :] for j in (0, 1, 2))
        n3, m3, d3 = (acc_ref[j, pl.ds(loc, BLK), :] for j in (3, 4, 5))
        mx = jnp.maximum(jnp.maximum(m1, m2), m3)
        w1, w2, w3 = jnp.exp(m1 - mx), jnp.exp(m2 - mx), jnp.exp(m3 - mx)
        num = w1 * n1 + w2 * n2 + w3 * n3
        den = w1 * d1 + w2 * d2 + w3 * d3
        o_ref[pl.ds(loc, BLK), :] = (num / den).astype(o_ref.dtype)
        return c
    lax.fori_loop(0, SPAN // BLK, body, 0)


def _attn_bias():
    slopes = np.exp2(-8.0 * np.arange(1, ATTN_HEADS + 1, dtype=np.float64) / ATTN_HEADS)
    qi = np.arange(BLK)[:, None]
    kj = np.arange(2 * BLK)[None, :]
    rel = qi + BLK - kj
    out = np.zeros((N_GROUPS, 2, ATTN_HEADS, BLK, 2 * BLK), np.float32)
    for gi, (window, d) in enumerate(ATTN_GROUPS):
        n_back = window // d
        assert n_back == BLK
        valid = (rel >= 0) & (rel <= n_back)
        bias = -slopes[:, None, None] * (rel * d)[None].astype(np.float64)
        out[gi, 0] = np.where(valid[None], bias, NEG)
        out[gi, 1] = np.where((valid & (kj >= BLK))[None], bias, NEG)
    return jnp.asarray(out)


def _attention(proj, batch, seq, interpret):
    n = batch * seq
    spans = seq // SPAN
    n_hp = GROUP_W // LANES
    qcol = lambda g: (COL_QA + g * GROUP_W) // LANES
    kcol = lambda g: (COL_KA + g * GROUP_W) // LANES
    vcol = lambda g: (COL_VA + g * GROUP_W) // LANES
    q_specs = [pl.BlockSpec((SPAN, LANES), functools.partial(lambda b, hp, s, c: (b * spans + s, c + hp), c=qcol(g)))
               for g in range(N_GROUPS)]
    k_specs = [pl.BlockSpec((seq, LANES), functools.partial(lambda b, hp, s, c: (b, c + hp), c=kcol(g)))
               for g in range(N_GROUPS)]
    v_specs = [pl.BlockSpec((seq, LANES), functools.partial(lambda b, hp, s, c: (b, c + hp), c=vcol(g)))
               for g in range(N_GROUPS)]
    bias_spec = pl.BlockSpec((N_GROUPS, 2, 2, BLK, 2 * BLK), lambda b, hp, s: (0, 0, hp, 0, 0))
    return pl.pallas_call(
        functools.partial(_attn_kernel, seq=seq),
        grid=(batch, n_hp, spans),
        in_specs=q_specs + k_specs + v_specs + [bias_spec],
        out_specs=pl.BlockSpec((SPAN, LANES), lambda b, hp, s: (b * spans + s, hp)),
        out_shape=jax.ShapeDtypeStruct((n, GROUP_W), BF16),
        scratch_shapes=[
            pltpu.VMEM((2, SPAN, LANES), F32),
            pltpu.VMEM((4, seq, LANES), F32),
            pltpu.VMEM((6, SPAN, LANES), F32),
        ],
        compiler_params=_cparams(("arbitrary", "arbitrary", "arbitrary")),
        interpret=interpret,
        name="attn",
    )(*([proj] * 9), _attn_bias())


def _ret_kernel(q_ref, k_ref, v_ref, gr_ref, dec_ref, xi_ref, zeta_ref, gch_ref, o_ref, st_ref, *, seq):
    st_ref[...] = jnp.zeros_like(st_ref)
    nt = (((1,), (1,)), ((), ()))
    scale = RET_DK ** -0.5

    def body(c, carry):
        r0 = pl.multiple_of(c * RET_CHUNK, RET_CHUNK)
        qi = q_ref[pl.ds(r0, RET_CHUNK), :]
        kf = k_ref[pl.ds(r0, RET_CHUNK), :].astype(F32) * scale
        ki = kf.astype(BF16)
        kz_t = jnp.transpose(kf * zeta_ref[0]).astype(BF16)
        vi = v_ref[pl.ds(r0, RET_CHUNK), :]
        att = lax.dot_general(qi, ki, nt, preferred_element_type=F32) * dec_ref[0]
        inner = jnp.dot(att.astype(BF16), vi, preferred_element_type=F32)
        st = st_ref[...]
        cross = jnp.dot(qi, st.astype(BF16), preferred_element_type=F32) * xi_ref[0]
        st_ref[...] = gch_ref[0] * st + jnp.dot(kz_t, vi, preferred_element_type=F32)
        y = inner + cross
        mu = jnp.mean(y, axis=-1, keepdims=True)
        yc = y - mu
        var = jnp.mean(yc * yc, axis=-1, keepdims=True)
        yn = yc * lax.rsqrt(var + EPS)
        g = gr_ref[pl.ds(r0, RET_CHUNK), :].astype(F32)
        o_ref[pl.ds(r0, RET_CHUNK), :] = (g * jax.nn.sigmoid(g) * yn).astype(o_ref.dtype)
        return carry
    lax.fori_loop(0, seq // RET_CHUNK, body, 0)


def _ret_tables():
    c = RET_CHUNK
    log_g = np.log1p(-np.exp2(-5.0 - np.arange(RET_HEADS, dtype=np.float64)))
    pos = np.arange(c, dtype=np.float64)
    diff = pos[:, None] - pos[None, :]
    dec = np.where(diff >= 0, np.exp(log_g[:, None, None] * np.maximum(diff, 0.0)), 0.0)
    xi = np.exp(log_g[:, None] * (pos + 1.0))[..., None] * np.ones((1, 1, RET_DV))
    zeta = np.exp(log_g[:, None] * (c - 1.0 - pos))[..., None] * np.ones((1, 1, RET_DK))
    gch = np.exp(log_g * c)[:, None, None] * np.ones((1, 1, RET_DV))
    return tuple(jnp.asarray(t, F32) for t in (dec, xi, zeta, gch))


def _retention(proj, batch, seq, interpret):
    n = batch * seq
    dec, xi, zeta, gch = _ret_tables()
    return pl.pallas_call(
        functools.partial(_ret_kernel, seq=seq),
        grid=(batch, RET_HEADS),
        in_specs=[
            pl.BlockSpec((seq, RET_DK), lambda b, h: (b, COL_QR // RET_DK + h)),
            pl.BlockSpec((seq, RET_DK), lambda b, h: (b, COL_KR // RET_DK + h)),
            pl.BlockSpec((seq, RET_DV), lambda b, h: (b, COL_VR // RET_DV + h)),
            pl.BlockSpec((seq, RET_DV), lambda b, h: (b, COL_GR // RET_DV + h)),
            pl.BlockSpec((1, RET_CHUNK, RET_CHUNK), lambda b, h: (h, 0, 0)),
            pl.BlockSpec((1, RET_CHUNK, RET_DV), lambda b, h: (h, 0, 0)),
            pl.BlockSpec((1, RET_CHUNK, RET_DK), lambda b, h: (h, 0, 0)),
            pl.BlockSpec((1, 1, RET_DV), lambda b, h: (h, 0, 0)),
        ],
        out_specs=pl.BlockSpec((seq, RET_DV), lambda b, h: (b, h)),
        out_shape=jax.ShapeDtypeStruct((n, RET_HEADS * RET_DV), BF16),
        scratch_shapes=[pltpu.VMEM((RET_DK, RET_DV), F32)],
        compiler_params=_cparams(("arbitrary", "arbitrary")),
        interpret=interpret,
        name="retention",
    )(proj, proj, proj, proj, dec, xi, zeta, gch)


ROUTER_OFF = N_EXPERT_GROUPS


def _mix_kernel(oa_ref, or_ref, ga_ref, gr_ref, x_ref, pa_ref, pr_ref, wo_ref, gf_ref, wr_ref, br_ref,
                x1_ref, h2_ref, route_ref, cnt_ref, carry_ref):
    i = pl.program_id(0)

    @pl.when(i == 0)
    def _():
        carry_ref[...] = jnp.zeros_like(carry_ref)

    a = jnp.dot(oa_ref[...], pa_ref[...], preferred_element_type=F32)
    r = jnp.dot(or_ref[...], pr_ref[...], preferred_element_type=F32)
    merged = (jax.nn.sigmoid(ga_ref[...].astype(F32)) * a + jax.nn.sigmoid(gr_ref[...].astype(F32)) * r)
    x1 = x_ref[...] + jnp.dot(merged.astype(BF16), wo_ref[...], preferred_element_type=F32)
    x1_ref[...] = x1
    ms = jnp.mean(x1 * x1, axis=-1, keepdims=True)
    h2 = x1 * lax.rsqrt(ms + EPS) * gf_ref[...]
    h2_ref[...] = h2.astype(h2_ref.dtype)

    logits = jnp.dot(h2, wr_ref[...], preferred_element_type=F32, precision=lax.Precision.HIGHEST) + br_ref[...]
    tm = logits.shape[0]
    lane = lax.broadcasted_iota(jnp.int32, (tm, LANES), 1)
    big = jnp.int32(4 * LANES)
    ninf = -jnp.inf
    is_g = lane < N_EXPERT_GROUPS
    gl = jnp.where(is_g, logits, ninf)
    gmax = jnp.max(gl, axis=-1, keepdims=True)
    gsum = jnp.sum(jnp.where(is_g, jnp.exp(gl - gmax), 0.0), axis=-1, keepdims=True)
    g_val = 1.0 / gsum
    g_idx = jnp.min(jnp.where(jnp.logical_and(is_g, gl == gmax), lane, big), axis=-1, keepdims=True)
    lo = ROUTER_OFF + EXPERTS_PER_GROUP * g_idx
    in_grp = jnp.logical_and(lane >= lo, lane < lo + EXPERTS_PER_GROUP)
    el = jnp.where(in_grp, logits, ninf)
    v1 = jnp.max(el, axis=-1, keepdims=True)
    i1 = jnp.min(jnp.where(jnp.logical_and(in_grp, el == v1), lane, big), axis=-1, keepdims=True)
    rest = jnp.logical_and(in_grp, lane != i1)
    el2 = jnp.where(rest, logits, ninf)
    v2 = jnp.max(el2, axis=-1, keepdims=True)
    i2 = jnp.min(jnp.where(jnp.logical_and(rest, el2 == v2), lane, big), axis=-1, keepdims=True)
    t = jnp.exp(v2 - v1)
    w1 = g_val / (1.0 + t)
    w2 = g_val * t / (1.0 + t)

    sel = jnp.logical_or(lane == i1, lane == i2)
    sel_bf = jnp.where(sel, 1.0, 0.0).astype(BF16)
    row = lax.broadcasted_iota(jnp.int32, (tm, tm), 0)
    col = lax.broadcasted_iota(jnp.int32, (tm, tm), 1)
    tri = jnp.where(col < row, 1.0, 0.0).astype(BF16)
    before = jnp.dot(tri, sel_bf, preferred_element_type=F32) + carry_ref[...]
    r1 = jnp.sum(jnp.where(lane == i1, before, 0.0), axis=-1, keepdims=True)
    r2 = jnp.sum(jnp.where(lane == i2, before, 0.0), axis=-1, keepdims=True)
    carry = carry_ref[...] + jnp.sum(jnp.where(sel, 1.0, 0.0), axis=0, keepdims=True)
    carry_ref[...] = carry
    cnt_ref[...] = carry

    vals = ((i1 - ROUTER_OFF).astype(F32), (i2 - ROUTER_OFF).astype(F32), w1, w2, r1, r2)
    route = jnp.zeros((tm, LANES), F32)
    for j, v in enumerate(vals):
        route = jnp.where(lane == j, v, route)
    route_ref[...] = route


def _mix(o_attn, o_ret, proj, x2, pa, pr, wo, g_ffn, w_router, b_router, interpret):
    n = x2.shape[0]
    tm = MIX_TM
    const = lambda i: (0, 0)
    return pl.pallas_call(
        _mix_kernel,
        grid=(n // tm,),
        in_specs=[
            pl.BlockSpec((tm, GROUP_W), lambda i: (i, 0)),
            pl.BlockSpec((tm, D_MODEL), lambda i: (i, 0)),
            pl.BlockSpec((tm, D_MODEL), lambda i: (i, COL_GATE_A // D_MODEL)),
            pl.BlockSpec((tm, D_MODEL), lambda i: (i, COL_GATE_R // D_MODEL)),
            pl.BlockSpec((tm, D_MODEL), lambda i: (i, 0)),
            pl.BlockSpec((GROUP_W, D_MODEL), const),
            pl.BlockSpec((D_MODEL, D_MODEL), const),
            pl.BlockSpec((D_MODEL, D_MODEL), const),
            pl.BlockSpec((1, D_MODEL), const),
            pl.BlockSpec((D_MODEL, LANES), const),
            pl.BlockSpec((1, LANES), const),
        ],
        out_specs=[
            pl.BlockSpec((tm, D_MODEL), lambda i: (i, 0)),
            pl.BlockSpec((tm, D_MODEL), lambda i: (i, 0)),
            pl.BlockSpec((tm, LANES), lambda i: (i, 0)),
            pl.BlockSpec((1, LANES), const),
        ],
        out_shape=[
            jax.ShapeDtypeStruct((n, D_MODEL), F32),
            jax.ShapeDtypeStruct((n, D_MODEL), BF16),
            jax.ShapeDtypeStruct((n, LANES), F32),
            jax.ShapeDtypeStruct((1, LANES), F32),
        ],
        scratch_shapes=[pltpu.VMEM((1, LANES), F32)],
        compiler_params=_cparams(("arbitrary",)),
        interpret=interpret,
        name="mix_router",
    )(o_attn, o_ret, proj, proj, x2, pa, pr, wo, g_ffn, w_router, b_router)


def _expert_kernel(te_ref, tv_ref, xs_ref, wg_ref, wu_ref, wd_ref, o_ref, wg_s, wu_s, wd_s):
    i = pl.program_id(0)
    changed = jnp.logical_or(i == 0, te_ref[i] != te_ref[jnp.maximum(i - 1, 0)])

    @pl.when(changed)
    def _():
        wg_s[...] = wg_ref[0].astype(BF16)
        wu_s[...] = wu_ref[0].astype(BF16)
        wd_s[...] = wd_ref[0].astype(BF16)

    @pl.when(tv_ref[i] != 0)
    def _():
        xs = xs_ref[...]
        a = jnp.dot(xs, wg_s[...], preferred_element_type=F32)
        u = jnp.dot(xs, wu_s[...], preferred_element_type=F32)
        hid = (a * jax.nn.sigmoid(a) * u).astype(BF16)
        o_ref[...] = jnp.dot(hid, wd_s[...], preferred_element_type=F32).astype(o_ref.dtype)

    @pl.when(tv_ref[i] == 0)
    def _():
        o_ref[...] = jnp.zeros_like(o_ref)


def _experts(xs, tile_expert, tile_valid, w_gate, w_up, w_down, interpret):
    p = xs.shape[0]
    n_tiles = p // EXP_TM
    grid_spec = pltpu.PrefetchScalarGridSpec(
        num_scalar_prefetch=2,
        grid=(n_tiles,),
        in_specs=[
            pl.BlockSpec((EXP_TM, D_MODEL), lambda i, te, tv: (i, 0)),
            pl.BlockSpec((1, D_MODEL, EXPERT_FF), lambda i, te, tv: (te[i], 0, 0)),
            pl.BlockSpec((1, D_MODEL, EXPERT_FF), lambda i, te, tv: (te[i], 0, 0)),
            pl.BlockSpec((1, EXPERT_FF, D_MODEL), lambda i, te, tv: (te[i], 0, 0)),
        ],
        out_specs=pl.BlockSpec((EXP_TM, D_MODEL), lambda i, te, tv: (i, 0)),
        scratch_shapes=[
            pltpu.VMEM((D_MODEL, EXPERT_FF), BF16),
            pltpu.VMEM((D_MODEL, EXPERT_FF), BF16),
            pltpu.VMEM((EXPERT_FF, D_MODEL), BF16),
        ],
    )
    return pl.pallas_call(
        _expert_kernel,
        grid_spec=grid_spec,
        out_shape=jax.ShapeDtypeStruct((p, D_MODEL), BF16),
        compiler_params=_cparams(("arbitrary",)),
        interpret=interpret,
        name="experts",
    )(tile_expert, tile_valid, xs, w_gate, w_up, w_down)


def _final_kernel(x1_ref, ya_ref, yb_ref, route_ref, g_ref, o_ref):
    route = route_ref[...]
    w1 = route[:, 2:3]
    w2 = route[:, 3:4]
    x2 = x1_ref[...] + w1 * ya_ref[...].astype(F32) + w2 * yb_ref[...].astype(F32)
    ms = jnp.mean(x2 * x2, axis=-1, keepdims=True)
    o_ref[...] = x2 * lax.rsqrt(ms + EPS) * g_ref[...]


def _final(x1, ya, yb, route, g_final, interpret):
    n = x1.shape[0]
    tm = FIN_TM
    row = lambda i: (i, 0)
    return pl.pallas_call(
        _final_kernel,
        grid=(n // tm,),
        in_specs=[
            pl.BlockSpec((tm, D_MODEL), row),
            pl.BlockSpec((tm, D_MODEL), row),
            pl.BlockSpec((tm, D_MODEL), row),
            pl.BlockSpec((tm, LANES), row),
            pl.BlockSpec((1, D_MODEL), lambda i: (0, 0)),
        ],
        out_specs=pl.BlockSpec((tm, D_MODEL), row),
        out_shape=jax.ShapeDtypeStruct((n, D_MODEL), F32),
        compiler_params=_cparams(("arbitrary",)),
        interpret=interpret,
        name="combine_final",
    )(x1, ya, yb, route, g_final)


def _permute_w_in(w_in):
    splits = np.cumsum([QKV_W, QKV_W, QKV_W, 512, 512, 1024, 1024, D_MODEL, D_MODEL])[:-1].tolist()
    qa, ka, va, qr, kr, vr, gr, gate_a, gate_r = jnp.split(w_in, splits, axis=-1)
    return jnp.concatenate([gate_a, gate_r, qa, ka, va, qr, kr, vr, gr], axis=-1).astype(BF16)


def _route_plan(route, counts):
    n = route.shape[0]
    e1 = route[:, 0].astype(jnp.int32)
    e2 = route[:, 1].astype(jnp.int32)
    r1 = route[:, 4].astype(jnp.int32)
    r2 = route[:, 5].astype(jnp.int32)
    cnt = counts[0, ROUTER_OFF:ROUTER_OFF + N_EXPERTS].astype(jnp.int32)
    padded = ((cnt + EXP_TM - 1) // EXP_TM) * EXP_TM
    ends = jnp.cumsum(padded)
    offs = ends - padded
    dest1 = offs[e1] + r1
    dest2 = offs[e2] + r2
    n_rows = 2 * n + N_EXPERTS * EXP_TM
    n_tiles = n_rows // EXP_TM
    tile_start = jnp.arange(n_tiles, dtype=jnp.int32) * EXP_TM
    tile_valid = (tile_start < ends[-1]).astype(jnp.int32)
    te = jnp.minimum(jnp.searchsorted(ends, tile_start, side="right"), N_EXPERTS - 1).astype(jnp.int32)
    last_valid = jnp.maximum(ends[-1] // EXP_TM - 1, 0)
    tile_expert = jnp.where(tile_valid == 1, te, te[last_valid])
    tok = jnp.arange(n, dtype=jnp.int32)
    src = jnp.zeros((n_rows,), jnp.int32).at[dest1].set(tok).at[dest2].set(tok)
    return dest1, dest2, src, tile_expert, tile_valid


def _forward(x, g_mix, w_in, w_attn_branch, w_ret_branch, w_out, g_ffn, w_group_router, b_group_router,
             w_expert_router, b_expert_router, w_gate, w_up, w_down, g_final, interpret=False):
    batch, seq, d = x.shape
    n = batch * seq
    x2 = x.reshape(n, d)
    proj = _proj(x2, g_mix[0][None, :], _permute_w_in(w_in[0]), interpret)
    o_attn = _attention(proj, batch, seq, interpret)
    o_ret = _retention(proj, batch, seq, interpret)
    pad = LANES - N_EXPERT_GROUPS - N_EXPERTS
    w_router = jnp.concatenate([w_group_router[0], w_expert_router[0], jnp.zeros((d, pad), F32)], axis=-1)
    b_router = jnp.concatenate([b_group_router[0], b_expert_router[0], jnp.zeros((pad,), F32)])[None, :]
    x1, h2, route, counts = _mix(o_attn, o_ret, proj, x2, w_attn_branch[0].astype(BF16),
                                 w_ret_branch[0].astype(BF16), w_out[0].astype(BF16), g_ffn[0][None, :],
                                 w_router, b_router, interpret)
    dest1, dest2, src, tile_expert, tile_valid = _route_plan(route, counts)
    xs = jnp.take(h2, src, axis=0)
    ys = _experts(xs, tile_expert, tile_valid, w_gate[0], w_up[0], w_down[0], interpret)
    ya = jnp.take(ys, dest1, axis=0)
    yb = jnp.take(ys, dest2, axis=0)
    out = _final(x1, ya, yb, route, g_final[None, :], interpret)
    return out.reshape(batch, seq, d)


def kernel(x, g_mix, w_in, w_attn_branch, w_ret_branch, w_out, g_ffn, w_group_router, b_group_router,
           w_expert_router, b_expert_router, w_gate, w_up, w_down, g_final):
    return _forward(x, g_mix, w_in, w_attn_branch, w_ret_branch, w_out, g_ffn, w_group_router,
                    b_group_router, w_expert_router, b_expert_router, w_gate, w_up, w_down, g_final)
```

```python
import functools

import numpy as np
import jax
import jax.numpy as jnp
from jax import lax
from jax.experimental import pallas as pl
from jax.experimental.pallas import tpu as pltpu
from jax.experimental.pallas import tpu_sc as plsc

F32 = jnp.float32
BF16 = jnp.bfloat16

D_MODEL = 1024
ATTN_GROUPS = ((128, 1), (512, 4), (2048, 16))
N_GROUPS = len(ATTN_GROUPS)
ATTN_HEADS = 8
HEAD_DIM = 64
GROUP_W = ATTN_HEADS * HEAD_DIM
QKV_W = N_GROUPS * GROUP_W
RET_HEADS = 4
RET_DK = 128
RET_DV = 256
RET_CHUNK = 128
RET_TS = 1024
N_EXPERT_GROUPS = 4
EXPERTS_PER_GROUP = 8
N_EXPERTS = N_EXPERT_GROUPS * EXPERTS_PER_GROUP
EXPERT_FF = 512
EPS = 1e-6

LANES = 128
BLK = 128
SPAN = 2048
NEG = -1e30
ATTN_UNROLL = 4

COL_GATE_A = 0
COL_GATE_R = 1024
COL_VR = 2048
COL_GR = COL_VR + RET_HEADS * RET_DV
COL_QA = COL_GR + RET_HEADS * RET_DV
COL_KA = COL_QA + QKV_W
COL_VA = COL_KA + QKV_W
COL_QR = COL_VA + QKV_W
COL_KR = COL_QR + RET_HEADS * RET_DK
IN_WIDTH = COL_KR + RET_HEADS * RET_DK

PROJ_TM = 512
PROJ_TN = IN_WIDTH // 2
MXU_N = 256
MIX_TM = 512
MIX_CHUNK = 256
ROUTE_ROWS = 8
EXP_TM = 256
SC_WINDOW = 128
SC_ROW_WORDS = 256
FIN_TM = 512
VMEM_LIMIT = 56 * 1024 * 1024


def _cparams(sem):
    return pltpu.CompilerParams(dimension_semantics=sem, vmem_limit_bytes=VMEM_LIMIT)


def _proj_kernel(x_ref, g_ref, w_ref, o_ref):
    x = x_ref[...]
    ms = jnp.mean(x * x, axis=-1, keepdims=True)
    h = (x * lax.rsqrt(ms + EPS) * g_ref[...]).astype(BF16)
    for c in range(PROJ_TN // MXU_N):
        sl = slice(c * MXU_N, (c + 1) * MXU_N)
        o_ref[:, sl] = jnp.dot(h, w_ref[:, sl], preferred_element_type=F32).astype(o_ref.dtype)


def _proj(x2, g_mix, w_in_bf16, interpret):
    n = x2.shape[0]
    return pl.pallas_call(
        _proj_kernel,
        grid=(IN_WIDTH // PROJ_TN, n // PROJ_TM),
        in_specs=[
            pl.BlockSpec((PROJ_TM, D_MODEL), lambda j, i: (i, 0)),
            pl.BlockSpec((1, D_MODEL), lambda j, i: (0, 0)),
            pl.BlockSpec((D_MODEL, PROJ_TN), lambda j, i: (0, j)),
        ],
        out_specs=pl.BlockSpec((PROJ_TM, PROJ_TN), lambda j, i: (i, j)),
        out_shape=jax.ShapeDtypeStruct((n, IN_WIDTH), BF16),
        compiler_params=_cparams(("arbitrary", "arbitrary")),
        interpret=interpret,
        name="proj",
    )(x2, g_mix, w_in_bf16)


def _attn_unit(q2, kk, vv, bias_a, bias_b):
    lane = lax.broadcasted_iota(jnp.int32, (BLK, LANES), 1)
    left = lane < HEAD_DIM
    zero = jnp.zeros_like(q2)
    nt = (((1,), (1,)), ((), ()))
    outs = []
    for qh, bias in ((jnp.where(left, q2, zero), bias_a), (jnp.where(left, zero, q2), bias_b)):
        s = lax.dot_general(qh, kk, nt, preferred_element_type=F32) + bias
        m = jnp.max(s, axis=-1, keepdims=True)
        p = jnp.exp(s - m)
        den = jnp.sum(p, axis=-1, keepdims=True)
        o = jnp.dot(p.astype(BF16), vv, preferred_element_type=F32)
        outs.append((o, m, den))
    (oa, ma, da), (ob, mb, db) = outs
    return jnp.where(left, oa, ob), jnp.where(left, ma, mb), jnp.where(left, da, db)


def _attn_kernel(q1_ref, q2_ref, q3_ref, k1_ref, k2_ref, k3_ref, v1_ref, v2_ref, v3_ref,
                 bias_ref, o_ref, qf_ref, kvf_ref, acc_ref, *, seq):
    s_id = pl.program_id(2)
    conv_rows = 256

    @pl.when(s_id == 0)
    def _():
        for slot, ref in enumerate((k2_ref, v2_ref, k3_ref, v3_ref)):
            def body(i, c, slot=slot, ref=ref):
                r0 = pl.multiple_of(i * conv_rows, conv_rows)
                kvf_ref[slot, pl.ds(r0, conv_rows), :] = ref[pl.ds(r0, conv_rows), :].astype(F32)
                return c
            lax.fori_loop(0, seq // conv_rows, body, 0)

    for slot, ref in enumerate((q2_ref, q3_ref)):
        def body(i, c, slot=slot, ref=ref):
            r0 = pl.multiple_of(i * conv_rows, conv_rows)
            qf_ref[slot, pl.ds(r0, conv_rows), :] = ref[pl.ds(r0, conv_rows), :].astype(F32) * 0.125
            return c
        lax.fori_loop(0, SPAN // conv_rows, body, 0)

    def dilated_unit(gi, d, m, r, first):
        slot = gi - 1
        loc = BLK * m * d + r
        cur = s_id * SPAN + loc
        prev = jnp.maximum(cur - BLK * d, r)
        q2 = qf_ref[slot, pl.ds(loc, BLK, stride=d), :].astype(BF16)
        kk = jnp.concatenate([kvf_ref[2 * slot, pl.ds(prev, BLK, stride=d), :],
                              kvf_ref[2 * slot, pl.ds(cur, BLK, stride=d), :]], axis=0).astype(BF16)
        vv = jnp.concatenate([kvf_ref[2 * slot + 1, pl.ds(prev, BLK, stride=d), :],
                              kvf_ref[2 * slot + 1, pl.ds(cur, BLK, stride=d), :]], axis=0).astype(BF16)
        num, mx, den = _attn_unit(q2, kk, vv, bias_ref[gi, first, 0], bias_ref[gi, first, 1])
        acc_ref[3 * slot + 0, pl.ds(loc, BLK, stride=d), :] = num
        acc_ref[3 * slot + 1, pl.ds(loc, BLK, stride=d), :] = mx
        acc_ref[3 * slot + 2, pl.ds(loc, BLK, stride=d), :] = den

    d3 = ATTN_GROUPS[2][1]
    first_span = jnp.where(s_id == 0, 1, 0)

    def body3(i, c):
        for u in range(ATTN_UNROLL):
            dilated_unit(2, d3, 0, i * ATTN_UNROLL + u, first_span)
        return c
    lax.fori_loop(0, d3 // ATTN_UNROLL, body3, 0)

    d2 = ATTN_GROUPS[1][1]

    def body2(m, c):
        first = jnp.where(jnp.logical_and(s_id == 0, m == 0), 1, 0)
        for r in range(d2):
            dilated_unit(1, d2, m, r, first)
        return c
    lax.fori_loop(0, SPAN // (BLK * d2), body2, 0)

    def dense_unit(m):
        loc = pl.multiple_of(m * BLK, BLK)
        cur = pl.multiple_of(s_id * SPAN + loc, BLK)
        prev = pl.multiple_of(jnp.maximum(cur - BLK, 0), BLK)
        first = jnp.where(cur == 0, 1, 0)
        q2 = q1_ref[pl.ds(loc, BLK), :] * 0.125
        kk = jnp.concatenate([k1_ref[pl.ds(prev, BLK), :], k1_ref[pl.ds(cur, BLK), :]], axis=0)
        vv = jnp.concatenate([v1_ref[pl.ds(prev, BLK), :], v1_ref[pl.ds(cur, BLK), :]], axis=0)
        n1, m1, d1 = _attn_unit(q2, kk, vv, bias_ref[0, first, 0], bias_ref[0, first, 1])
        n2, m2, dd2 = (acc_ref[j, pl.ds(loc, BLK), :] for j in (0, 1, 2))
        n3, m3, dd3 = (acc_ref[j, pl.ds(loc, BLK), :] for j in (3, 4, 5))
        mx = jnp.maximum(jnp.maximum(m1, m2), m3)
        w1, w2, w3 = jnp.exp(m1 - mx), jnp.exp(m2 - mx), jnp.exp(m3 - mx)
        num = w1 * n1 + w2 * n2 + w3 * n3
        den = w1 * d1 + w2 * dd2 + w3 * dd3
        o_ref[pl.ds(loc, BLK), :] = (num / den).astype(o_ref.dtype)

    def body1(i, c):
        for u in range(ATTN_UNROLL):
            dense_unit(i * ATTN_UNROLL + u)
        return c
    lax.fori_loop(0, SPAN // (BLK * ATTN_UNROLL), body1, 0)


def _attn_bias():
    slopes = np.exp2(-8.0 * np.arange(1, ATTN_HEADS + 1, dtype=np.float64) / ATTN_HEADS)
    qi = np.arange(BLK)[:, None]
    kj = np.arange(2 * BLK)[None, :]
    rel = qi + BLK - kj
    out = np.zeros((N_GROUPS, 2, ATTN_HEADS, BLK, 2 * BLK), np.float32)
    for gi, (window, d) in enumerate(ATTN_GROUPS):
        n_back = window // d
        assert n_back == BLK
        valid = (rel >= 0) & (rel <= n_back)
        bias = -slopes[:, None, None] * (rel * d)[None].astype(np.float64)
        out[gi, 0] = np.where(valid[None], bias, NEG)
        out[gi, 1] = np.where((valid & (kj >= BLK))[None], bias, NEG)
    return jnp.asarray(out)


def _attention(proj, batch, seq, interpret):
    n = batch * seq
    spans = seq // SPAN
    n_hp = GROUP_W // LANES
    qcol = lambda g: (COL_QA + g * GROUP_W) // LANES
    kcol = lambda g: (COL_KA + g * GROUP_W) // LANES
    vcol = lambda g: (COL_VA + g * GROUP_W) // LANES
    q_specs = [pl.BlockSpec((SPAN, LANES), functools.partial(lambda b, hp, s, c: (b * spans + s, c + hp), c=qcol(g)))
               for g in range(N_GROUPS)]
    k_specs = [pl.BlockSpec((seq, LANES), functools.partial(lambda b, hp, s, c: (b, c + hp), c=kcol(g)))
               for g in range(N_GROUPS)]
    v_specs = [pl.BlockSpec((seq, LANES), functools.partial(lambda b, hp, s, c: (b, c + hp), c=vcol(g)))
               for g in range(N_GROUPS)]
    bias_spec = pl.BlockSpec((N_GROUPS, 2, 2, BLK, 2 * BLK), lambda b, hp, s: (0, 0, hp, 0, 0))
    return pl.pallas_call(
        functools.partial(_attn_kernel, seq=seq),
        grid=(batch, n_hp, spans),
        in_specs=q_specs + k_specs + v_specs + [bias_spec],
        out_specs=pl.BlockSpec((SPAN, LANES), lambda b, hp, s: (b * spans + s, hp)),
        out_shape=jax.ShapeDtypeStruct((n, GROUP_W), BF16),
        scratch_shapes=[
            pltpu.VMEM((2, SPAN, LANES), F32),
            pltpu.VMEM((4, seq, LANES), F32),
            pltpu.VMEM((6, SPAN, LANES), F32),
        ],
        compiler_params=_cparams(("arbitrary", "arbitrary", "arbitrary")),
        interpret=interpret,
        name="attn",
    )(*([proj] * 9), _attn_bias())


def _ret_kernel(q_ref, k_ref, v_ref, gr_ref, dec_ref, xi_ref, zeta_ref, gch_ref, o_ref, st_ref):
    @pl.when(pl.program_id(1) == 0)
    def _():
        st_ref[...] = jnp.zeros_like(st_ref)

    nt = (((1,), (1,)), ((), ()))
    scale = RET_DK ** -0.5

    def body(c, carry):
        r0 = pl.multiple_of(c * RET_CHUNK, RET_CHUNK)
        rows = pl.ds(r0, RET_CHUNK)
        for h in range(RET_HEADS):
            kcols = slice(h * RET_DK, (h + 1) * RET_DK)
            vcols = slice(h * RET_DV, (h + 1) * RET_DV)
            qi = q_ref[rows, kcols]
            kf = k_ref[rows, kcols].astype(F32) * scale
            ki = kf.astype(BF16)
            kz_t = jnp.transpose(kf * zeta_ref[h]).astype(BF16)
            vi = v_ref[rows, vcols]
            att = lax.dot_general(qi, ki, nt, preferred_element_type=F32) * dec_ref[h]
            inner = jnp.dot(att.astype(BF16), vi, preferred_element_type=F32)
            st = st_ref[h]
            cross = jnp.dot(qi, st.astype(BF16), preferred_element_type=F32) * xi_ref[h]
            st_ref[h] = gch_ref[h] * st + jnp.dot(kz_t, vi, preferred_element_type=F32)
            y = inner + cross
            mu = jnp.mean(y, axis=-1, keepdims=True)
            yc = y - mu
            var = jnp.mean(yc * yc, axis=-1, keepdims=True)
            yn = yc * lax.rsqrt(var + EPS)
            g = gr_ref[rows, vcols].astype(F32)
            o_ref[rows, vcols] = (g * jax.nn.sigmoid(g) * yn).astype(o_ref.dtype)
        return carry
    lax.fori_loop(0, RET_TS // RET_CHUNK, body, 0)


def _ret_tables():
    c = RET_CHUNK
    log_g = np.log1p(-np.exp2(-5.0 - np.arange(RET_HEADS, dtype=np.float64)))
    pos = np.arange(c, dtype=np.float64)
    diff = pos[:, None] - pos[None, :]
    dec = np.where(diff >= 0, np.exp(log_g[:, None, None] * np.maximum(diff, 0.0)), 0.0)
    xi = np.exp(log_g[:, None] * (pos + 1.0))[..., None] * np.ones((1, 1, RET_DV))
    zeta = np.exp(log_g[:, None] * (c - 1.0 - pos))[..., None] * np.ones((1, 1, RET_DK))
    gch = np.exp(log_g * c)[:, None, None] * np.ones((1, 1, RET_DV))
    return tuple(jnp.asarray(t, F32) for t in (dec, xi, zeta, gch))


def _retention(proj, batch, seq, interpret):
    n = batch * seq
    dec, xi, zeta, gch = _ret_tables()
    qk_w = RET_HEADS * RET_DK
    v_w = RET_HEADS * RET_DV
    nts = seq // RET_TS
    const3 = lambda b, t: (0, 0, 0)
    return pl.pallas_call(
        _ret_kernel,
        grid=(batch, nts),
        in_specs=[
            pl.BlockSpec((RET_TS, qk_w), lambda b, t: (b * nts + t, COL_QR // qk_w)),
            pl.BlockSpec((RET_TS, qk_w), lambda b, t: (b * nts + t, COL_KR // qk_w)),
            pl.BlockSpec((RET_TS, v_w), lambda b, t: (b * nts + t, COL_VR // v_w)),
            pl.BlockSpec((RET_TS, v_w), lambda b, t: (b * nts + t, COL_GR // v_w)),
            pl.BlockSpec((RET_HEADS, RET_CHUNK, RET_CHUNK), const3),
            pl.BlockSpec((RET_HEADS, RET_CHUNK, RET_DV), const3),
            pl.BlockSpec((RET_HEADS, RET_CHUNK, RET_DK), const3),
            pl.BlockSpec((RET_HEADS, 1, RET_DV), const3),
        ],
        out_specs=pl.BlockSpec((RET_TS, v_w), lambda b, t: (b * nts + t, 0)),
        out_shape=jax.ShapeDtypeStruct((n, v_w), BF16),
        scratch_shapes=[pltpu.VMEM((RET_HEADS, RET_DK, RET_DV), F32)],
        compiler_params=_cparams(("arbitrary", "arbitrary")),
        interpret=interpret,
        name="retention",
    )(proj, proj, proj, proj, dec, xi, zeta, gch)


ROUTER_OFF = N_EXPERT_GROUPS


def _pack_bf16_pair(a, b):
    hi = lax.bitcast_convert_type(a.astype(BF16).astype(F32), jnp.uint32)
    lo = lax.bitcast_convert_type(b.astype(BF16).astype(F32), jnp.uint32)
    return lax.bitcast_convert_type(hi | (lo >> 16), jnp.int32)


def _unpack_bf16_pair(w):
    u = lax.bitcast_convert_type(w, jnp.uint32)
    a = lax.bitcast_convert_type(u & jnp.uint32(0xFFFF0000), F32).astype(BF16)
    b = lax.bitcast_convert_type(u << 16, F32).astype(BF16)
    return a, b


def _mix_kernel(oa_ref, or_ref, ga_ref, gr_ref, x_ref, pa_ref, pr_ref, wo_ref, gf_ref, wr_ref, br_ref,
                x1_ref, h2_ref, route_ref, route_t_ref, cnt_ref, carry_ref):
    @pl.when(pl.program_id(0) == 0)
    def _():
        carry_ref[...] = jnp.zeros_like(carry_ref)

    for c in range(MIX_TM // MIX_CHUNK):
        _mix_rows(pl.ds(c * MIX_CHUNK, MIX_CHUNK), oa_ref, or_ref, ga_ref, gr_ref, x_ref, pa_ref, pr_ref, wo_ref,
                  gf_ref, wr_ref, br_ref, x1_ref, h2_ref, route_ref, route_t_ref, cnt_ref, carry_ref)


def _mix_rows(rows, oa_ref, or_ref, ga_ref, gr_ref, x_ref, pa_ref, pr_ref, wo_ref, gf_ref, wr_ref, br_ref,
              x1_ref, h2_ref, route_ref, route_t_ref, cnt_ref, carry_ref):
    a = jnp.dot(oa_ref[rows, :], pa_ref[...], preferred_element_type=F32)
    r = jnp.dot(or_ref[rows, :], pr_ref[...], preferred_element_type=F32)
    merged = (jax.nn.sigmoid(ga_ref[rows, :].astype(F32)) * a + jax.nn.sigmoid(gr_ref[rows, :].astype(F32)) * r)
    x1 = x_ref[rows, :] + jnp.dot(merged.astype(BF16), wo_ref[...], preferred_element_type=F32)
    x1_ref[rows, :] = x1
    ms = jnp.mean(x1 * x1, axis=-1, keepdims=True)
    h2 = x1 * lax.rsqrt(ms + EPS) * gf_ref[...]
    half = D_MODEL // 2
    h2_ref[rows, :] = _pack_bf16_pair(h2[:, :half], h2[:, half:])

    h_hi = h2.astype(BF16)
    h_lo = (h2 - h_hi.astype(F32)).astype(BF16)
    both = jnp.dot(h_hi, wr_ref[...], preferred_element_type=F32)
    logits = (both[:, :LANES] + both[:, LANES:]
              + jnp.dot(h_lo, wr_ref[:, :LANES], preferred_element_type=F32) + br_ref[...])
    tm = logits.shape[0]
    lane = lax.broadcasted_iota(jnp.int32, (tm, LANES), 1)
    big = jnp.int32(4 * LANES)
    ninf = -jnp.inf
    is_g = lane < N_EXPERT_GROUPS
    gl = jnp.where(is_g, logits, ninf)
    gmax = jnp.max(gl, axis=-1, keepdims=True)
    gsum = jnp.sum(jnp.where(is_g, jnp.exp(gl - gmax), 0.0), axis=-1, keepdims=True)
    g_val = 1.0 / gsum
    g_idx = jnp.min(jnp.where(jnp.logical_and(is_g, gl == gmax), lane, big), axis=-1, keepdims=True)
    lo = ROUTER_OFF + EXPERTS_PER_GROUP * g_idx
    in_grp = jnp.logical_and(lane >= lo, lane < lo + EXPERTS_PER_GROUP)
    el = jnp.where(in_grp, logits, ninf)
    v1 = jnp.max(el, axis=-1, keepdims=True)
    i1 = jnp.min(jnp.where(jnp.logical_and(in_grp, el == v1), lane, big), axis=-1, keepdims=True)
    rest = jnp.logical_and(in_grp, lane != i1)
    el2 = jnp.where(rest, logits, ninf)
    v2 = jnp.max(el2, axis=-1, keepdims=True)
    i2 = jnp.min(jnp.where(jnp.logical_and(rest, el2 == v2), lane, big), axis=-1, keepdims=True)
    t = jnp.exp(v2 - v1)
    w1 = g_val / (1.0 + t)
    w2 = g_val * t / (1.0 + t)

    sel = jnp.logical_or(lane == i1, lane == i2)
    sel_bf = jnp.where(sel, 1.0, 0.0).astype(BF16)
    row = lax.broadcasted_iota(jnp.int32, (tm, tm), 0)
    col = lax.broadcasted_iota(jnp.int32, (tm, tm), 1)
    tri = jnp.where(col < row, 1.0, 0.0).astype(BF16)
    before = jnp.dot(tri, sel_bf, preferred_element_type=F32) + carry_ref[...]
    r1 = jnp.sum(jnp.where(lane == i1, before, 0.0), axis=-1, keepdims=True)
    r2 = jnp.sum(jnp.where(lane == i2, before, 0.0), axis=-1, keepdims=True)
    carry = carry_ref[...] + jnp.sum(jnp.where(sel, 1.0, 0.0), axis=0, keepdims=True)
    carry_ref[...] = carry
    cnt_ref[...] = carry

    vals = ((i1 - ROUTER_OFF).astype(F32), (i2 - ROUTER_OFF).astype(F32), w1, w2, r1, r2)
    route = jnp.zeros((tm, LANES), F32)
    for j, v in enumerate(vals):
        route = jnp.where(lane == j, v, route)
    route_ref[rows, :] = route
    route_t_ref[:, rows] = jnp.transpose(route)[:ROUTE_ROWS, :]


def _mix(o_attn, o_ret, proj, x2, pa, pr, wo, g_ffn, w_router, b_router, interpret):
    n = x2.shape[0]
    tm = MIX_TM
    const = lambda i: (0, 0)
    return pl.pallas_call(
        _mix_kernel,
        grid=(n // tm,),
        in_specs=[
            pl.BlockSpec((tm, GROUP_W), lambda i: (i, 0)),
            pl.BlockSpec((tm, D_MODEL), lambda i: (i, 0)),
            pl.BlockSpec((tm, D_MODEL), lambda i: (i, COL_GATE_A // D_MODEL)),
            pl.BlockSpec((tm, D_MODEL), lambda i: (i, COL_GATE_R // D_MODEL)),
            pl.BlockSpec((tm, D_MODEL), lambda i: (i, 0)),
            pl.BlockSpec((GROUP_W, D_MODEL), const),
            pl.BlockSpec((D_MODEL, D_MODEL), const),
            pl.BlockSpec((D_MODEL, D_MODEL), const),
            pl.BlockSpec((1, D_MODEL), const),
            pl.BlockSpec((D_MODEL, 2 * LANES), const),
            pl.BlockSpec((1, LANES), const),
        ],
        out_specs=[
            pl.BlockSpec((tm, D_MODEL), lambda i: (i, 0)),
            pl.BlockSpec((tm, D_MODEL // 2), lambda i: (i, 0)),
            pl.BlockSpec((tm, LANES), lambda i: (i, 0)),
            pl.BlockSpec((ROUTE_ROWS, tm), lambda i: (0, i)),
            pl.BlockSpec((1, LANES), const),
        ],
        out_shape=[
            jax.ShapeDtypeStruct((n, D_MODEL), F32),
            jax.ShapeDtypeStruct((n, D_MODEL // 2), jnp.int32),
            jax.ShapeDtypeStruct((n, LANES), F32),
            jax.ShapeDtypeStruct((ROUTE_ROWS, n), F32),
            jax.ShapeDtypeStruct((1, LANES), F32),
        ],
        scratch_shapes=[pltpu.VMEM((1, LANES), F32)],
        compiler_params=_cparams(("arbitrary",)),
        interpret=interpret,
        name="mix_router",
    )(o_attn, o_ret, proj, proj, x2, pa, pr, wo, g_ffn, w_router, b_router)


def _expert_kernel(te_ref, tv_ref, xs_ref, wg_ref, wu_ref, wd_ref, o_ref, wg_s, wu_s, wd_s):
    i = pl.program_id(0)
    changed = jnp.logical_or(i == 0, te_ref[i] != te_ref[jnp.maximum(i - 1, 0)])

    @pl.when(changed)
    def _():
        wg_s[...] = wg_ref[0].astype(BF16)
        wu_s[...] = wu_ref[0].astype(BF16)
        wd_s[...] = wd_ref[0].astype(BF16)

    @pl.when(tv_ref[i] != 0)
    def _():
        xs = jnp.concatenate(_unpack_bf16_pair(xs_ref[...]), axis=1)
        a = jnp.dot(xs, wg_s[...], preferred_element_type=F32)
        u = jnp.dot(xs, wu_s[...], preferred_element_type=F32)
        hid = (a * jax.nn.sigmoid(a) * u).astype(BF16)
        y = jnp.dot(hid, wd_s[...], preferred_element_type=F32)
        half = D_MODEL // 2
        o_ref[...] = _pack_bf16_pair(y[:, :half], y[:, half:])

    @pl.when(tv_ref[i] == 0)
    def _():
        o_ref[...] = jnp.zeros_like(o_ref)


def _experts(xs, tile_expert, tile_valid, w_gate, w_up, w_down, interpret):
    p = xs.shape[0]
    n_tiles = p // EXP_TM
    grid_spec = pltpu.PrefetchScalarGridSpec(
        num_scalar_prefetch=2,
        grid=(n_tiles,),
        in_specs=[
            pl.BlockSpec((EXP_TM, D_MODEL // 2), lambda i, te, tv: (i, 0)),
            pl.BlockSpec((1, D_MODEL, EXPERT_FF), lambda i, te, tv: (te[i], 0, 0)),
            pl.BlockSpec((1, D_MODEL, EXPERT_FF), lambda i, te, tv: (te[i], 0, 0)),
            pl.BlockSpec((1, EXPERT_FF, D_MODEL), lambda i, te, tv: (te[i], 0, 0)),
        ],
        out_specs=pl.BlockSpec((EXP_TM, D_MODEL // 2), lambda i, te, tv: (i, 0)),
        scratch_shapes=[
            pltpu.VMEM((D_MODEL, EXPERT_FF), BF16),
            pltpu.VMEM((D_MODEL, EXPERT_FF), BF16),
            pltpu.VMEM((EXPERT_FF, D_MODEL), BF16),
        ],
    )
    return pl.pallas_call(
        _expert_kernel,
        grid_spec=grid_spec,
        out_shape=jax.ShapeDtypeStruct((p, D_MODEL // 2), jnp.int32),
        compiler_params=_cparams(("arbitrary",)),
        interpret=interpret,
        name="experts",
    )(tile_expert, tile_valid, xs, w_gate, w_up, w_down)


def _final_kernel(x1_ref, ya_ref, yb_ref, route_ref, g_ref, o_ref):
    route = route_ref[...]
    w1 = route[:, 2:3]
    w2 = route[:, 3:4]
    ya = jnp.concatenate(_unpack_bf16_pair(ya_ref[...]), axis=1).astype(F32)
    yb = jnp.concatenate(_unpack_bf16_pair(yb_ref[...]), axis=1).astype(F32)
    x2 = x1_ref[...] + w1 * ya + w2 * yb
    ms = jnp.mean(x2 * x2, axis=-1, keepdims=True)
    o_ref[...] = x2 * lax.rsqrt(ms + EPS) * g_ref[...]


def _final(x1, yab, route, g_final, interpret):
    n = x1.shape[0]
    tm = FIN_TM
    row = lambda i: (i, 0)
    return pl.pallas_call(
        _final_kernel,
        grid=(n // tm,),
        in_specs=[
            pl.BlockSpec((tm, D_MODEL), row),
            pl.BlockSpec((tm, D_MODEL // 2), row),
            pl.BlockSpec((tm, D_MODEL // 2), lambda i: (i + n // tm, 0)),
            pl.BlockSpec((tm, LANES), row),
            pl.BlockSpec((1, D_MODEL), lambda i: (0, 0)),
        ],
        out_specs=pl.BlockSpec((tm, D_MODEL), row),
        out_shape=jax.ShapeDtypeStruct((n, D_MODEL), F32),
        compiler_params=_cparams(("arbitrary",)),
        interpret=interpret,
        name="combine_final",
    )(x1, yab, yab, route, g_final)


def _permute_w_in(w_in):
    splits = np.cumsum([QKV_W, QKV_W, QKV_W, 512, 512, 1024, 1024, D_MODEL, D_MODEL])[:-1].tolist()
    qa, ka, va, qr, kr, vr, gr, gate_a, gate_r = jnp.split(w_in, splits, axis=-1)
    return jnp.concatenate([gate_a, gate_r, vr, gr, qa, ka, va, qr, kr], axis=-1).astype(BF16)


def _route_plan(route_t, counts, n):
    e1 = route_t[0].astype(jnp.int32)
    e2 = route_t[1].astype(jnp.int32)
    r1 = route_t[4].astype(jnp.int32)
    r2 = route_t[5].astype(jnp.int32)
    cnt = counts[0, ROUTER_OFF:ROUTER_OFF + N_EXPERTS].astype(jnp.int32)
    padded = ((cnt + EXP_TM - 1) // EXP_TM) * EXP_TM
    ends = jnp.cumsum(padded)
    offs = ends - padded
    lanes = jnp.arange(N_EXPERTS, dtype=jnp.int32)[None, :]
    dest1 = jnp.sum(jnp.where(e1[:, None] == lanes, offs[None, :], 0), axis=1) + r1
    dest2 = jnp.sum(jnp.where(e2[:, None] == lanes, offs[None, :], 0), axis=1) + r2
    n_rows = 2 * n + N_EXPERTS * EXP_TM
    n_tiles = n_rows // EXP_TM
    tile_start = jnp.arange(n_tiles, dtype=jnp.int32) * EXP_TM
    tile_valid = (tile_start < ends[-1]).astype(jnp.int32)
    te = jnp.sum((ends[None, :] <= tile_start[:, None]).astype(jnp.int32), axis=1)
    te_last = jnp.sum((ends[None, :] <= ends[-1] - EXP_TM).astype(jnp.int32))
    tile_expert = jnp.minimum(jnp.where(tile_valid == 1, te, te_last), N_EXPERTS - 1)
    return dest1, dest2, tile_expert, tile_valid, n_rows


def _sc_mesh():
    return plsc.VectorSubcoreMesh(core_axis_name="core", subcore_axis_name="subcore")


def _split_rows(idx):
    return (2 * idx[:, None] + jnp.arange(2, dtype=jnp.int32)[None, :]).reshape(1, -1)


def _sc_scatter_rows(rows, idx_a, idx_b, n_out):
    n_in, w = rows.shape

    @functools.partial(pl.kernel, out_type=jax.ShapeDtypeStruct((n_out, w), rows.dtype), mesh=_sc_mesh(),
                       scratch_types=[], name="sc_scatter_rows")
    def scatter(x_hbm, ia_hbm, ib_hbm, o_hbm):
        def body(x_vmem, ia_vmem, ib_vmem):
            pltpu.sync_copy(x_vmem, o_hbm.at[ia_vmem.at[0]])
            pltpu.sync_copy(x_vmem, o_hbm.at[ib_vmem.at[0]])

        pltpu.emit_pipeline(
            body,
            grid=(n_in // SC_WINDOW,),
            in_specs=[pl.BlockSpec((SC_WINDOW, w), lambda i: (i, 0)),
                      pl.BlockSpec((1, SC_WINDOW), lambda i: (0, i)),
                      pl.BlockSpec((1, SC_WINDOW), lambda i: (0, i))],
            out_specs=[],
            core_axis_name=("core", "subcore"),
            dimension_semantics=(pltpu.PARALLEL,),
        )(x_hbm, ia_hbm, ib_hbm)

    return scatter(rows, idx_a, idx_b)


def _sc_gather_rows(table, idx):
    n_idx = idx.shape[1]
    w = table.shape[1]

    @functools.partial(pl.kernel, out_type=jax.ShapeDtypeStruct((n_idx, w), table.dtype), mesh=_sc_mesh(),
                       scratch_types=[], name="sc_gather_rows")
    def gather(t_hbm, i_hbm, o_hbm):
        def body(i_vmem, o_vmem):
            pltpu.sync_copy(t_hbm.at[i_vmem.at[0]], o_vmem)

        pltpu.emit_pipeline(
            body,
            grid=(n_idx // SC_WINDOW,),
            in_specs=[pl.BlockSpec((1, SC_WINDOW), lambda i: (0, i))],
            out_specs=[pl.BlockSpec((SC_WINDOW, w), lambda i: (i, 0))],
            core_axis_name=("core", "subcore"),
            dimension_semantics=(pltpu.PARALLEL,),
        )(i_hbm, o_hbm)

    return gather(table, idx)


def _forward(x, g_mix, w_in, w_attn_branch, w_ret_branch, w_out, g_ffn, w_group_router, b_group_router,
             w_expert_router, b_expert_router, w_gate, w_up, w_down, g_final, interpret=False):
    batch, seq, d = x.shape
    n = batch * seq
    x2 = x.reshape(n, d)
    proj = _proj(x2, g_mix[0][None, :], _permute_w_in(w_in[0]), interpret)
    o_attn = _attention(proj, batch, seq, interpret)
    o_ret = _retention(proj, batch, seq, interpret)
    pad = LANES - N_EXPERT_GROUPS - N_EXPERTS
    w_router = jnp.concatenate([w_group_router[0], w_expert_router[0], jnp.zeros((d, pad), F32)], axis=-1)
    w_router_hi = w_router.astype(BF16)
    w_router_lo = (w_router - w_router_hi.astype(F32)).astype(BF16)
    w_router2 = jnp.concatenate([w_router_hi, w_router_lo], axis=-1)
    b_router = jnp.concatenate([b_group_router[0], b_expert_router[0], jnp.zeros((pad,), F32)])[None, :]
    x1, h2p, route, route_t, counts = _mix(o_attn, o_ret, proj, x2, w_attn_branch[0].astype(BF16),
                                           w_ret_branch[0].astype(BF16), w_out[0].astype(BF16),
                                           g_ffn[0][None, :], w_router2, b_router, interpret)
    dest1, dest2, tile_expert, tile_valid, n_rows = _route_plan(route_t, counts, n)
    xs = _sc_scatter_rows(h2p.reshape(2 * n, SC_ROW_WORDS), _split_rows(dest1), _split_rows(dest2), 2 * n_rows)
    ys = _experts(xs.reshape(n_rows, d // 2), tile_expert, tile_valid, w_gate[0], w_up[0], w_down[0], interpret)
    yab = _sc_gather_rows(ys.reshape(2 * n_rows, SC_ROW_WORDS), _split_rows(jnp.concatenate([dest1, dest2])))
    out = _final(x1, yab.reshape(2 * n, d // 2), route, g_final[None, :], interpret)
    return out.reshape(batch, seq, d)


def kernel(x, g_mix, w_in, w_attn_branch, w_ret_branch, w_out, g_ffn, w_group_router, b_group_router,
           w_expert_router, b_expert_router, w_gate, w_up, w_down, g_final):
    return _forward(x, g_mix, w_in, w_attn_branch, w_ret_branch, w_out, g_ffn, w_group_router,
                    b_group_router, w_expert_router, b_expert_router, w_gate, w_up, w_down, g_final)
```

```python
import functools

import numpy as np
import jax
import jax.numpy as jnp
from jax import lax
from jax.experimental import pallas as pl
from jax.experimental.pallas import tpu as pltpu
from jax.experimental.pallas import tpu_sc as plsc

F32 = jnp.float32
BF16 = jnp.bfloat16

D_MODEL = 1024
ATTN_GROUPS = ((128, 1), (512, 4), (2048, 16))
N_GROUPS = len(ATTN_GROUPS)
ATTN_HEADS = 8
HEAD_DIM = 64
GROUP_W = ATTN_HEADS * HEAD_DIM
QKV_W = N_GROUPS * GROUP_W
RET_HEADS = 4
RET_DK = 128
RET_DV = 256
RET_CHUNK = 128
RET_TS = 1024
N_EXPERT_GROUPS = 4
EXPERTS_PER_GROUP = 8
N_EXPERTS = N_EXPERT_GROUPS * EXPERTS_PER_GROUP
EXPERT_FF = 512
EPS = 1e-6

LANES = 128
BLK = 128
SPAN = 2048
NEG = -1e30
ATTN_UNROLL = 4

COL_GATE_A = 0
COL_GATE_R = 1024
COL_VR = 2048
COL_GR = COL_VR + RET_HEADS * RET_DV
COL_QA = COL_GR + RET_HEADS * RET_DV
COL_KA = COL_QA + QKV_W
COL_VA = COL_KA + QKV_W
COL_QR = COL_VA + QKV_W
COL_KR = COL_QR + RET_HEADS * RET_DK
IN_WIDTH = COL_KR + RET_HEADS * RET_DK

PROJ_TM = 512
PROJ_TN = IN_WIDTH // 2
MXU_N = 256
MIX_TM = 512
MIX_CHUNK = 256
ROUTE_ROWS = 8
EXP_TM = 256
SC_WINDOW = 128
SC_ROW_WORDS = 256
FIN_TM = 512
VMEM_LIMIT = 56 * 1024 * 1024


def _cparams(sem):
    return pltpu.CompilerParams(dimension_semantics=sem, vmem_limit_bytes=VMEM_LIMIT)


def _proj_kernel(x_ref, g_ref, w_ref, o_ref):
    x = x_ref[...]
    ms = jnp.mean(x * x, axis=-1, keepdims=True)
    h = (x * lax.rsqrt(ms + EPS) * g_ref[...]).astype(BF16)
    for c in range(PROJ_TN // MXU_N):
        sl = slice(c * MXU_N, (c + 1) * MXU_N)
        o_ref[:, sl] = jnp.dot(h, w_ref[:, sl], preferred_element_type=F32).astype(o_ref.dtype)


def _proj(x2, g_mix, w_in_bf16, interpret):
    n = x2.shape[0]
    return pl.pallas_call(
        _proj_kernel,
        grid=(IN_WIDTH // PROJ_TN, n // PROJ_TM),
        in_specs=[
            pl.BlockSpec((PROJ_TM, D_MODEL), lambda j, i: (i, 0)),
            pl.BlockSpec((1, D_MODEL), lambda j, i: (0, 0)),
            pl.BlockSpec((D_MODEL, PROJ_TN), lambda j, i: (0, j)),
        ],
        out_specs=pl.BlockSpec((PROJ_TM, PROJ_TN), lambda j, i: (i, j)),
        out_shape=jax.ShapeDtypeStruct((n, IN_WIDTH), BF16),
        compiler_params=_cparams(("arbitrary", "arbitrary")),
        interpret=interpret,
        name="proj",
    )(x2, g_mix, w_in_bf16)


def _attn_unit(q2, kk, vv, bias_a, bias_b):
    lane = lax.broadcasted_iota(jnp.int32, (BLK, LANES), 1)
    left = lane < HEAD_DIM
    zero = jnp.zeros_like(q2)
    nt = (((1,), (1,)), ((), ()))
    outs = []
    for qh, bias in ((jnp.where(left, q2, zero), bias_a), (jnp.where(left, zero, q2), bias_b)):
        s = lax.dot_general(qh, kk, nt, preferred_element_type=F32) + bias
        m = jnp.max(s, axis=-1, keepdims=True)
        p = jnp.exp(s - m)
        den = jnp.sum(p, axis=-1, keepdims=True)
        o = jnp.dot(p.astype(BF16), vv, preferred_element_type=F32)
        outs.append((o, m, den))
    (oa, ma, da), (ob, mb, db) = outs
    return jnp.where(left, oa, ob), jnp.where(left, ma, mb), jnp.where(left, da, db)


def _attn_kernel(q1_ref, q2_ref, q3_ref, k1_ref, k2_ref, k3_ref, v1_ref, v2_ref, v3_ref,
                 bias_ref, o_ref, qf_ref, kvf_ref, acc_ref, *, seq):
    s_id = pl.program_id(2)
    conv_rows = 256

    @pl.when(s_id == 0)
    def _():
        for slot, ref in enumerate((k2_ref, v2_ref, k3_ref, v3_ref)):
            def body(i, c, slot=slot, ref=ref):
                r0 = pl.multiple_of(i * conv_rows, conv_rows)
                kvf_ref[slot, pl.ds(r0, conv_rows), :] = ref[pl.ds(r0, conv_rows), :].astype(F32)
                return c
            lax.fori_loop(0, seq // conv_rows, body, 0)

    for slot, ref in enumerate((q2_ref, q3_ref)):
        def body(i, c, slot=slot, ref=ref):
            r0 = pl.multiple_of(i * conv_rows, conv_rows)
            qf_ref[slot, pl.ds(r0, conv_rows), :] = ref[pl.ds(r0, conv_rows), :].astype(F32) * 0.125
            return c
        lax.fori_loop(0, SPAN // conv_rows, body, 0)

    def dilated_unit(gi, d, m, r, first):
        slot = gi - 1
        loc = BLK * m * d + r
        cur = s_id * SPAN + loc
        prev = jnp.maximum(cur - BLK * d, r)
        q2 = qf_ref[slot, pl.ds(loc, BLK, stride=d), :].astype(BF16)
        kk = jnp.concatenate([kvf_ref[2 * slot, pl.ds(prev, BLK, stride=d), :],
                              kvf_ref[2 * slot, pl.ds(cur, BLK, stride=d), :]], axis=0).astype(BF16)
        vv = jnp.concatenate([kvf_ref[2 * slot + 1, pl.ds(prev, BLK, stride=d), :],
                              kvf_ref[2 * slot + 1, pl.ds(cur, BLK, stride=d), :]], axis=0).astype(BF16)
        num, mx, den = _attn_unit(q2, kk, vv, bias_ref[gi, first, 0], bias_ref[gi, first, 1])
        acc_ref[3 * slot + 0, pl.ds(loc, BLK, stride=d), :] = num
        acc_ref[3 * slot + 1, pl.ds(loc, BLK, stride=d), :] = mx
        acc_ref[3 * slot + 2, pl.ds(loc, BLK, stride=d), :] = den

    d3 = ATTN_GROUPS[2][1]
    first_span = jnp.where(s_id == 0, 1, 0)

    def body3(i, c):
        for u in range(ATTN_UNROLL):
            dilated_unit(2, d3, 0, i * ATTN_UNROLL + u, first_span)
        return c
    lax.fori_loop(0, d3 // ATTN_UNROLL, body3, 0)

    d2 = ATTN_GROUPS[1][1]

    def body2(m, c):
        first = jnp.where(jnp.logical_and(s_id == 0, m == 0), 1, 0)
        for r in range(d2):
            dilated_unit(1, d2, m, r, first)
        return c
    lax.fori_loop(0, SPAN // (BLK * d2), body2, 0)

    def dense_unit(m):
        loc = pl.multiple_of(m * BLK, BLK)
        cur = pl.multiple_of(s_id * SPAN + loc, BLK)
        prev = pl.multiple_of(jnp.maximum(cur - BLK, 0), BLK)
        first = jnp.where(cur == 0, 1, 0)
        q2 = q1_ref[pl.ds(loc, BLK), :] * 0.125
        kk = jnp.concatenate([k1_ref[pl.ds(prev, BLK), :], k1_ref[pl.ds(cur, BLK), :]], axis=0)
        vv = jnp.concatenate([v1_ref[pl.ds(prev, BLK), :], v1_ref[pl.ds(cur, BLK), :]], axis=0)
        n1, m1, d1 = _attn_unit(q2, kk, vv, bias_ref[0, first, 0], bias_ref[0, first, 1])
        n2, m2, dd2 = (acc_ref[j, pl.ds(loc, BLK), :] for j in (0, 1, 2))
        n3, m3, dd3 = (acc_ref[j, pl.ds(loc, BLK), :] for j in (3, 4, 5))
        mx = jnp.maximum(jnp.maximum(m1, m2), m3)
        w1, w2, w3 = jnp.exp(m1 - mx), jnp.exp(m2 - mx), jnp.exp(m3 - mx)
        num = w1 * n1 + w2 * n2 + w3 * n3
        den = w1 * d1 + w2 * dd2 + w3 * dd3
        o_ref[pl.ds(loc, BLK), :] = (num / den).astype(o_ref.dtype)

    def body1(i, c):
        for u in range(ATTN_UNROLL):
            dense_unit(i * ATTN_UNROLL + u)
        return c
    lax.fori_loop(0, SPAN // (BLK * ATTN_UNROLL), body1, 0)


def _attn_bias():
    slopes = np.exp2(-8.0 * np.arange(1, ATTN_HEADS + 1, dtype=np.float64) / ATTN_HEADS)
    qi = np.arange(BLK)[:, None]
    kj = np.arange(2 * BLK)[None, :]
    rel = qi + BLK - kj
    out = np.zeros((N_GROUPS, 2, ATTN_HEADS, BLK, 2 * BLK), np.float32)
    for gi, (window, d) in enumerate(ATTN_GROUPS):
        n_back = window // d
        assert n_back == BLK
        valid = (rel >= 0) & (rel <= n_back)
        bias = -slopes[:, None, None] * (rel * d)[None].astype(np.float64)
        out[gi, 0] = np.where(valid[None], bias, NEG)
        out[gi, 1] = np.where((valid & (kj >= BLK))[None], bias, NEG)
    return jnp.asarray(out)


def _attention(proj, batch, seq, interpret):
    n = batch * seq
    spans = seq // SPAN
    n_hp = GROUP_W // LANES
    qcol = lambda g: (COL_QA + g * GROUP_W) // LANES
    kcol = lambda g: (COL_KA + g * GROUP_W) // LANES
    vcol = lambda g: (COL_VA + g * GROUP_W) // LANES
    q_specs = [pl.BlockSpec((SPAN, LANES), functools.partial(lambda b, hp, s, c: (b * spans + s, c + hp), c=qcol(g)))
               for g in range(N_GROUPS)]
    k_specs = [pl.BlockSpec((seq, LANES), functools.partial(lambda b, hp, s, c: (b, c + hp), c=kcol(g)))
               for g in range(N_GROUPS)]
    v_specs = [pl.BlockSpec((seq, LANES), functools.partial(lambda b, hp, s, c: (b, c + hp), c=vcol(g)))
               for g in range(N_GROUPS)]
    bias_spec = pl.BlockSpec((N_GROUPS, 2, 2, BLK, 2 * BLK), lambda b, hp, s: (0, 0, hp, 0, 0))
    return pl.pallas_call(
        functools.partial(_attn_kernel, seq=seq),
        grid=(batch, n_hp, spans),
        in_specs=q_specs + k_specs + v_specs + [bias_spec],
        out_specs=pl.BlockSpec((SPAN, LANES), lambda b, hp, s: (b * spans + s, hp)),
        out_shape=jax.ShapeDtypeStruct((n, GROUP_W), BF16),
        scratch_shapes=[
            pltpu.VMEM((2, SPAN, LANES), F32),
            pltpu.VMEM((4, seq, LANES), F32),
            pltpu.VMEM((6, SPAN, LANES), F32),
        ],
        compiler_params=_cparams(("arbitrary", "arbitrary", "arbitrary")),
        interpret=interpret,
        name="attn",
    )(*([proj] * 9), _attn_bias())


def _ret_kernel(q_ref, k_ref, v_ref, gr_ref, dec_ref, xi_ref, zeta_ref, gch_ref, o_ref, st_ref):
    @pl.when(pl.program_id(1) == 0)
    def _():
        st_ref[...] = jnp.zeros_like(st_ref)

    nt = (((1,), (1,)), ((), ()))
    scale = RET_DK ** -0.5

    def body(c, carry):
        r0 = pl.multiple_of(c * RET_CHUNK, RET_CHUNK)
        rows = pl.ds(r0, RET_CHUNK)
        for h in range(RET_HEADS):
            kcols = slice(h * RET_DK, (h + 1) * RET_DK)
            vcols = slice(h * RET_DV, (h + 1) * RET_DV)
            qi = q_ref[rows, kcols]
            kf = k_ref[rows, kcols].astype(F32) * scale
            ki = kf.astype(BF16)
            kz_t = jnp.transpose(kf * zeta_ref[h]).astype(BF16)
            vi = v_ref[rows, vcols]
            att = lax.dot_general(qi, ki, nt, preferred_element_type=F32) * dec_ref[h]
            inner = jnp.dot(att.astype(BF16), vi, preferred_element_type=F32)
            st = st_ref[h]
            cross = jnp.dot(qi, st.astype(BF16), preferred_element_type=F32) * xi_ref[h]
            st_ref[h] = gch_ref[h] * st + jnp.dot(kz_t, vi, preferred_element_type=F32)
            y = inner + cross
            mu = jnp.mean(y, axis=-1, keepdims=True)
            yc = y - mu
            var = jnp.mean(yc * yc, axis=-1, keepdims=True)
            yn = yc * lax.rsqrt(var + EPS)
            g = gr_ref[rows, vcols].astype(F32)
            o_ref[rows, vcols] = (g * jax.nn.sigmoid(g) * yn).astype(o_ref.dtype)
        return carry
    lax.fori_loop(0, RET_TS // RET_CHUNK, body, 0)


def _ret_tables():
    c = RET_CHUNK
    log_g = np.log1p(-np.exp2(-5.0 - np.arange(RET_HEADS, dtype=np.float64)))
    pos = np.arange(c, dtype=np.float64)
    diff = pos[:, None] - pos[None, :]
    dec = np.where(diff >= 0, np.exp(log_g[:, None, None] * np.maximum(diff, 0.0)), 0.0)
    xi = np.exp(log_g[:, None] * (pos + 1.0))[..., None] * np.ones((1, 1, RET_DV))
    zeta = np.exp(log_g[:, None] * (c - 1.0 - pos))[..., None] * np.ones((1, 1, RET_DK))
    gch = np.exp(log_g * c)[:, None, None] * np.ones((1, 1, RET_DV))
    return tuple(jnp.asarray(t, F32) for t in (dec, xi, zeta, gch))


def _retention(proj, batch, seq, interpret):
    n = batch * seq
    dec, xi, zeta, gch = _ret_tables()
    qk_w = RET_HEADS * RET_DK
    v_w = RET_HEADS * RET_DV
    nts = seq // RET_TS
    const3 = lambda b, t: (0, 0, 0)
    return pl.pallas_call(
        _ret_kernel,
        grid=(batch, nts),
        in_specs=[
            pl.BlockSpec((RET_TS, qk_w), lambda b, t: (b * nts + t, COL_QR // qk_w)),
            pl.BlockSpec((RET_TS, qk_w), lambda b, t: (b * nts + t, COL_KR // qk_w)),
            pl.BlockSpec((RET_TS, v_w), lambda b, t: (b * nts + t, COL_VR // v_w)),
            pl.BlockSpec((RET_TS, v_w), lambda b, t: (b * nts + t, COL_GR // v_w)),
            pl.BlockSpec((RET_HEADS, RET_CHUNK, RET_CHUNK), const3),
            pl.BlockSpec((RET_HEADS, RET_CHUNK, RET_DV), const3),
            pl.BlockSpec((RET_HEADS, RET_CHUNK, RET_DK), const3),
            pl.BlockSpec((RET_HEADS, 1, RET_DV), const3),
        ],
        out_specs=pl.BlockSpec((RET_TS, v_w), lambda b, t: (b * nts + t, 0)),
        out_shape=jax.ShapeDtypeStruct((n, v_w), BF16),
        scratch_shapes=[pltpu.VMEM((RET_HEADS, RET_DK, RET_DV), F32)],
        compiler_params=_cparams(("arbitrary", "arbitrary")),
        interpret=interpret,
        name="retention",
    )(proj, proj, proj, proj, dec, xi, zeta, gch)


ROUTER_OFF = N_EXPERT_GROUPS


def _pack_bf16_pair(a, b):
    hi = lax.bitcast_convert_type(a.astype(BF16).astype(F32), jnp.uint32)
    lo = lax.bitcast_convert_type(b.astype(BF16).astype(F32), jnp.uint32)
    return lax.bitcast_convert_type(hi | (lo >> 16), jnp.int32)


def _unpack_bf16_pair(w):
    u = lax.bitcast_convert_type(w, jnp.uint32)
    a = lax.bitcast_convert_type(u & jnp.uint32(0xFFFF0000), F32).astype(BF16)
    b = lax.bitcast_convert_type(u << 16, F32).astype(BF16)
    return a, b


def _pack_rows(y):
    q = D_MODEL // 4
    return (_pack_bf16_pair(y[:, 0:q], y[:, 2 * q:3 * q]), _pack_bf16_pair(y[:, q:2 * q], y[:, 3 * q:4 * q]))


def _unpack_rows(slab0, slab1):
    q0, q2 = _unpack_bf16_pair(slab0)
    q1, q3 = _unpack_bf16_pair(slab1)
    return jnp.concatenate([q0, q1, q2, q3], axis=1)


def _mix_kernel(oa_ref, or_ref, ga_ref, gr_ref, x_ref, pa_ref, pr_ref, wo_ref, gf_ref, wr_ref, br_ref,
                x1_ref, h2_ref, route_ref, route_t_ref, cnt_ref, carry_ref):
    @pl.when(pl.program_id(0) == 0)
    def _():
        carry_ref[...] = jnp.zeros_like(carry_ref)

    for c in range(MIX_TM // MIX_CHUNK):
        _mix_rows(pl.ds(c * MIX_CHUNK, MIX_CHUNK), oa_ref, or_ref, ga_ref, gr_ref, x_ref, pa_ref, pr_ref, wo_ref,
                  gf_ref, wr_ref, br_ref, x1_ref, h2_ref, route_ref, route_t_ref, cnt_ref, carry_ref)


def _mix_rows(rows, oa_ref, or_ref, ga_ref, gr_ref, x_ref, pa_ref, pr_ref, wo_ref, gf_ref, wr_ref, br_ref,
              x1_ref, h2_ref, route_ref, route_t_ref, cnt_ref, carry_ref):
    a = jnp.dot(oa_ref[rows, :], pa_ref[...], preferred_element_type=F32)
    r = jnp.dot(or_ref[rows, :], pr_ref[...], preferred_element_type=F32)
    merged = (jax.nn.sigmoid(ga_ref[rows, :].astype(F32)) * a + jax.nn.sigmoid(gr_ref[rows, :].astype(F32)) * r)
    x1 = x_ref[rows, :] + jnp.dot(merged.astype(BF16), wo_ref[...], preferred_element_type=F32)
    x1_ref[rows, :] = x1
    ms = jnp.mean(x1 * x1, axis=-1, keepdims=True)
    h2 = x1 * lax.rsqrt(ms + EPS) * gf_ref[...]
    h2_ref[0, rows, :], h2_ref[1, rows, :] = _pack_rows(h2)

    h_hi = h2.astype(BF16)
    h_lo = (h2 - h_hi.astype(F32)).astype(BF16)
    both = jnp.dot(h_hi, wr_ref[...], preferred_element_type=F32)
    logits = (both[:, :LANES] + both[:, LANES:]
              + jnp.dot(h_lo, wr_ref[:, :LANES], preferred_element_type=F32) + br_ref[...])
    tm = logits.shape[0]
    lane = lax.broadcasted_iota(jnp.int32, (tm, LANES), 1)
    big = jnp.int32(4 * LANES)
    ninf = -jnp.inf
    is_g = lane < N_EXPERT_GROUPS
    gl = jnp.where(is_g, logits, ninf)
    gmax = jnp.max(gl, axis=-1, keepdims=True)
    gsum = jnp.sum(jnp.where(is_g, jnp.exp(gl - gmax), 0.0), axis=-1, keepdims=True)
    g_val = 1.0 / gsum
    g_idx = jnp.min(jnp.where(jnp.logical_and(is_g, gl == gmax), lane, big), axis=-1, keepdims=True)
    lo = ROUTER_OFF + EXPERTS_PER_GROUP * g_idx
    in_grp = jnp.logical_and(lane >= lo, lane < lo + EXPERTS_PER_GROUP)
    el = jnp.where(in_grp, logits, ninf)
    v1 = jnp.max(el, axis=-1, keepdims=True)
    i1 = jnp.min(jnp.where(jnp.logical_and(in_grp, el == v1), lane, big), axis=-1, keepdims=True)
    rest = jnp.logical_and(in_grp, lane != i1)
    el2 = jnp.where(rest, logits, ninf)
    v2 = jnp.max(el2, axis=-1, keepdims=True)
    i2 = jnp.min(jnp.where(jnp.logical_and(rest, el2 == v2), lane, big), axis=-1, keepdims=True)
    t = jnp.exp(v2 - v1)
    w1 = g_val / (1.0 + t)
    w2 = g_val * t / (1.0 + t)

    sel = jnp.logical_or(lane == i1, lane == i2)
    sel_bf = jnp.where(sel, 1.0, 0.0).astype(BF16)
    row = lax.broadcasted_iota(jnp.int32, (tm, tm), 0)
    col = lax.broadcasted_iota(jnp.int32, (tm, tm), 1)
    tri = jnp.where(col < row, 1.0, 0.0).astype(BF16)
    before = jnp.dot(tri, sel_bf, preferred_element_type=F32) + carry_ref[...]
    r1 = jnp.sum(jnp.where(lane == i1, before, 0.0), axis=-1, keepdims=True)
    r2 = jnp.sum(jnp.where(lane == i2, before, 0.0), axis=-1, keepdims=True)
    carry = carry_ref[...] + jnp.sum(jnp.where(sel, 1.0, 0.0), axis=0, keepdims=True)
    carry_ref[...] = carry
    cnt_ref[...] = carry

    vals = ((i1 - ROUTER_OFF).astype(F32), (i2 - ROUTER_OFF).astype(F32), w1, w2, r1, r2)
    route = jnp.zeros((tm, LANES), F32)
    for j, v in enumerate(vals):
        route = jnp.where(lane == j, v, route)
    route_ref[rows, :] = route
    route_t_ref[:, rows] = jnp.transpose(route)[:ROUTE_ROWS, :]


def _mix(o_attn, o_ret, proj, x2, pa, pr, wo, g_ffn, w_router, b_router, interpret):
    n = x2.shape[0]
    tm = MIX_TM
    const = lambda i: (0, 0)
    return pl.pallas_call(
        _mix_kernel,
        grid=(n // tm,),
        in_specs=[
            pl.BlockSpec((tm, GROUP_W), lambda i: (i, 0)),
            pl.BlockSpec((tm, D_MODEL), lambda i: (i, 0)),
            pl.BlockSpec((tm, D_MODEL), lambda i: (i, COL_GATE_A // D_MODEL)),
            pl.BlockSpec((tm, D_MODEL), lambda i: (i, COL_GATE_R // D_MODEL)),
            pl.BlockSpec((tm, D_MODEL), lambda i: (i, 0)),
            pl.BlockSpec((GROUP_W, D_MODEL), const),
            pl.BlockSpec((D_MODEL, D_MODEL), const),
            pl.BlockSpec((D_MODEL, D_MODEL), const),
            pl.BlockSpec((1, D_MODEL), const),
            pl.BlockSpec((D_MODEL, 2 * LANES), const),
            pl.BlockSpec((1, LANES), const),
        ],
        out_specs=[
            pl.BlockSpec((tm, D_MODEL), lambda i: (i, 0)),
            pl.BlockSpec((2, tm, SC_ROW_WORDS), lambda i: (0, i, 0)),
            pl.BlockSpec((tm, LANES), lambda i: (i, 0)),
            pl.BlockSpec((ROUTE_ROWS, tm), lambda i: (0, i)),
            pl.BlockSpec((1, LANES), const),
        ],
        out_shape=[
            jax.ShapeDtypeStruct((n, D_MODEL), F32),
            jax.ShapeDtypeStruct((2, n, SC_ROW_WORDS), jnp.int32),
            jax.ShapeDtypeStruct((n, LANES), F32),
            jax.ShapeDtypeStruct((ROUTE_ROWS, n), F32),
            jax.ShapeDtypeStruct((1, LANES), F32),
        ],
        scratch_shapes=[pltpu.VMEM((1, LANES), F32)],
        compiler_params=_cparams(("arbitrary",)),
        interpret=interpret,
        name="mix_router",
    )(o_attn, o_ret, proj, proj, x2, pa, pr, wo, g_ffn, w_router, b_router)


def _expert_kernel(te_ref, tv_ref, xs_ref, wg_ref, wu_ref, wd_ref, o_ref, wg_s, wu_s, wd_s):
    i = pl.program_id(0)
    changed = jnp.logical_or(i == 0, te_ref[i] != te_ref[jnp.maximum(i - 1, 0)])

    @pl.when(changed)
    def _():
        wg_s[...] = wg_ref[0].astype(BF16)
        wu_s[...] = wu_ref[0].astype(BF16)
        wd_s[...] = wd_ref[0].astype(BF16)

    @pl.when(tv_ref[i] != 0)
    def _():
        xs = _unpack_rows(xs_ref[0], xs_ref[1])
        a = jnp.dot(xs, wg_s[...], preferred_element_type=F32)
        u = jnp.dot(xs, wu_s[...], preferred_element_type=F32)
        hid = (a * jax.nn.sigmoid(a) * u).astype(BF16)
        y = jnp.dot(hid, wd_s[...], preferred_element_type=F32)
        o_ref[0], o_ref[1] = _pack_rows(y)

    @pl.when(tv_ref[i] == 0)
    def _():
        o_ref[...] = jnp.zeros_like(o_ref)


def _experts(xs, tile_expert, tile_valid, w_gate, w_up, w_down, interpret):
    p = xs.shape[1]
    n_tiles = p // EXP_TM
    grid_spec = pltpu.PrefetchScalarGridSpec(
        num_scalar_prefetch=2,
        grid=(n_tiles,),
        in_specs=[
            pl.BlockSpec((2, EXP_TM, SC_ROW_WORDS), lambda i, te, tv: (0, i, 0)),
            pl.BlockSpec((1, D_MODEL, EXPERT_FF), lambda i, te, tv: (te[i], 0, 0)),
            pl.BlockSpec((1, D_MODEL, EXPERT_FF), lambda i, te, tv: (te[i], 0, 0)),
            pl.BlockSpec((1, EXPERT_FF, D_MODEL), lambda i, te, tv: (te[i], 0, 0)),
        ],
        out_specs=pl.BlockSpec((2, EXP_TM, SC_ROW_WORDS), lambda i, te, tv: (0, i, 0)),
        scratch_shapes=[
            pltpu.VMEM((D_MODEL, EXPERT_FF), BF16),
            pltpu.VMEM((D_MODEL, EXPERT_FF), BF16),
            pltpu.VMEM((EXPERT_FF, D_MODEL), BF16),
        ],
    )
    return pl.pallas_call(
        _expert_kernel,
        grid_spec=grid_spec,
        out_shape=jax.ShapeDtypeStruct((2, p, SC_ROW_WORDS), jnp.int32),
        compiler_params=_cparams(("arbitrary",)),
        interpret=interpret,
        name="experts",
    )(tile_expert, tile_valid, xs, w_gate, w_up, w_down)


def _final_kernel(x1_ref, yab_ref, route_ref, g_ref, o_ref):
    route = route_ref[...]
    w1 = route[:, 2:3]
    w2 = route[:, 3:4]
    ya = _unpack_rows(yab_ref[0], yab_ref[2]).astype(F32)
    yb = _unpack_rows(yab_ref[1], yab_ref[3]).astype(F32)
    x2 = x1_ref[...] + w1 * ya + w2 * yb
    ms = jnp.mean(x2 * x2, axis=-1, keepdims=True)
    o_ref[...] = x2 * lax.rsqrt(ms + EPS) * g_ref[...]


def _final(x1, yab, route, g_final, interpret):
    n = x1.shape[0]
    tm = FIN_TM
    row = lambda i: (i, 0)
    return pl.pallas_call(
        _final_kernel,
        grid=(n // tm,),
        in_specs=[
            pl.BlockSpec((tm, D_MODEL), row),
            pl.BlockSpec((4, tm, SC_ROW_WORDS), lambda i: (0, i, 0)),
            pl.BlockSpec((tm, LANES), row),
            pl.BlockSpec((1, D_MODEL), lambda i: (0, 0)),
        ],
        out_specs=pl.BlockSpec((tm, D_MODEL), row),
        out_shape=jax.ShapeDtypeStruct((n, D_MODEL), F32),
        compiler_params=_cparams(("arbitrary",)),
        interpret=interpret,
        name="combine_final",
    )(x1, yab, route, g_final)


def _permute_w_in(w_in):
    splits = np.cumsum([QKV_W, QKV_W, QKV_W, 512, 512, 1024, 1024, D_MODEL, D_MODEL])[:-1].tolist()
    qa, ka, va, qr, kr, vr, gr, gate_a, gate_r = jnp.split(w_in, splits, axis=-1)
    return jnp.concatenate([gate_a, gate_r, vr, gr, qa, ka, va, qr, kr], axis=-1).astype(BF16)


def _route_plan(route_t, counts, n):
    e1 = route_t[0].astype(jnp.int32)
    e2 = route_t[1].astype(jnp.int32)
    r1 = route_t[4].astype(jnp.int32)
    r2 = route_t[5].astype(jnp.int32)
    cnt = counts[0, ROUTER_OFF:ROUTER_OFF + N_EXPERTS].astype(jnp.int32)
    padded = ((cnt + EXP_TM - 1) // EXP_TM) * EXP_TM
    ends = jnp.cumsum(padded)
    offs = ends - padded
    lanes = jnp.arange(N_EXPERTS, dtype=jnp.int32)[None, :]
    dest1 = jnp.sum(jnp.where(e1[:, None] == lanes, offs[None, :], 0), axis=1) + r1
    dest2 = jnp.sum(jnp.where(e2[:, None] == lanes, offs[None, :], 0), axis=1) + r2
    n_rows = 2 * n + N_EXPERTS * EXP_TM
    n_tiles = n_rows // EXP_TM
    tile_start = jnp.arange(n_tiles, dtype=jnp.int32) * EXP_TM
    tile_valid = (tile_start < ends[-1]).astype(jnp.int32)
    te = jnp.sum((ends[None, :] <= tile_start[:, None]).astype(jnp.int32), axis=1)
    te_last = jnp.sum((ends[None, :] <= ends[-1] - EXP_TM).astype(jnp.int32))
    tile_expert = jnp.minimum(jnp.where(tile_valid == 1, te, te_last), N_EXPERTS - 1)
    return dest1, dest2, tile_expert, tile_valid, n_rows


def _sc_mesh():
    return plsc.VectorSubcoreMesh(core_axis_name="core", subcore_axis_name="subcore")


def _sc_scatter_rows(rows, idx_a, idx_b, n_out):
    n_in, w = rows.shape

    @functools.partial(pl.kernel, out_type=jax.ShapeDtypeStruct((n_out, w), rows.dtype), mesh=_sc_mesh(),
                       scratch_types=[], name="sc_scatter_rows")
    def scatter(x_hbm, ia_hbm, ib_hbm, o_hbm):
        def body(x_vmem, ia_vmem, ib_vmem):
            pltpu.sync_copy(x_vmem, o_hbm.at[ia_vmem.at[0]])
            pltpu.sync_copy(x_vmem, o_hbm.at[ib_vmem.at[0]])

        pltpu.emit_pipeline(
            body,
            grid=(n_in // SC_WINDOW,),
            in_specs=[pl.BlockSpec((SC_WINDOW, w), lambda i: (i, 0)),
                      pl.BlockSpec((1, SC_WINDOW), lambda i: (0, i)),
                      pl.BlockSpec((1, SC_WINDOW), lambda i: (0, i))],
            out_specs=[],
            core_axis_name=("core", "subcore"),
            dimension_semantics=(pltpu.PARALLEL,),
        )(x_hbm, ia_hbm, ib_hbm)

    return scatter(rows, idx_a, idx_b)


def _sc_gather_rows(table, idx):
    n_idx = idx.shape[1]
    w = table.shape[1]

    @functools.partial(pl.kernel, out_type=jax.ShapeDtypeStruct((n_idx, w), table.dtype), mesh=_sc_mesh(),
                       scratch_types=[], name="sc_gather_rows")
    def gather(t_hbm, i_hbm, o_hbm):
        def body(i_vmem, o_vmem):
            pltpu.sync_copy(t_hbm.at[i_vmem.at[0]], o_vmem)

        pltpu.emit_pipeline(
            body,
            grid=(n_idx // SC_WINDOW,),
            in_specs=[pl.BlockSpec((1, SC_WINDOW), lambda i: (0, i))],
            out_specs=[pl.BlockSpec((SC_WINDOW, w), lambda i: (i, 0))],
            core_axis_name=("core", "subcore"),
            dimension_semantics=(pltpu.PARALLEL,),
        )(i_hbm, o_hbm)

    return gather(table, idx)


def _forward(x, g_mix, w_in, w_attn_branch, w_ret_branch, w_out, g_ffn, w_group_router, b_group_router,
             w_expert_router, b_expert_router, w_gate, w_up, w_down, g_final, interpret=False):
    batch, seq, d = x.shape
    n = batch * seq
    x2 = x.reshape(n, d)
    proj = _proj(x2, g_mix[0][None, :], _permute_w_in(w_in[0]), interpret)
    o_attn = _attention(proj, batch, seq, interpret)
    o_ret = _retention(proj, batch, seq, interpret)
    pad = LANES - N_EXPERT_GROUPS - N_EXPERTS
    w_router = jnp.concatenate([w_group_router[0], w_expert_router[0], jnp.zeros((d, pad), F32)], axis=-1)
    w_router_hi = w_router.astype(BF16)
    w_router_lo = (w_router - w_router_hi.astype(F32)).astype(BF16)
    w_router2 = jnp.concatenate([w_router_hi, w_router_lo], axis=-1)
    b_router = jnp.concatenate([b_group_router[0], b_expert_router[0], jnp.zeros((pad,), F32)])[None, :]
    x1, h2p, route, route_t, counts = _mix(o_attn, o_ret, proj, x2, w_attn_branch[0].astype(BF16),
                                           w_ret_branch[0].astype(BF16), w_out[0].astype(BF16),
                                           g_ffn[0][None, :], w_router2, b_router, interpret)
    dest1, dest2, tile_expert, tile_valid, n_rows = _route_plan(route_t, counts, n)
    idx_a = jnp.concatenate([dest1, dest1 + n_rows])[None, :]
    idx_b = jnp.concatenate([dest2, dest2 + n_rows])[None, :]
    xs = _sc_scatter_rows(h2p.reshape(2 * n, SC_ROW_WORDS), idx_a, idx_b, 2 * n_rows)
    ys = _experts(xs.reshape(2, n_rows, SC_ROW_WORDS), tile_expert, tile_valid, w_gate[0], w_up[0], w_down[0],
                  interpret)
    idx_g = jnp.concatenate([dest1, dest2, dest1 + n_rows, dest2 + n_rows])[None, :]
    yab = _sc_gather_rows(ys.reshape(2 * n_rows, SC_ROW_WORDS), idx_g)
    out = _final(x1, yab.reshape(4, n, SC_ROW_WORDS), route, g_final[None, :], interpret)
    return out.reshape(batch, seq, d)


def kernel(x, g_mix, w_in, w_attn_branch, w_ret_branch, w_out, g_ffn, w_group_router, b_group_router,
           w_expert_router, b_expert_router, w_gate, w_up, w_down, g_final):
    return _forward(x, g_mix, w_in, w_attn_branch, w_ret_branch, w_out, g_ffn, w_group_router,
                    b_group_router, w_expert_router, b_expert_router, w_gate, w_up, w_down, g_final)
```

```python
import functools

import numpy as np
import jax
import jax.numpy as jnp
from jax import lax
from jax.experimental import pallas as pl
from jax.experimental.pallas import tpu as pltpu
from jax.experimental.pallas import tpu_sc as plsc

F32 = jnp.float32
BF16 = jnp.bfloat16

D_MODEL = 1024
ATTN_GROUPS = ((128, 1), (512, 4), (2048, 16))
N_GROUPS = len(ATTN_GROUPS)
ATTN_HEADS = 8
HEAD_DIM = 64
GROUP_W = ATTN_HEADS * HEAD_DIM
QKV_W = N_GROUPS * GROUP_W
RET_HEADS = 4
RET_DK = 128
RET_DV = 256
RET_CHUNK = 128
RET_TS = 1024
N_EXPERT_GROUPS = 4
EXPERTS_PER_GROUP = 8
N_EXPERTS = N_EXPERT_GROUPS * EXPERTS_PER_GROUP
EXPERT_FF = 512
EPS = 1e-6

LANES = 128
BLK = 128
SPAN = 2048
NEG = -1e30
ATTN_UNROLL = 8

COL_GATE_A = 0
COL_GATE_R = 1024
COL_VR = 2048
COL_GR = COL_VR + RET_HEADS * RET_DV
COL_QA = COL_GR + RET_HEADS * RET_DV
COL_KA = COL_QA + QKV_W
COL_VA = COL_KA + QKV_W
COL_QR = COL_VA + QKV_W
COL_KR = COL_QR + RET_HEADS * RET_DK
IN_WIDTH = COL_KR + RET_HEADS * RET_DK

PROJ_TM = 512
PROJ_TN = IN_WIDTH // 2
MXU_N = 256
MIX_TM = 512
MIX_CHUNK = 256
ROUTE_ROWS = 8
EXP_TM = 256
SC_WINDOW = 128
SC_ROW_WORDS = 256
FIN_TM = 512
VMEM_LIMIT = 56 * 1024 * 1024


def _cparams(sem):
    return pltpu.CompilerParams(dimension_semantics=sem, vmem_limit_bytes=VMEM_LIMIT)


def _sigmoid(x):
    return 0.5 * jnp.tanh(0.5 * x) + 0.5


def _proj_kernel(x_ref, g_ref, w_ref, o_ref):
    x = x_ref[...]
    ms = jnp.mean(x * x, axis=-1, keepdims=True)
    h = (x * lax.rsqrt(ms + EPS) * g_ref[...]).astype(BF16)
    for c in range(PROJ_TN // MXU_N):
        sl = slice(c * MXU_N, (c + 1) * MXU_N)
        o_ref[:, sl] = jnp.dot(h, w_ref[:, sl], preferred_element_type=F32).astype(o_ref.dtype)


def _proj(x2, g_mix, w_in_bf16, interpret):
    n = x2.shape[0]
    return pl.pallas_call(
        _proj_kernel,
        grid=(IN_WIDTH // PROJ_TN, n // PROJ_TM),
        in_specs=[
            pl.BlockSpec((PROJ_TM, D_MODEL), lambda j, i: (i, 0)),
            pl.BlockSpec((1, D_MODEL), lambda j, i: (0, 0)),
            pl.BlockSpec((D_MODEL, PROJ_TN), lambda j, i: (0, j)),
        ],
        out_specs=pl.BlockSpec((PROJ_TM, PROJ_TN), lambda j, i: (i, j)),
        out_shape=jax.ShapeDtypeStruct((n, IN_WIDTH), BF16),
        compiler_params=_cparams(("arbitrary", "arbitrary")),
        interpret=interpret,
        name="proj",
    )(x2, g_mix, w_in_bf16)


def _attn_unit(q2, kk, vv, bias_a, bias_b):
    lane = lax.broadcasted_iota(jnp.int32, (BLK, LANES), 1)
    left = lane < HEAD_DIM
    zero = jnp.zeros_like(q2)
    nt = (((1,), (1,)), ((), ()))
    q_st = jnp.concatenate([jnp.where(left, q2, zero), jnp.where(left, zero, q2)], axis=0)
    s = lax.dot_general(q_st, kk, nt, preferred_element_type=F32) + jnp.concatenate([bias_a, bias_b], axis=0)
    m = jnp.max(s, axis=-1, keepdims=True)
    p = jnp.exp(s - m)
    den = jnp.sum(p, axis=-1, keepdims=True)
    o = jnp.dot(p.astype(BF16), vv, preferred_element_type=F32)
    return (jnp.where(left, o[:BLK], o[BLK:]), jnp.where(left, m[:BLK], m[BLK:]),
            jnp.where(left, den[:BLK], den[BLK:]))


def _attn_kernel(q1_ref, q2_ref, q3_ref, k1_ref, k2_ref, k3_ref, v1_ref, v2_ref, v3_ref,
                 bias_ref, o_ref, qf_ref, kvf_ref, acc_ref, *, seq):
    s_id = pl.program_id(2)
    conv_rows = 256

    @pl.when(s_id == 0)
    def _():
        for slot, ref in enumerate((k2_ref, v2_ref, k3_ref, v3_ref)):
            def body(i, c, slot=slot, ref=ref):
                r0 = pl.multiple_of(i * conv_rows, conv_rows)
                kvf_ref[slot, pl.ds(r0, conv_rows), :] = ref[pl.ds(r0, conv_rows), :].astype(F32)
                return c
            lax.fori_loop(0, seq // conv_rows, body, 0)

    for slot, ref in enumerate((q2_ref, q3_ref)):
        def body(i, c, slot=slot, ref=ref):
            r0 = pl.multiple_of(i * conv_rows, conv_rows)
            qf_ref[slot, pl.ds(r0, conv_rows), :] = ref[pl.ds(r0, conv_rows), :].astype(F32) * 0.125
            return c
        lax.fori_loop(0, SPAN // conv_rows, body, 0)

    def dilated_unit(gi, d, m, r, first):
        slot = gi - 1
        loc = BLK * m * d + r
        cur = s_id * SPAN + loc
        prev = jnp.maximum(cur - BLK * d, r)
        q2 = qf_ref[slot, pl.ds(loc, BLK, stride=d), :].astype(BF16)
        kk = jnp.concatenate([kvf_ref[2 * slot, pl.ds(prev, BLK, stride=d), :],
                              kvf_ref[2 * slot, pl.ds(cur, BLK, stride=d), :]], axis=0).astype(BF16)
        vv = jnp.concatenate([kvf_ref[2 * slot + 1, pl.ds(prev, BLK, stride=d), :],
                              kvf_ref[2 * slot + 1, pl.ds(cur, BLK, stride=d), :]], axis=0).astype(BF16)
        num, mx, den = _attn_unit(q2, kk, vv, bias_ref[gi, first, 0], bias_ref[gi, first, 1])
        acc_ref[3 * slot + 0, pl.ds(loc, BLK, stride=d), :] = num
        acc_ref[3 * slot + 1, pl.ds(loc, BLK, stride=d), :] = mx
        acc_ref[3 * slot + 2, pl.ds(loc, BLK, stride=d), :] = den

    d3 = ATTN_GROUPS[2][1]
    first_span = jnp.where(s_id == 0, 1, 0)

    def body3(i, c):
        for u in range(ATTN_UNROLL):
            dilated_unit(2, d3, 0, i * ATTN_UNROLL + u, first_span)
        return c
    lax.fori_loop(0, d3 // ATTN_UNROLL, body3, 0)

    d2 = ATTN_GROUPS[1][1]

    blocks2 = ATTN_UNROLL // d2

    def body2(i, c):
        for mm in range(blocks2):
            m = i * blocks2 + mm
            first = jnp.where(jnp.logical_and(s_id == 0, m == 0), 1, 0)
            for r in range(d2):
                dilated_unit(1, d2, m, r, first)
        return c
    lax.fori_loop(0, SPAN // (BLK * d2 * blocks2), body2, 0)

    def dense_unit(m):
        loc = pl.multiple_of(m * BLK, BLK)
        cur = pl.multiple_of(s_id * SPAN + loc, BLK)
        prev = pl.multiple_of(jnp.maximum(cur - BLK, 0), BLK)
        first = jnp.where(cur == 0, 1, 0)
        q2 = q1_ref[pl.ds(loc, BLK), :] * 0.125
        kk = jnp.concatenate([k1_ref[pl.ds(prev, BLK), :], k1_ref[pl.ds(cur, BLK), :]], axis=0)
        vv = jnp.concatenate([v1_ref[pl.ds(prev, BLK), :], v1_ref[pl.ds(cur, BLK), :]], axis=0)
        n1, m1, d1 = _attn_unit(q2, kk, vv, bias_ref[0, first, 0], bias_ref[0, first, 1])
        n2, m2, dd2 = (acc_ref[j, pl.ds(loc, BLK), :] for j in (0, 1, 2))
        n3, m3, dd3 = (acc_ref[j, pl.ds(loc, BLK), :] for j in (3, 4, 5))
        mx = jnp.maximum(jnp.maximum(m1, m2), m3)
        w1, w2, w3 = jnp.exp(m1 - mx), jnp.exp(m2 - mx), jnp.exp(m3 - mx)
        num = w1 * n1 + w2 * n2 + w3 * n3
        den = w1 * d1 + w2 * dd2 + w3 * dd3
        o_ref[pl.ds(loc, BLK), :] = (num / den).astype(o_ref.dtype)

    def body1(i, c):
        for u in range(ATTN_UNROLL):
            dense_unit(i * ATTN_UNROLL + u)
        return c
    lax.fori_loop(0, SPAN // (BLK * ATTN_UNROLL), body1, 0)


def _attn_bias():
    slopes = np.exp2(-8.0 * np.arange(1, ATTN_HEADS + 1, dtype=np.float64) / ATTN_HEADS)
    qi = np.arange(BLK)[:, None]
    kj = np.arange(2 * BLK)[None, :]
    rel = qi + BLK - kj
    out = np.zeros((N_GROUPS, 2, ATTN_HEADS, BLK, 2 * BLK), np.float32)
    for gi, (window, d) in enumerate(ATTN_GROUPS):
        n_back = window // d
        assert n_back == BLK
        valid = (rel >= 0) & (rel <= n_back)
        bias = -slopes[:, None, None] * (rel * d)[None].astype(np.float64)
        out[gi, 0] = np.where(valid[None], bias, NEG)
        out[gi, 1] = np.where((valid & (kj >= BLK))[None], bias, NEG)
    return jnp.asarray(out)


def _attention(proj, batch, seq, interpret):
    n = batch * seq
    spans = seq // SPAN
    n_hp = GROUP_W // LANES
    qcol = lambda g: (COL_QA + g * GROUP_W) // LANES
    kcol = lambda g: (COL_KA + g * GROUP_W) // LANES
    vcol = lambda g: (COL_VA + g * GROUP_W) // LANES
    q_specs = [pl.BlockSpec((SPAN, LANES), functools.partial(lambda b, hp, s, c: (b * spans + s, c + hp), c=qcol(g)))
               for g in range(N_GROUPS)]
    k_specs = [pl.BlockSpec((seq, LANES), functools.partial(lambda b, hp, s, c: (b, c + hp), c=kcol(g)))
               for g in range(N_GROUPS)]
    v_specs = [pl.BlockSpec((seq, LANES), functools.partial(lambda b, hp, s, c: (b, c + hp), c=vcol(g)))
               for g in range(N_GROUPS)]
    bias_spec = pl.BlockSpec((N_GROUPS, 2, 2, BLK, 2 * BLK), lambda b, hp, s: (0, 0, hp, 0, 0))
    return pl.pallas_call(
        functools.partial(_attn_kernel, seq=seq),
        grid=(batch, n_hp, spans),
        in_specs=q_specs + k_specs + v_specs + [bias_spec],
        out_specs=pl.BlockSpec((SPAN, LANES), lambda b, hp, s: (b * spans + s, hp)),
        out_shape=jax.ShapeDtypeStruct((n, GROUP_W), BF16),
        scratch_shapes=[
            pltpu.VMEM((2, SPAN, LANES), F32),
            pltpu.VMEM((4, seq, LANES), F32),
            pltpu.VMEM((6, SPAN, LANES), F32),
        ],
        compiler_params=_cparams(("arbitrary", "arbitrary", "arbitrary")),
        interpret=interpret,
        name="attn",
    )(*([proj] * 9), _attn_bias())


def _ret_kernel(q_ref, k_ref, v_ref, gr_ref, dec_ref, xi_ref, zeta_ref, gch_ref, o_ref, st_ref):
    @pl.when(pl.program_id(1) == 0)
    def _():
        st_ref[...] = jnp.zeros_like(st_ref)

    nt = (((1,), (1,)), ((), ()))
    scale = RET_DK ** -0.5

    def body(c, carry):
        r0 = pl.multiple_of(c * RET_CHUNK, RET_CHUNK)
        rows = pl.ds(r0, RET_CHUNK)
        for h in range(RET_HEADS):
            kcols = slice(h * RET_DK, (h + 1) * RET_DK)
            vcols = slice(h * RET_DV, (h + 1) * RET_DV)
            qi = q_ref[rows, kcols]
            kf = k_ref[rows, kcols].astype(F32) * scale
            ki = kf.astype(BF16)
            kz_t = jnp.transpose(kf * zeta_ref[h]).astype(BF16)
            vi = v_ref[rows, vcols]
            att = lax.dot_general(qi, ki, nt, preferred_element_type=F32) * dec_ref[h]
            inner = jnp.dot(att.astype(BF16), vi, preferred_element_type=F32)
            st = st_ref[h]
            cross = jnp.dot(qi, st.astype(BF16), preferred_element_type=F32) * xi_ref[h]
            st_ref[h] = gch_ref[h] * st + jnp.dot(kz_t, vi, preferred_element_type=F32)
            y = inner + cross
            mu = jnp.mean(y, axis=-1, keepdims=True)
            yc = y - mu
            var = jnp.mean(yc * yc, axis=-1, keepdims=True)
            yn = yc * lax.rsqrt(var + EPS)
            g = gr_ref[rows, vcols].astype(F32)
            o_ref[rows, vcols] = (g * _sigmoid(g) * yn).astype(o_ref.dtype)
        return carry
    lax.fori_loop(0, RET_TS // RET_CHUNK, body, 0)


def _ret_tables():
    c = RET_CHUNK
    log_g = np.log1p(-np.exp2(-5.0 - np.arange(RET_HEADS, dtype=np.float64)))
    pos = np.arange(c, dtype=np.float64)
    diff = pos[:, None] - pos[None, :]
    dec = np.where(diff >= 0, np.exp(log_g[:, None, None] * np.maximum(diff, 0.0)), 0.0)
    xi = np.exp(log_g[:, None] * (pos + 1.0))[..., None] * np.ones((1, 1, RET_DV))
    zeta = np.exp(log_g[:, None] * (c - 1.0 - pos))[..., None] * np.ones((1, 1, RET_DK))
    gch = np.exp(log_g * c)[:, None, None] * np.ones((1, 1, RET_DV))
    return tuple(jnp.asarray(t, F32) for t in (dec, xi, zeta, gch))


def _retention(proj, batch, seq, interpret):
    n = batch * seq
    dec, xi, zeta, gch = _ret_tables()
    qk_w = RET_HEADS * RET_DK
    v_w = RET_HEADS * RET_DV
    nts = seq // RET_TS
    const3 = lambda b, t: (0, 0, 0)
    return pl.pallas_call(
        _ret_kernel,
        grid=(batch, nts),
        in_specs=[
            pl.BlockSpec((RET_TS, qk_w), lambda b, t: (b * nts + t, COL_QR // qk_w)),
            pl.BlockSpec((RET_TS, qk_w), lambda b, t: (b * nts + t, COL_KR // qk_w)),
            pl.BlockSpec((RET_TS, v_w), lambda b, t: (b * nts + t, COL_VR // v_w)),
            pl.BlockSpec((RET_TS, v_w), lambda b, t: (b * nts + t, COL_GR // v_w)),
            pl.BlockSpec((RET_HEADS, RET_CHUNK, RET_CHUNK), const3),
            pl.BlockSpec((RET_HEADS, RET_CHUNK, RET_DV), const3),
            pl.BlockSpec((RET_HEADS, RET_CHUNK, RET_DK), const3),
            pl.BlockSpec((RET_HEADS, 1, RET_DV), const3),
        ],
        out_specs=pl.BlockSpec((RET_TS, v_w), lambda b, t: (b * nts + t, 0)),
        out_shape=jax.ShapeDtypeStruct((n, v_w), BF16),
        scratch_shapes=[pltpu.VMEM((RET_HEADS, RET_DK, RET_DV), F32)],
        compiler_params=_cparams(("arbitrary", "arbitrary")),
        interpret=interpret,
        name="retention",
    )(proj, proj, proj, proj, dec, xi, zeta, gch)


ROUTER_OFF = N_EXPERT_GROUPS


def _pack_bf16_pair(a, b):
    hi = lax.bitcast_convert_type(a.astype(BF16).astype(F32), jnp.uint32)
    lo = lax.bitcast_convert_type(b.astype(BF16).astype(F32), jnp.uint32)
    return lax.bitcast_convert_type(hi | (lo >> 16), jnp.int32)


def _unpack_bf16_pair(w):
    u = lax.bitcast_convert_type(w, jnp.uint32)
    a = lax.bitcast_convert_type(u & jnp.uint32(0xFFFF0000), F32).astype(BF16)
    b = lax.bitcast_convert_type(u << 16, F32).astype(BF16)
    return a, b


def _pack_rows(y):
    q = D_MODEL // 4
    return (_pack_bf16_pair(y[:, 0:q], y[:, 2 * q:3 * q]), _pack_bf16_pair(y[:, q:2 * q], y[:, 3 * q:4 * q]))


def _unpack_rows(slab0, slab1):
    q0, q2 = _unpack_bf16_pair(slab0)
    q1, q3 = _unpack_bf16_pair(slab1)
    return jnp.concatenate([q0, q1, q2, q3], axis=1)


def _mix_kernel(oa_ref, or_ref, ga_ref, gr_ref, x_ref, pa_ref, pr_ref, wo_ref, gf_ref, wr_ref, br_ref,
                x1_ref, h2_ref, route_ref, route_t_ref, cnt_ref, carry_ref):
    @pl.when(pl.program_id(0) == 0)
    def _():
        carry_ref[...] = jnp.zeros_like(carry_ref)

    for c in range(MIX_TM // MIX_CHUNK):
        _mix_rows(pl.ds(c * MIX_CHUNK, MIX_CHUNK), oa_ref, or_ref, ga_ref, gr_ref, x_ref, pa_ref, pr_ref, wo_ref,
                  gf_ref, wr_ref, br_ref, x1_ref, h2_ref, route_ref, route_t_ref, cnt_ref, carry_ref)


def _mix_rows(rows, oa_ref, or_ref, ga_ref, gr_ref, x_ref, pa_ref, pr_ref, wo_ref, gf_ref, wr_ref, br_ref,
              x1_ref, h2_ref, route_ref, route_t_ref, cnt_ref, carry_ref):
    a = jnp.dot(oa_ref[rows, :], pa_ref[...], preferred_element_type=F32)
    r = jnp.dot(or_ref[rows, :], pr_ref[...], preferred_element_type=F32)
    merged = (_sigmoid(ga_ref[rows, :].astype(F32)) * a + _sigmoid(gr_ref[rows, :].astype(F32)) * r)
    x1 = x_ref[rows, :] + jnp.dot(merged.astype(BF16), wo_ref[...], preferred_element_type=F32)
    x1_ref[rows, :] = x1
    ms = jnp.mean(x1 * x1, axis=-1, keepdims=True)
    h2 = x1 * lax.rsqrt(ms + EPS) * gf_ref[...]
    h2_ref[0, rows, :], h2_ref[1, rows, :] = _pack_rows(h2)

    h_hi = h2.astype(BF16)
    h_lo = (h2 - h_hi.astype(F32)).astype(BF16)
    both = jnp.dot(h_hi, wr_ref[...], preferred_element_type=F32)
    logits = (both[:, :LANES] + both[:, LANES:]
              + jnp.dot(h_lo, wr_ref[:, :LANES], preferred_element_type=F32) + br_ref[...])
    tm = logits.shape[0]
    lane = lax.broadcasted_iota(jnp.int32, (tm, LANES), 1).astype(F32)
    big = jnp.float32(4 * LANES)
    ninf = -jnp.inf
    is_g = lane < N_EXPERT_GROUPS
    gl = jnp.where(is_g, logits, ninf)
    gmax = jnp.max(gl, axis=-1, keepdims=True)
    gsum = jnp.sum(jnp.where(is_g, jnp.exp(gl - gmax), 0.0), axis=-1, keepdims=True)
    g_val = 1.0 / gsum
    g_idx = jnp.min(jnp.where(jnp.logical_and(is_g, gl == gmax), lane, big), axis=-1, keepdims=True)
    lo = ROUTER_OFF + EXPERTS_PER_GROUP * g_idx
    in_grp = jnp.logical_and(lane >= lo, lane < lo + EXPERTS_PER_GROUP)
    el = jnp.where(in_grp, logits, ninf)
    v1 = jnp.max(el, axis=-1, keepdims=True)
    i1 = jnp.min(jnp.where(jnp.logical_and(in_grp, el == v1), lane, big), axis=-1, keepdims=True)
    rest = jnp.logical_and(in_grp, lane != i1)
    el2 = jnp.where(rest, logits, ninf)
    v2 = jnp.max(el2, axis=-1, keepdims=True)
    i2 = jnp.min(jnp.where(jnp.logical_and(rest, el2 == v2), lane, big), axis=-1, keepdims=True)
    t = jnp.exp(v2 - v1)
    w1 = g_val / (1.0 + t)
    w2 = g_val * t / (1.0 + t)

    sel = jnp.logical_or(lane == i1, lane == i2)
    sel_bf = jnp.where(sel, 1.0, 0.0).astype(BF16)
    row = lax.broadcasted_iota(jnp.int32, (tm, tm), 0)
    col = lax.broadcasted_iota(jnp.int32, (tm, tm), 1)
    tri = jnp.where(col < row, 1.0, 0.0).astype(BF16)
    before = jnp.dot(tri, sel_bf, preferred_element_type=F32) + carry_ref[...]
    r1 = jnp.sum(jnp.where(lane == i1, before, 0.0), axis=-1, keepdims=True)
    r2 = jnp.sum(jnp.where(lane == i2, before, 0.0), axis=-1, keepdims=True)
    carry = carry_ref[...] + jnp.sum(jnp.where(sel, 1.0, 0.0), axis=0, keepdims=True)
    carry_ref[...] = carry
    cnt_ref[...] = carry

    vals = (i1 - ROUTER_OFF, i2 - ROUTER_OFF, w1, w2, r1, r2)
    route = jnp.zeros((tm, LANES), F32)
    for j, v in enumerate(vals):
        route = jnp.where(lane == j, v, route)
    route_ref[rows, :] = route
    route_t_ref[:, rows] = jnp.transpose(route)[:ROUTE_ROWS, :]


def _mix(o_attn, o_ret, proj, x2, pa, pr, wo, g_ffn, w_router, b_router, interpret):
    n = x2.shape[0]
    tm = MIX_TM
    const = lambda i: (0, 0)
    return pl.pallas_call(
        _mix_kernel,
        grid=(n // tm,),
        in_specs=[
            pl.BlockSpec((tm, GROUP_W), lambda i: (i, 0)),
            pl.BlockSpec((tm, D_MODEL), lambda i: (i, 0)),
            pl.BlockSpec((tm, D_MODEL), lambda i: (i, COL_GATE_A // D_MODEL)),
            pl.BlockSpec((tm, D_MODEL), lambda i: (i, COL_GATE_R // D_MODEL)),
            pl.BlockSpec((tm, D_MODEL), lambda i: (i, 0)),
            pl.BlockSpec((GROUP_W, D_MODEL), const),
            pl.BlockSpec((D_MODEL, D_MODEL), const),
            pl.BlockSpec((D_MODEL, D_MODEL), const),
            pl.BlockSpec((1, D_MODEL), const),
            pl.BlockSpec((D_MODEL, 2 * LANES), const),
            pl.BlockSpec((1, LANES), const),
        ],
        out_specs=[
            pl.BlockSpec((tm, D_MODEL), lambda i: (i, 0)),
            pl.BlockSpec((2, tm, SC_ROW_WORDS), lambda i: (0, i, 0)),
            pl.BlockSpec((tm, LANES), lambda i: (i, 0)),
            pl.BlockSpec((ROUTE_ROWS, tm), lambda i: (0, i)),
            pl.BlockSpec((1, LANES), const),
        ],
        out_shape=[
            jax.ShapeDtypeStruct((n, D_MODEL), F32),
            jax.ShapeDtypeStruct((2, n, SC_ROW_WORDS), jnp.int32),
            jax.ShapeDtypeStruct((n, LANES), F32),
            jax.ShapeDtypeStruct((ROUTE_ROWS, n), F32),
            jax.ShapeDtypeStruct((1, LANES), F32),
        ],
        scratch_shapes=[pltpu.VMEM((1, LANES), F32)],
        compiler_params=_cparams(("arbitrary",)),
        interpret=interpret,
        name="mix_router",
    )(o_attn, o_ret, proj, proj, x2, pa, pr, wo, g_ffn, w_router, b_router)


def _expert_kernel(te_ref, tv_ref, xs_ref, wg_ref, wu_ref, wd_ref, o_ref, wg_s, wu_s, wd_s):
    i = pl.program_id(0)
    changed = jnp.logical_or(i == 0, te_ref[i] != te_ref[jnp.maximum(i - 1, 0)])

    @pl.when(changed)
    def _():
        wg_s[...] = wg_ref[0].astype(BF16)
        wu_s[...] = wu_ref[0].astype(BF16)
        wd_s[...] = wd_ref[0].astype(BF16)

    @pl.when(tv_ref[i] != 0)
    def _():
        xs = _unpack_rows(xs_ref[0], xs_ref[1])
        a = jnp.dot(xs, wg_s[...], preferred_element_type=F32)
        u = jnp.dot(xs, wu_s[...], preferred_element_type=F32)
        hid = (a * _sigmoid(a) * u).astype(BF16)
        y = jnp.dot(hid, wd_s[...], preferred_element_type=F32)
        o_ref[0], o_ref[1] = _pack_rows(y)

    @pl.when(tv_ref[i] == 0)
    def _():
        o_ref[...] = jnp.zeros_like(o_ref)


def _experts(xs, tile_expert, tile_valid, w_gate, w_up, w_down, interpret):
    p = xs.shape[1]
    n_tiles = p // EXP_TM
    grid_spec = pltpu.PrefetchScalarGridSpec(
        num_scalar_prefetch=2,
        grid=(n_tiles,),
        in_specs=[
            pl.BlockSpec((2, EXP_TM, SC_ROW_WORDS), lambda i, te, tv: (0, i, 0)),
            pl.BlockSpec((1, D_MODEL, EXPERT_FF), lambda i, te, tv: (te[i], 0, 0)),
            pl.BlockSpec((1, D_MODEL, EXPERT_FF), lambda i, te, tv: (te[i], 0, 0)),
            pl.BlockSpec((1, EXPERT_FF, D_MODEL), lambda i, te, tv: (te[i], 0, 0)),
        ],
        out_specs=pl.BlockSpec((2, EXP_TM, SC_ROW_WORDS), lambda i, te, tv: (0, i, 0)),
        scratch_shapes=[
            pltpu.VMEM((D_MODEL, EXPERT_FF), BF16),
            pltpu.VMEM((D_MODEL, EXPERT_FF), BF16),
            pltpu.VMEM((EXPERT_FF, D_MODEL), BF16),
        ],
    )
    return pl.pallas_call(
        _expert_kernel,
        grid_spec=grid_spec,
        out_shape=jax.ShapeDtypeStruct((2, p, SC_ROW_WORDS), jnp.int32),
        compiler_params=_cparams(("arbitrary",)),
        interpret=interpret,
        name="experts",
    )(tile_expert, tile_valid, xs, w_gate, w_up, w_down)


def _final_kernel(x1_ref, yab_ref, route_ref, g_ref, o_ref):
    route = route_ref[...]
    w1 = route[:, 2:3]
    w2 = route[:, 3:4]
    ya = _unpack_rows(yab_ref[0], yab_ref[2]).astype(F32)
    yb = _unpack_rows(yab_ref[1], yab_ref[3]).astype(F32)
    x2 = x1_ref[...] + w1 * ya + w2 * yb
    ms = jnp.mean(x2 * x2, axis=-1, keepdims=True)
    o_ref[...] = x2 * lax.rsqrt(ms + EPS) * g_ref[...]


def _final(x1, yab, route, g_final, interpret):
    n = x1.shape[0]
    tm = FIN_TM
    row = lambda i: (i, 0)
    return pl.pallas_call(
        _final_kernel,
        grid=(n // tm,),
        in_specs=[
            pl.BlockSpec((tm, D_MODEL), row),
            pl.BlockSpec((4, tm, SC_ROW_WORDS), lambda i: (0, i, 0)),
            pl.BlockSpec((tm, LANES), row),
            pl.BlockSpec((1, D_MODEL), lambda i: (0, 0)),
        ],
        out_specs=pl.BlockSpec((tm, D_MODEL), row),
        out_shape=jax.ShapeDtypeStruct((n, D_MODEL), F32),
        compiler_params=_cparams(("arbitrary",)),
        interpret=interpret,
        name="combine_final",
    )(x1, yab, route, g_final)


def _permute_w_in(w_in):
    splits = np.cumsum([QKV_W, QKV_W, QKV_W, 512, 512, 1024, 1024, D_MODEL, D_MODEL])[:-1].tolist()
    qa, ka, va, qr, kr, vr, gr, gate_a, gate_r = jnp.split(w_in, splits, axis=-1)
    return jnp.concatenate([gate_a, gate_r, vr, gr, qa, ka, va, qr, kr], axis=-1).astype(BF16)


def _route_plan(route_t, counts, n):
    e1 = route_t[0].astype(jnp.int32)
    e2 = route_t[1].astype(jnp.int32)
    r1 = route_t[4].astype(jnp.int32)
    r2 = route_t[5].astype(jnp.int32)
    cnt = counts[0, ROUTER_OFF:ROUTER_OFF + N_EXPERTS].astype(jnp.int32)
    padded = ((cnt + EXP_TM - 1) // EXP_TM) * EXP_TM
    ends = jnp.cumsum(padded)
    offs = ends - padded
    lanes = jnp.arange(N_EXPERTS, dtype=jnp.int32)[None, :]
    dest1 = jnp.sum(jnp.where(e1[:, None] == lanes, offs[None, :], 0), axis=1) + r1
    dest2 = jnp.sum(jnp.where(e2[:, None] == lanes, offs[None, :], 0), axis=1) + r2
    n_rows = 2 * n + N_EXPERTS * EXP_TM
    n_tiles = n_rows // EXP_TM
    tile_start = jnp.arange(n_tiles, dtype=jnp.int32) * EXP_TM
    tile_valid = (tile_start < ends[-1]).astype(jnp.int32)
    te = jnp.sum((ends[None, :] <= tile_start[:, None]).astype(jnp.int32), axis=1)
    te_last = jnp.sum((ends[None, :] <= ends[-1] - EXP_TM).astype(jnp.int32))
    tile_expert = jnp.minimum(jnp.where(tile_valid == 1, te, te_last), N_EXPERTS - 1)
    return dest1, dest2, tile_expert, tile_valid, n_rows


def _sc_mesh():
    return plsc.VectorSubcoreMesh(core_axis_name="core", subcore_axis_name="subcore")


def _sc_scatter_rows(rows, idx_a, idx_b, n_out):
    n_in, w = rows.shape

    @functools.partial(pl.kernel, out_type=jax.ShapeDtypeStruct((n_out, w), rows.dtype), mesh=_sc_mesh(),
                       scratch_types=[], name="sc_scatter_rows")
    def scatter(x_hbm, ia_hbm, ib_hbm, o_hbm):
        def body(x_vmem, ia_vmem, ib_vmem):
            pltpu.sync_copy(x_vmem, o_hbm.at[ia_vmem.at[0]])
            pltpu.sync_copy(x_vmem, o_hbm.at[ib_vmem.at[0]])

        pltpu.emit_pipeline(
            body,
            grid=(n_in // SC_WINDOW,),
            in_specs=[pl.BlockSpec((SC_WINDOW, w), lambda i: (i, 0)),
                      pl.BlockSpec((1, SC_WINDOW), lambda i: (0, i)),
                      pl.BlockSpec((1, SC_WINDOW), lambda i: (0, i))],
            out_specs=[],
            core_axis_name=("core", "subcore"),
            dimension_semantics=(pltpu.PARALLEL,),
        )(x_hbm, ia_hbm, ib_hbm)

    return scatter(rows, idx_a, idx_b)


def _sc_gather_rows(table, idx):
    n_idx = idx.shape[1]
    w = table.shape[1]

    @functools.partial(pl.kernel, out_type=jax.ShapeDtypeStruct((n_idx, w), table.dtype), mesh=_sc_mesh(),
                       scratch_types=[], name="sc_gather_rows")
    def gather(t_hbm, i_hbm, o_hbm):
        def body(i_vmem, o_vmem):
            pltpu.sync_copy(t_hbm.at[i_vmem.at[0]], o_vmem)

        pltpu.emit_pipeline(
            body,
            grid=(n_idx // SC_WINDOW,),
            in_specs=[pl.BlockSpec((1, SC_WINDOW), lambda i: (0, i))],
            out_specs=[pl.BlockSpec((SC_WINDOW, w), lambda i: (i, 0))],
            core_axis_name=("core", "subcore"),
            dimension_semantics=(pltpu.PARALLEL,),
        )(i_hbm, o_hbm)

    return gather(table, idx)


def _forward(x, g_mix, w_in, w_attn_branch, w_ret_branch, w_out, g_ffn, w_group_router, b_group_router,
             w_expert_router, b_expert_router, w_gate, w_up, w_down, g_final, interpret=False):
    batch, seq, d = x.shape
    n = batch * seq
    x2 = x.reshape(n, d)
    proj = _proj(x2, g_mix[0][None, :], _permute_w_in(w_in[0]), interpret)
    o_attn = _attention(proj, batch, seq, interpret)
    o_ret = _retention(proj, batch, seq, interpret)
    pad = LANES - N_EXPERT_GROUPS - N_EXPERTS
    w_router = jnp.concatenate([w_group_router[0], w_expert_router[0], jnp.zeros((d, pad), F32)], axis=-1)
    w_router_hi = w_router.astype(BF16)
    w_router_lo = (w_router - w_router_hi.astype(F32)).astype(BF16)
    w_router2 = jnp.concatenate([w_router_hi, w_router_lo], axis=-1)
    b_router = jnp.concatenate([b_group_router[0], b_expert_router[0], jnp.zeros((pad,), F32)])[None, :]
    x1, h2p, route, route_t, counts = _mix(o_attn, o_ret, proj, x2, w_attn_branch[0].astype(BF16),
                                           w_ret_branch[0].astype(BF16), w_out[0].astype(BF16),
                                           g_ffn[0][None, :], w_router2, b_router, interpret)
    dest1, dest2, tile_expert, tile_valid, n_rows = _route_plan(route_t, counts, n)
    idx_a = jnp.concatenate([dest1, dest1 + n_rows])[None, :]
    idx_b = jnp.concatenate([dest2, dest2 + n_rows])[None, :]
    xs = _sc_scatter_rows(h2p.reshape(2 * n, SC_ROW_WORDS), idx_a, idx_b, 2 * n_rows)
    ys = _experts(xs.reshape(2, n_rows, SC_ROW_WORDS), tile_expert, tile_valid, w_gate[0], w_up[0], w_down[0],
                  interpret)
    idx_g = jnp.concatenate([dest1, dest2, dest1 + n_rows, dest2 + n_rows])[None, :]
    yab = _sc_gather_rows(ys.reshape(2 * n_rows, SC_ROW_WORDS), idx_g)
    out = _final(x1, yab.reshape(4, n, SC_ROW_WORDS), route, g_final[None, :], interpret)
    return out.reshape(batch, seq, d)


def kernel(x, g_mix, w_in, w_attn_branch, w_ret_branch, w_out, g_ffn, w_group_router, b_group_router,
           w_expert_router, b_expert_router, w_gate, w_up, w_down, g_final):
    return _forward(x, g_mix, w_in, w_attn_branch, w_ret_branch, w_out, g_ffn, w_group_router,
                    b_group_router, w_expert_router, b_expert_router, w_gate, w_up, w_down, g_final)
```

```python
import functools

import numpy as np
import jax
import jax.numpy as jnp
from jax import lax
from jax.experimental import pallas as pl
from jax.experimental.pallas import tpu as pltpu
from jax.experimental.pallas import tpu_sc as plsc

F32 = jnp.float32
BF16 = jnp.bfloat16

D_MODEL = 1024
ATTN_GROUPS = ((128, 1), (512, 4), (2048, 16))
N_GROUPS = len(ATTN_GROUPS)
ATTN_HEADS = 8
HEAD_DIM = 64
GROUP_W = ATTN_HEADS * HEAD_DIM
QKV_W = N_GROUPS * GROUP_W
RET_HEADS = 4
RET_DK = 128
RET_DV = 256
RET_CHUNK = 128
RET_TS = 1024
N_EXPERT_GROUPS = 4
EXPERTS_PER_GROUP = 8
N_EXPERTS = N_EXPERT_GROUPS * EXPERTS_PER_GROUP
EXPERT_FF = 512
EPS = 1e-6

LANES = 128
BLK = 128
SPAN = 2048
NEG = -1e30
ATTN_UNROLL = 8

COL_GATE_A = 0
COL_GATE_R = 1024
COL_VR = 2048
COL_GR = COL_VR + RET_HEADS * RET_DV
COL_QA = COL_GR + RET_HEADS * RET_DV
COL_KA = COL_QA + QKV_W
COL_VA = COL_KA + QKV_W
COL_QR = COL_VA + QKV_W
COL_KR = COL_QR + RET_HEADS * RET_DK
IN_WIDTH = COL_KR + RET_HEADS * RET_DK

PROJ_TM = 512
PROJ_TN = IN_WIDTH // 2
MXU_N = 256
MIX_TM = 512
MIX_CHUNK = 256
ROUTE_ROWS = 8
EXP_TM = 512
SC_WINDOW = 128
SC_ROW_WORDS = 256
FIN_TM = 512
VMEM_LIMIT = 56 * 1024 * 1024


def _cparams(sem):
    return pltpu.CompilerParams(dimension_semantics=sem, vmem_limit_bytes=VMEM_LIMIT)


def _sigmoid(x):
    return 0.5 * jnp.tanh(0.5 * x) + 0.5


def _proj_kernel(x_ref, g_ref, w_ref, o_ref):
    x = x_ref[...]
    ms = jnp.mean(x * x, axis=-1, keepdims=True)
    h = (x * lax.rsqrt(ms + EPS) * g_ref[...]).astype(BF16)
    for c in range(PROJ_TN // MXU_N):
        sl = slice(c * MXU_N, (c + 1) * MXU_N)
        o_ref[:, sl] = jnp.dot(h, w_ref[:, sl], preferred_element_type=F32).astype(o_ref.dtype)


def _proj(x2, g_mix, w_in_bf16, interpret):
    n = x2.shape[0]
    return pl.pallas_call(
        _proj_kernel,
        grid=(IN_WIDTH // PROJ_TN, n // PROJ_TM),
        in_specs=[
            pl.BlockSpec((PROJ_TM, D_MODEL), lambda j, i: (i, 0)),
            pl.BlockSpec((1, D_MODEL), lambda j, i: (0, 0)),
            pl.BlockSpec((D_MODEL, PROJ_TN), lambda j, i: (0, j)),
        ],
        out_specs=pl.BlockSpec((PROJ_TM, PROJ_TN), lambda j, i: (i, j)),
        out_shape=jax.ShapeDtypeStruct((n, IN_WIDTH), BF16),
        compiler_params=_cparams(("arbitrary", "arbitrary")),
        interpret=interpret,
        name="proj",
    )(x2, g_mix, w_in_bf16)


def _attn_unit(q2, kk, vv, bias_a, bias_b):
    lane = lax.broadcasted_iota(jnp.int32, (BLK, LANES), 1)
    left = lane < HEAD_DIM
    zero = jnp.zeros_like(q2)
    nt = (((1,), (1,)), ((), ()))
    q_st = jnp.concatenate([jnp.where(left, q2, zero), jnp.where(left, zero, q2)], axis=0)
    s = lax.dot_general(q_st, kk, nt, preferred_element_type=F32) + jnp.concatenate([bias_a, bias_b], axis=0)
    m = jnp.max(s, axis=-1, keepdims=True)
    p = jnp.exp(s - m)
    den = jnp.sum(p, axis=-1, keepdims=True)
    o = jnp.dot(p.astype(BF16), vv, preferred_element_type=F32)
    return (jnp.where(left, o[:BLK], o[BLK:]), jnp.where(left, m[:BLK], m[BLK:]),
            jnp.where(left, den[:BLK], den[BLK:]))


def _attn_kernel(q1_ref, q2_ref, q3_ref, k1_ref, k2_ref, k3_ref, v1_ref, v2_ref, v3_ref,
                 bias_ref, o_ref, qf_ref, kvf_ref, acc_ref, *, seq):
    s_id = pl.program_id(2)
    conv_rows = 256

    @pl.when(s_id == 0)
    def _():
        for slot, ref in enumerate((k2_ref, v2_ref, k3_ref, v3_ref)):
            def body(i, c, slot=slot, ref=ref):
                r0 = pl.multiple_of(i * conv_rows, conv_rows)
                kvf_ref[slot, pl.ds(r0, conv_rows), :] = ref[pl.ds(r0, conv_rows), :].astype(F32)
                return c
            lax.fori_loop(0, seq // conv_rows, body, 0)

    for slot, ref in enumerate((q2_ref, q3_ref)):
        def body(i, c, slot=slot, ref=ref):
            r0 = pl.multiple_of(i * conv_rows, conv_rows)
            qf_ref[slot, pl.ds(r0, conv_rows), :] = ref[pl.ds(r0, conv_rows), :].astype(F32) * 0.125
            return c
        lax.fori_loop(0, SPAN // conv_rows, body, 0)

    def dilated_unit(gi, d, m, r, first):
        slot = gi - 1
        loc = BLK * m * d + r
        cur = s_id * SPAN + loc
        prev = jnp.maximum(cur - BLK * d, r)
        q2 = qf_ref[slot, pl.ds(loc, BLK, stride=d), :].astype(BF16)
        kk = jnp.concatenate([kvf_ref[2 * slot, pl.ds(prev, BLK, stride=d), :],
                              kvf_ref[2 * slot, pl.ds(cur, BLK, stride=d), :]], axis=0).astype(BF16)
        vv = jnp.concatenate([kvf_ref[2 * slot + 1, pl.ds(prev, BLK, stride=d), :],
                              kvf_ref[2 * slot + 1, pl.ds(cur, BLK, stride=d), :]], axis=0).astype(BF16)
        num, mx, den = _attn_unit(q2, kk, vv, bias_ref[gi, first, 0], bias_ref[gi, first, 1])
        acc_ref[3 * slot + 0, pl.ds(loc, BLK, stride=d), :] = num
        acc_ref[3 * slot + 1, pl.ds(loc, BLK, stride=d), :] = mx
        acc_ref[3 * slot + 2, pl.ds(loc, BLK, stride=d), :] = den

    d3 = ATTN_GROUPS[2][1]
    first_span = jnp.where(s_id == 0, 1, 0)

    def body3(i, c):
        for u in range(ATTN_UNROLL):
            dilated_unit(2, d3, 0, i * ATTN_UNROLL + u, first_span)
        return c
    lax.fori_loop(0, d3 // ATTN_UNROLL, body3, 0)

    d2 = ATTN_GROUPS[1][1]

    blocks2 = ATTN_UNROLL // d2

    def body2(i, c):
        for mm in range(blocks2):
            m = i * blocks2 + mm
            first = jnp.where(jnp.logical_and(s_id == 0, m == 0), 1, 0)
            for r in range(d2):
                dilated_unit(1, d2, m, r, first)
        return c
    lax.fori_loop(0, SPAN // (BLK * d2 * blocks2), body2, 0)

    def dense_unit(m):
        loc = pl.multiple_of(m * BLK, BLK)
        cur = pl.multiple_of(s_id * SPAN + loc, BLK)
        prev = pl.multiple_of(jnp.maximum(cur - BLK, 0), BLK)
        first = jnp.where(cur == 0, 1, 0)
        q2 = q1_ref[pl.ds(loc, BLK), :] * 0.125
        kk = jnp.concatenate([k1_ref[pl.ds(prev, BLK), :], k1_ref[pl.ds(cur, BLK), :]], axis=0)
        vv = jnp.concatenate([v1_ref[pl.ds(prev, BLK), :], v1_ref[pl.ds(cur, BLK), :]], axis=0)
        n1, m1, d1 = _attn_unit(q2, kk, vv, bias_ref[0, first, 0], bias_ref[0, first, 1])
        n2, m2, dd2 = (acc_ref[j, pl.ds(loc, BLK), :] for j in (0, 1, 2))
        n3, m3, dd3 = (acc_ref[j, pl.ds(loc, BLK), :] for j in (3, 4, 5))
        mx = jnp.maximum(jnp.maximum(m1, m2), m3)
        w1, w2, w3 = jnp.exp(m1 - mx), jnp.exp(m2 - mx), jnp.exp(m3 - mx)
        num = w1 * n1 + w2 * n2 + w3 * n3
        den = w1 * d1 + w2 * dd2 + w3 * dd3
        o_ref[pl.ds(loc, BLK), :] = (num / den).astype(o_ref.dtype)

    def body1(i, c):
        for u in range(ATTN_UNROLL):
            dense_unit(i * ATTN_UNROLL + u)
        return c
    lax.fori_loop(0, SPAN // (BLK * ATTN_UNROLL), body1, 0)


def _attn_bias():
    slopes = np.exp2(-8.0 * np.arange(1, ATTN_HEADS + 1, dtype=np.float64) / ATTN_HEADS)
    qi = np.arange(BLK)[:, None]
    kj = np.arange(2 * BLK)[None, :]
    rel = qi + BLK - kj
    out = np.zeros((N_GROUPS, 2, ATTN_HEADS, BLK, 2 * BLK), np.float32)
    for gi, (window, d) in enumerate(ATTN_GROUPS):
        n_back = window // d
        assert n_back == BLK
        valid = (rel >= 0) & (rel <= n_back)
        bias = -slopes[:, None, None] * (rel * d)[None].astype(np.float64)
        out[gi, 0] = np.where(valid[None], bias, NEG)
        out[gi, 1] = np.where((valid & (kj >= BLK))[None], bias, NEG)
    return jnp.asarray(out)


def _attention(proj, batch, seq, interpret):
    n = batch * seq
    spans = seq // SPAN
    n_hp = GROUP_W // LANES
    qcol = lambda g: (COL_QA + g * GROUP_W) // LANES
    kcol = lambda g: (COL_KA + g * GROUP_W) // LANES
    vcol = lambda g: (COL_VA + g * GROUP_W) // LANES
    q_specs = [pl.BlockSpec((SPAN, LANES), functools.partial(lambda b, hp, s, c: (b * spans + s, c + hp), c=qcol(g)))
               for g in range(N_GROUPS)]
    k_specs = [pl.BlockSpec((seq, LANES), functools.partial(lambda b, hp, s, c: (b, c + hp), c=kcol(g)))
               for g in range(N_GROUPS)]
    v_specs = [pl.BlockSpec((seq, LANES), functools.partial(lambda b, hp, s, c: (b, c + hp), c=vcol(g)))
               for g in range(N_GROUPS)]
    bias_spec = pl.BlockSpec((N_GROUPS, 2, 2, BLK, 2 * BLK), lambda b, hp, s: (0, 0, hp, 0, 0))
    return pl.pallas_call(
        functools.partial(_attn_kernel, seq=seq),
        grid=(batch, n_hp, spans),
        in_specs=q_specs + k_specs + v_specs + [bias_spec],
        out_specs=pl.BlockSpec((SPAN, LANES), lambda b, hp, s: (b * spans + s, hp)),
        out_shape=jax.ShapeDtypeStruct((n, GROUP_W), BF16),
        scratch_shapes=[
            pltpu.VMEM((2, SPAN, LANES), F32),
            pltpu.VMEM((4, seq, LANES), F32),
            pltpu.VMEM((6, SPAN, LANES), F32),
        ],
        compiler_params=_cparams(("arbitrary", "arbitrary", "arbitrary")),
        interpret=interpret,
        name="attn",
    )(*([proj] * 9), _attn_bias())


def _ret_kernel(q_ref, k_ref, v_ref, gr_ref, dec_ref, xi_ref, zeta_ref, gch_ref, o_ref, st_ref):
    @pl.when(pl.program_id(1) == 0)
    def _():
        st_ref[...] = jnp.zeros_like(st_ref)

    nt = (((1,), (1,)), ((), ()))
    scale = RET_DK ** -0.5

    def body(c, carry):
        r0 = pl.multiple_of(c * RET_CHUNK, RET_CHUNK)
        rows = pl.ds(r0, RET_CHUNK)
        for h in range(RET_HEADS):
            kcols = slice(h * RET_DK, (h + 1) * RET_DK)
            vcols = slice(h * RET_DV, (h + 1) * RET_DV)
            qi = q_ref[rows, kcols]
            kf = k_ref[rows, kcols].astype(F32) * scale
            ki = kf.astype(BF16)
            kz_t = jnp.transpose(kf * zeta_ref[h]).astype(BF16)
            vi = v_ref[rows, vcols]
            att = lax.dot_general(qi, ki, nt, preferred_element_type=F32) * dec_ref[h]
            inner = jnp.dot(att.astype(BF16), vi, preferred_element_type=F32)
            st = st_ref[h]
            cross = jnp.dot(qi, st.astype(BF16), preferred_element_type=F32) * xi_ref[h]
            st_ref[h] = gch_ref[h] * st + jnp.dot(kz_t, vi, preferred_element_type=F32)
            y = inner + cross
            mu = jnp.mean(y, axis=-1, keepdims=True)
            yc = y - mu
            var = jnp.mean(yc * yc, axis=-1, keepdims=True)
            yn = yc * lax.rsqrt(var + EPS)
            g = gr_ref[rows, vcols].astype(F32)
            o_ref[rows, vcols] = (g * _sigmoid(g) * yn).astype(o_ref.dtype)
        return carry
    lax.fori_loop(0, RET_TS // RET_CHUNK, body, 0)


def _ret_tables():
    c = RET_CHUNK
    log_g = np.log1p(-np.exp2(-5.0 - np.arange(RET_HEADS, dtype=np.float64)))
    pos = np.arange(c, dtype=np.float64)
    diff = pos[:, None] - pos[None, :]
    dec = np.where(diff >= 0, np.exp(log_g[:, None, None] * np.maximum(diff, 0.0)), 0.0)
    xi = np.exp(log_g[:, None] * (pos + 1.0))[..., None] * np.ones((1, 1, RET_DV))
    zeta = np.exp(log_g[:, None] * (c - 1.0 - pos))[..., None] * np.ones((1, 1, RET_DK))
    gch = np.exp(log_g * c)[:, None, None] * np.ones((1, 1, RET_DV))
    return tuple(jnp.asarray(t, F32) for t in (dec, xi, zeta, gch))


def _retention(proj, batch, seq, interpret):
    n = batch * seq
    dec, xi, zeta, gch = _ret_tables()
    qk_w = RET_HEADS * RET_DK
    v_w = RET_HEADS * RET_DV
    nts = seq // RET_TS
    const3 = lambda b, t: (0, 0, 0)
    return pl.pallas_call(
        _ret_kernel,
        grid=(batch, nts),
        in_specs=[
            pl.BlockSpec((RET_TS, qk_w), lambda b, t: (b * nts + t, COL_QR // qk_w)),
            pl.BlockSpec((RET_TS, qk_w), lambda b, t: (b * nts + t, COL_KR // qk_w)),
            pl.BlockSpec((RET_TS, v_w), lambda b, t: (b * nts + t, COL_VR // v_w)),
            pl.BlockSpec((RET_TS, v_w), lambda b, t: (b * nts + t, COL_GR // v_w)),
            pl.BlockSpec((RET_HEADS, RET_CHUNK, RET_CHUNK), const3),
            pl.BlockSpec((RET_HEADS, RET_CHUNK, RET_DV), const3),
            pl.BlockSpec((RET_HEADS, RET_CHUNK, RET_DK), const3),
            pl.BlockSpec((RET_HEADS, 1, RET_DV), const3),
        ],
        out_specs=pl.BlockSpec((RET_TS, v_w), lambda b, t: (b * nts + t, 0)),
        out_shape=jax.ShapeDtypeStruct((n, v_w), BF16),
        scratch_shapes=[pltpu.VMEM((RET_HEADS, RET_DK, RET_DV), F32)],
        compiler_params=_cparams(("arbitrary", "arbitrary")),
        interpret=interpret,
        name="retention",
    )(proj, proj, proj, proj, dec, xi, zeta, gch)


ROUTER_OFF = N_EXPERT_GROUPS


def _pack_bf16_pair(a, b):
    hi = lax.bitcast_convert_type(a.astype(BF16).astype(F32), jnp.uint32)
    lo = lax.bitcast_convert_type(b.astype(BF16).astype(F32), jnp.uint32)
    return lax.bitcast_convert_type(hi | (lo >> 16), jnp.int32)


def _unpack_bf16_pair(w):
    u = lax.bitcast_convert_type(w, jnp.uint32)
    a = lax.bitcast_convert_type(u & jnp.uint32(0xFFFF0000), F32).astype(BF16)
    b = lax.bitcast_convert_type(u << 16, F32).astype(BF16)
    return a, b


def _pack_rows(y):
    q = D_MODEL // 4
    return (_pack_bf16_pair(y[:, 0:q], y[:, 2 * q:3 * q]), _pack_bf16_pair(y[:, q:2 * q], y[:, 3 * q:4 * q]))


def _unpack_rows(slab0, slab1):
    q0, q2 = _unpack_bf16_pair(slab0)
    q1, q3 = _unpack_bf16_pair(slab1)
    return jnp.concatenate([q0, q1, q2, q3], axis=1)


def _mix_kernel(oa_ref, or_ref, ga_ref, gr_ref, x_ref, pa_ref, pr_ref, wo_ref, gf_ref, wr_ref, br_ref,
                x1_ref, h2_ref, route_ref, route_t_ref, cnt_ref, carry_ref):
    @pl.when(pl.program_id(0) == 0)
    def _():
        carry_ref[...] = jnp.zeros_like(carry_ref)

    for c in range(MIX_TM // MIX_CHUNK):
        _mix_rows(pl.ds(c * MIX_CHUNK, MIX_CHUNK), oa_ref, or_ref, ga_ref, gr_ref, x_ref, pa_ref, pr_ref, wo_ref,
                  gf_ref, wr_ref, br_ref, x1_ref, h2_ref, route_ref, route_t_ref, cnt_ref, carry_ref)


def _mix_rows(rows, oa_ref, or_ref, ga_ref, gr_ref, x_ref, pa_ref, pr_ref, wo_ref, gf_ref, wr_ref, br_ref,
              x1_ref, h2_ref, route_ref, route_t_ref, cnt_ref, carry_ref):
    a = jnp.dot(oa_ref[rows, :], pa_ref[...], preferred_element_type=F32)
    r = jnp.dot(or_ref[rows, :], pr_ref[...], preferred_element_type=F32)
    merged = (_sigmoid(ga_ref[rows, :].astype(F32)) * a + _sigmoid(gr_ref[rows, :].astype(F32)) * r)
    x1 = x_ref[rows, :] + jnp.dot(merged.astype(BF16), wo_ref[...], preferred_element_type=F32)
    x1_ref[rows, :] = x1
    ms = jnp.mean(x1 * x1, axis=-1, keepdims=True)
    h2 = x1 * lax.rsqrt(ms + EPS) * gf_ref[...]
    h2_ref[0, rows, :], h2_ref[1, rows, :] = _pack_rows(h2)

    h_hi = h2.astype(BF16)
    h_lo = (h2 - h_hi.astype(F32)).astype(BF16)
    both = jnp.dot(h_hi, wr_ref[...], preferred_element_type=F32)
    logits = (both[:, :LANES] + both[:, LANES:]
              + jnp.dot(h_lo, wr_ref[:, :LANES], preferred_element_type=F32) + br_ref[...])
    tm = logits.shape[0]
    lane = lax.broadcasted_iota(jnp.int32, (tm, LANES), 1).astype(F32)
    big = jnp.float32(4 * LANES)
    ninf = -jnp.inf
    is_g = lane < N_EXPERT_GROUPS
    gl = jnp.where(is_g, logits, ninf)
    gmax = jnp.max(gl, axis=-1, keepdims=True)
    gsum = jnp.sum(jnp.where(is_g, jnp.exp(gl - gmax), 0.0), axis=-1, keepdims=True)
    g_val = 1.0 / gsum
    g_idx = jnp.min(jnp.where(jnp.logical_and(is_g, gl == gmax), lane, big), axis=-1, keepdims=True)
    lo = ROUTER_OFF + EXPERTS_PER_GROUP * g_idx
    in_grp = jnp.logical_and(lane >= lo, lane < lo + EXPERTS_PER_GROUP)
    el = jnp.where(in_grp, logits, ninf)
    v1 = jnp.max(el, axis=-1, keepdims=True)
    i1 = jnp.min(jnp.where(jnp.logical_and(in_grp, el == v1), lane, big), axis=-1, keepdims=True)
    rest = jnp.logical_and(in_grp, lane != i1)
    el2 = jnp.where(rest, logits, ninf)
    v2 = jnp.max(el2, axis=-1, keepdims=True)
    i2 = jnp.min(jnp.where(jnp.logical_and(rest, el2 == v2), lane, big), axis=-1, keepdims=True)
    t = jnp.exp(v2 - v1)
    w1 = g_val / (1.0 + t)
    w2 = g_val * t / (1.0 + t)

    sel = jnp.logical_or(lane == i1, lane == i2)
    sel_bf = jnp.where(sel, 1.0, 0.0).astype(BF16)
    row = lax.broadcasted_iota(jnp.int32, (tm, tm), 0)
    col = lax.broadcasted_iota(jnp.int32, (tm, tm), 1)
    tri = jnp.where(col < row, 1.0, 0.0).astype(BF16)
    before = jnp.dot(tri, sel_bf, preferred_element_type=F32) + carry_ref[...]
    r1 = jnp.sum(jnp.where(lane == i1, before, 0.0), axis=-1, keepdims=True)
    r2 = jnp.sum(jnp.where(lane == i2, before, 0.0), axis=-1, keepdims=True)
    carry = carry_ref[...] + jnp.sum(jnp.where(sel, 1.0, 0.0), axis=0, keepdims=True)
    carry_ref[...] = carry
    cnt_ref[...] = carry

    vals = (i1 - ROUTER_OFF, i2 - ROUTER_OFF, w1, w2, r1, r2)
    route = jnp.zeros((tm, LANES), F32)
    for j, v in enumerate(vals):
        route = jnp.where(lane == j, v, route)
    route_ref[rows, :] = route
    route_t_ref[:, rows] = jnp.transpose(route)[:ROUTE_ROWS, :]


def _mix(o_attn, o_ret, proj, x2, pa, pr, wo, g_ffn, w_router, b_router, interpret):
    n = x2.shape[0]
    tm = MIX_TM
    const = lambda i: (0, 0)
    return pl.pallas_call(
        _mix_kernel,
        grid=(n // tm,),
        in_specs=[
            pl.BlockSpec((tm, GROUP_W), lambda i: (i, 0)),
            pl.BlockSpec((tm, D_MODEL), lambda i: (i, 0)),
            pl.BlockSpec((tm, D_MODEL), lambda i: (i, COL_GATE_A // D_MODEL)),
            pl.BlockSpec((tm, D_MODEL), lambda i: (i, COL_GATE_R // D_MODEL)),
            pl.BlockSpec((tm, D_MODEL), lambda i: (i, 0)),
            pl.BlockSpec((GROUP_W, D_MODEL), const),
            pl.BlockSpec((D_MODEL, D_MODEL), const),
            pl.BlockSpec((D_MODEL, D_MODEL), const),
            pl.BlockSpec((1, D_MODEL), const),
            pl.BlockSpec((D_MODEL, 2 * LANES), const),
            pl.BlockSpec((1, LANES), const),
        ],
        out_specs=[
            pl.BlockSpec((tm, D_MODEL), lambda i: (i, 0)),
            pl.BlockSpec((2, tm, SC_ROW_WORDS), lambda i: (0, i, 0)),
            pl.BlockSpec((tm, LANES), lambda i: (i, 0)),
            pl.BlockSpec((ROUTE_ROWS, tm), lambda i: (0, i)),
            pl.BlockSpec((1, LANES), const),
        ],
        out_shape=[
            jax.ShapeDtypeStruct((n, D_MODEL), F32),
            jax.ShapeDtypeStruct((2, n, SC_ROW_WORDS), jnp.int32),
            jax.ShapeDtypeStruct((n, LANES), F32),
            jax.ShapeDtypeStruct((ROUTE_ROWS, n), F32),
            jax.ShapeDtypeStruct((1, LANES), F32),
        ],
        scratch_shapes=[pltpu.VMEM((1, LANES), F32)],
        compiler_params=_cparams(("arbitrary",)),
        interpret=interpret,
        name="mix_router",
    )(o_attn, o_ret, proj, proj, x2, pa, pr, wo, g_ffn, w_router, b_router)


def _expert_kernel(te_ref, tv_ref, xs_ref, wg_ref, wu_ref, wd_ref, o_ref, wg_s, wu_s, wd_s):
    i = pl.program_id(0)
    changed = jnp.logical_or(i == 0, te_ref[i] != te_ref[jnp.maximum(i - 1, 0)])

    @pl.when(changed)
    def _():
        wg_s[...] = wg_ref[0].astype(BF16)
        wu_s[...] = wu_ref[0].astype(BF16)
        wd_s[...] = wd_ref[0].astype(BF16)

    @pl.when(tv_ref[i] != 0)
    def _():
        xs = _unpack_rows(xs_ref[0], xs_ref[1])
        a = jnp.dot(xs, wg_s[...], preferred_element_type=F32)
        u = jnp.dot(xs, wu_s[...], preferred_element_type=F32)
        hid = (a * _sigmoid(a) * u).astype(BF16)
        y = jnp.dot(hid, wd_s[...], preferred_element_type=F32)
        o_ref[0], o_ref[1] = _pack_rows(y)

    @pl.when(tv_ref[i] == 0)
    def _():
        o_ref[...] = jnp.zeros_like(o_ref)


def _experts(xs, tile_expert, tile_valid, w_gate, w_up, w_down, interpret):
    p = xs.shape[1]
    n_tiles = p // EXP_TM
    grid_spec = pltpu.PrefetchScalarGridSpec(
        num_scalar_prefetch=2,
        grid=(n_tiles,),
        in_specs=[
            pl.BlockSpec((2, EXP_TM, SC_ROW_WORDS), lambda i, te, tv: (0, i, 0)),
            pl.BlockSpec((1, D_MODEL, EXPERT_FF), lambda i, te, tv: (te[i], 0, 0)),
            pl.BlockSpec((1, D_MODEL, EXPERT_FF), lambda i, te, tv: (te[i], 0, 0)),
            pl.BlockSpec((1, EXPERT_FF, D_MODEL), lambda i, te, tv: (te[i], 0, 0)),
        ],
        out_specs=pl.BlockSpec((2, EXP_TM, SC_ROW_WORDS), lambda i, te, tv: (0, i, 0)),
        scratch_shapes=[
            pltpu.VMEM((D_MODEL, EXPERT_FF), BF16),
            pltpu.VMEM((D_MODEL, EXPERT_FF), BF16),
            pltpu.VMEM((EXPERT_FF, D_MODEL), BF16),
        ],
    )
    return pl.pallas_call(
        _expert_kernel,
        grid_spec=grid_spec,
        out_shape=jax.ShapeDtypeStruct((2, p, SC_ROW_WORDS), jnp.int32),
        compiler_params=_cparams(("arbitrary",)),
        interpret=interpret,
        name="experts",
    )(tile_expert, tile_valid, xs, w_gate, w_up, w_down)


def _final_kernel(x1_ref, yab_ref, route_ref, g_ref, o_ref):
    route = route_ref[...]
    w1 = route[:, 2:3]
    w2 = route[:, 3:4]
    ya = _unpack_rows(yab_ref[0], yab_ref[2]).astype(F32)
    yb = _unpack_rows(yab_ref[1], yab_ref[3]).astype(F32)
    x2 = x1_ref[...] + w1 * ya + w2 * yb
    ms = jnp.mean(x2 * x2, axis=-1, keepdims=True)
    o_ref[...] = x2 * lax.rsqrt(ms + EPS) * g_ref[...]


def _final(x1, yab, route, g_final, interpret):
    n = x1.shape[0]
    tm = FIN_TM
    row = lambda i: (i, 0)
    return pl.pallas_call(
        _final_kernel,
        grid=(n // tm,),
        in_specs=[
            pl.BlockSpec((tm, D_MODEL), row),
            pl.BlockSpec((4, tm, SC_ROW_WORDS), lambda i: (0, i, 0)),
            pl.BlockSpec((tm, LANES), row),
            pl.BlockSpec((1, D_MODEL), lambda i: (0, 0)),
        ],
        out_specs=pl.BlockSpec((tm, D_MODEL), row),
        out_shape=jax.ShapeDtypeStruct((n, D_MODEL), F32),
        compiler_params=_cparams(("arbitrary",)),
        interpret=interpret,
        name="combine_final",
    )(x1, yab, route, g_final)


def _permute_w_in(w_in):
    splits = np.cumsum([QKV_W, QKV_W, QKV_W, 512, 512, 1024, 1024, D_MODEL, D_MODEL])[:-1].tolist()
    qa, ka, va, qr, kr, vr, gr, gate_a, gate_r = jnp.split(w_in, splits, axis=-1)
    return jnp.concatenate([gate_a, gate_r, vr, gr, qa, ka, va, qr, kr], axis=-1).astype(BF16)


def _route_plan(route_t, counts, n):
    e1 = route_t[0].astype(jnp.int32)
    e2 = route_t[1].astype(jnp.int32)
    r1 = route_t[4].astype(jnp.int32)
    r2 = route_t[5].astype(jnp.int32)
    cnt = counts[0, ROUTER_OFF:ROUTER_OFF + N_EXPERTS].astype(jnp.int32)
    padded = ((cnt + EXP_TM - 1) // EXP_TM) * EXP_TM
    ends = jnp.cumsum(padded)
    offs = ends - padded
    lanes = jnp.arange(N_EXPERTS, dtype=jnp.int32)[None, :]
    dest1 = jnp.sum(jnp.where(e1[:, None] == lanes, offs[None, :], 0), axis=1) + r1
    dest2 = jnp.sum(jnp.where(e2[:, None] == lanes, offs[None, :], 0), axis=1) + r2
    n_rows = 2 * n + N_EXPERTS * EXP_TM
    n_tiles = n_rows // EXP_TM
    tile_start = jnp.arange(n_tiles, dtype=jnp.int32) * EXP_TM
    tile_valid = (tile_start < ends[-1]).astype(jnp.int32)
    te = jnp.sum((ends[None, :] <= tile_start[:, None]).astype(jnp.int32), axis=1)
    te_last = jnp.sum((ends[None, :] <= ends[-1] - EXP_TM).astype(jnp.int32))
    tile_expert = jnp.minimum(jnp.where(tile_valid == 1, te, te_last), N_EXPERTS - 1)
    return dest1, dest2, tile_expert, tile_valid, n_rows


def _sc_mesh():
    return plsc.VectorSubcoreMesh(core_axis_name="core", subcore_axis_name="subcore")


def _sc_scatter_rows(rows, idx_a, idx_b, n_out):
    n_in, w = rows.shape

    @functools.partial(pl.kernel, out_type=jax.ShapeDtypeStruct((n_out, w), rows.dtype), mesh=_sc_mesh(),
                       scratch_types=[], name="sc_scatter_rows")
    def scatter(x_hbm, ia_hbm, ib_hbm, o_hbm):
        def body(x_vmem, ia_vmem, ib_vmem):
            pltpu.sync_copy(x_vmem, o_hbm.at[ia_vmem.at[0]])
            pltpu.sync_copy(x_vmem, o_hbm.at[ib_vmem.at[0]])

        pltpu.emit_pipeline(
            body,
            grid=(n_in // SC_WINDOW,),
            in_specs=[pl.BlockSpec((SC_WINDOW, w), lambda i: (i, 0)),
                      pl.BlockSpec((1, SC_WINDOW), lambda i: (0, i)),
                      pl.BlockSpec((1, SC_WINDOW), lambda i: (0, i))],
            out_specs=[],
            core_axis_name=("core", "subcore"),
            dimension_semantics=(pltpu.PARALLEL,),
        )(x_hbm, ia_hbm, ib_hbm)

    return scatter(rows, idx_a, idx_b)


def _sc_gather_rows(table, idx):
    n_idx = idx.shape[1]
    w = table.shape[1]

    @functools.partial(pl.kernel, out_type=jax.ShapeDtypeStruct((n_idx, w), table.dtype), mesh=_sc_mesh(),
                       scratch_types=[], name="sc_gather_rows")
    def gather(t_hbm, i_hbm, o_hbm):
        def body(i_vmem, o_vmem):
            pltpu.sync_copy(t_hbm.at[i_vmem.at[0]], o_vmem)

        pltpu.emit_pipeline(
            body,
            grid=(n_idx // SC_WINDOW,),
            in_specs=[pl.BlockSpec((1, SC_WINDOW), lambda i: (0, i))],
            out_specs=[pl.BlockSpec((SC_WINDOW, w), lambda i: (i, 0))],
            core_axis_name=("core", "subcore"),
            dimension_semantics=(pltpu.PARALLEL,),
        )(i_hbm, o_hbm)

    return gather(table, idx)


def _forward(x, g_mix, w_in, w_attn_branch, w_ret_branch, w_out, g_ffn, w_group_router, b_group_router,
             w_expert_router, b_expert_router, w_gate, w_up, w_down, g_final, interpret=False):
    batch, seq, d = x.shape
    n = batch * seq
    x2 = x.reshape(n, d)
    proj = _proj(x2, g_mix[0][None, :], _permute_w_in(w_in[0]), interpret)
    o_attn = _attention(proj, batch, seq, interpret)
    o_ret = _retention(proj, batch, seq, interpret)
    pad = LANES - N_EXPERT_GROUPS - N_EXPERTS
    w_router = jnp.concatenate([w_group_router[0], w_expert_router[0], jnp.zeros((d, pad), F32)], axis=-1)
    w_router_hi = w_router.astype(BF16)
    w_router_lo = (w_router - w_router_hi.astype(F32)).astype(BF16)
    w_router2 = jnp.concatenate([w_router_hi, w_router_lo], axis=-1)
    b_router = jnp.concatenate([b_group_router[0], b_expert_router[0], jnp.zeros((pad,), F32)])[None, :]
    x1, h2p, route, route_t, counts = _mix(o_attn, o_ret, proj, x2, w_attn_branch[0].astype(BF16),
                                           w_ret_branch[0].astype(BF16), w_out[0].astype(BF16),
                                           g_ffn[0][None, :], w_router2, b_router, interpret)
    dest1, dest2, tile_expert, tile_valid, n_rows = _route_plan(route_t, counts, n)
    idx_a = jnp.concatenate([dest1, dest1 + n_rows])[None, :]
    idx_b = jnp.concatenate([dest2, dest2 + n_rows])[None, :]
    xs = _sc_scatter_rows(h2p.reshape(2 * n, SC_ROW_WORDS), idx_a, idx_b, 2 * n_rows)
    ys = _experts(xs.reshape(2, n_rows, SC_ROW_WORDS), tile_expert, tile_valid, w_gate[0], w_up[0], w_down[0],
                  interpret)
    idx_g = jnp.concatenate([dest1, dest2, dest1 + n_rows, dest2 + n_rows])[None, :]
    yab = _sc_gather_rows(ys.reshape(2 * n_rows, SC_ROW_WORDS), idx_g)
    out = _final(x1, yab.reshape(4, n, SC_ROW_WORDS), route, g_final[None, :], interpret)
    return out.reshape(batch, seq, d)


def kernel(x, g_mix, w_in, w_attn_branch, w_ret_branch, w_out, g_ffn, w_group_router, b_group_router,
           w_expert_router, b_expert_router, w_gate, w_up, w_down, g_final):
    return _forward(x, g_mix, w_in, w_attn_branch, w_ret_branch, w_out, g_ffn, w_group_router,
                    b_group_router, w_expert_router, b_expert_router, w_gate, w_up, w_down, g_final)
```

```python
import functools

import numpy as np
import jax
import jax.numpy as jnp
from jax import lax
from jax.experimental import pallas as pl
from jax.experimental.pallas import tpu as pltpu
from jax.experimental.pallas import tpu_sc as plsc

F32 = jnp.float32
BF16 = jnp.bfloat16

D_MODEL = 1024
ATTN_GROUPS = ((128, 1), (512, 4), (2048, 16))
N_GROUPS = len(ATTN_GROUPS)
ATTN_HEADS = 8
HEAD_DIM = 64
GROUP_W = ATTN_HEADS * HEAD_DIM
QKV_W = N_GROUPS * GROUP_W
RET_HEADS = 4
RET_DK = 128
RET_DV = 256
RET_CHUNK = 128
RET_TS = 1024
N_EXPERT_GROUPS = 4
EXPERTS_PER_GROUP = 8
N_EXPERTS = N_EXPERT_GROUPS * EXPERTS_PER_GROUP
EXPERT_FF = 512
EPS = 1e-6

LANES = 128
BLK = 128
SPAN = 2048
NEG = -1e30
ATTN_UNROLL = 8

COL_GATE_A = 0
COL_GATE_R = 1024
COL_VR = 2048
COL_GR = COL_VR + RET_HEADS * RET_DV
COL_QA = COL_GR + RET_HEADS * RET_DV
COL_KA = COL_QA + QKV_W
COL_VA = COL_KA + QKV_W
COL_QR = COL_VA + QKV_W
COL_KR = COL_QR + RET_HEADS * RET_DK
IN_WIDTH = COL_KR + RET_HEADS * RET_DK

PROJ_TM = 512
PROJ_TN = IN_WIDTH // 2
MXU_N = 256
MIX_TM = 512
MIX_CHUNK = 256
ROUTE_ROWS = 8
EXP_TM = 512
SC_WINDOW = 128
SC_ROW_WORDS = 256
FIN_TM = 512
VMEM_LIMIT = 56 * 1024 * 1024


def _cparams(sem):
    return pltpu.CompilerParams(dimension_semantics=sem, vmem_limit_bytes=VMEM_LIMIT)


def _sigmoid(x):
    return 0.5 * jnp.tanh(0.5 * x) + 0.5


def _proj_kernel(x_ref, g_ref, w_ref, wg_ref, wu_ref, wd_ref, o_ref, wg_o, wu_o, wd_o):
    x = x_ref[...]
    ms = jnp.mean(x * x, axis=-1, keepdims=True)
    h = (x * lax.rsqrt(ms + EPS) * g_ref[...]).astype(BF16)
    for c in range(PROJ_TN // MXU_N):
        sl = slice(c * MXU_N, (c + 1) * MXU_N)
        o_ref[:, sl] = jnp.dot(h, w_ref[:, sl], preferred_element_type=F32).astype(o_ref.dtype)
    wg_o[...] = wg_ref[...].astype(BF16)
    wu_o[...] = wu_ref[...].astype(BF16)
    wd_o[...] = wd_ref[...].astype(BF16)


def _proj(x2, g_mix, w_in_bf16, w_gate, w_up, w_down, interpret):
    n = x2.shape[0]
    n_i = n // PROJ_TM
    steps = (IN_WIDTH // PROJ_TN) * n_i
    flat = [w.reshape(-1, w.shape[-1]) for w in (w_gate, w_up, w_down)]
    w_specs = [pl.BlockSpec((w.shape[0] // steps, w.shape[1]), lambda j, i: (j * n_i + i, 0)) for w in flat]
    outs = pl.pallas_call(
        _proj_kernel,
        grid=(IN_WIDTH // PROJ_TN, n_i),
        in_specs=[
            pl.BlockSpec((PROJ_TM, D_MODEL), lambda j, i: (i, 0)),
            pl.BlockSpec((1, D_MODEL), lambda j, i: (0, 0)),
            pl.BlockSpec((D_MODEL, PROJ_TN), lambda j, i: (0, j)),
        ] + w_specs,
        out_specs=[pl.BlockSpec((PROJ_TM, PROJ_TN), lambda j, i: (i, j))] + w_specs,
        out_shape=[jax.ShapeDtypeStruct((n, IN_WIDTH), BF16)]
        + [jax.ShapeDtypeStruct(w.shape, BF16) for w in flat],
        compiler_params=_cparams(("arbitrary", "arbitrary")),
        interpret=interpret,
        name="proj",
    )(x2, g_mix, w_in_bf16, *flat)
    return outs[0], outs[1].reshape(w_gate.shape), outs[2].reshape(w_up.shape), outs[3].reshape(w_down.shape)


def _attn_unit(q2, kk, vv, bias_a, bias_b):
    lane = lax.broadcasted_iota(jnp.int32, (BLK, LANES), 1)
    left = lane < HEAD_DIM
    zero = jnp.zeros_like(q2)
    nt = (((1,), (1,)), ((), ()))
    q_st = jnp.concatenate([jnp.where(left, q2, zero), jnp.where(left, zero, q2)], axis=0)
    s = lax.dot_general(q_st, kk, nt, preferred_element_type=F32) + jnp.concatenate([bias_a, bias_b], axis=0)
    m = jnp.max(s, axis=-1, keepdims=True)
    p = jnp.exp(s - m)
    den = jnp.sum(p, axis=-1, keepdims=True)
    o = jnp.dot(p.astype(BF16), vv, preferred_element_type=F32)
    return (jnp.where(left, o[:BLK], o[BLK:]), jnp.where(left, m[:BLK], m[BLK:]),
            jnp.where(left, den[:BLK], den[BLK:]))


def _attn_kernel(q1_ref, q2_ref, q3_ref, k1_ref, k2_ref, k3_ref, v1_ref, v2_ref, v3_ref,
                 bias_ref, o_ref, qf_ref, kvf_ref, acc_ref, *, seq):
    s_id = pl.program_id(2)
    conv_rows = 256

    @pl.when(s_id == 0)
    def _():
        for slot, ref in enumerate((k2_ref, v2_ref, k3_ref, v3_ref)):
            def body(i, c, slot=slot, ref=ref):
                r0 = pl.multiple_of(i * conv_rows, conv_rows)
                kvf_ref[slot, pl.ds(r0, conv_rows), :] = ref[pl.ds(r0, conv_rows), :].astype(F32)
                return c
            lax.fori_loop(0, seq // conv_rows, body, 0)

    for slot, ref in enumerate((q2_ref, q3_ref)):
        def body(i, c, slot=slot, ref=ref):
            r0 = pl.multiple_of(i * conv_rows, conv_rows)
            qf_ref[slot, pl.ds(r0, conv_rows), :] = ref[pl.ds(r0, conv_rows), :].astype(F32) * 0.125
            return c
        lax.fori_loop(0, SPAN // conv_rows, body, 0)

    def dilated_unit(gi, d, m, r, first):
        slot = gi - 1
        loc = BLK * m * d + r
        cur = s_id * SPAN + loc
        prev = jnp.maximum(cur - BLK * d, r)
        q2 = qf_ref[slot, pl.ds(loc, BLK, stride=d), :].astype(BF16)
        kk = jnp.concatenate([kvf_ref[2 * slot, pl.ds(prev, BLK, stride=d), :],
                              kvf_ref[2 * slot, pl.ds(cur, BLK, stride=d), :]], axis=0).astype(BF16)
        vv = jnp.concatenate([kvf_ref[2 * slot + 1, pl.ds(prev, BLK, stride=d), :],
                              kvf_ref[2 * slot + 1, pl.ds(cur, BLK, stride=d), :]], axis=0).astype(BF16)
        num, mx, den = _attn_unit(q2, kk, vv, bias_ref[gi, first, 0], bias_ref[gi, first, 1])
        acc_ref[3 * slot + 0, pl.ds(loc, BLK, stride=d), :] = num
        acc_ref[3 * slot + 1, pl.ds(loc, BLK, stride=d), :] = mx
        acc_ref[3 * slot + 2, pl.ds(loc, BLK, stride=d), :] = den

    d3 = ATTN_GROUPS[2][1]
    first_span = jnp.where(s_id == 0, 1, 0)

    def body3(i, c):
        for u in range(ATTN_UNROLL):
            dilated_unit(2, d3, 0, i * ATTN_UNROLL + u, first_span)
        return c
    lax.fori_loop(0, d3 // ATTN_UNROLL, body3, 0)

    d2 = ATTN_GROUPS[1][1]

    blocks2 = ATTN_UNROLL // d2

    def body2(i, c):
        for mm in range(blocks2):
            m = i * blocks2 + mm
            first = jnp.where(jnp.logical_and(s_id == 0, m == 0), 1, 0)
            for r in range(d2):
                dilated_unit(1, d2, m, r, first)
        return c
    lax.fori_loop(0, SPAN // (BLK * d2 * blocks2), body2, 0)

    def dense_unit(m):
        loc = pl.multiple_of(m * BLK, BLK)
        cur = pl.multiple_of(s_id * SPAN + loc, BLK)
        prev = pl.multiple_of(jnp.maximum(cur - BLK, 0), BLK)
        first = jnp.where(cur == 0, 1, 0)
        q2 = q1_ref[pl.ds(loc, BLK), :] * 0.125
        kk = jnp.concatenate([k1_ref[pl.ds(prev, BLK), :], k1_ref[pl.ds(cur, BLK), :]], axis=0)
        vv = jnp.concatenate([v1_ref[pl.ds(prev, BLK), :], v1_ref[pl.ds(cur, BLK), :]], axis=0)
        n1, m1, d1 = _attn_unit(q2, kk, vv, bias_ref[0, first, 0], bias_ref[0, first, 1])
        n2, m2, dd2 = (acc_ref[j, pl.ds(loc, BLK), :] for j in (0, 1, 2))
        n3, m3, dd3 = (acc_ref[j, pl.ds(loc, BLK), :] for j in (3, 4, 5))
        mx = jnp.maximum(jnp.maximum(m1, m2), m3)
        w1, w2, w3 = jnp.exp(m1 - mx), jnp.exp(m2 - mx), jnp.exp(m3 - mx)
        num = w1 * n1 + w2 * n2 + w3 * n3
        den = w1 * d1 + w2 * dd2 + w3 * dd3
        o_ref[pl.ds(loc, BLK), :] = (num / den).astype(o_ref.dtype)

    def body1(i, c):
        for u in range(ATTN_UNROLL):
            dense_unit(i * ATTN_UNROLL + u)
        return c
    lax.fori_loop(0, SPAN // (BLK * ATTN_UNROLL), body1, 0)


def _attn_bias():
    slopes = np.exp2(-8.0 * np.arange(1, ATTN_HEADS + 1, dtype=np.float64) / ATTN_HEADS)
    qi = np.arange(BLK)[:, None]
    kj = np.arange(2 * BLK)[None, :]
    rel = qi + BLK - kj
    out = np.zeros((N_GROUPS, 2, ATTN_HEADS, BLK, 2 * BLK), np.float32)
    for gi, (window, d) in enumerate(ATTN_GROUPS):
        n_back = window // d
        assert n_back == BLK
        valid = (rel >= 0) & (rel <= n_back)
        bias = -slopes[:, None, None] * (rel * d)[None].astype(np.float64)
        out[gi, 0] = np.where(valid[None], bias, NEG)
        out[gi, 1] = np.where((valid & (kj >= BLK))[None], bias, NEG)
    return jnp.asarray(out)


def _attention(proj, batch, seq, interpret):
    n = batch * seq
    spans = seq // SPAN
    n_hp = GROUP_W // LANES
    qcol = lambda g: (COL_QA + g * GROUP_W) // LANES
    kcol = lambda g: (COL_KA + g * GROUP_W) // LANES
    vcol = lambda g: (COL_VA + g * GROUP_W) // LANES
    q_specs = [pl.BlockSpec((SPAN, LANES), functools.partial(lambda b, hp, s, c: (b * spans + s, c + hp), c=qcol(g)))
               for g in range(N_GROUPS)]
    k_specs = [pl.BlockSpec((seq, LANES), functools.partial(lambda b, hp, s, c: (b, c + hp), c=kcol(g)))
               for g in range(N_GROUPS)]
    v_specs = [pl.BlockSpec((seq, LANES), functools.partial(lambda b, hp, s, c: (b, c + hp), c=vcol(g)))
               for g in range(N_GROUPS)]
    bias_spec = pl.BlockSpec((N_GROUPS, 2, 2, BLK, 2 * BLK), lambda b, hp, s: (0, 0, hp, 0, 0))
    return pl.pallas_call(
        functools.partial(_attn_kernel, seq=seq),
        grid=(batch, n_hp, spans),
        in_specs=q_specs + k_specs + v_specs + [bias_spec],
        out_specs=pl.BlockSpec((SPAN, LANES), lambda b, hp, s: (b * spans + s, hp)),
        out_shape=jax.ShapeDtypeStruct((n, GROUP_W), BF16),
        scratch_shapes=[
            pltpu.VMEM((2, SPAN, LANES), F32),
            pltpu.VMEM((4, seq, LANES), F32),
            pltpu.VMEM((6, SPAN, LANES), F32),
        ],
        compiler_params=_cparams(("arbitrary", "arbitrary", "arbitrary")),
        interpret=interpret,
        name="attn",
    )(*([proj] * 9), _attn_bias())


def _ret_kernel(q_ref, k_ref, v_ref, gr_ref, dec_ref, xi_ref, zeta_ref, gch_ref, o_ref, st_ref):
    @pl.when(pl.program_id(1) == 0)
    def _():
        st_ref[...] = jnp.zeros_like(st_ref)

    nt = (((1,), (1,)), ((), ()))
    scale = RET_DK ** -0.5

    def body(c, carry):
        r0 = pl.multiple_of(c * RET_CHUNK, RET_CHUNK)
        rows = pl.ds(r0, RET_CHUNK)
        for h in range(RET_HEADS):
            kcols = slice(h * RET_DK, (h + 1) * RET_DK)
            vcols = slice(h * RET_DV, (h + 1) * RET_DV)
            qi = q_ref[rows, kcols]
            kf = k_ref[rows, kcols].astype(F32) * scale
            ki = kf.astype(BF16)
            kz_t = jnp.transpose(kf * zeta_ref[h]).astype(BF16)
            vi = v_ref[rows, vcols]
            att = lax.dot_general(qi, ki, nt, preferred_element_type=F32) * dec_ref[h]
            inner = jnp.dot(att.astype(BF16), vi, preferred_element_type=F32)
            st = st_ref[h]
            cross = jnp.dot(qi, st.astype(BF16), preferred_element_type=F32) * xi_ref[h]
            st_ref[h] = gch_ref[h] * st + jnp.dot(kz_t, vi, preferred_element_type=F32)
            y = inner + cross
            mu = jnp.mean(y, axis=-1, keepdims=True)
            yc = y - mu
            var = jnp.mean(yc * yc, axis=-1, keepdims=True)
            yn = yc * lax.rsqrt(var + EPS)
            g = gr_ref[rows, vcols].astype(F32)
            o_ref[rows, vcols] = (g * _sigmoid(g) * yn).astype(o_ref.dtype)
        return carry
    lax.fori_loop(0, RET_TS // RET_CHUNK, body, 0)


def _ret_tables():
    c = RET_CHUNK
    log_g = np.log1p(-np.exp2(-5.0 - np.arange(RET_HEADS, dtype=np.float64)))
    pos = np.arange(c, dtype=np.float64)
    diff = pos[:, None] - pos[None, :]
    dec = np.where(diff >= 0, np.exp(log_g[:, None, None] * np.maximum(diff, 0.0)), 0.0)
    xi = np.exp(log_g[:, None] * (pos + 1.0))[..., None] * np.ones((1, 1, RET_DV))
    zeta = np.exp(log_g[:, None] * (c - 1.0 - pos))[..., None] * np.ones((1, 1, RET_DK))
    gch = np.exp(log_g * c)[:, None, None] * np.ones((1, 1, RET_DV))
    return tuple(jnp.asarray(t, F32) for t in (dec, xi, zeta, gch))


def _retention(proj, batch, seq, interpret):
    n = batch * seq
    dec, xi, zeta, gch = _ret_tables()
    qk_w = RET_HEADS * RET_DK
    v_w = RET_HEADS * RET_DV
    nts = seq // RET_TS
    const3 = lambda b, t: (0, 0, 0)
    return pl.pallas_call(
        _ret_kernel,
        grid=(batch, nts),
        in_specs=[
            pl.BlockSpec((RET_TS, qk_w), lambda b, t: (b * nts + t, COL_QR // qk_w)),
            pl.BlockSpec((RET_TS, qk_w), lambda b, t: (b * nts + t, COL_KR // qk_w)),
            pl.BlockSpec((RET_TS, v_w), lambda b, t: (b * nts + t, COL_VR // v_w)),
            pl.BlockSpec((RET_TS, v_w), lambda b, t: (b * nts + t, COL_GR // v_w)),
            pl.BlockSpec((RET_HEADS, RET_CHUNK, RET_CHUNK), const3),
            pl.BlockSpec((RET_HEADS, RET_CHUNK, RET_DV), const3),
            pl.BlockSpec((RET_HEADS, RET_CHUNK, RET_DK), const3),
            pl.BlockSpec((RET_HEADS, 1, RET_DV), const3),
        ],
        out_specs=pl.BlockSpec((RET_TS, v_w), lambda b, t: (b * nts + t, 0)),
        out_shape=jax.ShapeDtypeStruct((n, v_w), BF16),
        scratch_shapes=[pltpu.VMEM((RET_HEADS, RET_DK, RET_DV), F32)],
        compiler_params=_cparams(("arbitrary", "arbitrary")),
        interpret=interpret,
        name="retention",
    )(proj, proj, proj, proj, dec, xi, zeta, gch)


ROUTER_OFF = N_EXPERT_GROUPS


def _pack_bf16_pair(a, b):
    hi = lax.bitcast_convert_type(a.astype(BF16).astype(F32), jnp.uint32)
    lo = lax.bitcast_convert_type(b.astype(BF16).astype(F32), jnp.uint32)
    return lax.bitcast_convert_type(hi | (lo >> 16), jnp.int32)


def _unpack_bf16_pair(w):
    u = lax.bitcast_convert_type(w, jnp.uint32)
    a = lax.bitcast_convert_type(u & jnp.uint32(0xFFFF0000), F32).astype(BF16)
    b = lax.bitcast_convert_type(u << 16, F32).astype(BF16)
    return a, b


def _pack_rows(y):
    q = D_MODEL // 4
    return (_pack_bf16_pair(y[:, 0:q], y[:, 2 * q:3 * q]), _pack_bf16_pair(y[:, q:2 * q], y[:, 3 * q:4 * q]))


def _unpack_rows(slab0, slab1):
    q0, q2 = _unpack_bf16_pair(slab0)
    q1, q3 = _unpack_bf16_pair(slab1)
    return jnp.concatenate([q0, q1, q2, q3], axis=1)


def _mix_kernel(oa_ref, or_ref, ga_ref, gr_ref, x_ref, pa_ref, pr_ref, wo_ref, gf_ref, wr_ref, br_ref,
                x1_ref, h2_ref, route_ref, route_t_ref, cnt_ref, carry_ref):
    @pl.when(pl.program_id(0) == 0)
    def _():
        carry_ref[...] = jnp.zeros_like(carry_ref)

    for c in range(MIX_TM // MIX_CHUNK):
        _mix_rows(pl.ds(c * MIX_CHUNK, MIX_CHUNK), oa_ref, or_ref, ga_ref, gr_ref, x_ref, pa_ref, pr_ref, wo_ref,
                  gf_ref, wr_ref, br_ref, x1_ref, h2_ref, route_ref, route_t_ref, cnt_ref, carry_ref)


def _mix_rows(rows, oa_ref, or_ref, ga_ref, gr_ref, x_ref, pa_ref, pr_ref, wo_ref, gf_ref, wr_ref, br_ref,
              x1_ref, h2_ref, route_ref, route_t_ref, cnt_ref, carry_ref):
    a = jnp.dot(oa_ref[rows, :], pa_ref[...], preferred_element_type=F32)
    r = jnp.dot(or_ref[rows, :], pr_ref[...], preferred_element_type=F32)
    merged = (_sigmoid(ga_ref[rows, :].astype(F32)) * a + _sigmoid(gr_ref[rows, :].astype(F32)) * r)
    x1 = x_ref[rows, :] + jnp.dot(merged.astype(BF16), wo_ref[...], preferred_element_type=F32)
    x1_ref[rows, :] = x1
    ms = jnp.mean(x1 * x1, axis=-1, keepdims=True)
    h2 = x1 * lax.rsqrt(ms + EPS) * gf_ref[...]
    h2_ref[0, rows, :], h2_ref[1, rows, :] = _pack_rows(h2)

    h_hi = h2.astype(BF16)
    h_lo = (h2 - h_hi.astype(F32)).astype(BF16)
    both = jnp.dot(h_hi, wr_ref[...], preferred_element_type=F32)
    logits = (both[:, :LANES] + both[:, LANES:]
              + jnp.dot(h_lo, wr_ref[:, :LANES], preferred_element_type=F32) + br_ref[...])
    tm = logits.shape[0]
    lane = lax.broadcasted_iota(jnp.int32, (tm, LANES), 1).astype(F32)
    big = jnp.float32(4 * LANES)
    ninf = -jnp.inf
    is_g = lane < N_EXPERT_GROUPS
    gl = jnp.where(is_g, logits, ninf)
    gmax = jnp.max(gl, axis=-1, keepdims=True)
    gsum = jnp.sum(jnp.where(is_g, jnp.exp(gl - gmax), 0.0), axis=-1, keepdims=True)
    g_val = 1.0 / gsum
    g_idx = jnp.min(jnp.where(jnp.logical_and(is_g, gl == gmax), lane, big), axis=-1, keepdims=True)
    lo = ROUTER_OFF + EXPERTS_PER_GROUP * g_idx
    in_grp = jnp.logical_and(lane >= lo, lane < lo + EXPERTS_PER_GROUP)
    el = jnp.where(in_grp, logits, ninf)
    v1 = jnp.max(el, axis=-1, keepdims=True)
    i1 = jnp.min(jnp.where(jnp.logical_and(in_grp, el == v1), lane, big), axis=-1, keepdims=True)
    rest = jnp.logical_and(in_grp, lane != i1)
    el2 = jnp.where(rest, logits, ninf)
    v2 = jnp.max(el2, axis=-1, keepdims=True)
    i2 = jnp.min(jnp.where(jnp.logical_and(rest, el2 == v2), lane, big), axis=-1, keepdims=True)
    t = jnp.exp(v2 - v1)
    w1 = g_val / (1.0 + t)
    w2 = g_val * t / (1.0 + t)

    sel = jnp.logical_or(lane == i1, lane == i2)
    sel_bf = jnp.where(sel, 1.0, 0.0).astype(BF16)
    row = lax.broadcasted_iota(jnp.int32, (tm, tm), 0)
    col = lax.broadcasted_iota(jnp.int32, (tm, tm), 1)
    tri = jnp.where(col < row, 1.0, 0.0).astype(BF16)
    before = jnp.dot(tri, sel_bf, preferred_element_type=F32) + carry_ref[...]
    r1 = jnp.sum(jnp.where(lane == i1, before, 0.0), axis=-1, keepdims=True)
    r2 = jnp.sum(jnp.where(lane == i2, before, 0.0), axis=-1, keepdims=True)
    carry = carry_ref[...] + jnp.sum(jnp.where(sel, 1.0, 0.0), axis=0, keepdims=True)
    carry_ref[...] = carry
    cnt_ref[...] = carry

    vals = (i1 - ROUTER_OFF, i2 - ROUTER_OFF, w1, w2, r1, r2)
    route = jnp.zeros((tm, LANES), F32)
    for j, v in enumerate(vals):
        route = jnp.where(lane == j, v, route)
    route_ref[rows, :] = route
    route_t_ref[:, rows] = jnp.transpose(route)[:ROUTE_ROWS, :]


def _mix(o_attn, o_ret, proj, x2, pa, pr, wo, g_ffn, w_router, b_router, interpret):
    n = x2.shape[0]
    tm = MIX_TM
    const = lambda i: (0, 0)
    return pl.pallas_call(
        _mix_kernel,
        grid=(n // tm,),
        in_specs=[
            pl.BlockSpec((tm, GROUP_W), lambda i: (i, 0)),
            pl.BlockSpec((tm, D_MODEL), lambda i: (i, 0)),
            pl.BlockSpec((tm, D_MODEL), lambda i: (i, COL_GATE_A // D_MODEL)),
            pl.BlockSpec((tm, D_MODEL), lambda i: (i, COL_GATE_R // D_MODEL)),
            pl.BlockSpec((tm, D_MODEL), lambda i: (i, 0)),
            pl.BlockSpec((GROUP_W, D_MODEL), const),
            pl.BlockSpec((D_MODEL, D_MODEL), const),
            pl.BlockSpec((D_MODEL, D_MODEL), const),
            pl.BlockSpec((1, D_MODEL), const),
            pl.BlockSpec((D_MODEL, 2 * LANES), const),
            pl.BlockSpec((1, LANES), const),
        ],
        out_specs=[
            pl.BlockSpec((tm, D_MODEL), lambda i: (i, 0)),
            pl.BlockSpec((2, tm, SC_ROW_WORDS), lambda i: (0, i, 0)),
            pl.BlockSpec((tm, LANES), lambda i: (i, 0)),
            pl.BlockSpec((ROUTE_ROWS, tm), lambda i: (0, i)),
            pl.BlockSpec((1, LANES), const),
        ],
        out_shape=[
            jax.ShapeDtypeStruct((n, D_MODEL), F32),
            jax.ShapeDtypeStruct((2, n, SC_ROW_WORDS), jnp.int32),
            jax.ShapeDtypeStruct((n, LANES), F32),
            jax.ShapeDtypeStruct((ROUTE_ROWS, n), F32),
            jax.ShapeDtypeStruct((1, LANES), F32),
        ],
        scratch_shapes=[pltpu.VMEM((1, LANES), F32)],
        compiler_params=_cparams(("arbitrary",)),
        interpret=interpret,
        name="mix_router",
    )(o_attn, o_ret, proj, proj, x2, pa, pr, wo, g_ffn, w_router, b_router)


def _expert_kernel(te_ref, tv_ref, xs_ref, wg_ref, wu_ref, wd_ref, o_ref):
    i = pl.program_id(0)

    @pl.when(tv_ref[i] != 0)
    def _():
        xs = _unpack_rows(xs_ref[0], xs_ref[1])
        a = jnp.dot(xs, wg_ref[0], preferred_element_type=F32)
        u = jnp.dot(xs, wu_ref[0], preferred_element_type=F32)
        hid = (a * _sigmoid(a) * u).astype(BF16)
        y = jnp.dot(hid, wd_ref[0], preferred_element_type=F32)
        o_ref[0], o_ref[1] = _pack_rows(y)

    @pl.when(tv_ref[i] == 0)
    def _():
        o_ref[...] = jnp.zeros_like(o_ref)


def _experts(xs, tile_expert, tile_valid, w_gate, w_up, w_down, interpret):
    p = xs.shape[1]
    n_tiles = p // EXP_TM
    grid_spec = pltpu.PrefetchScalarGridSpec(
        num_scalar_prefetch=2,
        grid=(n_tiles,),
        in_specs=[
            pl.BlockSpec((2, EXP_TM, SC_ROW_WORDS), lambda i, te, tv: (0, i, 0)),
            pl.BlockSpec((1, D_MODEL, EXPERT_FF), lambda i, te, tv: (te[i], 0, 0)),
            pl.BlockSpec((1, D_MODEL, EXPERT_FF), lambda i, te, tv: (te[i], 0, 0)),
            pl.BlockSpec((1, EXPERT_FF, D_MODEL), lambda i, te, tv: (te[i], 0, 0)),
        ],
        out_specs=pl.BlockSpec((2, EXP_TM, SC_ROW_WORDS), lambda i, te, tv: (0, i, 0)),
    )
    return pl.pallas_call(
        _expert_kernel,
        grid_spec=grid_spec,
        out_shape=jax.ShapeDtypeStruct((2, p, SC_ROW_WORDS), jnp.int32),
        compiler_params=_cparams(("arbitrary",)),
        interpret=interpret,
        name="experts",
    )(tile_expert, tile_valid, xs, w_gate, w_up, w_down)


def _final_kernel(x1_ref, yab_ref, route_ref, g_ref, o_ref):
    route = route_ref[...]
    w1 = route[:, 2:3]
    w2 = route[:, 3:4]
    ya = _unpack_rows(yab_ref[0], yab_ref[2]).astype(F32)
    yb = _unpack_rows(yab_ref[1], yab_ref[3]).astype(F32)
    x2 = x1_ref[...] + w1 * ya + w2 * yb
    ms = jnp.mean(x2 * x2, axis=-1, keepdims=True)
    o_ref[...] = x2 * lax.rsqrt(ms + EPS) * g_ref[...]


def _final(x1, yab, route, g_final, interpret):
    n = x1.shape[0]
    tm = FIN_TM
    row = lambda i: (i, 0)
    return pl.pallas_call(
        _final_kernel,
        grid=(n // tm,),
        in_specs=[
            pl.BlockSpec((tm, D_MODEL), row),
            pl.BlockSpec((4, tm, SC_ROW_WORDS), lambda i: (0, i, 0)),
            pl.BlockSpec((tm, LANES), row),
            pl.BlockSpec((1, D_MODEL), lambda i: (0, 0)),
        ],
        out_specs=pl.BlockSpec((tm, D_MODEL), row),
        out_shape=jax.ShapeDtypeStruct((n, D_MODEL), F32),
        compiler_params=_cparams(("arbitrary",)),
        interpret=interpret,
        name="combine_final",
    )(x1, yab, route, g_final)


def _permute_w_in(w_in):
    splits = np.cumsum([QKV_W, QKV_W, QKV_W, 512, 512, 1024, 1024, D_MODEL, D_MODEL])[:-1].tolist()
    qa, ka, va, qr, kr, vr, gr, gate_a, gate_r = jnp.split(w_in, splits, axis=-1)
    return jnp.concatenate([gate_a, gate_r, vr, gr, qa, ka, va, qr, kr], axis=-1).astype(BF16)


def _route_plan(route_t, counts, n):
    e1 = route_t[0].astype(jnp.int32)
    e2 = route_t[1].astype(jnp.int32)
    r1 = route_t[4].astype(jnp.int32)
    r2 = route_t[5].astype(jnp.int32)
    cnt = counts[0, ROUTER_OFF:ROUTER_OFF + N_EXPERTS].astype(jnp.int32)
    padded = ((cnt + EXP_TM - 1) // EXP_TM) * EXP_TM
    ends = jnp.cumsum(padded)
    offs = ends - padded
    lanes = jnp.arange(N_EXPERTS, dtype=jnp.int32)[None, :]
    dest1 = jnp.sum(jnp.where(e1[:, None] == lanes, offs[None, :], 0), axis=1) + r1
    dest2 = jnp.sum(jnp.where(e2[:, None] == lanes, offs[None, :], 0), axis=1) + r2
    n_rows = 2 * n + N_EXPERTS * EXP_TM
    n_tiles = n_rows // EXP_TM
    tile_start = jnp.arange(n_tiles, dtype=jnp.int32) * EXP_TM
    tile_valid = (tile_start < ends[-1]).astype(jnp.int32)
    te = jnp.sum((ends[None, :] <= tile_start[:, None]).astype(jnp.int32), axis=1)
    te_last = jnp.sum((ends[None, :] <= ends[-1] - EXP_TM).astype(jnp.int32))
    tile_expert = jnp.minimum(jnp.where(tile_valid == 1, te, te_last), N_EXPERTS - 1)
    return dest1, dest2, tile_expert, tile_valid, n_rows


def _sc_mesh():
    return plsc.VectorSubcoreMesh(core_axis_name="core", subcore_axis_name="subcore")


def _sc_scatter_rows(rows, idx_a, idx_b, n_out):
    n_in, w = rows.shape

    @functools.partial(pl.kernel, out_type=jax.ShapeDtypeStruct((n_out, w), rows.dtype), mesh=_sc_mesh(),
                       scratch_types=[], name="sc_scatter_rows")
    def scatter(x_hbm, ia_hbm, ib_hbm, o_hbm):
        def body(x_vmem, ia_vmem, ib_vmem):
            pltpu.sync_copy(x_vmem, o_hbm.at[ia_vmem.at[0]])
            pltpu.sync_copy(x_vmem, o_hbm.at[ib_vmem.at[0]])

        pltpu.emit_pipeline(
            body,
            grid=(n_in // SC_WINDOW,),
            in_specs=[pl.BlockSpec((SC_WINDOW, w), lambda i: (i, 0)),
                      pl.BlockSpec((1, SC_WINDOW), lambda i: (0, i)),
                      pl.BlockSpec((1, SC_WINDOW), lambda i: (0, i))],
            out_specs=[],
            core_axis_name=("core", "subcore"),
            dimension_semantics=(pltpu.PARALLEL,),
        )(x_hbm, ia_hbm, ib_hbm)

    return scatter(rows, idx_a, idx_b)


def _sc_gather_rows(table, idx):
    n_idx = idx.shape[1]
    w = table.shape[1]

    @functools.partial(pl.kernel, out_type=jax.ShapeDtypeStruct((n_idx, w), table.dtype), mesh=_sc_mesh(),
                       scratch_types=[], name="sc_gather_rows")
    def gather(t_hbm, i_hbm, o_hbm):
        def body(i_vmem, o_vmem):
            pltpu.sync_copy(t_hbm.at[i_vmem.at[0]], o_vmem)

        pltpu.emit_pipeline(
            body,
            grid=(n_idx // SC_WINDOW,),
            in_specs=[pl.BlockSpec((1, SC_WINDOW), lambda i: (0, i))],
            out_specs=[pl.BlockSpec((SC_WINDOW, w), lambda i: (i, 0))],
            core_axis_name=("core", "subcore"),
            dimension_semantics=(pltpu.PARALLEL,),
        )(i_hbm, o_hbm)

    return gather(table, idx)


def _forward(x, g_mix, w_in, w_attn_branch, w_ret_branch, w_out, g_ffn, w_group_router, b_group_router,
             w_expert_router, b_expert_router, w_gate, w_up, w_down, g_final, interpret=False):
    batch, seq, d = x.shape
    n = batch * seq
    x2 = x.reshape(n, d)
    proj, wg_bf, wu_bf, wd_bf = _proj(x2, g_mix[0][None, :], _permute_w_in(w_in[0]), w_gate[0], w_up[0],
                                      w_down[0], interpret)
    o_attn = _attention(proj, batch, seq, interpret)
    o_ret = _retention(proj, batch, seq, interpret)
    pad = LANES - N_EXPERT_GROUPS - N_EXPERTS
    w_router = jnp.concatenate([w_group_router[0], w_expert_router[0], jnp.zeros((d, pad), F32)], axis=-1)
    w_router_hi = w_router.astype(BF16)
    w_router_lo = (w_router - w_router_hi.astype(F32)).astype(BF16)
    w_router2 = jnp.concatenate([w_router_hi, w_router_lo], axis=-1)
    b_router = jnp.concatenate([b_group_router[0], b_expert_router[0], jnp.zeros((pad,), F32)])[None, :]
    x1, h2p, route, route_t, counts = _mix(o_attn, o_ret, proj, x2, w_attn_branch[0].astype(BF16),
                                           w_ret_branch[0].astype(BF16), w_out[0].astype(BF16),
                                           g_ffn[0][None, :], w_router2, b_router, interpret)
    dest1, dest2, tile_expert, tile_valid, n_rows = _route_plan(route_t, counts, n)
    idx_a = jnp.concatenate([dest1, dest1 + n_rows])[None, :]
    idx_b = jnp.concatenate([dest2, dest2 + n_rows])[None, :]
    xs = _sc_scatter_rows(h2p.reshape(2 * n, SC_ROW_WORDS), idx_a, idx_b, 2 * n_rows)
    ys = _experts(xs.reshape(2, n_rows, SC_ROW_WORDS), tile_expert, tile_valid, wg_bf, wu_bf, wd_bf, interpret)
    idx_g = jnp.concatenate([dest1, dest2, dest1 + n_rows, dest2 + n_rows])[None, :]
    yab = _sc_gather_rows(ys.reshape(2 * n_rows, SC_ROW_WORDS), idx_g)
    out = _final(x1, yab.reshape(4, n, SC_ROW_WORDS), route, g_final[None, :], interpret)
    return out.reshape(batch, seq, d)


def kernel(x, g_mix, w_in, w_attn_branch, w_ret_branch, w_out, g_ffn, w_group_router, b_group_router,
           w_expert_router, b_expert_router, w_gate, w_up, w_down, g_final):
    return _forward(x, g_mix, w_in, w_attn_branch, w_ret_branch, w_out, g_ffn, w_group_router,
                    b_group_router, w_expert_router, b_expert_router, w_gate, w_up, w_down, g_final)
```

```python
import functools

import numpy as np
import jax
import jax.numpy as jnp
from jax import lax
from jax.experimental import pallas as pl
from jax.experimental.pallas import tpu as pltpu
from jax.experimental.pallas import tpu_sc as plsc

F32 = jnp.float32
BF16 = jnp.bfloat16

D_MODEL = 1024
ATTN_GROUPS = ((128, 1), (512, 4), (2048, 16))
N_GROUPS = len(ATTN_GROUPS)
ATTN_HEADS = 8
HEAD_DIM = 64
GROUP_W = ATTN_HEADS * HEAD_DIM
QKV_W = N_GROUPS * GROUP_W
RET_HEADS = 4
RET_DK = 128
RET_DV = 256
RET_CHUNK = 128
RET_TS = 1024
N_EXPERT_GROUPS = 4
EXPERTS_PER_GROUP = 8
N_EXPERTS = N_EXPERT_GROUPS * EXPERTS_PER_GROUP
EXPERT_FF = 512
EPS = 1e-6

LANES = 128
BLK = 128
SPAN = 2048
NEG = -1e30
ATTN_UNROLL = 16
ACC_PARTS = 3

COL_GATE_A = 0
COL_GATE_R = 1024
COL_VR = 2048
COL_GR = COL_VR + RET_HEADS * RET_DV
COL_QA = COL_GR + RET_HEADS * RET_DV
COL_KA = COL_QA + QKV_W
COL_VA = COL_KA + QKV_W
COL_QR = COL_VA + QKV_W
COL_KR = COL_QR + RET_HEADS * RET_DK
IN_WIDTH = COL_KR + RET_HEADS * RET_DK

PROJ_TM = 512
PROJ_TN = IN_WIDTH // 2
MXU_N = 256
MIX_TM = 512
MIX_CHUNK = 256
ROUTE_ROWS = 8
EXP_TM = 512
SC_WINDOW = 128
SC_ROW_WORDS = 256
FIN_TM = 512
VMEM_LIMIT = 56 * 1024 * 1024


def _cparams(sem):
    return pltpu.CompilerParams(dimension_semantics=sem, vmem_limit_bytes=VMEM_LIMIT)


def _sigmoid(x):
    return 0.5 * jnp.tanh(0.5 * x) + 0.5


def _proj_kernel(x_ref, g_ref, w_ref, wg_ref, wu_ref, wd_ref, o_ref, wg_o, wu_o, wd_o):
    x = x_ref[...]
    ms = jnp.mean(x * x, axis=-1, keepdims=True)
    h = (x * lax.rsqrt(ms + EPS) * g_ref[...]).astype(BF16)
    for c in range(PROJ_TN // MXU_N):
        sl = slice(c * MXU_N, (c + 1) * MXU_N)
        o_ref[:, sl] = jnp.dot(h, w_ref[:, sl], preferred_element_type=F32).astype(o_ref.dtype)
    wg_o[...] = wg_ref[...].astype(BF16)
    wu_o[...] = wu_ref[...].astype(BF16)
    wd_o[...] = wd_ref[...].astype(BF16)


def _proj(x2, g_mix, w_in_bf16, w_gate, w_up, w_down, interpret):
    n = x2.shape[0]
    n_i = n // PROJ_TM
    steps = (IN_WIDTH // PROJ_TN) * n_i
    flat = [w.reshape(-1, w.shape[-1]) for w in (w_gate, w_up, w_down)]
    w_specs = [pl.BlockSpec((w.shape[0] // steps, w.shape[1]), lambda j, i: (j * n_i + i, 0)) for w in flat]
    outs = pl.pallas_call(
        _proj_kernel,
        grid=(IN_WIDTH // PROJ_TN, n_i),
        in_specs=[
            pl.BlockSpec((PROJ_TM, D_MODEL), lambda j, i: (i, 0)),
            pl.BlockSpec((1, D_MODEL), lambda j, i: (0, 0)),
            pl.BlockSpec((D_MODEL, PROJ_TN), lambda j, i: (0, j)),
        ] + w_specs,
        out_specs=[pl.BlockSpec((PROJ_TM, PROJ_TN), lambda j, i: (i, j))] + w_specs,
        out_shape=[jax.ShapeDtypeStruct((n, IN_WIDTH), BF16)]
        + [jax.ShapeDtypeStruct(w.shape, BF16) for w in flat],
        compiler_params=_cparams(("arbitrary", "arbitrary")),
        interpret=interpret,
        name="proj",
    )(x2, g_mix, w_in_bf16, *flat)
    return outs[0], outs[1].reshape(w_gate.shape), outs[2].reshape(w_up.shape), outs[3].reshape(w_down.shape)


def _attn_unit(q2, kk, vv, bias_a, bias_b):
    lane = lax.broadcasted_iota(jnp.int32, (BLK, LANES), 1)
    left = lane < HEAD_DIM
    zero = jnp.zeros_like(q2)
    nt = (((1,), (1,)), ((), ()))
    q_st = jnp.concatenate([jnp.where(left, q2, zero), jnp.where(left, zero, q2)], axis=0)
    s = lax.dot_general(q_st, kk, nt, preferred_element_type=F32) + jnp.concatenate([bias_a, bias_b], axis=0)
    m = jnp.max(s, axis=-1, keepdims=True)
    p = jnp.exp(s - m)
    den = jnp.sum(p, axis=-1, keepdims=True)
    o = jnp.dot(p.astype(BF16), vv, preferred_element_type=F32)
    return (jnp.where(left, o[:BLK], o[BLK:]), jnp.where(left, m[:BLK], m[BLK:]),
            jnp.where(left, den[:BLK], den[BLK:]))


def _attn_kernel(q1_ref, q2_ref, q3_ref, k1_ref, k2_ref, k3_ref, v1_ref, v2_ref, v3_ref,
                 bias_ref, o_ref, qf_ref, kvf_ref, acc_ref, *, seq):
    s_id = pl.program_id(2)
    conv_rows = 256

    @pl.when(s_id == 0)
    def _():
        for slot, ref in enumerate((k2_ref, v2_ref, k3_ref, v3_ref)):
            def body(i, c, slot=slot, ref=ref):
                r0 = pl.multiple_of(i * conv_rows, conv_rows)
                kvf_ref[slot, pl.ds(r0, conv_rows), :] = ref[pl.ds(r0, conv_rows), :].astype(F32)
                return c
            lax.fori_loop(0, seq // conv_rows, body, 0)

    for slot, ref in enumerate((q2_ref, q3_ref)):
        def body(i, c, slot=slot, ref=ref):
            r0 = pl.multiple_of(i * conv_rows, conv_rows)
            qf_ref[slot, pl.ds(r0, conv_rows), :] = ref[pl.ds(r0, conv_rows), :].astype(F32) * 0.125
            return c
        lax.fori_loop(0, SPAN // conv_rows, body, 0)

    def dilated_unit(gi, d, m, r, first):
        slot = gi - 1
        loc = BLK * m * d + r
        cur = s_id * SPAN + loc
        prev = jnp.maximum(cur - BLK * d, r)
        q2 = qf_ref[slot, pl.ds(loc, BLK, stride=d), :].astype(BF16)
        kk = jnp.concatenate([kvf_ref[2 * slot, pl.ds(prev, BLK, stride=d), :],
                              kvf_ref[2 * slot, pl.ds(cur, BLK, stride=d), :]], axis=0).astype(BF16)
        vv = jnp.concatenate([kvf_ref[2 * slot + 1, pl.ds(prev, BLK, stride=d), :],
                              kvf_ref[2 * slot + 1, pl.ds(cur, BLK, stride=d), :]], axis=0).astype(BF16)
        parts = _attn_unit(q2, kk, vv, bias_ref[gi, first, 0], bias_ref[gi, first, 1])
        for j, part in enumerate(parts):
            acc_ref[ACC_PARTS * slot + j, pl.ds(loc, BLK, stride=d), :] = part

    d3 = ATTN_GROUPS[2][1]
    first_span = jnp.where(s_id == 0, 1, 0)

    def body3(i, c):
        for u in range(ATTN_UNROLL):
            dilated_unit(2, d3, 0, i * ATTN_UNROLL + u, first_span)
        return c
    lax.fori_loop(0, d3 // ATTN_UNROLL, body3, 0)

    d2 = ATTN_GROUPS[1][1]

    blocks2 = ATTN_UNROLL // d2

    def body2(i, c):
        for mm in range(blocks2):
            m = i * blocks2 + mm
            first = jnp.where(jnp.logical_and(s_id == 0, m == 0), 1, 0)
            for r in range(d2):
                dilated_unit(1, d2, m, r, first)
        return c
    lax.fori_loop(0, SPAN // (BLK * d2 * blocks2), body2, 0)

    def dense_unit(m):
        loc = pl.multiple_of(m * BLK, BLK)
        cur = pl.multiple_of(s_id * SPAN + loc, BLK)
        prev = pl.multiple_of(jnp.maximum(cur - BLK, 0), BLK)
        first = jnp.where(cur == 0, 1, 0)
        q2 = q1_ref[pl.ds(loc, BLK), :] * 0.125
        kk = jnp.concatenate([k1_ref[pl.ds(prev, BLK), :], k1_ref[pl.ds(cur, BLK), :]], axis=0)
        vv = jnp.concatenate([v1_ref[pl.ds(prev, BLK), :], v1_ref[pl.ds(cur, BLK), :]], axis=0)
        n1, m1, d1 = _attn_unit(q2, kk, vv, bias_ref[0, first, 0], bias_ref[0, first, 1])
        n2, m2, dd2 = (acc_ref[j, pl.ds(loc, BLK), :] for j in range(ACC_PARTS))
        n3, m3, dd3 = (acc_ref[ACC_PARTS + j, pl.ds(loc, BLK), :] for j in range(ACC_PARTS))
        mx = jnp.maximum(jnp.maximum(m1, m2), m3)
        w1, w2, w3 = jnp.exp(m1 - mx), jnp.exp(m2 - mx), jnp.exp(m3 - mx)
        num = w1 * n1 + w2 * n2 + w3 * n3
        den = w1 * d1 + w2 * dd2 + w3 * dd3
        o_ref[pl.ds(loc, BLK), :] = (num / den).astype(o_ref.dtype)

    def body1(i, c):
        for u in range(ATTN_UNROLL):
            dense_unit(i * ATTN_UNROLL + u)
        return c
    lax.fori_loop(0, SPAN // (BLK * ATTN_UNROLL), body1, 0)


def _attn_bias():
    slopes = np.exp2(-8.0 * np.arange(1, ATTN_HEADS + 1, dtype=np.float64) / ATTN_HEADS)
    qi = np.arange(BLK)[:, None]
    kj = np.arange(2 * BLK)[None, :]
    rel = qi + BLK - kj
    out = np.zeros((N_GROUPS, 2, ATTN_HEADS, BLK, 2 * BLK), np.float32)
    for gi, (window, d) in enumerate(ATTN_GROUPS):
        n_back = window // d
        assert n_back == BLK
        valid = (rel >= 0) & (rel <= n_back)
        bias = -slopes[:, None, None] * (rel * d)[None].astype(np.float64)
        out[gi, 0] = np.where(valid[None], bias, NEG)
        out[gi, 1] = np.where((valid & (kj >= BLK))[None], bias, NEG)
    return jnp.asarray(out)


def _attention(proj, batch, seq, interpret):
    n = batch * seq
    spans = seq // SPAN
    n_hp = GROUP_W // LANES
    qcol = lambda g: (COL_QA + g * GROUP_W) // LANES
    kcol = lambda g: (COL_KA + g * GROUP_W) // LANES
    vcol = lambda g: (COL_VA + g * GROUP_W) // LANES
    q_specs = [pl.BlockSpec((SPAN, LANES), functools.partial(lambda b, hp, s, c: (b * spans + s, c + hp), c=qcol(g)))
               for g in range(N_GROUPS)]
    k_specs = [pl.BlockSpec((seq, LANES), functools.partial(lambda b, hp, s, c: (b, c + hp), c=kcol(g)))
               for g in range(N_GROUPS)]
    v_specs = [pl.BlockSpec((seq, LANES), functools.partial(lambda b, hp, s, c: (b, c + hp), c=vcol(g)))
               for g in range(N_GROUPS)]
    bias_spec = pl.BlockSpec((N_GROUPS, 2, 2, BLK, 2 * BLK), lambda b, hp, s: (0, 0, hp, 0, 0))
    return pl.pallas_call(
        functools.partial(_attn_kernel, seq=seq),
        grid=(batch, n_hp, spans),
        in_specs=q_specs + k_specs + v_specs + [bias_spec],
        out_specs=pl.BlockSpec((SPAN, LANES), lambda b, hp, s: (b * spans + s, hp)),
        out_shape=jax.ShapeDtypeStruct((n, GROUP_W), BF16),
        scratch_shapes=[
            pltpu.VMEM((2, SPAN, LANES), F32),
            pltpu.VMEM((4, seq, LANES), F32),
            pltpu.VMEM((2 * ACC_PARTS, SPAN, LANES), F32),
        ],
        compiler_params=_cparams(("arbitrary", "arbitrary", "arbitrary")),
        interpret=interpret,
        name="attn",
    )(*([proj] * 9), _attn_bias())


def _ret_kernel(q_ref, k_ref, v_ref, gr_ref, dec_ref, xi_ref, zeta_ref, gch_ref, o_ref, st_ref):
    @pl.when(pl.program_id(1) == 0)
    def _():
        st_ref[...] = jnp.zeros_like(st_ref)

    nt = (((1,), (1,)), ((), ()))
    scale = RET_DK ** -0.5

    def body(c, carry):
        r0 = pl.multiple_of(c * RET_CHUNK, RET_CHUNK)
        rows = pl.ds(r0, RET_CHUNK)
        for h in range(RET_HEADS):
            kcols = slice(h * RET_DK, (h + 1) * RET_DK)
            vcols = slice(h * RET_DV, (h + 1) * RET_DV)
            qi = q_ref[rows, kcols]
            kf = k_ref[rows, kcols].astype(F32) * scale
            ki = kf.astype(BF16)
            kz_t = jnp.transpose(kf * zeta_ref[h]).astype(BF16)
            vi = v_ref[rows, vcols]
            att = lax.dot_general(qi, ki, nt, preferred_element_type=F32) * dec_ref[h]
            inner = jnp.dot(att.astype(BF16), vi, preferred_element_type=F32)
            st = st_ref[h]
            cross = jnp.dot(qi, st.astype(BF16), preferred_element_type=F32) * xi_ref[h]
            st_ref[h] = gch_ref[h] * st + jnp.dot(kz_t, vi, preferred_element_type=F32)
            y = inner + cross
            mu = jnp.mean(y, axis=-1, keepdims=True)
            yc = y - mu
            var = jnp.mean(yc * yc, axis=-1, keepdims=True)
            yn = yc * lax.rsqrt(var + EPS)
            g = gr_ref[rows, vcols].astype(F32)
            o_ref[rows, vcols] = (g * _sigmoid(g) * yn).astype(o_ref.dtype)
        return carry
    lax.fori_loop(0, RET_TS // RET_CHUNK, body, 0)


def _ret_tables():
    c = RET_CHUNK
    log_g = np.log1p(-np.exp2(-5.0 - np.arange(RET_HEADS, dtype=np.float64)))
    pos = np.arange(c, dtype=np.float64)
    diff = pos[:, None] - pos[None, :]
    dec = np.where(diff >= 0, np.exp(log_g[:, None, None] * np.maximum(diff, 0.0)), 0.0)
    xi = np.exp(log_g[:, None] * (pos + 1.0))[..., None] * np.ones((1, 1, RET_DV))
    zeta = np.exp(log_g[:, None] * (c - 1.0 - pos))[..., None] * np.ones((1, 1, RET_DK))
    gch = np.exp(log_g * c)[:, None, None] * np.ones((1, 1, RET_DV))
    return tuple(jnp.asarray(t, F32) for t in (dec, xi, zeta, gch))


def _retention(proj, batch, seq, interpret):
    n = batch * seq
    dec, xi, zeta, gch = _ret_tables()
    qk_w = RET_HEADS * RET_DK
    v_w = RET_HEADS * RET_DV
    nts = seq // RET_TS
    const3 = lambda b, t: (0, 0, 0)
    return pl.pallas_call(
        _ret_kernel,
        grid=(batch, nts),
        in_specs=[
            pl.BlockSpec((RET_TS, qk_w), lambda b, t: (b * nts + t, COL_QR // qk_w)),
            pl.BlockSpec((RET_TS, qk_w), lambda b, t: (b * nts + t, COL_KR // qk_w)),
            pl.BlockSpec((RET_TS, v_w), lambda b, t: (b * nts + t, COL_VR // v_w)),
            pl.BlockSpec((RET_TS, v_w), lambda b, t: (b * nts + t, COL_GR // v_w)),
            pl.BlockSpec((RET_HEADS, RET_CHUNK, RET_CHUNK), const3),
            pl.BlockSpec((RET_HEADS, RET_CHUNK, RET_DV), const3),
            pl.BlockSpec((RET_HEADS, RET_CHUNK, RET_DK), const3),
            pl.BlockSpec((RET_HEADS, 1, RET_DV), const3),
        ],
        out_specs=pl.BlockSpec((RET_TS, v_w), lambda b, t: (b * nts + t, 0)),
        out_shape=jax.ShapeDtypeStruct((n, v_w), BF16),
        scratch_shapes=[pltpu.VMEM((RET_HEADS, RET_DK, RET_DV), F32)],
        compiler_params=_cparams(("arbitrary", "arbitrary")),
        interpret=interpret,
        name="retention",
    )(proj, proj, proj, proj, dec, xi, zeta, gch)


ROUTER_OFF = N_EXPERT_GROUPS


def _pack_bf16_pair(a, b):
    hi = lax.bitcast_convert_type(a.astype(BF16).astype(F32), jnp.uint32)
    lo = lax.bitcast_convert_type(b.astype(BF16).astype(F32), jnp.uint32)
    return lax.bitcast_convert_type(hi | (lo >> 16), jnp.int32)


def _unpack_bf16_pair(w):
    u = lax.bitcast_convert_type(w, jnp.uint32)
    a = lax.bitcast_convert_type(u & jnp.uint32(0xFFFF0000), F32).astype(BF16)
    b = lax.bitcast_convert_type(u << 16, F32).astype(BF16)
    return a, b


def _pack_rows(y):
    q = D_MODEL // 4
    return (_pack_bf16_pair(y[:, 0:q], y[:, 2 * q:3 * q]), _pack_bf16_pair(y[:, q:2 * q], y[:, 3 * q:4 * q]))


def _unpack_rows(slab0, slab1):
    q0, q2 = _unpack_bf16_pair(slab0)
    q1, q3 = _unpack_bf16_pair(slab1)
    return jnp.concatenate([q0, q1, q2, q3], axis=1)


def _mix_kernel(oa_ref, or_ref, ga_ref, gr_ref, x_ref, pa_ref, pr_ref, wo_ref, gf_ref, wr_ref, br_ref,
                x1_ref, h2_ref, route_ref, route_t_ref, cnt_ref, carry_ref):
    @pl.when(pl.program_id(0) == 0)
    def _():
        carry_ref[...] = jnp.zeros_like(carry_ref)

    for c in range(MIX_TM // MIX_CHUNK):
        _mix_rows(pl.ds(c * MIX_CHUNK, MIX_CHUNK), oa_ref, or_ref, ga_ref, gr_ref, x_ref, pa_ref, pr_ref, wo_ref,
                  gf_ref, wr_ref, br_ref, x1_ref, h2_ref, route_ref, route_t_ref, cnt_ref, carry_ref)


def _mix_rows(rows, oa_ref, or_ref, ga_ref, gr_ref, x_ref, pa_ref, pr_ref, wo_ref, gf_ref, wr_ref, br_ref,
              x1_ref, h2_ref, route_ref, route_t_ref, cnt_ref, carry_ref):
    a = jnp.dot(oa_ref[rows, :], pa_ref[...], preferred_element_type=F32)
    r = jnp.dot(or_ref[rows, :], pr_ref[...], preferred_element_type=F32)
    merged = (_sigmoid(ga_ref[rows, :].astype(F32)) * a + _sigmoid(gr_ref[rows, :].astype(F32)) * r)
    x1 = x_ref[rows, :] + jnp.dot(merged.astype(BF16), wo_ref[...], preferred_element_type=F32)
    x1_ref[rows, :] = x1
    ms = jnp.mean(x1 * x1, axis=-1, keepdims=True)
    h2 = x1 * lax.rsqrt(ms + EPS) * gf_ref[...]
    h2_ref[0, rows, :], h2_ref[1, rows, :] = _pack_rows(h2)

    h_hi = h2.astype(BF16)
    h_lo = (h2 - h_hi.astype(F32)).astype(BF16)
    both = jnp.dot(h_hi, wr_ref[...], preferred_element_type=F32)
    logits = (both[:, :LANES] + both[:, LANES:]
              + jnp.dot(h_lo, wr_ref[:, :LANES], preferred_element_type=F32) + br_ref[...])
    tm = logits.shape[0]
    lane = lax.broadcasted_iota(jnp.int32, (tm, LANES), 1).astype(F32)
    big = jnp.float32(4 * LANES)
    ninf = -jnp.inf
    is_g = lane < N_EXPERT_GROUPS
    gl = jnp.where(is_g, logits, ninf)
    gmax = jnp.max(gl, axis=-1, keepdims=True)
    gsum = jnp.sum(jnp.where(is_g, jnp.exp(gl - gmax), 0.0), axis=-1, keepdims=True)
    g_val = 1.0 / gsum
    g_idx = jnp.min(jnp.where(jnp.logical_and(is_g, gl == gmax), lane, big), axis=-1, keepdims=True)
    lo = ROUTER_OFF + EXPERTS_PER_GROUP * g_idx
    in_grp = jnp.logical_and(lane >= lo, lane < lo + EXPERTS_PER_GROUP)
    el = jnp.where(in_grp, logits, ninf)
    v1 = jnp.max(el, axis=-1, keepdims=True)
    i1 = jnp.min(jnp.where(jnp.logical_and(in_grp, el == v1), lane, big), axis=-1, keepdims=True)
    rest = jnp.logical_and(in_grp, lane != i1)
    el2 = jnp.where(rest, logits, ninf)
    v2 = jnp.max(el2, axis=-1, keepdims=True)
    i2 = jnp.min(jnp.where(jnp.logical_and(rest, el2 == v2), lane, big), axis=-1, keepdims=True)
    t = jnp.exp(v2 - v1)
    w1 = g_val / (1.0 + t)
    w2 = g_val * t / (1.0 + t)

    sel = jnp.logical_or(lane == i1, lane == i2)
    sel_bf = jnp.where(sel, 1.0, 0.0).astype(BF16)
    row = lax.broadcasted_iota(jnp.int32, (tm, tm), 0)
    col = lax.broadcasted_iota(jnp.int32, (tm, tm), 1)
    tri = jnp.where(col < row, 1.0, 0.0).astype(BF16)
    before = jnp.dot(tri, sel_bf, preferred_element_type=F32) + carry_ref[...]
    r1 = jnp.sum(jnp.where(lane == i1, before, 0.0), axis=-1, keepdims=True)
    r2 = jnp.sum(jnp.where(lane == i2, before, 0.0), axis=-1, keepdims=True)
    carry = carry_ref[...] + jnp.sum(jnp.where(sel, 1.0, 0.0), axis=0, keepdims=True)
    carry_ref[...] = carry
    cnt_ref[...] = carry

    vals = (i1 - ROUTER_OFF, i2 - ROUTER_OFF, w1, w2, r1, r2)
    route = jnp.zeros((tm, LANES), F32)
    for j, v in enumerate(vals):
        route = jnp.where(lane == j, v, route)
    route_ref[rows, :] = route
    route_t_ref[:, rows] = jnp.transpose(route)[:ROUTE_ROWS, :]


def _mix(o_attn, o_ret, proj, x2, pa, pr, wo, g_ffn, w_router, b_router, interpret):
    n = x2.shape[0]
    tm = MIX_TM
    const = lambda i: (0, 0)
    return pl.pallas_call(
        _mix_kernel,
        grid=(n // tm,),
        in_specs=[
            pl.BlockSpec((tm, GROUP_W), lambda i: (i, 0)),
            pl.BlockSpec((tm, D_MODEL), lambda i: (i, 0)),
            pl.BlockSpec((tm, D_MODEL), lambda i: (i, COL_GATE_A // D_MODEL)),
            pl.BlockSpec((tm, D_MODEL), lambda i: (i, COL_GATE_R // D_MODEL)),
            pl.BlockSpec((tm, D_MODEL), lambda i: (i, 0)),
            pl.BlockSpec((GROUP_W, D_MODEL), const),
            pl.BlockSpec((D_MODEL, D_MODEL), const),
            pl.BlockSpec((D_MODEL, D_MODEL), const),
            pl.BlockSpec((1, D_MODEL), const),
            pl.BlockSpec((D_MODEL, 2 * LANES), const),
            pl.BlockSpec((1, LANES), const),
        ],
        out_specs=[
            pl.BlockSpec((tm, D_MODEL), lambda i: (i, 0)),
            pl.BlockSpec((2, tm, SC_ROW_WORDS), lambda i: (0, i, 0)),
            pl.BlockSpec((tm, LANES), lambda i: (i, 0)),
            pl.BlockSpec((ROUTE_ROWS, tm), lambda i: (0, i)),
            pl.BlockSpec((1, LANES), const),
        ],
        out_shape=[
            jax.ShapeDtypeStruct((n, D_MODEL), F32),
            jax.ShapeDtypeStruct((2, n, SC_ROW_WORDS), jnp.int32),
            jax.ShapeDtypeStruct((n, LANES), F32),
            jax.ShapeDtypeStruct((ROUTE_ROWS, n), F32),
            jax.ShapeDtypeStruct((1, LANES), F32),
        ],
        scratch_shapes=[pltpu.VMEM((1, LANES), F32)],
        compiler_params=_cparams(("arbitrary",)),
        interpret=interpret,
        name="mix_router",
    )(o_attn, o_ret, proj, proj, x2, pa, pr, wo, g_ffn, w_router, b_router)


def _expert_kernel(te_ref, tv_ref, xs_ref, wg_ref, wu_ref, wd_ref, o_ref):
    i = pl.program_id(0)

    @pl.when(tv_ref[i] != 0)
    def _():
        xs = _unpack_rows(xs_ref[0], xs_ref[1])
        a = jnp.dot(xs, wg_ref[0], preferred_element_type=F32)
        u = jnp.dot(xs, wu_ref[0], preferred_element_type=F32)
        hid = (a * _sigmoid(a) * u).astype(BF16)
        y = jnp.dot(hid, wd_ref[0], preferred_element_type=F32)
        o_ref[0], o_ref[1] = _pack_rows(y)

    @pl.when(tv_ref[i] == 0)
    def _():
        o_ref[...] = jnp.zeros_like(o_ref)


def _experts(xs, tile_expert, tile_valid, w_gate, w_up, w_down, interpret):
    p = xs.shape[1]
    n_tiles = p // EXP_TM
    grid_spec = pltpu.PrefetchScalarGridSpec(
        num_scalar_prefetch=2,
        grid=(n_tiles,),
        in_specs=[
            pl.BlockSpec((2, EXP_TM, SC_ROW_WORDS), lambda i, te, tv: (0, i, 0)),
            pl.BlockSpec((1, D_MODEL, EXPERT_FF), lambda i, te, tv: (te[i], 0, 0)),
            pl.BlockSpec((1, D_MODEL, EXPERT_FF), lambda i, te, tv: (te[i], 0, 0)),
            pl.BlockSpec((1, EXPERT_FF, D_MODEL), lambda i, te, tv: (te[i], 0, 0)),
        ],
        out_specs=pl.BlockSpec((2, EXP_TM, SC_ROW_WORDS), lambda i, te, tv: (0, i, 0)),
    )
    return pl.pallas_call(
        _expert_kernel,
        grid_spec=grid_spec,
        out_shape=jax.ShapeDtypeStruct((2, p, SC_ROW_WORDS), jnp.int32),
        compiler_params=_cparams(("arbitrary",)),
        interpret=interpret,
        name="experts",
    )(tile_expert, tile_valid, xs, w_gate, w_up, w_down)


def _final_kernel(x1_ref, yab_ref, route_ref, g_ref, o_ref):
    route = route_ref[...]
    w1 = route[:, 2:3]
    w2 = route[:, 3:4]
    ya = _unpack_rows(yab_ref[0], yab_ref[2]).astype(F32)
    yb = _unpack_rows(yab_ref[1], yab_ref[3]).astype(F32)
    x2 = x1_ref[...] + w1 * ya + w2 * yb
    ms = jnp.mean(x2 * x2, axis=-1, keepdims=True)
    o_ref[...] = x2 * lax.rsqrt(ms + EPS) * g_ref[...]


def _final(x1, yab, route, g_final, interpret):
    n = x1.shape[0]
    tm = FIN_TM
    row = lambda i: (i, 0)
    return pl.pallas_call(
        _final_kernel,
        grid=(n // tm,),
        in_specs=[
            pl.BlockSpec((tm, D_MODEL), row),
            pl.BlockSpec((4, tm, SC_ROW_WORDS), lambda i: (0, i, 0)),
            pl.BlockSpec((tm, LANES), row),
            pl.BlockSpec((1, D_MODEL), lambda i: (0, 0)),
        ],
        out_specs=pl.BlockSpec((tm, D_MODEL), row),
        out_shape=jax.ShapeDtypeStruct((n, D_MODEL), F32),
        compiler_params=_cparams(("arbitrary",)),
        interpret=interpret,
        name="combine_final",
    )(x1, yab, route, g_final)


def _permute_w_in(w_in):
    splits = np.cumsum([QKV_W, QKV_W, QKV_W, 512, 512, 1024, 1024, D_MODEL, D_MODEL])[:-1].tolist()
    qa, ka, va, qr, kr, vr, gr, gate_a, gate_r = jnp.split(w_in, splits, axis=-1)
    return jnp.concatenate([gate_a, gate_r, vr, gr, qa, ka, va, qr, kr], axis=-1).astype(BF16)


def _route_plan(route_t, counts, n):
    e1 = route_t[0].astype(jnp.int32)
    e2 = route_t[1].astype(jnp.int32)
    r1 = route_t[4].astype(jnp.int32)
    r2 = route_t[5].astype(jnp.int32)
    cnt = counts[0, ROUTER_OFF:ROUTER_OFF + N_EXPERTS].astype(jnp.int32)
    padded = ((cnt + EXP_TM - 1) // EXP_TM) * EXP_TM
    ends = jnp.cumsum(padded)
    offs = ends - padded
    lanes = jnp.arange(N_EXPERTS, dtype=jnp.int32)[None, :]
    dest1 = jnp.sum(jnp.where(e1[:, None] == lanes, offs[None, :], 0), axis=1) + r1
    dest2 = jnp.sum(jnp.where(e2[:, None] == lanes, offs[None, :], 0), axis=1) + r2
    n_rows = 2 * n + N_EXPERTS * EXP_TM
    n_tiles = n_rows // EXP_TM
    tile_start = jnp.arange(n_tiles, dtype=jnp.int32) * EXP_TM
    tile_valid = (tile_start < ends[-1]).astype(jnp.int32)
    te = jnp.sum((ends[None, :] <= tile_start[:, None]).astype(jnp.int32), axis=1)
    te_last = jnp.sum((ends[None, :] <= ends[-1] - EXP_TM).astype(jnp.int32))
    tile_expert = jnp.minimum(jnp.where(tile_valid == 1, te, te_last), N_EXPERTS - 1)
    return dest1, dest2, tile_expert, tile_valid, n_rows


def _sc_mesh():
    return plsc.VectorSubcoreMesh(core_axis_name="core", subcore_axis_name="subcore")


def _sc_scatter_rows(rows, idx_a, idx_b, n_out):
    n_in, w = rows.shape

    @functools.partial(pl.kernel, out_type=jax.ShapeDtypeStruct((n_out, w), rows.dtype), mesh=_sc_mesh(),
                       scratch_types=[], name="sc_scatter_rows")
    def scatter(x_hbm, ia_hbm, ib_hbm, o_hbm):
        def body(x_vmem, ia_vmem, ib_vmem):
            pltpu.sync_copy(x_vmem, o_hbm.at[ia_vmem.at[0]])
            pltpu.sync_copy(x_vmem, o_hbm.at[ib_vmem.at[0]])

        pltpu.emit_pipeline(
            body,
            grid=(n_in // SC_WINDOW,),
            in_specs=[pl.BlockSpec((SC_WINDOW, w), lambda i: (i, 0)),
                      pl.BlockSpec((1, SC_WINDOW), lambda i: (0, i)),
                      pl.BlockSpec((1, SC_WINDOW), lambda i: (0, i))],
            out_specs=[],
            core_axis_name=("core", "subcore"),
            dimension_semantics=(pltpu.PARALLEL,),
        )(x_hbm, ia_hbm, ib_hbm)

    return scatter(rows, idx_a, idx_b)


def _sc_gather_rows(table, idx):
    n_idx = idx.shape[1]
    w = table.shape[1]

    @functools.partial(pl.kernel, out_type=jax.ShapeDtypeStruct((n_idx, w), table.dtype), mesh=_sc_mesh(),
                       scratch_types=[], name="sc_gather_rows")
    def gather(t_hbm, i_hbm, o_hbm):
        def body(i_vmem, o_vmem):
            pltpu.sync_copy(t_hbm.at[i_vmem.at[0]], o_vmem)

        pltpu.emit_pipeline(
            body,
            grid=(n_idx // SC_WINDOW,),
            in_specs=[pl.BlockSpec((1, SC_WINDOW), lambda i: (0, i))],
            out_specs=[pl.BlockSpec((SC_WINDOW, w), lambda i: (i, 0))],
            core_axis_name=("core", "subcore"),
            dimension_semantics=(pltpu.PARALLEL,),
        )(i_hbm, o_hbm)

    return gather(table, idx)


def _forward(x, g_mix, w_in, w_attn_branch, w_ret_branch, w_out, g_ffn, w_group_router, b_group_router,
             w_expert_router, b_expert_router, w_gate, w_up, w_down, g_final, interpret=False):
    batch, seq, d = x.shape
    n = batch * seq
    x2 = x.reshape(n, d)
    proj, wg_bf, wu_bf, wd_bf = _proj(x2, g_mix[0][None, :], _permute_w_in(w_in[0]), w_gate[0], w_up[0],
                                      w_down[0], interpret)
    o_attn = _attention(proj, batch, seq, interpret)
    o_ret = _retention(proj, batch, seq, interpret)
    pad = LANES - N_EXPERT_GROUPS - N_EXPERTS
    w_router = jnp.concatenate([w_group_router[0], w_expert_router[0], jnp.zeros((d, pad), F32)], axis=-1)
    w_router_hi = w_router.astype(BF16)
    w_router_lo = (w_router - w_router_hi.astype(F32)).astype(BF16)
    w_router2 = jnp.concatenate([w_router_hi, w_router_lo], axis=-1)
    b_router = jnp.concatenate([b_group_router[0], b_expert_router[0], jnp.zeros((pad,), F32)])[None, :]
    x1, h2p, route, route_t, counts = _mix(o_attn, o_ret, proj, x2, w_attn_branch[0].astype(BF16),
                                           w_ret_branch[0].astype(BF16), w_out[0].astype(BF16),
                                           g_ffn[0][None, :], w_router2, b_router, interpret)
    dest1, dest2, tile_expert, tile_valid, n_rows = _route_plan(route_t, counts, n)
    idx_a = jnp.concatenate([dest1, dest1 + n_rows])[None, :]
    idx_b = jnp.concatenate([dest2, dest2 + n_rows])[None, :]
    xs = _sc_scatter_rows(h2p.reshape(2 * n, SC_ROW_WORDS), idx_a, idx_b, 2 * n_rows)
    ys = _experts(xs.reshape(2, n_rows, SC_ROW_WORDS), tile_expert, tile_valid, wg_bf, wu_bf, wd_bf, interpret)
    idx_g = jnp.concatenate([dest1, dest2, dest1 + n_rows, dest2 + n_rows])[None, :]
    yab = _sc_gather_rows(ys.reshape(2 * n_rows, SC_ROW_WORDS), idx_g)
    out = _final(x1, yab.reshape(4, n, SC_ROW_WORDS), route, g_final[None, :], interpret)
    return out.reshape(batch, seq, d)


def kernel(x, g_mix, w_in, w_attn_branch, w_ret_branch, w_out, g_ffn, w_group_router, b_group_router,
           w_expert_router, b_expert_router, w_gate, w_up, w_down, g_final):
    return _forward(x, g_mix, w_in, w_attn_branch, w_ret_branch, w_out, g_ffn, w_group_router,
                    b_group_router, w_expert_router, b_expert_router, w_gate, w_up, w_down, g_final)
```

```python
import functools

import numpy as np
import jax
import jax.numpy as jnp
from jax import lax
from jax.experimental import pallas as pl
from jax.experimental.pallas import tpu as pltpu
from jax.experimental.pallas import tpu_sc as plsc

F32 = jnp.float32
BF16 = jnp.bfloat16

D_MODEL = 1024
ATTN_GROUPS = ((128, 1), (512, 4), (2048, 16))
N_GROUPS = len(ATTN_GROUPS)
ATTN_HEADS = 8
HEAD_DIM = 64
GROUP_W = ATTN_HEADS * HEAD_DIM
QKV_W = N_GROUPS * GROUP_W
RET_HEADS = 4
RET_DK = 128
RET_DV = 256
RET_CHUNK = 128
RET_TS = 1024
N_EXPERT_GROUPS = 4
EXPERTS_PER_GROUP = 8
N_EXPERTS = N_EXPERT_GROUPS * EXPERTS_PER_GROUP
EXPERT_FF = 512
EPS = 1e-6

LANES = 128
BLK = 128
SPAN = 2048
NEG = -1e30
ATTN_UNROLL = 16
ACC_PARTS = 3

COL_GATE_A = 0
COL_GATE_R = 1024
COL_VR = 2048
COL_GR = COL_VR + RET_HEADS * RET_DV
COL_QA = COL_GR + RET_HEADS * RET_DV
COL_KA = COL_QA + QKV_W
COL_VA = COL_KA + QKV_W
COL_QR = COL_VA + QKV_W
COL_KR = COL_QR + RET_HEADS * RET_DK
IN_WIDTH = COL_KR + RET_HEADS * RET_DK

PROJ_TM = 512
PROJ_TN = IN_WIDTH // 2
MXU_N = 256
MIX_TM = 1024
MIX_CHUNK = 512
ROUTE_ROWS = 8
EXP_TM = 512
SC_WINDOW = 128
SC_ROW_WORDS = 256
FIN_TM = 512
VMEM_LIMIT = 56 * 1024 * 1024


def _cparams(sem):
    return pltpu.CompilerParams(dimension_semantics=sem, vmem_limit_bytes=VMEM_LIMIT)


def _sigmoid(x):
    return 0.5 * jnp.tanh(0.5 * x) + 0.5


def _proj_kernel(x_ref, g_ref, w_ref, wg_ref, wu_ref, wd_ref, o_ref, wg_o, wu_o, wd_o):
    x = x_ref[...]
    ms = jnp.mean(x * x, axis=-1, keepdims=True)
    h = (x * lax.rsqrt(ms + EPS) * g_ref[...]).astype(BF16)
    for c in range(PROJ_TN // MXU_N):
        sl = slice(c * MXU_N, (c + 1) * MXU_N)
        o_ref[:, sl] = jnp.dot(h, w_ref[:, sl], preferred_element_type=F32).astype(o_ref.dtype)
    wg_o[...] = wg_ref[...].astype(BF16)
    wu_o[...] = wu_ref[...].astype(BF16)
    wd_o[...] = wd_ref[...].astype(BF16)


def _proj(x2, g_mix, w_in_bf16, w_gate, w_up, w_down, interpret):
    n = x2.shape[0]
    n_i = n // PROJ_TM
    steps = (IN_WIDTH // PROJ_TN) * n_i
    flat = [w.reshape(-1, w.shape[-1]) for w in (w_gate, w_up, w_down)]
    w_specs = [pl.BlockSpec((w.shape[0] // steps, w.shape[1]), lambda j, i: (j * n_i + i, 0)) for w in flat]
    outs = pl.pallas_call(
        _proj_kernel,
        grid=(IN_WIDTH // PROJ_TN, n_i),
        in_specs=[
            pl.BlockSpec((PROJ_TM, D_MODEL), lambda j, i: (i, 0)),
            pl.BlockSpec((1, D_MODEL), lambda j, i: (0, 0)),
            pl.BlockSpec((D_MODEL, PROJ_TN), lambda j, i: (0, j)),
        ] + w_specs,
        out_specs=[pl.BlockSpec((PROJ_TM, PROJ_TN), lambda j, i: (i, j))] + w_specs,
        out_shape=[jax.ShapeDtypeStruct((n, IN_WIDTH), BF16)]
        + [jax.ShapeDtypeStruct(w.shape, BF16) for w in flat],
        compiler_params=_cparams(("arbitrary", "arbitrary")),
        interpret=interpret,
        name="proj",
    )(x2, g_mix, w_in_bf16, *flat)
    return outs[0], outs[1].reshape(w_gate.shape), outs[2].reshape(w_up.shape), outs[3].reshape(w_down.shape)


def _attn_unit(q2, kk, vv, bias_a, bias_b):
    lane = lax.broadcasted_iota(jnp.int32, (BLK, LANES), 1)
    left = lane < HEAD_DIM
    zero = jnp.zeros_like(q2)
    nt = (((1,), (1,)), ((), ()))
    q_st = jnp.concatenate([jnp.where(left, q2, zero), jnp.where(left, zero, q2)], axis=0)
    s = lax.dot_general(q_st, kk, nt, preferred_element_type=F32) + jnp.concatenate([bias_a, bias_b], axis=0)
    m = jnp.max(s, axis=-1, keepdims=True)
    p = jnp.exp(s - m)
    den = jnp.sum(p, axis=-1, keepdims=True)
    o = jnp.dot(p.astype(BF16), vv, preferred_element_type=F32)
    return (jnp.where(left, o[:BLK], o[BLK:]), jnp.where(left, m[:BLK], m[BLK:]),
            jnp.where(left, den[:BLK], den[BLK:]))


def _attn_kernel(q1_ref, q2_ref, q3_ref, k1_ref, k2_ref, k3_ref, v1_ref, v2_ref, v3_ref,
                 bias_ref, o_ref, qf_ref, kvf_ref, acc_ref, *, seq):
    s_id = pl.program_id(2)
    conv_rows = 256

    @pl.when(s_id == 0)
    def _():
        for slot, ref in enumerate((k2_ref, v2_ref, k3_ref, v3_ref)):
            def body(i, c, slot=slot, ref=ref):
                r0 = pl.multiple_of(i * conv_rows, conv_rows)
                kvf_ref[slot, pl.ds(r0, conv_rows), :] = ref[pl.ds(r0, conv_rows), :].astype(F32)
                return c
            lax.fori_loop(0, seq // conv_rows, body, 0)

    for slot, ref in enumerate((q2_ref, q3_ref)):
        def body(i, c, slot=slot, ref=ref):
            r0 = pl.multiple_of(i * conv_rows, conv_rows)
            qf_ref[slot, pl.ds(r0, conv_rows), :] = ref[pl.ds(r0, conv_rows), :].astype(F32) * 0.125
            return c
        lax.fori_loop(0, SPAN // conv_rows, body, 0)

    def dilated_unit(gi, d, m, r, first):
        slot = gi - 1
        loc = BLK * m * d + r
        cur = s_id * SPAN + loc
        prev = jnp.maximum(cur - BLK * d, r)
        q2 = qf_ref[slot, pl.ds(loc, BLK, stride=d), :].astype(BF16)
        kk = jnp.concatenate([kvf_ref[2 * slot, pl.ds(prev, BLK, stride=d), :],
                              kvf_ref[2 * slot, pl.ds(cur, BLK, stride=d), :]], axis=0).astype(BF16)
        vv = jnp.concatenate([kvf_ref[2 * slot + 1, pl.ds(prev, BLK, stride=d), :],
                              kvf_ref[2 * slot + 1, pl.ds(cur, BLK, stride=d), :]], axis=0).astype(BF16)
        parts = _attn_unit(q2, kk, vv, bias_ref[gi, first, 0], bias_ref[gi, first, 1])
        for j, part in enumerate(parts):
            acc_ref[ACC_PARTS * slot + j, pl.ds(loc, BLK, stride=d), :] = part

    d3 = ATTN_GROUPS[2][1]
    first_span = jnp.where(s_id == 0, 1, 0)

    def body3(i, c):
        for u in range(ATTN_UNROLL):
            dilated_unit(2, d3, 0, i * ATTN_UNROLL + u, first_span)
        return c
    lax.fori_loop(0, d3 // ATTN_UNROLL, body3, 0)

    d2 = ATTN_GROUPS[1][1]

    blocks2 = ATTN_UNROLL // d2

    def body2(i, c):
        for mm in range(blocks2):
            m = i * blocks2 + mm
            first = jnp.where(jnp.logical_and(s_id == 0, m == 0), 1, 0)
            for r in range(d2):
                dilated_unit(1, d2, m, r, first)
        return c
    lax.fori_loop(0, SPAN // (BLK * d2 * blocks2), body2, 0)

    def dense_unit(m):
        loc = pl.multiple_of(m * BLK, BLK)
        cur = pl.multiple_of(s_id * SPAN + loc, BLK)
        prev = pl.multiple_of(jnp.maximum(cur - BLK, 0), BLK)
        first = jnp.where(cur == 0, 1, 0)
        q2 = q1_ref[pl.ds(loc, BLK), :] * 0.125
        kk = jnp.concatenate([k1_ref[pl.ds(prev, BLK), :], k1_ref[pl.ds(cur, BLK), :]], axis=0)
        vv = jnp.concatenate([v1_ref[pl.ds(prev, BLK), :], v1_ref[pl.ds(cur, BLK), :]], axis=0)
        n1, m1, d1 = _attn_unit(q2, kk, vv, bias_ref[0, first, 0], bias_ref[0, first, 1])
        n2, m2, dd2 = (acc_ref[j, pl.ds(loc, BLK), :] for j in range(ACC_PARTS))
        n3, m3, dd3 = (acc_ref[ACC_PARTS + j, pl.ds(loc, BLK), :] for j in range(ACC_PARTS))
        mx = jnp.maximum(jnp.maximum(m1, m2), m3)
        w1, w2, w3 = jnp.exp(m1 - mx), jnp.exp(m2 - mx), jnp.exp(m3 - mx)
        num = w1 * n1 + w2 * n2 + w3 * n3
        den = w1 * d1 + w2 * dd2 + w3 * dd3
        o_ref[pl.ds(loc, BLK), :] = (num / den).astype(o_ref.dtype)

    def body1(i, c):
        for u in range(ATTN_UNROLL):
            dense_unit(i * ATTN_UNROLL + u)
        return c
    lax.fori_loop(0, SPAN // (BLK * ATTN_UNROLL), body1, 0)


def _attn_bias():
    slopes = np.exp2(-8.0 * np.arange(1, ATTN_HEADS + 1, dtype=np.float64) / ATTN_HEADS)
    qi = np.arange(BLK)[:, None]
    kj = np.arange(2 * BLK)[None, :]
    rel = qi + BLK - kj
    out = np.zeros((N_GROUPS, 2, ATTN_HEADS, BLK, 2 * BLK), np.float32)
    for gi, (window, d) in enumerate(ATTN_GROUPS):
        n_back = window // d
        assert n_back == BLK
        valid = (rel >= 0) & (rel <= n_back)
        bias = -slopes[:, None, None] * (rel * d)[None].astype(np.float64)
        out[gi, 0] = np.where(valid[None], bias, NEG)
        out[gi, 1] = np.where((valid & (kj >= BLK))[None], bias, NEG)
    return jnp.asarray(out)


def _attention(proj, batch, seq, interpret):
    n = batch * seq
    spans = seq // SPAN
    n_hp = GROUP_W // LANES
    qcol = lambda g: (COL_QA + g * GROUP_W) // LANES
    kcol = lambda g: (COL_KA + g * GROUP_W) // LANES
    vcol = lambda g: (COL_VA + g * GROUP_W) // LANES
    q_specs = [pl.BlockSpec((SPAN, LANES), functools.partial(lambda b, hp, s, c: (b * spans + s, c + hp), c=qcol(g)))
               for g in range(N_GROUPS)]
    k_specs = [pl.BlockSpec((seq, LANES), functools.partial(lambda b, hp, s, c: (b, c + hp), c=kcol(g)))
               for g in range(N_GROUPS)]
    v_specs = [pl.BlockSpec((seq, LANES), functools.partial(lambda b, hp, s, c: (b, c + hp), c=vcol(g)))
               for g in range(N_GROUPS)]
    bias_spec = pl.BlockSpec((N_GROUPS, 2, 2, BLK, 2 * BLK), lambda b, hp, s: (0, 0, hp, 0, 0))
    return pl.pallas_call(
        functools.partial(_attn_kernel, seq=seq),
        grid=(batch, n_hp, spans),
        in_specs=q_specs + k_specs + v_specs + [bias_spec],
        out_specs=pl.BlockSpec((SPAN, LANES), lambda b, hp, s: (b * spans + s, hp)),
        out_shape=jax.ShapeDtypeStruct((n, GROUP_W), BF16),
        scratch_shapes=[
            pltpu.VMEM((2, SPAN, LANES), F32),
            pltpu.VMEM((4, seq, LANES), F32),
            pltpu.VMEM((2 * ACC_PARTS, SPAN, LANES), F32),
        ],
        compiler_params=_cparams(("arbitrary", "arbitrary", "arbitrary")),
        interpret=interpret,
        name="attn",
    )(*([proj] * 9), _attn_bias())


def _ret_kernel(q_ref, k_ref, v_ref, gr_ref, dec_ref, xi_ref, zeta_ref, gch_ref, o_ref, st_ref):
    @pl.when(pl.program_id(1) == 0)
    def _():
        st_ref[...] = jnp.zeros_like(st_ref)

    nt = (((1,), (1,)), ((), ()))
    scale = RET_DK ** -0.5

    for c in range(RET_TS // RET_CHUNK):
        rows = pl.ds(c * RET_CHUNK, RET_CHUNK)
        for h in range(RET_HEADS):
            kcols = slice(h * RET_DK, (h + 1) * RET_DK)
            vcols = slice(h * RET_DV, (h + 1) * RET_DV)
            qi = q_ref[rows, kcols]
            kf = k_ref[rows, kcols].astype(F32) * scale
            ki = kf.astype(BF16)
            kz_t = jnp.transpose(kf * zeta_ref[h]).astype(BF16)
            vi = v_ref[rows, vcols]
            att = lax.dot_general(qi, ki, nt, preferred_element_type=F32) * dec_ref[h]
            inner = jnp.dot(att.astype(BF16), vi, preferred_element_type=F32)
            st = st_ref[h]
            cross = jnp.dot(qi, st.astype(BF16), preferred_element_type=F32) * xi_ref[h]
            st_ref[h] = gch_ref[h] * st + jnp.dot(kz_t, vi, preferred_element_type=F32)
            y = inner + cross
            mu = jnp.mean(y, axis=-1, keepdims=True)
            yc = y - mu
            var = jnp.mean(yc * yc, axis=-1, keepdims=True)
            yn = yc * lax.rsqrt(var + EPS)
            g = gr_ref[rows, vcols].astype(F32)
            o_ref[rows, vcols] = (g * _sigmoid(g) * yn).astype(o_ref.dtype)


def _ret_tables():
    c = RET_CHUNK
    log_g = np.log1p(-np.exp2(-5.0 - np.arange(RET_HEADS, dtype=np.float64)))
    pos = np.arange(c, dtype=np.float64)
    diff = pos[:, None] - pos[None, :]
    dec = np.where(diff >= 0, np.exp(log_g[:, None, None] * np.maximum(diff, 0.0)), 0.0)
    xi = np.exp(log_g[:, None] * (pos + 1.0))[..., None] * np.ones((1, 1, RET_DV))
    zeta = np.exp(log_g[:, None] * (c - 1.0 - pos))[..., None] * np.ones((1, 1, RET_DK))
    gch = np.exp(log_g * c)[:, None, None] * np.ones((1, 1, RET_DV))
    return tuple(jnp.asarray(t, F32) for t in (dec, xi, zeta, gch))


def _retention(proj, batch, seq, interpret):
    n = batch * seq
    dec, xi, zeta, gch = _ret_tables()
    qk_w = RET_HEADS * RET_DK
    v_w = RET_HEADS * RET_DV
    nts = seq // RET_TS
    const3 = lambda b, t: (0, 0, 0)
    return pl.pallas_call(
        _ret_kernel,
        grid=(batch, nts),
        in_specs=[
            pl.BlockSpec((RET_TS, qk_w), lambda b, t: (b * nts + t, COL_QR // qk_w)),
            pl.BlockSpec((RET_TS, qk_w), lambda b, t: (b * nts + t, COL_KR // qk_w)),
            pl.BlockSpec((RET_TS, v_w), lambda b, t: (b * nts + t, COL_VR // v_w)),
            pl.BlockSpec((RET_TS, v_w), lambda b, t: (b * nts + t, COL_GR // v_w)),
            pl.BlockSpec((RET_HEADS, RET_CHUNK, RET_CHUNK), const3),
            pl.BlockSpec((RET_HEADS, RET_CHUNK, RET_DV), const3),
            pl.BlockSpec((RET_HEADS, RET_CHUNK, RET_DK), const3),
            pl.BlockSpec((RET_HEADS, 1, RET_DV), const3),
        ],
        out_specs=pl.BlockSpec((RET_TS, v_w), lambda b, t: (b * nts + t, 0)),
        out_shape=jax.ShapeDtypeStruct((n, v_w), BF16),
        scratch_shapes=[pltpu.VMEM((RET_HEADS, RET_DK, RET_DV), F32)],
        compiler_params=_cparams(("arbitrary", "arbitrary")),
        interpret=interpret,
        name="retention",
    )(proj, proj, proj, proj, dec, xi, zeta, gch)


ROUTER_OFF = N_EXPERT_GROUPS


def _pack_bf16_pair(a, b):
    hi = lax.bitcast_convert_type(a.astype(BF16).astype(F32), jnp.uint32)
    lo = lax.bitcast_convert_type(b.astype(BF16).astype(F32), jnp.uint32)
    return lax.bitcast_convert_type(hi | (lo >> 16), jnp.int32)


def _unpack_bf16_pair(w):
    u = lax.bitcast_convert_type(w, jnp.uint32)
    a = lax.bitcast_convert_type(u & jnp.uint32(0xFFFF0000), F32).astype(BF16)
    b = lax.bitcast_convert_type(u << 16, F32).astype(BF16)
    return a, b


def _pack_rows(y):
    q = D_MODEL // 4
    return (_pack_bf16_pair(y[:, 0:q], y[:, 2 * q:3 * q]), _pack_bf16_pair(y[:, q:2 * q], y[:, 3 * q:4 * q]))


def _unpack_rows(slab0, slab1):
    q0, q2 = _unpack_bf16_pair(slab0)
    q1, q3 = _unpack_bf16_pair(slab1)
    return jnp.concatenate([q0, q1, q2, q3], axis=1)


def _mix_kernel(oa_ref, or_ref, ga_ref, gr_ref, x_ref, pa_ref, pr_ref, wo_ref, gf_ref, wr_ref, br_ref,
                x1_ref, h2_ref, route_ref, route_t_ref, cnt_ref, carry_ref):
    @pl.when(pl.program_id(0) == 0)
    def _():
        carry_ref[...] = jnp.zeros_like(carry_ref)

    for c in range(MIX_TM // MIX_CHUNK):
        _mix_rows(pl.ds(c * MIX_CHUNK, MIX_CHUNK), oa_ref, or_ref, ga_ref, gr_ref, x_ref, pa_ref, pr_ref, wo_ref,
                  gf_ref, wr_ref, br_ref, x1_ref, h2_ref, route_ref, route_t_ref, cnt_ref, carry_ref)


def _mix_rows(rows, oa_ref, or_ref, ga_ref, gr_ref, x_ref, pa_ref, pr_ref, wo_ref, gf_ref, wr_ref, br_ref,
              x1_ref, h2_ref, route_ref, route_t_ref, cnt_ref, carry_ref):
    a = jnp.dot(oa_ref[rows, :], pa_ref[...], preferred_element_type=F32)
    r = jnp.dot(or_ref[rows, :], pr_ref[...], preferred_element_type=F32)
    merged = (_sigmoid(ga_ref[rows, :].astype(F32)) * a + _sigmoid(gr_ref[rows, :].astype(F32)) * r)
    x1 = x_ref[rows, :] + jnp.dot(merged.astype(BF16), wo_ref[...], preferred_element_type=F32)
    x1_ref[rows, :] = x1
    ms = jnp.mean(x1 * x1, axis=-1, keepdims=True)
    h2 = x1 * lax.rsqrt(ms + EPS) * gf_ref[...]
    h2_ref[0, rows, :], h2_ref[1, rows, :] = _pack_rows(h2)

    h_hi = h2.astype(BF16)
    h_lo = (h2 - h_hi.astype(F32)).astype(BF16)
    both = jnp.dot(h_hi, wr_ref[...], preferred_element_type=F32)
    logits = (both[:, :LANES] + both[:, LANES:]
              + jnp.dot(h_lo, wr_ref[:, :LANES], preferred_element_type=F32) + br_ref[...])
    tm = logits.shape[0]
    lane = lax.broadcasted_iota(jnp.int32, (tm, LANES), 1).astype(F32)
    big = jnp.float32(4 * LANES)
    ninf = -jnp.inf
    is_g = lane < N_EXPERT_GROUPS
    gl = jnp.where(is_g, logits, ninf)
    gmax = jnp.max(gl, axis=-1, keepdims=True)
    gsum = jnp.sum(jnp.where(is_g, jnp.exp(gl - gmax), 0.0), axis=-1, keepdims=True)
    g_val = 1.0 / gsum
    g_idx = jnp.min(jnp.where(jnp.logical_and(is_g, gl == gmax), lane, big), axis=-1, keepdims=True)
    lo = ROUTER_OFF + EXPERTS_PER_GROUP * g_idx
    in_grp = jnp.logical_and(lane >= lo, lane < lo + EXPERTS_PER_GROUP)
    el = jnp.where(in_grp, logits, ninf)
    v1 = jnp.max(el, axis=-1, keepdims=True)
    i1 = jnp.min(jnp.where(jnp.logical_and(in_grp, el == v1), lane, big), axis=-1, keepdims=True)
    rest = jnp.logical_and(in_grp, lane != i1)
    el2 = jnp.where(rest, logits, ninf)
    v2 = jnp.max(el2, axis=-1, keepdims=True)
    i2 = jnp.min(jnp.where(jnp.logical_and(rest, el2 == v2), lane, big), axis=-1, keepdims=True)
    t = jnp.exp(v2 - v1)
    w1 = g_val / (1.0 + t)
    w2 = g_val * t / (1.0 + t)

    sel = jnp.logical_or(lane == i1, lane == i2)
    sel_bf = jnp.where(sel, 1.0, 0.0).astype(BF16)
    row = lax.broadcasted_iota(jnp.int32, (tm, tm), 0)
    col = lax.broadcasted_iota(jnp.int32, (tm, tm), 1)
    tri = jnp.where(col < row, 1.0, 0.0).astype(BF16)
    before = jnp.dot(tri, sel_bf, preferred_element_type=F32) + carry_ref[...]
    r1 = jnp.sum(jnp.where(lane == i1, before, 0.0), axis=-1, keepdims=True)
    r2 = jnp.sum(jnp.where(lane == i2, before, 0.0), axis=-1, keepdims=True)
    carry = carry_ref[...] + jnp.sum(jnp.where(sel, 1.0, 0.0), axis=0, keepdims=True)
    carry_ref[...] = carry
    cnt_ref[...] = carry

    vals = (i1 - ROUTER_OFF, i2 - ROUTER_OFF, w1, w2, r1, r2)
    route = jnp.zeros((tm, LANES), F32)
    for j, v in enumerate(vals):
        route = jnp.where(lane == j, v, route)
    route_ref[rows, :] = route
    route_t_ref[:, rows] = jnp.transpose(route)[:ROUTE_ROWS, :]


def _mix(o_attn, o_ret, proj, x2, pa, pr, wo, g_ffn, w_router, b_router, interpret):
    n = x2.shape[0]
    tm = MIX_TM
    const = lambda i: (0, 0)
    return pl.pallas_call(
        _mix_kernel,
        grid=(n // tm,),
        in_specs=[
            pl.BlockSpec((tm, GROUP_W), lambda i: (i, 0)),
            pl.BlockSpec((tm, D_MODEL), lambda i: (i, 0)),
            pl.BlockSpec((tm, D_MODEL), lambda i: (i, COL_GATE_A // D_MODEL)),
            pl.BlockSpec((tm, D_MODEL), lambda i: (i, COL_GATE_R // D_MODEL)),
            pl.BlockSpec((tm, D_MODEL), lambda i: (i, 0)),
            pl.BlockSpec((GROUP_W, D_MODEL), const),
            pl.BlockSpec((D_MODEL, D_MODEL), const),
            pl.BlockSpec((D_MODEL, D_MODEL), const),
            pl.BlockSpec((1, D_MODEL), const),
            pl.BlockSpec((D_MODEL, 2 * LANES), const),
            pl.BlockSpec((1, LANES), const),
        ],
        out_specs=[
            pl.BlockSpec((tm, D_MODEL), lambda i: (i, 0)),
            pl.BlockSpec((2, tm, SC_ROW_WORDS), lambda i: (0, i, 0)),
            pl.BlockSpec((tm, LANES), lambda i: (i, 0)),
            pl.BlockSpec((ROUTE_ROWS, tm), lambda i: (0, i)),
            pl.BlockSpec((1, LANES), const),
        ],
        out_shape=[
            jax.ShapeDtypeStruct((n, D_MODEL), F32),
            jax.ShapeDtypeStruct((2, n, SC_ROW_WORDS), jnp.int32),
            jax.ShapeDtypeStruct((n, LANES), F32),
            jax.ShapeDtypeStruct((ROUTE_ROWS, n), F32),
            jax.ShapeDtypeStruct((1, LANES), F32),
        ],
        scratch_shapes=[pltpu.VMEM((1, LANES), F32)],
        compiler_params=_cparams(("arbitrary",)),
        interpret=interpret,
        name="mix_router",
    )(o_attn, o_ret, proj, proj, x2, pa, pr, wo, g_ffn, w_router, b_router)


def _expert_kernel(te_ref, tv_ref, xs_ref, wg_ref, wu_ref, wd_ref, o_ref):
    i = pl.program_id(0)

    @pl.when(tv_ref[i] != 0)
    def _():
        xs = _unpack_rows(xs_ref[0], xs_ref[1])
        a = jnp.dot(xs, wg_ref[0], preferred_element_type=F32)
        u = jnp.dot(xs, wu_ref[0], preferred_element_type=F32)
        hid = (a * _sigmoid(a) * u).astype(BF16)
        y = jnp.dot(hid, wd_ref[0], preferred_element_type=F32)
        o_ref[0], o_ref[1] = _pack_rows(y)

    @pl.when(tv_ref[i] == 0)
    def _():
        o_ref[...] = jnp.zeros_like(o_ref)


def _experts(xs, tile_expert, tile_valid, w_gate, w_up, w_down, interpret):
    p = xs.shape[1]
    n_tiles = p // EXP_TM
    grid_spec = pltpu.PrefetchScalarGridSpec(
        num_scalar_prefetch=2,
        grid=(n_tiles,),
        in_specs=[
            pl.BlockSpec((2, EXP_TM, SC_ROW_WORDS), lambda i, te, tv: (0, i, 0)),
            pl.BlockSpec((1, D_MODEL, EXPERT_FF), lambda i, te, tv: (te[i], 0, 0)),
            pl.BlockSpec((1, D_MODEL, EXPERT_FF), lambda i, te, tv: (te[i], 0, 0)),
            pl.BlockSpec((1, EXPERT_FF, D_MODEL), lambda i, te, tv: (te[i], 0, 0)),
        ],
        out_specs=pl.BlockSpec((2, EXP_TM, SC_ROW_WORDS), lambda i, te, tv: (0, i, 0)),
    )
    return pl.pallas_call(
        _expert_kernel,
        grid_spec=grid_spec,
        out_shape=jax.ShapeDtypeStruct((2, p, SC_ROW_WORDS), jnp.int32),
        compiler_params=_cparams(("arbitrary",)),
        interpret=interpret,
        name="experts",
    )(tile_expert, tile_valid, xs, w_gate, w_up, w_down)


def _final_kernel(x1_ref, yab_ref, route_ref, g_ref, o_ref):
    route = route_ref[...]
    w1 = route[:, 2:3]
    w2 = route[:, 3:4]
    ya = _unpack_rows(yab_ref[0], yab_ref[2]).astype(F32)
    yb = _unpack_rows(yab_ref[1], yab_ref[3]).astype(F32)
    x2 = x1_ref[...] + w1 * ya + w2 * yb
    ms = jnp.mean(x2 * x2, axis=-1, keepdims=True)
    o_ref[...] = x2 * lax.rsqrt(ms + EPS) * g_ref[...]


def _final(x1, yab, route, g_final, interpret):
    n = x1.shape[0]
    tm = FIN_TM
    row = lambda i: (i, 0)
    return pl.pallas_call(
        _final_kernel,
        grid=(n // tm,),
        in_specs=[
            pl.BlockSpec((tm, D_MODEL), row),
            pl.BlockSpec((4, tm, SC_ROW_WORDS), lambda i: (0, i, 0)),
            pl.BlockSpec((tm, LANES), row),
            pl.BlockSpec((1, D_MODEL), lambda i: (0, 0)),
        ],
        out_specs=pl.BlockSpec((tm, D_MODEL), row),
        out_shape=jax.ShapeDtypeStruct((n, D_MODEL), F32),
        compiler_params=_cparams(("arbitrary",)),
        interpret=interpret,
        name="combine_final",
    )(x1, yab, route, g_final)


def _permute_w_in(w_in):
    splits = np.cumsum([QKV_W, QKV_W, QKV_W, 512, 512, 1024, 1024, D_MODEL, D_MODEL])[:-1].tolist()
    qa, ka, va, qr, kr, vr, gr, gate_a, gate_r = jnp.split(w_in, splits, axis=-1)
    return jnp.concatenate([gate_a, gate_r, vr, gr, qa, ka, va, qr, kr], axis=-1).astype(BF16)


def _route_plan(route_t, counts, n):
    e1 = route_t[0].astype(jnp.int32)
    e2 = route_t[1].astype(jnp.int32)
    r1 = route_t[4].astype(jnp.int32)
    r2 = route_t[5].astype(jnp.int32)
    cnt = counts[0, ROUTER_OFF:ROUTER_OFF + N_EXPERTS].astype(jnp.int32)
    padded = ((cnt + EXP_TM - 1) // EXP_TM) * EXP_TM
    ends = jnp.cumsum(padded)
    offs = ends - padded
    lanes = jnp.arange(N_EXPERTS, dtype=jnp.int32)[None, :]
    dest1 = jnp.sum(jnp.where(e1[:, None] == lanes, offs[None, :], 0), axis=1) + r1
    dest2 = jnp.sum(jnp.where(e2[:, None] == lanes, offs[None, :], 0), axis=1) + r2
    n_rows = 2 * n + N_EXPERTS * EXP_TM
    n_tiles = n_rows // EXP_TM
    tile_start = jnp.arange(n_tiles, dtype=jnp.int32) * EXP_TM
    tile_valid = (tile_start < ends[-1]).astype(jnp.int32)
    te = jnp.sum((ends[None, :] <= tile_start[:, None]).astype(jnp.int32), axis=1)
    te_last = jnp.sum((ends[None, :] <= ends[-1] - EXP_TM).astype(jnp.int32))
    tile_expert = jnp.minimum(jnp.where(tile_valid == 1, te, te_last), N_EXPERTS - 1)
    return dest1, dest2, tile_expert, tile_valid, n_rows


def _sc_mesh():
    return plsc.VectorSubcoreMesh(core_axis_name="core", subcore_axis_name="subcore")


def _sc_scatter_rows(rows, idx_a, idx_b, n_out):
    n_in, w = rows.shape

    @functools.partial(pl.kernel, out_type=jax.ShapeDtypeStruct((n_out, w), rows.dtype), mesh=_sc_mesh(),
                       scratch_types=[], name="sc_scatter_rows")
    def scatter(x_hbm, ia_hbm, ib_hbm, o_hbm):
        def body(x_vmem, ia_vmem, ib_vmem):
            pltpu.sync_copy(x_vmem, o_hbm.at[ia_vmem.at[0]])
            pltpu.sync_copy(x_vmem, o_hbm.at[ib_vmem.at[0]])

        pltpu.emit_pipeline(
            body,
            grid=(n_in // SC_WINDOW,),
            in_specs=[pl.BlockSpec((SC_WINDOW, w), lambda i: (i, 0)),
                      pl.BlockSpec((1, SC_WINDOW), lambda i: (0, i)),
                      pl.BlockSpec((1, SC_WINDOW), lambda i: (0, i))],
            out_specs=[],
            core_axis_name=("core", "subcore"),
            dimension_semantics=(pltpu.PARALLEL,),
        )(x_hbm, ia_hbm, ib_hbm)

    return scatter(rows, idx_a, idx_b)


def _sc_gather_rows(table, idx):
    n_idx = idx.shape[1]
    w = table.shape[1]

    @functools.partial(pl.kernel, out_type=jax.ShapeDtypeStruct((n_idx, w), table.dtype), mesh=_sc_mesh(),
                       scratch_types=[], name="sc_gather_rows")
    def gather(t_hbm, i_hbm, o_hbm):
        def body(i_vmem, o_vmem):
            pltpu.sync_copy(t_hbm.at[i_vmem.at[0]], o_vmem)

        pltpu.emit_pipeline(
            body,
            grid=(n_idx // SC_WINDOW,),
            in_specs=[pl.BlockSpec((1, SC_WINDOW), lambda i: (0, i))],
            out_specs=[pl.BlockSpec((SC_WINDOW, w), lambda i: (i, 0))],
            core_axis_name=("core", "subcore"),
            dimension_semantics=(pltpu.PARALLEL,),
        )(i_hbm, o_hbm)

    return gather(table, idx)


def _forward(x, g_mix, w_in, w_attn_branch, w_ret_branch, w_out, g_ffn, w_group_router, b_group_router,
             w_expert_router, b_expert_router, w_gate, w_up, w_down, g_final, interpret=False):
    batch, seq, d = x.shape
    n = batch * seq
    x2 = x.reshape(n, d)
    proj, wg_bf, wu_bf, wd_bf = _proj(x2, g_mix[0][None, :], _permute_w_in(w_in[0]), w_gate[0], w_up[0],
                                      w_down[0], interpret)
    o_attn = _attention(proj, batch, seq, interpret)
    o_ret = _retention(proj, batch, seq, interpret)
    pad = LANES - N_EXPERT_GROUPS - N_EXPERTS
    w_router = jnp.concatenate([w_group_router[0], w_expert_router[0], jnp.zeros((d, pad), F32)], axis=-1)
    w_router_hi = w_router.astype(BF16)
    w_router_lo = (w_router - w_router_hi.astype(F32)).astype(BF16)
    w_router2 = jnp.concatenate([w_router_hi, w_router_lo], axis=-1)
    b_router = jnp.concatenate([b_group_router[0], b_expert_router[0], jnp.zeros((pad,), F32)])[None, :]
    x1, h2p, route, route_t, counts = _mix(o_attn, o_ret, proj, x2, w_attn_branch[0].astype(BF16),
                                           w_ret_branch[0].astype(BF16), w_out[0].astype(BF16),
                                           g_ffn[0][None, :], w_router2, b_router, interpret)
    dest1, dest2, tile_expert, tile_valid, n_rows = _route_plan(route_t, counts, n)
    idx_a = jnp.concatenate([dest1, dest1 + n_rows])[None, :]
    idx_b = jnp.concatenate([dest2, dest2 + n_rows])[None, :]
    xs = _sc_scatter_rows(h2p.reshape(2 * n, SC_ROW_WORDS), idx_a, idx_b, 2 * n_rows)
    ys = _experts(xs.reshape(2, n_rows, SC_ROW_WORDS), tile_expert, tile_valid, wg_bf, wu_bf, wd_bf, interpret)
    idx_g = jnp.concatenate([dest1, dest2, dest1 + n_rows, dest2 + n_rows])[None, :]
    yab = _sc_gather_rows(ys.reshape(2 * n_rows, SC_ROW_WORDS), idx_g)
    out = _final(x1, yab.reshape(4, n, SC_ROW_WORDS), route, g_final[None, :], interpret)
    return out.reshape(batch, seq, d)


def kernel(x, g_mix, w_in, w_attn_branch, w_ret_branch, w_out, g_ffn, w_group_router, b_group_router,
           w_expert_router, b_expert_router, w_gate, w_up, w_down, g_final):
    return _forward(x, g_mix, w_in, w_attn_branch, w_ret_branch, w_out, g_ffn, w_group_router,
                    b_group_router, w_expert_router, b_expert_router, w_gate, w_up, w_down, g_final)
```

```python
import functools

import numpy as np
import jax
import jax.numpy as jnp
from jax import lax
from jax.experimental import pallas as pl
from jax.experimental.pallas import tpu as pltpu
from jax.experimental.pallas import tpu_sc as plsc

F32 = jnp.float32
BF16 = jnp.bfloat16

D_MODEL = 1024
ATTN_GROUPS = ((128, 1), (512, 4), (2048, 16))
N_GROUPS = len(ATTN_GROUPS)
ATTN_HEADS = 8
HEAD_DIM = 64
GROUP_W = ATTN_HEADS * HEAD_DIM
QKV_W = N_GROUPS * GROUP_W
RET_HEADS = 4
RET_DK = 128
RET_DV = 256
RET_CHUNK = 128
RET_TS = 1024
N_EXPERT_GROUPS = 4
EXPERTS_PER_GROUP = 8
N_EXPERTS = N_EXPERT_GROUPS * EXPERTS_PER_GROUP
EXPERT_FF = 512
EPS = 1e-6

LANES = 128
BLK = 128
SPAN = 2048
NEG = -1e30
ATTN_UNROLL = 16
ACC_PARTS = 3
ATTN_STAGES = 9

COL_GATE_A = 0
COL_GATE_R = 1024
COL_VR = 2048
COL_GR = COL_VR + RET_HEADS * RET_DV
COL_QA = COL_GR + RET_HEADS * RET_DV
COL_KA = COL_QA + QKV_W
COL_VA = COL_KA + QKV_W
COL_QR = COL_VA + QKV_W
COL_KR = COL_QR + RET_HEADS * RET_DK
IN_WIDTH = COL_KR + RET_HEADS * RET_DK

PROJ_TM = 512
PROJ_TN = IN_WIDTH // 2
MXU_N = 256
MIX_TM = 1024
MIX_CHUNK = 512
ROUTE_ROWS = 8
EXP_TM = 512
SC_WINDOW = 128
SC_ROW_WORDS = 256
FIN_TM = 512
VMEM_LIMIT = 56 * 1024 * 1024


def _cparams(sem):
    return pltpu.CompilerParams(dimension_semantics=sem, vmem_limit_bytes=VMEM_LIMIT)


def _sigmoid(x):
    return 0.5 * jnp.tanh(0.5 * x) + 0.5


def _proj_kernel(x_ref, g_ref, w_ref, wg_ref, wu_ref, wd_ref, o_ref, wg_o, wu_o, wd_o):
    x = x_ref[...]
    ms = jnp.mean(x * x, axis=-1, keepdims=True)
    h = (x * lax.rsqrt(ms + EPS) * g_ref[...]).astype(BF16)
    for c in range(PROJ_TN // MXU_N):
        sl = slice(c * MXU_N, (c + 1) * MXU_N)
        o_ref[:, sl] = jnp.dot(h, w_ref[:, sl], preferred_element_type=F32).astype(o_ref.dtype)
    wg_o[...] = wg_ref[...].astype(BF16)
    wu_o[...] = wu_ref[...].astype(BF16)
    wd_o[...] = wd_ref[...].astype(BF16)


def _proj(x2, g_mix, w_in_bf16, w_gate, w_up, w_down, interpret):
    n = x2.shape[0]
    n_i = n // PROJ_TM
    steps = (IN_WIDTH // PROJ_TN) * n_i
    flat = [w.reshape(-1, w.shape[-1]) for w in (w_gate, w_up, w_down)]
    w_specs = [pl.BlockSpec((w.shape[0] // steps, w.shape[1]), lambda j, i: (j * n_i + i, 0)) for w in flat]
    outs = pl.pallas_call(
        _proj_kernel,
        grid=(IN_WIDTH // PROJ_TN, n_i),
        in_specs=[
            pl.BlockSpec((PROJ_TM, D_MODEL), lambda j, i: (i, 0)),
            pl.BlockSpec((1, D_MODEL), lambda j, i: (0, 0)),
            pl.BlockSpec((D_MODEL, PROJ_TN), lambda j, i: (0, j)),
        ] + w_specs,
        out_specs=[pl.BlockSpec((PROJ_TM, PROJ_TN), lambda j, i: (i, j))] + w_specs,
        out_shape=[jax.ShapeDtypeStruct((n, IN_WIDTH), BF16)]
        + [jax.ShapeDtypeStruct(w.shape, BF16) for w in flat],
        compiler_params=_cparams(("arbitrary", "arbitrary")),
        interpret=interpret,
        name="proj",
    )(x2, g_mix, w_in_bf16, *flat)
    return outs[0], outs[1].reshape(w_gate.shape), outs[2].reshape(w_up.shape), outs[3].reshape(w_down.shape)


def _attn_unit(q2, kk, vv, bias_a, bias_b):
    lane = lax.broadcasted_iota(jnp.int32, (BLK, LANES), 1)
    left = lane < HEAD_DIM
    zero = jnp.zeros_like(q2)
    nt = (((1,), (1,)), ((), ()))
    q_st = jnp.concatenate([jnp.where(left, q2, zero), jnp.where(left, zero, q2)], axis=0)
    s = lax.dot_general(q_st, kk, nt, preferred_element_type=F32) + jnp.concatenate([bias_a, bias_b], axis=0)
    m = jnp.max(s, axis=-1, keepdims=True)
    p = jnp.exp(s - m)
    den = jnp.sum(p, axis=-1, keepdims=True)
    o = jnp.dot(p.astype(BF16), vv, preferred_element_type=F32)
    return (jnp.where(left, o[:BLK], o[BLK:]), jnp.where(left, m[:BLK], m[BLK:]),
            jnp.where(left, den[:BLK], den[BLK:]))


def _attn_kernel(q1_ref, q2_ref, q3_ref, k1_ref, k2_ref, k3_ref, v1_ref, v2_ref, v3_ref,
                 bias_ref, o_ref, tmp_ref, qd_ref, kvd_ref, acc_ref, *, seq):
    s_id = pl.program_id(2)
    step = ATTN_GROUPS[1][1]

    def deinterleave(src_ref, src_row0, stage, dst_ref, dst_slot, res_pitch, dst_off, d, scale):
        chunk = min(256, SPAN // d)
        quarter = SPAN // step
        for c0 in range(0, SPAN, 256):
            x = src_ref[pl.ds(pl.multiple_of(src_row0 + c0, 256), 256), :].astype(F32)
            tmp_ref[stage, c0:c0 + 256, :] = x if scale == 1.0 else x * scale

        def strided_pass(src_stage, base, length, put):
            for b in range(step):
                for j0 in range(0, length // step, chunk):
                    put(b, j0, tmp_ref[src_stage, pl.ds(base + b + step * j0, chunk, stride=step), :])

        def put_out(res, j0, rows):
            row0 = res * res_pitch + j0 + dst_off
            if not isinstance(row0, int):
                row0 = pl.multiple_of(row0, chunk)
            dst_ref[dst_slot, pl.ds(row0, chunk), :] = rows.astype(BF16)

        if d == step:
            strided_pass(stage, 0, SPAN, put_out)
        else:
            def put_mid(b, j0, rows):
                tmp_ref[stage + 1, b * quarter + j0:b * quarter + j0 + chunk, :] = rows
            strided_pass(stage, 0, SPAN, put_mid)
            for b in range(step):
                strided_pass(stage + 1, b * quarter, quarter,
                             functools.partial(lambda a, j0, rows, b: put_out(a * step + b, j0, rows), b=b))

    span_row0 = pl.multiple_of(s_id * SPAN, SPAN)
    stage = 0
    for gi, (q_ref, k_ref, v_ref) in ((1, (q2_ref, k2_ref, v2_ref)), (2, (q3_ref, k3_ref, v3_ref))):
        d = ATTN_GROUPS[gi][1]
        n_stage = 1 if d == step else 2
        deinterleave(q_ref, 0, stage, qd_ref, gi - 1, SPAN // d, 0, d, 0.125)
        deinterleave(k_ref, span_row0, stage + n_stage, kvd_ref, 2 * (gi - 1), seq // d, s_id * (SPAN // d), d, 1.0)
        deinterleave(v_ref, span_row0, stage + 2 * n_stage, kvd_ref, 2 * (gi - 1) + 1, seq // d,
                     s_id * (SPAN // d), d, 1.0)
        stage += 3 * n_stage

    def dilated_unit(gi, d, m, r, first):
        slot = gi - 1
        loc = BLK * m * d + r
        q_row = r * (SPAN // d) + BLK * m
        cur = pl.multiple_of(r * (seq // d) + s_id * (SPAN // d) + BLK * m, BLK)
        prev = pl.multiple_of(jnp.where(first == 1, cur, cur - BLK), BLK)
        q2 = qd_ref[slot, pl.ds(q_row, BLK), :]
        kk = jnp.concatenate([kvd_ref[2 * slot, pl.ds(prev, BLK), :], kvd_ref[2 * slot, pl.ds(cur, BLK), :]], axis=0)
        vv = jnp.concatenate([kvd_ref[2 * slot + 1, pl.ds(prev, BLK), :],
                              kvd_ref[2 * slot + 1, pl.ds(cur, BLK), :]], axis=0)
        parts = _attn_unit(q2, kk, vv, bias_ref[gi, first, 0], bias_ref[gi, first, 1])
        for j, part in enumerate(parts):
            acc_ref[ACC_PARTS * slot + j, pl.ds(loc, BLK, stride=d), :] = part

    d3 = ATTN_GROUPS[2][1]
    first_span = jnp.where(s_id == 0, 1, 0)

    def body3(i, c):
        for u in range(ATTN_UNROLL):
            dilated_unit(2, d3, 0, i * ATTN_UNROLL + u, first_span)
        return c
    lax.fori_loop(0, d3 // ATTN_UNROLL, body3, 0)

    d2 = ATTN_GROUPS[1][1]

    blocks2 = ATTN_UNROLL // d2

    def body2(i, c):
        for mm in range(blocks2):
            m = i * blocks2 + mm
            first = jnp.where(jnp.logical_and(s_id == 0, m == 0), 1, 0)
            for r in range(d2):
                dilated_unit(1, d2, m, r, first)
        return c
    lax.fori_loop(0, SPAN // (BLK * d2 * blocks2), body2, 0)

    def dense_unit(m):
        loc = pl.multiple_of(m * BLK, BLK)
        cur = pl.multiple_of(s_id * SPAN + loc, BLK)
        prev = pl.multiple_of(jnp.maximum(cur - BLK, 0), BLK)
        first = jnp.where(cur == 0, 1, 0)
        q2 = q1_ref[pl.ds(loc, BLK), :] * 0.125
        kk = jnp.concatenate([k1_ref[pl.ds(prev, BLK), :], k1_ref[pl.ds(cur, BLK), :]], axis=0)
        vv = jnp.concatenate([v1_ref[pl.ds(prev, BLK), :], v1_ref[pl.ds(cur, BLK), :]], axis=0)
        n1, m1, d1 = _attn_unit(q2, kk, vv, bias_ref[0, first, 0], bias_ref[0, first, 1])
        n2, m2, dd2 = (acc_ref[j, pl.ds(loc, BLK), :] for j in range(ACC_PARTS))
        n3, m3, dd3 = (acc_ref[ACC_PARTS + j, pl.ds(loc, BLK), :] for j in range(ACC_PARTS))
        mx = jnp.maximum(jnp.maximum(m1, m2), m3)
        w1, w2, w3 = jnp.exp(m1 - mx), jnp.exp(m2 - mx), jnp.exp(m3 - mx)
        num = w1 * n1 + w2 * n2 + w3 * n3
        den = w1 * d1 + w2 * dd2 + w3 * dd3
        o_ref[pl.ds(loc, BLK), :] = (num / den).astype(o_ref.dtype)

    def body1(i, c):
        for u in range(ATTN_UNROLL):
            dense_unit(i * ATTN_UNROLL + u)
        return c
    lax.fori_loop(0, SPAN // (BLK * ATTN_UNROLL), body1, 0)


def _attn_bias():
    slopes = np.exp2(-8.0 * np.arange(1, ATTN_HEADS + 1, dtype=np.float64) / ATTN_HEADS)
    qi = np.arange(BLK)[:, None]
    kj = np.arange(2 * BLK)[None, :]
    rel = qi + BLK - kj
    out = np.zeros((N_GROUPS, 2, ATTN_HEADS, BLK, 2 * BLK), np.float32)
    for gi, (window, d) in enumerate(ATTN_GROUPS):
        n_back = window // d
        assert n_back == BLK
        valid = (rel >= 0) & (rel <= n_back)
        bias = -slopes[:, None, None] * (rel * d)[None].astype(np.float64)
        out[gi, 0] = np.where(valid[None], bias, NEG)
        out[gi, 1] = np.where((valid & (kj >= BLK))[None], bias, NEG)
    return jnp.asarray(out)


def _attention(proj, batch, seq, interpret):
    n = batch * seq
    spans = seq // SPAN
    n_hp = GROUP_W // LANES
    qcol = lambda g: (COL_QA + g * GROUP_W) // LANES
    kcol = lambda g: (COL_KA + g * GROUP_W) // LANES
    vcol = lambda g: (COL_VA + g * GROUP_W) // LANES
    q_specs = [pl.BlockSpec((SPAN, LANES), functools.partial(lambda b, hp, s, c: (b * spans + s, c + hp), c=qcol(g)))
               for g in range(N_GROUPS)]
    k_specs = [pl.BlockSpec((seq, LANES), functools.partial(lambda b, hp, s, c: (b, c + hp), c=kcol(g)))
               for g in range(N_GROUPS)]
    v_specs = [pl.BlockSpec((seq, LANES), functools.partial(lambda b, hp, s, c: (b, c + hp), c=vcol(g)))
               for g in range(N_GROUPS)]
    bias_spec = pl.BlockSpec((N_GROUPS, 2, 2, BLK, 2 * BLK), lambda b, hp, s: (0, 0, hp, 0, 0))
    return pl.pallas_call(
        functools.partial(_attn_kernel, seq=seq),
        grid=(batch, n_hp, spans),
        in_specs=q_specs + k_specs + v_specs + [bias_spec],
        out_specs=pl.BlockSpec((SPAN, LANES), lambda b, hp, s: (b * spans + s, hp)),
        out_shape=jax.ShapeDtypeStruct((n, GROUP_W), BF16),
        scratch_shapes=[
            pltpu.VMEM((ATTN_STAGES, SPAN, LANES), F32),
            pltpu.VMEM((2, SPAN, LANES), BF16),
            pltpu.VMEM((4, seq, LANES), BF16),
            pltpu.VMEM((2 * ACC_PARTS, SPAN, LANES), F32),
        ],
        compiler_params=_cparams(("arbitrary", "arbitrary", "arbitrary")),
        interpret=interpret,
        name="attn",
    )(*([proj] * 9), _attn_bias())


def _ret_kernel(q_ref, k_ref, v_ref, gr_ref, dec_ref, xi_ref, zeta_ref, gch_ref, o_ref, st_ref):
    @pl.when(pl.program_id(1) == 0)
    def _():
        st_ref[...] = jnp.zeros_like(st_ref)

    nt = (((1,), (1,)), ((), ()))
    scale = RET_DK ** -0.5

    for c in range(RET_TS // RET_CHUNK):
        rows = pl.ds(c * RET_CHUNK, RET_CHUNK)
        for h in range(RET_HEADS):
            kcols = slice(h * RET_DK, (h + 1) * RET_DK)
            vcols = slice(h * RET_DV, (h + 1) * RET_DV)
            qi = q_ref[rows, kcols]
            kf = k_ref[rows, kcols].astype(F32) * scale
            ki = kf.astype(BF16)
            kz_t = jnp.transpose(kf * zeta_ref[h]).astype(BF16)
            vi = v_ref[rows, vcols]
            att = lax.dot_general(qi, ki, nt, preferred_element_type=F32) * dec_ref[h]
            inner = jnp.dot(att.astype(BF16), vi, preferred_element_type=F32)
            st = st_ref[h]
            cross = jnp.dot(qi, st.astype(BF16), preferred_element_type=F32) * xi_ref[h]
            st_ref[h] = gch_ref[h] * st + jnp.dot(kz_t, vi, preferred_element_type=F32)
            y = inner + cross
            mu = jnp.mean(y, axis=-1, keepdims=True)
            yc = y - mu
            var = jnp.mean(yc * yc, axis=-1, keepdims=True)
            yn = yc * lax.rsqrt(var + EPS)
            g = gr_ref[rows, vcols].astype(F32)
            o_ref[rows, vcols] = (g * _sigmoid(g) * yn).astype(o_ref.dtype)


def _ret_tables():
    c = RET_CHUNK
    log_g = np.log1p(-np.exp2(-5.0 - np.arange(RET_HEADS, dtype=np.float64)))
    pos = np.arange(c, dtype=np.float64)
    diff = pos[:, None] - pos[None, :]
    dec = np.where(diff >= 0, np.exp(log_g[:, None, None] * np.maximum(diff, 0.0)), 0.0)
    xi = np.exp(log_g[:, None] * (pos + 1.0))[..., None] * np.ones((1, 1, RET_DV))
    zeta = np.exp(log_g[:, None] * (c - 1.0 - pos))[..., None] * np.ones((1, 1, RET_DK))
    gch = np.exp(log_g * c)[:, None, None] * np.ones((1, 1, RET_DV))
    return tuple(jnp.asarray(t, F32) for t in (dec, xi, zeta, gch))


def _retention(proj, batch, seq, interpret):
    n = batch * seq
    dec, xi, zeta, gch = _ret_tables()
    qk_w = RET_HEADS * RET_DK
    v_w = RET_HEADS * RET_DV
    nts = seq // RET_TS
    const3 = lambda b, t: (0, 0, 0)
    return pl.pallas_call(
        _ret_kernel,
        grid=(batch, nts),
        in_specs=[
            pl.BlockSpec((RET_TS, qk_w), lambda b, t: (b * nts + t, COL_QR // qk_w)),
            pl.BlockSpec((RET_TS, qk_w), lambda b, t: (b * nts + t, COL_KR // qk_w)),
            pl.BlockSpec((RET_TS, v_w), lambda b, t: (b * nts + t, COL_VR // v_w)),
            pl.BlockSpec((RET_TS, v_w), lambda b, t: (b * nts + t, COL_GR // v_w)),
            pl.BlockSpec((RET_HEADS, RET_CHUNK, RET_CHUNK), const3),
            pl.BlockSpec((RET_HEADS, RET_CHUNK, RET_DV), const3),
            pl.BlockSpec((RET_HEADS, RET_CHUNK, RET_DK), const3),
            pl.BlockSpec((RET_HEADS, 1, RET_DV), const3),
        ],
        out_specs=pl.BlockSpec((RET_TS, v_w), lambda b, t: (b * nts + t, 0)),
        out_shape=jax.ShapeDtypeStruct((n, v_w), BF16),
        scratch_shapes=[pltpu.VMEM((RET_HEADS, RET_DK, RET_DV), F32)],
        compiler_params=_cparams(("arbitrary", "arbitrary")),
        interpret=interpret,
        name="retention",
    )(proj, proj, proj, proj, dec, xi, zeta, gch)


ROUTER_OFF = N_EXPERT_GROUPS


def _pack_bf16_pair(a, b):
    hi = lax.bitcast_convert_type(a.astype(BF16).astype(F32), jnp.uint32)
    lo = lax.bitcast_convert_type(b.astype(BF16).astype(F32), jnp.uint32)
    return lax.bitcast_convert_type(hi | (lo >> 16), jnp.int32)


def _unpack_bf16_pair(w):
    u = lax.bitcast_convert_type(w, jnp.uint32)
    a = lax.bitcast_convert_type(u & jnp.uint32(0xFFFF0000), F32).astype(BF16)
    b = lax.bitcast_convert_type(u << 16, F32).astype(BF16)
    return a, b


def _pack_rows(y):
    q = D_MODEL // 4
    return (_pack_bf16_pair(y[:, 0:q], y[:, 2 * q:3 * q]), _pack_bf16_pair(y[:, q:2 * q], y[:, 3 * q:4 * q]))


def _unpack_rows(slab0, slab1):
    q0, q2 = _unpack_bf16_pair(slab0)
    q1, q3 = _unpack_bf16_pair(slab1)
    return jnp.concatenate([q0, q1, q2, q3], axis=1)


def _mix_kernel(oa_ref, or_ref, ga_ref, gr_ref, x_ref, pa_ref, pr_ref, wo_ref, gf_ref, wr_ref, br_ref,
                x1_ref, h2_ref, route_ref, route_t_ref, cnt_ref, carry_ref):
    @pl.when(pl.program_id(0) == 0)
    def _():
        carry_ref[...] = jnp.zeros_like(carry_ref)

    for c in range(MIX_TM // MIX_CHUNK):
        _mix_rows(pl.ds(c * MIX_CHUNK, MIX_CHUNK), oa_ref, or_ref, ga_ref, gr_ref, x_ref, pa_ref, pr_ref, wo_ref,
                  gf_ref, wr_ref, br_ref, x1_ref, h2_ref, route_ref, route_t_ref, cnt_ref, carry_ref)


def _mix_rows(rows, oa_ref, or_ref, ga_ref, gr_ref, x_ref, pa_ref, pr_ref, wo_ref, gf_ref, wr_ref, br_ref,
              x1_ref, h2_ref, route_ref, route_t_ref, cnt_ref, carry_ref):
    a = jnp.dot(oa_ref[rows, :], pa_ref[...], preferred_element_type=F32)
    r = jnp.dot(or_ref[rows, :], pr_ref[...], preferred_element_type=F32)
    merged = (_sigmoid(ga_ref[rows, :].astype(F32)) * a + _sigmoid(gr_ref[rows, :].astype(F32)) * r)
    x1 = x_ref[rows, :] + jnp.dot(merged.astype(BF16), wo_ref[...], preferred_element_type=F32)
    x1_ref[rows, :] = x1
    ms = jnp.mean(x1 * x1, axis=-1, keepdims=True)
    h2 = x1 * lax.rsqrt(ms + EPS) * gf_ref[...]
    h2_ref[0, rows, :], h2_ref[1, rows, :] = _pack_rows(h2)

    h_hi = h2.astype(BF16)
    h_lo = (h2 - h_hi.astype(F32)).astype(BF16)
    both = jnp.dot(h_hi, wr_ref[...], preferred_element_type=F32)
    logits = (both[:, :LANES] + both[:, LANES:]
              + jnp.dot(h_lo, wr_ref[:, :LANES], preferred_element_type=F32) + br_ref[...])
    tm = logits.shape[0]
    lane = lax.broadcasted_iota(jnp.int32, (tm, LANES), 1).astype(F32)
    big = jnp.float32(4 * LANES)
    ninf = -jnp.inf
    is_g = lane < N_EXPERT_GROUPS
    gl = jnp.where(is_g, logits, ninf)
    gmax = jnp.max(gl, axis=-1, keepdims=True)
    gsum = jnp.sum(jnp.where(is_g, jnp.exp(gl - gmax), 0.0), axis=-1, keepdims=True)
    g_val = 1.0 / gsum
    g_idx = jnp.min(jnp.where(jnp.logical_and(is_g, gl == gmax), lane, big), axis=-1, keepdims=True)
    lo = ROUTER_OFF + EXPERTS_PER_GROUP * g_idx
    in_grp = jnp.logical_and(lane >= lo, lane < lo + EXPERTS_PER_GROUP)
    el = jnp.where(in_grp, logits, ninf)
    v1 = jnp.max(el, axis=-1, keepdims=True)
    i1 = jnp.min(jnp.where(jnp.logical_and(in_grp, el == v1), lane, big), axis=-1, keepdims=True)
    rest = jnp.logical_and(in_grp, lane != i1)
    el2 = jnp.where(rest, logits, ninf)
    v2 = jnp.max(el2, axis=-1, keepdims=True)
    i2 = jnp.min(jnp.where(jnp.logical_and(rest, el2 == v2), lane, big), axis=-1, keepdims=True)
    t = jnp.exp(v2 - v1)
    w1 = g_val / (1.0 + t)
    w2 = g_val * t / (1.0 + t)

    sel = jnp.logical_or(lane == i1, lane == i2)
    sel_bf = jnp.where(sel, 1.0, 0.0).astype(BF16)
    row = lax.broadcasted_iota(jnp.int32, (tm, tm), 0)
    col = lax.broadcasted_iota(jnp.int32, (tm, tm), 1)
    tri = jnp.where(col < row, 1.0, 0.0).astype(BF16)
    before = jnp.dot(tri, sel_bf, preferred_element_type=F32) + carry_ref[...]
    r1 = jnp.sum(jnp.where(lane == i1, before, 0.0), axis=-1, keepdims=True)
    r2 = jnp.sum(jnp.where(lane == i2, before, 0.0), axis=-1, keepdims=True)
    carry = carry_ref[...] + jnp.sum(jnp.where(sel, 1.0, 0.0), axis=0, keepdims=True)
    carry_ref[...] = carry
    cnt_ref[...] = carry

    vals = (i1 - ROUTER_OFF, i2 - ROUTER_OFF, w1, w2, r1, r2)
    route = jnp.zeros((tm, LANES), F32)
    for j, v in enumerate(vals):
        route = jnp.where(lane == j, v, route)
    route_ref[rows, :] = route
    route_t_ref[:, rows] = jnp.transpose(route)[:ROUTE_ROWS, :]


def _mix(o_attn, o_ret, proj, x2, pa, pr, wo, g_ffn, w_router, b_router, interpret):
    n = x2.shape[0]
    tm = MIX_TM
    const = lambda i: (0, 0)
    return pl.pallas_call(
        _mix_kernel,
        grid=(n // tm,),
        in_specs=[
            pl.BlockSpec((tm, GROUP_W), lambda i: (i, 0)),
            pl.BlockSpec((tm, D_MODEL), lambda i: (i, 0)),
            pl.BlockSpec((tm, D_MODEL), lambda i: (i, COL_GATE_A // D_MODEL)),
            pl.BlockSpec((tm, D_MODEL), lambda i: (i, COL_GATE_R // D_MODEL)),
            pl.BlockSpec((tm, D_MODEL), lambda i: (i, 0)),
            pl.BlockSpec((GROUP_W, D_MODEL), const),
            pl.BlockSpec((D_MODEL, D_MODEL), const),
            pl.BlockSpec((D_MODEL, D_MODEL), const),
            pl.BlockSpec((1, D_MODEL), const),
            pl.BlockSpec((D_MODEL, 2 * LANES), const),
            pl.BlockSpec((1, LANES), const),
        ],
        out_specs=[
            pl.BlockSpec((tm, D_MODEL), lambda i: (i, 0)),
            pl.BlockSpec((2, tm, SC_ROW_WORDS), lambda i: (0, i, 0)),
            pl.BlockSpec((tm, LANES), lambda i: (i, 0)),
            pl.BlockSpec((ROUTE_ROWS, tm), lambda i: (0, i)),
            pl.BlockSpec((1, LANES), const),
        ],
        out_shape=[
            jax.ShapeDtypeStruct((n, D_MODEL), F32),
            jax.ShapeDtypeStruct((2, n, SC_ROW_WORDS), jnp.int32),
            jax.ShapeDtypeStruct((n, LANES), F32),
            jax.ShapeDtypeStruct((ROUTE_ROWS, n), F32),
            jax.ShapeDtypeStruct((1, LANES), F32),
        ],
        scratch_shapes=[pltpu.VMEM((1, LANES), F32)],
        compiler_params=_cparams(("arbitrary",)),
        interpret=interpret,
        name="mix_router",
    )(o_attn, o_ret, proj, proj, x2, pa, pr, wo, g_ffn, w_router, b_router)


def _expert_kernel(te_ref, tv_ref, xs_ref, wg_ref, wu_ref, wd_ref, o_ref):
    i = pl.program_id(0)

    @pl.when(tv_ref[i] != 0)
    def _():
        xs = _unpack_rows(xs_ref[0], xs_ref[1])
        a = jnp.dot(xs, wg_ref[0], preferred_element_type=F32)
        u = jnp.dot(xs, wu_ref[0], preferred_element_type=F32)
        hid = (a * _sigmoid(a) * u).astype(BF16)
        y = jnp.dot(hid, wd_ref[0], preferred_element_type=F32)
        o_ref[0], o_ref[1] = _pack_rows(y)

    @pl.when(tv_ref[i] == 0)
    def _():
        o_ref[...] = jnp.zeros_like(o_ref)


def _experts(xs, tile_expert, tile_valid, w_gate, w_up, w_down, interpret):
    p = xs.shape[1]
    n_tiles = p // EXP_TM
    grid_spec = pltpu.PrefetchScalarGridSpec(
        num_scalar_prefetch=2,
        grid=(n_tiles,),
        in_specs=[
            pl.BlockSpec((2, EXP_TM, SC_ROW_WORDS), lambda i, te, tv: (0, i, 0)),
            pl.BlockSpec((1, D_MODEL, EXPERT_FF), lambda i, te, tv: (te[i], 0, 0)),
            pl.BlockSpec((1, D_MODEL, EXPERT_FF), lambda i, te, tv: (te[i], 0, 0)),
            pl.BlockSpec((1, EXPERT_FF, D_MODEL), lambda i, te, tv: (te[i], 0, 0)),
        ],
        out_specs=pl.BlockSpec((2, EXP_TM, SC_ROW_WORDS), lambda i, te, tv: (0, i, 0)),
    )
    return pl.pallas_call(
        _expert_kernel,
        grid_spec=grid_spec,
        out_shape=jax.ShapeDtypeStruct((2, p, SC_ROW_WORDS), jnp.int32),
        compiler_params=_cparams(("arbitrary",)),
        interpret=interpret,
        name="experts",
    )(tile_expert, tile_valid, xs, w_gate, w_up, w_down)


def _final_kernel(x1_ref, yab_ref, route_ref, g_ref, o_ref):
    route = route_ref[...]
    w1 = route[:, 2:3]
    w2 = route[:, 3:4]
    ya = _unpack_rows(yab_ref[0], yab_ref[2]).astype(F32)
    yb = _unpack_rows(yab_ref[1], yab_ref[3]).astype(F32)
    x2 = x1_ref[...] + w1 * ya + w2 * yb
    ms = jnp.mean(x2 * x2, axis=-1, keepdims=True)
    o_ref[...] = x2 * lax.rsqrt(ms + EPS) * g_ref[...]


def _final(x1, yab, route, g_final, interpret):
    n = x1.shape[0]
    tm = FIN_TM
    row = lambda i: (i, 0)
    return pl.pallas_call(
        _final_kernel,
        grid=(n // tm,),
        in_specs=[
            pl.BlockSpec((tm, D_MODEL), row),
            pl.BlockSpec((4, tm, SC_ROW_WORDS), lambda i: (0, i, 0)),
            pl.BlockSpec((tm, LANES), row),
            pl.BlockSpec((1, D_MODEL), lambda i: (0, 0)),
        ],
        out_specs=pl.BlockSpec((tm, D_MODEL), row),
        out_shape=jax.ShapeDtypeStruct((n, D_MODEL), F32),
        compiler_params=_cparams(("arbitrary",)),
        interpret=interpret,
        name="combine_final",
    )(x1, yab, route, g_final)


def _permute_w_in(w_in):
    splits = np.cumsum([QKV_W, QKV_W, QKV_W, 512, 512, 1024, 1024, D_MODEL, D_MODEL])[:-1].tolist()
    qa, ka, va, qr, kr, vr, gr, gate_a, gate_r = jnp.split(w_in, splits, axis=-1)
    return jnp.concatenate([gate_a, gate_r, vr, gr, qa, ka, va, qr, kr], axis=-1).astype(BF16)


def _route_plan(route_t, counts, n):
    e1 = route_t[0].astype(jnp.int32)
    e2 = route_t[1].astype(jnp.int32)
    r1 = route_t[4].astype(jnp.int32)
    r2 = route_t[5].astype(jnp.int32)
    cnt = counts[0, ROUTER_OFF:ROUTER_OFF + N_EXPERTS].astype(jnp.int32)
    padded = ((cnt + EXP_TM - 1) // EXP_TM) * EXP_TM
    ends = jnp.cumsum(padded)
    offs = ends - padded
    lanes = jnp.arange(N_EXPERTS, dtype=jnp.int32)[None, :]
    dest1 = jnp.sum(jnp.where(e1[:, None] == lanes, offs[None, :], 0), axis=1) + r1
    dest2 = jnp.sum(jnp.where(e2[:, None] == lanes, offs[None, :], 0), axis=1) + r2
    n_rows = 2 * n + N_EXPERTS * EXP_TM
    n_tiles = n_rows // EXP_TM
    tile_start = jnp.arange(n_tiles, dtype=jnp.int32) * EXP_TM
    tile_valid = (tile_start < ends[-1]).astype(jnp.int32)
    te = jnp.sum((ends[None, :] <= tile_start[:, None]).astype(jnp.int32), axis=1)
    te_last = jnp.sum((ends[None, :] <= ends[-1] - EXP_TM).astype(jnp.int32))
    tile_expert = jnp.minimum(jnp.where(tile_valid == 1, te, te_last), N_EXPERTS - 1)
    return dest1, dest2, tile_expert, tile_valid, n_rows


def _sc_mesh():
    return plsc.VectorSubcoreMesh(core_axis_name="core", subcore_axis_name="subcore")


def _sc_scatter_rows(rows, idx_a, idx_b, n_out):
    n_in, w = rows.shape

    @functools.partial(pl.kernel, out_type=jax.ShapeDtypeStruct((n_out, w), rows.dtype), mesh=_sc_mesh(),
                       scratch_types=[], name="sc_scatter_rows")
    def scatter(x_hbm, ia_hbm, ib_hbm, o_hbm):
        def body(x_vmem, ia_vmem, ib_vmem):
            pltpu.sync_copy(x_vmem, o_hbm.at[ia_vmem.at[0]])
            pltpu.sync_copy(x_vmem, o_hbm.at[ib_vmem.at[0]])

        pltpu.emit_pipeline(
            body,
            grid=(n_in // SC_WINDOW,),
            in_specs=[pl.BlockSpec((SC_WINDOW, w), lambda i: (i, 0)),
                      pl.BlockSpec((1, SC_WINDOW), lambda i: (0, i)),
                      pl.BlockSpec((1, SC_WINDOW), lambda i: (0, i))],
            out_specs=[],
            core_axis_name=("core", "subcore"),
            dimension_semantics=(pltpu.PARALLEL,),
        )(x_hbm, ia_hbm, ib_hbm)

    return scatter(rows, idx_a, idx_b)


def _sc_gather_rows(table, idx):
    n_idx = idx.shape[1]
    w = table.shape[1]

    @functools.partial(pl.kernel, out_type=jax.ShapeDtypeStruct((n_idx, w), table.dtype), mesh=_sc_mesh(),
                       scratch_types=[], name="sc_gather_rows")
    def gather(t_hbm, i_hbm, o_hbm):
        def body(i_vmem, o_vmem):
            pltpu.sync_copy(t_hbm.at[i_vmem.at[0]], o_vmem)

        pltpu.emit_pipeline(
            body,
            grid=(n_idx // SC_WINDOW,),
            in_specs=[pl.BlockSpec((1, SC_WINDOW), lambda i: (0, i))],
            out_specs=[pl.BlockSpec((SC_WINDOW, w), lambda i: (i, 0))],
            core_axis_name=("core", "subcore"),
            dimension_semantics=(pltpu.PARALLEL,),
        )(i_hbm, o_hbm)

    return gather(table, idx)


def _forward(x, g_mix, w_in, w_attn_branch, w_ret_branch, w_out, g_ffn, w_group_router, b_group_router,
             w_expert_router, b_expert_router, w_gate, w_up, w_down, g_final, interpret=False):
    batch, seq, d = x.shape
    n = batch * seq
    x2 = x.reshape(n, d)
    proj, wg_bf, wu_bf, wd_bf = _proj(x2, g_mix[0][None, :], _permute_w_in(w_in[0]), w_gate[0], w_up[0],
                                      w_down[0], interpret)
    o_attn = _attention(proj, batch, seq, interpret)
    o_ret = _retention(proj, batch, seq, interpret)
    pad = LANES - N_EXPERT_GROUPS - N_EXPERTS
    w_router = jnp.concatenate([w_group_router[0], w_expert_router[0], jnp.zeros((d, pad), F32)], axis=-1)
    w_router_hi = w_router.astype(BF16)
    w_router_lo = (w_router - w_router_hi.astype(F32)).astype(BF16)
    w_router2 = jnp.concatenate([w_router_hi, w_router_lo], axis=-1)
    b_router = jnp.concatenate([b_group_router[0], b_expert_router[0], jnp.zeros((pad,), F32)])[None, :]
    x1, h2p, route, route_t, counts = _mix(o_attn, o_ret, proj, x2, w_attn_branch[0].astype(BF16),
                                           w_ret_branch[0].astype(BF16), w_out[0].astype(BF16),
                                           g_ffn[0][None, :], w_router2, b_router, interpret)
    dest1, dest2, tile_expert, tile_valid, n_rows = _route_plan(route_t, counts, n)
    idx_a = jnp.concatenate([dest1, dest1 + n_rows])[None, :]
    idx_b = jnp.concatenate([dest2, dest2 + n_rows])[None, :]
    xs = _sc_scatter_rows(h2p.reshape(2 * n, SC_ROW_WORDS), idx_a, idx_b, 2 * n_rows)
    ys = _experts(xs.reshape(2, n_rows, SC_ROW_WORDS), tile_expert, tile_valid, wg_bf, wu_bf, wd_bf, interpret)
    idx_g = jnp.concatenate([dest1, dest2, dest1 + n_rows, dest2 + n_rows])[None, :]
    yab = _sc_gather_rows(ys.reshape(2 * n_rows, SC_ROW_WORDS), idx_g)
    out = _final(x1, yab.reshape(4, n, SC_ROW_WORDS), route, g_final[None, :], interpret)
    return out.reshape(batch, seq, d)


def kernel(x, g_mix, w_in, w_attn_branch, w_ret_branch, w_out, g_ffn, w_group_router, b_group_router,
           w_expert_router, b_expert_router, w_gate, w_up, w_down, g_final):
    return _forward(x, g_mix, w_in, w_attn_branch, w_ret_branch, w_out, g_ffn, w_group_router,
                    b_group_router, w_expert_router, b_expert_router, w_gate, w_up, w_down, g_final)
```

```python
import functools

import numpy as np
import jax
import jax.numpy as jnp
from jax import lax
from jax.experimental import pallas as pl
from jax.experimental.pallas import tpu as pltpu
from jax.experimental.pallas import tpu_sc as plsc

F32 = jnp.float32
BF16 = jnp.bfloat16

D_MODEL = 1024
ATTN_GROUPS = ((128, 1), (512, 4), (2048, 16))
N_GROUPS = len(ATTN_GROUPS)
ATTN_HEADS = 8
HEAD_DIM = 64
GROUP_W = ATTN_HEADS * HEAD_DIM
QKV_W = N_GROUPS * GROUP_W
RET_HEADS = 4
RET_DK = 128
RET_DV = 256
RET_CHUNK = 128
RET_TS = 1024
N_EXPERT_GROUPS = 4
EXPERTS_PER_GROUP = 8
N_EXPERTS = N_EXPERT_GROUPS * EXPERTS_PER_GROUP
EXPERT_FF = 512
EPS = 1e-6

LANES = 128
BLK = 128
SPAN = 2048
NEG = -1e30
ACC_PARTS = 3
ATTN_STAGES = 9

COL_GATE_A = 0
COL_GATE_R = 1024
COL_VR = 2048
COL_GR = COL_VR + RET_HEADS * RET_DV
COL_QA = COL_GR + RET_HEADS * RET_DV
COL_KA = COL_QA + QKV_W
COL_VA = COL_KA + QKV_W
COL_QR = COL_VA + QKV_W
COL_KR = COL_QR + RET_HEADS * RET_DK
IN_WIDTH = COL_KR + RET_HEADS * RET_DK

PROJ_TM = 512
PROJ_TN = IN_WIDTH // 2
MXU_N = 256
MIX_TM = 1024
MIX_CHUNK = 512
ROUTE_ROWS = 8
EXP_TM = 512
SC_WINDOW = 128
SC_ROW_WORDS = 256
FIN_TM = 512
VMEM_LIMIT = 56 * 1024 * 1024


def _cparams(sem):
    return pltpu.CompilerParams(dimension_semantics=sem, vmem_limit_bytes=VMEM_LIMIT)


def _sigmoid(x):
    return 0.5 * jnp.tanh(0.5 * x) + 0.5


def _proj_kernel(x_ref, g_ref, w_ref, wg_ref, wu_ref, wd_ref, o_ref, wg_o, wu_o, wd_o):
    x = x_ref[...]
    ms = jnp.mean(x * x, axis=-1, keepdims=True)
    h = (x * lax.rsqrt(ms + EPS) * g_ref[...]).astype(BF16)
    for c in range(PROJ_TN // MXU_N):
        sl = slice(c * MXU_N, (c + 1) * MXU_N)
        o_ref[:, sl] = jnp.dot(h, w_ref[:, sl], preferred_element_type=F32).astype(o_ref.dtype)
    wg_o[...] = wg_ref[...].astype(BF16)
    wu_o[...] = wu_ref[...].astype(BF16)
    wd_o[...] = wd_ref[...].astype(BF16)


def _proj(x2, g_mix, w_in_bf16, w_gate, w_up, w_down, interpret):
    n = x2.shape[0]
    n_i = n // PROJ_TM
    steps = (IN_WIDTH // PROJ_TN) * n_i
    flat = [w.reshape(-1, w.shape[-1]) for w in (w_gate, w_up, w_down)]
    w_specs = [pl.BlockSpec((w.shape[0] // steps, w.shape[1]), lambda j, i: (j * n_i + i, 0)) for w in flat]
    outs = pl.pallas_call(
        _proj_kernel,
        grid=(IN_WIDTH // PROJ_TN, n_i),
        in_specs=[
            pl.BlockSpec((PROJ_TM, D_MODEL), lambda j, i: (i, 0)),
            pl.BlockSpec((1, D_MODEL), lambda j, i: (0, 0)),
            pl.BlockSpec((D_MODEL, PROJ_TN), lambda j, i: (0, j)),
        ] + w_specs,
        out_specs=[pl.BlockSpec((PROJ_TM, PROJ_TN), lambda j, i: (i, j))] + w_specs,
        out_shape=[jax.ShapeDtypeStruct((n, IN_WIDTH), BF16)]
        + [jax.ShapeDtypeStruct(w.shape, BF16) for w in flat],
        compiler_params=_cparams(("arbitrary", "arbitrary")),
        interpret=interpret,
        name="proj",
    )(x2, g_mix, w_in_bf16, *flat)
    return outs[0], outs[1].reshape(w_gate.shape), outs[2].reshape(w_up.shape), outs[3].reshape(w_down.shape)


def _attn_unit(q2, kk, vv, bias_a, bias_b):
    lane = lax.broadcasted_iota(jnp.int32, (BLK, LANES), 1)
    left = lane < HEAD_DIM
    zero = jnp.zeros_like(q2)
    nt = (((1,), (1,)), ((), ()))
    q_st = jnp.concatenate([jnp.where(left, q2, zero), jnp.where(left, zero, q2)], axis=0)
    s = lax.dot_general(q_st, kk, nt, preferred_element_type=F32) + jnp.concatenate([bias_a, bias_b], axis=0)
    m = jnp.max(s, axis=-1, keepdims=True)
    p = jnp.exp(s - m)
    den = jnp.sum(p, axis=-1, keepdims=True)
    o = jnp.dot(p.astype(BF16), vv, preferred_element_type=F32)
    return (jnp.where(left, o[:BLK], o[BLK:]), jnp.where(left, m[:BLK], m[BLK:]),
            jnp.where(left, den[:BLK], den[BLK:]))


def _attn_kernel(q1_ref, q2_ref, q3_ref, k1_ref, k2_ref, k3_ref, v1_ref, v2_ref, v3_ref,
                 bias_ref, o_ref, tmp_ref, qd_ref, kvd_ref, acc_ref, *, seq):
    s_id = pl.program_id(2)
    step = ATTN_GROUPS[1][1]

    def deinterleave(src_ref, src_row0, stage, dst_ref, dst_slot, res_pitch, dst_off, d, scale):
        chunk = min(256, SPAN // d)
        quarter = SPAN // step
        for c0 in range(0, SPAN, 256):
            x = src_ref[pl.ds(pl.multiple_of(src_row0 + c0, 256), 256), :].astype(F32)
            tmp_ref[stage, c0:c0 + 256, :] = x if scale == 1.0 else x * scale

        def strided_pass(src_stage, base, length, put):
            for b in range(step):
                for j0 in range(0, length // step, chunk):
                    put(b, j0, tmp_ref[src_stage, pl.ds(base + b + step * j0, chunk, stride=step), :])

        def put_out(res, j0, rows):
            row0 = res * res_pitch + j0 + dst_off
            if not isinstance(row0, int):
                row0 = pl.multiple_of(row0, chunk)
            dst_ref[dst_slot, pl.ds(row0, chunk), :] = rows.astype(BF16)

        if d == step:
            strided_pass(stage, 0, SPAN, put_out)
        else:
            def put_mid(b, j0, rows):
                tmp_ref[stage + 1, b * quarter + j0:b * quarter + j0 + chunk, :] = rows
            strided_pass(stage, 0, SPAN, put_mid)
            for b in range(step):
                strided_pass(stage + 1, b * quarter, quarter,
                             functools.partial(lambda a, j0, rows, b: put_out(a * step + b, j0, rows), b=b))

    span_row0 = pl.multiple_of(s_id * SPAN, SPAN)
    stage = 0
    for gi, (q_ref, k_ref, v_ref) in ((1, (q2_ref, k2_ref, v2_ref)), (2, (q3_ref, k3_ref, v3_ref))):
        d = ATTN_GROUPS[gi][1]
        n_stage = 1 if d == step else 2
        deinterleave(q_ref, 0, stage, qd_ref, gi - 1, SPAN // d, 0, d, 0.125)
        deinterleave(k_ref, span_row0, stage + n_stage, kvd_ref, 2 * (gi - 1), seq // d, s_id * (SPAN // d), d, 1.0)
        deinterleave(v_ref, span_row0, stage + 2 * n_stage, kvd_ref, 2 * (gi - 1) + 1, seq // d,
                     s_id * (SPAN // d), d, 1.0)
        stage += 3 * n_stage

    def dilated_unit(gi, d, m, r, first):
        slot = gi - 1
        loc = BLK * m * d + r
        q_row = r * (SPAN // d) + BLK * m
        cur = pl.multiple_of(r * (seq // d) + s_id * (SPAN // d) + BLK * m, BLK)
        prev = pl.multiple_of(jnp.where(first == 1, cur, cur - BLK), BLK)
        q2 = qd_ref[slot, pl.ds(q_row, BLK), :]
        kk = jnp.concatenate([kvd_ref[2 * slot, pl.ds(prev, BLK), :], kvd_ref[2 * slot, pl.ds(cur, BLK), :]], axis=0)
        vv = jnp.concatenate([kvd_ref[2 * slot + 1, pl.ds(prev, BLK), :],
                              kvd_ref[2 * slot + 1, pl.ds(cur, BLK), :]], axis=0)
        parts = _attn_unit(q2, kk, vv, bias_ref[gi, first, 0], bias_ref[gi, first, 1])
        for j, part in enumerate(parts):
            acc_ref[ACC_PARTS * slot + j, pl.ds(loc, BLK, stride=d), :] = part

    def dense_unit(m):
        loc = m * BLK
        cur = pl.multiple_of(s_id * SPAN + loc, BLK)
        prev = pl.multiple_of(jnp.maximum(cur - BLK, 0), BLK)
        first = jnp.where(cur == 0, 1, 0)
        q2 = q1_ref[loc:loc + BLK, :] * 0.125
        kk = jnp.concatenate([k1_ref[pl.ds(prev, BLK), :], k1_ref[pl.ds(cur, BLK), :]], axis=0)
        vv = jnp.concatenate([v1_ref[pl.ds(prev, BLK), :], v1_ref[pl.ds(cur, BLK), :]], axis=0)
        n1, m1, d1 = _attn_unit(q2, kk, vv, bias_ref[0, first, 0], bias_ref[0, first, 1])
        (n2, m2, d2), (n3, m3, d3) = (
            tuple(acc_ref[g * ACC_PARTS + j, loc:loc + BLK, :] for j in range(ACC_PARTS)) for g in range(2))
        mx = jnp.maximum(jnp.maximum(m1, m2), m3)
        w1, w2, w3 = jnp.exp(m1 - mx), jnp.exp(m2 - mx), jnp.exp(m3 - mx)
        num = w1 * n1 + w2 * n2 + w3 * n3
        den = w1 * d1 + w2 * d2 + w3 * d3
        o_ref[loc:loc + BLK, :] = (num / den).astype(o_ref.dtype)

    first_span = jnp.where(s_id == 0, 1, 0)
    for gi in (2, 1):
        d = ATTN_GROUPS[gi][1]
        for m in range(SPAN // (BLK * d)):
            for r in range(d):
                dilated_unit(gi, d, m, r, first_span if m == 0 else 0)
    for m in range(SPAN // BLK):
        dense_unit(m)


def _attn_bias():
    slopes = np.exp2(-8.0 * np.arange(1, ATTN_HEADS + 1, dtype=np.float64) / ATTN_HEADS)
    qi = np.arange(BLK)[:, None]
    kj = np.arange(2 * BLK)[None, :]
    rel = qi + BLK - kj
    out = np.zeros((N_GROUPS, 2, ATTN_HEADS, BLK, 2 * BLK), np.float32)
    for gi, (window, d) in enumerate(ATTN_GROUPS):
        n_back = window // d
        assert n_back == BLK
        valid = (rel >= 0) & (rel <= n_back)
        bias = -slopes[:, None, None] * (rel * d)[None].astype(np.float64)
        out[gi, 0] = np.where(valid[None], bias, NEG)
        out[gi, 1] = np.where((valid & (kj >= BLK))[None], bias, NEG)
    return jnp.asarray(out)


def _attention(proj, batch, seq, interpret):
    n = batch * seq
    spans = seq // SPAN
    n_hp = GROUP_W // LANES
    qcol = lambda g: (COL_QA + g * GROUP_W) // LANES
    kcol = lambda g: (COL_KA + g * GROUP_W) // LANES
    vcol = lambda g: (COL_VA + g * GROUP_W) // LANES
    q_specs = [pl.BlockSpec((SPAN, LANES), functools.partial(lambda b, hp, s, c: (b * spans + s, c + hp), c=qcol(g)))
               for g in range(N_GROUPS)]
    k_specs = [pl.BlockSpec((seq, LANES), functools.partial(lambda b, hp, s, c: (b, c + hp), c=kcol(g)))
               for g in range(N_GROUPS)]
    v_specs = [pl.BlockSpec((seq, LANES), functools.partial(lambda b, hp, s, c: (b, c + hp), c=vcol(g)))
               for g in range(N_GROUPS)]
    bias_spec = pl.BlockSpec((N_GROUPS, 2, 2, BLK, 2 * BLK), lambda b, hp, s: (0, 0, hp, 0, 0))
    return pl.pallas_call(
        functools.partial(_attn_kernel, seq=seq),
        grid=(batch, n_hp, spans),
        in_specs=q_specs + k_specs + v_specs + [bias_spec],
        out_specs=pl.BlockSpec((SPAN, LANES), lambda b, hp, s: (b * spans + s, hp)),
        out_shape=jax.ShapeDtypeStruct((n, GROUP_W), BF16),
        scratch_shapes=[
            pltpu.VMEM((ATTN_STAGES, SPAN, LANES), F32),
            pltpu.VMEM((2, SPAN, LANES), BF16),
            pltpu.VMEM((4, seq, LANES), BF16),
            pltpu.VMEM((2 * ACC_PARTS, SPAN, LANES), F32),
        ],
        compiler_params=_cparams(("arbitrary", "arbitrary", "arbitrary")),
        interpret=interpret,
        name="attn",
    )(*([proj] * 9), _attn_bias())


def _ret_kernel(q_ref, k_ref, v_ref, gr_ref, dec_ref, xi_ref, zeta_ref, gch_ref, o_ref, st_ref):
    @pl.when(pl.program_id(1) == 0)
    def _():
        st_ref[...] = jnp.zeros_like(st_ref)

    nt = (((1,), (1,)), ((), ()))
    scale = RET_DK ** -0.5

    for c in range(RET_TS // RET_CHUNK):
        rows = pl.ds(c * RET_CHUNK, RET_CHUNK)
        for h in range(RET_HEADS):
            kcols = slice(h * RET_DK, (h + 1) * RET_DK)
            vcols = slice(h * RET_DV, (h + 1) * RET_DV)
            qi = q_ref[rows, kcols]
            kf = k_ref[rows, kcols].astype(F32) * scale
            ki = kf.astype(BF16)
            kz_t = jnp.transpose(kf * zeta_ref[h]).astype(BF16)
            vi = v_ref[rows, vcols]
            att = lax.dot_general(qi, ki, nt, preferred_element_type=F32) * dec_ref[h]
            inner = jnp.dot(att.astype(BF16), vi, preferred_element_type=F32)
            st = st_ref[h]
            cross = jnp.dot(qi, st.astype(BF16), preferred_element_type=F32) * xi_ref[h]
            st_ref[h] = gch_ref[h] * st + jnp.dot(kz_t, vi, preferred_element_type=F32)
            y = inner + cross
            mu = jnp.mean(y, axis=-1, keepdims=True)
            yc = y - mu
            var = jnp.mean(yc * yc, axis=-1, keepdims=True)
            yn = yc * lax.rsqrt(var + EPS)
            g = gr_ref[rows, vcols].astype(F32)
            o_ref[rows, vcols] = (g * _sigmoid(g) * yn).astype(o_ref.dtype)


def _ret_tables():
    c = RET_CHUNK
    log_g = np.log1p(-np.exp2(-5.0 - np.arange(RET_HEADS, dtype=np.float64)))
    pos = np.arange(c, dtype=np.float64)
    diff = pos[:, None] - pos[None, :]
    dec = np.where(diff >= 0, np.exp(log_g[:, None, None] * np.maximum(diff, 0.0)), 0.0)
    xi = np.exp(log_g[:, None] * (pos + 1.0))[..., None] * np.ones((1, 1, RET_DV))
    zeta = np.exp(log_g[:, None] * (c - 1.0 - pos))[..., None] * np.ones((1, 1, RET_DK))
    gch = np.exp(log_g * c)[:, None, None] * np.ones((1, 1, RET_DV))
    return tuple(jnp.asarray(t, F32) for t in (dec, xi, zeta, gch))


def _retention(proj, batch, seq, interpret):
    n = batch * seq
    dec, xi, zeta, gch = _ret_tables()
    qk_w = RET_HEADS * RET_DK
    v_w = RET_HEADS * RET_DV
    nts = seq // RET_TS
    const3 = lambda b, t: (0, 0, 0)
    return pl.pallas_call(
        _ret_kernel,
        grid=(batch, nts),
        in_specs=[
            pl.BlockSpec((RET_TS, qk_w), lambda b, t: (b * nts + t, COL_QR // qk_w)),
            pl.BlockSpec((RET_TS, qk_w), lambda b, t: (b * nts + t, COL_KR // qk_w)),
            pl.BlockSpec((RET_TS, v_w), lambda b, t: (b * nts + t, COL_VR // v_w)),
            pl.BlockSpec((RET_TS, v_w), lambda b, t: (b * nts + t, COL_GR // v_w)),
            pl.BlockSpec((RET_HEADS, RET_CHUNK, RET_CHUNK), const3),
            pl.BlockSpec((RET_HEADS, RET_CHUNK, RET_DV), const3),
            pl.BlockSpec((RET_HEADS, RET_CHUNK, RET_DK), const3),
            pl.BlockSpec((RET_HEADS, 1, RET_DV), const3),
        ],
        out_specs=pl.BlockSpec((RET_TS, v_w), lambda b, t: (b * nts + t, 0)),
        out_shape=jax.ShapeDtypeStruct((n, v_w), BF16),
        scratch_shapes=[pltpu.VMEM((RET_HEADS, RET_DK, RET_DV), F32)],
        compiler_params=_cparams(("arbitrary", "arbitrary")),
        interpret=interpret,
        name="retention",
    )(proj, proj, proj, proj, dec, xi, zeta, gch)


ROUTER_OFF = N_EXPERT_GROUPS


def _pack_bf16_pair(a, b):
    hi = lax.bitcast_convert_type(a.astype(BF16).astype(F32), jnp.uint32)
    lo = lax.bitcast_convert_type(b.astype(BF16).astype(F32), jnp.uint32)
    return lax.bitcast_convert_type(hi | (lo >> 16), jnp.int32)


def _unpack_bf16_pair(w):
    u = lax.bitcast_convert_type(w, jnp.uint32)
    a = lax.bitcast_convert_type(u & jnp.uint32(0xFFFF0000), F32).astype(BF16)
    b = lax.bitcast_convert_type(u << 16, F32).astype(BF16)
    return a, b


def _pack_rows(y):
    q = D_MODEL // 4
    return (_pack_bf16_pair(y[:, 0:q], y[:, 2 * q:3 * q]), _pack_bf16_pair(y[:, q:2 * q], y[:, 3 * q:4 * q]))


def _unpack_rows(slab0, slab1):
    q0, q2 = _unpack_bf16_pair(slab0)
    q1, q3 = _unpack_bf16_pair(slab1)
    return jnp.concatenate([q0, q1, q2, q3], axis=1)


def _mix_kernel(oa_ref, or_ref, ga_ref, gr_ref, x_ref, pa_ref, pr_ref, wo_ref, gf_ref, wr_ref, br_ref,
                x1_ref, h2_ref, route_ref, route_t_ref, cnt_ref, carry_ref):
    @pl.when(pl.program_id(0) == 0)
    def _():
        carry_ref[...] = jnp.zeros_like(carry_ref)

    for c in range(MIX_TM // MIX_CHUNK):
        _mix_rows(pl.ds(c * MIX_CHUNK, MIX_CHUNK), oa_ref, or_ref, ga_ref, gr_ref, x_ref, pa_ref, pr_ref, wo_ref,
                  gf_ref, wr_ref, br_ref, x1_ref, h2_ref, route_ref, route_t_ref, cnt_ref, carry_ref)


def _mix_rows(rows, oa_ref, or_ref, ga_ref, gr_ref, x_ref, pa_ref, pr_ref, wo_ref, gf_ref, wr_ref, br_ref,
              x1_ref, h2_ref, route_ref, route_t_ref, cnt_ref, carry_ref):
    a = jnp.dot(oa_ref[rows, :], pa_ref[...], preferred_element_type=F32)
    r = jnp.dot(or_ref[rows, :], pr_ref[...], preferred_element_type=F32)
    merged = (_sigmoid(ga_ref[rows, :].astype(F32)) * a + _sigmoid(gr_ref[rows, :].astype(F32)) * r)
    x1 = x_ref[rows, :] + jnp.dot(merged.astype(BF16), wo_ref[...], preferred_element_type=F32)
    x1_ref[rows, :] = x1
    ms = jnp.mean(x1 * x1, axis=-1, keepdims=True)
    h2 = x1 * lax.rsqrt(ms + EPS) * gf_ref[...]
    h2_ref[0, rows, :], h2_ref[1, rows, :] = _pack_rows(h2)

    h_hi = h2.astype(BF16)
    h_lo = (h2 - h_hi.astype(F32)).astype(BF16)
    both = jnp.dot(h_hi, wr_ref[...], preferred_element_type=F32)
    logits = (both[:, :LANES] + both[:, LANES:]
              + jnp.dot(h_lo, wr_ref[:, :LANES], preferred_element_type=F32) + br_ref[...])
    tm = logits.shape[0]
    lane = lax.broadcasted_iota(jnp.int32, (tm, LANES), 1).astype(F32)
    big = jnp.float32(4 * LANES)
    ninf = -jnp.inf
    is_g = lane < N_EXPERT_GROUPS
    gl = jnp.where(is_g, logits, ninf)
    gmax = jnp.max(gl, axis=-1, keepdims=True)
    gsum = jnp.sum(jnp.where(is_g, jnp.exp(gl - gmax), 0.0), axis=-1, keepdims=True)
    g_val = 1.0 / gsum
    g_idx = jnp.min(jnp.where(jnp.logical_and(is_g, gl == gmax), lane, big), axis=-1, keepdims=True)
    lo = ROUTER_OFF + EXPERTS_PER_GROUP * g_idx
    in_grp = jnp.logical_and(lane >= lo, lane < lo + EXPERTS_PER_GROUP)
    el = jnp.where(in_grp, logits, ninf)
    v1 = jnp.max(el, axis=-1, keepdims=True)
    i1 = jnp.min(jnp.where(jnp.logical_and(in_grp, el == v1), lane, big), axis=-1, keepdims=True)
    rest = jnp.logical_and(in_grp, lane != i1)
    el2 = jnp.where(rest, logits, ninf)
    v2 = jnp.max(el2, axis=-1, keepdims=True)
    i2 = jnp.min(jnp.where(jnp.logical_and(rest, el2 == v2), lane, big), axis=-1, keepdims=True)
    t = jnp.exp(v2 - v1)
    w1 = g_val / (1.0 + t)
    w2 = g_val * t / (1.0 + t)

    sel = jnp.logical_or(lane == i1, lane == i2)
    sel_bf = jnp.where(sel, 1.0, 0.0).astype(BF16)
    row = lax.broadcasted_iota(jnp.int32, (tm, tm), 0)
    col = lax.broadcasted_iota(jnp.int32, (tm, tm), 1)
    tri = jnp.where(col < row, 1.0, 0.0).astype(BF16)
    before = jnp.dot(tri, sel_bf, preferred_element_type=F32) + carry_ref[...]
    r1 = jnp.sum(jnp.where(lane == i1, before, 0.0), axis=-1, keepdims=True)
    r2 = jnp.sum(jnp.where(lane == i2, before, 0.0), axis=-1, keepdims=True)
    carry = carry_ref[...] + jnp.sum(jnp.where(sel, 1.0, 0.0), axis=0, keepdims=True)
    carry_ref[...] = carry
    cnt_ref[...] = carry

    vals = (i1 - ROUTER_OFF, i2 - ROUTER_OFF, w1, w2, r1, r2)
    route = jnp.zeros((tm, LANES), F32)
    for j, v in enumerate(vals):
        route = jnp.where(lane == j, v, route)
    route_ref[rows, :] = route
    route_t_ref[:, rows] = jnp.transpose(route)[:ROUTE_ROWS, :]


def _mix(o_attn, o_ret, proj, x2, pa, pr, wo, g_ffn, w_router, b_router, interpret):
    n = x2.shape[0]
    tm = MIX_TM
    const = lambda i: (0, 0)
    return pl.pallas_call(
        _mix_kernel,
        grid=(n // tm,),
        in_specs=[
            pl.BlockSpec((tm, GROUP_W), lambda i: (i, 0)),
            pl.BlockSpec((tm, D_MODEL), lambda i: (i, 0)),
            pl.BlockSpec((tm, D_MODEL), lambda i: (i, COL_GATE_A // D_MODEL)),
            pl.BlockSpec((tm, D_MODEL), lambda i: (i, COL_GATE_R // D_MODEL)),
            pl.BlockSpec((tm, D_MODEL), lambda i: (i, 0)),
            pl.BlockSpec((GROUP_W, D_MODEL), const),
            pl.BlockSpec((D_MODEL, D_MODEL), const),
            pl.BlockSpec((D_MODEL, D_MODEL), const),
            pl.BlockSpec((1, D_MODEL), const),
            pl.BlockSpec((D_MODEL, 2 * LANES), const),
            pl.BlockSpec((1, LANES), const),
        ],
        out_specs=[
            pl.BlockSpec((tm, D_MODEL), lambda i: (i, 0)),
            pl.BlockSpec((2, tm, SC_ROW_WORDS), lambda i: (0, i, 0)),
            pl.BlockSpec((tm, LANES), lambda i: (i, 0)),
            pl.BlockSpec((ROUTE_ROWS, tm), lambda i: (0, i)),
            pl.BlockSpec((1, LANES), const),
        ],
        out_shape=[
            jax.ShapeDtypeStruct((n, D_MODEL), F32),
            jax.ShapeDtypeStruct((2, n, SC_ROW_WORDS), jnp.int32),
            jax.ShapeDtypeStruct((n, LANES), F32),
            jax.ShapeDtypeStruct((ROUTE_ROWS, n), F32),
            jax.ShapeDtypeStruct((1, LANES), F32),
        ],
        scratch_shapes=[pltpu.VMEM((1, LANES), F32)],
        compiler_params=_cparams(("arbitrary",)),
        interpret=interpret,
        name="mix_router",
    )(o_attn, o_ret, proj, proj, x2, pa, pr, wo, g_ffn, w_router, b_router)


def _expert_kernel(te_ref, tr_ref, xs_ref, wg_ref, wu_ref, wd_ref, o_ref):
    i = pl.program_id(0)

    @pl.when(tr_ref[i] == i)
    def _():
        xs = _unpack_rows(xs_ref[0], xs_ref[1])
        a = jnp.dot(xs, wg_ref[0], preferred_element_type=F32)
        u = jnp.dot(xs, wu_ref[0], preferred_element_type=F32)
        hid = (a * _sigmoid(a) * u).astype(BF16)
        y = jnp.dot(hid, wd_ref[0], preferred_element_type=F32)
        o_ref[0], o_ref[1] = _pack_rows(y)


def _experts(xs, tile_expert, tile_row, w_gate, w_up, w_down, interpret):
    p = xs.shape[1]
    n_tiles = p // EXP_TM
    grid_spec = pltpu.PrefetchScalarGridSpec(
        num_scalar_prefetch=2,
        grid=(n_tiles,),
        in_specs=[
            pl.BlockSpec((2, EXP_TM, SC_ROW_WORDS), lambda i, te, tr: (0, tr[i], 0)),
            pl.BlockSpec((1, D_MODEL, EXPERT_FF), lambda i, te, tr: (te[i], 0, 0)),
            pl.BlockSpec((1, D_MODEL, EXPERT_FF), lambda i, te, tr: (te[i], 0, 0)),
            pl.BlockSpec((1, EXPERT_FF, D_MODEL), lambda i, te, tr: (te[i], 0, 0)),
        ],
        out_specs=pl.BlockSpec((2, EXP_TM, SC_ROW_WORDS), lambda i, te, tr: (0, tr[i], 0)),
    )
    return pl.pallas_call(
        _expert_kernel,
        grid_spec=grid_spec,
        out_shape=jax.ShapeDtypeStruct((2, p, SC_ROW_WORDS), jnp.int32),
        compiler_params=_cparams(("arbitrary",)),
        interpret=interpret,
        name="experts",
    )(tile_expert, tile_row, xs, w_gate, w_up, w_down)


def _final_kernel(x1_ref, yab_ref, route_ref, g_ref, o_ref):
    route = route_ref[...]
    w1 = route[:, 2:3]
    w2 = route[:, 3:4]
    ya = _unpack_rows(yab_ref[0], yab_ref[2]).astype(F32)
    yb = _unpack_rows(yab_ref[1], yab_ref[3]).astype(F32)
    x2 = x1_ref[...] + w1 * ya + w2 * yb
    ms = jnp.mean(x2 * x2, axis=-1, keepdims=True)
    o_ref[...] = x2 * lax.rsqrt(ms + EPS) * g_ref[...]


def _final(x1, yab, route, g_final, interpret):
    n = x1.shape[0]
    tm = FIN_TM
    row = lambda i: (i, 0)
    return pl.pallas_call(
        _final_kernel,
        grid=(n // tm,),
        in_specs=[
            pl.BlockSpec((tm, D_MODEL), row),
            pl.BlockSpec((4, tm, SC_ROW_WORDS), lambda i: (0, i, 0)),
            pl.BlockSpec((tm, LANES), row),
            pl.BlockSpec((1, D_MODEL), lambda i: (0, 0)),
        ],
        out_specs=pl.BlockSpec((tm, D_MODEL), row),
        out_shape=jax.ShapeDtypeStruct((n, D_MODEL), F32),
        compiler_params=_cparams(("arbitrary",)),
        interpret=interpret,
        name="combine_final",
    )(x1, yab, route, g_final)


def _permute_w_in(w_in):
    splits = np.cumsum([QKV_W, QKV_W, QKV_W, 512, 512, 1024, 1024, D_MODEL, D_MODEL])[:-1].tolist()
    qa, ka, va, qr, kr, vr, gr, gate_a, gate_r = jnp.split(w_in, splits, axis=-1)
    return jnp.concatenate([gate_a, gate_r, vr, gr, qa, ka, va, qr, kr], axis=-1).astype(BF16)


def _route_plan(route_t, counts, n):
    experts = route_t[0:2].astype(jnp.int32)
    ranks = route_t[4:6].astype(jnp.int32)
    cnt = counts[0, ROUTER_OFF:ROUTER_OFF + N_EXPERTS].astype(jnp.int32)
    padded = ((cnt + EXP_TM - 1) // EXP_TM) * EXP_TM
    ends = jnp.cumsum(padded)
    offs = ends - padded
    dest = ranks
    for e in range(N_EXPERTS):
        dest = dest + jnp.where(experts == e, offs[e], 0)
    n_rows = 2 * n + N_EXPERTS * EXP_TM
    idx4 = jnp.concatenate([dest, dest + n_rows], axis=0)
    tile_row = jnp.minimum(jnp.arange(n_rows // EXP_TM, dtype=jnp.int32), ends[-1] // EXP_TM - 1)
    tile_expert = jnp.sum((ends[None, :] <= (tile_row * EXP_TM)[:, None]).astype(jnp.int32), axis=1)
    return idx4, tile_expert, tile_row, n_rows


def _sc_mesh():
    return plsc.VectorSubcoreMesh(core_axis_name="core", subcore_axis_name="subcore")


def _sc_scatter_rows(rows, idx4, n_out):
    n_in, w = rows.shape
    nb = idx4.shape[1] // SC_WINDOW

    @functools.partial(pl.kernel, out_type=jax.ShapeDtypeStruct((n_out, w), rows.dtype), mesh=_sc_mesh(),
                       scratch_types=[], name="sc_scatter_rows")
    def scatter(x_hbm, ia_hbm, ib_hbm, o_hbm):
        def body(x_vmem, ia_vmem, ib_vmem):
            pltpu.sync_copy(x_vmem, o_hbm.at[ia_vmem.at[0]])
            pltpu.sync_copy(x_vmem, o_hbm.at[ib_vmem.at[0]])

        pltpu.emit_pipeline(
            body,
            grid=(n_in // SC_WINDOW,),
            in_specs=[pl.BlockSpec((SC_WINDOW, w), lambda i: (i, 0)),
                      pl.BlockSpec((1, SC_WINDOW), lambda i: (2 * (i // nb), i % nb)),
                      pl.BlockSpec((1, SC_WINDOW), lambda i: (2 * (i // nb) + 1, i % nb))],
            out_specs=[],
            core_axis_name=("core", "subcore"),
            dimension_semantics=(pltpu.PARALLEL,),
        )(x_hbm, ia_hbm, ib_hbm)

    return scatter(rows, idx4, idx4)


def _sc_gather_rows(table, idx4):
    nb = idx4.shape[1] // SC_WINDOW
    n_idx = idx4.shape[0] * idx4.shape[1]
    w = table.shape[1]

    @functools.partial(pl.kernel, out_type=jax.ShapeDtypeStruct((n_idx, w), table.dtype), mesh=_sc_mesh(),
                       scratch_types=[], name="sc_gather_rows")
    def gather(t_hbm, i_hbm, o_hbm):
        def body(i_vmem, o_vmem):
            pltpu.sync_copy(t_hbm.at[i_vmem.at[0]], o_vmem)

        pltpu.emit_pipeline(
            body,
            grid=(n_idx // SC_WINDOW,),
            in_specs=[pl.BlockSpec((1, SC_WINDOW), lambda i: (i // nb, i % nb))],
            out_specs=[pl.BlockSpec((SC_WINDOW, w), lambda i: (i, 0))],
            core_axis_name=("core", "subcore"),
            dimension_semantics=(pltpu.PARALLEL,),
        )(i_hbm, o_hbm)

    return gather(table, idx4)


def _forward(x, g_mix, w_in, w_attn_branch, w_ret_branch, w_out, g_ffn, w_group_router, b_group_router,
             w_expert_router, b_expert_router, w_gate, w_up, w_down, g_final, interpret=False):
    batch, seq, d = x.shape
    n = batch * seq
    x2 = x.reshape(n, d)
    proj, wg_bf, wu_bf, wd_bf = _proj(x2, g_mix[0][None, :], _permute_w_in(w_in[0]), w_gate[0], w_up[0],
                                      w_down[0], interpret)
    o_attn = _attention(proj, batch, seq, interpret)
    o_ret = _retention(proj, batch, seq, interpret)
    pad = LANES - N_EXPERT_GROUPS - N_EXPERTS
    w_router = jnp.concatenate([w_group_router[0], w_expert_router[0], jnp.zeros((d, pad), F32)], axis=-1)
    w_router_hi = w_router.astype(BF16)
    w_router_lo = (w_router - w_router_hi.astype(F32)).astype(BF16)
    w_router2 = jnp.concatenate([w_router_hi, w_router_lo], axis=-1)
    b_router = jnp.concatenate([b_group_router[0], b_expert_router[0], jnp.zeros((pad,), F32)])[None, :]
    x1, h2p, route, route_t, counts = _mix(o_attn, o_ret, proj, x2, w_attn_branch[0].astype(BF16),
                                           w_ret_branch[0].astype(BF16), w_out[0].astype(BF16),
                                           g_ffn[0][None, :], w_router2, b_router, interpret)
    idx4, tile_expert, tile_row, n_rows = _route_plan(route_t, counts, n)
    xs = _sc_scatter_rows(h2p.reshape(2 * n, SC_ROW_WORDS), idx4, 2 * n_rows)
    ys = _experts(xs.reshape(2, n_rows, SC_ROW_WORDS), tile_expert, tile_row, wg_bf, wu_bf, wd_bf, interpret)
    yab = _sc_gather_rows(ys.reshape(2 * n_rows, SC_ROW_WORDS), idx4)
    out = _final(x1, yab.reshape(4, n, SC_ROW_WORDS), route, g_final[None, :], interpret)
    return out.reshape(batch, seq, d)


def kernel(x, g_mix, w_in, w_attn_branch, w_ret_branch, w_out, g_ffn, w_group_router, b_group_router,
           w_expert_router, b_expert_router, w_gate, w_up, w_down, g_final):
    return _forward(x, g_mix, w_in, w_attn_branch, w_ret_branch, w_out, g_ffn, w_group_router,
                    b_group_router, w_expert_router, b_expert_router, w_gate, w_up, w_down, g_final)
```

```python
import functools

import numpy as np
import jax
import jax.numpy as jnp
from jax import lax
from jax.experimental import pallas as pl
from jax.experimental.pallas import tpu as pltpu
from jax.experimental.pallas import tpu_sc as plsc

F32 = jnp.float32
BF16 = jnp.bfloat16

D_MODEL = 1024
ATTN_GROUPS = ((128, 1), (512, 4), (2048, 16))
N_GROUPS = len(ATTN_GROUPS)
ATTN_HEADS = 8
HEAD_DIM = 64
GROUP_W = ATTN_HEADS * HEAD_DIM
QKV_W = N_GROUPS * GROUP_W
RET_HEADS = 4
RET_DK = 128
RET_DV = 256
RET_CHUNK = 128
RET_TS = 1024
N_EXPERT_GROUPS = 4
EXPERTS_PER_GROUP = 8
N_EXPERTS = N_EXPERT_GROUPS * EXPERTS_PER_GROUP
EXPERT_FF = 512
EPS = 1e-6

LANES = 128
BLK = 128
SPAN = 2048
NEG = -1e30
ACC_PARTS = 3
ATTN_STAGES = 9

COL_GATE_A = 0
COL_GATE_R = 1024
COL_VR = 2048
COL_GR = COL_VR + RET_HEADS * RET_DV
COL_QA = COL_GR + RET_HEADS * RET_DV
COL_KA = COL_QA + QKV_W
COL_VA = COL_KA + QKV_W
COL_QR = COL_VA + QKV_W
COL_KR = COL_QR + RET_HEADS * RET_DK
IN_WIDTH = COL_KR + RET_HEADS * RET_DK

PROJ_TM = 512
PROJ_TN = IN_WIDTH // 2
MXU_N = 256
MIX_TM = 1024
MIX_CHUNK = 512
ROUTE_ROWS = 8
EXP_TM = 512
SC_WINDOW = 128
SC_ROW_WORDS = 256
FIN_TM = 512
VMEM_LIMIT = 56 * 1024 * 1024


def _cparams(sem):
    return pltpu.CompilerParams(dimension_semantics=sem, vmem_limit_bytes=VMEM_LIMIT)


def _sigmoid(x):
    return 0.5 * jnp.tanh(0.5 * x) + 0.5


def _proj_kernel(x_ref, g_ref, w_ref, wg_ref, wu_ref, wd_ref, o_ref, wg_o, wu_o, wd_o):
    x = x_ref[...]
    ms = jnp.mean(x * x, axis=-1, keepdims=True)
    h = (x * lax.rsqrt(ms + EPS) * g_ref[...]).astype(BF16)
    for c in range(PROJ_TN // MXU_N):
        sl = slice(c * MXU_N, (c + 1) * MXU_N)
        o_ref[:, sl] = jnp.dot(h, w_ref[:, sl], preferred_element_type=F32).astype(o_ref.dtype)
    wg_o[...] = wg_ref[...].astype(BF16)
    wu_o[...] = wu_ref[...].astype(BF16)
    wd_o[...] = wd_ref[...].astype(BF16)


def _proj(x2, g_mix, w_in_bf16, w_gate, w_up, w_down, interpret):
    n = x2.shape[0]
    n_i = n // PROJ_TM
    steps = (IN_WIDTH // PROJ_TN) * n_i
    flat = [w.reshape(-1, w.shape[-1]) for w in (w_gate, w_up, w_down)]
    w_specs = [pl.BlockSpec((w.shape[0] // steps, w.shape[1]), lambda j, i: (j * n_i + i, 0)) for w in flat]
    outs = pl.pallas_call(
        _proj_kernel,
        grid=(IN_WIDTH // PROJ_TN, n_i),
        in_specs=[
            pl.BlockSpec((PROJ_TM, D_MODEL), lambda j, i: (i, 0)),
            pl.BlockSpec((1, D_MODEL), lambda j, i: (0, 0)),
            pl.BlockSpec((D_MODEL, PROJ_TN), lambda j, i: (0, j)),
        ] + w_specs,
        out_specs=[pl.BlockSpec((PROJ_TM, PROJ_TN), lambda j, i: (i, j))] + w_specs,
        out_shape=[jax.ShapeDtypeStruct((n, IN_WIDTH), BF16)]
        + [jax.ShapeDtypeStruct(w.shape, BF16) for w in flat],
        compiler_params=_cparams(("arbitrary", "arbitrary")),
        interpret=interpret,
        name="proj",
    )(x2, g_mix, w_in_bf16, *flat)
    return outs[0], outs[1].reshape(w_gate.shape), outs[2].reshape(w_up.shape), outs[3].reshape(w_down.shape)


def _attn_unit(q2, kk, vv, bias_a, bias_b):
    lane = lax.broadcasted_iota(jnp.int32, (BLK, LANES), 1)
    left = lane < HEAD_DIM
    zero = jnp.zeros_like(q2)
    nt = (((1,), (1,)), ((), ()))
    q_st = jnp.concatenate([jnp.where(left, q2, zero), jnp.where(left, zero, q2)], axis=0)
    s = lax.dot_general(q_st, kk, nt, preferred_element_type=F32) + jnp.concatenate([bias_a, bias_b], axis=0)
    m = jnp.max(s, axis=-1, keepdims=True)
    p = jnp.exp(s - m)
    den = jnp.sum(p, axis=-1, keepdims=True)
    o = jnp.dot(p.astype(BF16), vv, preferred_element_type=F32)
    return (jnp.where(left, o[:BLK], o[BLK:]), jnp.where(left, m[:BLK], m[BLK:]),
            jnp.where(left, den[:BLK], den[BLK:]))


def _attn_kernel(q1_ref, q2_ref, q3_ref, k1_ref, k2_ref, k3_ref, v1_ref, v2_ref, v3_ref,
                 bias_ref, o_ref, tmp_ref, qd_ref, kvd_ref, acc_ref, *, seq):
    s_id = pl.program_id(2)
    step = ATTN_GROUPS[1][1]

    def deinterleave(src_ref, src_row0, stage, dst_ref, dst_slot, res_pitch, dst_off, d, scale):
        chunk = min(256, SPAN // d)
        quarter = SPAN // step
        for c0 in range(0, SPAN, 256):
            x = src_ref[pl.ds(pl.multiple_of(src_row0 + c0, 256), 256), :].astype(F32)
            tmp_ref[stage, c0:c0 + 256, :] = x if scale == 1.0 else x * scale

        def strided_pass(src_stage, base, length, put):
            for b in range(step):
                for j0 in range(0, length // step, chunk):
                    put(b, j0, tmp_ref[src_stage, pl.ds(base + b + step * j0, chunk, stride=step), :])

        def put_out(res, j0, rows):
            row0 = res * res_pitch + j0 + dst_off
            if not isinstance(row0, int):
                row0 = pl.multiple_of(row0, chunk)
            dst_ref[dst_slot, pl.ds(row0, chunk), :] = rows.astype(BF16)

        if d == step:
            strided_pass(stage, 0, SPAN, put_out)
        else:
            def put_mid(b, j0, rows):
                tmp_ref[stage + 1, b * quarter + j0:b * quarter + j0 + chunk, :] = rows
            strided_pass(stage, 0, SPAN, put_mid)
            for b in range(step):
                strided_pass(stage + 1, b * quarter, quarter,
                             functools.partial(lambda a, j0, rows, b: put_out(a * step + b, j0, rows), b=b))

    span_row0 = pl.multiple_of(s_id * SPAN, SPAN)
    stage = 0
    for gi, (q_ref, k_ref, v_ref) in ((1, (q2_ref, k2_ref, v2_ref)), (2, (q3_ref, k3_ref, v3_ref))):
        d = ATTN_GROUPS[gi][1]
        n_stage = 1 if d == step else 2
        deinterleave(q_ref, 0, stage, qd_ref, gi - 1, SPAN // d, 0, d, 0.125)
        deinterleave(k_ref, span_row0, stage + n_stage, kvd_ref, 2 * (gi - 1), seq // d, s_id * (SPAN // d), d, 1.0)
        deinterleave(v_ref, span_row0, stage + 2 * n_stage, kvd_ref, 2 * (gi - 1) + 1, seq // d,
                     s_id * (SPAN // d), d, 1.0)
        stage += 3 * n_stage

    def dilated_unit(gi, d, m, r, first):
        slot = gi - 1
        loc = BLK * m * d + r
        q_row = r * (SPAN // d) + BLK * m
        cur = pl.multiple_of(r * (seq // d) + s_id * (SPAN // d) + BLK * m, BLK)
        prev = pl.multiple_of(jnp.where(first == 1, cur, cur - BLK), BLK)
        q2 = qd_ref[slot, pl.ds(q_row, BLK), :]
        kk = jnp.concatenate([kvd_ref[2 * slot, pl.ds(prev, BLK), :], kvd_ref[2 * slot, pl.ds(cur, BLK), :]], axis=0)
        vv = jnp.concatenate([kvd_ref[2 * slot + 1, pl.ds(prev, BLK), :],
                              kvd_ref[2 * slot + 1, pl.ds(cur, BLK), :]], axis=0)
        parts = _attn_unit(q2, kk, vv, bias_ref[gi, first, 0], bias_ref[gi, first, 1])
        for j, part in enumerate(parts):
            acc_ref[ACC_PARTS * slot + j, pl.ds(loc, BLK, stride=d), :] = part

    def dense_unit(m):
        loc = m * BLK
        cur = pl.multiple_of(s_id * SPAN + loc, BLK)
        prev = pl.multiple_of(jnp.maximum(cur - BLK, 0), BLK)
        first = jnp.where(cur == 0, 1, 0)
        q2 = q1_ref[loc:loc + BLK, :] * 0.125
        kk = jnp.concatenate([k1_ref[pl.ds(prev, BLK), :], k1_ref[pl.ds(cur, BLK), :]], axis=0)
        vv = jnp.concatenate([v1_ref[pl.ds(prev, BLK), :], v1_ref[pl.ds(cur, BLK), :]], axis=0)
        n1, m1, d1 = _attn_unit(q2, kk, vv, bias_ref[0, first, 0], bias_ref[0, first, 1])
        (n2, m2, d2), (n3, m3, d3) = (
            tuple(acc_ref[g * ACC_PARTS + j, loc:loc + BLK, :] for j in range(ACC_PARTS)) for g in range(2))
        mx = jnp.maximum(jnp.maximum(m1, m2), m3)
        w1, w2, w3 = jnp.exp(m1 - mx), jnp.exp(m2 - mx), jnp.exp(m3 - mx)
        num = w1 * n1 + w2 * n2 + w3 * n3
        den = w1 * d1 + w2 * d2 + w3 * d3
        o_ref[loc:loc + BLK, :] = (num / den).astype(o_ref.dtype)

    first_span = jnp.where(s_id == 0, 1, 0)
    for gi in (2, 1):
        d = ATTN_GROUPS[gi][1]
        for m in range(SPAN // (BLK * d)):
            for r in range(d):
                dilated_unit(gi, d, m, r, first_span if m == 0 else 0)
    for m in range(SPAN // BLK):
        dense_unit(m)


def _attn_bias():
    slopes = np.exp2(-8.0 * np.arange(1, ATTN_HEADS + 1, dtype=np.float64) / ATTN_HEADS)
    qi = np.arange(BLK)[:, None]
    kj = np.arange(2 * BLK)[None, :]
    rel = qi + BLK - kj
    out = np.zeros((N_GROUPS, 2, ATTN_HEADS, BLK, 2 * BLK), np.float32)
    for gi, (window, d) in enumerate(ATTN_GROUPS):
        n_back = window // d
        assert n_back == BLK
        valid = (rel >= 0) & (rel <= n_back)
        bias = -slopes[:, None, None] * (rel * d)[None].astype(np.float64)
        out[gi, 0] = np.where(valid[None], bias, NEG)
        out[gi, 1] = np.where((valid & (kj >= BLK))[None], bias, NEG)
    return jnp.asarray(out)


def _attention(proj, batch, seq, interpret):
    n = batch * seq
    spans = seq // SPAN
    n_hp = GROUP_W // LANES
    qcol = lambda g: (COL_QA + g * GROUP_W) // LANES
    kcol = lambda g: (COL_KA + g * GROUP_W) // LANES
    vcol = lambda g: (COL_VA + g * GROUP_W) // LANES
    q_specs = [pl.BlockSpec((SPAN, LANES), functools.partial(lambda b, hp, s, c: (b * spans + s, c + hp), c=qcol(g)))
               for g in range(N_GROUPS)]
    k_specs = [pl.BlockSpec((seq, LANES), functools.partial(lambda b, hp, s, c: (b, c + hp), c=kcol(g)))
               for g in range(N_GROUPS)]
    v_specs = [pl.BlockSpec((seq, LANES), functools.partial(lambda b, hp, s, c: (b, c + hp), c=vcol(g)))
               for g in range(N_GROUPS)]
    bias_spec = pl.BlockSpec((N_GROUPS, 2, 2, BLK, 2 * BLK), lambda b, hp, s: (0, 0, hp, 0, 0))
    return pl.pallas_call(
        functools.partial(_attn_kernel, seq=seq),
        grid=(batch, n_hp, spans),
        in_specs=q_specs + k_specs + v_specs + [bias_spec],
        out_specs=pl.BlockSpec((SPAN, LANES), lambda b, hp, s: (b * spans + s, hp)),
        out_shape=jax.ShapeDtypeStruct((n, GROUP_W), BF16),
        scratch_shapes=[
            pltpu.VMEM((ATTN_STAGES, SPAN, LANES), F32),
            pltpu.VMEM((2, SPAN, LANES), BF16),
            pltpu.VMEM((4, seq, LANES), BF16),
            pltpu.VMEM((2 * ACC_PARTS, SPAN, LANES), F32),
        ],
        compiler_params=_cparams(("arbitrary", "arbitrary", "arbitrary")),
        interpret=interpret,
        name="attn",
    )(*([proj] * 9), _attn_bias())


def _ret_kernel(q_ref, k_ref, v_ref, gr_ref, dec_ref, xi_ref, zeta_ref, gch_ref, o_ref, st_ref):
    @pl.when(pl.program_id(1) == 0)
    def _():
        st_ref[...] = jnp.zeros_like(st_ref)

    nt = (((1,), (1,)), ((), ()))
    scale = RET_DK ** -0.5

    for c in range(RET_TS // RET_CHUNK):
        rows = pl.ds(c * RET_CHUNK, RET_CHUNK)
        for h in range(RET_HEADS):
            kcols = slice(h * RET_DK, (h + 1) * RET_DK)
            vcols = slice(h * RET_DV, (h + 1) * RET_DV)
            qi = q_ref[rows, kcols]
            kf = k_ref[rows, kcols].astype(F32) * scale
            ki = kf.astype(BF16)
            kz_t = jnp.transpose(kf * zeta_ref[h]).astype(BF16)
            vi = v_ref[rows, vcols]
            att = lax.dot_general(qi, ki, nt, preferred_element_type=F32) * dec_ref[h]
            inner = jnp.dot(att.astype(BF16), vi, preferred_element_type=F32)
            st = st_ref[h]
            cross = jnp.dot(qi, st.astype(BF16), preferred_element_type=F32) * xi_ref[h]
            st_ref[h] = gch_ref[h] * st + jnp.dot(kz_t, vi, preferred_element_type=F32)
            y = inner + cross
            mu = jnp.mean(y, axis=-1, keepdims=True)
            yc = y - mu
            var = jnp.mean(yc * yc, axis=-1, keepdims=True)
            yn = yc * lax.rsqrt(var + EPS)
            g = gr_ref[rows, vcols].astype(F32)
            o_ref[rows, vcols] = (g * _sigmoid(g) * yn).astype(o_ref.dtype)


def _ret_tables():
    c = RET_CHUNK
    log_g = np.log1p(-np.exp2(-5.0 - np.arange(RET_HEADS, dtype=np.float64)))
    pos = np.arange(c, dtype=np.float64)
    diff = pos[:, None] - pos[None, :]
    dec = np.where(diff >= 0, np.exp(log_g[:, None, None] * np.maximum(diff, 0.0)), 0.0)
    xi = np.exp(log_g[:, None] * (pos + 1.0))[..., None] * np.ones((1, 1, RET_DV))
    zeta = np.exp(log_g[:, None] * (c - 1.0 - pos))[..., None] * np.ones((1, 1, RET_DK))
    gch = np.exp(log_g * c)[:, None, None] * np.ones((1, 1, RET_DV))
    return tuple(jnp.asarray(t, F32) for t in (dec, xi, zeta, gch))


def _retention(proj, batch, seq, interpret):
    n = batch * seq
    dec, xi, zeta, gch = _ret_tables()
    qk_w = RET_HEADS * RET_DK
    v_w = RET_HEADS * RET_DV
    nts = seq // RET_TS
    const3 = lambda b, t: (0, 0, 0)
    return pl.pallas_call(
        _ret_kernel,
        grid=(batch, nts),
        in_specs=[
            pl.BlockSpec((RET_TS, qk_w), lambda b, t: (b * nts + t, COL_QR // qk_w)),
            pl.BlockSpec((RET_TS, qk_w), lambda b, t: (b * nts + t, COL_KR // qk_w)),
            pl.BlockSpec((RET_TS, v_w), lambda b, t: (b * nts + t, COL_VR // v_w)),
            pl.BlockSpec((RET_TS, v_w), lambda b, t: (b * nts + t, COL_GR // v_w)),
            pl.BlockSpec((RET_HEADS, RET_CHUNK, RET_CHUNK), const3),
            pl.BlockSpec((RET_HEADS, RET_CHUNK, RET_DV), const3),
            pl.BlockSpec((RET_HEADS, RET_CHUNK, RET_DK), const3),
            pl.BlockSpec((RET_HEADS, 1, RET_DV), const3),
        ],
        out_specs=pl.BlockSpec((RET_TS, v_w), lambda b, t: (b * nts + t, 0)),
        out_shape=jax.ShapeDtypeStruct((n, v_w), BF16),
        scratch_shapes=[pltpu.VMEM((RET_HEADS, RET_DK, RET_DV), F32)],
        compiler_params=_cparams(("arbitrary", "arbitrary")),
        interpret=interpret,
        name="retention",
    )(proj, proj, proj, proj, dec, xi, zeta, gch)


ROUTER_OFF = N_EXPERT_GROUPS


def _pack_bf16_pair(a, b):
    hi = lax.bitcast_convert_type(a.astype(BF16).astype(F32), jnp.uint32)
    lo = lax.bitcast_convert_type(b.astype(BF16).astype(F32), jnp.uint32)
    return lax.bitcast_convert_type(hi | (lo >> 16), jnp.int32)


def _unpack_bf16_pair(w):
    u = lax.bitcast_convert_type(w, jnp.uint32)
    a = lax.bitcast_convert_type(u & jnp.uint32(0xFFFF0000), F32).astype(BF16)
    b = lax.bitcast_convert_type(u << 16, F32).astype(BF16)
    return a, b


def _pack_rows(y):
    q = D_MODEL // 4
    return (_pack_bf16_pair(y[:, 0:q], y[:, 2 * q:3 * q]), _pack_bf16_pair(y[:, q:2 * q], y[:, 3 * q:4 * q]))


def _unpack_rows(slab0, slab1):
    q0, q2 = _unpack_bf16_pair(slab0)
    q1, q3 = _unpack_bf16_pair(slab1)
    return jnp.concatenate([q0, q1, q2, q3], axis=1)


def _mix_kernel(oa_ref, or_ref, ga_ref, gr_ref, x_ref, pa_ref, pr_ref, wo_ref, gf_ref, wr_ref, br_ref,
                x1_ref, h2_ref, route_ref, route_t_ref, cnt_ref, carry_ref):
    @pl.when(pl.program_id(0) == 0)
    def _():
        carry_ref[...] = jnp.zeros_like(carry_ref)

    for c in range(MIX_TM // MIX_CHUNK):
        _mix_rows(pl.ds(c * MIX_CHUNK, MIX_CHUNK), oa_ref, or_ref, ga_ref, gr_ref, x_ref, pa_ref, pr_ref, wo_ref,
                  gf_ref, wr_ref, br_ref, x1_ref, h2_ref, route_ref, route_t_ref, cnt_ref, carry_ref)


def _mix_rows(rows, oa_ref, or_ref, ga_ref, gr_ref, x_ref, pa_ref, pr_ref, wo_ref, gf_ref, wr_ref, br_ref,
              x1_ref, h2_ref, route_ref, route_t_ref, cnt_ref, carry_ref):
    a = jnp.dot(oa_ref[rows, :], pa_ref[...], preferred_element_type=F32)
    r = jnp.dot(or_ref[rows, :], pr_ref[...], preferred_element_type=F32)
    merged = (_sigmoid(ga_ref[rows, :].astype(F32)) * a + _sigmoid(gr_ref[rows, :].astype(F32)) * r)
    x1 = x_ref[rows, :] + jnp.dot(merged.astype(BF16), wo_ref[...], preferred_element_type=F32)
    x1_ref[rows, :] = x1
    ms = jnp.mean(x1 * x1, axis=-1, keepdims=True)
    h2 = x1 * lax.rsqrt(ms + EPS) * gf_ref[...]
    h2_ref[0, rows, :], h2_ref[1, rows, :] = _pack_rows(h2)

    h_hi = h2.astype(BF16)
    h_lo = (h2 - h_hi.astype(F32)).astype(BF16)
    both = jnp.dot(h_hi, wr_ref[...], preferred_element_type=F32)
    logits = (both[:, :LANES] + both[:, LANES:]
              + jnp.dot(h_lo, wr_ref[:, :LANES], preferred_element_type=F32) + br_ref[...])
    tm = logits.shape[0]
    lane = lax.broadcasted_iota(jnp.int32, (tm, LANES), 1).astype(F32)
    big = jnp.float32(4 * LANES)
    ninf = -jnp.inf
    is_g = lane < N_EXPERT_GROUPS
    gl = jnp.where(is_g, logits, ninf)
    gmax = jnp.max(gl, axis=-1, keepdims=True)
    gsum = jnp.sum(jnp.where(is_g, jnp.exp(gl - gmax), 0.0), axis=-1, keepdims=True)
    g_val = 1.0 / gsum
    g_idx = jnp.min(jnp.where(jnp.logical_and(is_g, gl == gmax), lane, big), axis=-1, keepdims=True)
    lo = ROUTER_OFF + EXPERTS_PER_GROUP * g_idx
    in_grp = jnp.logical_and(lane >= lo, lane < lo + EXPERTS_PER_GROUP)
    el = jnp.where(in_grp, logits, ninf)
    v1 = jnp.max(el, axis=-1, keepdims=True)
    i1 = jnp.min(jnp.where(jnp.logical_and(in_grp, el == v1), lane, big), axis=-1, keepdims=True)
    rest = jnp.logical_and(in_grp, lane != i1)
    el2 = jnp.where(rest, logits, ninf)
    v2 = jnp.max(el2, axis=-1, keepdims=True)
    i2 = jnp.min(jnp.where(jnp.logical_and(rest, el2 == v2), lane, big), axis=-1, keepdims=True)
    t = jnp.exp(v2 - v1)
    w1 = g_val / (1.0 + t)
    w2 = g_val * t / (1.0 + t)

    sel = jnp.logical_or(lane == i1, lane == i2)
    sel_bf = jnp.where(sel, 1.0, 0.0).astype(BF16)
    row = lax.broadcasted_iota(jnp.int32, (tm, tm), 0)
    col = lax.broadcasted_iota(jnp.int32, (tm, tm), 1)
    tri = jnp.where(col < row, 1.0, 0.0).astype(BF16)
    before = jnp.dot(tri, sel_bf, preferred_element_type=F32) + carry_ref[...]
    r1 = jnp.sum(jnp.where(lane == i1, before, 0.0), axis=-1, keepdims=True)
    r2 = jnp.sum(jnp.where(lane == i2, before, 0.0), axis=-1, keepdims=True)
    carry = carry_ref[...] + jnp.sum(jnp.where(sel, 1.0, 0.0), axis=0, keepdims=True)
    carry_ref[...] = carry
    cnt_ref[...] = carry

    vals = (i1 - ROUTER_OFF, i2 - ROUTER_OFF, w1, w2, r1, r2)
    route = jnp.zeros((tm, LANES), F32)
    for j, v in enumerate(vals):
        route = jnp.where(lane == j, v, route)
    route_ref[rows, :] = route
    route_t_ref[:, rows] = jnp.transpose(route)[:ROUTE_ROWS, :]


def _mix(o_attn, o_ret, proj, x2, pa, pr, wo, g_ffn, w_router, b_router, interpret):
    n = x2.shape[0]
    tm = MIX_TM
    const = lambda i: (0, 0)
    return pl.pallas_call(
        _mix_kernel,
        grid=(n // tm,),
        in_specs=[
            pl.BlockSpec((tm, GROUP_W), lambda i: (i, 0)),
            pl.BlockSpec((tm, D_MODEL), lambda i: (i, 0)),
            pl.BlockSpec((tm, D_MODEL), lambda i: (i, COL_GATE_A // D_MODEL)),
            pl.BlockSpec((tm, D_MODEL), lambda i: (i, COL_GATE_R // D_MODEL)),
            pl.BlockSpec((tm, D_MODEL), lambda i: (i, 0)),
            pl.BlockSpec((GROUP_W, D_MODEL), const),
            pl.BlockSpec((D_MODEL, D_MODEL), const),
            pl.BlockSpec((D_MODEL, D_MODEL), const),
            pl.BlockSpec((1, D_MODEL), const),
            pl.BlockSpec((D_MODEL, 2 * LANES), const),
            pl.BlockSpec((1, LANES), const),
        ],
        out_specs=[
            pl.BlockSpec((tm, D_MODEL), lambda i: (i, 0)),
            pl.BlockSpec((2, tm, SC_ROW_WORDS), lambda i: (0, i, 0)),
            pl.BlockSpec((tm, LANES), lambda i: (i, 0)),
            pl.BlockSpec((ROUTE_ROWS, tm), lambda i: (0, i)),
            pl.BlockSpec((1, LANES), const),
        ],
        out_shape=[
            jax.ShapeDtypeStruct((n, D_MODEL), F32),
            jax.ShapeDtypeStruct((2, n, SC_ROW_WORDS), jnp.int32),
            jax.ShapeDtypeStruct((n, LANES), F32),
            jax.ShapeDtypeStruct((ROUTE_ROWS, n), F32),
            jax.ShapeDtypeStruct((1, LANES), F32),
        ],
        scratch_shapes=[pltpu.VMEM((1, LANES), F32)],
        compiler_params=_cparams(("arbitrary",)),
        interpret=interpret,
        name="mix_router",
    )(o_attn, o_ret, proj, proj, x2, pa, pr, wo, g_ffn, w_router, b_router)


def _expert_kernel(te_ref, tr_ref, xs_ref, wg_ref, wu_ref, wd_ref, o_ref):
    i = pl.program_id(0)

    @pl.when(tr_ref[i] == i)
    def _():
        xs = _unpack_rows(xs_ref[0], xs_ref[1])
        a = jnp.dot(xs, wg_ref[0], preferred_element_type=F32)
        u = jnp.dot(xs, wu_ref[0], preferred_element_type=F32)
        hid = (a * _sigmoid(a) * u).astype(BF16)
        y = jnp.dot(hid, wd_ref[0], preferred_element_type=F32)
        o_ref[0], o_ref[1] = _pack_rows(y)


def _experts(xs, tile_expert, tile_row, w_gate, w_up, w_down, interpret):
    p = xs.shape[1]
    n_tiles = p // EXP_TM
    grid_spec = pltpu.PrefetchScalarGridSpec(
        num_scalar_prefetch=2,
        grid=(n_tiles,),
        in_specs=[
            pl.BlockSpec((2, EXP_TM, SC_ROW_WORDS), lambda i, te, tr: (0, tr[i], 0)),
            pl.BlockSpec((1, D_MODEL, EXPERT_FF), lambda i, te, tr: (te[i], 0, 0)),
            pl.BlockSpec((1, D_MODEL, EXPERT_FF), lambda i, te, tr: (te[i], 0, 0)),
            pl.BlockSpec((1, EXPERT_FF, D_MODEL), lambda i, te, tr: (te[i], 0, 0)),
        ],
        out_specs=pl.BlockSpec((2, EXP_TM, SC_ROW_WORDS), lambda i, te, tr: (0, tr[i], 0)),
    )
    return pl.pallas_call(
        _expert_kernel,
        grid_spec=grid_spec,
        out_shape=jax.ShapeDtypeStruct((2, p, SC_ROW_WORDS), jnp.int32),
        compiler_params=_cparams(("arbitrary",)),
        interpret=interpret,
        name="experts",
    )(tile_expert, tile_row, xs, w_gate, w_up, w_down)


def _final_kernel(x1_ref, yab_ref, route_ref, g_ref, o_ref):
    route = route_ref[...]
    w1 = route[:, 2:3]
    w2 = route[:, 3:4]
    ya = _unpack_rows(yab_ref[0], yab_ref[2]).astype(F32)
    yb = _unpack_rows(yab_ref[1], yab_ref[3]).astype(F32)
    x2 = x1_ref[...] + w1 * ya + w2 * yb
    ms = jnp.mean(x2 * x2, axis=-1, keepdims=True)
    o_ref[...] = x2 * lax.rsqrt(ms + EPS) * g_ref[...]


def _final(x1, yab, route, g_final, interpret):
    n = x1.shape[0]
    tm = FIN_TM
    row = lambda i: (i, 0)
    return pl.pallas_call(
        _final_kernel,
        grid=(n // tm,),
        in_specs=[
            pl.BlockSpec((tm, D_MODEL), row),
            pl.BlockSpec((4, tm, SC_ROW_WORDS), lambda i: (0, i, 0)),
            pl.BlockSpec((tm, LANES), row),
            pl.BlockSpec((1, D_MODEL), lambda i: (0, 0)),
        ],
        out_specs=pl.BlockSpec((tm, D_MODEL), row),
        out_shape=jax.ShapeDtypeStruct((n, D_MODEL), F32),
        compiler_params=_cparams(("arbitrary",)),
        interpret=interpret,
        name="combine_final",
    )(x1, yab, route, g_final)


def _permute_w_in(w_in):
    splits = np.cumsum([QKV_W, QKV_W, QKV_W, 512, 512, 1024, 1024, D_MODEL, D_MODEL])[:-1].tolist()
    qa, ka, va, qr, kr, vr, gr, gate_a, gate_r = jnp.split(w_in, splits, axis=-1)
    return jnp.concatenate([gate_a, gate_r, vr, gr, qa, ka, va, qr, kr], axis=-1).astype(BF16)


def _dest_kernel(offs_ref, route_t_ref, idx_ref, *, n_rows):
    route_t = route_t_ref[...]
    experts = route_t[0:2, :]
    dest = route_t[4:6, :].astype(jnp.int32)
    for e in range(N_EXPERTS):
        dest = dest + jnp.where(experts == float(e), offs_ref[e], 0)
    idx_ref[0:2, :] = dest
    idx_ref[2:4, :] = dest + n_rows


def _route_plan(route_t, counts, n, interpret):
    cnt = counts[0, ROUTER_OFF:ROUTER_OFF + N_EXPERTS].astype(jnp.int32)
    padded = ((cnt + EXP_TM - 1) // EXP_TM) * EXP_TM
    ends = jnp.cumsum(padded)
    offs = ends - padded
    n_rows = 2 * n + N_EXPERTS * EXP_TM
    idx4 = pl.pallas_call(
        functools.partial(_dest_kernel, n_rows=n_rows),
        grid_spec=pltpu.PrefetchScalarGridSpec(
            num_scalar_prefetch=1, grid=(1,),
            in_specs=[pl.BlockSpec(route_t.shape, lambda i, offs: (0, 0))],
            out_specs=pl.BlockSpec((4, n), lambda i, offs: (0, 0))),
        out_shape=jax.ShapeDtypeStruct((4, n), jnp.int32),
        interpret=interpret,
        name="route_dest",
    )(offs, route_t)
    tile_row = jnp.minimum(jnp.arange(n_rows // EXP_TM, dtype=jnp.int32), ends[-1] // EXP_TM - 1)
    tile_expert = jnp.sum((ends[None, :] <= (tile_row * EXP_TM)[:, None]).astype(jnp.int32), axis=1)
    return idx4, tile_expert, tile_row, n_rows


def _sc_mesh():
    return plsc.VectorSubcoreMesh(core_axis_name="core", subcore_axis_name="subcore")


def _sc_scatter_rows(rows, idx4, n_out):
    n_in, w = rows.shape
    nb = idx4.shape[1] // SC_WINDOW

    @functools.partial(pl.kernel, out_type=jax.ShapeDtypeStruct((n_out, w), rows.dtype), mesh=_sc_mesh(),
                       scratch_types=[], name="sc_scatter_rows")
    def scatter(x_hbm, ia_hbm, ib_hbm, o_hbm):
        def body(x_vmem, ia_vmem, ib_vmem):
            pltpu.sync_copy(x_vmem, o_hbm.at[ia_vmem.at[0]])
            pltpu.sync_copy(x_vmem, o_hbm.at[ib_vmem.at[0]])

        pltpu.emit_pipeline(
            body,
            grid=(n_in // SC_WINDOW,),
            in_specs=[pl.BlockSpec((SC_WINDOW, w), lambda i: (i, 0)),
                      pl.BlockSpec((1, SC_WINDOW), lambda i: (2 * (i // nb), i % nb)),
                      pl.BlockSpec((1, SC_WINDOW), lambda i: (2 * (i // nb) + 1, i % nb))],
            out_specs=[],
            core_axis_name=("core", "subcore"),
            dimension_semantics=(pltpu.PARALLEL,),
        )(x_hbm, ia_hbm, ib_hbm)

    return scatter(rows, idx4, idx4)


def _sc_gather_rows(table, idx4):
    nb = idx4.shape[1] // SC_WINDOW
    n_idx = idx4.shape[0] * idx4.shape[1]
    w = table.shape[1]

    @functools.partial(pl.kernel, out_type=jax.ShapeDtypeStruct((n_idx, w), table.dtype), mesh=_sc_mesh(),
                       scratch_types=[], name="sc_gather_rows")
    def gather(t_hbm, i_hbm, o_hbm):
        def body(i_vmem, o_vmem):
            pltpu.sync_copy(t_hbm.at[i_vmem.at[0]], o_vmem)

        pltpu.emit_pipeline(
            body,
            grid=(n_idx // SC_WINDOW,),
            in_specs=[pl.BlockSpec((1, SC_WINDOW), lambda i: (i // nb, i % nb))],
            out_specs=[pl.BlockSpec((SC_WINDOW, w), lambda i: (i, 0))],
            core_axis_name=("core", "subcore"),
            dimension_semantics=(pltpu.PARALLEL,),
        )(i_hbm, o_hbm)

    return gather(table, idx4)


def _forward(x, g_mix, w_in, w_attn_branch, w_ret_branch, w_out, g_ffn, w_group_router, b_group_router,
             w_expert_router, b_expert_router, w_gate, w_up, w_down, g_final, interpret=False):
    batch, seq, d = x.shape
    n = batch * seq
    x2 = x.reshape(n, d)
    proj, wg_bf, wu_bf, wd_bf = _proj(x2, g_mix[0][None, :], _permute_w_in(w_in[0]), w_gate[0], w_up[0],
                                      w_down[0], interpret)
    o_attn = _attention(proj, batch, seq, interpret)
    o_ret = _retention(proj, batch, seq, interpret)
    pad = LANES - N_EXPERT_GROUPS - N_EXPERTS
    w_router = jnp.concatenate([w_group_router[0], w_expert_router[0], jnp.zeros((d, pad), F32)], axis=-1)
    w_router_hi = w_router.astype(BF16)
    w_router_lo = (w_router - w_router_hi.astype(F32)).astype(BF16)
    w_router2 = jnp.concatenate([w_router_hi, w_router_lo], axis=-1)
    b_router = jnp.concatenate([b_group_router[0], b_expert_router[0], jnp.zeros((pad,), F32)])[None, :]
    x1, h2p, route, route_t, counts = _mix(o_attn, o_ret, proj, x2, w_attn_branch[0].astype(BF16),
                                           w_ret_branch[0].astype(BF16), w_out[0].astype(BF16),
                                           g_ffn[0][None, :], w_router2, b_router, interpret)
    idx4, tile_expert, tile_row, n_rows = _route_plan(route_t, counts, n, interpret)
    xs = _sc_scatter_rows(h2p.reshape(2 * n, SC_ROW_WORDS), idx4, 2 * n_rows)
    ys = _experts(xs.reshape(2, n_rows, SC_ROW_WORDS), tile_expert, tile_row, wg_bf, wu_bf, wd_bf, interpret)
    yab = _sc_gather_rows(ys.reshape(2 * n_rows, SC_ROW_WORDS), idx4)
    out = _final(x1, yab.reshape(4, n, SC_ROW_WORDS), route, g_final[None, :], interpret)
    return out.reshape(batch, seq, d)


def kernel(x, g_mix, w_in, w_attn_branch, w_ret_branch, w_out, g_ffn, w_group_router, b_group_router,
           w_expert_router, b_expert_router, w_gate, w_up, w_down, g_final):
    return _forward(x, g_mix, w_in, w_attn_branch, w_ret_branch, w_out, g_ffn, w_group_router,
                    b_group_router, w_expert_router, b_expert_router, w_gate, w_up, w_down, g_final)
```

```python
import functools

import numpy as np
import jax
import jax.numpy as jnp
from jax import lax
from jax.experimental import pallas as pl
from jax.experimental.pallas import tpu as pltpu
from jax.experimental.pallas import tpu_sc as plsc

F32 = jnp.float32
BF16 = jnp.bfloat16

D_MODEL = 1024
ATTN_GROUPS = ((128, 1), (512, 4), (2048, 16))
N_GROUPS = len(ATTN_GROUPS)
ATTN_HEADS = 8
HEAD_DIM = 64
GROUP_W = ATTN_HEADS * HEAD_DIM
QKV_W = N_GROUPS * GROUP_W
RET_HEADS = 4
RET_DK = 128
RET_DV = 256
RET_CHUNK = 128
RET_TS = 1024
N_EXPERT_GROUPS = 4
EXPERTS_PER_GROUP = 8
N_EXPERTS = N_EXPERT_GROUPS * EXPERTS_PER_GROUP
EXPERT_FF = 512
EPS = 1e-6

LANES = 128
BLK = 128
SPAN = 2048
NEG = -1e30
ACC_PARTS = 3
ATTN_STAGES = 9

COL_GATE_A = 0
COL_GATE_R = 1024
COL_VR = 2048
COL_GR = COL_VR + RET_HEADS * RET_DV
COL_QA = COL_GR + RET_HEADS * RET_DV
COL_KA = COL_QA + QKV_W
COL_VA = COL_KA + QKV_W
COL_QR = COL_VA + QKV_W
COL_KR = COL_QR + RET_HEADS * RET_DK
IN_WIDTH = COL_KR + RET_HEADS * RET_DK

PROJ_TM = 512
PROJ_TN = IN_WIDTH // 2
MXU_N = 256
MIX_TM = 1024
MIX_CHUNK = 512
ROUTE_ROWS = 8
EXP_TM = 512
SC_WINDOW = 128
SC_ROW_WORDS = 256
FIN_TM = 512
FIN_PARTS = 2
VMEM_LIMIT = 56 * 1024 * 1024


def _cparams(sem):
    return pltpu.CompilerParams(dimension_semantics=sem, vmem_limit_bytes=VMEM_LIMIT)


def _sigmoid(x):
    return 0.5 * jnp.tanh(0.5 * x) + 0.5


def _proj_kernel(x_ref, g_ref, w_ref, wg_ref, wu_ref, wd_ref, o_ref, wg_o, wu_o, wd_o):
    x = x_ref[...]
    ms = jnp.mean(x * x, axis=-1, keepdims=True)
    h = (x * lax.rsqrt(ms + EPS) * g_ref[...]).astype(BF16)
    for c in range(PROJ_TN // MXU_N):
        sl = slice(c * MXU_N, (c + 1) * MXU_N)
        o_ref[:, sl] = jnp.dot(h, w_ref[:, sl], preferred_element_type=F32).astype(o_ref.dtype)
    wg_o[...] = wg_ref[...].astype(BF16)
    wu_o[...] = wu_ref[...].astype(BF16)
    wd_o[...] = wd_ref[...].astype(BF16)


def _proj(x2, g_mix, w_in_bf16, w_gate, w_up, w_down, interpret):
    n = x2.shape[0]
    n_i = n // PROJ_TM
    steps = (IN_WIDTH // PROJ_TN) * n_i
    flat = [w.reshape(-1, w.shape[-1]) for w in (w_gate, w_up, w_down)]
    w_specs = [pl.BlockSpec((w.shape[0] // steps, w.shape[1]), lambda j, i: (j * n_i + i, 0)) for w in flat]
    outs = pl.pallas_call(
        _proj_kernel,
        grid=(IN_WIDTH // PROJ_TN, n_i),
        in_specs=[
            pl.BlockSpec((PROJ_TM, D_MODEL), lambda j, i: (i, 0)),
            pl.BlockSpec((1, D_MODEL), lambda j, i: (0, 0)),
            pl.BlockSpec((D_MODEL, PROJ_TN), lambda j, i: (0, j)),
        ] + w_specs,
        out_specs=[pl.BlockSpec((PROJ_TM, PROJ_TN), lambda j, i: (i, j))] + w_specs,
        out_shape=[jax.ShapeDtypeStruct((n, IN_WIDTH), BF16)]
        + [jax.ShapeDtypeStruct(w.shape, BF16) for w in flat],
        compiler_params=_cparams(("arbitrary", "arbitrary")),
        interpret=interpret,
        name="proj",
    )(x2, g_mix, w_in_bf16, *flat)
    return outs[0], outs[1].reshape(w_gate.shape), outs[2].reshape(w_up.shape), outs[3].reshape(w_down.shape)


def _attn_unit(q2, kk, vv, bias_a, bias_b):
    lane = lax.broadcasted_iota(jnp.int32, (BLK, LANES), 1)
    left = lane < HEAD_DIM
    zero = jnp.zeros_like(q2)
    nt = (((1,), (1,)), ((), ()))
    q_st = jnp.concatenate([jnp.where(left, q2, zero), jnp.where(left, zero, q2)], axis=0)
    s = lax.dot_general(q_st, kk, nt, preferred_element_type=F32) + jnp.concatenate([bias_a, bias_b], axis=0)
    m = jnp.max(s, axis=-1, keepdims=True)
    p = jnp.exp(s - m)
    den = jnp.sum(p, axis=-1, keepdims=True)
    o = jnp.dot(p.astype(BF16), vv, preferred_element_type=F32)
    return (jnp.where(left, o[:BLK], o[BLK:]), jnp.where(left, m[:BLK], m[BLK:]),
            jnp.where(left, den[:BLK], den[BLK:]))


def _attn_kernel(q1_ref, q2_ref, q3_ref, k1_ref, k2_ref, k3_ref, v1_ref, v2_ref, v3_ref,
                 bias_ref, o_ref, tmp_ref, qd_ref, kvd_ref, acc_ref, *, seq):
    s_id = pl.program_id(2)
    step = ATTN_GROUPS[1][1]

    def deinterleave(src_ref, src_row0, stage, dst_ref, dst_slot, res_pitch, dst_off, d, scale):
        chunk = min(256, SPAN // d)
        quarter = SPAN // step
        for c0 in range(0, SPAN, 256):
            x = src_ref[pl.ds(pl.multiple_of(src_row0 + c0, 256), 256), :].astype(F32)
            tmp_ref[stage, c0:c0 + 256, :] = x if scale == 1.0 else x * scale

        def strided_pass(src_stage, base, length, put):
            for b in range(step):
                for j0 in range(0, length // step, chunk):
                    put(b, j0, tmp_ref[src_stage, pl.ds(base + b + step * j0, chunk, stride=step), :])

        def put_out(res, j0, rows):
            row0 = res * res_pitch + j0 + dst_off
            if not isinstance(row0, int):
                row0 = pl.multiple_of(row0, chunk)
            dst_ref[dst_slot, pl.ds(row0, chunk), :] = rows.astype(BF16)

        if d == step:
            strided_pass(stage, 0, SPAN, put_out)
        else:
            def put_mid(b, j0, rows):
                tmp_ref[stage + 1, b * quarter + j0:b * quarter + j0 + chunk, :] = rows
            strided_pass(stage, 0, SPAN, put_mid)
            for b in range(step):
                strided_pass(stage + 1, b * quarter, quarter,
                             functools.partial(lambda a, j0, rows, b: put_out(a * step + b, j0, rows), b=b))

    span_row0 = pl.multiple_of(s_id * SPAN, SPAN)
    stage = 0
    for gi, (q_ref, k_ref, v_ref) in ((1, (q2_ref, k2_ref, v2_ref)), (2, (q3_ref, k3_ref, v3_ref))):
        d = ATTN_GROUPS[gi][1]
        n_stage = 1 if d == step else 2
        deinterleave(q_ref, 0, stage, qd_ref, gi - 1, SPAN // d, 0, d, 0.125)
        deinterleave(k_ref, span_row0, stage + n_stage, kvd_ref, 2 * (gi - 1), seq // d, s_id * (SPAN // d), d, 1.0)
        deinterleave(v_ref, span_row0, stage + 2 * n_stage, kvd_ref, 2 * (gi - 1) + 1, seq // d,
                     s_id * (SPAN // d), d, 1.0)
        stage += 3 * n_stage

    def dilated_unit(gi, d, m, r, first):
        slot = gi - 1
        loc = BLK * m * d + r
        q_row = r * (SPAN // d) + BLK * m
        cur = pl.multiple_of(r * (seq // d) + s_id * (SPAN // d) + BLK * m, BLK)
        prev = pl.multiple_of(jnp.where(first == 1, cur, cur - BLK), BLK)
        q2 = qd_ref[slot, pl.ds(q_row, BLK), :]
        kk = jnp.concatenate([kvd_ref[2 * slot, pl.ds(prev, BLK), :], kvd_ref[2 * slot, pl.ds(cur, BLK), :]], axis=0)
        vv = jnp.concatenate([kvd_ref[2 * slot + 1, pl.ds(prev, BLK), :],
                              kvd_ref[2 * slot + 1, pl.ds(cur, BLK), :]], axis=0)
        parts = _attn_unit(q2, kk, vv, bias_ref[gi, first, 0], bias_ref[gi, first, 1])
        for j, part in enumerate(parts):
            acc_ref[ACC_PARTS * slot + j, pl.ds(loc, BLK, stride=d), :] = part

    def dense_unit(m):
        loc = m * BLK
        cur = pl.multiple_of(s_id * SPAN + loc, BLK)
        prev = pl.multiple_of(jnp.maximum(cur - BLK, 0), BLK)
        first = jnp.where(cur == 0, 1, 0)
        q2 = q1_ref[loc:loc + BLK, :] * 0.125
        kk = jnp.concatenate([k1_ref[pl.ds(prev, BLK), :], k1_ref[pl.ds(cur, BLK), :]], axis=0)
        vv = jnp.concatenate([v1_ref[pl.ds(prev, BLK), :], v1_ref[pl.ds(cur, BLK), :]], axis=0)
        n1, m1, d1 = _attn_unit(q2, kk, vv, bias_ref[0, first, 0], bias_ref[0, first, 1])
        (n2, m2, d2), (n3, m3, d3) = (
            tuple(acc_ref[g * ACC_PARTS + j, loc:loc + BLK, :] for j in range(ACC_PARTS)) for g in range(2))
        mx = jnp.maximum(jnp.maximum(m1, m2), m3)
        w1, w2, w3 = jnp.exp(m1 - mx), jnp.exp(m2 - mx), jnp.exp(m3 - mx)
        num = w1 * n1 + w2 * n2 + w3 * n3
        den = w1 * d1 + w2 * d2 + w3 * d3
        o_ref[loc:loc + BLK, :] = (num / den).astype(o_ref.dtype)

    first_span = jnp.where(s_id == 0, 1, 0)
    for gi in (2, 1):
        d = ATTN_GROUPS[gi][1]
        for m in range(SPAN // (BLK * d)):
            for r in range(d):
                dilated_unit(gi, d, m, r, first_span if m == 0 else 0)
    for m in range(SPAN // BLK):
        dense_unit(m)


def _attn_bias():
    slopes = np.exp2(-8.0 * np.arange(1, ATTN_HEADS + 1, dtype=np.float64) / ATTN_HEADS)
    qi = np.arange(BLK)[:, None]
    kj = np.arange(2 * BLK)[None, :]
    rel = qi + BLK - kj
    out = np.zeros((N_GROUPS, 2, ATTN_HEADS, BLK, 2 * BLK), np.float32)
    for gi, (window, d) in enumerate(ATTN_GROUPS):
        n_back = window // d
        assert n_back == BLK
        valid = (rel >= 0) & (rel <= n_back)
        bias = -slopes[:, None, None] * (rel * d)[None].astype(np.float64)
        out[gi, 0] = np.where(valid[None], bias, NEG)
        out[gi, 1] = np.where((valid & (kj >= BLK))[None], bias, NEG)
    return jnp.asarray(out)


def _attention(proj, batch, seq, interpret):
    n = batch * seq
    spans = seq // SPAN
    n_hp = GROUP_W // LANES
    qcol = lambda g: (COL_QA + g * GROUP_W) // LANES
    kcol = lambda g: (COL_KA + g * GROUP_W) // LANES
    vcol = lambda g: (COL_VA + g * GROUP_W) // LANES
    q_specs = [pl.BlockSpec((SPAN, LANES), functools.partial(lambda b, hp, s, c: (b * spans + s, c + hp), c=qcol(g)))
               for g in range(N_GROUPS)]
    k_specs = [pl.BlockSpec((seq, LANES), functools.partial(lambda b, hp, s, c: (b, c + hp), c=kcol(g)))
               for g in range(N_GROUPS)]
    v_specs = [pl.BlockSpec((seq, LANES), functools.partial(lambda b, hp, s, c: (b, c + hp), c=vcol(g)))
               for g in range(N_GROUPS)]
    bias_spec = pl.BlockSpec((N_GROUPS, 2, 2, BLK, 2 * BLK), lambda b, hp, s: (0, 0, hp, 0, 0))
    return pl.pallas_call(
        functools.partial(_attn_kernel, seq=seq),
        grid=(batch, n_hp, spans),
        in_specs=q_specs + k_specs + v_specs + [bias_spec],
        out_specs=pl.BlockSpec((SPAN, LANES), lambda b, hp, s: (b * spans + s, hp)),
        out_shape=jax.ShapeDtypeStruct((n, GROUP_W), BF16),
        scratch_shapes=[
            pltpu.VMEM((ATTN_STAGES, SPAN, LANES), F32),
            pltpu.VMEM((2, SPAN, LANES), BF16),
            pltpu.VMEM((4, seq, LANES), BF16),
            pltpu.VMEM((2 * ACC_PARTS, SPAN, LANES), F32),
        ],
        compiler_params=_cparams(("arbitrary", "arbitrary", "arbitrary")),
        interpret=interpret,
        name="attn",
    )(*([proj] * 9), _attn_bias())


def _ret_kernel(q_ref, k_ref, v_ref, gr_ref, dec_ref, xi_ref, zeta_ref, gch_ref, o_ref, st_ref):
    @pl.when(pl.program_id(1) == 0)
    def _():
        st_ref[...] = jnp.zeros_like(st_ref)

    nt = (((1,), (1,)), ((), ()))
    scale = RET_DK ** -0.5

    for c in range(RET_TS // RET_CHUNK):
        rows = pl.ds(c * RET_CHUNK, RET_CHUNK)
        for h in range(RET_HEADS):
            kcols = slice(h * RET_DK, (h + 1) * RET_DK)
            vcols = slice(h * RET_DV, (h + 1) * RET_DV)
            qi = q_ref[rows, kcols]
            kf = k_ref[rows, kcols].astype(F32) * scale
            ki = kf.astype(BF16)
            kz_t = jnp.transpose(kf * zeta_ref[h]).astype(BF16)
            vi = v_ref[rows, vcols]
            att = lax.dot_general(qi, ki, nt, preferred_element_type=F32) * dec_ref[h]
            inner = jnp.dot(att.astype(BF16), vi, preferred_element_type=F32)
            st = st_ref[h]
            cross = jnp.dot(qi, st.astype(BF16), preferred_element_type=F32) * xi_ref[h]
            st_ref[h] = gch_ref[h] * st + jnp.dot(kz_t, vi, preferred_element_type=F32)
            y = inner + cross
            mu = jnp.mean(y, axis=-1, keepdims=True)
            yc = y - mu
            var = jnp.mean(yc * yc, axis=-1, keepdims=True)
            yn = yc * lax.rsqrt(var + EPS)
            g = gr_ref[rows, vcols].astype(F32)
            o_ref[rows, vcols] = (g * _sigmoid(g) * yn).astype(o_ref.dtype)


def _ret_tables():
    c = RET_CHUNK
    log_g = np.log1p(-np.exp2(-5.0 - np.arange(RET_HEADS, dtype=np.float64)))
    pos = np.arange(c, dtype=np.float64)
    diff = pos[:, None] - pos[None, :]
    dec = np.where(diff >= 0, np.exp(log_g[:, None, None] * np.maximum(diff, 0.0)), 0.0)
    xi = np.exp(log_g[:, None] * (pos + 1.0))[..., None] * np.ones((1, 1, RET_DV))
    zeta = np.exp(log_g[:, None] * (c - 1.0 - pos))[..., None] * np.ones((1, 1, RET_DK))
    gch = np.exp(log_g * c)[:, None, None] * np.ones((1, 1, RET_DV))
    return tuple(jnp.asarray(t, F32) for t in (dec, xi, zeta, gch))


def _retention(proj, batch, seq, interpret):
    n = batch * seq
    dec, xi, zeta, gch = _ret_tables()
    qk_w = RET_HEADS * RET_DK
    v_w = RET_HEADS * RET_DV
    nts = seq // RET_TS
    const3 = lambda b, t: (0, 0, 0)
    return pl.pallas_call(
        _ret_kernel,
        grid=(batch, nts),
        in_specs=[
            pl.BlockSpec((RET_TS, qk_w), lambda b, t: (b * nts + t, COL_QR // qk_w)),
            pl.BlockSpec((RET_TS, qk_w), lambda b, t: (b * nts + t, COL_KR // qk_w)),
            pl.BlockSpec((RET_TS, v_w), lambda b, t: (b * nts + t, COL_VR // v_w)),
            pl.BlockSpec((RET_TS, v_w), lambda b, t: (b * nts + t, COL_GR // v_w)),
            pl.BlockSpec((RET_HEADS, RET_CHUNK, RET_CHUNK), const3),
            pl.BlockSpec((RET_HEADS, RET_CHUNK, RET_DV), const3),
            pl.BlockSpec((RET_HEADS, RET_CHUNK, RET_DK), const3),
            pl.BlockSpec((RET_HEADS, 1, RET_DV), const3),
        ],
        out_specs=pl.BlockSpec((RET_TS, v_w), lambda b, t: (b * nts + t, 0)),
        out_shape=jax.ShapeDtypeStruct((n, v_w), BF16),
        scratch_shapes=[pltpu.VMEM((RET_HEADS, RET_DK, RET_DV), F32)],
        compiler_params=_cparams(("arbitrary", "arbitrary")),
        interpret=interpret,
        name="retention",
    )(proj, proj, proj, proj, dec, xi, zeta, gch)


ROUTER_OFF = N_EXPERT_GROUPS


def _pack_bf16_pair(a, b):
    hi = lax.bitcast_convert_type(a.astype(BF16).astype(F32), jnp.uint32)
    lo = lax.bitcast_convert_type(b.astype(BF16).astype(F32), jnp.uint32)
    return lax.bitcast_convert_type(hi | (lo >> 16), jnp.int32)


def _unpack_bf16_pair(w):
    u = lax.bitcast_convert_type(w, jnp.uint32)
    a = lax.bitcast_convert_type(u & jnp.uint32(0xFFFF0000), F32).astype(BF16)
    b = lax.bitcast_convert_type(u << 16, F32).astype(BF16)
    return a, b


def _pack_rows(y):
    q = D_MODEL // 4
    return (_pack_bf16_pair(y[:, 0:q], y[:, 2 * q:3 * q]), _pack_bf16_pair(y[:, q:2 * q], y[:, 3 * q:4 * q]))


def _unpack_rows(slab0, slab1):
    q0, q2 = _unpack_bf16_pair(slab0)
    q1, q3 = _unpack_bf16_pair(slab1)
    return jnp.concatenate([q0, q1, q2, q3], axis=1)


def _mix_kernel(oa_ref, or_ref, ga_ref, gr_ref, x_ref, pa_ref, pr_ref, wo_ref, gf_ref, wr_ref, br_ref,
                x1_ref, h2_ref, route_ref, route_t_ref, cnt_ref, carry_ref):
    @pl.when(pl.program_id(0) == 0)
    def _():
        carry_ref[...] = jnp.zeros_like(carry_ref)

    for c in range(MIX_TM // MIX_CHUNK):
        _mix_rows(pl.ds(c * MIX_CHUNK, MIX_CHUNK), oa_ref, or_ref, ga_ref, gr_ref, x_ref, pa_ref, pr_ref, wo_ref,
                  gf_ref, wr_ref, br_ref, x1_ref, h2_ref, route_ref, route_t_ref, cnt_ref, carry_ref)


def _mix_rows(rows, oa_ref, or_ref, ga_ref, gr_ref, x_ref, pa_ref, pr_ref, wo_ref, gf_ref, wr_ref, br_ref,
              x1_ref, h2_ref, route_ref, route_t_ref, cnt_ref, carry_ref):
    a = jnp.dot(oa_ref[rows, :], pa_ref[...], preferred_element_type=F32)
    r = jnp.dot(or_ref[rows, :], pr_ref[...], preferred_element_type=F32)
    merged = (_sigmoid(ga_ref[rows, :].astype(F32)) * a + _sigmoid(gr_ref[rows, :].astype(F32)) * r)
    x1 = x_ref[rows, :] + jnp.dot(merged.astype(BF16), wo_ref[...], preferred_element_type=F32)
    x1_ref[rows, :] = x1
    ms = jnp.mean(x1 * x1, axis=-1, keepdims=True)
    h2 = x1 * lax.rsqrt(ms + EPS) * gf_ref[...]
    h2_ref[0, rows, :], h2_ref[1, rows, :] = _pack_rows(h2)

    h_hi = h2.astype(BF16)
    h_lo = (h2 - h_hi.astype(F32)).astype(BF16)
    both = jnp.dot(h_hi, wr_ref[...], preferred_element_type=F32)
    logits = (both[:, :LANES] + both[:, LANES:]
              + jnp.dot(h_lo, wr_ref[:, :LANES], preferred_element_type=F32) + br_ref[...])
    tm = logits.shape[0]
    lane = lax.broadcasted_iota(jnp.int32, (tm, LANES), 1).astype(F32)
    big = jnp.float32(4 * LANES)
    ninf = -jnp.inf
    is_g = lane < N_EXPERT_GROUPS
    gl = jnp.where(is_g, logits, ninf)
    gmax = jnp.max(gl, axis=-1, keepdims=True)
    gsum = jnp.sum(jnp.where(is_g, jnp.exp(gl - gmax), 0.0), axis=-1, keepdims=True)
    g_val = 1.0 / gsum
    g_idx = jnp.min(jnp.where(jnp.logical_and(is_g, gl == gmax), lane, big), axis=-1, keepdims=True)
    lo = ROUTER_OFF + EXPERTS_PER_GROUP * g_idx
    in_grp = jnp.logical_and(lane >= lo, lane < lo + EXPERTS_PER_GROUP)
    el = jnp.where(in_grp, logits, ninf)
    v1 = jnp.max(el, axis=-1, keepdims=True)
    i1 = jnp.min(jnp.where(jnp.logical_and(in_grp, el == v1), lane, big), axis=-1, keepdims=True)
    rest = jnp.logical_and(in_grp, lane != i1)
    el2 = jnp.where(rest, logits, ninf)
    v2 = jnp.max(el2, axis=-1, keepdims=True)
    i2 = jnp.min(jnp.where(jnp.logical_and(rest, el2 == v2), lane, big), axis=-1, keepdims=True)
    t = jnp.exp(v2 - v1)
    w1 = g_val / (1.0 + t)
    w2 = g_val * t / (1.0 + t)

    sel = jnp.logical_or(lane == i1, lane == i2)
    sel_bf = jnp.where(sel, 1.0, 0.0).astype(BF16)
    row = lax.broadcasted_iota(jnp.int32, (tm, tm), 0)
    col = lax.broadcasted_iota(jnp.int32, (tm, tm), 1)
    tri = jnp.where(col < row, 1.0, 0.0).astype(BF16)
    before = jnp.dot(tri, sel_bf, preferred_element_type=F32) + carry_ref[...]
    r1 = jnp.sum(jnp.where(lane == i1, before, 0.0), axis=-1, keepdims=True)
    r2 = jnp.sum(jnp.where(lane == i2, before, 0.0), axis=-1, keepdims=True)
    carry = carry_ref[...] + jnp.sum(jnp.where(sel, 1.0, 0.0), axis=0, keepdims=True)
    carry_ref[...] = carry
    cnt_ref[...] = carry

    vals = (i1 - ROUTER_OFF, i2 - ROUTER_OFF, w1, w2, r1, r2)
    route = jnp.zeros((tm, LANES), F32)
    for j, v in enumerate(vals):
        route = jnp.where(lane == j, v, route)
    route_ref[rows, :] = route
    route_t_ref[:, rows] = jnp.transpose(route)[:ROUTE_ROWS, :]


def _mix(o_attn, o_ret, proj, x2, pa, pr, wo, g_ffn, w_router, b_router, interpret):
    n = x2.shape[0]
    tm = MIX_TM
    const = lambda i: (0, 0)
    return pl.pallas_call(
        _mix_kernel,
        grid=(n // tm,),
        in_specs=[
            pl.BlockSpec((tm, GROUP_W), lambda i: (i, 0)),
            pl.BlockSpec((tm, D_MODEL), lambda i: (i, 0)),
            pl.BlockSpec((tm, D_MODEL), lambda i: (i, COL_GATE_A // D_MODEL)),
            pl.BlockSpec((tm, D_MODEL), lambda i: (i, COL_GATE_R // D_MODEL)),
            pl.BlockSpec((tm, D_MODEL), lambda i: (i, 0)),
            pl.BlockSpec((GROUP_W, D_MODEL), const),
            pl.BlockSpec((D_MODEL, D_MODEL), const),
            pl.BlockSpec((D_MODEL, D_MODEL), const),
            pl.BlockSpec((1, D_MODEL), const),
            pl.BlockSpec((D_MODEL, 2 * LANES), const),
            pl.BlockSpec((1, LANES), const),
        ],
        out_specs=[
            pl.BlockSpec((tm, D_MODEL), lambda i: (i, 0)),
            pl.BlockSpec((2, tm, SC_ROW_WORDS), lambda i: (0, i, 0)),
            pl.BlockSpec((tm, LANES), lambda i: (i, 0)),
            pl.BlockSpec((ROUTE_ROWS, tm), lambda i: (0, i)),
            pl.BlockSpec((1, LANES), const),
        ],
        out_shape=[
            jax.ShapeDtypeStruct((n, D_MODEL), F32),
            jax.ShapeDtypeStruct((2, n, SC_ROW_WORDS), jnp.int32),
            jax.ShapeDtypeStruct((n, LANES), F32),
            jax.ShapeDtypeStruct((ROUTE_ROWS, n), F32),
            jax.ShapeDtypeStruct((1, LANES), F32),
        ],
        scratch_shapes=[pltpu.VMEM((1, LANES), F32)],
        compiler_params=_cparams(("arbitrary",)),
        interpret=interpret,
        name="mix_router",
    )(o_attn, o_ret, proj, proj, x2, pa, pr, wo, g_ffn, w_router, b_router)


def _expert_kernel(te_ref, tr_ref, xs_ref, wg_ref, wu_ref, wd_ref, o_ref):
    i = pl.program_id(0)

    @pl.when(tr_ref[i] == i)
    def _():
        xs = _unpack_rows(xs_ref[0], xs_ref[1])
        a = jnp.dot(xs, wg_ref[0], preferred_element_type=F32)
        u = jnp.dot(xs, wu_ref[0], preferred_element_type=F32)
        hid = (a * _sigmoid(a) * u).astype(BF16)
        y = jnp.dot(hid, wd_ref[0], preferred_element_type=F32)
        o_ref[0], o_ref[1] = _pack_rows(y)


def _experts(xs, tile_expert, tile_row, w_gate, w_up, w_down, interpret):
    p = xs.shape[1]
    n_tiles = p // EXP_TM
    grid_spec = pltpu.PrefetchScalarGridSpec(
        num_scalar_prefetch=2,
        grid=(n_tiles,),
        in_specs=[
            pl.BlockSpec((2, EXP_TM, SC_ROW_WORDS), lambda i, te, tr: (0, tr[i], 0)),
            pl.BlockSpec((1, D_MODEL, EXPERT_FF), lambda i, te, tr: (te[i], 0, 0)),
            pl.BlockSpec((1, D_MODEL, EXPERT_FF), lambda i, te, tr: (te[i], 0, 0)),
            pl.BlockSpec((1, EXPERT_FF, D_MODEL), lambda i, te, tr: (te[i], 0, 0)),
        ],
        out_specs=pl.BlockSpec((2, EXP_TM, SC_ROW_WORDS), lambda i, te, tr: (0, tr[i], 0)),
    )
    return pl.pallas_call(
        _expert_kernel,
        grid_spec=grid_spec,
        out_shape=jax.ShapeDtypeStruct((2, p, SC_ROW_WORDS), jnp.int32),
        compiler_params=_cparams(("arbitrary",)),
        interpret=interpret,
        name="experts",
    )(tile_expert, tile_row, xs, w_gate, w_up, w_down)


def _final_kernel(x1_ref, yab_ref, route_ref, g_ref, *rest):
    o_ref = rest[-1]
    route = route_ref[...]
    w1 = route[:, 2:3]
    w2 = route[:, 3:4]
    ya = _unpack_rows(yab_ref[0], yab_ref[2]).astype(F32)
    yb = _unpack_rows(yab_ref[1], yab_ref[3]).astype(F32)
    x2 = x1_ref[...] + w1 * ya + w2 * yb
    ms = jnp.mean(x2 * x2, axis=-1, keepdims=True)
    o_ref[...] = x2 * lax.rsqrt(ms + EPS) * g_ref[...]


def _final(x1, yab, route, g_final, prev_out, part, interpret):
    n = x1.shape[0]
    tm = FIN_TM
    steps = n // (tm * FIN_PARTS)
    row = lambda i: (part * steps + i, 0)
    return pl.pallas_call(
        _final_kernel,
        grid=(steps,),
        in_specs=[
            pl.BlockSpec((tm, D_MODEL), row),
            pl.BlockSpec((4, tm, SC_ROW_WORDS), lambda i: (0, i, 0)),
            pl.BlockSpec((tm, LANES), row),
            pl.BlockSpec((1, D_MODEL), lambda i: (0, 0)),
        ] + ([] if prev_out is None else [pl.BlockSpec(memory_space=pl.ANY)]),
        out_specs=pl.BlockSpec((tm, D_MODEL), row),
        out_shape=jax.ShapeDtypeStruct((n, D_MODEL), F32),
        input_output_aliases={} if prev_out is None else {4: 0},
        compiler_params=_cparams(("arbitrary",)),
        interpret=interpret,
        name="combine_final",
    )(x1, yab, route, g_final, *([] if prev_out is None else [prev_out]))


def _permute_w_in(w_in):
    splits = np.cumsum([QKV_W, QKV_W, QKV_W, 512, 512, 1024, 1024, D_MODEL, D_MODEL])[:-1].tolist()
    qa, ka, va, qr, kr, vr, gr, gate_a, gate_r = jnp.split(w_in, splits, axis=-1)
    return jnp.concatenate([gate_a, gate_r, vr, gr, qa, ka, va, qr, kr], axis=-1).astype(BF16)


def _dest_kernel(offs_ref, route_t_ref, idx_ref, *, n_rows):
    route_t = route_t_ref[...]
    experts = route_t[0:2, :]
    dest = route_t[4:6, :].astype(jnp.int32)
    for e in range(N_EXPERTS):
        dest = dest + jnp.where(experts == float(e), offs_ref[e], 0)
    idx_ref[0:2, :] = dest
    idx_ref[2:4, :] = dest + n_rows


def _route_plan(route_t, counts, n, interpret):
    cnt = counts[0, ROUTER_OFF:ROUTER_OFF + N_EXPERTS].astype(jnp.int32)
    padded = ((cnt + EXP_TM - 1) // EXP_TM) * EXP_TM
    ends = jnp.cumsum(padded)
    offs = ends - padded
    n_rows = 2 * n + N_EXPERTS * EXP_TM
    idx4 = pl.pallas_call(
        functools.partial(_dest_kernel, n_rows=n_rows),
        grid_spec=pltpu.PrefetchScalarGridSpec(
            num_scalar_prefetch=1, grid=(1,),
            in_specs=[pl.BlockSpec(route_t.shape, lambda i, offs: (0, 0))],
            out_specs=pl.BlockSpec((4, n), lambda i, offs: (0, 0))),
        out_shape=jax.ShapeDtypeStruct((4, n), jnp.int32),
        interpret=interpret,
        name="route_dest",
    )(offs, route_t)
    tile_row = jnp.minimum(jnp.arange(n_rows // EXP_TM, dtype=jnp.int32), ends[-1] // EXP_TM - 1)
    tile_expert = jnp.sum((ends[None, :] <= (tile_row * EXP_TM)[:, None]).astype(jnp.int32), axis=1)
    return idx4, tile_expert, tile_row, n_rows


def _sc_mesh():
    return plsc.VectorSubcoreMesh(core_axis_name="core", subcore_axis_name="subcore")


def _sc_scatter_rows(rows, idx4, n_out):
    n_in, w = rows.shape
    nb = idx4.shape[1] // SC_WINDOW

    @functools.partial(pl.kernel, out_type=jax.ShapeDtypeStruct((n_out, w), rows.dtype), mesh=_sc_mesh(),
                       scratch_types=[], name="sc_scatter_rows")
    def scatter(x_hbm, ia_hbm, ib_hbm, o_hbm):
        def body(x_vmem, ia_vmem, ib_vmem):
            pltpu.sync_copy(x_vmem, o_hbm.at[ia_vmem.at[0]])
            pltpu.sync_copy(x_vmem, o_hbm.at[ib_vmem.at[0]])

        pltpu.emit_pipeline(
            body,
            grid=(n_in // SC_WINDOW,),
            in_specs=[pl.BlockSpec((SC_WINDOW, w), lambda i: (i, 0)),
                      pl.BlockSpec((1, SC_WINDOW), lambda i: (2 * (i // nb), i % nb)),
                      pl.BlockSpec((1, SC_WINDOW), lambda i: (2 * (i // nb) + 1, i % nb))],
            out_specs=[],
            core_axis_name=("core", "subcore"),
            dimension_semantics=(pltpu.PARALLEL,),
        )(x_hbm, ia_hbm, ib_hbm)

    return scatter(rows, idx4, idx4)


def _sc_gather_rows(table, idx4, part):
    nb = idx4.shape[1] // (SC_WINDOW * FIN_PARTS)
    n_idx = idx4.shape[0] * nb * SC_WINDOW
    w = table.shape[1]

    @functools.partial(pl.kernel, out_type=jax.ShapeDtypeStruct((n_idx, w), table.dtype), mesh=_sc_mesh(),
                       scratch_types=[], name="sc_gather_rows")
    def gather(t_hbm, i_hbm, o_hbm):
        def body(i_vmem, o_vmem):
            pltpu.sync_copy(t_hbm.at[i_vmem.at[0]], o_vmem)

        pltpu.emit_pipeline(
            body,
            grid=(n_idx // SC_WINDOW,),
            in_specs=[pl.BlockSpec((1, SC_WINDOW), lambda i: (i // nb, part * nb + i % nb))],
            out_specs=[pl.BlockSpec((SC_WINDOW, w), lambda i: (i, 0))],
            core_axis_name=("core", "subcore"),
            dimension_semantics=(pltpu.PARALLEL,),
        )(i_hbm, o_hbm)

    return gather(table, idx4)


def _forward(x, g_mix, w_in, w_attn_branch, w_ret_branch, w_out, g_ffn, w_group_router, b_group_router,
             w_expert_router, b_expert_router, w_gate, w_up, w_down, g_final, interpret=False):
    batch, seq, d = x.shape
    n = batch * seq
    x2 = x.reshape(n, d)
    proj, wg_bf, wu_bf, wd_bf = _proj(x2, g_mix[0][None, :], _permute_w_in(w_in[0]), w_gate[0], w_up[0],
                                      w_down[0], interpret)
    o_attn = _attention(proj, batch, seq, interpret)
    o_ret = _retention(proj, batch, seq, interpret)
    pad = LANES - N_EXPERT_GROUPS - N_EXPERTS
    w_router = jnp.concatenate([w_group_router[0], w_expert_router[0], jnp.zeros((d, pad), F32)], axis=-1)
    w_router_hi = w_router.astype(BF16)
    w_router_lo = (w_router - w_router_hi.astype(F32)).astype(BF16)
    w_router2 = jnp.concatenate([w_router_hi, w_router_lo], axis=-1)
    b_router = jnp.concatenate([b_group_router[0], b_expert_router[0], jnp.zeros((pad,), F32)])[None, :]
    x1, h2p, route, route_t, counts = _mix(o_attn, o_ret, proj, x2, w_attn_branch[0].astype(BF16),
                                           w_ret_branch[0].astype(BF16), w_out[0].astype(BF16),
                                           g_ffn[0][None, :], w_router2, b_router, interpret)
    idx4, tile_expert, tile_row, n_rows = _route_plan(route_t, counts, n, interpret)
    xs = _sc_scatter_rows(h2p.reshape(2 * n, SC_ROW_WORDS), idx4, 2 * n_rows)
    ys = _experts(xs.reshape(2, n_rows, SC_ROW_WORDS), tile_expert, tile_row, wg_bf, wu_bf, wd_bf, interpret)
    ys_rows = ys.reshape(2 * n_rows, SC_ROW_WORDS)
    out = None
    for part in range(FIN_PARTS):
        yab = _sc_gather_rows(ys_rows, idx4, part)
        out = _final(x1, yab.reshape(4, n // FIN_PARTS, SC_ROW_WORDS), route, g_final[None, :], out, part, interpret)
    return out.reshape(batch, seq, d)


def kernel(x, g_mix, w_in, w_attn_branch, w_ret_branch, w_out, g_ffn, w_group_router, b_group_router,
           w_expert_router, b_expert_router, w_gate, w_up, w_down, g_final):
    return _forward(x, g_mix, w_in, w_attn_branch, w_ret_branch, w_out, g_ffn, w_group_router,
                    b_group_router, w_expert_router, b_expert_router, w_gate, w_up, w_down, g_final)
```

```python
import functools

import numpy as np
import jax
import jax.numpy as jnp
from jax import lax
from jax.experimental import pallas as pl
from jax.experimental.pallas import tpu as pltpu
from jax.experimental.pallas import tpu_sc as plsc

F32 = jnp.float32
BF16 = jnp.bfloat16

D_MODEL = 1024
ATTN_GROUPS = ((128, 1), (512, 4), (2048, 16))
N_GROUPS = len(ATTN_GROUPS)
ATTN_HEADS = 8
HEAD_DIM = 64
GROUP_W = ATTN_HEADS * HEAD_DIM
QKV_W = N_GROUPS * GROUP_W
RET_HEADS = 4
RET_DK = 128
RET_DV = 256
RET_CHUNK = 128
RET_TS = 1024
N_EXPERT_GROUPS = 4
EXPERTS_PER_GROUP = 8
N_EXPERTS = N_EXPERT_GROUPS * EXPERTS_PER_GROUP
EXPERT_FF = 512
EPS = 1e-6

LANES = 128
BLK = 128
SPAN = 2048
NEG = -1e30
ACC_PARTS = 3
ATTN_STAGES = 9

COL_GATE_A = 0
COL_GATE_R = 1024
COL_VR = 2048
COL_GR = COL_VR + RET_HEADS * RET_DV
COL_QA = COL_GR + RET_HEADS * RET_DV
COL_KA = COL_QA + QKV_W
COL_VA = COL_KA + QKV_W
COL_QR = COL_VA + QKV_W
COL_KR = COL_QR + RET_HEADS * RET_DK
IN_WIDTH = COL_KR + RET_HEADS * RET_DK

PROJ_TM = 512
PROJ_TN = IN_WIDTH // 2
MXU_N = 256
MIX_TM = 1024
MIX_CHUNK = 512
ROUTE_CHUNK = 256
ROUTE_ROWS = 8
EXP_TM = 512
SC_WINDOW = 128
SC_ROW_WORDS = 256
FIN_TM = 512
FIN_PARTS = 1
VMEM_LIMIT = 56 * 1024 * 1024


def _cparams(sem):
    return pltpu.CompilerParams(dimension_semantics=sem, vmem_limit_bytes=VMEM_LIMIT)


def _sigmoid(x):
    return 0.5 * jnp.tanh(0.5 * x) + 0.5


def _proj_kernel(x_ref, g_ref, w_ref, wg_ref, wu_ref, wd_ref, o_ref, wg_o, wu_o, wd_o):
    x = x_ref[...]
    ms = jnp.mean(x * x, axis=-1, keepdims=True)
    h = (x * lax.rsqrt(ms + EPS) * g_ref[...]).astype(BF16)
    for c in range(PROJ_TN // MXU_N):
        sl = slice(c * MXU_N, (c + 1) * MXU_N)
        o_ref[:, sl] = jnp.dot(h, w_ref[:, sl], preferred_element_type=F32).astype(o_ref.dtype)
    wg_o[...] = wg_ref[...].astype(BF16)
    wu_o[...] = wu_ref[...].astype(BF16)
    wd_o[...] = wd_ref[...].astype(BF16)


def _proj(x2, g_mix, w_in_bf16, w_gate, w_up, w_down, interpret):
    n = x2.shape[0]
    n_i = n // PROJ_TM
    steps = (IN_WIDTH // PROJ_TN) * n_i
    flat = [w.reshape(-1, w.shape[-1]) for w in (w_gate, w_up, w_down)]
    w_specs = [pl.BlockSpec((w.shape[0] // steps, w.shape[1]), lambda j, i: (j * n_i + i, 0)) for w in flat]
    outs = pl.pallas_call(
        _proj_kernel,
        grid=(IN_WIDTH // PROJ_TN, n_i),
        in_specs=[
            pl.BlockSpec((PROJ_TM, D_MODEL), lambda j, i: (i, 0)),
            pl.BlockSpec((1, D_MODEL), lambda j, i: (0, 0)),
            pl.BlockSpec((D_MODEL, PROJ_TN), lambda j, i: (0, j)),
        ] + w_specs,
        out_specs=[pl.BlockSpec((PROJ_TM, PROJ_TN), lambda j, i: (i, j))] + w_specs,
        out_shape=[jax.ShapeDtypeStruct((n, IN_WIDTH), BF16)]
        + [jax.ShapeDtypeStruct(w.shape, BF16) for w in flat],
        compiler_params=_cparams(("arbitrary", "arbitrary")),
        interpret=interpret,
        name="proj",
    )(x2, g_mix, w_in_bf16, *flat)
    return outs[0], outs[1].reshape(w_gate.shape), outs[2].reshape(w_up.shape), outs[3].reshape(w_down.shape)


def _attn_unit(q2, kk, vv, bias_a, bias_b):
    lane = lax.broadcasted_iota(jnp.int32, (BLK, LANES), 1)
    left = lane < HEAD_DIM
    zero = jnp.zeros_like(q2)
    nt = (((1,), (1,)), ((), ()))
    q_st = jnp.concatenate([jnp.where(left, q2, zero), jnp.where(left, zero, q2)], axis=0)
    s = lax.dot_general(q_st, kk, nt, preferred_element_type=F32) + jnp.concatenate([bias_a, bias_b], axis=0)
    m = jnp.max(s, axis=-1, keepdims=True)
    p = jnp.exp(s - m)
    den = jnp.sum(p, axis=-1, keepdims=True)
    o = jnp.dot(p.astype(BF16), vv, preferred_element_type=F32)
    return (jnp.where(left, o[:BLK], o[BLK:]), jnp.where(left, m[:BLK], m[BLK:]),
            jnp.where(left, den[:BLK], den[BLK:]))


def _attn_kernel(q1_ref, q2_ref, q3_ref, k1_ref, k2_ref, k3_ref, v1_ref, v2_ref, v3_ref,
                 bias_ref, o_ref, tmp_ref, qd_ref, kvd_ref, acc_ref, *, seq):
    s_id = pl.program_id(2)
    step = ATTN_GROUPS[1][1]

    def deinterleave(src_ref, src_row0, stage, dst_ref, dst_slot, res_pitch, dst_off, d, scale):
        chunk = min(256, SPAN // d)
        quarter = SPAN // step
        for c0 in range(0, SPAN, 256):
            x = src_ref[pl.ds(pl.multiple_of(src_row0 + c0, 256), 256), :].astype(F32)
            tmp_ref[stage, c0:c0 + 256, :] = x if scale == 1.0 else x * scale

        def strided_pass(src_stage, base, length, put):
            for b in range(step):
                for j0 in range(0, length // step, chunk):
                    put(b, j0, tmp_ref[src_stage, pl.ds(base + b + step * j0, chunk, stride=step), :])

        def put_out(res, j0, rows):
            row0 = res * res_pitch + j0 + dst_off
            if not isinstance(row0, int):
                row0 = pl.multiple_of(row0, chunk)
            dst_ref[dst_slot, pl.ds(row0, chunk), :] = rows.astype(BF16)

        if d == step:
            strided_pass(stage, 0, SPAN, put_out)
        else:
            def put_mid(b, j0, rows):
                tmp_ref[stage + 1, b * quarter + j0:b * quarter + j0 + chunk, :] = rows
            strided_pass(stage, 0, SPAN, put_mid)
            for b in range(step):
                strided_pass(stage + 1, b * quarter, quarter,
                             functools.partial(lambda a, j0, rows, b: put_out(a * step + b, j0, rows), b=b))

    span_row0 = pl.multiple_of(s_id * SPAN, SPAN)
    stage = 0
    for gi, (q_ref, k_ref, v_ref) in ((1, (q2_ref, k2_ref, v2_ref)), (2, (q3_ref, k3_ref, v3_ref))):
        d = ATTN_GROUPS[gi][1]
        n_stage = 1 if d == step else 2
        deinterleave(q_ref, 0, stage, qd_ref, gi - 1, SPAN // d, 0, d, 0.125)
        deinterleave(k_ref, span_row0, stage + n_stage, kvd_ref, 2 * (gi - 1), seq // d, s_id * (SPAN // d), d, 1.0)
        deinterleave(v_ref, span_row0, stage + 2 * n_stage, kvd_ref, 2 * (gi - 1) + 1, seq // d,
                     s_id * (SPAN // d), d, 1.0)
        stage += 3 * n_stage

    def dilated_unit(gi, d, m, r, first):
        slot = gi - 1
        loc = BLK * m * d + r
        q_row = r * (SPAN // d) + BLK * m
        cur = pl.multiple_of(r * (seq // d) + s_id * (SPAN // d) + BLK * m, BLK)
        prev = pl.multiple_of(jnp.where(first == 1, cur, cur - BLK), BLK)
        q2 = qd_ref[slot, pl.ds(q_row, BLK), :]
        kk = jnp.concatenate([kvd_ref[2 * slot, pl.ds(prev, BLK), :], kvd_ref[2 * slot, pl.ds(cur, BLK), :]], axis=0)
        vv = jnp.concatenate([kvd_ref[2 * slot + 1, pl.ds(prev, BLK), :],
                              kvd_ref[2 * slot + 1, pl.ds(cur, BLK), :]], axis=0)
        parts = _attn_unit(q2, kk, vv, bias_ref[gi, first, 0], bias_ref[gi, first, 1])
        for j, part in enumerate(parts):
            acc_ref[ACC_PARTS * slot + j, pl.ds(loc, BLK, stride=d), :] = part

    def dense_unit(m):
        loc = m * BLK
        cur = pl.multiple_of(s_id * SPAN + loc, BLK)
        prev = pl.multiple_of(jnp.maximum(cur - BLK, 0), BLK)
        first = jnp.where(cur == 0, 1, 0)
        q2 = q1_ref[loc:loc + BLK, :] * 0.125
        kk = jnp.concatenate([k1_ref[pl.ds(prev, BLK), :], k1_ref[pl.ds(cur, BLK), :]], axis=0)
        vv = jnp.concatenate([v1_ref[pl.ds(prev, BLK), :], v1_ref[pl.ds(cur, BLK), :]], axis=0)
        n1, m1, d1 = _attn_unit(q2, kk, vv, bias_ref[0, first, 0], bias_ref[0, first, 1])
        (n2, m2, d2), (n3, m3, d3) = (
            tuple(acc_ref[g * ACC_PARTS + j, loc:loc + BLK, :] for j in range(ACC_PARTS)) for g in range(2))
        mx = jnp.maximum(jnp.maximum(m1, m2), m3)
        w1, w2, w3 = jnp.exp(m1 - mx), jnp.exp(m2 - mx), jnp.exp(m3 - mx)
        num = w1 * n1 + w2 * n2 + w3 * n3
        den = w1 * d1 + w2 * d2 + w3 * d3
        o_ref[loc:loc + BLK, :] = (num / den).astype(o_ref.dtype)

    first_span = jnp.where(s_id == 0, 1, 0)
    for gi in (2, 1):
        d = ATTN_GROUPS[gi][1]
        for m in range(SPAN // (BLK * d)):
            for r in range(d):
                dilated_unit(gi, d, m, r, first_span if m == 0 else 0)
    for m in range(SPAN // BLK):
        dense_unit(m)


def _attn_bias():
    slopes = np.exp2(-8.0 * np.arange(1, ATTN_HEADS + 1, dtype=np.float64) / ATTN_HEADS)
    qi = np.arange(BLK)[:, None]
    kj = np.arange(2 * BLK)[None, :]
    rel = qi + BLK - kj
    out = np.zeros((N_GROUPS, 2, ATTN_HEADS, BLK, 2 * BLK), np.float32)
    for gi, (window, d) in enumerate(ATTN_GROUPS):
        n_back = window // d
        assert n_back == BLK
        valid = (rel >= 0) & (rel <= n_back)
        bias = -slopes[:, None, None] * (rel * d)[None].astype(np.float64)
        out[gi, 0] = np.where(valid[None], bias, NEG)
        out[gi, 1] = np.where((valid & (kj >= BLK))[None], bias, NEG)
    return jnp.asarray(out)


def _attention(proj, batch, seq, interpret):
    n = batch * seq
    spans = seq // SPAN
    n_hp = GROUP_W // LANES
    qcol = lambda g: (COL_QA + g * GROUP_W) // LANES
    kcol = lambda g: (COL_KA + g * GROUP_W) // LANES
    vcol = lambda g: (COL_VA + g * GROUP_W) // LANES
    q_specs = [pl.BlockSpec((SPAN, LANES), functools.partial(lambda b, hp, s, c: (b * spans + s, c + hp), c=qcol(g)))
               for g in range(N_GROUPS)]
    k_specs = [pl.BlockSpec((seq, LANES), functools.partial(lambda b, hp, s, c: (b, c + hp), c=kcol(g)))
               for g in range(N_GROUPS)]
    v_specs = [pl.BlockSpec((seq, LANES), functools.partial(lambda b, hp, s, c: (b, c + hp), c=vcol(g)))
               for g in range(N_GROUPS)]
    bias_spec = pl.BlockSpec((N_GROUPS, 2, 2, BLK, 2 * BLK), lambda b, hp, s: (0, 0, hp, 0, 0))
    return pl.pallas_call(
        functools.partial(_attn_kernel, seq=seq),
        grid=(batch, n_hp, spans),
        in_specs=q_specs + k_specs + v_specs + [bias_spec],
        out_specs=pl.BlockSpec((SPAN, LANES), lambda b, hp, s: (b * spans + s, hp)),
        out_shape=jax.ShapeDtypeStruct((n, GROUP_W), BF16),
        scratch_shapes=[
            pltpu.VMEM((ATTN_STAGES, SPAN, LANES), F32),
            pltpu.VMEM((2, SPAN, LANES), BF16),
            pltpu.VMEM((4, seq, LANES), BF16),
            pltpu.VMEM((2 * ACC_PARTS, SPAN, LANES), F32),
        ],
        compiler_params=_cparams(("arbitrary", "arbitrary", "arbitrary")),
        interpret=interpret,
        name="attn",
    )(*([proj] * 9), _attn_bias())


def _ret_kernel(q_ref, k_ref, v_ref, gr_ref, dec_ref, xi_ref, zeta_ref, gch_ref, o_ref, st_ref):
    @pl.when(pl.program_id(1) == 0)
    def _():
        st_ref[...] = jnp.zeros_like(st_ref)

    nt = (((1,), (1,)), ((), ()))
    scale = RET_DK ** -0.5

    for c in range(RET_TS // RET_CHUNK):
        rows = pl.ds(c * RET_CHUNK, RET_CHUNK)
        for h in range(RET_HEADS):
            kcols = slice(h * RET_DK, (h + 1) * RET_DK)
            vcols = slice(h * RET_DV, (h + 1) * RET_DV)
            qi = q_ref[rows, kcols]
            kf = k_ref[rows, kcols].astype(F32) * scale
            ki = kf.astype(BF16)
            kz_t = jnp.transpose(kf * zeta_ref[h]).astype(BF16)
            vi = v_ref[rows, vcols]
            att = lax.dot_general(qi, ki, nt, preferred_element_type=F32) * dec_ref[h]
            inner = jnp.dot(att.astype(BF16), vi, preferred_element_type=F32)
            st = st_ref[h]
            cross = jnp.dot(qi, st.astype(BF16), preferred_element_type=F32) * xi_ref[h]
            st_ref[h] = gch_ref[h] * st + jnp.dot(kz_t, vi, preferred_element_type=F32)
            y = inner + cross
            mu = jnp.mean(y, axis=-1, keepdims=True)
            yc = y - mu
            var = jnp.mean(yc * yc, axis=-1, keepdims=True)
            yn = yc * lax.rsqrt(var + EPS)
            g = gr_ref[rows, vcols].astype(F32)
            o_ref[rows, vcols] = (g * _sigmoid(g) * yn).astype(o_ref.dtype)


def _ret_tables():
    c = RET_CHUNK
    log_g = np.log1p(-np.exp2(-5.0 - np.arange(RET_HEADS, dtype=np.float64)))
    pos = np.arange(c, dtype=np.float64)
    diff = pos[:, None] - pos[None, :]
    dec = np.where(diff >= 0, np.exp(log_g[:, None, None] * np.maximum(diff, 0.0)), 0.0)
    xi = np.exp(log_g[:, None] * (pos + 1.0))[..., None] * np.ones((1, 1, RET_DV))
    zeta = np.exp(log_g[:, None] * (c - 1.0 - pos))[..., None] * np.ones((1, 1, RET_DK))
    gch = np.exp(log_g * c)[:, None, None] * np.ones((1, 1, RET_DV))
    return tuple(jnp.asarray(t, F32) for t in (dec, xi, zeta, gch))


def _retention(proj, batch, seq, interpret):
    n = batch * seq
    dec, xi, zeta, gch = _ret_tables()
    qk_w = RET_HEADS * RET_DK
    v_w = RET_HEADS * RET_DV
    nts = seq // RET_TS
    const3 = lambda b, t: (0, 0, 0)
    return pl.pallas_call(
        _ret_kernel,
        grid=(batch, nts),
        in_specs=[
            pl.BlockSpec((RET_TS, qk_w), lambda b, t: (b * nts + t, COL_QR // qk_w)),
            pl.BlockSpec((RET_TS, qk_w), lambda b, t: (b * nts + t, COL_KR // qk_w)),
            pl.BlockSpec((RET_TS, v_w), lambda b, t: (b * nts + t, COL_VR // v_w)),
            pl.BlockSpec((RET_TS, v_w), lambda b, t: (b * nts + t, COL_GR // v_w)),
            pl.BlockSpec((RET_HEADS, RET_CHUNK, RET_CHUNK), const3),
            pl.BlockSpec((RET_HEADS, RET_CHUNK, RET_DV), const3),
            pl.BlockSpec((RET_HEADS, RET_CHUNK, RET_DK), const3),
            pl.BlockSpec((RET_HEADS, 1, RET_DV), const3),
        ],
        out_specs=pl.BlockSpec((RET_TS, v_w), lambda b, t: (b * nts + t, 0)),
        out_shape=jax.ShapeDtypeStruct((n, v_w), BF16),
        scratch_shapes=[pltpu.VMEM((RET_HEADS, RET_DK, RET_DV), F32)],
        compiler_params=_cparams(("arbitrary", "arbitrary")),
        interpret=interpret,
        name="retention",
    )(proj, proj, proj, proj, dec, xi, zeta, gch)


ROUTER_OFF = N_EXPERT_GROUPS


def _pack_bf16_pair(a, b):
    hi = lax.bitcast_convert_type(a.astype(BF16).astype(F32), jnp.uint32)
    lo = lax.bitcast_convert_type(b.astype(BF16).astype(F32), jnp.uint32)
    return lax.bitcast_convert_type(hi | (lo >> 16), jnp.int32)


def _unpack_bf16_pair(w):
    u = lax.bitcast_convert_type(w, jnp.uint32)
    a = lax.bitcast_convert_type(u & jnp.uint32(0xFFFF0000), F32).astype(BF16)
    b = lax.bitcast_convert_type(u << 16, F32).astype(BF16)
    return a, b


def _pack_rows(y):
    q = D_MODEL // 4
    return (_pack_bf16_pair(y[:, 0:q], y[:, 2 * q:3 * q]), _pack_bf16_pair(y[:, q:2 * q], y[:, 3 * q:4 * q]))


def _unpack_rows(slab0, slab1):
    q0, q2 = _unpack_bf16_pair(slab0)
    q1, q3 = _unpack_bf16_pair(slab1)
    return jnp.concatenate([q0, q1, q2, q3], axis=1)


def _mix_kernel(oa_ref, or_ref, ga_ref, gr_ref, x_ref, pa_ref, pr_ref, wo_ref, gf_ref, wr_ref, br_ref,
                x1_ref, h2_ref, route_ref, route_t_ref, cnt_ref, carry_ref, logit_ref):
    step = pl.program_id(0)

    @pl.when(step == 0)
    def _():
        carry_ref[...] = jnp.zeros_like(carry_ref)
        logit_ref[...] = jnp.zeros_like(logit_ref)

    routing = iter([functools.partial(_route_rows, pl.ds(c * ROUTE_CHUNK, ROUTE_CHUNK), step > 0, logit_ref,
                                      route_ref, route_t_ref, cnt_ref, carry_ref)
                    for c in range(MIX_TM // ROUTE_CHUNK)])
    for c in range(MIX_TM // MIX_CHUNK):
        for _ in _mix_rows(pl.ds(c * MIX_CHUNK, MIX_CHUNK), oa_ref, or_ref, ga_ref, gr_ref, x_ref, pa_ref, pr_ref,
                           wo_ref, gf_ref, wr_ref, br_ref, x1_ref, h2_ref, logit_ref):
            next(routing, lambda: None)()
    for piece in routing:
        piece()


def _mix_rows(rows, oa_ref, or_ref, ga_ref, gr_ref, x_ref, pa_ref, pr_ref, wo_ref, gf_ref, wr_ref, br_ref,
              x1_ref, h2_ref, logit_ref):
    a = jnp.dot(oa_ref[rows, :], pa_ref[...], preferred_element_type=F32)
    yield
    r = jnp.dot(or_ref[rows, :], pr_ref[...], preferred_element_type=F32)
    yield
    merged = (_sigmoid(ga_ref[rows, :].astype(F32)) * a + _sigmoid(gr_ref[rows, :].astype(F32)) * r)
    x1 = x_ref[rows, :] + jnp.dot(merged.astype(BF16), wo_ref[...], preferred_element_type=F32)
    x1_ref[rows, :] = x1
    ms = jnp.mean(x1 * x1, axis=-1, keepdims=True)
    h2 = x1 * lax.rsqrt(ms + EPS) * gf_ref[...]
    h2_ref[0, rows, :], h2_ref[1, rows, :] = _pack_rows(h2)

    h_hi = h2.astype(BF16)
    h_lo = (h2 - h_hi.astype(F32)).astype(BF16)
    both = jnp.dot(h_hi, wr_ref[...], preferred_element_type=F32)
    logit_ref[rows, :] = (both[:, :LANES] + both[:, LANES:]
                          + jnp.dot(h_lo, wr_ref[:, :LANES], preferred_element_type=F32) + br_ref[...])


def _route_rows(rows, live, logit_ref, route_ref, route_t_ref, cnt_ref, carry_ref):
    logits = logit_ref[rows, :]
    tm = logits.shape[0]
    lane = lax.broadcasted_iota(jnp.int32, (tm, LANES), 1).astype(F32)
    big = jnp.float32(4 * LANES)
    ninf = -jnp.inf
    is_g = lane < N_EXPERT_GROUPS
    gl = jnp.where(is_g, logits, ninf)
    gmax = jnp.max(gl, axis=-1, keepdims=True)
    gsum = jnp.sum(jnp.where(is_g, jnp.exp(gl - gmax), 0.0), axis=-1, keepdims=True)
    g_val = 1.0 / gsum
    g_idx = jnp.min(jnp.where(jnp.logical_and(is_g, gl == gmax), lane, big), axis=-1, keepdims=True)
    lo = ROUTER_OFF + EXPERTS_PER_GROUP * g_idx
    in_grp = jnp.logical_and(lane >= lo, lane < lo + EXPERTS_PER_GROUP)
    el = jnp.where(in_grp, logits, ninf)
    v1 = jnp.max(el, axis=-1, keepdims=True)
    i1 = jnp.min(jnp.where(jnp.logical_and(in_grp, el == v1), lane, big), axis=-1, keepdims=True)
    rest = jnp.logical_and(in_grp, lane != i1)
    el2 = jnp.where(rest, logits, ninf)
    v2 = jnp.max(el2, axis=-1, keepdims=True)
    i2 = jnp.min(jnp.where(jnp.logical_and(rest, el2 == v2), lane, big), axis=-1, keepdims=True)
    t = jnp.exp(v2 - v1)
    w1 = g_val / (1.0 + t)
    w2 = g_val * t / (1.0 + t)

    sel = jnp.logical_or(lane == i1, lane == i2)
    sel_bf = jnp.where(sel, 1.0, 0.0).astype(BF16)
    row = lax.broadcasted_iota(jnp.int32, (tm, tm), 0)
    col = lax.broadcasted_iota(jnp.int32, (tm, tm), 1)
    tri = jnp.where(col < row, 1.0, 0.0).astype(BF16)
    before = jnp.dot(tri, sel_bf, preferred_element_type=F32) + carry_ref[...]
    r1 = jnp.sum(jnp.where(lane == i1, before, 0.0), axis=-1, keepdims=True)
    r2 = jnp.sum(jnp.where(lane == i2, before, 0.0), axis=-1, keepdims=True)
    carry = carry_ref[...] + jnp.where(live, jnp.sum(jnp.where(sel, 1.0, 0.0), axis=0, keepdims=True), 0.0)
    carry_ref[...] = carry
    cnt_ref[...] = carry

    vals = (i1 - ROUTER_OFF, i2 - ROUTER_OFF, w1, w2, r1, r2)
    route = jnp.zeros((tm, LANES), F32)
    for j, v in enumerate(vals):
        route = jnp.where(lane == j, v, route)
    route_ref[rows, :] = route
    route_t_ref[:, rows] = jnp.transpose(route)[:ROUTE_ROWS, :]


def _mix(o_attn, o_ret, proj, x2, pa, pr, wo, g_ffn, w_router, b_router, interpret):
    n = x2.shape[0]
    tm = MIX_TM
    last = n // tm - 1
    const = lambda i: (0, 0)
    cur = lambda i: jnp.minimum(i, last)
    prev = lambda i: jnp.maximum(i - 1, 0)
    return pl.pallas_call(
        _mix_kernel,
        grid=(n // tm + 1,),
        in_specs=[
            pl.BlockSpec((tm, GROUP_W), lambda i: (cur(i), 0)),
            pl.BlockSpec((tm, D_MODEL), lambda i: (cur(i), 0)),
            pl.BlockSpec((tm, D_MODEL), lambda i: (cur(i), COL_GATE_A // D_MODEL)),
            pl.BlockSpec((tm, D_MODEL), lambda i: (cur(i), COL_GATE_R // D_MODEL)),
            pl.BlockSpec((tm, D_MODEL), lambda i: (cur(i), 0)),
            pl.BlockSpec((GROUP_W, D_MODEL), const),
            pl.BlockSpec((D_MODEL, D_MODEL), const),
            pl.BlockSpec((D_MODEL, D_MODEL), const),
            pl.BlockSpec((1, D_MODEL), const),
            pl.BlockSpec((D_MODEL, 2 * LANES), const),
            pl.BlockSpec((1, LANES), const),
        ],
        out_specs=[
            pl.BlockSpec((tm, D_MODEL), lambda i: (cur(i), 0)),
            pl.BlockSpec((2, tm, SC_ROW_WORDS), lambda i: (0, cur(i), 0)),
            pl.BlockSpec((tm, LANES), lambda i: (prev(i), 0)),
            pl.BlockSpec((ROUTE_ROWS, tm), lambda i: (0, prev(i))),
            pl.BlockSpec((1, LANES), const),
        ],
        out_shape=[
            jax.ShapeDtypeStruct((n, D_MODEL), F32),
            jax.ShapeDtypeStruct((2, n, SC_ROW_WORDS), jnp.int32),
            jax.ShapeDtypeStruct((n, LANES), F32),
            jax.ShapeDtypeStruct((ROUTE_ROWS, n), F32),
            jax.ShapeDtypeStruct((1, LANES), F32),
        ],
        scratch_shapes=[pltpu.VMEM((1, LANES), F32), pltpu.VMEM((tm, LANES), F32)],
        compiler_params=_cparams(("arbitrary",)),
        interpret=interpret,
        name="mix_router",
    )(o_attn, o_ret, proj, proj, x2, pa, pr, wo, g_ffn, w_router, b_router)


def _expert_kernel(te_ref, tr_ref, xs_ref, wg_ref, wu_ref, wd_ref, o_ref):
    i = pl.program_id(0)

    @pl.when(tr_ref[i] == i)
    def _():
        xs = _unpack_rows(xs_ref[0], xs_ref[1])
        a = jnp.dot(xs, wg_ref[0], preferred_element_type=F32)
        u = jnp.dot(xs, wu_ref[0], preferred_element_type=F32)
        hid = (a * _sigmoid(a) * u).astype(BF16)
        y = jnp.dot(hid, wd_ref[0], preferred_element_type=F32)
        o_ref[0], o_ref[1] = _pack_rows(y)


def _experts(xs, tile_expert, tile_row, w_gate, w_up, w_down, interpret):
    p = xs.shape[1]
    n_tiles = p // EXP_TM
    grid_spec = pltpu.PrefetchScalarGridSpec(
        num_scalar_prefetch=2,
        grid=(n_tiles,),
        in_specs=[
            pl.BlockSpec((2, EXP_TM, SC_ROW_WORDS), lambda i, te, tr: (0, tr[i], 0)),
            pl.BlockSpec((1, D_MODEL, EXPERT_FF), lambda i, te, tr: (te[i], 0, 0)),
            pl.BlockSpec((1, D_MODEL, EXPERT_FF), lambda i, te, tr: (te[i], 0, 0)),
            pl.BlockSpec((1, EXPERT_FF, D_MODEL), lambda i, te, tr: (te[i], 0, 0)),
        ],
        out_specs=pl.BlockSpec((2, EXP_TM, SC_ROW_WORDS), lambda i, te, tr: (0, tr[i], 0)),
    )
    return pl.pallas_call(
        _expert_kernel,
        grid_spec=grid_spec,
        out_shape=jax.ShapeDtypeStruct((2, p, SC_ROW_WORDS), jnp.int32),
        compiler_params=_cparams(("arbitrary",)),
        interpret=interpret,
        name="experts",
    )(tile_expert, tile_row, xs, w_gate, w_up, w_down)


def _final_kernel(x1_ref, yab_ref, route_ref, g_ref, *rest):
    o_ref = rest[-1]
    route = route_ref[...]
    w1 = route[:, 2:3]
    w2 = route[:, 3:4]
    ya = _unpack_rows(yab_ref[0], yab_ref[2]).astype(F32)
    yb = _unpack_rows(yab_ref[1], yab_ref[3]).astype(F32)
    x2 = x1_ref[...] + w1 * ya + w2 * yb
    ms = jnp.mean(x2 * x2, axis=-1, keepdims=True)
    o_ref[...] = x2 * lax.rsqrt(ms + EPS) * g_ref[...]


def _final(x1, yab, route, g_final, prev_out, part, interpret):
    n = x1.shape[0]
    tm = FIN_TM
    steps = n // (tm * FIN_PARTS)
    row = lambda i: (part * steps + i, 0)
    return pl.pallas_call(
        _final_kernel,
        grid=(steps,),
        in_specs=[
            pl.BlockSpec((tm, D_MODEL), row),
            pl.BlockSpec((4, tm, SC_ROW_WORDS), lambda i: (0, i, 0)),
            pl.BlockSpec((tm, LANES), row),
            pl.BlockSpec((1, D_MODEL), lambda i: (0, 0)),
        ] + ([] if prev_out is None else [pl.BlockSpec(memory_space=pl.ANY)]),
        out_specs=pl.BlockSpec((tm, D_MODEL), row),
        out_shape=jax.ShapeDtypeStruct((n, D_MODEL), F32),
        input_output_aliases={} if prev_out is None else {4: 0},
        compiler_params=_cparams(("arbitrary",)),
        interpret=interpret,
        name="combine_final",
    )(x1, yab, route, g_final, *([] if prev_out is None else [prev_out]))


def _permute_w_in(w_in):
    splits = np.cumsum([QKV_W, QKV_W, QKV_W, 512, 512, 1024, 1024, D_MODEL, D_MODEL])[:-1].tolist()
    qa, ka, va, qr, kr, vr, gr, gate_a, gate_r = jnp.split(w_in, splits, axis=-1)
    return jnp.concatenate([gate_a, gate_r, vr, gr, qa, ka, va, qr, kr], axis=-1).astype(BF16)


def _dest_kernel(offs_ref, route_t_ref, idx_ref, *, n_rows):
    route_t = route_t_ref[...]
    experts = route_t[0:2, :]
    dest = route_t[4:6, :].astype(jnp.int32)
    for e in range(N_EXPERTS):
        dest = dest + jnp.where(experts == float(e), offs_ref[e], 0)
    idx_ref[0:2, :] = dest
    idx_ref[2:4, :] = dest + n_rows


def _route_plan(route_t, counts, n, interpret):
    cnt = counts[0, ROUTER_OFF:ROUTER_OFF + N_EXPERTS].astype(jnp.int32)
    padded = ((cnt + EXP_TM - 1) // EXP_TM) * EXP_TM
    ends = jnp.cumsum(padded)
    offs = ends - padded
    n_rows = 2 * n + N_EXPERTS * EXP_TM
    idx4 = pl.pallas_call(
        functools.partial(_dest_kernel, n_rows=n_rows),
        grid_spec=pltpu.PrefetchScalarGridSpec(
            num_scalar_prefetch=1, grid=(1,),
            in_specs=[pl.BlockSpec(route_t.shape, lambda i, offs: (0, 0))],
            out_specs=pl.BlockSpec((4, n), lambda i, offs: (0, 0))),
        out_shape=jax.ShapeDtypeStruct((4, n), jnp.int32),
        interpret=interpret,
        name="route_dest",
    )(offs, route_t)
    tile_row = jnp.minimum(jnp.arange(n_rows // EXP_TM, dtype=jnp.int32), ends[-1] // EXP_TM - 1)
    tile_expert = jnp.sum((ends[None, :] <= (tile_row * EXP_TM)[:, None]).astype(jnp.int32), axis=1)
    return idx4, tile_expert, tile_row, n_rows


def _sc_mesh():
    return plsc.VectorSubcoreMesh(core_axis_name="core", subcore_axis_name="subcore")


def _sc_scatter_rows(rows, idx4, n_out):
    n_in, w = rows.shape
    nb = idx4.shape[1] // SC_WINDOW

    @functools.partial(pl.kernel, out_type=jax.ShapeDtypeStruct((n_out, w), rows.dtype), mesh=_sc_mesh(),
                       scratch_types=[], name="sc_scatter_rows")
    def scatter(x_hbm, ia_hbm, ib_hbm, o_hbm):
        def body(x_vmem, ia_vmem, ib_vmem):
            pltpu.sync_copy(x_vmem, o_hbm.at[ia_vmem.at[0]])
            pltpu.sync_copy(x_vmem, o_hbm.at[ib_vmem.at[0]])

        pltpu.emit_pipeline(
            body,
            grid=(n_in // SC_WINDOW,),
            in_specs=[pl.BlockSpec((SC_WINDOW, w), lambda i: (i, 0)),
                      pl.BlockSpec((1, SC_WINDOW), lambda i: (2 * (i // nb), i % nb)),
                      pl.BlockSpec((1, SC_WINDOW), lambda i: (2 * (i // nb) + 1, i % nb))],
            out_specs=[],
            core_axis_name=("core", "subcore"),
            dimension_semantics=(pltpu.PARALLEL,),
        )(x_hbm, ia_hbm, ib_hbm)

    return scatter(rows, idx4, idx4)


def _sc_gather_rows(table, idx4, part):
    nb = idx4.shape[1] // (SC_WINDOW * FIN_PARTS)
    n_idx = idx4.shape[0] * nb * SC_WINDOW
    w = table.shape[1]

    @functools.partial(pl.kernel, out_type=jax.ShapeDtypeStruct((n_idx, w), table.dtype), mesh=_sc_mesh(),
                       scratch_types=[], name="sc_gather_rows")
    def gather(t_hbm, i_hbm, o_hbm):
        def body(i_vmem, o_vmem):
            pltpu.sync_copy(t_hbm.at[i_vmem.at[0]], o_vmem)

        pltpu.emit_pipeline(
            body,
            grid=(n_idx // SC_WINDOW,),
            in_specs=[pl.BlockSpec((1, SC_WINDOW), lambda i: (i // nb, part * nb + i % nb))],
            out_specs=[pl.BlockSpec((SC_WINDOW, w), lambda i: (i, 0))],
            core_axis_name=("core", "subcore"),
            dimension_semantics=(pltpu.PARALLEL,),
        )(i_hbm, o_hbm)

    return gather(table, idx4)


def _forward(x, g_mix, w_in, w_attn_branch, w_ret_branch, w_out, g_ffn, w_group_router, b_group_router,
             w_expert_router, b_expert_router, w_gate, w_up, w_down, g_final, interpret=False):
    batch, seq, d = x.shape
    n = batch * seq
    x2 = x.reshape(n, d)
    proj, wg_bf, wu_bf, wd_bf = _proj(x2, g_mix[0][None, :], _permute_w_in(w_in[0]), w_gate[0], w_up[0],
                                      w_down[0], interpret)
    o_attn = _attention(proj, batch, seq, interpret)
    o_ret = _retention(proj, batch, seq, interpret)
    pad = LANES - N_EXPERT_GROUPS - N_EXPERTS
    w_router = jnp.concatenate([w_group_router[0], w_expert_router[0], jnp.zeros((d, pad), F32)], axis=-1)
    w_router_hi = w_router.astype(BF16)
    w_router_lo = (w_router - w_router_hi.astype(F32)).astype(BF16)
    w_router2 = jnp.concatenate([w_router_hi, w_router_lo], axis=-1)
    b_router = jnp.concatenate([b_group_router[0], b_expert_router[0], jnp.zeros((pad,), F32)])[None, :]
    x1, h2p, route, route_t, counts = _mix(o_attn, o_ret, proj, x2, w_attn_branch[0].astype(BF16),
                                           w_ret_branch[0].astype(BF16), w_out[0].astype(BF16),
                                           g_ffn[0][None, :], w_router2, b_router, interpret)
    idx4, tile_expert, tile_row, n_rows = _route_plan(route_t, counts, n, interpret)
    xs = _sc_scatter_rows(h2p.reshape(2 * n, SC_ROW_WORDS), idx4, 2 * n_rows)
    ys = _experts(xs.reshape(2, n_rows, SC_ROW_WORDS), tile_expert, tile_row, wg_bf, wu_bf, wd_bf, interpret)
    ys_rows = ys.reshape(2 * n_rows, SC_ROW_WORDS)
    out = None
    for part in range(FIN_PARTS):
        yab = _sc_gather_rows(ys_rows, idx4, part)
        out = _final(x1, yab.reshape(4, n // FIN_PARTS, SC_ROW_WORDS), route, g_final[None, :], out, part, interpret)
    return out.reshape(batch, seq, d)


def kernel(x, g_mix, w_in, w_attn_branch, w_ret_branch, w_out, g_ffn, w_group_router, b_group_router,
           w_expert_router, b_expert_router, w_gate, w_up, w_down, g_final):
    return _forward(x, g_mix, w_in, w_attn_branch, w_ret_branch, w_out, g_ffn, w_group_router,
                    b_group_router, w_expert_router, b_expert_router, w_gate, w_up, w_down, g_final)
```

```python
import functools

import numpy as np
import jax
import jax.numpy as jnp
from jax import lax
from jax.experimental import pallas as pl
from jax.experimental.pallas import tpu as pltpu
from jax.experimental.pallas import tpu_sc as plsc

F32 = jnp.float32
BF16 = jnp.bfloat16

D_MODEL = 1024
ATTN_GROUPS = ((128, 1), (512, 4), (2048, 16))
N_GROUPS = len(ATTN_GROUPS)
ATTN_HEADS = 8
HEAD_DIM = 64
GROUP_W = ATTN_HEADS * HEAD_DIM
QKV_W = N_GROUPS * GROUP_W
RET_HEADS = 4
RET_DK = 128
RET_DV = 256
RET_CHUNK = 128
RET_TS = 1024
N_EXPERT_GROUPS = 4
EXPERTS_PER_GROUP = 8
N_EXPERTS = N_EXPERT_GROUPS * EXPERTS_PER_GROUP
EXPERT_FF = 512
EPS = 1e-6

LANES = 128
BLK = 128
SPAN = 2048
NEG = -1e30
ACC_PARTS = 3
ATTN_STAGES = 9

COL_GATE_A = 0
COL_GATE_R = 1024
COL_VR = 2048
COL_GR = COL_VR + RET_HEADS * RET_DV
COL_QA = COL_GR + RET_HEADS * RET_DV
COL_KA = COL_QA + QKV_W
COL_VA = COL_KA + QKV_W
COL_QR = COL_VA + QKV_W
COL_KR = COL_QR + RET_HEADS * RET_DK
IN_WIDTH = COL_KR + RET_HEADS * RET_DK

PROJ_TM = 512
PROJ_TN = IN_WIDTH // 2
MXU_N = 256
MIX_TM = 1024
MIX_CHUNK = 512
ROUTE_CHUNK = 256
ROUTE_ROWS = 8
EXP_TM = 512
SC_WINDOW = 128
SC_ROW_WORDS = 256
FIN_TM = 512
FIN_PARTS = 1
VMEM_LIMIT = 56 * 1024 * 1024


def _cparams(sem):
    return pltpu.CompilerParams(dimension_semantics=sem, vmem_limit_bytes=VMEM_LIMIT)


def _sigmoid(x):
    return 0.5 * jnp.tanh(0.5 * x) + 0.5


def _proj_kernel(x_ref, g_ref, w_ref, wg_ref, wu_ref, wd_ref, o_ref, wg_o, wu_o, wd_o):
    x = x_ref[...]
    ms = jnp.mean(x * x, axis=-1, keepdims=True)
    h = (x * lax.rsqrt(ms + EPS) * g_ref[...]).astype(BF16)
    for c in range(PROJ_TN // MXU_N):
        sl = slice(c * MXU_N, (c + 1) * MXU_N)
        o_ref[:, sl] = jnp.dot(h, w_ref[:, sl], preferred_element_type=F32).astype(o_ref.dtype)
    wg_o[...] = wg_ref[...].astype(BF16)
    wu_o[...] = wu_ref[...].astype(BF16)
    wd_o[...] = wd_ref[...].astype(BF16)


def _proj(x2, g_mix, w_in_bf16, w_gate, w_up, w_down, interpret):
    n = x2.shape[0]
    n_i = n // PROJ_TM
    steps = (IN_WIDTH // PROJ_TN) * n_i
    flat = [w.reshape(-1, w.shape[-1]) for w in (w_gate, w_up, w_down)]
    w_specs = [pl.BlockSpec((w.shape[0] // steps, w.shape[1]), lambda j, i: (j * n_i + i, 0)) for w in flat]
    outs = pl.pallas_call(
        _proj_kernel,
        grid=(IN_WIDTH // PROJ_TN, n_i),
        in_specs=[
            pl.BlockSpec((PROJ_TM, D_MODEL), lambda j, i: (i, 0)),
            pl.BlockSpec((1, D_MODEL), lambda j, i: (0, 0)),
            pl.BlockSpec((D_MODEL, PROJ_TN), lambda j, i: (0, j)),
        ] + w_specs,
        out_specs=[pl.BlockSpec((PROJ_TM, PROJ_TN), lambda j, i: (i, j))] + w_specs,
        out_shape=[jax.ShapeDtypeStruct((n, IN_WIDTH), BF16)]
        + [jax.ShapeDtypeStruct(w.shape, BF16) for w in flat],
        compiler_params=_cparams(("arbitrary", "arbitrary")),
        interpret=interpret,
        name="proj",
    )(x2, g_mix, w_in_bf16, *flat)
    return outs[0], outs[1].reshape(w_gate.shape), outs[2].reshape(w_up.shape), outs[3].reshape(w_down.shape)


def _attn_unit(q2, kk, vv, bias_a, bias_b):
    lane = lax.broadcasted_iota(jnp.int32, (BLK, LANES), 1)
    left = lane < HEAD_DIM
    zero = jnp.zeros_like(q2)
    nt = (((1,), (1,)), ((), ()))
    q_st = jnp.concatenate([jnp.where(left, q2, zero), jnp.where(left, zero, q2)], axis=0)
    s = lax.dot_general(q_st, kk, nt, preferred_element_type=F32) + jnp.concatenate([bias_a, bias_b], axis=0)
    m = jnp.max(s, axis=-1, keepdims=True)
    p = jnp.exp(s - m)
    den = jnp.sum(p, axis=-1, keepdims=True)
    o = jnp.dot(p.astype(BF16), vv, preferred_element_type=F32)
    return (jnp.where(left, o[:BLK], o[BLK:]), jnp.where(left, m[:BLK], m[BLK:]),
            jnp.where(left, den[:BLK], den[BLK:]))


def _attn_kernel(q1_ref, q2_ref, q3_ref, k1_ref, k2_ref, k3_ref, v1_ref, v2_ref, v3_ref,
                 bias_ref, o_ref, tmp_ref, qd_ref, kvd_ref, acc_ref, accw_ref, *, seq):
    s_id = pl.program_id(2)
    step = ATTN_GROUPS[1][1]

    quarter = SPAN // step
    span_row0 = pl.multiple_of(s_id * SPAN, SPAN)
    first_stage = {1: 0, 2: 3}

    def operands(gi):
        d = ATTN_GROUPS[gi][1]
        q_ref, k_ref, v_ref = ((q2_ref, k2_ref, v2_ref), (q3_ref, k3_ref, v3_ref))[gi - 1]
        n_stage = 1 if d == step else 2
        kv_off = s_id * (SPAN // d)
        return [(q_ref, 0, first_stage[gi], qd_ref, gi - 1, SPAN // d, 0, 0.125),
                (k_ref, span_row0, first_stage[gi] + n_stage, kvd_ref, 2 * (gi - 1), seq // d, kv_off, 1.0),
                (v_ref, span_row0, first_stage[gi] + 2 * n_stage, kvd_ref, 2 * (gi - 1) + 1, seq // d, kv_off, 1.0)]

    def staging_pieces(gi):
        pieces = []
        for src_ref, row0, stage, _, _, _, _, scale in operands(gi):
            for c0 in range(0, SPAN, 256):
                def to_f32(src_ref=src_ref, row0=row0, stage=stage, scale=scale, c0=c0):
                    x = src_ref[pl.ds(pl.multiple_of(row0 + c0, 256), 256), :].astype(F32)
                    tmp_ref[stage, c0:c0 + 256, :] = x if scale == 1.0 else x * scale
                pieces.append(to_f32)
        if ATTN_GROUPS[gi][1] != step:
            for _, _, stage, _, _, _, _, _ in operands(gi):
                for b in range(step):
                    for j0 in range(0, quarter, 256):
                        def first_pass(stage=stage, b=b, j0=j0):
                            tmp_ref[stage + 1, b * quarter + j0:b * quarter + j0 + 256, :] = (
                                tmp_ref[stage, pl.ds(b + step * j0, 256, stride=step), :])
                        pieces.append(first_pass)
        return pieces

    def residue_pieces(gi, r):
        d = ATTN_GROUPS[gi][1]
        chunk = min(256, SPAN // d)
        pieces = []
        for _, _, stage, dst_ref, slot, pitch, off, _ in operands(gi):
            for j0 in range(0, SPAN // d, chunk):
                def last_pass(stage=stage, dst_ref=dst_ref, slot=slot, pitch=pitch, off=off, j0=j0):
                    if d == step:
                        rows = tmp_ref[stage, pl.ds(r + step * j0, chunk, stride=step), :]
                    else:
                        a, b = divmod(r, step)
                        rows = tmp_ref[stage + 1, pl.ds(b * quarter + a + step * j0, chunk, stride=step), :]
                    row0 = r * pitch + j0 + off
                    if not isinstance(row0, int):
                        row0 = pl.multiple_of(row0, chunk)
                    dst_ref[slot, pl.ds(row0, chunk), :] = rows.astype(BF16)
                pieces.append(last_pass)
        return pieces

    def dilated_unit(gi, d, m, r, first):
        slot = gi - 1
        loc = BLK * m * d + r
        q_row = r * (SPAN // d) + BLK * m
        cur = pl.multiple_of(r * (seq // d) + s_id * (SPAN // d) + BLK * m, BLK)
        prev = pl.multiple_of(jnp.where(first == 1, cur, cur - BLK), BLK)
        q2 = qd_ref[slot, pl.ds(q_row, BLK), :]
        kk = jnp.concatenate([kvd_ref[2 * slot, pl.ds(prev, BLK), :], kvd_ref[2 * slot, pl.ds(cur, BLK), :]], axis=0)
        vv = jnp.concatenate([kvd_ref[2 * slot + 1, pl.ds(prev, BLK), :],
                              kvd_ref[2 * slot + 1, pl.ds(cur, BLK), :]], axis=0)
        parts = _attn_unit(q2, kk, vv, bias_ref[gi, first, 0], bias_ref[gi, first, 1])
        for j, part in enumerate(parts):
            if d == step:
                acc_ref[ACC_PARTS * slot + j, pl.ds(loc, BLK, stride=d), :] = part
            else:
                a, b = divmod(r, step)
                accw_ref[j, pl.ds(b * quarter + a, BLK, stride=step), :] = part

    def reinterleave_pieces():
        pieces = []
        for j in range(ACC_PARTS):
            for b in range(step):
                for j0 in range(0, quarter, 256):
                    def piece(j=j, b=b, j0=j0):
                        acc_ref[ACC_PARTS + j, pl.ds(b + step * j0, 256, stride=step), :] = (
                            accw_ref[j, b * quarter + j0:b * quarter + j0 + 256, :])
                    pieces.append(piece)
        return pieces

    def dense_unit(m):
        loc = m * BLK
        cur = pl.multiple_of(s_id * SPAN + loc, BLK)
        prev = pl.multiple_of(jnp.maximum(cur - BLK, 0), BLK)
        first = jnp.where(cur == 0, 1, 0)
        q2 = q1_ref[loc:loc + BLK, :] * 0.125
        kk = jnp.concatenate([k1_ref[pl.ds(prev, BLK), :], k1_ref[pl.ds(cur, BLK), :]], axis=0)
        vv = jnp.concatenate([v1_ref[pl.ds(prev, BLK), :], v1_ref[pl.ds(cur, BLK), :]], axis=0)
        n1, m1, d1 = _attn_unit(q2, kk, vv, bias_ref[0, first, 0], bias_ref[0, first, 1])
        (n2, m2, d2), (n3, m3, d3) = (
            tuple(acc_ref[g * ACC_PARTS + j, loc:loc + BLK, :] for j in range(ACC_PARTS)) for g in range(2))
        mx = jnp.maximum(jnp.maximum(m1, m2), m3)
        w1, w2, w3 = jnp.exp(m1 - mx), jnp.exp(m2 - mx), jnp.exp(m3 - mx)
        num = w1 * n1 + w2 * n2 + w3 * n3
        den = w1 * d1 + w2 * d2 + w3 * d3
        o_ref[loc:loc + BLK, :] = (num / den).astype(o_ref.dtype)

    first_span = jnp.where(s_id == 0, 1, 0)
    for gi in (1, 2):
        for piece in staging_pieces(gi) + [p for r in range(ATTN_GROUPS[gi][1]) for p in residue_pieces(gi, r)]:
            piece()
    for gi in (2, 1):
        d = ATTN_GROUPS[gi][1]
        for m in range(SPAN // (BLK * d)):
            for r in range(d):
                dilated_unit(gi, d, m, r, first_span if m == 0 else 0)
        if d != step:
            for piece in reinterleave_pieces():
                piece()
    for m in range(SPAN // BLK):
        dense_unit(m)


def _attn_bias():
    slopes = np.exp2(-8.0 * np.arange(1, ATTN_HEADS + 1, dtype=np.float64) / ATTN_HEADS)
    qi = np.arange(BLK)[:, None]
    kj = np.arange(2 * BLK)[None, :]
    rel = qi + BLK - kj
    out = np.zeros((N_GROUPS, 2, ATTN_HEADS, BLK, 2 * BLK), np.float32)
    for gi, (window, d) in enumerate(ATTN_GROUPS):
        n_back = window // d
        assert n_back == BLK
        valid = (rel >= 0) & (rel <= n_back)
        bias = -slopes[:, None, None] * (rel * d)[None].astype(np.float64)
        out[gi, 0] = np.where(valid[None], bias, NEG)
        out[gi, 1] = np.where((valid & (kj >= BLK))[None], bias, NEG)
    return jnp.asarray(out)


def _attention(proj, batch, seq, interpret):
    n = batch * seq
    spans = seq // SPAN
    n_hp = GROUP_W // LANES
    qcol = lambda g: (COL_QA + g * GROUP_W) // LANES
    kcol = lambda g: (COL_KA + g * GROUP_W) // LANES
    vcol = lambda g: (COL_VA + g * GROUP_W) // LANES
    q_specs = [pl.BlockSpec((SPAN, LANES), functools.partial(lambda b, hp, s, c: (b * spans + s, c + hp), c=qcol(g)))
               for g in range(N_GROUPS)]
    k_specs = [pl.BlockSpec((seq, LANES), functools.partial(lambda b, hp, s, c: (b, c + hp), c=kcol(g)))
               for g in range(N_GROUPS)]
    v_specs = [pl.BlockSpec((seq, LANES), functools.partial(lambda b, hp, s, c: (b, c + hp), c=vcol(g)))
               for g in range(N_GROUPS)]
    bias_spec = pl.BlockSpec((N_GROUPS, 2, 2, BLK, 2 * BLK), lambda b, hp, s: (0, 0, hp, 0, 0))
    return pl.pallas_call(
        functools.partial(_attn_kernel, seq=seq),
        grid=(batch, n_hp, spans),
        in_specs=q_specs + k_specs + v_specs + [bias_spec],
        out_specs=pl.BlockSpec((SPAN, LANES), lambda b, hp, s: (b * spans + s, hp)),
        out_shape=jax.ShapeDtypeStruct((n, GROUP_W), BF16),
        scratch_shapes=[
            pltpu.VMEM((ATTN_STAGES, SPAN, LANES), F32),
            pltpu.VMEM((2, SPAN, LANES), BF16),
            pltpu.VMEM((4, seq, LANES), BF16),
            pltpu.VMEM((2 * ACC_PARTS, SPAN, LANES), F32),
            pltpu.VMEM((ACC_PARTS, SPAN, LANES), F32),
        ],
        compiler_params=_cparams(("arbitrary", "arbitrary", "arbitrary")),
        interpret=interpret,
        name="attn",
    )(*([proj] * 9), _attn_bias())


def _ret_kernel(q_ref, k_ref, v_ref, gr_ref, dec_ref, xi_ref, zeta_ref, gch_ref, o_ref, st_ref):
    @pl.when(pl.program_id(1) == 0)
    def _():
        st_ref[...] = jnp.zeros_like(st_ref)

    nt = (((1,), (1,)), ((), ()))
    scale = RET_DK ** -0.5

    for c in range(RET_TS // RET_CHUNK):
        rows = pl.ds(c * RET_CHUNK, RET_CHUNK)
        for h in range(RET_HEADS):
            kcols = slice(h * RET_DK, (h + 1) * RET_DK)
            vcols = slice(h * RET_DV, (h + 1) * RET_DV)
            qi = q_ref[rows, kcols]
            kf = k_ref[rows, kcols].astype(F32) * scale
            ki = kf.astype(BF16)
            kz_t = jnp.transpose(kf * zeta_ref[h]).astype(BF16)
            vi = v_ref[rows, vcols]
            att = lax.dot_general(qi, ki, nt, preferred_element_type=F32) * dec_ref[h]
            inner = jnp.dot(att.astype(BF16), vi, preferred_element_type=F32)
            st = st_ref[h]
            cross = jnp.dot(qi, st.astype(BF16), preferred_element_type=F32) * xi_ref[h]
            st_ref[h] = gch_ref[h] * st + jnp.dot(kz_t, vi, preferred_element_type=F32)
            y = inner + cross
            mu = jnp.mean(y, axis=-1, keepdims=True)
            yc = y - mu
            var = jnp.mean(yc * yc, axis=-1, keepdims=True)
            yn = yc * lax.rsqrt(var + EPS)
            g = gr_ref[rows, vcols].astype(F32)
            o_ref[rows, vcols] = (g * _sigmoid(g) * yn).astype(o_ref.dtype)


def _ret_tables():
    c = RET_CHUNK
    log_g = np.log1p(-np.exp2(-5.0 - np.arange(RET_HEADS, dtype=np.float64)))
    pos = np.arange(c, dtype=np.float64)
    diff = pos[:, None] - pos[None, :]
    dec = np.where(diff >= 0, np.exp(log_g[:, None, None] * np.maximum(diff, 0.0)), 0.0)
    xi = np.exp(log_g[:, None] * (pos + 1.0))[..., None] * np.ones((1, 1, RET_DV))
    zeta = np.exp(log_g[:, None] * (c - 1.0 - pos))[..., None] * np.ones((1, 1, RET_DK))
    gch = np.exp(log_g * c)[:, None, None] * np.ones((1, 1, RET_DV))
    return tuple(jnp.asarray(t, F32) for t in (dec, xi, zeta, gch))


def _retention(proj, batch, seq, interpret):
    n = batch * seq
    dec, xi, zeta, gch = _ret_tables()
    qk_w = RET_HEADS * RET_DK
    v_w = RET_HEADS * RET_DV
    nts = seq // RET_TS
    const3 = lambda b, t: (0, 0, 0)
    return pl.pallas_call(
        _ret_kernel,
        grid=(batch, nts),
        in_specs=[
            pl.BlockSpec((RET_TS, qk_w), lambda b, t: (b * nts + t, COL_QR // qk_w)),
            pl.BlockSpec((RET_TS, qk_w), lambda b, t: (b * nts + t, COL_KR // qk_w)),
            pl.BlockSpec((RET_TS, v_w), lambda b, t: (b * nts + t, COL_VR // v_w)),
            pl.BlockSpec((RET_TS, v_w), lambda b, t: (b * nts + t, COL_GR // v_w)),
            pl.BlockSpec((RET_HEADS, RET_CHUNK, RET_CHUNK), const3),
            pl.BlockSpec((RET_HEADS, RET_CHUNK, RET_DV), const3),
            pl.BlockSpec((RET_HEADS, RET_CHUNK, RET_DK), const3),
            pl.BlockSpec((RET_HEADS, 1, RET_DV), const3),
        ],
        out_specs=pl.BlockSpec((RET_TS, v_w), lambda b, t: (b * nts + t, 0)),
        out_shape=jax.ShapeDtypeStruct((n, v_w), BF16),
        scratch_shapes=[pltpu.VMEM((RET_HEADS, RET_DK, RET_DV), F32)],
        compiler_params=_cparams(("arbitrary", "arbitrary")),
        interpret=interpret,
        name="retention",
    )(proj, proj, proj, proj, dec, xi, zeta, gch)


ROUTER_OFF = N_EXPERT_GROUPS


def _pack_bf16_pair(a, b):
    hi = lax.bitcast_convert_type(a.astype(BF16).astype(F32), jnp.uint32)
    lo = lax.bitcast_convert_type(b.astype(BF16).astype(F32), jnp.uint32)
    return lax.bitcast_convert_type(hi | (lo >> 16), jnp.int32)


def _unpack_bf16_pair(w):
    u = lax.bitcast_convert_type(w, jnp.uint32)
    a = lax.bitcast_convert_type(u & jnp.uint32(0xFFFF0000), F32).astype(BF16)
    b = lax.bitcast_convert_type(u << 16, F32).astype(BF16)
    return a, b


def _pack_rows(y):
    q = D_MODEL // 4
    return (_pack_bf16_pair(y[:, 0:q], y[:, 2 * q:3 * q]), _pack_bf16_pair(y[:, q:2 * q], y[:, 3 * q:4 * q]))


def _unpack_rows(slab0, slab1):
    q0, q2 = _unpack_bf16_pair(slab0)
    q1, q3 = _unpack_bf16_pair(slab1)
    return jnp.concatenate([q0, q1, q2, q3], axis=1)


def _mix_kernel(oa_ref, or_ref, ga_ref, gr_ref, x_ref, pa_ref, pr_ref, wo_ref, gf_ref, wr_ref, br_ref,
                x1_ref, h2_ref, route_ref, route_t_ref, cnt_ref, carry_ref, logit_ref):
    step = pl.program_id(0)

    @pl.when(step == 0)
    def _():
        carry_ref[...] = jnp.zeros_like(carry_ref)
        logit_ref[...] = jnp.zeros_like(logit_ref)

    routing = iter([functools.partial(_route_rows, pl.ds(c * ROUTE_CHUNK, ROUTE_CHUNK), step > 0, logit_ref,
                                      route_ref, route_t_ref, cnt_ref, carry_ref)
                    for c in range(MIX_TM // ROUTE_CHUNK)])
    for c in range(MIX_TM // MIX_CHUNK):
        for _ in _mix_rows(pl.ds(c * MIX_CHUNK, MIX_CHUNK), oa_ref, or_ref, ga_ref, gr_ref, x_ref, pa_ref, pr_ref,
                           wo_ref, gf_ref, wr_ref, br_ref, x1_ref, h2_ref, logit_ref):
            next(routing, lambda: None)()
    for piece in routing:
        piece()


def _mix_rows(rows, oa_ref, or_ref, ga_ref, gr_ref, x_ref, pa_ref, pr_ref, wo_ref, gf_ref, wr_ref, br_ref,
              x1_ref, h2_ref, logit_ref):
    a = jnp.dot(oa_ref[rows, :], pa_ref[...], preferred_element_type=F32)
    yield
    r = jnp.dot(or_ref[rows, :], pr_ref[...], preferred_element_type=F32)
    yield
    merged = (_sigmoid(ga_ref[rows, :].astype(F32)) * a + _sigmoid(gr_ref[rows, :].astype(F32)) * r)
    x1 = x_ref[rows, :] + jnp.dot(merged.astype(BF16), wo_ref[...], preferred_element_type=F32)
    x1_ref[rows, :] = x1
    ms = jnp.mean(x1 * x1, axis=-1, keepdims=True)
    h2 = x1 * lax.rsqrt(ms + EPS) * gf_ref[...]
    h2_ref[0, rows, :], h2_ref[1, rows, :] = _pack_rows(h2)

    h_hi = h2.astype(BF16)
    h_lo = (h2 - h_hi.astype(F32)).astype(BF16)
    both = jnp.dot(h_hi, wr_ref[...], preferred_element_type=F32)
    logit_ref[rows, :] = (both[:, :LANES] + both[:, LANES:]
                          + jnp.dot(h_lo, wr_ref[:, :LANES], preferred_element_type=F32) + br_ref[...])


def _route_rows(rows, live, logit_ref, route_ref, route_t_ref, cnt_ref, carry_ref):
    logits = logit_ref[rows, :]
    tm = logits.shape[0]
    lane = lax.broadcasted_iota(jnp.int32, (tm, LANES), 1).astype(F32)
    big = jnp.float32(4 * LANES)
    ninf = -jnp.inf
    is_g = lane < N_EXPERT_GROUPS
    gl = jnp.where(is_g, logits, ninf)
    gmax = jnp.max(gl, axis=-1, keepdims=True)
    gsum = jnp.sum(jnp.where(is_g, jnp.exp(gl - gmax), 0.0), axis=-1, keepdims=True)
    g_val = 1.0 / gsum
    g_idx = jnp.min(jnp.where(jnp.logical_and(is_g, gl == gmax), lane, big), axis=-1, keepdims=True)
    lo = ROUTER_OFF + EXPERTS_PER_GROUP * g_idx
    in_grp = jnp.logical_and(lane >= lo, lane < lo + EXPERTS_PER_GROUP)
    el = jnp.where(in_grp, logits, ninf)
    v1 = jnp.max(el, axis=-1, keepdims=True)
    i1 = jnp.min(jnp.where(jnp.logical_and(in_grp, el == v1), lane, big), axis=-1, keepdims=True)
    rest = jnp.logical_and(in_grp, lane != i1)
    el2 = jnp.where(rest, logits, ninf)
    v2 = jnp.max(el2, axis=-1, keepdims=True)
    i2 = jnp.min(jnp.where(jnp.logical_and(rest, el2 == v2), lane, big), axis=-1, keepdims=True)
    t = jnp.exp(v2 - v1)
    w1 = g_val / (1.0 + t)
    w2 = g_val * t / (1.0 + t)

    sel = jnp.logical_or(lane == i1, lane == i2)
    sel_bf = jnp.where(sel, 1.0, 0.0).astype(BF16)
    row = lax.broadcasted_iota(jnp.int32, (tm, tm), 0)
    col = lax.broadcasted_iota(jnp.int32, (tm, tm), 1)
    tri = jnp.where(col < row, 1.0, 0.0).astype(BF16)
    before = jnp.dot(tri, sel_bf, preferred_element_type=F32) + carry_ref[...]
    r1 = jnp.sum(jnp.where(lane == i1, before, 0.0), axis=-1, keepdims=True)
    r2 = jnp.sum(jnp.where(lane == i2, before, 0.0), axis=-1, keepdims=True)
    carry = carry_ref[...] + jnp.where(live, jnp.sum(jnp.where(sel, 1.0, 0.0), axis=0, keepdims=True), 0.0)
    carry_ref[...] = carry
    cnt_ref[...] = carry

    vals = (i1 - ROUTER_OFF, i2 - ROUTER_OFF, w1, w2, r1, r2)
    route = jnp.zeros((tm, LANES), F32)
    for j, v in enumerate(vals):
        route = jnp.where(lane == j, v, route)
    route_ref[rows, :] = route
    route_t_ref[:, rows] = jnp.transpose(route)[:ROUTE_ROWS, :]


def _mix(o_attn, o_ret, proj, x2, pa, pr, wo, g_ffn, w_router, b_router, interpret):
    n = x2.shape[0]
    tm = MIX_TM
    last = n // tm - 1
    const = lambda i: (0, 0)
    cur = lambda i: jnp.minimum(i, last)
    prev = lambda i: jnp.maximum(i - 1, 0)
    return pl.pallas_call(
        _mix_kernel,
        grid=(n // tm + 1,),
        in_specs=[
            pl.BlockSpec((tm, GROUP_W), lambda i: (cur(i), 0)),
            pl.BlockSpec((tm, D_MODEL), lambda i: (cur(i), 0)),
            pl.BlockSpec((tm, D_MODEL), lambda i: (cur(i), COL_GATE_A // D_MODEL)),
            pl.BlockSpec((tm, D_MODEL), lambda i: (cur(i), COL_GATE_R // D_MODEL)),
            pl.BlockSpec((tm, D_MODEL), lambda i: (cur(i), 0)),
            pl.BlockSpec((GROUP_W, D_MODEL), const),
            pl.BlockSpec((D_MODEL, D_MODEL), const),
            pl.BlockSpec((D_MODEL, D_MODEL), const),
            pl.BlockSpec((1, D_MODEL), const),
            pl.BlockSpec((D_MODEL, 2 * LANES), const),
            pl.BlockSpec((1, LANES), const),
        ],
        out_specs=[
            pl.BlockSpec((tm, D_MODEL), lambda i: (cur(i), 0)),
            pl.BlockSpec((2, tm, SC_ROW_WORDS), lambda i: (0, cur(i), 0)),
            pl.BlockSpec((tm, LANES), lambda i: (prev(i), 0)),
            pl.BlockSpec((ROUTE_ROWS, tm), lambda i: (0, prev(i))),
            pl.BlockSpec((1, LANES), const),
        ],
        out_shape=[
            jax.ShapeDtypeStruct((n, D_MODEL), F32),
            jax.ShapeDtypeStruct((2, n, SC_ROW_WORDS), jnp.int32),
            jax.ShapeDtypeStruct((n, LANES), F32),
            jax.ShapeDtypeStruct((ROUTE_ROWS, n), F32),
            jax.ShapeDtypeStruct((1, LANES), F32),
        ],
        scratch_shapes=[pltpu.VMEM((1, LANES), F32), pltpu.VMEM((tm, LANES), F32)],
        compiler_params=_cparams(("arbitrary",)),
        interpret=interpret,
        name="mix_router",
    )(o_attn, o_ret, proj, proj, x2, pa, pr, wo, g_ffn, w_router, b_router)


def _expert_kernel(te_ref, tr_ref, xs_ref, wg_ref, wu_ref, wd_ref, o_ref):
    i = pl.program_id(0)

    @pl.when(tr_ref[i] == i)
    def _():
        xs = _unpack_rows(xs_ref[0], xs_ref[1])
        a = jnp.dot(xs, wg_ref[0], preferred_element_type=F32)
        u = jnp.dot(xs, wu_ref[0], preferred_element_type=F32)
        hid = (a * _sigmoid(a) * u).astype(BF16)
        y = jnp.dot(hid, wd_ref[0], preferred_element_type=F32)
        o_ref[0], o_ref[1] = _pack_rows(y)


def _experts(xs, tile_expert, tile_row, w_gate, w_up, w_down, interpret):
    p = xs.shape[1]
    n_tiles = p // EXP_TM
    grid_spec = pltpu.PrefetchScalarGridSpec(
        num_scalar_prefetch=2,
        grid=(n_tiles,),
        in_specs=[
            pl.BlockSpec((2, EXP_TM, SC_ROW_WORDS), lambda i, te, tr: (0, tr[i], 0)),
            pl.BlockSpec((1, D_MODEL, EXPERT_FF), lambda i, te, tr: (te[i], 0, 0)),
            pl.BlockSpec((1, D_MODEL, EXPERT_FF), lambda i, te, tr: (te[i], 0, 0)),
            pl.BlockSpec((1, EXPERT_FF, D_MODEL), lambda i, te, tr: (te[i], 0, 0)),
        ],
        out_specs=pl.BlockSpec((2, EXP_TM, SC_ROW_WORDS), lambda i, te, tr: (0, tr[i], 0)),
    )
    return pl.pallas_call(
        _expert_kernel,
        grid_spec=grid_spec,
        out_shape=jax.ShapeDtypeStruct((2, p, SC_ROW_WORDS), jnp.int32),
        compiler_params=_cparams(("arbitrary",)),
        interpret=interpret,
        name="experts",
    )(tile_expert, tile_row, xs, w_gate, w_up, w_down)


def _final_kernel(x1_ref, yab_ref, route_ref, g_ref, *rest):
    o_ref = rest[-1]
    route = route_ref[...]
    w1 = route[:, 2:3]
    w2 = route[:, 3:4]
    ya = _unpack_rows(yab_ref[0], yab_ref[2]).astype(F32)
    yb = _unpack_rows(yab_ref[1], yab_ref[3]).astype(F32)
    x2 = x1_ref[...] + w1 * ya + w2 * yb
    ms = jnp.mean(x2 * x2, axis=-1, keepdims=True)
    o_ref[...] = x2 * lax.rsqrt(ms + EPS) * g_ref[...]


def _final(x1, yab, route, g_final, prev_out, part, interpret):
    n = x1.shape[0]
    tm = FIN_TM
    steps = n // (tm * FIN_PARTS)
    row = lambda i: (part * steps + i, 0)
    return pl.pallas_call(
        _final_kernel,
        grid=(steps,),
        in_specs=[
            pl.BlockSpec((tm, D_MODEL), row),
            pl.BlockSpec((4, tm, SC_ROW_WORDS), lambda i: (0, i, 0)),
            pl.BlockSpec((tm, LANES), row),
            pl.BlockSpec((1, D_MODEL), lambda i: (0, 0)),
        ] + ([] if prev_out is None else [pl.BlockSpec(memory_space=pl.ANY)]),
        out_specs=pl.BlockSpec((tm, D_MODEL), row),
        out_shape=jax.ShapeDtypeStruct((n, D_MODEL), F32),
        input_output_aliases={} if prev_out is None else {4: 0},
        compiler_params=_cparams(("arbitrary",)),
        interpret=interpret,
        name="combine_final",
    )(x1, yab, route, g_final, *([] if prev_out is None else [prev_out]))


def _permute_w_in(w_in):
    splits = np.cumsum([QKV_W, QKV_W, QKV_W, 512, 512, 1024, 1024, D_MODEL, D_MODEL])[:-1].tolist()
    qa, ka, va, qr, kr, vr, gr, gate_a, gate_r = jnp.split(w_in, splits, axis=-1)
    return jnp.concatenate([gate_a, gate_r, vr, gr, qa, ka, va, qr, kr], axis=-1).astype(BF16)


def _dest_kernel(offs_ref, route_t_ref, idx_ref, *, n_rows):
    route_t = route_t_ref[...]
    experts = route_t[0:2, :]
    dest = route_t[4:6, :].astype(jnp.int32)
    for e in range(N_EXPERTS):
        dest = dest + jnp.where(experts == float(e), offs_ref[e], 0)
    idx_ref[0:2, :] = dest
    idx_ref[2:4, :] = dest + n_rows


def _route_plan(route_t, counts, n, interpret):
    cnt = counts[0, ROUTER_OFF:ROUTER_OFF + N_EXPERTS].astype(jnp.int32)
    padded = ((cnt + EXP_TM - 1) // EXP_TM) * EXP_TM
    ends = jnp.cumsum(padded)
    offs = ends - padded
    n_rows = 2 * n + N_EXPERTS * EXP_TM
    idx4 = pl.pallas_call(
        functools.partial(_dest_kernel, n_rows=n_rows),
        grid_spec=pltpu.PrefetchScalarGridSpec(
            num_scalar_prefetch=1, grid=(1,),
            in_specs=[pl.BlockSpec(route_t.shape, lambda i, offs: (0, 0))],
            out_specs=pl.BlockSpec((4, n), lambda i, offs: (0, 0))),
        out_shape=jax.ShapeDtypeStruct((4, n), jnp.int32),
        interpret=interpret,
        name="route_dest",
    )(offs, route_t)
    tile_row = jnp.minimum(jnp.arange(n_rows // EXP_TM, dtype=jnp.int32), ends[-1] // EXP_TM - 1)
    tile_expert = jnp.sum((ends[None, :] <= (tile_row * EXP_TM)[:, None]).astype(jnp.int32), axis=1)
    return idx4, tile_expert, tile_row, n_rows


def _sc_mesh():
    return plsc.VectorSubcoreMesh(core_axis_name="core", subcore_axis_name="subcore")


def _sc_scatter_rows(rows, idx4, n_out):
    n_in, w = rows.shape
    nb = idx4.shape[1] // SC_WINDOW

    @functools.partial(pl.kernel, out_type=jax.ShapeDtypeStruct((n_out, w), rows.dtype), mesh=_sc_mesh(),
                       scratch_types=[], name="sc_scatter_rows")
    def scatter(x_hbm, ia_hbm, ib_hbm, o_hbm):
        def body(x_vmem, ia_vmem, ib_vmem):
            pltpu.sync_copy(x_vmem, o_hbm.at[ia_vmem.at[0]])
            pltpu.sync_copy(x_vmem, o_hbm.at[ib_vmem.at[0]])

        pltpu.emit_pipeline(
            body,
            grid=(n_in // SC_WINDOW,),
            in_specs=[pl.BlockSpec((SC_WINDOW, w), lambda i: (i, 0)),
                      pl.BlockSpec((1, SC_WINDOW), lambda i: (2 * (i // nb), i % nb)),
                      pl.BlockSpec((1, SC_WINDOW), lambda i: (2 * (i // nb) + 1, i % nb))],
            out_specs=[],
            core_axis_name=("core", "subcore"),
            dimension_semantics=(pltpu.PARALLEL,),
        )(x_hbm, ia_hbm, ib_hbm)

    return scatter(rows, idx4, idx4)


def _sc_gather_rows(table, idx4, part):
    nb = idx4.shape[1] // (SC_WINDOW * FIN_PARTS)
    n_idx = idx4.shape[0] * nb * SC_WINDOW
    w = table.shape[1]

    @functools.partial(pl.kernel, out_type=jax.ShapeDtypeStruct((n_idx, w), table.dtype), mesh=_sc_mesh(),
                       scratch_types=[], name="sc_gather_rows")
    def gather(t_hbm, i_hbm, o_hbm):
        def body(i_vmem, o_vmem):
            pltpu.sync_copy(t_hbm.at[i_vmem.at[0]], o_vmem)

        pltpu.emit_pipeline(
            body,
            grid=(n_idx // SC_WINDOW,),
            in_specs=[pl.BlockSpec((1, SC_WINDOW), lambda i: (i // nb, part * nb + i % nb))],
            out_specs=[pl.BlockSpec((SC_WINDOW, w), lambda i: (i, 0))],
            core_axis_name=("core", "subcore"),
            dimension_semantics=(pltpu.PARALLEL,),
        )(i_hbm, o_hbm)

    return gather(table, idx4)


def _forward(x, g_mix, w_in, w_attn_branch, w_ret_branch, w_out, g_ffn, w_group_router, b_group_router,
             w_expert_router, b_expert_router, w_gate, w_up, w_down, g_final, interpret=False):
    batch, seq, d = x.shape
    n = batch * seq
    x2 = x.reshape(n, d)
    proj, wg_bf, wu_bf, wd_bf = _proj(x2, g_mix[0][None, :], _permute_w_in(w_in[0]), w_gate[0], w_up[0],
                                      w_down[0], interpret)
    o_attn = _attention(proj, batch, seq, interpret)
    o_ret = _retention(proj, batch, seq, interpret)
    pad = LANES - N_EXPERT_GROUPS - N_EXPERTS
    w_router = jnp.concatenate([w_group_router[0], w_expert_router[0], jnp.zeros((d, pad), F32)], axis=-1)
    w_router_hi = w_router.astype(BF16)
    w_router_lo = (w_router - w_router_hi.astype(F32)).astype(BF16)
    w_router2 = jnp.concatenate([w_router_hi, w_router_lo], axis=-1)
    b_router = jnp.concatenate([b_group_router[0], b_expert_router[0], jnp.zeros((pad,), F32)])[None, :]
    x1, h2p, route, route_t, counts = _mix(o_attn, o_ret, proj, x2, w_attn_branch[0].astype(BF16),
                                           w_ret_branch[0].astype(BF16), w_out[0].astype(BF16),
                                           g_ffn[0][None, :], w_router2, b_router, interpret)
    idx4, tile_expert, tile_row, n_rows = _route_plan(route_t, counts, n, interpret)
    xs = _sc_scatter_rows(h2p.reshape(2 * n, SC_ROW_WORDS), idx4, 2 * n_rows)
    ys = _experts(xs.reshape(2, n_rows, SC_ROW_WORDS), tile_expert, tile_row, wg_bf, wu_bf, wd_bf, interpret)
    ys_rows = ys.reshape(2 * n_rows, SC_ROW_WORDS)
    out = None
    for part in range(FIN_PARTS):
        yab = _sc_gather_rows(ys_rows, idx4, part)
        out = _final(x1, yab.reshape(4, n // FIN_PARTS, SC_ROW_WORDS), route, g_final[None, :], out, part, interpret)
    return out.reshape(batch, seq, d)


def kernel(x, g_mix, w_in, w_attn_branch, w_ret_branch, w_out, g_ffn, w_group_router, b_group_router,
           w_expert_router, b_expert_router, w_gate, w_up, w_down, g_final):
    return _forward(x, g_mix, w_in, w_attn_branch, w_ret_branch, w_out, g_ffn, w_group_router,
                    b_group_router, w_expert_router, b_expert_router, w_gate, w_up, w_down, g_final)
```

```python
import functools

import numpy as np
import jax
import jax.numpy as jnp
from jax import lax
from jax.experimental import pallas as pl
from jax.experimental.pallas import tpu as pltpu
from jax.experimental.pallas import tpu_sc as plsc

F32 = jnp.float32
BF16 = jnp.bfloat16

D_MODEL = 1024
ATTN_GROUPS = ((128, 1), (512, 4), (2048, 16))
N_GROUPS = len(ATTN_GROUPS)
ATTN_HEADS = 8
HEAD_DIM = 64
GROUP_W = ATTN_HEADS * HEAD_DIM
QKV_W = N_GROUPS * GROUP_W
RET_HEADS = 4
RET_DK = 128
RET_DV = 256
RET_CHUNK = 128
RET_TS = 1024
N_EXPERT_GROUPS = 4
EXPERTS_PER_GROUP = 8
N_EXPERTS = N_EXPERT_GROUPS * EXPERTS_PER_GROUP
EXPERT_FF = 512
EPS = 1e-6

LANES = 128
BLK = 128
SPAN = 2048
NEG = -1e30
ACC_PARTS = 3
ATTN_STAGES = 9

COL_GATE_A = 0
COL_GATE_R = 1024
COL_VR = 2048
COL_GR = COL_VR + RET_HEADS * RET_DV
COL_QA = COL_GR + RET_HEADS * RET_DV
COL_KA = COL_QA + QKV_W
COL_VA = COL_KA + QKV_W
COL_QR = COL_VA + QKV_W
COL_KR = COL_QR + RET_HEADS * RET_DK
IN_WIDTH = COL_KR + RET_HEADS * RET_DK

PROJ_TM = 512
PROJ_NORM_ROWS = 128
PROJ_TN = IN_WIDTH // 2
MXU_N = 256
MIX_TM = 1024
MIX_CHUNK = 512
ROUTE_CHUNK = 256
ROUTE_ROWS = 8
EXP_TM = 512
EXP_CHUNK = 256
SC_WINDOW = 128
SC_ROW_WORDS = 256
FIN_TM = 512
FIN_PARTS = 1
VMEM_LIMIT = 56 * 1024 * 1024


def _cparams(sem):
    return pltpu.CompilerParams(dimension_semantics=sem, vmem_limit_bytes=VMEM_LIMIT)


def _sigmoid(x):
    return 0.5 * jnp.tanh(0.5 * x) + 0.5


def _proj_kernel(xn_ref, x0_ref, g_ref, w_ref, wg_ref, wu_ref, wd_ref, o_ref, wg_o, wu_o, wd_o, ha_ref, hb_ref):
    i = pl.program_id(1)

    def normalise(x_ref, rows, h_ref):
        x = x_ref[rows, :]
        ms = jnp.mean(x * x, axis=-1, keepdims=True)
        h_ref[rows, :] = (x * lax.rsqrt(ms + EPS) * g_ref[...]).astype(BF16)

    row_chunks = [pl.ds(r0, PROJ_NORM_ROWS) for r0 in range(0, PROJ_TM, PROJ_NORM_ROWS)]

    @pl.when(i == 0)
    def _():
        for rows in row_chunks:
            normalise(x0_ref, rows, ha_ref)

    def cast_piece(src, dst, rows):
        dst[rows, :] = src[rows, :].astype(BF16)

    def body(cur_ref, nxt_ref):
        side = [functools.partial(normalise, xn_ref, rows, nxt_ref) for rows in row_chunks]
        for src, dst in ((wg_ref, wg_o), (wu_ref, wu_o), (wd_ref, wd_o)):
            half = src.shape[0] // 2
            side += [functools.partial(cast_piece, src, dst, pl.ds(k * half, half)) for k in range(2)]
        side = iter(side)
        for c in range(PROJ_TN // MXU_N):
            sl = slice(c * MXU_N, (c + 1) * MXU_N)
            o_ref[:, sl] = jnp.dot(cur_ref[...], w_ref[:, sl], preferred_element_type=F32).astype(o_ref.dtype)
            if c % 2 == 0:
                next(side, lambda: None)()
        for piece in side:
            piece()

    @pl.when(i % 2 == 0)
    def _():
        body(ha_ref, hb_ref)

    @pl.when(i % 2 == 1)
    def _():
        body(hb_ref, ha_ref)


def _proj(x2, g_mix, w_in_bf16, w_gate, w_up, w_down, interpret):
    n = x2.shape[0]
    n_i = n // PROJ_TM
    steps = (IN_WIDTH // PROJ_TN) * n_i
    flat = [w.reshape(-1, w.shape[-1]) for w in (w_gate, w_up, w_down)]
    w_specs = [pl.BlockSpec((w.shape[0] // steps, w.shape[1]), lambda j, i: (j * n_i + i, 0)) for w in flat]
    outs = pl.pallas_call(
        _proj_kernel,
        grid=(IN_WIDTH // PROJ_TN, n_i),
        in_specs=[
            pl.BlockSpec((PROJ_TM, D_MODEL), lambda j, i: (jnp.minimum(i + 1, n_i - 1), 0)),
            pl.BlockSpec((PROJ_TM, D_MODEL), lambda j, i: (0, 0)),
            pl.BlockSpec((1, D_MODEL), lambda j, i: (0, 0)),
            pl.BlockSpec((D_MODEL, PROJ_TN), lambda j, i: (0, j)),
        ] + w_specs,
        out_specs=[pl.BlockSpec((PROJ_TM, PROJ_TN), lambda j, i: (i, j))] + w_specs,
        out_shape=[jax.ShapeDtypeStruct((n, IN_WIDTH), BF16)]
        + [jax.ShapeDtypeStruct(w.shape, BF16) for w in flat],
        scratch_shapes=[pltpu.VMEM((PROJ_TM, D_MODEL), BF16), pltpu.VMEM((PROJ_TM, D_MODEL), BF16)],
        compiler_params=_cparams(("arbitrary", "arbitrary")),
        interpret=interpret,
        name="proj",
    )(x2, x2, g_mix, w_in_bf16, *flat)
    return outs[0], outs[1].reshape(w_gate.shape), outs[2].reshape(w_up.shape), outs[3].reshape(w_down.shape)


def _attn_unit(q2, kk, vv, bias_a, bias_b):
    lane = lax.broadcasted_iota(jnp.int32, (BLK, LANES), 1)
    left = lane < HEAD_DIM
    zero = jnp.zeros_like(q2)
    nt = (((1,), (1,)), ((), ()))
    q_st = jnp.concatenate([jnp.where(left, q2, zero), jnp.where(left, zero, q2)], axis=0)
    s = lax.dot_general(q_st, kk, nt, preferred_element_type=F32) + jnp.concatenate([bias_a, bias_b], axis=0)
    m = jnp.max(s, axis=-1, keepdims=True)
    p = jnp.exp(s - m)
    den = jnp.sum(p, axis=-1, keepdims=True)
    o = jnp.dot(p.astype(BF16), vv, preferred_element_type=F32)
    return (jnp.where(left, o[:BLK], o[BLK:]), jnp.where(left, m[:BLK], m[BLK:]),
            jnp.where(left, den[:BLK], den[BLK:]))


def _attn_kernel(q1_ref, q2_ref, q3_ref, k1_ref, k2_ref, k3_ref, v1_ref, v2_ref, v3_ref,
                 bias_ref, o_ref, tmp_ref, qd_ref, kvd_ref, acc_ref, accw_ref, *, seq):
    s_id = pl.program_id(2)
    step = ATTN_GROUPS[1][1]

    quarter = SPAN // step
    span_row0 = pl.multiple_of(s_id * SPAN, SPAN)
    first_stage = {1: 0, 2: 3}

    def operands(gi):
        d = ATTN_GROUPS[gi][1]
        q_ref, k_ref, v_ref = ((q2_ref, k2_ref, v2_ref), (q3_ref, k3_ref, v3_ref))[gi - 1]
        n_stage = 1 if d == step else 2
        kv_off = s_id * (SPAN // d)
        return [(q_ref, 0, first_stage[gi], qd_ref, gi - 1, SPAN // d, 0, 0.125),
                (k_ref, span_row0, first_stage[gi] + n_stage, kvd_ref, 2 * (gi - 1), seq // d, kv_off, 1.0),
                (v_ref, span_row0, first_stage[gi] + 2 * n_stage, kvd_ref, 2 * (gi - 1) + 1, seq // d, kv_off, 1.0)]

    def staging_pieces(gi):
        pieces = []
        for src_ref, row0, stage, _, _, _, _, scale in operands(gi):
            for c0 in range(0, SPAN, 256):
                def to_f32(src_ref=src_ref, row0=row0, stage=stage, scale=scale, c0=c0):
                    x = src_ref[pl.ds(pl.multiple_of(row0 + c0, 256), 256), :].astype(F32)
                    tmp_ref[stage, c0:c0 + 256, :] = x if scale == 1.0 else x * scale
                pieces.append(to_f32)
        if ATTN_GROUPS[gi][1] != step:
            for _, _, stage, _, _, _, _, _ in operands(gi):
                for b in range(step):
                    for j0 in range(0, quarter, 256):
                        def first_pass(stage=stage, b=b, j0=j0):
                            tmp_ref[stage + 1, b * quarter + j0:b * quarter + j0 + 256, :] = (
                                tmp_ref[stage, pl.ds(b + step * j0, 256, stride=step), :])
                        pieces.append(first_pass)
        return pieces

    def residue_pieces(gi, r):
        d = ATTN_GROUPS[gi][1]
        chunk = min(256, SPAN // d)
        pieces = []
        for _, _, stage, dst_ref, slot, pitch, off, _ in operands(gi):
            for j0 in range(0, SPAN // d, chunk):
                def last_pass(stage=stage, dst_ref=dst_ref, slot=slot, pitch=pitch, off=off, j0=j0):
                    if d == step:
                        rows = tmp_ref[stage, pl.ds(r + step * j0, chunk, stride=step), :]
                    else:
                        a, b = divmod(r, step)
                        rows = tmp_ref[stage + 1, pl.ds(b * quarter + a + step * j0, chunk, stride=step), :]
                    row0 = r * pitch + j0 + off
                    if not isinstance(row0, int):
                        row0 = pl.multiple_of(row0, chunk)
                    dst_ref[slot, pl.ds(row0, chunk), :] = rows.astype(BF16)
                pieces.append(last_pass)
        return pieces

    def dilated_unit(gi, d, m, r, first):
        slot = gi - 1
        loc = BLK * m * d + r
        q_row = r * (SPAN // d) + BLK * m
        cur = pl.multiple_of(r * (seq // d) + s_id * (SPAN // d) + BLK * m, BLK)
        prev = pl.multiple_of(jnp.where(first == 1, cur, cur - BLK), BLK)
        q2 = qd_ref[slot, pl.ds(q_row, BLK), :]
        kk = jnp.concatenate([kvd_ref[2 * slot, pl.ds(prev, BLK), :], kvd_ref[2 * slot, pl.ds(cur, BLK), :]], axis=0)
        vv = jnp.concatenate([kvd_ref[2 * slot + 1, pl.ds(prev, BLK), :],
                              kvd_ref[2 * slot + 1, pl.ds(cur, BLK), :]], axis=0)
        parts = _attn_unit(q2, kk, vv, bias_ref[gi, first, 0], bias_ref[gi, first, 1])
        for j, part in enumerate(parts):
            if d == step:
                acc_ref[ACC_PARTS * slot + j, pl.ds(loc, BLK, stride=d), :] = part
            else:
                a, b = divmod(r, step)
                accw_ref[j, pl.ds(b * quarter + a, BLK, stride=step), :] = part

    def reinterleave_pieces():
        pieces = []
        for j in range(ACC_PARTS):
            for b in range(step):
                for j0 in range(0, quarter, 256):
                    def piece(j=j, b=b, j0=j0):
                        acc_ref[ACC_PARTS + j, pl.ds(b + step * j0, 256, stride=step), :] = (
                            accw_ref[j, b * quarter + j0:b * quarter + j0 + 256, :])
                    pieces.append(piece)
        return pieces

    def dense_unit(m):
        loc = m * BLK
        cur = pl.multiple_of(s_id * SPAN + loc, BLK)
        prev = pl.multiple_of(jnp.maximum(cur - BLK, 0), BLK)
        first = jnp.where(cur == 0, 1, 0)
        q2 = q1_ref[loc:loc + BLK, :] * 0.125
        kk = jnp.concatenate([k1_ref[pl.ds(prev, BLK), :], k1_ref[pl.ds(cur, BLK), :]], axis=0)
        vv = jnp.concatenate([v1_ref[pl.ds(prev, BLK), :], v1_ref[pl.ds(cur, BLK), :]], axis=0)
        n1, m1, d1 = _attn_unit(q2, kk, vv, bias_ref[0, first, 0], bias_ref[0, first, 1])
        (n2, m2, d2), (n3, m3, d3) = (
            tuple(acc_ref[g * ACC_PARTS + j, loc:loc + BLK, :] for j in range(ACC_PARTS)) for g in range(2))
        mx = jnp.maximum(jnp.maximum(m1, m2), m3)
        w1, w2, w3 = jnp.exp(m1 - mx), jnp.exp(m2 - mx), jnp.exp(m3 - mx)
        num = w1 * n1 + w2 * n2 + w3 * n3
        den = w1 * d1 + w2 * d2 + w3 * d3
        o_ref[loc:loc + BLK, :] = (num / den).astype(o_ref.dtype)

    first_span = jnp.where(s_id == 0, 1, 0)
    for gi in (1, 2):
        for piece in staging_pieces(gi) + [p for r in range(ATTN_GROUPS[gi][1]) for p in residue_pieces(gi, r)]:
            piece()
    for gi in (2, 1):
        d = ATTN_GROUPS[gi][1]
        for m in range(SPAN // (BLK * d)):
            for r in range(d):
                dilated_unit(gi, d, m, r, first_span if m == 0 else 0)
        if d != step:
            for piece in reinterleave_pieces():
                piece()
    for m in range(SPAN // BLK):
        dense_unit(m)


def _attn_bias():
    slopes = np.exp2(-8.0 * np.arange(1, ATTN_HEADS + 1, dtype=np.float64) / ATTN_HEADS)
    qi = np.arange(BLK)[:, None]
    kj = np.arange(2 * BLK)[None, :]
    rel = qi + BLK - kj
    out = np.zeros((N_GROUPS, 2, ATTN_HEADS, BLK, 2 * BLK), np.float32)
    for gi, (window, d) in enumerate(ATTN_GROUPS):
        n_back = window // d
        assert n_back == BLK
        valid = (rel >= 0) & (rel <= n_back)
        bias = -slopes[:, None, None] * (rel * d)[None].astype(np.float64)
        out[gi, 0] = np.where(valid[None], bias, NEG)
        out[gi, 1] = np.where((valid & (kj >= BLK))[None], bias, NEG)
    return jnp.asarray(out)


def _attention(proj, batch, seq, interpret):
    n = batch * seq
    spans = seq // SPAN
    n_hp = GROUP_W // LANES
    qcol = lambda g: (COL_QA + g * GROUP_W) // LANES
    kcol = lambda g: (COL_KA + g * GROUP_W) // LANES
    vcol = lambda g: (COL_VA + g * GROUP_W) // LANES
    q_specs = [pl.BlockSpec((SPAN, LANES), functools.partial(lambda b, hp, s, c: (b * spans + s, c + hp), c=qcol(g)))
               for g in range(N_GROUPS)]
    k_specs = [pl.BlockSpec((seq, LANES), functools.partial(lambda b, hp, s, c: (b, c + hp), c=kcol(g)))
               for g in range(N_GROUPS)]
    v_specs = [pl.BlockSpec((seq, LANES), functools.partial(lambda b, hp, s, c: (b, c + hp), c=vcol(g)))
               for g in range(N_GROUPS)]
    bias_spec = pl.BlockSpec((N_GROUPS, 2, 2, BLK, 2 * BLK), lambda b, hp, s: (0, 0, hp, 0, 0))
    return pl.pallas_call(
        functools.partial(_attn_kernel, seq=seq),
        grid=(batch, n_hp, spans),
        in_specs=q_specs + k_specs + v_specs + [bias_spec],
        out_specs=pl.BlockSpec((SPAN, LANES), lambda b, hp, s: (b * spans + s, hp)),
        out_shape=jax.ShapeDtypeStruct((n, GROUP_W), BF16),
        scratch_shapes=[
            pltpu.VMEM((ATTN_STAGES, SPAN, LANES), F32),
            pltpu.VMEM((2, SPAN, LANES), BF16),
            pltpu.VMEM((4, seq, LANES), BF16),
            pltpu.VMEM((2 * ACC_PARTS, SPAN, LANES), F32),
            pltpu.VMEM((ACC_PARTS, SPAN, LANES), F32),
        ],
        compiler_params=_cparams(("arbitrary", "arbitrary", "arbitrary")),
        interpret=interpret,
        name="attn",
    )(*([proj] * 9), _attn_bias())


def _ret_kernel(q_ref, k_ref, v_ref, gr_ref, dec_ref, xi_ref, zeta_ref, gch_ref, o_ref, st_ref):
    @pl.when(pl.program_id(1) == 0)
    def _():
        st_ref[...] = jnp.zeros_like(st_ref)

    nt = (((1,), (1,)), ((), ()))
    scale = RET_DK ** -0.5

    for c in range(RET_TS // RET_CHUNK):
        rows = pl.ds(c * RET_CHUNK, RET_CHUNK)
        for h in range(RET_HEADS):
            kcols = slice(h * RET_DK, (h + 1) * RET_DK)
            vcols = slice(h * RET_DV, (h + 1) * RET_DV)
            qi = q_ref[rows, kcols]
            kf = k_ref[rows, kcols].astype(F32) * scale
            ki = kf.astype(BF16)
            kz_t = jnp.transpose(kf * zeta_ref[h]).astype(BF16)
            vi = v_ref[rows, vcols]
            att = lax.dot_general(qi, ki, nt, preferred_element_type=F32) * dec_ref[h]
            inner = jnp.dot(att.astype(BF16), vi, preferred_element_type=F32)
            st = st_ref[h]
            cross = jnp.dot(qi, st.astype(BF16), preferred_element_type=F32) * xi_ref[h]
            st_ref[h] = gch_ref[h] * st + jnp.dot(kz_t, vi, preferred_element_type=F32)
            y = inner + cross
            mu = jnp.mean(y, axis=-1, keepdims=True)
            yc = y - mu
            var = jnp.mean(yc * yc, axis=-1, keepdims=True)
            yn = yc * lax.rsqrt(var + EPS)
            g = gr_ref[rows, vcols].astype(F32)
            o_ref[rows, vcols] = (g * _sigmoid(g) * yn).astype(o_ref.dtype)


def _ret_tables():
    c = RET_CHUNK
    log_g = np.log1p(-np.exp2(-5.0 - np.arange(RET_HEADS, dtype=np.float64)))
    pos = np.arange(c, dtype=np.float64)
    diff = pos[:, None] - pos[None, :]
    dec = np.where(diff >= 0, np.exp(log_g[:, None, None] * np.maximum(diff, 0.0)), 0.0)
    xi = np.exp(log_g[:, None] * (pos + 1.0))[..., None] * np.ones((1, 1, RET_DV))
    zeta = np.exp(log_g[:, None] * (c - 1.0 - pos))[..., None] * np.ones((1, 1, RET_DK))
    gch = np.exp(log_g * c)[:, None, None] * np.ones((1, 1, RET_DV))
    return tuple(jnp.asarray(t, F32) for t in (dec, xi, zeta, gch))


def _retention(proj, batch, seq, interpret):
    n = batch * seq
    dec, xi, zeta, gch = _ret_tables()
    qk_w = RET_HEADS * RET_DK
    v_w = RET_HEADS * RET_DV
    nts = seq // RET_TS
    const3 = lambda b, t: (0, 0, 0)
    return pl.pallas_call(
        _ret_kernel,
        grid=(batch, nts),
        in_specs=[
            pl.BlockSpec((RET_TS, qk_w), lambda b, t: (b * nts + t, COL_QR // qk_w)),
            pl.BlockSpec((RET_TS, qk_w), lambda b, t: (b * nts + t, COL_KR // qk_w)),
            pl.BlockSpec((RET_TS, v_w), lambda b, t: (b * nts + t, COL_VR // v_w)),
            pl.BlockSpec((RET_TS, v_w), lambda b, t: (b * nts + t, COL_GR // v_w)),
            pl.BlockSpec((RET_HEADS, RET_CHUNK, RET_CHUNK), const3),
            pl.BlockSpec((RET_HEADS, RET_CHUNK, RET_DV), const3),
            pl.BlockSpec((RET_HEADS, RET_CHUNK, RET_DK), const3),
            pl.BlockSpec((RET_HEADS, 1, RET_DV), const3),
        ],
        out_specs=pl.BlockSpec((RET_TS, v_w), lambda b, t: (b * nts + t, 0)),
        out_shape=jax.ShapeDtypeStruct((n, v_w), BF16),
        scratch_shapes=[pltpu.VMEM((RET_HEADS, RET_DK, RET_DV), F32)],
        compiler_params=_cparams(("arbitrary", "arbitrary")),
        interpret=interpret,
        name="retention",
    )(proj, proj, proj, proj, dec, xi, zeta, gch)


ROUTER_OFF = N_EXPERT_GROUPS


def _pack_bf16_pair(a, b):
    hi = lax.bitcast_convert_type(a.astype(BF16).astype(F32), jnp.uint32)
    lo = lax.bitcast_convert_type(b.astype(BF16).astype(F32), jnp.uint32)
    return lax.bitcast_convert_type(hi | (lo >> 16), jnp.int32)


def _unpack_bf16_pair(w):
    u = lax.bitcast_convert_type(w, jnp.uint32)
    a = lax.bitcast_convert_type(u & jnp.uint32(0xFFFF0000), F32).astype(BF16)
    b = lax.bitcast_convert_type(u << 16, F32).astype(BF16)
    return a, b


def _pack_rows(y):
    q = D_MODEL // 4
    return (_pack_bf16_pair(y[:, 0:q], y[:, 2 * q:3 * q]), _pack_bf16_pair(y[:, q:2 * q], y[:, 3 * q:4 * q]))


def _unpack_rows(slab0, slab1):
    q0, q2 = _unpack_bf16_pair(slab0)
    q1, q3 = _unpack_bf16_pair(slab1)
    return jnp.concatenate([q0, q1, q2, q3], axis=1)


def _mix_kernel(oa_ref, or_ref, ga_ref, gr_ref, x_ref, pa_ref, pr_ref, wo_ref, gf_ref, wr_ref, br_ref,
                x1_ref, h2_ref, route_ref, route_t_ref, cnt_ref, carry_ref, logit_ref):
    step = pl.program_id(0)

    @pl.when(step == 0)
    def _():
        carry_ref[...] = jnp.zeros_like(carry_ref)
        logit_ref[...] = jnp.zeros_like(logit_ref)

    routing = iter([functools.partial(_route_rows, pl.ds(c * ROUTE_CHUNK, ROUTE_CHUNK), step > 0, logit_ref,
                                      route_ref, route_t_ref, cnt_ref, carry_ref)
                    for c in range(MIX_TM // ROUTE_CHUNK)])
    for c in range(MIX_TM // MIX_CHUNK):
        for _ in _mix_rows(pl.ds(c * MIX_CHUNK, MIX_CHUNK), oa_ref, or_ref, ga_ref, gr_ref, x_ref, pa_ref, pr_ref,
                           wo_ref, gf_ref, wr_ref, br_ref, x1_ref, h2_ref, logit_ref):
            next(routing, lambda: None)()
    for piece in routing:
        piece()


def _mix_rows(rows, oa_ref, or_ref, ga_ref, gr_ref, x_ref, pa_ref, pr_ref, wo_ref, gf_ref, wr_ref, br_ref,
              x1_ref, h2_ref, logit_ref):
    a = jnp.dot(oa_ref[rows, :], pa_ref[...], preferred_element_type=F32)
    yield
    r = jnp.dot(or_ref[rows, :], pr_ref[...], preferred_element_type=F32)
    yield
    merged = (_sigmoid(ga_ref[rows, :].astype(F32)) * a + _sigmoid(gr_ref[rows, :].astype(F32)) * r)
    x1 = x_ref[rows, :] + jnp.dot(merged.astype(BF16), wo_ref[...], preferred_element_type=F32)
    x1_ref[rows, :] = x1
    ms = jnp.mean(x1 * x1, axis=-1, keepdims=True)
    h2 = x1 * lax.rsqrt(ms + EPS) * gf_ref[...]
    h2_ref[0, rows, :], h2_ref[1, rows, :] = _pack_rows(h2)

    h_hi = h2.astype(BF16)
    h_lo = (h2 - h_hi.astype(F32)).astype(BF16)
    both = jnp.dot(h_hi, wr_ref[...], preferred_element_type=F32)
    logit_ref[rows, :] = (both[:, :LANES] + both[:, LANES:]
                          + jnp.dot(h_lo, wr_ref[:, :LANES], preferred_element_type=F32) + br_ref[...])


def _route_rows(rows, live, logit_ref, route_ref, route_t_ref, cnt_ref, carry_ref):
    logits = logit_ref[rows, :]
    tm = logits.shape[0]
    lane = lax.broadcasted_iota(jnp.int32, (tm, LANES), 1).astype(F32)
    big = jnp.float32(4 * LANES)
    ninf = -jnp.inf
    is_g = lane < N_EXPERT_GROUPS
    gl = jnp.where(is_g, logits, ninf)
    gmax = jnp.max(gl, axis=-1, keepdims=True)
    gsum = jnp.sum(jnp.where(is_g, jnp.exp(gl - gmax), 0.0), axis=-1, keepdims=True)
    g_val = 1.0 / gsum
    g_idx = jnp.min(jnp.where(jnp.logical_and(is_g, gl == gmax), lane, big), axis=-1, keepdims=True)
    lo = ROUTER_OFF + EXPERTS_PER_GROUP * g_idx
    in_grp = jnp.logical_and(lane >= lo, lane < lo + EXPERTS_PER_GROUP)
    el = jnp.where(in_grp, logits, ninf)
    v1 = jnp.max(el, axis=-1, keepdims=True)
    i1 = jnp.min(jnp.where(jnp.logical_and(in_grp, el == v1), lane, big), axis=-1, keepdims=True)
    rest = jnp.logical_and(in_grp, lane != i1)
    el2 = jnp.where(rest, logits, ninf)
    v2 = jnp.max(el2, axis=-1, keepdims=True)
    i2 = jnp.min(jnp.where(jnp.logical_and(rest, el2 == v2), lane, big), axis=-1, keepdims=True)
    t = jnp.exp(v2 - v1)
    w1 = g_val / (1.0 + t)
    w2 = g_val * t / (1.0 + t)

    sel = jnp.logical_or(lane == i1, lane == i2)
    sel_bf = jnp.where(sel, 1.0, 0.0).astype(BF16)
    row = lax.broadcasted_iota(jnp.int32, (tm, tm), 0)
    col = lax.broadcasted_iota(jnp.int32, (tm, tm), 1)
    tri = jnp.where(col < row, 1.0, 0.0).astype(BF16)
    before = jnp.dot(tri, sel_bf, preferred_element_type=F32) + carry_ref[...]
    r1 = jnp.sum(jnp.where(lane == i1, before, 0.0), axis=-1, keepdims=True)
    r2 = jnp.sum(jnp.where(lane == i2, before, 0.0), axis=-1, keepdims=True)
    carry = carry_ref[...] + jnp.where(live, jnp.sum(jnp.where(sel, 1.0, 0.0), axis=0, keepdims=True), 0.0)
    carry_ref[...] = carry
    cnt_ref[...] = carry

    vals = (i1 - ROUTER_OFF, i2 - ROUTER_OFF, w1, w2, r1, r2)
    route = jnp.zeros((tm, LANES), F32)
    for j, v in enumerate(vals):
        route = jnp.where(lane == j, v, route)
    route_ref[rows, :] = route
    route_t_ref[:, rows] = jnp.transpose(route)[:ROUTE_ROWS, :]


def _mix(o_attn, o_ret, proj, x2, pa, pr, wo, g_ffn, w_router, b_router, interpret):
    n = x2.shape[0]
    tm = MIX_TM
    last = n // tm - 1
    const = lambda i: (0, 0)
    cur = lambda i: jnp.minimum(i, last)
    prev = lambda i: jnp.maximum(i - 1, 0)
    return pl.pallas_call(
        _mix_kernel,
        grid=(n // tm + 1,),
        in_specs=[
            pl.BlockSpec((tm, GROUP_W), lambda i: (cur(i), 0)),
            pl.BlockSpec((tm, D_MODEL), lambda i: (cur(i), 0)),
            pl.BlockSpec((tm, D_MODEL), lambda i: (cur(i), COL_GATE_A // D_MODEL)),
            pl.BlockSpec((tm, D_MODEL), lambda i: (cur(i), COL_GATE_R // D_MODEL)),
            pl.BlockSpec((tm, D_MODEL), lambda i: (cur(i), 0)),
            pl.BlockSpec((GROUP_W, D_MODEL), const),
            pl.BlockSpec((D_MODEL, D_MODEL), const),
            pl.BlockSpec((D_MODEL, D_MODEL), const),
            pl.BlockSpec((1, D_MODEL), const),
            pl.BlockSpec((D_MODEL, 2 * LANES), const),
            pl.BlockSpec((1, LANES), const),
        ],
        out_specs=[
            pl.BlockSpec((tm, D_MODEL), lambda i: (cur(i), 0)),
            pl.BlockSpec((2, tm, SC_ROW_WORDS), lambda i: (0, cur(i), 0)),
            pl.BlockSpec((tm, LANES), lambda i: (prev(i), 0)),
            pl.BlockSpec((ROUTE_ROWS, tm), lambda i: (0, prev(i))),
            pl.BlockSpec((1, LANES), const),
        ],
        out_shape=[
            jax.ShapeDtypeStruct((n, D_MODEL), F32),
            jax.ShapeDtypeStruct((2, n, SC_ROW_WORDS), jnp.int32),
            jax.ShapeDtypeStruct((n, LANES), F32),
            jax.ShapeDtypeStruct((ROUTE_ROWS, n), F32),
            jax.ShapeDtypeStruct((1, LANES), F32),
        ],
        scratch_shapes=[pltpu.VMEM((1, LANES), F32), pltpu.VMEM((tm, LANES), F32)],
        compiler_params=_cparams(("arbitrary",)),
        interpret=interpret,
        name="mix_router",
    )(o_attn, o_ret, proj, proj, x2, pa, pr, wo, g_ffn, w_router, b_router)


def _expert_kernel(te_ref, tr_ref, xs_ref, wg_ref, wu_ref, wd_ref, o_ref):
    i = pl.program_id(0)

    def expert_rows(rows):
        xs = _unpack_rows(xs_ref[0, rows, :], xs_ref[1, rows, :])
        a = jnp.dot(xs, wg_ref[0], preferred_element_type=F32)
        yield
        u = jnp.dot(xs, wu_ref[0], preferred_element_type=F32)
        yield
        hid = (a * _sigmoid(a) * u).astype(BF16)
        y = jnp.dot(hid, wd_ref[0], preferred_element_type=F32)
        yield
        o_ref[0, rows, :], o_ref[1, rows, :] = _pack_rows(y)

    @pl.when(tr_ref[i] == i)
    def _():
        active = [expert_rows(pl.ds(r0, EXP_CHUNK)) for r0 in range(0, EXP_TM, EXP_CHUNK)]
        while active:
            active = [g for g in active if next(g, StopIteration) is not StopIteration]


def _experts(xs, tile_expert, tile_row, w_gate, w_up, w_down, interpret):
    p = xs.shape[1]
    n_tiles = p // EXP_TM
    grid_spec = pltpu.PrefetchScalarGridSpec(
        num_scalar_prefetch=2,
        grid=(n_tiles,),
        in_specs=[
            pl.BlockSpec((2, EXP_TM, SC_ROW_WORDS), lambda i, te, tr: (0, tr[i], 0)),
            pl.BlockSpec((1, D_MODEL, EXPERT_FF), lambda i, te, tr: (te[i], 0, 0)),
            pl.BlockSpec((1, D_MODEL, EXPERT_FF), lambda i, te, tr: (te[i], 0, 0)),
            pl.BlockSpec((1, EXPERT_FF, D_MODEL), lambda i, te, tr: (te[i], 0, 0)),
        ],
        out_specs=pl.BlockSpec((2, EXP_TM, SC_ROW_WORDS), lambda i, te, tr: (0, tr[i], 0)),
    )
    return pl.pallas_call(
        _expert_kernel,
        grid_spec=grid_spec,
        out_shape=jax.ShapeDtypeStruct((2, p, SC_ROW_WORDS), jnp.int32),
        compiler_params=_cparams(("arbitrary",)),
        interpret=interpret,
        name="experts",
    )(tile_expert, tile_row, xs, w_gate, w_up, w_down)


def _final_kernel(x1_ref, yab_ref, route_ref, g_ref, *rest):
    o_ref = rest[-1]
    route = route_ref[...]
    w1 = route[:, 2:3]
    w2 = route[:, 3:4]
    ya = _unpack_rows(yab_ref[0], yab_ref[2]).astype(F32)
    yb = _unpack_rows(yab_ref[1], yab_ref[3]).astype(F32)
    x2 = x1_ref[...] + w1 * ya + w2 * yb
    ms = jnp.mean(x2 * x2, axis=-1, keepdims=True)
    o_ref[...] = x2 * lax.rsqrt(ms + EPS) * g_ref[...]


def _final(x1, yab, route, g_final, prev_out, part, interpret):
    n = x1.shape[0]
    tm = FIN_TM
    steps = n // (tm * FIN_PARTS)
    row = lambda i: (part * steps + i, 0)
    return pl.pallas_call(
        _final_kernel,
        grid=(steps,),
        in_specs=[
            pl.BlockSpec((tm, D_MODEL), row),
            pl.BlockSpec((4, tm, SC_ROW_WORDS), lambda i: (0, i, 0)),
            pl.BlockSpec((tm, LANES), row),
            pl.BlockSpec((1, D_MODEL), lambda i: (0, 0)),
        ] + ([] if prev_out is None else [pl.BlockSpec(memory_space=pl.ANY)]),
        out_specs=pl.BlockSpec((tm, D_MODEL), row),
        out_shape=jax.ShapeDtypeStruct((n, D_MODEL), F32),
        input_output_aliases={} if prev_out is None else {4: 0},
        compiler_params=_cparams(("arbitrary",)),
        interpret=interpret,
        name="combine_final",
    )(x1, yab, route, g_final, *([] if prev_out is None else [prev_out]))


def _permute_w_in(w_in):
    splits = np.cumsum([QKV_W, QKV_W, QKV_W, 512, 512, 1024, 1024, D_MODEL, D_MODEL])[:-1].tolist()
    qa, ka, va, qr, kr, vr, gr, gate_a, gate_r = jnp.split(w_in, splits, axis=-1)
    return jnp.concatenate([gate_a, gate_r, vr, gr, qa, ka, va, qr, kr], axis=-1).astype(BF16)


def _dest_kernel(offs_ref, route_t_ref, idx_ref, *, n_rows):
    route_t = route_t_ref[...]
    experts = route_t[0:2, :]
    dest = route_t[4:6, :].astype(jnp.int32)
    for e in range(N_EXPERTS):
        dest = dest + jnp.where(experts == float(e), offs_ref[e], 0)
    idx_ref[0:2, :] = dest
    idx_ref[2:4, :] = dest + n_rows


def _route_plan(route_t, counts, n, interpret):
    cnt = counts[0, ROUTER_OFF:ROUTER_OFF + N_EXPERTS].astype(jnp.int32)
    padded = ((cnt + EXP_TM - 1) // EXP_TM) * EXP_TM
    ends = jnp.cumsum(padded)
    offs = ends - padded
    n_rows = 2 * n + N_EXPERTS * EXP_TM
    idx4 = pl.pallas_call(
        functools.partial(_dest_kernel, n_rows=n_rows),
        grid_spec=pltpu.PrefetchScalarGridSpec(
            num_scalar_prefetch=1, grid=(1,),
            in_specs=[pl.BlockSpec(route_t.shape, lambda i, offs: (0, 0))],
            out_specs=pl.BlockSpec((4, n), lambda i, offs: (0, 0))),
        out_shape=jax.ShapeDtypeStruct((4, n), jnp.int32),
        interpret=interpret,
        name="route_dest",
    )(offs, route_t)
    tile_row = jnp.minimum(jnp.arange(n_rows // EXP_TM, dtype=jnp.int32), ends[-1] // EXP_TM - 1)
    tile_expert = jnp.sum((ends[None, :] <= (tile_row * EXP_TM)[:, None]).astype(jnp.int32), axis=1)
    return idx4, tile_expert, tile_row, n_rows


def _sc_mesh():
    return plsc.VectorSubcoreMesh(core_axis_name="core", subcore_axis_name="subcore")


def _sc_scatter_rows(rows, idx4, n_out):
    n_in, w = rows.shape
    nb = idx4.shape[1] // SC_WINDOW

    @functools.partial(pl.kernel, out_type=jax.ShapeDtypeStruct((n_out, w), rows.dtype), mesh=_sc_mesh(),
                       scratch_types=[], name="sc_scatter_rows")
    def scatter(x_hbm, ia_hbm, ib_hbm, o_hbm):
        def body(x_vmem, ia_vmem, ib_vmem):
            pltpu.sync_copy(x_vmem, o_hbm.at[ia_vmem.at[0]])
            pltpu.sync_copy(x_vmem, o_hbm.at[ib_vmem.at[0]])

        pltpu.emit_pipeline(
            body,
            grid=(n_in // SC_WINDOW,),
            in_specs=[pl.BlockSpec((SC_WINDOW, w), lambda i: (i, 0)),
                      pl.BlockSpec((1, SC_WINDOW), lambda i: (2 * (i // nb), i % nb)),
                      pl.BlockSpec((1, SC_WINDOW), lambda i: (2 * (i // nb) + 1, i % nb))],
            out_specs=[],
            core_axis_name=("core", "subcore"),
            dimension_semantics=(pltpu.PARALLEL,),
        )(x_hbm, ia_hbm, ib_hbm)

    return scatter(rows, idx4, idx4)


def _sc_gather_rows(table, idx4, part):
    nb = idx4.shape[1] // (SC_WINDOW * FIN_PARTS)
    n_idx = idx4.shape[0] * nb * SC_WINDOW
    w = table.shape[1]

    @functools.partial(pl.kernel, out_type=jax.ShapeDtypeStruct((n_idx, w), table.dtype), mesh=_sc_mesh(),
                       scratch_types=[], name="sc_gather_rows")
    def gather(t_hbm, i_hbm, o_hbm):
        def body(i_vmem, o_vmem):
            pltpu.sync_copy(t_hbm.at[i_vmem.at[0]], o_vmem)

        pltpu.emit_pipeline(
            body,
            grid=(n_idx // SC_WINDOW,),
            in_specs=[pl.BlockSpec((1, SC_WINDOW), lambda i: (i // nb, part * nb + i % nb))],
            out_specs=[pl.BlockSpec((SC_WINDOW, w), lambda i: (i, 0))],
            core_axis_name=("core", "subcore"),
            dimension_semantics=(pltpu.PARALLEL,),
        )(i_hbm, o_hbm)

    return gather(table, idx4)


def _forward(x, g_mix, w_in, w_attn_branch, w_ret_branch, w_out, g_ffn, w_group_router, b_group_router,
             w_expert_router, b_expert_router, w_gate, w_up, w_down, g_final, interpret=False):
    batch, seq, d = x.shape
    n = batch * seq
    x2 = x.reshape(n, d)
    proj, wg_bf, wu_bf, wd_bf = _proj(x2, g_mix[0][None, :], _permute_w_in(w_in[0]), w_gate[0], w_up[0],
                                      w_down[0], interpret)
    o_attn = _attention(proj, batch, seq, interpret)
    o_ret = _retention(proj, batch, seq, interpret)
    pad = LANES - N_EXPERT_GROUPS - N_EXPERTS
    w_router = jnp.concatenate([w_group_router[0], w_expert_router[0], jnp.zeros((d, pad), F32)], axis=-1)
    w_router_hi = w_router.astype(BF16)
    w_router_lo = (w_router - w_router_hi.astype(F32)).astype(BF16)
    w_router2 = jnp.concatenate([w_router_hi, w_router_lo], axis=-1)
    b_router = jnp.concatenate([b_group_router[0], b_expert_router[0], jnp.zeros((pad,), F32)])[None, :]
    x1, h2p, route, route_t, counts = _mix(o_attn, o_ret, proj, x2, w_attn_branch[0].astype(BF16),
                                           w_ret_branch[0].astype(BF16), w_out[0].astype(BF16),
                                           g_ffn[0][None, :], w_router2, b_router, interpret)
    idx4, tile_expert, tile_row, n_rows = _route_plan(route_t, counts, n, interpret)
    xs = _sc_scatter_rows(h2p.reshape(2 * n, SC_ROW_WORDS), idx4, 2 * n_rows)
    ys = _experts(xs.reshape(2, n_rows, SC_ROW_WORDS), tile_expert, tile_row, wg_bf, wu_bf, wd_bf, interpret)
    ys_rows = ys.reshape(2 * n_rows, SC_ROW_WORDS)
    out = None
    for part in range(FIN_PARTS):
        yab = _sc_gather_rows(ys_rows, idx4, part)
        out = _final(x1, yab.reshape(4, n // FIN_PARTS, SC_ROW_WORDS), route, g_final[None, :], out, part, interpret)
    return out.reshape(batch, seq, d)


def kernel(x, g_mix, w_in, w_attn_branch, w_ret_branch, w_out, g_ffn, w_group_router, b_group_router,
           w_expert_router, b_expert_router, w_gate, w_up, w_down, g_final):
    return _forward(x, g_mix, w_in, w_attn_branch, w_ret_branch, w_out, g_ffn, w_group_router,
                    b_group_router, w_expert_router, b_expert_router, w_gate, w_up, w_down, g_final)
```

```python
import functools

import numpy as np
import jax
import jax.numpy as jnp
from jax import lax
from jax.experimental import pallas as pl
from jax.experimental.pallas import tpu as pltpu
from jax.experimental.pallas import tpu_sc as plsc

F32 = jnp.float32
BF16 = jnp.bfloat16

D_MODEL = 1024
ATTN_GROUPS = ((128, 1), (512, 4), (2048, 16))
N_GROUPS = len(ATTN_GROUPS)
ATTN_HEADS = 8
HEAD_DIM = 64
GROUP_W = ATTN_HEADS * HEAD_DIM
QKV_W = N_GROUPS * GROUP_W
RET_HEADS = 4
RET_DK = 128
RET_DV = 256
RET_CHUNK = 128
RET_TS = 1024
N_EXPERT_GROUPS = 4
EXPERTS_PER_GROUP = 8
N_EXPERTS = N_EXPERT_GROUPS * EXPERTS_PER_GROUP
EXPERT_FF = 512
EPS = 1e-6

LANES = 128
BLK = 128
SPAN = 2048
NEG = -1e30
ACC_PARTS = 3
ATTN_STAGES = 9

COL_GATE_A = 0
COL_GATE_R = 1024
COL_VR = 2048
COL_GR = COL_VR + RET_HEADS * RET_DV
COL_QA = COL_GR + RET_HEADS * RET_DV
COL_KA = COL_QA + QKV_W
COL_VA = COL_KA + QKV_W
COL_QR = COL_VA + QKV_W
COL_KR = COL_QR + RET_HEADS * RET_DK
IN_WIDTH = COL_KR + RET_HEADS * RET_DK

PROJ_TM = 512
PROJ_TN = IN_WIDTH // 2
MXU_N = 256
MIX_TM = 1024
MIX_CHUNK = 512
ROUTE_CHUNK = 256
ROUTE_ROWS = 8
EXP_TM = 512
SC_WINDOW = 128
SC_ROW_WORDS = 256
FIN_TM = 1024
FIN_PARTS = 1
VMEM_LIMIT = 56 * 1024 * 1024


def _cparams(sem):
    return pltpu.CompilerParams(dimension_semantics=sem, vmem_limit_bytes=VMEM_LIMIT)


def _sigmoid(x):
    return 0.5 * jnp.tanh(0.5 * x) + 0.5


def _proj_kernel(x_ref, g_ref, w_ref, wg_ref, wu_ref, wd_ref, o_ref, wg_o, wu_o, wd_o):
    x = x_ref[...]
    ms = jnp.mean(x * x, axis=-1, keepdims=True)
    h = (x * lax.rsqrt(ms + EPS) * g_ref[...]).astype(BF16)
    for c in range(PROJ_TN // MXU_N):
        sl = slice(c * MXU_N, (c + 1) * MXU_N)
        o_ref[:, sl] = jnp.dot(h, w_ref[:, sl], preferred_element_type=F32).astype(o_ref.dtype)
    wg_o[...] = wg_ref[...].astype(BF16)
    wu_o[...] = wu_ref[...].astype(BF16)
    wd_o[...] = wd_ref[...].astype(BF16)


def _proj(x2, g_mix, w_in_bf16, w_gate, w_up, w_down, interpret):
    n = x2.shape[0]
    n_i = n // PROJ_TM
    steps = (IN_WIDTH // PROJ_TN) * n_i
    flat = [w.reshape(-1, w.shape[-1]) for w in (w_gate, w_up, w_down)]
    w_specs = [pl.BlockSpec((w.shape[0] // steps, w.shape[1]), lambda j, i: (j * n_i + i, 0)) for w in flat]
    outs = pl.pallas_call(
        _proj_kernel,
        grid=(IN_WIDTH // PROJ_TN, n_i),
        in_specs=[
            pl.BlockSpec((PROJ_TM, D_MODEL), lambda j, i: (i, 0)),
            pl.BlockSpec((1, D_MODEL), lambda j, i: (0, 0)),
            pl.BlockSpec((D_MODEL, PROJ_TN), lambda j, i: (0, j)),
        ] + w_specs,
        out_specs=[pl.BlockSpec((PROJ_TM, PROJ_TN), lambda j, i: (i, j))] + w_specs,
        out_shape=[jax.ShapeDtypeStruct((n, IN_WIDTH), BF16)]
        + [jax.ShapeDtypeStruct(w.shape, BF16) for w in flat],
        compiler_params=_cparams(("arbitrary", "arbitrary")),
        interpret=interpret,
        name="proj",
    )(x2, g_mix, w_in_bf16, *flat)
    return outs[0], outs[1].reshape(w_gate.shape), outs[2].reshape(w_up.shape), outs[3].reshape(w_down.shape)


def _attn_unit(q2, kk, vv, bias_a, bias_b):
    lane = lax.broadcasted_iota(jnp.int32, (BLK, LANES), 1)
    left = lane < HEAD_DIM
    zero = jnp.zeros_like(q2)
    nt = (((1,), (1,)), ((), ()))
    q_st = jnp.concatenate([jnp.where(left, q2, zero), jnp.where(left, zero, q2)], axis=0)
    s = lax.dot_general(q_st, kk, nt, preferred_element_type=F32) + jnp.concatenate([bias_a, bias_b], axis=0)
    m = jnp.max(s, axis=-1, keepdims=True)
    p = jnp.exp(s - m)
    den = jnp.sum(p, axis=-1, keepdims=True)
    o = jnp.dot(p.astype(BF16), vv, preferred_element_type=F32)
    return (jnp.where(left, o[:BLK], o[BLK:]), jnp.where(left, m[:BLK], m[BLK:]),
            jnp.where(left, den[:BLK], den[BLK:]))


def _attn_kernel(q1_ref, q2_ref, q3_ref, k1_ref, k2_ref, k3_ref, v1_ref, v2_ref, v3_ref,
                 bias_ref, o_ref, tmp_ref, qd_ref, kvd_ref, acc_ref, accw_ref, *, seq):
    s_id = pl.program_id(2)
    step = ATTN_GROUPS[1][1]

    quarter = SPAN // step
    span_row0 = pl.multiple_of(s_id * SPAN, SPAN)
    first_stage = {1: 0, 2: 3}

    def operands(gi):
        d = ATTN_GROUPS[gi][1]
        q_ref, k_ref, v_ref = ((q2_ref, k2_ref, v2_ref), (q3_ref, k3_ref, v3_ref))[gi - 1]
        n_stage = 1 if d == step else 2
        kv_off = s_id * (SPAN // d)
        return [(q_ref, 0, first_stage[gi], qd_ref, gi - 1, SPAN // d, 0, 0.125),
                (k_ref, span_row0, first_stage[gi] + n_stage, kvd_ref, 2 * (gi - 1), seq // d, kv_off, 1.0),
                (v_ref, span_row0, first_stage[gi] + 2 * n_stage, kvd_ref, 2 * (gi - 1) + 1, seq // d, kv_off, 1.0)]

    def staging_pieces(gi):
        pieces = []
        for src_ref, row0, stage, _, _, _, _, scale in operands(gi):
            for c0 in range(0, SPAN, 256):
                def to_f32(src_ref=src_ref, row0=row0, stage=stage, scale=scale, c0=c0):
                    x = src_ref[pl.ds(pl.multiple_of(row0 + c0, 256), 256), :].astype(F32)
                    tmp_ref[stage, c0:c0 + 256, :] = x if scale == 1.0 else x * scale
                pieces.append(to_f32)
        if ATTN_GROUPS[gi][1] != step:
            for _, _, stage, _, _, _, _, _ in operands(gi):
                for b in range(step):
                    for j0 in range(0, quarter, 256):
                        def first_pass(stage=stage, b=b, j0=j0):
                            tmp_ref[stage + 1, b * quarter + j0:b * quarter + j0 + 256, :] = (
                                tmp_ref[stage, pl.ds(b + step * j0, 256, stride=step), :])
                        pieces.append(first_pass)
        return pieces

    def residue_pieces(gi, r):
        d = ATTN_GROUPS[gi][1]
        chunk = min(256, SPAN // d)
        pieces = []
        for _, _, stage, dst_ref, slot, pitch, off, _ in operands(gi):
            for j0 in range(0, SPAN // d, chunk):
                def last_pass(stage=stage, dst_ref=dst_ref, slot=slot, pitch=pitch, off=off, j0=j0):
                    if d == step:
                        rows = tmp_ref[stage, pl.ds(r + step * j0, chunk, stride=step), :]
                    else:
                        a, b = divmod(r, step)
                        rows = tmp_ref[stage + 1, pl.ds(b * quarter + a + step * j0, chunk, stride=step), :]
                    row0 = r * pitch + j0 + off
                    if not isinstance(row0, int):
                        row0 = pl.multiple_of(row0, chunk)
                    dst_ref[slot, pl.ds(row0, chunk), :] = rows.astype(BF16)
                pieces.append(last_pass)
        return pieces

    def dilated_unit(gi, d, m, r, first):
        slot = gi - 1
        loc = BLK * m * d + r
        q_row = r * (SPAN // d) + BLK * m
        cur = pl.multiple_of(r * (seq // d) + s_id * (SPAN // d) + BLK * m, BLK)
        prev = pl.multiple_of(jnp.where(first == 1, cur, cur - BLK), BLK)
        q2 = qd_ref[slot, pl.ds(q_row, BLK), :]
        kk = jnp.concatenate([kvd_ref[2 * slot, pl.ds(prev, BLK), :], kvd_ref[2 * slot, pl.ds(cur, BLK), :]], axis=0)
        vv = jnp.concatenate([kvd_ref[2 * slot + 1, pl.ds(prev, BLK), :],
                              kvd_ref[2 * slot + 1, pl.ds(cur, BLK), :]], axis=0)
        parts = _attn_unit(q2, kk, vv, bias_ref[gi, first, 0], bias_ref[gi, first, 1])
        for j, part in enumerate(parts):
            if d == step:
                acc_ref[ACC_PARTS * slot + j, pl.ds(loc, BLK, stride=d), :] = part
            else:
                a, b = divmod(r, step)
                accw_ref[j, pl.ds(b * quarter + a, BLK, stride=step), :] = part

    def reinterleave_pieces():
        pieces = []
        for j in range(ACC_PARTS):
            for b in range(step):
                for j0 in range(0, quarter, 256):
                    def piece(j=j, b=b, j0=j0):
                        acc_ref[ACC_PARTS + j, pl.ds(b + step * j0, 256, stride=step), :] = (
                            accw_ref[j, b * quarter + j0:b * quarter + j0 + 256, :])
                    pieces.append(piece)
        return pieces

    def dense_unit(m):
        loc = m * BLK
        cur = pl.multiple_of(s_id * SPAN + loc, BLK)
        prev = pl.multiple_of(jnp.maximum(cur - BLK, 0), BLK)
        first = jnp.where(cur == 0, 1, 0)
        q2 = q1_ref[loc:loc + BLK, :] * 0.125
        kk = jnp.concatenate([k1_ref[pl.ds(prev, BLK), :], k1_ref[pl.ds(cur, BLK), :]], axis=0)
        vv = jnp.concatenate([v1_ref[pl.ds(prev, BLK), :], v1_ref[pl.ds(cur, BLK), :]], axis=0)
        n1, m1, d1 = _attn_unit(q2, kk, vv, bias_ref[0, first, 0], bias_ref[0, first, 1])
        (n2, m2, d2), (n3, m3, d3) = (
            tuple(acc_ref[g * ACC_PARTS + j, loc:loc + BLK, :] for j in range(ACC_PARTS)) for g in range(2))
        mx = jnp.maximum(jnp.maximum(m1, m2), m3)
        w1, w2, w3 = jnp.exp(m1 - mx), jnp.exp(m2 - mx), jnp.exp(m3 - mx)
        num = w1 * n1 + w2 * n2 + w3 * n3
        den = w1 * d1 + w2 * d2 + w3 * d3
        o_ref[loc:loc + BLK, :] = (num / den).astype(o_ref.dtype)

    first_span = jnp.where(s_id == 0, 1, 0)
    for gi in (1, 2):
        for piece in staging_pieces(gi) + [p for r in range(ATTN_GROUPS[gi][1]) for p in residue_pieces(gi, r)]:
            piece()
    for gi in (2, 1):
        d = ATTN_GROUPS[gi][1]
        for m in range(SPAN // (BLK * d)):
            for r in range(d):
                dilated_unit(gi, d, m, r, first_span if m == 0 else 0)
        if d != step:
            for piece in reinterleave_pieces():
                piece()
    for m in range(SPAN // BLK):
        dense_unit(m)


def _attn_bias():
    slopes = np.exp2(-8.0 * np.arange(1, ATTN_HEADS + 1, dtype=np.float64) / ATTN_HEADS)
    qi = np.arange(BLK)[:, None]
    kj = np.arange(2 * BLK)[None, :]
    rel = qi + BLK - kj
    out = np.zeros((N_GROUPS, 2, ATTN_HEADS, BLK, 2 * BLK), np.float32)
    for gi, (window, d) in enumerate(ATTN_GROUPS):
        n_back = window // d
        assert n_back == BLK
        valid = (rel >= 0) & (rel <= n_back)
        bias = -slopes[:, None, None] * (rel * d)[None].astype(np.float64)
        out[gi, 0] = np.where(valid[None], bias, NEG)
        out[gi, 1] = np.where((valid & (kj >= BLK))[None], bias, NEG)
    return jnp.asarray(out)


def _attention(proj, batch, seq, interpret):
    n = batch * seq
    spans = seq // SPAN
    n_hp = GROUP_W // LANES
    qcol = lambda g: (COL_QA + g * GROUP_W) // LANES
    kcol = lambda g: (COL_KA + g * GROUP_W) // LANES
    vcol = lambda g: (COL_VA + g * GROUP_W) // LANES
    q_specs = [pl.BlockSpec((SPAN, LANES), functools.partial(lambda b, hp, s, c: (b * spans + s, c + hp), c=qcol(g)))
               for g in range(N_GROUPS)]
    k_specs = [pl.BlockSpec((seq, LANES), functools.partial(lambda b, hp, s, c: (b, c + hp), c=kcol(g)))
               for g in range(N_GROUPS)]
    v_specs = [pl.BlockSpec((seq, LANES), functools.partial(lambda b, hp, s, c: (b, c + hp), c=vcol(g)))
               for g in range(N_GROUPS)]
    bias_spec = pl.BlockSpec((N_GROUPS, 2, 2, BLK, 2 * BLK), lambda b, hp, s: (0, 0, hp, 0, 0))
    return pl.pallas_call(
        functools.partial(_attn_kernel, seq=seq),
        grid=(batch, n_hp, spans),
        in_specs=q_specs + k_specs + v_specs + [bias_spec],
        out_specs=pl.BlockSpec((SPAN, LANES), lambda b, hp, s: (b * spans + s, hp)),
        out_shape=jax.ShapeDtypeStruct((n, GROUP_W), BF16),
        scratch_shapes=[
            pltpu.VMEM((ATTN_STAGES, SPAN, LANES), F32),
            pltpu.VMEM((2, SPAN, LANES), BF16),
            pltpu.VMEM((4, seq, LANES), BF16),
            pltpu.VMEM((2 * ACC_PARTS, SPAN, LANES), F32),
            pltpu.VMEM((ACC_PARTS, SPAN, LANES), F32),
        ],
        compiler_params=_cparams(("arbitrary", "arbitrary", "arbitrary")),
        interpret=interpret,
        name="attn",
    )(*([proj] * 9), _attn_bias())


def _ret_kernel(q_ref, k_ref, v_ref, gr_ref, dec_ref, xi_ref, zeta_ref, gch_ref, o_ref, st_ref):
    @pl.when(pl.program_id(1) == 0)
    def _():
        st_ref[...] = jnp.zeros_like(st_ref)

    nt = (((1,), (1,)), ((), ()))
    scale = RET_DK ** -0.5

    for c in range(RET_TS // RET_CHUNK):
        rows = pl.ds(c * RET_CHUNK, RET_CHUNK)
        for h in range(RET_HEADS):
            kcols = slice(h * RET_DK, (h + 1) * RET_DK)
            vcols = slice(h * RET_DV, (h + 1) * RET_DV)
            qi = q_ref[rows, kcols]
            kf = k_ref[rows, kcols].astype(F32) * scale
            ki = kf.astype(BF16)
            kz_t = jnp.transpose(kf * zeta_ref[h]).astype(BF16)
            vi = v_ref[rows, vcols]
            att = lax.dot_general(qi, ki, nt, preferred_element_type=F32) * dec_ref[h]
            inner = jnp.dot(att.astype(BF16), vi, preferred_element_type=F32)
            st = st_ref[h]
            cross = jnp.dot(qi, st.astype(BF16), preferred_element_type=F32) * xi_ref[h]
            st_ref[h] = gch_ref[h] * st + jnp.dot(kz_t, vi, preferred_element_type=F32)
            y = inner + cross
            mu = jnp.mean(y, axis=-1, keepdims=True)
            yc = y - mu
            var = jnp.mean(yc * yc, axis=-1, keepdims=True)
            yn = yc * lax.rsqrt(var + EPS)
            g = gr_ref[rows, vcols].astype(F32)
            o_ref[rows, vcols] = (g * _sigmoid(g) * yn).astype(o_ref.dtype)


def _ret_tables():
    c = RET_CHUNK
    log_g = np.log1p(-np.exp2(-5.0 - np.arange(RET_HEADS, dtype=np.float64)))
    pos = np.arange(c, dtype=np.float64)
    diff = pos[:, None] - pos[None, :]
    dec = np.where(diff >= 0, np.exp(log_g[:, None, None] * np.maximum(diff, 0.0)), 0.0)
    xi = np.exp(log_g[:, None] * (pos + 1.0))[..., None] * np.ones((1, 1, RET_DV))
    zeta = np.exp(log_g[:, None] * (c - 1.0 - pos))[..., None] * np.ones((1, 1, RET_DK))
    gch = np.exp(log_g * c)[:, None, None] * np.ones((1, 1, RET_DV))
    return tuple(jnp.asarray(t, F32) for t in (dec, xi, zeta, gch))


def _retention(proj, batch, seq, interpret):
    n = batch * seq
    dec, xi, zeta, gch = _ret_tables()
    qk_w = RET_HEADS * RET_DK
    v_w = RET_HEADS * RET_DV
    nts = seq // RET_TS
    const3 = lambda b, t: (0, 0, 0)
    return pl.pallas_call(
        _ret_kernel,
        grid=(batch, nts),
        in_specs=[
            pl.BlockSpec((RET_TS, qk_w), lambda b, t: (b * nts + t, COL_QR // qk_w)),
            pl.BlockSpec((RET_TS, qk_w), lambda b, t: (b * nts + t, COL_KR // qk_w)),
            pl.BlockSpec((RET_TS, v_w), lambda b, t: (b * nts + t, COL_VR // v_w)),
            pl.BlockSpec((RET_TS, v_w), lambda b, t: (b * nts + t, COL_GR // v_w)),
            pl.BlockSpec((RET_HEADS, RET_CHUNK, RET_CHUNK), const3),
            pl.BlockSpec((RET_HEADS, RET_CHUNK, RET_DV), const3),
            pl.BlockSpec((RET_HEADS, RET_CHUNK, RET_DK), const3),
            pl.BlockSpec((RET_HEADS, 1, RET_DV), const3),
        ],
        out_specs=pl.BlockSpec((RET_TS, v_w), lambda b, t: (b * nts + t, 0)),
        out_shape=jax.ShapeDtypeStruct((n, v_w), BF16),
        scratch_shapes=[pltpu.VMEM((RET_HEADS, RET_DK, RET_DV), F32)],
        compiler_params=_cparams(("arbitrary", "arbitrary")),
        interpret=interpret,
        name="retention",
    )(proj, proj, proj, proj, dec, xi, zeta, gch)


ROUTER_OFF = N_EXPERT_GROUPS


def _pack_bf16_pair(a, b):
    hi = lax.bitcast_convert_type(a.astype(BF16).astype(F32), jnp.uint32)
    lo = lax.bitcast_convert_type(b.astype(BF16).astype(F32), jnp.uint32)
    return lax.bitcast_convert_type(hi | (lo >> 16), jnp.int32)


def _unpack_bf16_pair(w):
    u = lax.bitcast_convert_type(w, jnp.uint32)
    a = lax.bitcast_convert_type(u & jnp.uint32(0xFFFF0000), F32).astype(BF16)
    b = lax.bitcast_convert_type(u << 16, F32).astype(BF16)
    return a, b


def _pack_rows(y):
    q = D_MODEL // 4
    return (_pack_bf16_pair(y[:, 0:q], y[:, 2 * q:3 * q]), _pack_bf16_pair(y[:, q:2 * q], y[:, 3 * q:4 * q]))


def _unpack_rows(slab0, slab1):
    q0, q2 = _unpack_bf16_pair(slab0)
    q1, q3 = _unpack_bf16_pair(slab1)
    return jnp.concatenate([q0, q1, q2, q3], axis=1)


def _mix_kernel(oa_ref, or_ref, ga_ref, gr_ref, x_ref, pa_ref, pr_ref, wo_ref, gf_ref, wr_ref, br_ref,
                x1_ref, h2_ref, route_ref, route_t_ref, cnt_ref, carry_ref, logit_ref):
    step = pl.program_id(0)

    @pl.when(step == 0)
    def _():
        carry_ref[...] = jnp.zeros_like(carry_ref)
        logit_ref[...] = jnp.zeros_like(logit_ref)

    routing = iter([functools.partial(_route_rows, pl.ds(c * ROUTE_CHUNK, ROUTE_CHUNK), step > 0, logit_ref,
                                      route_ref, route_t_ref, cnt_ref, carry_ref)
                    for c in range(MIX_TM // ROUTE_CHUNK)])
    for c in range(MIX_TM // MIX_CHUNK):
        for _ in _mix_rows(pl.ds(c * MIX_CHUNK, MIX_CHUNK), oa_ref, or_ref, ga_ref, gr_ref, x_ref, pa_ref, pr_ref,
                           wo_ref, gf_ref, wr_ref, br_ref, x1_ref, h2_ref, logit_ref):
            next(routing, lambda: None)()
    for piece in routing:
        piece()


def _mix_rows(rows, oa_ref, or_ref, ga_ref, gr_ref, x_ref, pa_ref, pr_ref, wo_ref, gf_ref, wr_ref, br_ref,
              x1_ref, h2_ref, logit_ref):
    a = jnp.dot(oa_ref[rows, :], pa_ref[...], preferred_element_type=F32)
    yield
    r = jnp.dot(or_ref[rows, :], pr_ref[...], preferred_element_type=F32)
    yield
    merged = (_sigmoid(ga_ref[rows, :].astype(F32)) * a + _sigmoid(gr_ref[rows, :].astype(F32)) * r)
    x1 = x_ref[rows, :] + jnp.dot(merged.astype(BF16), wo_ref[...], preferred_element_type=F32)
    x1_ref[rows, :] = x1
    ms = jnp.mean(x1 * x1, axis=-1, keepdims=True)
    h2 = x1 * lax.rsqrt(ms + EPS) * gf_ref[...]
    h2_ref[0, rows, :], h2_ref[1, rows, :] = _pack_rows(h2)

    h_hi = h2.astype(BF16)
    h_lo = (h2 - h_hi.astype(F32)).astype(BF16)
    both = jnp.dot(h_hi, wr_ref[...], preferred_element_type=F32)
    logit_ref[rows, :] = (both[:, :LANES] + both[:, LANES:]
                          + jnp.dot(h_lo, wr_ref[:, :LANES], preferred_element_type=F32) + br_ref[...])


def _route_rows(rows, live, logit_ref, route_ref, route_t_ref, cnt_ref, carry_ref):
    logits = logit_ref[rows, :]
    tm = logits.shape[0]
    lane = lax.broadcasted_iota(jnp.int32, (tm, LANES), 1).astype(F32)
    big = jnp.float32(4 * LANES)
    ninf = -jnp.inf
    is_g = lane < N_EXPERT_GROUPS
    gl = jnp.where(is_g, logits, ninf)
    gmax = jnp.max(gl, axis=-1, keepdims=True)
    gsum = jnp.sum(jnp.where(is_g, jnp.exp(gl - gmax), 0.0), axis=-1, keepdims=True)
    g_val = 1.0 / gsum
    g_idx = jnp.min(jnp.where(jnp.logical_and(is_g, gl == gmax), lane, big), axis=-1, keepdims=True)
    lo = ROUTER_OFF + EXPERTS_PER_GROUP * g_idx
    in_grp = jnp.logical_and(lane >= lo, lane < lo + EXPERTS_PER_GROUP)
    el = jnp.where(in_grp, logits, ninf)
    v1 = jnp.max(el, axis=-1, keepdims=True)
    i1 = jnp.min(jnp.where(jnp.logical_and(in_grp, el == v1), lane, big), axis=-1, keepdims=True)
    rest = jnp.logical_and(in_grp, lane != i1)
    el2 = jnp.where(rest, logits, ninf)
    v2 = jnp.max(el2, axis=-1, keepdims=True)
    i2 = jnp.min(jnp.where(jnp.logical_and(rest, el2 == v2), lane, big), axis=-1, keepdims=True)
    t = jnp.exp(v2 - v1)
    w1 = g_val / (1.0 + t)
    w2 = g_val * t / (1.0 + t)

    sel = jnp.logical_or(lane == i1, lane == i2)
    sel_bf = jnp.where(sel, 1.0, 0.0).astype(BF16)
    row = lax.broadcasted_iota(jnp.int32, (tm, tm), 0)
    col = lax.broadcasted_iota(jnp.int32, (tm, tm), 1)
    tri = jnp.where(col < row, 1.0, 0.0).astype(BF16)
    before = jnp.dot(tri, sel_bf, preferred_element_type=F32) + carry_ref[...]
    r1 = jnp.sum(jnp.where(lane == i1, before, 0.0), axis=-1, keepdims=True)
    r2 = jnp.sum(jnp.where(lane == i2, before, 0.0), axis=-1, keepdims=True)
    carry = carry_ref[...] + jnp.where(live, jnp.sum(jnp.where(sel, 1.0, 0.0), axis=0, keepdims=True), 0.0)
    carry_ref[...] = carry
    cnt_ref[...] = carry

    vals = (i1 - ROUTER_OFF, i2 - ROUTER_OFF, w1, w2, r1, r2)
    route = jnp.zeros((tm, LANES), F32)
    for j, v in enumerate(vals):
        route = jnp.where(lane == j, v, route)
    route_ref[rows, :] = route
    route_t_ref[:, rows] = jnp.transpose(route)[:ROUTE_ROWS, :]


def _mix(o_attn, o_ret, proj, x2, pa, pr, wo, g_ffn, w_router, b_router, interpret):
    n = x2.shape[0]
    tm = MIX_TM
    last = n // tm - 1
    const = lambda i: (0, 0)
    cur = lambda i: jnp.minimum(i, last)
    prev = lambda i: jnp.maximum(i - 1, 0)
    return pl.pallas_call(
        _mix_kernel,
        grid=(n // tm + 1,),
        in_specs=[
            pl.BlockSpec((tm, GROUP_W), lambda i: (cur(i), 0)),
            pl.BlockSpec((tm, D_MODEL), lambda i: (cur(i), 0)),
            pl.BlockSpec((tm, D_MODEL), lambda i: (cur(i), COL_GATE_A // D_MODEL)),
            pl.BlockSpec((tm, D_MODEL), lambda i: (cur(i), COL_GATE_R // D_MODEL)),
            pl.BlockSpec((tm, D_MODEL), lambda i: (cur(i), 0)),
            pl.BlockSpec((GROUP_W, D_MODEL), const),
            pl.BlockSpec((D_MODEL, D_MODEL), const),
            pl.BlockSpec((D_MODEL, D_MODEL), const),
            pl.BlockSpec((1, D_MODEL), const),
            pl.BlockSpec((D_MODEL, 2 * LANES), const),
            pl.BlockSpec((1, LANES), const),
        ],
        out_specs=[
            pl.BlockSpec((tm, D_MODEL), lambda i: (cur(i), 0)),
            pl.BlockSpec((2, tm, SC_ROW_WORDS), lambda i: (0, cur(i), 0)),
            pl.BlockSpec((tm, LANES), lambda i: (prev(i), 0)),
            pl.BlockSpec((ROUTE_ROWS, tm), lambda i: (0, prev(i))),
            pl.BlockSpec((1, LANES), const),
        ],
        out_shape=[
            jax.ShapeDtypeStruct((n, D_MODEL), F32),
            jax.ShapeDtypeStruct((2, n, SC_ROW_WORDS), jnp.int32),
            jax.ShapeDtypeStruct((n, LANES), F32),
            jax.ShapeDtypeStruct((ROUTE_ROWS, n), F32),
            jax.ShapeDtypeStruct((1, LANES), F32),
        ],
        scratch_shapes=[pltpu.VMEM((1, LANES), F32), pltpu.VMEM((tm, LANES), F32)],
        compiler_params=_cparams(("arbitrary",)),
        interpret=interpret,
        name="mix_router",
    )(o_attn, o_ret, proj, proj, x2, pa, pr, wo, g_ffn, w_router, b_router)


def _expert_kernel(te_ref, tr_ref, xs_ref, wg_ref, wu_ref, wd_ref, o_ref):
    i = pl.program_id(0)

    @pl.when(tr_ref[i] == i)
    def _():
        xs = _unpack_rows(xs_ref[0], xs_ref[1])
        a = jnp.dot(xs, wg_ref[0], preferred_element_type=F32)
        u = jnp.dot(xs, wu_ref[0], preferred_element_type=F32)
        hid = (a * _sigmoid(a) * u).astype(BF16)
        y = jnp.dot(hid, wd_ref[0], preferred_element_type=F32)
        o_ref[0], o_ref[1] = _pack_rows(y)


def _experts(xs, tile_expert, tile_row, w_gate, w_up, w_down, interpret):
    p = xs.shape[1]
    n_tiles = p // EXP_TM
    grid_spec = pltpu.PrefetchScalarGridSpec(
        num_scalar_prefetch=2,
        grid=(n_tiles,),
        in_specs=[
            pl.BlockSpec((2, EXP_TM, SC_ROW_WORDS), lambda i, te, tr: (0, tr[i], 0)),
            pl.BlockSpec((1, D_MODEL, EXPERT_FF), lambda i, te, tr: (te[i], 0, 0)),
            pl.BlockSpec((1, D_MODEL, EXPERT_FF), lambda i, te, tr: (te[i], 0, 0)),
            pl.BlockSpec((1, EXPERT_FF, D_MODEL), lambda i, te, tr: (te[i], 0, 0)),
        ],
        out_specs=pl.BlockSpec((2, EXP_TM, SC_ROW_WORDS), lambda i, te, tr: (0, tr[i], 0)),
    )
    return pl.pallas_call(
        _expert_kernel,
        grid_spec=grid_spec,
        out_shape=jax.ShapeDtypeStruct((2, p, SC_ROW_WORDS), jnp.int32),
        compiler_params=_cparams(("arbitrary",)),
        interpret=interpret,
        name="experts",
    )(tile_expert, tile_row, xs, w_gate, w_up, w_down)


def _final_kernel(x1_ref, yab_ref, route_ref, g_ref, *rest):
    o_ref = rest[-1]
    route = route_ref[...]
    w1 = route[:, 2:3]
    w2 = route[:, 3:4]
    ya = _unpack_rows(yab_ref[0], yab_ref[2]).astype(F32)
    yb = _unpack_rows(yab_ref[1], yab_ref[3]).astype(F32)
    x2 = x1_ref[...] + w1 * ya + w2 * yb
    ms = jnp.mean(x2 * x2, axis=-1, keepdims=True)
    o_ref[...] = x2 * lax.rsqrt(ms + EPS) * g_ref[...]


def _final(x1, yab, route, g_final, prev_out, part, interpret):
    n = x1.shape[0]
    tm = FIN_TM
    steps = n // (tm * FIN_PARTS)
    row = lambda i: (part * steps + i, 0)
    return pl.pallas_call(
        _final_kernel,
        grid=(steps,),
        in_specs=[
            pl.BlockSpec((tm, D_MODEL), row),
            pl.BlockSpec((4, tm, SC_ROW_WORDS), lambda i: (0, i, 0)),
            pl.BlockSpec((tm, LANES), row),
            pl.BlockSpec((1, D_MODEL), lambda i: (0, 0)),
        ] + ([] if prev_out is None else [pl.BlockSpec(memory_space=pl.ANY)]),
        out_specs=pl.BlockSpec((tm, D_MODEL), row),
        out_shape=jax.ShapeDtypeStruct((n, D_MODEL), F32),
        input_output_aliases={} if prev_out is None else {4: 0},
        compiler_params=_cparams(("arbitrary",)),
        interpret=interpret,
        name="combine_final",
    )(x1, yab, route, g_final, *([] if prev_out is None else [prev_out]))


def _permute_w_in(w_in):
    splits = np.cumsum([QKV_W, QKV_W, QKV_W, 512, 512, 1024, 1024, D_MODEL, D_MODEL])[:-1].tolist()
    qa, ka, va, qr, kr, vr, gr, gate_a, gate_r = jnp.split(w_in, splits, axis=-1)
    return jnp.concatenate([p.astype(BF16) for p in (gate_a, gate_r, vr, gr, qa, ka, va, qr, kr)], axis=-1)


def _dest_kernel(offs_ref, route_t_ref, idx_ref, *, n_rows):
    route_t = route_t_ref[...]
    experts = route_t[0:2, :]
    dest = route_t[4:6, :].astype(jnp.int32)
    for e in range(N_EXPERTS):
        dest = dest + jnp.where(experts == float(e), offs_ref[e], 0)
    idx_ref[0:2, :] = dest
    idx_ref[2:4, :] = dest + n_rows


def _route_plan(route_t, counts, n, interpret):
    cnt = counts[0, ROUTER_OFF:ROUTER_OFF + N_EXPERTS].astype(jnp.int32)
    padded = ((cnt + EXP_TM - 1) // EXP_TM) * EXP_TM
    ends = jnp.cumsum(padded)
    offs = ends - padded
    n_rows = 2 * n + N_EXPERTS * EXP_TM
    idx4 = pl.pallas_call(
        functools.partial(_dest_kernel, n_rows=n_rows),
        grid_spec=pltpu.PrefetchScalarGridSpec(
            num_scalar_prefetch=1, grid=(1,),
            in_specs=[pl.BlockSpec(route_t.shape, lambda i, offs: (0, 0))],
            out_specs=pl.BlockSpec((4, n), lambda i, offs: (0, 0))),
        out_shape=jax.ShapeDtypeStruct((4, n), jnp.int32),
        interpret=interpret,
        name="route_dest",
    )(offs, route_t)
    tile_row = jnp.minimum(jnp.arange(n_rows // EXP_TM, dtype=jnp.int32), ends[-1] // EXP_TM - 1)
    tile_expert = jnp.sum((ends[None, :] <= (tile_row * EXP_TM)[:, None]).astype(jnp.int32), axis=1)
    return idx4, tile_expert, tile_row, n_rows


def _sc_mesh():
    return plsc.VectorSubcoreMesh(core_axis_name="core", subcore_axis_name="subcore")


def _sc_scatter_rows(rows, idx4, n_out):
    n_in, w = rows.shape
    nb = idx4.shape[1] // SC_WINDOW

    @functools.partial(pl.kernel, out_type=jax.ShapeDtypeStruct((n_out, w), rows.dtype), mesh=_sc_mesh(),
                       scratch_types=[], name="sc_scatter_rows")
    def scatter(x_hbm, ia_hbm, ib_hbm, o_hbm):
        def body(x_vmem, ia_vmem, ib_vmem):
            pltpu.sync_copy(x_vmem, o_hbm.at[ia_vmem.at[0]])
            pltpu.sync_copy(x_vmem, o_hbm.at[ib_vmem.at[0]])

        pltpu.emit_pipeline(
            body,
            grid=(n_in // SC_WINDOW,),
            in_specs=[pl.BlockSpec((SC_WINDOW, w), lambda i: (i, 0)),
                      pl.BlockSpec((1, SC_WINDOW), lambda i: (2 * (i // nb), i % nb)),
                      pl.BlockSpec((1, SC_WINDOW), lambda i: (2 * (i // nb) + 1, i % nb))],
            out_specs=[],
            core_axis_name=("core", "subcore"),
            dimension_semantics=(pltpu.PARALLEL,),
        )(x_hbm, ia_hbm, ib_hbm)

    return scatter(rows, idx4, idx4)


def _sc_gather_rows(table, idx4, part):
    nb = idx4.shape[1] // (SC_WINDOW * FIN_PARTS)
    n_idx = idx4.shape[0] * nb * SC_WINDOW
    w = table.shape[1]

    @functools.partial(pl.kernel, out_type=jax.ShapeDtypeStruct((n_idx, w), table.dtype), mesh=_sc_mesh(),
                       scratch_types=[], name="sc_gather_rows")
    def gather(t_hbm, i_hbm, o_hbm):
        def body(i_vmem, o_vmem):
            pltpu.sync_copy(t_hbm.at[i_vmem.at[0]], o_vmem)

        pltpu.emit_pipeline(
            body,
            grid=(n_idx // SC_WINDOW,),
            in_specs=[pl.BlockSpec((1, SC_WINDOW), lambda i: (i // nb, part * nb + i % nb))],
            out_specs=[pl.BlockSpec((SC_WINDOW, w), lambda i: (i, 0))],
            core_axis_name=("core", "subcore"),
            dimension_semantics=(pltpu.PARALLEL,),
        )(i_hbm, o_hbm)

    return gather(table, idx4)


def _forward(x, g_mix, w_in, w_attn_branch, w_ret_branch, w_out, g_ffn, w_group_router, b_group_router,
             w_expert_router, b_expert_router, w_gate, w_up, w_down, g_final, interpret=False):
    batch, seq, d = x.shape
    n = batch * seq
    x2 = x.reshape(n, d)
    proj, wg_bf, wu_bf, wd_bf = _proj(x2, g_mix[0][None, :], _permute_w_in(w_in[0]), w_gate[0], w_up[0],
                                      w_down[0], interpret)
    o_attn = _attention(proj, batch, seq, interpret)
    o_ret = _retention(proj, batch, seq, interpret)
    pad = LANES - N_EXPERT_GROUPS - N_EXPERTS
    w_router = jnp.concatenate([w_group_router[0], w_expert_router[0], jnp.zeros((d, pad), F32)], axis=-1)
    w_router_hi = w_router.astype(BF16)
    w_router_lo = (w_router - w_router_hi.astype(F32)).astype(BF16)
    w_router2 = jnp.concatenate([w_router_hi, w_router_lo], axis=-1)
    b_router = jnp.concatenate([b_group_router[0], b_expert_router[0], jnp.zeros((pad,), F32)])[None, :]
    x1, h2p, route, route_t, counts = _mix(o_attn, o_ret, proj, x2, w_attn_branch[0].astype(BF16),
                                           w_ret_branch[0].astype(BF16), w_out[0].astype(BF16),
                                           g_ffn[0][None, :], w_router2, b_router, interpret)
    idx4, tile_expert, tile_row, n_rows = _route_plan(route_t, counts, n, interpret)
    xs = _sc_scatter_rows(h2p.reshape(2 * n, SC_ROW_WORDS), idx4, 2 * n_rows)
    ys = _experts(xs.reshape(2, n_rows, SC_ROW_WORDS), tile_expert, tile_row, wg_bf, wu_bf, wd_bf, interpret)
    ys_rows = ys.reshape(2 * n_rows, SC_ROW_WORDS)
    out = None
    for part in range(FIN_PARTS):
        yab = _sc_gather_rows(ys_rows, idx4, part)
        out = _final(x1, yab.reshape(4, n // FIN_PARTS, SC_ROW_WORDS), route, g_final[None, :], out, part, interpret)
    return out.reshape(batch, seq, d)


def kernel(x, g_mix, w_in, w_attn_branch, w_ret_branch, w_out, g_ffn, w_group_router, b_group_router,
           w_expert_router, b_expert_router, w_gate, w_up, w_down, g_final):
    return _forward(x, g_mix, w_in, w_attn_branch, w_ret_branch, w_out, g_ffn, w_group_router,
                    b_group_router, w_expert_router, b_expert_router, w_gate, w_up, w_down, g_final)
```

```python
import functools

import numpy as np
import jax
import jax.numpy as jnp
from jax import lax
from jax.experimental import pallas as pl
from jax.experimental.pallas import tpu as pltpu
from jax.experimental.pallas import tpu_sc as plsc

F32 = jnp.float32
BF16 = jnp.bfloat16

D_MODEL = 1024
ATTN_GROUPS = ((128, 1), (512, 4), (2048, 16))
N_GROUPS = len(ATTN_GROUPS)
ATTN_HEADS = 8
HEAD_DIM = 64
GROUP_W = ATTN_HEADS * HEAD_DIM
QKV_W = N_GROUPS * GROUP_W
RET_HEADS = 4
RET_DK = 128
RET_DV = 256
RET_CHUNK = 128
RET_TS = 1024
N_EXPERT_GROUPS = 4
EXPERTS_PER_GROUP = 8
N_EXPERTS = N_EXPERT_GROUPS * EXPERTS_PER_GROUP
EXPERT_FF = 512
EPS = 1e-6

LANES = 128
BLK = 128
SPAN = 2048
NEG = -1e30
ACC_PARTS = 3
ATTN_STAGES = 9

COL_QA = 0
COL_KA = COL_QA + QKV_W
COL_VA = COL_KA + QKV_W
COL_QR = COL_VA + QKV_W
COL_KR = COL_QR + RET_HEADS * RET_DK
COL_VR = COL_KR + RET_HEADS * RET_DK
COL_GR = COL_VR + RET_HEADS * RET_DV
COL_GATE_A = COL_GR + RET_HEADS * RET_DV
COL_GATE_R = COL_GATE_A + D_MODEL
IN_WIDTH = COL_GATE_R + D_MODEL
COL_BLOCK = 512

PROJ_TM = 512
PROJ_TN = IN_WIDTH // 2
MXU_N = 256
MIX_TM = 1024
MIX_CHUNK = 512
ROUTE_CHUNK = 256
ROUTE_ROWS = 8
EXP_TM = 512
SC_WINDOW = 128
SC_ROW_WORDS = 256
FIN_TM = 1024
VMEM_LIMIT = 56 * 1024 * 1024


def _cparams(sem):
    return pltpu.CompilerParams(dimension_semantics=sem, vmem_limit_bytes=VMEM_LIMIT)


def _sigmoid(x):
    return 0.5 * jnp.tanh(0.5 * x) + 0.5


def _proj_kernel(x_ref, g_ref, w_ref, wg_ref, wu_ref, wd_ref, o_ref, wg_o, wu_o, wd_o):
    x = x_ref[...]
    ms = jnp.mean(x * x, axis=-1, keepdims=True)
    h = (x * lax.rsqrt(ms + EPS) * g_ref[...]).astype(BF16)
    for c in range(PROJ_TN // MXU_N):
        sl = slice(c * MXU_N, (c + 1) * MXU_N)
        o_ref[:, sl] = jnp.dot(h, w_ref[:, sl], preferred_element_type=F32).astype(o_ref.dtype)
    wg_o[...] = wg_ref[...].astype(BF16)
    wu_o[...] = wu_ref[...].astype(BF16)
    wd_o[...] = wd_ref[...].astype(BF16)


def _proj(x2, g_mix, w_in_bf16, w_gate, w_up, w_down, interpret):
    n = x2.shape[0]
    n_i = n // PROJ_TM
    steps = (IN_WIDTH // PROJ_TN) * n_i
    flat = [w.reshape(-1, w.shape[-1]) for w in (w_gate, w_up, w_down)]
    w_specs = [pl.BlockSpec((w.shape[0] // steps, w.shape[1]), lambda j, i: (j * n_i + i, 0)) for w in flat]
    outs = pl.pallas_call(
        _proj_kernel,
        grid=(IN_WIDTH // PROJ_TN, n_i),
        in_specs=[
            pl.BlockSpec((PROJ_TM, D_MODEL), lambda j, i: (i, 0)),
            pl.BlockSpec((1, D_MODEL), lambda j, i: (0, 0)),
            pl.BlockSpec((D_MODEL, PROJ_TN), lambda j, i: (0, j)),
        ] + w_specs,
        out_specs=[pl.BlockSpec((PROJ_TM, PROJ_TN), lambda j, i: (i, j))] + w_specs,
        out_shape=[jax.ShapeDtypeStruct((n, IN_WIDTH), BF16)]
        + [jax.ShapeDtypeStruct(w.shape, BF16) for w in flat],
        compiler_params=_cparams(("arbitrary", "arbitrary")),
        interpret=interpret,
        name="proj",
    )(x2, g_mix, w_in_bf16, *flat)
    return outs[0], outs[1].reshape(w_gate.shape), outs[2].reshape(w_up.shape), outs[3].reshape(w_down.shape)


def _attn_unit(q2, kk, vv, bias_a, bias_b):
    lane = lax.broadcasted_iota(jnp.int32, (BLK, LANES), 1)
    left = lane < HEAD_DIM
    zero = jnp.zeros_like(q2)
    nt = (((1,), (1,)), ((), ()))
    q_st = jnp.concatenate([jnp.where(left, q2, zero), jnp.where(left, zero, q2)], axis=0)
    s = lax.dot_general(q_st, kk, nt, preferred_element_type=F32) + jnp.concatenate([bias_a, bias_b], axis=0)
    m = jnp.max(s, axis=-1, keepdims=True)
    p = jnp.exp(s - m)
    den = jnp.sum(p, axis=-1, keepdims=True)
    o = jnp.dot(p.astype(BF16), vv, preferred_element_type=F32)
    return (jnp.where(left, o[:BLK], o[BLK:]), jnp.where(left, m[:BLK], m[BLK:]),
            jnp.where(left, den[:BLK], den[BLK:]))


def _attn_kernel(q1_ref, q2_ref, q3_ref, k1_ref, k2_ref, k3_ref, v1_ref, v2_ref, v3_ref,
                 bias_ref, o_ref, tmp_ref, qd_ref, kvd_ref, acc_ref, accw_ref, *, seq):
    s_id = pl.program_id(2)
    step = ATTN_GROUPS[1][1]

    quarter = SPAN // step
    span_row0 = pl.multiple_of(s_id * SPAN, SPAN)
    first_stage = {1: 0, 2: 3}

    def operands(gi):
        d = ATTN_GROUPS[gi][1]
        q_ref, k_ref, v_ref = ((q2_ref, k2_ref, v2_ref), (q3_ref, k3_ref, v3_ref))[gi - 1]
        n_stage = 1 if d == step else 2
        kv_off = s_id * (SPAN // d)
        return [(q_ref, 0, first_stage[gi], qd_ref, gi - 1, SPAN // d, 0, 0.125),
                (k_ref, span_row0, first_stage[gi] + n_stage, kvd_ref, 2 * (gi - 1), seq // d, kv_off, 1.0),
                (v_ref, span_row0, first_stage[gi] + 2 * n_stage, kvd_ref, 2 * (gi - 1) + 1, seq // d, kv_off, 1.0)]

    def staging_pieces(gi):
        pieces = []
        for src_ref, row0, stage, _, _, _, _, scale in operands(gi):
            for c0 in range(0, SPAN, 256):
                def to_f32(src_ref=src_ref, row0=row0, stage=stage, scale=scale, c0=c0):
                    x = src_ref[pl.ds(pl.multiple_of(row0 + c0, 256), 256), :].astype(F32)
                    tmp_ref[stage, c0:c0 + 256, :] = x if scale == 1.0 else x * scale
                pieces.append(to_f32)
        if ATTN_GROUPS[gi][1] != step:
            for _, _, stage, _, _, _, _, _ in operands(gi):
                for b in range(step):
                    for j0 in range(0, quarter, 256):
                        def first_pass(stage=stage, b=b, j0=j0):
                            tmp_ref[stage + 1, b * quarter + j0:b * quarter + j0 + 256, :] = (
                                tmp_ref[stage, pl.ds(b + step * j0, 256, stride=step), :])
                        pieces.append(first_pass)
        return pieces

    def residue_pieces(gi, r):
        d = ATTN_GROUPS[gi][1]
        chunk = min(256, SPAN // d)
        pieces = []
        for _, _, stage, dst_ref, slot, pitch, off, _ in operands(gi):
            for j0 in range(0, SPAN // d, chunk):
                def last_pass(stage=stage, dst_ref=dst_ref, slot=slot, pitch=pitch, off=off, j0=j0):
                    if d == step:
                        rows = tmp_ref[stage, pl.ds(r + step * j0, chunk, stride=step), :]
                    else:
                        a, b = divmod(r, step)
                        rows = tmp_ref[stage + 1, pl.ds(b * quarter + a + step * j0, chunk, stride=step), :]
                    row0 = r * pitch + j0 + off
                    if not isinstance(row0, int):
                        row0 = pl.multiple_of(row0, chunk)
                    dst_ref[slot, pl.ds(row0, chunk), :] = rows.astype(BF16)
                pieces.append(last_pass)
        return pieces

    def dilated_unit(gi, d, m, r, first):
        slot = gi - 1
        loc = BLK * m * d + r
        q_row = r * (SPAN // d) + BLK * m
        cur = pl.multiple_of(r * (seq // d) + s_id * (SPAN // d) + BLK * m, BLK)
        prev = pl.multiple_of(jnp.where(first == 1, cur, cur - BLK), BLK)
        q2 = qd_ref[slot, pl.ds(q_row, BLK), :]
        kk = jnp.concatenate([kvd_ref[2 * slot, pl.ds(prev, BLK), :], kvd_ref[2 * slot, pl.ds(cur, BLK), :]], axis=0)
        vv = jnp.concatenate([kvd_ref[2 * slot + 1, pl.ds(prev, BLK), :],
                              kvd_ref[2 * slot + 1, pl.ds(cur, BLK), :]], axis=0)
        parts = _attn_unit(q2, kk, vv, bias_ref[gi, first, 0], bias_ref[gi, first, 1])
        for j, part in enumerate(parts):
            if d == step:
                acc_ref[ACC_PARTS * slot + j, pl.ds(loc, BLK, stride=d), :] = part
            else:
                a, b = divmod(r, step)
                accw_ref[j, pl.ds(b * quarter + a, BLK, stride=step), :] = part

    def reinterleave_pieces():
        pieces = []
        for j in range(ACC_PARTS):
            for b in range(step):
                for j0 in range(0, quarter, 256):
                    def piece(j=j, b=b, j0=j0):
                        acc_ref[ACC_PARTS + j, pl.ds(b + step * j0, 256, stride=step), :] = (
                            accw_ref[j, b * quarter + j0:b * quarter + j0 + 256, :])
                    pieces.append(piece)
        return pieces

    def dense_unit(m):
        loc = m * BLK
        cur = pl.multiple_of(s_id * SPAN + loc, BLK)
        prev = pl.multiple_of(jnp.maximum(cur - BLK, 0), BLK)
        first = jnp.where(cur == 0, 1, 0)
        q2 = q1_ref[loc:loc + BLK, :] * 0.125
        kk = jnp.concatenate([k1_ref[pl.ds(prev, BLK), :], k1_ref[pl.ds(cur, BLK), :]], axis=0)
        vv = jnp.concatenate([v1_ref[pl.ds(prev, BLK), :], v1_ref[pl.ds(cur, BLK), :]], axis=0)
        n1, m1, d1 = _attn_unit(q2, kk, vv, bias_ref[0, first, 0], bias_ref[0, first, 1])
        (n2, m2, d2), (n3, m3, d3) = (
            tuple(acc_ref[g * ACC_PARTS + j, loc:loc + BLK, :] for j in range(ACC_PARTS)) for g in range(2))
        mx = jnp.maximum(jnp.maximum(m1, m2), m3)
        w1, w2, w3 = jnp.exp(m1 - mx), jnp.exp(m2 - mx), jnp.exp(m3 - mx)
        num = w1 * n1 + w2 * n2 + w3 * n3
        den = w1 * d1 + w2 * d2 + w3 * d3
        o_ref[loc:loc + BLK, :] = (num / den).astype(o_ref.dtype)

    first_span = jnp.where(s_id == 0, 1, 0)
    for gi in (1, 2):
        for piece in staging_pieces(gi) + [p for r in range(ATTN_GROUPS[gi][1]) for p in residue_pieces(gi, r)]:
            piece()
    for gi in (2, 1):
        d = ATTN_GROUPS[gi][1]
        for m in range(SPAN // (BLK * d)):
            for r in range(d):
                dilated_unit(gi, d, m, r, first_span if m == 0 else 0)
        if d != step:
            for piece in reinterleave_pieces():
                piece()
    for m in range(SPAN // BLK):
        dense_unit(m)


def _attn_bias():
    slopes = np.exp2(-8.0 * np.arange(1, ATTN_HEADS + 1, dtype=np.float64) / ATTN_HEADS)
    qi = np.arange(BLK)[:, None]
    kj = np.arange(2 * BLK)[None, :]
    rel = qi + BLK - kj
    out = np.zeros((N_GROUPS, 2, ATTN_HEADS, BLK, 2 * BLK), np.float32)
    for gi, (window, d) in enumerate(ATTN_GROUPS):
        n_back = window // d
        assert n_back == BLK
        valid = (rel >= 0) & (rel <= n_back)
        bias = -slopes[:, None, None] * (rel * d)[None].astype(np.float64)
        out[gi, 0] = np.where(valid[None], bias, NEG)
        out[gi, 1] = np.where((valid & (kj >= BLK))[None], bias, NEG)
    return jnp.asarray(out)


def _attention(proj, batch, seq, interpret):
    n = batch * seq
    spans = seq // SPAN
    n_hp = GROUP_W // LANES
    qcol = lambda g: (COL_QA + g * GROUP_W) // LANES
    kcol = lambda g: (COL_KA + g * GROUP_W) // LANES
    vcol = lambda g: (COL_VA + g * GROUP_W) // LANES
    q_specs = [pl.BlockSpec((SPAN, LANES), functools.partial(lambda b, hp, s, c: (b * spans + s, c + hp), c=qcol(g)))
               for g in range(N_GROUPS)]
    k_specs = [pl.BlockSpec((seq, LANES), functools.partial(lambda b, hp, s, c: (b, c + hp), c=kcol(g)))
               for g in range(N_GROUPS)]
    v_specs = [pl.BlockSpec((seq, LANES), functools.partial(lambda b, hp, s, c: (b, c + hp), c=vcol(g)))
               for g in range(N_GROUPS)]
    bias_spec = pl.BlockSpec((N_GROUPS, 2, 2, BLK, 2 * BLK), lambda b, hp, s: (0, 0, hp, 0, 0))
    return pl.pallas_call(
        functools.partial(_attn_kernel, seq=seq),
        grid=(batch, n_hp, spans),
        in_specs=q_specs + k_specs + v_specs + [bias_spec],
        out_specs=pl.BlockSpec((SPAN, LANES), lambda b, hp, s: (b * spans + s, hp)),
        out_shape=jax.ShapeDtypeStruct((n, GROUP_W), BF16),
        scratch_shapes=[
            pltpu.VMEM((ATTN_STAGES, SPAN, LANES), F32),
            pltpu.VMEM((2, SPAN, LANES), BF16),
            pltpu.VMEM((4, seq, LANES), BF16),
            pltpu.VMEM((2 * ACC_PARTS, SPAN, LANES), F32),
            pltpu.VMEM((ACC_PARTS, SPAN, LANES), F32),
        ],
        compiler_params=_cparams(("arbitrary", "arbitrary", "arbitrary")),
        interpret=interpret,
        name="attn",
    )(*([proj] * 9), _attn_bias())


def _ret_kernel(q_ref, k_ref, v0_ref, v1_ref, g0_ref, g1_ref, dec_ref, xi_ref, zeta_ref, gch_ref, o_ref, st_ref):
    @pl.when(pl.program_id(1) == 0)
    def _():
        st_ref[...] = jnp.zeros_like(st_ref)

    nt = (((1,), (1,)), ((), ()))
    scale = RET_DK ** -0.5
    heads_per_block = COL_BLOCK // RET_DV

    for c in range(RET_TS // RET_CHUNK):
        rows = pl.ds(c * RET_CHUNK, RET_CHUNK)
        for h in range(RET_HEADS):
            kcols = slice(h * RET_DK, (h + 1) * RET_DK)
            vcols = slice(h * RET_DV, (h + 1) * RET_DV)
            v_ref, gr_ref = ((v0_ref, g0_ref), (v1_ref, g1_ref))[h // heads_per_block]
            hcols = slice((h % heads_per_block) * RET_DV, (h % heads_per_block + 1) * RET_DV)
            qi = q_ref[rows, kcols]
            kf = k_ref[rows, kcols].astype(F32) * scale
            ki = kf.astype(BF16)
            kz_t = jnp.transpose(kf * zeta_ref[h]).astype(BF16)
            vi = v_ref[rows, hcols]
            att = lax.dot_general(qi, ki, nt, preferred_element_type=F32) * dec_ref[h]
            inner = jnp.dot(att.astype(BF16), vi, preferred_element_type=F32)
            st = st_ref[h]
            cross = jnp.dot(qi, st.astype(BF16), preferred_element_type=F32) * xi_ref[h]
            st_ref[h] = gch_ref[h] * st + jnp.dot(kz_t, vi, preferred_element_type=F32)
            y = inner + cross
            mu = jnp.mean(y, axis=-1, keepdims=True)
            yc = y - mu
            var = jnp.mean(yc * yc, axis=-1, keepdims=True)
            yn = yc * lax.rsqrt(var + EPS)
            g = gr_ref[rows, hcols].astype(F32)
            o_ref[rows, vcols] = (g * _sigmoid(g) * yn).astype(o_ref.dtype)


def _ret_tables():
    c = RET_CHUNK
    log_g = np.log1p(-np.exp2(-5.0 - np.arange(RET_HEADS, dtype=np.float64)))
    pos = np.arange(c, dtype=np.float64)
    diff = pos[:, None] - pos[None, :]
    dec = np.where(diff >= 0, np.exp(log_g[:, None, None] * np.maximum(diff, 0.0)), 0.0)
    xi = np.exp(log_g[:, None] * (pos + 1.0))[..., None] * np.ones((1, 1, RET_DV))
    zeta = np.exp(log_g[:, None] * (c - 1.0 - pos))[..., None] * np.ones((1, 1, RET_DK))
    gch = np.exp(log_g * c)[:, None, None] * np.ones((1, 1, RET_DV))
    return tuple(jnp.asarray(t, F32) for t in (dec, xi, zeta, gch))


def _retention(proj, batch, seq, interpret):
    n = batch * seq
    dec, xi, zeta, gch = _ret_tables()
    qk_w = RET_HEADS * RET_DK
    v_w = RET_HEADS * RET_DV
    nts = seq // RET_TS
    const3 = lambda b, t: (0, 0, 0)
    return pl.pallas_call(
        _ret_kernel,
        grid=(batch, nts),
        in_specs=[
            pl.BlockSpec((RET_TS, qk_w), lambda b, t: (b * nts + t, COL_QR // qk_w)),
            pl.BlockSpec((RET_TS, qk_w), lambda b, t: (b * nts + t, COL_KR // qk_w)),
            pl.BlockSpec((RET_TS, COL_BLOCK), lambda b, t: (b * nts + t, COL_VR // COL_BLOCK)),
            pl.BlockSpec((RET_TS, COL_BLOCK), lambda b, t: (b * nts + t, COL_VR // COL_BLOCK + 1)),
            pl.BlockSpec((RET_TS, COL_BLOCK), lambda b, t: (b * nts + t, COL_GR // COL_BLOCK)),
            pl.BlockSpec((RET_TS, COL_BLOCK), lambda b, t: (b * nts + t, COL_GR // COL_BLOCK + 1)),
            pl.BlockSpec((RET_HEADS, RET_CHUNK, RET_CHUNK), const3),
            pl.BlockSpec((RET_HEADS, RET_CHUNK, RET_DV), const3),
            pl.BlockSpec((RET_HEADS, RET_CHUNK, RET_DK), const3),
            pl.BlockSpec((RET_HEADS, 1, RET_DV), const3),
        ],
        out_specs=pl.BlockSpec((RET_TS, v_w), lambda b, t: (b * nts + t, 0)),
        out_shape=jax.ShapeDtypeStruct((n, v_w), BF16),
        scratch_shapes=[pltpu.VMEM((RET_HEADS, RET_DK, RET_DV), F32)],
        compiler_params=_cparams(("arbitrary", "arbitrary")),
        interpret=interpret,
        name="retention",
    )(proj, proj, proj, proj, proj, proj, dec, xi, zeta, gch)


ROUTER_OFF = N_EXPERT_GROUPS


def _pack_bf16_pair(a, b):
    hi = lax.bitcast_convert_type(a.astype(BF16).astype(F32), jnp.uint32)
    lo = lax.bitcast_convert_type(b.astype(BF16).astype(F32), jnp.uint32)
    return lax.bitcast_convert_type(hi | (lo >> 16), jnp.int32)


def _unpack_bf16_pair(w):
    u = lax.bitcast_convert_type(w, jnp.uint32)
    a = lax.bitcast_convert_type(u & jnp.uint32(0xFFFF0000), F32).astype(BF16)
    b = lax.bitcast_convert_type(u << 16, F32).astype(BF16)
    return a, b


def _pack_rows(y):
    q = D_MODEL // 4
    return (_pack_bf16_pair(y[:, 0:q], y[:, 2 * q:3 * q]), _pack_bf16_pair(y[:, q:2 * q], y[:, 3 * q:4 * q]))


def _unpack_rows(slab0, slab1):
    q0, q2 = _unpack_bf16_pair(slab0)
    q1, q3 = _unpack_bf16_pair(slab1)
    return jnp.concatenate([q0, q1, q2, q3], axis=1)


def _mix_kernel(oa_ref, or_ref, ga0_ref, ga1_ref, gr0_ref, gr1_ref, x_ref, pa_ref, pr_ref, wo_ref, gf_ref, wr_ref, br_ref,
                x1_ref, h2_ref, route_ref, route_t_ref, cnt_ref, carry_ref, logit_ref):
    step = pl.program_id(0)

    @pl.when(step == 0)
    def _():
        carry_ref[...] = jnp.zeros_like(carry_ref)
        logit_ref[...] = jnp.zeros_like(logit_ref)

    routing = iter([functools.partial(_route_rows, pl.ds(c * ROUTE_CHUNK, ROUTE_CHUNK), step > 0, logit_ref,
                                      route_ref, route_t_ref, cnt_ref, carry_ref)
                    for c in range(MIX_TM // ROUTE_CHUNK)])
    for c in range(MIX_TM // MIX_CHUNK):
        for _ in _mix_rows(pl.ds(c * MIX_CHUNK, MIX_CHUNK), oa_ref, or_ref, (ga0_ref, ga1_ref), (gr0_ref, gr1_ref),
                           x_ref, pa_ref, pr_ref, wo_ref, gf_ref, wr_ref, br_ref, x1_ref, h2_ref, logit_ref):
            next(routing, lambda: None)()
    for piece in routing:
        piece()


def _mix_rows(rows, oa_ref, or_ref, ga_refs, gr_refs, x_ref, pa_ref, pr_ref, wo_ref, gf_ref, wr_ref, br_ref,
              x1_ref, h2_ref, logit_ref):
    a = jnp.dot(oa_ref[rows, :], pa_ref[...], preferred_element_type=F32)
    yield
    r = jnp.dot(or_ref[rows, :], pr_ref[...], preferred_element_type=F32)
    yield
    merged = jnp.concatenate(
        [_sigmoid(ga[rows, :].astype(F32)) * a[:, k * COL_BLOCK:(k + 1) * COL_BLOCK]
         + _sigmoid(gr[rows, :].astype(F32)) * r[:, k * COL_BLOCK:(k + 1) * COL_BLOCK]
         for k, (ga, gr) in enumerate(zip(ga_refs, gr_refs))], axis=1)
    x1 = x_ref[rows, :] + jnp.dot(merged.astype(BF16), wo_ref[...], preferred_element_type=F32)
    x1_ref[rows, :] = x1
    ms = jnp.mean(x1 * x1, axis=-1, keepdims=True)
    h2 = x1 * lax.rsqrt(ms + EPS) * gf_ref[...]
    h2_ref[0, rows, :], h2_ref[1, rows, :] = _pack_rows(h2)

    h_hi = h2.astype(BF16)
    h_lo = (h2 - h_hi.astype(F32)).astype(BF16)
    both = jnp.dot(h_hi, wr_ref[...], preferred_element_type=F32)
    logit_ref[rows, :] = (both[:, :LANES] + both[:, LANES:]
                          + jnp.dot(h_lo, wr_ref[:, :LANES], preferred_element_type=F32) + br_ref[...])


def _route_rows(rows, live, logit_ref, route_ref, route_t_ref, cnt_ref, carry_ref):
    logits = logit_ref[rows, :]
    tm = logits.shape[0]
    lane = lax.broadcasted_iota(jnp.int32, (tm, LANES), 1).astype(F32)
    big = jnp.float32(4 * LANES)
    ninf = -jnp.inf
    is_g = lane < N_EXPERT_GROUPS
    gl = jnp.where(is_g, logits, ninf)
    gmax = jnp.max(gl, axis=-1, keepdims=True)
    gsum = jnp.sum(jnp.where(is_g, jnp.exp(gl - gmax), 0.0), axis=-1, keepdims=True)
    g_val = 1.0 / gsum
    g_idx = jnp.min(jnp.where(jnp.logical_and(is_g, gl == gmax), lane, big), axis=-1, keepdims=True)
    lo = ROUTER_OFF + EXPERTS_PER_GROUP * g_idx
    in_grp = jnp.logical_and(lane >= lo, lane < lo + EXPERTS_PER_GROUP)
    el = jnp.where(in_grp, logits, ninf)
    v1 = jnp.max(el, axis=-1, keepdims=True)
    i1 = jnp.min(jnp.where(jnp.logical_and(in_grp, el == v1), lane, big), axis=-1, keepdims=True)
    rest = jnp.logical_and(in_grp, lane != i1)
    el2 = jnp.where(rest, logits, ninf)
    v2 = jnp.max(el2, axis=-1, keepdims=True)
    i2 = jnp.min(jnp.where(jnp.logical_and(rest, el2 == v2), lane, big), axis=-1, keepdims=True)
    t = jnp.exp(v2 - v1)
    w1 = g_val / (1.0 + t)
    w2 = g_val * t / (1.0 + t)

    sel = jnp.logical_or(lane == i1, lane == i2)
    sel_bf = jnp.where(sel, 1.0, 0.0).astype(BF16)
    row = lax.broadcasted_iota(jnp.int32, (tm, tm), 0)
    col = lax.broadcasted_iota(jnp.int32, (tm, tm), 1)
    tri = jnp.where(col < row, 1.0, 0.0).astype(BF16)
    before = jnp.dot(tri, sel_bf, preferred_element_type=F32) + carry_ref[...]
    r1 = jnp.sum(jnp.where(lane == i1, before, 0.0), axis=-1, keepdims=True)
    r2 = jnp.sum(jnp.where(lane == i2, before, 0.0), axis=-1, keepdims=True)
    carry = carry_ref[...] + jnp.where(live, jnp.sum(jnp.where(sel, 1.0, 0.0), axis=0, keepdims=True), 0.0)
    carry_ref[...] = carry
    cnt_ref[...] = carry

    vals = (i1 - ROUTER_OFF, i2 - ROUTER_OFF, w1, w2, r1, r2)
    route = jnp.zeros((tm, LANES), F32)
    for j, v in enumerate(vals):
        route = jnp.where(lane == j, v, route)
    route_ref[rows, :] = route
    route_t_ref[:, rows] = jnp.transpose(route)[:ROUTE_ROWS, :]


def _mix(o_attn, o_ret, proj, x2, pa, pr, wo, g_ffn, w_router, b_router, interpret):
    n = x2.shape[0]
    tm = MIX_TM
    last = n // tm - 1
    const = lambda i: (0, 0)
    cur = lambda i: jnp.minimum(i, last)
    prev = lambda i: jnp.maximum(i - 1, 0)
    return pl.pallas_call(
        _mix_kernel,
        grid=(n // tm + 1,),
        in_specs=[
            pl.BlockSpec((tm, GROUP_W), lambda i: (cur(i), 0)),
            pl.BlockSpec((tm, D_MODEL), lambda i: (cur(i), 0)),
            pl.BlockSpec((tm, COL_BLOCK), lambda i: (cur(i), COL_GATE_A // COL_BLOCK)),
            pl.BlockSpec((tm, COL_BLOCK), lambda i: (cur(i), COL_GATE_A // COL_BLOCK + 1)),
            pl.BlockSpec((tm, COL_BLOCK), lambda i: (cur(i), COL_GATE_R // COL_BLOCK)),
            pl.BlockSpec((tm, COL_BLOCK), lambda i: (cur(i), COL_GATE_R // COL_BLOCK + 1)),
            pl.BlockSpec((tm, D_MODEL), lambda i: (cur(i), 0)),
            pl.BlockSpec((GROUP_W, D_MODEL), const),
            pl.BlockSpec((D_MODEL, D_MODEL), const),
            pl.BlockSpec((D_MODEL, D_MODEL), const),
            pl.BlockSpec((1, D_MODEL), const),
            pl.BlockSpec((D_MODEL, 2 * LANES), const),
            pl.BlockSpec((1, LANES), const),
        ],
        out_specs=[
            pl.BlockSpec((tm, D_MODEL), lambda i: (cur(i), 0)),
            pl.BlockSpec((2, tm, SC_ROW_WORDS), lambda i: (0, cur(i), 0)),
            pl.BlockSpec((tm, LANES), lambda i: (prev(i), 0)),
            pl.BlockSpec((ROUTE_ROWS, tm), lambda i: (0, prev(i))),
            pl.BlockSpec((1, LANES), const),
        ],
        out_shape=[
            jax.ShapeDtypeStruct((n, D_MODEL), F32),
            jax.ShapeDtypeStruct((2, n, SC_ROW_WORDS), jnp.int32),
            jax.ShapeDtypeStruct((n, LANES), F32),
            jax.ShapeDtypeStruct((ROUTE_ROWS, n), F32),
            jax.ShapeDtypeStruct((1, LANES), F32),
        ],
        scratch_shapes=[pltpu.VMEM((1, LANES), F32), pltpu.VMEM((tm, LANES), F32)],
        compiler_params=_cparams(("arbitrary",)),
        interpret=interpret,
        name="mix_router",
    )(o_attn, o_ret, proj, proj, proj, proj, x2, pa, pr, wo, g_ffn, w_router, b_router)


def _expert_kernel(te_ref, tr_ref, xs_ref, wg_ref, wu_ref, wd_ref, o_ref):
    i = pl.program_id(0)

    @pl.when(tr_ref[i] == i)
    def _():
        xs = _unpack_rows(xs_ref[0], xs_ref[1])
        a = jnp.dot(xs, wg_ref[0], preferred_element_type=F32)
        u = jnp.dot(xs, wu_ref[0], preferred_element_type=F32)
        hid = (a * _sigmoid(a) * u).astype(BF16)
        y = jnp.dot(hid, wd_ref[0], preferred_element_type=F32)
        o_ref[0], o_ref[1] = _pack_rows(y)


def _experts(xs, tile_expert, tile_row, w_gate, w_up, w_down, interpret):
    p = xs.shape[1]
    n_tiles = p // EXP_TM
    grid_spec = pltpu.PrefetchScalarGridSpec(
        num_scalar_prefetch=2,
        grid=(n_tiles,),
        in_specs=[
            pl.BlockSpec((2, EXP_TM, SC_ROW_WORDS), lambda i, te, tr: (0, tr[i], 0)),
            pl.BlockSpec((1, D_MODEL, EXPERT_FF), lambda i, te, tr: (te[i], 0, 0)),
            pl.BlockSpec((1, D_MODEL, EXPERT_FF), lambda i, te, tr: (te[i], 0, 0)),
            pl.BlockSpec((1, EXPERT_FF, D_MODEL), lambda i, te, tr: (te[i], 0, 0)),
        ],
        out_specs=pl.BlockSpec((2, EXP_TM, SC_ROW_WORDS), lambda i, te, tr: (0, tr[i], 0)),
    )
    return pl.pallas_call(
        _expert_kernel,
        grid_spec=grid_spec,
        out_shape=jax.ShapeDtypeStruct((2, p, SC_ROW_WORDS), jnp.int32),
        compiler_params=_cparams(("arbitrary",)),
        interpret=interpret,
        name="experts",
    )(tile_expert, tile_row, xs, w_gate, w_up, w_down)


def _final_kernel(x1_ref, yab_ref, route_ref, g_ref, o_ref):
    route = route_ref[...]
    w1 = route[:, 2:3]
    w2 = route[:, 3:4]
    ya = _unpack_rows(yab_ref[0], yab_ref[2]).astype(F32)
    yb = _unpack_rows(yab_ref[1], yab_ref[3]).astype(F32)
    x2 = x1_ref[...] + w1 * ya + w2 * yb
    ms = jnp.mean(x2 * x2, axis=-1, keepdims=True)
    o_ref[...] = x2 * lax.rsqrt(ms + EPS) * g_ref[...]


def _final(x1, yab, route, g_final, interpret):
    n = x1.shape[0]
    tm = FIN_TM
    row = lambda i: (i, 0)
    return pl.pallas_call(
        _final_kernel,
        grid=(n // tm,),
        in_specs=[
            pl.BlockSpec((tm, D_MODEL), row),
            pl.BlockSpec((4, tm, SC_ROW_WORDS), lambda i: (0, i, 0)),
            pl.BlockSpec((tm, LANES), row),
            pl.BlockSpec((1, D_MODEL), lambda i: (0, 0)),
        ],
        out_specs=pl.BlockSpec((tm, D_MODEL), row),
        out_shape=jax.ShapeDtypeStruct((n, D_MODEL), F32),
        compiler_params=_cparams(("arbitrary",)),
        interpret=interpret,
        name="combine_final",
    )(x1, yab, route, g_final)


def _dest_kernel(offs_ref, route_t_ref, idx_ref, *, n_rows):
    route_t = route_t_ref[...]
    experts = route_t[0:2, :]
    dest = route_t[4:6, :].astype(jnp.int32)
    for e in range(N_EXPERTS):
        dest = dest + jnp.where(experts == float(e), offs_ref[e], 0)
    idx_ref[0:2, :] = dest
    idx_ref[2:4, :] = dest + n_rows


def _route_plan(route_t, counts, n, interpret):
    cnt = counts[0, ROUTER_OFF:ROUTER_OFF + N_EXPERTS].astype(jnp.int32)
    padded = ((cnt + EXP_TM - 1) // EXP_TM) * EXP_TM
    ends = jnp.cumsum(padded)
    offs = ends - padded
    n_rows = 2 * n + N_EXPERTS * EXP_TM
    idx4 = pl.pallas_call(
        functools.partial(_dest_kernel, n_rows=n_rows),
        grid_spec=pltpu.PrefetchScalarGridSpec(
            num_scalar_prefetch=1, grid=(1,),
            in_specs=[pl.BlockSpec(route_t.shape, lambda i, offs: (0, 0))],
            out_specs=pl.BlockSpec((4, n), lambda i, offs: (0, 0))),
        out_shape=jax.ShapeDtypeStruct((4, n), jnp.int32),
        interpret=interpret,
        name="route_dest",
    )(offs, route_t)
    tile_row = jnp.minimum(jnp.arange(n_rows // EXP_TM, dtype=jnp.int32), ends[-1] // EXP_TM - 1)
    tile_expert = jnp.sum((ends[None, :] <= (tile_row * EXP_TM)[:, None]).astype(jnp.int32), axis=1)
    return idx4, tile_expert, tile_row, n_rows


def _sc_mesh():
    return plsc.VectorSubcoreMesh(core_axis_name="core", subcore_axis_name="subcore")


def _sc_scatter_rows(rows, idx4, n_out):
    n_in, w = rows.shape
    nb = idx4.shape[1] // SC_WINDOW

    @functools.partial(pl.kernel, out_type=jax.ShapeDtypeStruct((n_out, w), rows.dtype), mesh=_sc_mesh(),
                       scratch_types=[], name="sc_scatter_rows")
    def scatter(x_hbm, ia_hbm, ib_hbm, o_hbm):
        def body(x_vmem, ia_vmem, ib_vmem):
            pltpu.sync_copy(x_vmem, o_hbm.at[ia_vmem.at[0]])
            pltpu.sync_copy(x_vmem, o_hbm.at[ib_vmem.at[0]])

        pltpu.emit_pipeline(
            body,
            grid=(n_in // SC_WINDOW,),
            in_specs=[pl.BlockSpec((SC_WINDOW, w), lambda i: (i, 0)),
                      pl.BlockSpec((1, SC_WINDOW), lambda i: (2 * (i // nb), i % nb)),
                      pl.BlockSpec((1, SC_WINDOW), lambda i: (2 * (i // nb) + 1, i % nb))],
            out_specs=[],
            core_axis_name=("core", "subcore"),
            dimension_semantics=(pltpu.PARALLEL,),
        )(x_hbm, ia_hbm, ib_hbm)

    return scatter(rows, idx4, idx4)


def _sc_gather_rows(table, idx4):
    nb = idx4.shape[1] // SC_WINDOW
    n_idx = idx4.shape[0] * idx4.shape[1]
    w = table.shape[1]

    @functools.partial(pl.kernel, out_type=jax.ShapeDtypeStruct((n_idx, w), table.dtype), mesh=_sc_mesh(),
                       scratch_types=[], name="sc_gather_rows")
    def gather(t_hbm, i_hbm, o_hbm):
        def body(i_vmem, o_vmem):
            pltpu.sync_copy(t_hbm.at[i_vmem.at[0]], o_vmem)

        pltpu.emit_pipeline(
            body,
            grid=(n_idx // SC_WINDOW,),
            in_specs=[pl.BlockSpec((1, SC_WINDOW), lambda i: (i // nb, i % nb))],
            out_specs=[pl.BlockSpec((SC_WINDOW, w), lambda i: (i, 0))],
            core_axis_name=("core", "subcore"),
            dimension_semantics=(pltpu.PARALLEL,),
        )(i_hbm, o_hbm)

    return gather(table, idx4)


def _forward(x, g_mix, w_in, w_attn_branch, w_ret_branch, w_out, g_ffn, w_group_router, b_group_router,
             w_expert_router, b_expert_router, w_gate, w_up, w_down, g_final, interpret=False):
    batch, seq, d = x.shape
    n = batch * seq
    x2 = x.reshape(n, d)
    proj, wg_bf, wu_bf, wd_bf = _proj(x2, g_mix[0][None, :], w_in[0].astype(BF16), w_gate[0], w_up[0],
                                      w_down[0], interpret)
    o_attn = _attention(proj, batch, seq, interpret)
    o_ret = _retention(proj, batch, seq, interpret)
    pad = LANES - N_EXPERT_GROUPS - N_EXPERTS
    w_router = jnp.concatenate([w_group_router[0], w_expert_router[0], jnp.zeros((d, pad), F32)], axis=-1)
    w_router_hi = w_router.astype(BF16)
    w_router_lo = (w_router - w_router_hi.astype(F32)).astype(BF16)
    w_router2 = jnp.concatenate([w_router_hi, w_router_lo], axis=-1)
    b_router = jnp.concatenate([b_group_router[0], b_expert_router[0], jnp.zeros((pad,), F32)])[None, :]
    x1, h2p, route, route_t, counts = _mix(o_attn, o_ret, proj, x2, w_attn_branch[0].astype(BF16),
                                           w_ret_branch[0].astype(BF16), w_out[0].astype(BF16),
                                           g_ffn[0][None, :], w_router2, b_router, interpret)
    idx4, tile_expert, tile_row, n_rows = _route_plan(route_t, counts, n, interpret)
    xs = _sc_scatter_rows(h2p.reshape(2 * n, SC_ROW_WORDS), idx4, 2 * n_rows)
    ys = _experts(xs.reshape(2, n_rows, SC_ROW_WORDS), tile_expert, tile_row, wg_bf, wu_bf, wd_bf, interpret)
    yab = _sc_gather_rows(ys.reshape(2 * n_rows, SC_ROW_WORDS), idx4)
    out = _final(x1, yab.reshape(4, n, SC_ROW_WORDS), route, g_final[None, :], interpret)
    return out.reshape(batch, seq, d)


def kernel(x, g_mix, w_in, w_attn_branch, w_ret_branch, w_out, g_ffn, w_group_router, b_group_router,
           w_expert_router, b_expert_router, w_gate, w_up, w_down, g_final):
    return _forward(x, g_mix, w_in, w_attn_branch, w_ret_branch, w_out, g_ffn, w_group_router,
                    b_group_router, w_expert_router, b_expert_router, w_gate, w_up, w_down, g_final)
```

```python
import functools

import numpy as np
import jax
import jax.numpy as jnp
from jax import lax
from jax.experimental import pallas as pl
from jax.experimental.pallas import tpu as pltpu
from jax.experimental.pallas import tpu_sc as plsc

F32 = jnp.float32
BF16 = jnp.bfloat16

D_MODEL = 1024
ATTN_GROUPS = ((128, 1), (512, 4), (2048, 16))
N_GROUPS = len(ATTN_GROUPS)
ATTN_HEADS = 8
HEAD_DIM = 64
GROUP_W = ATTN_HEADS * HEAD_DIM
QKV_W = N_GROUPS * GROUP_W
RET_HEADS = 4
RET_DK = 128
RET_DV = 256
RET_CHUNK = 128
RET_TS = 1024
N_EXPERT_GROUPS = 4
EXPERTS_PER_GROUP = 8
N_EXPERTS = N_EXPERT_GROUPS * EXPERTS_PER_GROUP
EXPERT_FF = 512
EPS = 1e-6

LANES = 128
BLK = 128
SPAN = 2048
NEG = -1e30
ACC_PARTS = 3

COL_QA = 0
COL_KA = COL_QA + QKV_W
COL_VA = COL_KA + QKV_W
COL_QR = COL_VA + QKV_W
COL_KR = COL_QR + RET_HEADS * RET_DK
COL_VR = COL_KR + RET_HEADS * RET_DK
COL_GR = COL_VR + RET_HEADS * RET_DV
COL_GATE_A = COL_GR + RET_HEADS * RET_DV
COL_GATE_R = COL_GATE_A + D_MODEL
IN_WIDTH = COL_GATE_R + D_MODEL
COL_BLOCK = 512

PROJ_TM = 512
PROJ_TN = IN_WIDTH // 2
MXU_N = 256
MIX_TM = 1024
MIX_CHUNK = 512
ROUTE_CHUNK = 256
ROUTE_ROWS = 8
EXP_TM = 512
SC_WINDOW = 128
SC_ROW_WORDS = 256
FIN_TM = 1024
VMEM_LIMIT = 56 * 1024 * 1024


def _cparams(sem):
    return pltpu.CompilerParams(dimension_semantics=sem, vmem_limit_bytes=VMEM_LIMIT)


def _sigmoid(x):
    return 0.5 * jnp.tanh(0.5 * x) + 0.5


def _dilated_chunks():
    chunks = {}
    per_group = GROUP_W // MXU_N
    for t, col in enumerate((COL_QA, COL_KA, COL_VA)):
        for gi in (1, 2):
            for k in range(per_group):
                chunks[(col + gi * GROUP_W) // MXU_N + k] = (gi, t * GROUP_W + k * MXU_N)
    return chunks


def _proj_kernel(x_ref, g_ref, w_ref, wg_ref, wu_ref, wd_ref, o_ref, d4_ref, d16_ref, wg_o, wu_o, wd_o,
                 stage_ref, mid_ref):
    x = x_ref[...]
    ms = jnp.mean(x * x, axis=-1, keepdims=True)
    h = (x * lax.rsqrt(ms + EPS) * g_ref[...]).astype(BF16)
    step = ATTN_GROUPS[1][1]
    quarter = PROJ_TM // step

    def matmuls(dilated):
        for c in range(PROJ_TN // MXU_N):
            sl = slice(c * MXU_N, (c + 1) * MXU_N)
            res = jnp.dot(h, w_ref[:, sl], preferred_element_type=F32)
            o_ref[:, sl] = res.astype(o_ref.dtype)
            if c not in dilated:
                continue
            gi, col = dilated[c]
            slot = c % 2
            for lt in range(MXU_N // LANES):
                stage_ref[slot, lt] = res[:, lt * LANES:(lt + 1) * LANES]
            for lt in range(MXU_N // LANES):
                cols = slice(col + lt * LANES, col + (lt + 1) * LANES)
                for b in range(step):
                    rows = stage_ref[slot, lt, pl.ds(b, quarter, stride=step), :]
                    if gi == 1:
                        d4_ref[0, b, :, cols] = rows.astype(BF16)
                    else:
                        mid_ref[slot, lt, b * quarter:(b + 1) * quarter, :] = rows
                if gi == 2:
                    for b in range(step):
                        for a in range(step):
                            rows = mid_ref[slot, lt, pl.ds(b * quarter + a, quarter // step, stride=step), :]
                            d16_ref[0, a * step + b, :, cols] = rows.astype(BF16)

    @pl.when(pl.program_id(0) == 0)
    def _():
        matmuls(_dilated_chunks())

    @pl.when(pl.program_id(0) != 0)
    def _():
        matmuls({})

    wg_o[...] = wg_ref[...].astype(BF16)
    wu_o[...] = wu_ref[...].astype(BF16)
    wd_o[...] = wd_ref[...].astype(BF16)


def _proj(x2, g_mix, w_in_bf16, w_gate, w_up, w_down, batch, seq, interpret):
    n = x2.shape[0]
    n_i = n // PROJ_TM
    tiles = seq // PROJ_TM
    steps = (IN_WIDTH // PROJ_TN) * n_i
    flat = [w.reshape(-1, w.shape[-1]) for w in (w_gate, w_up, w_down)]
    w_specs = [pl.BlockSpec((w.shape[0] // steps, w.shape[1]), lambda j, i: (j * n_i + i, 0)) for w in flat]
    dils = [ATTN_GROUPS[gi][1] for gi in (1, 2)]

    def dilated_map(j, i):
        t = jnp.where(j == 0, i, n_i - 1)
        return (t // tiles, 0, t % tiles, 0)

    outs = pl.pallas_call(
        _proj_kernel,
        grid=(IN_WIDTH // PROJ_TN, n_i),
        in_specs=[
            pl.BlockSpec((PROJ_TM, D_MODEL), lambda j, i: (i, 0)),
            pl.BlockSpec((1, D_MODEL), lambda j, i: (0, 0)),
            pl.BlockSpec((D_MODEL, PROJ_TN), lambda j, i: (0, j)),
        ] + w_specs,
        out_specs=[pl.BlockSpec((PROJ_TM, PROJ_TN), lambda j, i: (i, j))]
        + [pl.BlockSpec((1, d, PROJ_TM // d, 3 * GROUP_W), dilated_map) for d in dils] + w_specs,
        out_shape=[jax.ShapeDtypeStruct((n, IN_WIDTH), BF16)]
        + [jax.ShapeDtypeStruct((batch, d, seq // d, 3 * GROUP_W), BF16) for d in dils]
        + [jax.ShapeDtypeStruct(w.shape, BF16) for w in flat],
        scratch_shapes=[pltpu.VMEM((2, MXU_N // LANES, PROJ_TM, LANES), F32),
                        pltpu.VMEM((2, MXU_N // LANES, PROJ_TM, LANES), F32)],
        compiler_params=_cparams(("arbitrary", "arbitrary")),
        interpret=interpret,
        name="proj",
    )(x2, g_mix, w_in_bf16, *flat)
    return (outs[0], outs[1], outs[2], outs[3].reshape(w_gate.shape), outs[4].reshape(w_up.shape),
            outs[5].reshape(w_down.shape))


def _attn_unit(q2, kk, vv, bias_a, bias_b):
    lane = lax.broadcasted_iota(jnp.int32, (BLK, LANES), 1)
    left = lane < HEAD_DIM
    zero = jnp.zeros_like(q2)
    nt = (((1,), (1,)), ((), ()))
    q_st = jnp.concatenate([jnp.where(left, q2, zero), jnp.where(left, zero, q2)], axis=0)
    s = lax.dot_general(q_st, kk, nt, preferred_element_type=F32) + jnp.concatenate([bias_a, bias_b], axis=0)
    m = jnp.max(s, axis=-1, keepdims=True)
    p = jnp.exp(s - m)
    den = jnp.sum(p, axis=-1, keepdims=True)
    o = jnp.dot(p.astype(BF16), vv, preferred_element_type=F32)
    return (jnp.where(left, o[:BLK], o[BLK:]), jnp.where(left, m[:BLK], m[BLK:]),
            jnp.where(left, den[:BLK], den[BLK:]))


def _attn_kernel(q1_ref, k1_ref, v1_ref, q2_ref, k2_ref, v2_ref, q3_ref, k3_ref, v3_ref,
                 bias_ref, o_ref, acc_ref, accw_ref, *, seq):
    s_id = pl.program_id(2)
    step = ATTN_GROUPS[1][1]
    quarter = SPAN // step
    dilated = {1: (q2_ref, k2_ref, v2_ref), 2: (q3_ref, k3_ref, v3_ref)}

    def dilated_unit(gi, d, m, r, first):
        slot = gi - 1
        q_ref, k_ref, v_ref = dilated[gi]
        loc = BLK * m * d + r
        cur = pl.multiple_of(s_id * (SPAN // d) + BLK * m, BLK)
        prev = pl.multiple_of(jnp.where(first == 1, cur, cur - BLK), BLK)
        q2 = q_ref[0, r, BLK * m:BLK * (m + 1), :] * 0.125
        kk = jnp.concatenate([k_ref[0, r, pl.ds(prev, BLK), :], k_ref[0, r, pl.ds(cur, BLK), :]], axis=0)
        vv = jnp.concatenate([v_ref[0, r, pl.ds(prev, BLK), :], v_ref[0, r, pl.ds(cur, BLK), :]], axis=0)
        parts = _attn_unit(q2, kk, vv, bias_ref[gi, first, 0], bias_ref[gi, first, 1])
        for j, part in enumerate(parts):
            if d == step:
                acc_ref[ACC_PARTS * slot + j, pl.ds(loc, BLK, stride=d), :] = part
            else:
                a, b = divmod(r, step)
                accw_ref[j, pl.ds(b * quarter + a, BLK, stride=step), :] = part

    def reinterleave_pieces():
        pieces = []
        for j in range(ACC_PARTS):
            for b in range(step):
                for j0 in range(0, quarter, 256):
                    def piece(j=j, b=b, j0=j0):
                        acc_ref[ACC_PARTS + j, pl.ds(b + step * j0, 256, stride=step), :] = (
                            accw_ref[j, b * quarter + j0:b * quarter + j0 + 256, :])
                    pieces.append(piece)
        return pieces

    def dense_unit(m):
        loc = m * BLK
        cur = pl.multiple_of(s_id * SPAN + loc, BLK)
        prev = pl.multiple_of(jnp.maximum(cur - BLK, 0), BLK)
        first = jnp.where(cur == 0, 1, 0)
        q2 = q1_ref[loc:loc + BLK, :] * 0.125
        kk = jnp.concatenate([k1_ref[pl.ds(prev, BLK), :], k1_ref[pl.ds(cur, BLK), :]], axis=0)
        vv = jnp.concatenate([v1_ref[pl.ds(prev, BLK), :], v1_ref[pl.ds(cur, BLK), :]], axis=0)
        n1, m1, d1 = _attn_unit(q2, kk, vv, bias_ref[0, first, 0], bias_ref[0, first, 1])
        (n2, m2, d2), (n3, m3, d3) = (
            tuple(acc_ref[g * ACC_PARTS + j, loc:loc + BLK, :] for j in range(ACC_PARTS)) for g in range(2))
        mx = jnp.maximum(jnp.maximum(m1, m2), m3)
        w1, w2, w3 = jnp.exp(m1 - mx), jnp.exp(m2 - mx), jnp.exp(m3 - mx)
        num = w1 * n1 + w2 * n2 + w3 * n3
        den = w1 * d1 + w2 * d2 + w3 * d3
        o_ref[loc:loc + BLK, :] = (num / den).astype(o_ref.dtype)

    first_span = jnp.where(s_id == 0, 1, 0)
    for gi in (2, 1):
        d = ATTN_GROUPS[gi][1]
        for m in range(SPAN // (BLK * d)):
            for r in range(d):
                dilated_unit(gi, d, m, r, first_span if m == 0 else 0)
        if d != step:
            for piece in reinterleave_pieces():
                piece()
    for m in range(SPAN // BLK):
        dense_unit(m)


def _attn_bias():
    slopes = np.exp2(-8.0 * np.arange(1, ATTN_HEADS + 1, dtype=np.float64) / ATTN_HEADS)
    qi = np.arange(BLK)[:, None]
    kj = np.arange(2 * BLK)[None, :]
    rel = qi + BLK - kj
    out = np.zeros((N_GROUPS, 2, ATTN_HEADS, BLK, 2 * BLK), np.float32)
    for gi, (window, d) in enumerate(ATTN_GROUPS):
        n_back = window // d
        assert n_back == BLK
        valid = (rel >= 0) & (rel <= n_back)
        bias = -slopes[:, None, None] * (rel * d)[None].astype(np.float64)
        out[gi, 0] = np.where(valid[None], bias, NEG)
        out[gi, 1] = np.where((valid & (kj >= BLK))[None], bias, NEG)
    return jnp.asarray(out)


def _attention(proj, dilated, batch, seq, interpret):
    n = batch * seq
    spans = seq // SPAN
    n_hp = GROUP_W // LANES
    heads = GROUP_W // LANES
    specs = [pl.BlockSpec((SPAN, LANES), lambda b, hp, s: (b * spans + s, COL_QA // LANES + hp)),
             pl.BlockSpec((seq, LANES), lambda b, hp, s: (b, COL_KA // LANES + hp)),
             pl.BlockSpec((seq, LANES), lambda b, hp, s: (b, COL_VA // LANES + hp))]
    operands = [proj, proj, proj]
    for gi, arr in zip((1, 2), dilated):
        d = ATTN_GROUPS[gi][1]
        specs += [pl.BlockSpec((1, d, SPAN // d, LANES), lambda b, hp, s: (b, 0, s, hp)),
                  pl.BlockSpec((1, d, seq // d, LANES), lambda b, hp, s: (b, 0, 0, heads + hp)),
                  pl.BlockSpec((1, d, seq // d, LANES), lambda b, hp, s: (b, 0, 0, 2 * heads + hp))]
        operands += [arr, arr, arr]
    bias_spec = pl.BlockSpec((N_GROUPS, 2, 2, BLK, 2 * BLK), lambda b, hp, s: (0, 0, hp, 0, 0))
    return pl.pallas_call(
        functools.partial(_attn_kernel, seq=seq),
        grid=(batch, n_hp, spans),
        in_specs=specs + [bias_spec],
        out_specs=pl.BlockSpec((SPAN, LANES), lambda b, hp, s: (b * spans + s, hp)),
        out_shape=jax.ShapeDtypeStruct((n, GROUP_W), BF16),
        scratch_shapes=[
            pltpu.VMEM((2 * ACC_PARTS, SPAN, LANES), F32),
            pltpu.VMEM((ACC_PARTS, SPAN, LANES), F32),
        ],
        compiler_params=_cparams(("arbitrary", "arbitrary", "arbitrary")),
        interpret=interpret,
        name="attn",
    )(*operands, _attn_bias())


def _ret_kernel(q_ref, k_ref, v0_ref, v1_ref, g0_ref, g1_ref, dec_ref, xi_ref, zeta_ref, gch_ref, o_ref, st_ref):
    @pl.when(pl.program_id(1) == 0)
    def _():
        st_ref[...] = jnp.zeros_like(st_ref)

    nt = (((1,), (1,)), ((), ()))
    scale = RET_DK ** -0.5
    heads_per_block = COL_BLOCK // RET_DV

    for c in range(RET_TS // RET_CHUNK):
        rows = pl.ds(c * RET_CHUNK, RET_CHUNK)
        for h in range(RET_HEADS):
            kcols = slice(h * RET_DK, (h + 1) * RET_DK)
            vcols = slice(h * RET_DV, (h + 1) * RET_DV)
            v_ref, gr_ref = ((v0_ref, g0_ref), (v1_ref, g1_ref))[h // heads_per_block]
            hcols = slice((h % heads_per_block) * RET_DV, (h % heads_per_block + 1) * RET_DV)
            qi = q_ref[rows, kcols]
            kf = k_ref[rows, kcols].astype(F32) * scale
            ki = kf.astype(BF16)
            kz_t = jnp.transpose(kf * zeta_ref[h]).astype(BF16)
            vi = v_ref[rows, hcols]
            att = lax.dot_general(qi, ki, nt, preferred_element_type=F32) * dec_ref[h]
            inner = jnp.dot(att.astype(BF16), vi, preferred_element_type=F32)
            st = st_ref[h]
            cross = jnp.dot(qi, st.astype(BF16), preferred_element_type=F32) * xi_ref[h]
            st_ref[h] = gch_ref[h] * st + jnp.dot(kz_t, vi, preferred_element_type=F32)
            y = inner + cross
            mu = jnp.mean(y, axis=-1, keepdims=True)
            yc = y - mu
            var = jnp.mean(yc * yc, axis=-1, keepdims=True)
            yn = yc * lax.rsqrt(var + EPS)
            g = gr_ref[rows, hcols].astype(F32)
            o_ref[rows, vcols] = (g * _sigmoid(g) * yn).astype(o_ref.dtype)


def _ret_tables():
    c = RET_CHUNK
    log_g = np.log1p(-np.exp2(-5.0 - np.arange(RET_HEADS, dtype=np.float64)))
    pos = np.arange(c, dtype=np.float64)
    diff = pos[:, None] - pos[None, :]
    dec = np.where(diff >= 0, np.exp(log_g[:, None, None] * np.maximum(diff, 0.0)), 0.0)
    xi = np.exp(log_g[:, None] * (pos + 1.0))[..., None] * np.ones((1, 1, RET_DV))
    zeta = np.exp(log_g[:, None] * (c - 1.0 - pos))[..., None] * np.ones((1, 1, RET_DK))
    gch = np.exp(log_g * c)[:, None, None] * np.ones((1, 1, RET_DV))
    return tuple(jnp.asarray(t, F32) for t in (dec, xi, zeta, gch))


def _retention(proj, batch, seq, interpret):
    n = batch * seq
    dec, xi, zeta, gch = _ret_tables()
    qk_w = RET_HEADS * RET_DK
    v_w = RET_HEADS * RET_DV
    nts = seq // RET_TS
    const3 = lambda b, t: (0, 0, 0)
    return pl.pallas_call(
        _ret_kernel,
        grid=(batch, nts),
        in_specs=[
            pl.BlockSpec((RET_TS, qk_w), lambda b, t: (b * nts + t, COL_QR // qk_w)),
            pl.BlockSpec((RET_TS, qk_w), lambda b, t: (b * nts + t, COL_KR // qk_w)),
            pl.BlockSpec((RET_TS, COL_BLOCK), lambda b, t: (b * nts + t, COL_VR // COL_BLOCK)),
            pl.BlockSpec((RET_TS, COL_BLOCK), lambda b, t: (b * nts + t, COL_VR // COL_BLOCK + 1)),
            pl.BlockSpec((RET_TS, COL_BLOCK), lambda b, t: (b * nts + t, COL_GR // COL_BLOCK)),
            pl.BlockSpec((RET_TS, COL_BLOCK), lambda b, t: (b * nts + t, COL_GR // COL_BLOCK + 1)),
            pl.BlockSpec((RET_HEADS, RET_CHUNK, RET_CHUNK), const3),
            pl.BlockSpec((RET_HEADS, RET_CHUNK, RET_DV), const3),
            pl.BlockSpec((RET_HEADS, RET_CHUNK, RET_DK), const3),
            pl.BlockSpec((RET_HEADS, 1, RET_DV), const3),
        ],
        out_specs=pl.BlockSpec((RET_TS, v_w), lambda b, t: (b * nts + t, 0)),
        out_shape=jax.ShapeDtypeStruct((n, v_w), BF16),
        scratch_shapes=[pltpu.VMEM((RET_HEADS, RET_DK, RET_DV), F32)],
        compiler_params=_cparams(("arbitrary", "arbitrary")),
        interpret=interpret,
        name="retention",
    )(proj, proj, proj, proj, proj, proj, dec, xi, zeta, gch)


ROUTER_OFF = N_EXPERT_GROUPS


def _pack_bf16_pair(a, b):
    hi = lax.bitcast_convert_type(a.astype(BF16).astype(F32), jnp.uint32)
    lo = lax.bitcast_convert_type(b.astype(BF16).astype(F32), jnp.uint32)
    return lax.bitcast_convert_type(hi | (lo >> 16), jnp.int32)


def _unpack_bf16_pair(w):
    u = lax.bitcast_convert_type(w, jnp.uint32)
    a = lax.bitcast_convert_type(u & jnp.uint32(0xFFFF0000), F32).astype(BF16)
    b = lax.bitcast_convert_type(u << 16, F32).astype(BF16)
    return a, b


def _pack_rows(y):
    q = D_MODEL // 4
    return (_pack_bf16_pair(y[:, 0:q], y[:, 2 * q:3 * q]), _pack_bf16_pair(y[:, q:2 * q], y[:, 3 * q:4 * q]))


def _unpack_rows(slab0, slab1):
    q0, q2 = _unpack_bf16_pair(slab0)
    q1, q3 = _unpack_bf16_pair(slab1)
    return jnp.concatenate([q0, q1, q2, q3], axis=1)


def _mix_kernel(oa_ref, or_ref, ga0_ref, ga1_ref, gr0_ref, gr1_ref, x_ref, pa_ref, pr_ref, wo_ref, gf_ref, wr_ref, br_ref,
                x1_ref, h2_ref, route_ref, route_t_ref, cnt_ref, carry_ref, logit_ref):
    step = pl.program_id(0)

    @pl.when(step == 0)
    def _():
        carry_ref[...] = jnp.zeros_like(carry_ref)
        logit_ref[...] = jnp.zeros_like(logit_ref)

    routing = iter([functools.partial(_route_rows, pl.ds(c * ROUTE_CHUNK, ROUTE_CHUNK), step > 0, logit_ref,
                                      route_ref, route_t_ref, cnt_ref, carry_ref)
                    for c in range(MIX_TM // ROUTE_CHUNK)])
    for c in range(MIX_TM // MIX_CHUNK):
        for _ in _mix_rows(pl.ds(c * MIX_CHUNK, MIX_CHUNK), oa_ref, or_ref, (ga0_ref, ga1_ref), (gr0_ref, gr1_ref),
                           x_ref, pa_ref, pr_ref, wo_ref, gf_ref, wr_ref, br_ref, x1_ref, h2_ref, logit_ref):
            next(routing, lambda: None)()
    for piece in routing:
        piece()


def _mix_rows(rows, oa_ref, or_ref, ga_refs, gr_refs, x_ref, pa_ref, pr_ref, wo_ref, gf_ref, wr_ref, br_ref,
              x1_ref, h2_ref, logit_ref):
    a = jnp.dot(oa_ref[rows, :], pa_ref[...], preferred_element_type=F32)
    yield
    r = jnp.dot(or_ref[rows, :], pr_ref[...], preferred_element_type=F32)
    yield
    merged = jnp.concatenate(
        [_sigmoid(ga[rows, :].astype(F32)) * a[:, k * COL_BLOCK:(k + 1) * COL_BLOCK]
         + _sigmoid(gr[rows, :].astype(F32)) * r[:, k * COL_BLOCK:(k + 1) * COL_BLOCK]
         for k, (ga, gr) in enumerate(zip(ga_refs, gr_refs))], axis=1)
    x1 = x_ref[rows, :] + jnp.dot(merged.astype(BF16), wo_ref[...], preferred_element_type=F32)
    x1_ref[rows, :] = x1
    ms = jnp.mean(x1 * x1, axis=-1, keepdims=True)
    h2 = x1 * lax.rsqrt(ms + EPS) * gf_ref[...]
    h2_ref[0, rows, :], h2_ref[1, rows, :] = _pack_rows(h2)

    h_hi = h2.astype(BF16)
    h_lo = (h2 - h_hi.astype(F32)).astype(BF16)
    both = jnp.dot(h_hi, wr_ref[...], preferred_element_type=F32)
    logit_ref[rows, :] = (both[:, :LANES] + both[:, LANES:]
                          + jnp.dot(h_lo, wr_ref[:, :LANES], preferred_element_type=F32) + br_ref[...])


def _route_rows(rows, live, logit_ref, route_ref, route_t_ref, cnt_ref, carry_ref):
    logits = logit_ref[rows, :]
    tm = logits.shape[0]
    lane = lax.broadcasted_iota(jnp.int32, (tm, LANES), 1).astype(F32)
    big = jnp.float32(4 * LANES)
    ninf = -jnp.inf
    is_g = lane < N_EXPERT_GROUPS
    gl = jnp.where(is_g, logits, ninf)
    gmax = jnp.max(gl, axis=-1, keepdims=True)
    gsum = jnp.sum(jnp.where(is_g, jnp.exp(gl - gmax), 0.0), axis=-1, keepdims=True)
    g_val = 1.0 / gsum
    g_idx = jnp.min(jnp.where(jnp.logical_and(is_g, gl == gmax), lane, big), axis=-1, keepdims=True)
    lo = ROUTER_OFF + EXPERTS_PER_GROUP * g_idx
    in_grp = jnp.logical_and(lane >= lo, lane < lo + EXPERTS_PER_GROUP)
    el = jnp.where(in_grp, logits, ninf)
    v1 = jnp.max(el, axis=-1, keepdims=True)
    i1 = jnp.min(jnp.where(jnp.logical_and(in_grp, el == v1), lane, big), axis=-1, keepdims=True)
    rest = jnp.logical_and(in_grp, lane != i1)
    el2 = jnp.where(rest, logits, ninf)
    v2 = jnp.max(el2, axis=-1, keepdims=True)
    i2 = jnp.min(jnp.where(jnp.logical_and(rest, el2 == v2), lane, big), axis=-1, keepdims=True)
    t = jnp.exp(v2 - v1)
    w1 = g_val / (1.0 + t)
    w2 = g_val * t / (1.0 + t)

    sel = jnp.logical_or(lane == i1, lane == i2)
    sel_bf = jnp.where(sel, 1.0, 0.0).astype(BF16)
    row = lax.broadcasted_iota(jnp.int32, (tm, tm), 0)
    col = lax.broadcasted_iota(jnp.int32, (tm, tm), 1)
    tri = jnp.where(col < row, 1.0, 0.0).astype(BF16)
    before = jnp.dot(tri, sel_bf, preferred_element_type=F32) + carry_ref[...]
    r1 = jnp.sum(jnp.where(lane == i1, before, 0.0), axis=-1, keepdims=True)
    r2 = jnp.sum(jnp.where(lane == i2, before, 0.0), axis=-1, keepdims=True)
    carry = carry_ref[...] + jnp.where(live, jnp.sum(jnp.where(sel, 1.0, 0.0), axis=0, keepdims=True), 0.0)
    carry_ref[...] = carry
    cnt_ref[...] = carry

    vals = (i1 - ROUTER_OFF, i2 - ROUTER_OFF, w1, w2, r1, r2)
    route = jnp.zeros((tm, LANES), F32)
    for j, v in enumerate(vals):
        route = jnp.where(lane == j, v, route)
    route_ref[rows, :] = route
    route_t_ref[:, rows] = jnp.transpose(route)[:ROUTE_ROWS, :]


def _mix(o_attn, o_ret, proj, x2, pa, pr, wo, g_ffn, w_router, b_router, interpret):
    n = x2.shape[0]
    tm = MIX_TM
    last = n // tm - 1
    const = lambda i: (0, 0)
    cur = lambda i: jnp.minimum(i, last)
    prev = lambda i: jnp.maximum(i - 1, 0)
    return pl.pallas_call(
        _mix_kernel,
        grid=(n // tm + 1,),
        in_specs=[
            pl.BlockSpec((tm, GROUP_W), lambda i: (cur(i), 0)),
            pl.BlockSpec((tm, D_MODEL), lambda i: (cur(i), 0)),
            pl.BlockSpec((tm, COL_BLOCK), lambda i: (cur(i), COL_GATE_A // COL_BLOCK)),
            pl.BlockSpec((tm, COL_BLOCK), lambda i: (cur(i), COL_GATE_A // COL_BLOCK + 1)),
            pl.BlockSpec((tm, COL_BLOCK), lambda i: (cur(i), COL_GATE_R // COL_BLOCK)),
            pl.BlockSpec((tm, COL_BLOCK), lambda i: (cur(i), COL_GATE_R // COL_BLOCK + 1)),
            pl.BlockSpec((tm, D_MODEL), lambda i: (cur(i), 0)),
            pl.BlockSpec((GROUP_W, D_MODEL), const),
            pl.BlockSpec((D_MODEL, D_MODEL), const),
            pl.BlockSpec((D_MODEL, D_MODEL), const),
            pl.BlockSpec((1, D_MODEL), const),
            pl.BlockSpec((D_MODEL, 2 * LANES), const),
            pl.BlockSpec((1, LANES), const),
        ],
        out_specs=[
            pl.BlockSpec((tm, D_MODEL), lambda i: (cur(i), 0)),
            pl.BlockSpec((2, tm, SC_ROW_WORDS), lambda i: (0, cur(i), 0)),
            pl.BlockSpec((tm, LANES), lambda i: (prev(i), 0)),
            pl.BlockSpec((ROUTE_ROWS, tm), lambda i: (0, prev(i))),
            pl.BlockSpec((1, LANES), const),
        ],
        out_shape=[
            jax.ShapeDtypeStruct((n, D_MODEL), F32),
            jax.ShapeDtypeStruct((2, n, SC_ROW_WORDS), jnp.int32),
            jax.ShapeDtypeStruct((n, LANES), F32),
            jax.ShapeDtypeStruct((ROUTE_ROWS, n), F32),
            jax.ShapeDtypeStruct((1, LANES), F32),
        ],
        scratch_shapes=[pltpu.VMEM((1, LANES), F32), pltpu.VMEM((tm, LANES), F32)],
        compiler_params=_cparams(("arbitrary",)),
        interpret=interpret,
        name="mix_router",
    )(o_attn, o_ret, proj, proj, proj, proj, x2, pa, pr, wo, g_ffn, w_router, b_router)


def _expert_kernel(te_ref, tr_ref, xs_ref, wg_ref, wu_ref, wd_ref, o_ref):
    i = pl.program_id(0)

    @pl.when(tr_ref[i] == i)
    def _():
        xs = _unpack_rows(xs_ref[0], xs_ref[1])
        a = jnp.dot(xs, wg_ref[0], preferred_element_type=F32)
        u = jnp.dot(xs, wu_ref[0], preferred_element_type=F32)
        hid = (a * _sigmoid(a) * u).astype(BF16)
        y = jnp.dot(hid, wd_ref[0], preferred_element_type=F32)
        o_ref[0], o_ref[1] = _pack_rows(y)


def _experts(xs, tile_expert, tile_row, w_gate, w_up, w_down, interpret):
    p = xs.shape[1]
    n_tiles = p // EXP_TM
    grid_spec = pltpu.PrefetchScalarGridSpec(
        num_scalar_prefetch=2,
        grid=(n_tiles,),
        in_specs=[
            pl.BlockSpec((2, EXP_TM, SC_ROW_WORDS), lambda i, te, tr: (0, tr[i], 0)),
            pl.BlockSpec((1, D_MODEL, EXPERT_FF), lambda i, te, tr: (te[i], 0, 0)),
            pl.BlockSpec((1, D_MODEL, EXPERT_FF), lambda i, te, tr: (te[i], 0, 0)),
            pl.BlockSpec((1, EXPERT_FF, D_MODEL), lambda i, te, tr: (te[i], 0, 0)),
        ],
        out_specs=pl.BlockSpec((2, EXP_TM, SC_ROW_WORDS), lambda i, te, tr: (0, tr[i], 0)),
    )
    return pl.pallas_call(
        _expert_kernel,
        grid_spec=grid_spec,
        out_shape=jax.ShapeDtypeStruct((2, p, SC_ROW_WORDS), jnp.int32),
        compiler_params=_cparams(("arbitrary",)),
        interpret=interpret,
        name="experts",
    )(tile_expert, tile_row, xs, w_gate, w_up, w_down)


def _final_kernel(x1_ref, yab_ref, route_ref, g_ref, o_ref):
    route = route_ref[...]
    w1 = route[:, 2:3]
    w2 = route[:, 3:4]
    ya = _unpack_rows(yab_ref[0], yab_ref[2]).astype(F32)
    yb = _unpack_rows(yab_ref[1], yab_ref[3]).astype(F32)
    x2 = x1_ref[...] + w1 * ya + w2 * yb
    ms = jnp.mean(x2 * x2, axis=-1, keepdims=True)
    o_ref[...] = x2 * lax.rsqrt(ms + EPS) * g_ref[...]


def _final(x1, yab, route, g_final, interpret):
    n = x1.shape[0]
    tm = FIN_TM
    row = lambda i: (i, 0)
    return pl.pallas_call(
        _final_kernel,
        grid=(n // tm,),
        in_specs=[
            pl.BlockSpec((tm, D_MODEL), row),
            pl.BlockSpec((4, tm, SC_ROW_WORDS), lambda i: (0, i, 0)),
            pl.BlockSpec((tm, LANES), row),
            pl.BlockSpec((1, D_MODEL), lambda i: (0, 0)),
        ],
        out_specs=pl.BlockSpec((tm, D_MODEL), row),
        out_shape=jax.ShapeDtypeStruct((n, D_MODEL), F32),
        compiler_params=_cparams(("arbitrary",)),
        interpret=interpret,
        name="combine_final",
    )(x1, yab, route, g_final)


def _dest_kernel(offs_ref, route_t_ref, idx_ref, *, n_rows):
    route_t = route_t_ref[...]
    experts = route_t[0:2, :]
    dest = route_t[4:6, :].astype(jnp.int32)
    for e in range(N_EXPERTS):
        dest = dest + jnp.where(experts == float(e), offs_ref[e], 0)
    idx_ref[0:2, :] = dest
    idx_ref[2:4, :] = dest + n_rows


def _route_plan(route_t, counts, n, interpret):
    cnt = counts[0, ROUTER_OFF:ROUTER_OFF + N_EXPERTS].astype(jnp.int32)
    padded = ((cnt + EXP_TM - 1) // EXP_TM) * EXP_TM
    ends = jnp.cumsum(padded)
    offs = ends - padded
    n_rows = 2 * n + N_EXPERTS * EXP_TM
    idx4 = pl.pallas_call(
        functools.partial(_dest_kernel, n_rows=n_rows),
        grid_spec=pltpu.PrefetchScalarGridSpec(
            num_scalar_prefetch=1, grid=(1,),
            in_specs=[pl.BlockSpec(route_t.shape, lambda i, offs: (0, 0))],
            out_specs=pl.BlockSpec((4, n), lambda i, offs: (0, 0))),
        out_shape=jax.ShapeDtypeStruct((4, n), jnp.int32),
        interpret=interpret,
        name="route_dest",
    )(offs, route_t)
    tile_row = jnp.minimum(jnp.arange(n_rows // EXP_TM, dtype=jnp.int32), ends[-1] // EXP_TM - 1)
    tile_expert = jnp.sum((ends[None, :] <= (tile_row * EXP_TM)[:, None]).astype(jnp.int32), axis=1)
    return idx4, tile_expert, tile_row, n_rows


def _sc_mesh():
    return plsc.VectorSubcoreMesh(core_axis_name="core", subcore_axis_name="subcore")


def _sc_scatter_rows(rows, idx4, n_out):
    n_in, w = rows.shape
    nb = idx4.shape[1] // SC_WINDOW

    @functools.partial(pl.kernel, out_type=jax.ShapeDtypeStruct((n_out, w), rows.dtype), mesh=_sc_mesh(),
                       scratch_types=[], name="sc_scatter_rows")
    def scatter(x_hbm, ia_hbm, ib_hbm, o_hbm):
        def body(x_vmem, ia_vmem, ib_vmem):
            pltpu.sync_copy(x_vmem, o_hbm.at[ia_vmem.at[0]])
            pltpu.sync_copy(x_vmem, o_hbm.at[ib_vmem.at[0]])

        pltpu.emit_pipeline(
            body,
            grid=(n_in // SC_WINDOW,),
            in_specs=[pl.BlockSpec((SC_WINDOW, w), lambda i: (i, 0)),
                      pl.BlockSpec((1, SC_WINDOW), lambda i: (2 * (i // nb), i % nb)),
                      pl.BlockSpec((1, SC_WINDOW), lambda i: (2 * (i // nb) + 1, i % nb))],
            out_specs=[],
            core_axis_name=("core", "subcore"),
            dimension_semantics=(pltpu.PARALLEL,),
        )(x_hbm, ia_hbm, ib_hbm)

    return scatter(rows, idx4, idx4)


def _sc_gather_rows(table, idx4):
    nb = idx4.shape[1] // SC_WINDOW
    n_idx = idx4.shape[0] * idx4.shape[1]
    w = table.shape[1]

    @functools.partial(pl.kernel, out_type=jax.ShapeDtypeStruct((n_idx, w), table.dtype), mesh=_sc_mesh(),
                       scratch_types=[], name="sc_gather_rows")
    def gather(t_hbm, i_hbm, o_hbm):
        def body(i_vmem, o_vmem):
            pltpu.sync_copy(t_hbm.at[i_vmem.at[0]], o_vmem)

        pltpu.emit_pipeline(
            body,
            grid=(n_idx // SC_WINDOW,),
            in_specs=[pl.BlockSpec((1, SC_WINDOW), lambda i: (i // nb, i % nb))],
            out_specs=[pl.BlockSpec((SC_WINDOW, w), lambda i: (i, 0))],
            core_axis_name=("core", "subcore"),
            dimension_semantics=(pltpu.PARALLEL,),
        )(i_hbm, o_hbm)

    return gather(table, idx4)


def _forward(x, g_mix, w_in, w_attn_branch, w_ret_branch, w_out, g_ffn, w_group_router, b_group_router,
             w_expert_router, b_expert_router, w_gate, w_up, w_down, g_final, interpret=False):
    batch, seq, d = x.shape
    n = batch * seq
    x2 = x.reshape(n, d)
    proj, qkv_d4, qkv_d16, wg_bf, wu_bf, wd_bf = _proj(x2, g_mix[0][None, :], w_in[0].astype(BF16), w_gate[0],
                                                        w_up[0], w_down[0], batch, seq, interpret)
    o_attn = _attention(proj, (qkv_d4, qkv_d16), batch, seq, interpret)
    o_ret = _retention(proj, batch, seq, interpret)
    pad = LANES - N_EXPERT_GROUPS - N_EXPERTS
    w_router = jnp.concatenate([w_group_router[0], w_expert_router[0], jnp.zeros((d, pad), F32)], axis=-1)
    w_router_hi = w_router.astype(BF16)
    w_router_lo = (w_router - w_router_hi.astype(F32)).astype(BF16)
    w_router2 = jnp.concatenate([w_router_hi, w_router_lo], axis=-1)
    b_router = jnp.concatenate([b_group_router[0], b_expert_router[0], jnp.zeros((pad,), F32)])[None, :]
    x1, h2p, route, route_t, counts = _mix(o_attn, o_ret, proj, x2, w_attn_branch[0].astype(BF16),
                                           w_ret_branch[0].astype(BF16), w_out[0].astype(BF16),
                                           g_ffn[0][None, :], w_router2, b_router, interpret)
    idx4, tile_expert, tile_row, n_rows = _route_plan(route_t, counts, n, interpret)
    xs = _sc_scatter_rows(h2p.reshape(2 * n, SC_ROW_WORDS), idx4, 2 * n_rows)
    ys = _experts(xs.reshape(2, n_rows, SC_ROW_WORDS), tile_expert, tile_row, wg_bf, wu_bf, wd_bf, interpret)
    yab = _sc_gather_rows(ys.reshape(2 * n_rows, SC_ROW_WORDS), idx4)
    out = _final(x1, yab.reshape(4, n, SC_ROW_WORDS), route, g_final[None, :], interpret)
    return out.reshape(batch, seq, d)


def kernel(x, g_mix, w_in, w_attn_branch, w_ret_branch, w_out, g_ffn, w_group_router, b_group_router,
           w_expert_router, b_expert_router, w_gate, w_up, w_down, g_final):
    return _forward(x, g_mix, w_in, w_attn_branch, w_ret_branch, w_out, g_ffn, w_group_router,
                    b_group_router, w_expert_router, b_expert_router, w_gate, w_up, w_down, g_final)
```

```python
import functools

import numpy as np
import jax
import jax.numpy as jnp
from jax import lax
from jax.experimental import pallas as pl
from jax.experimental.pallas import tpu as pltpu
from jax.experimental.pallas import tpu_sc as plsc

F32 = jnp.float32
BF16 = jnp.bfloat16

D_MODEL = 1024
ATTN_GROUPS = ((128, 1), (512, 4), (2048, 16))
N_GROUPS = len(ATTN_GROUPS)
ATTN_HEADS = 8
HEAD_DIM = 64
GROUP_W = ATTN_HEADS * HEAD_DIM
QKV_W = N_GROUPS * GROUP_W
RET_HEADS = 4
RET_DK = 128
RET_DV = 256
RET_CHUNK = 128
RET_TS = 1024
N_EXPERT_GROUPS = 4
EXPERTS_PER_GROUP = 8
N_EXPERTS = N_EXPERT_GROUPS * EXPERTS_PER_GROUP
EXPERT_FF = 512
EPS = 1e-6

LANES = 128
BLK = 128
SPAN = 2048
NEG = -1e30
ACC_PARTS = 3

COL_QA = 0
COL_KA = COL_QA + QKV_W
COL_VA = COL_KA + QKV_W
COL_QR = COL_VA + QKV_W
COL_KR = COL_QR + RET_HEADS * RET_DK
COL_VR = COL_KR + RET_HEADS * RET_DK
COL_GR = COL_VR + RET_HEADS * RET_DV
COL_GATE_A = COL_GR + RET_HEADS * RET_DV
COL_GATE_R = COL_GATE_A + D_MODEL
IN_WIDTH = COL_GATE_R + D_MODEL
COL_BLOCK = 256

PROJ_TM = 512
PROJ_TN = IN_WIDTH // 2
MXU_N = 256
MIX_TM = 1024
MIX_CHUNK = 512
ROUTE_CHUNK = 256
ROUTE_ROWS = 8
EXP_TM = 512
SC_WINDOW = 128
SC_ROW_WORDS = 256
FIN_TM = 1024
VMEM_LIMIT = 56 * 1024 * 1024


def _cparams(sem):
    return pltpu.CompilerParams(dimension_semantics=sem, vmem_limit_bytes=VMEM_LIMIT)


def _sigmoid(x):
    return 0.5 * jnp.tanh(0.5 * x) + 0.5


def _proj_plans():
    plan_a, nat_col = [], 0
    for c in range(PROJ_TN // MXU_N):
        col = c * MXU_N
        section, within = divmod(col, QKV_W)
        gi = within // GROUP_W
        if col < COL_QR and gi > 0:
            plan_a.append((None, (gi, section * GROUP_W + within % GROUP_W)))
        else:
            plan_a.append((nat_col, None))
            nat_col += MXU_N
    plan_b = [(c * MXU_N, None) for c in range((IN_WIDTH - PROJ_TN) // MXU_N)]
    return plan_a, nat_col, plan_b


def _proj_kernel(*refs, plan, n_cast):
    has_dilated = any(d is not None for _, d in plan)
    x_ref, g_ref, w_ref = refs[:3]
    cast_in = refs[3:3 + n_cast]
    o_ref = refs[3 + n_cast]
    rest = refs[4 + n_cast:]
    if has_dilated:
        d4_ref, d16_ref = rest[:2]
        stage_ref, mid_ref = rest[2 + n_cast:]
        rest = rest[2:]
    cast_out = rest[:n_cast]

    x = x_ref[...]
    ms = jnp.mean(x * x, axis=-1, keepdims=True)
    h = (x * lax.rsqrt(ms + EPS) * g_ref[...]).astype(BF16)
    step = ATTN_GROUPS[1][1]
    quarter = PROJ_TM // step
    for c, (nat_col, dil) in enumerate(plan):
        res = jnp.dot(h, w_ref[:, c * MXU_N:(c + 1) * MXU_N], preferred_element_type=F32)
        if nat_col is not None:
            o_ref[:, nat_col:nat_col + MXU_N] = res.astype(o_ref.dtype)
        if dil is None:
            continue
        gi, col = dil
        slot = c % 2
        for lt in range(MXU_N // LANES):
            stage_ref[slot, lt] = res[:, lt * LANES:(lt + 1) * LANES]
        for lt in range(MXU_N // LANES):
            cols = slice(col + lt * LANES, col + (lt + 1) * LANES)
            for b in range(step):
                rows = stage_ref[slot, lt, pl.ds(b, quarter, stride=step), :]
                if gi == 1:
                    d4_ref[0, b, :, cols] = rows.astype(BF16)
                else:
                    mid_ref[slot, lt, b * quarter:(b + 1) * quarter, :] = rows
            if gi == 2:
                for b in range(step):
                    for a in range(step):
                        rows = mid_ref[slot, lt, pl.ds(b * quarter + a, quarter // step, stride=step), :]
                        d16_ref[0, a * step + b, :, cols] = rows.astype(BF16)

    for src, dst in zip(cast_in, cast_out):
        dst[...] = src[...].astype(BF16)


def _proj(x2, g_mix, w_in_bf16, w_gate, w_up, w_down, batch, seq, interpret):
    n = x2.shape[0]
    n_i = n // PROJ_TM
    tiles = seq // PROJ_TM
    plan_a, width_a, plan_b = _proj_plans()
    dils = [ATTN_GROUPS[gi][1] for gi in (1, 2)]
    row = lambda i: (i, 0)

    def call(plan, col_block, nat_width, casts, name):
        flat = [w.reshape(-1, w.shape[-1]) for w in casts]
        w_specs = [pl.BlockSpec((w.shape[0] // n_i, w.shape[1]), row) for w in flat]
        dilated = any(d is not None for _, d in plan)
        d_specs = [pl.BlockSpec((1, d, PROJ_TM // d, 3 * GROUP_W), lambda i: (i // tiles, 0, i % tiles, 0))
                   for d in dils] if dilated else []
        d_shapes = [jax.ShapeDtypeStruct((batch, d, seq // d, 3 * GROUP_W), BF16) for d in dils] if dilated else []
        stage = pltpu.VMEM((2, MXU_N // LANES, PROJ_TM, LANES), F32)
        return pl.pallas_call(
            functools.partial(_proj_kernel, plan=plan, n_cast=len(casts)),
            grid=(n_i,),
            in_specs=[pl.BlockSpec((PROJ_TM, D_MODEL), row),
                      pl.BlockSpec((1, D_MODEL), lambda i: (0, 0)),
                      pl.BlockSpec((D_MODEL, PROJ_TN), lambda i: (0, col_block))] + w_specs,
            out_specs=[pl.BlockSpec((PROJ_TM, nat_width), row)] + d_specs + w_specs,
            out_shape=[jax.ShapeDtypeStruct((n, nat_width), BF16)] + d_shapes
            + [jax.ShapeDtypeStruct(w.shape, BF16) for w in flat],
            scratch_shapes=[stage, stage] if dilated else [],
            compiler_params=_cparams(("arbitrary",)),
            interpret=interpret,
            name=name,
        )(x2, g_mix, w_in_bf16, *flat)

    nat_a, qkv_d4, qkv_d16, wg_bf, wu_bf = call(plan_a, 0, width_a, (w_gate, w_up), "proj_attn")
    nat_b, wd_bf = call(plan_b, 1, IN_WIDTH - PROJ_TN, (w_down,), "proj_rest")
    return (nat_a, nat_b, (qkv_d4, qkv_d16), wg_bf.reshape(w_gate.shape), wu_bf.reshape(w_up.shape),
            wd_bf.reshape(w_down.shape))


def _attn_unit(q2, kk, vv, bias_a, bias_b):
    lane = lax.broadcasted_iota(jnp.int32, (BLK, LANES), 1)
    left = lane < HEAD_DIM
    zero = jnp.zeros_like(q2)
    nt = (((1,), (1,)), ((), ()))
    q_st = jnp.concatenate([jnp.where(left, q2, zero), jnp.where(left, zero, q2)], axis=0)
    s = lax.dot_general(q_st, kk, nt, preferred_element_type=F32) + jnp.concatenate([bias_a, bias_b], axis=0)
    m = jnp.max(s, axis=-1, keepdims=True)
    p = jnp.exp(s - m)
    den = jnp.sum(p, axis=-1, keepdims=True)
    o = jnp.dot(p.astype(BF16), vv, preferred_element_type=F32)
    return (jnp.where(left, o[:BLK], o[BLK:]), jnp.where(left, m[:BLK], m[BLK:]),
            jnp.where(left, den[:BLK], den[BLK:]))


def _attn_kernel(q1_ref, k1_ref, v1_ref, q2_ref, k2_ref, v2_ref, q3_ref, k3_ref, v3_ref,
                 bias_ref, o_ref, acc_ref, accw_ref, *, seq):
    s_id = pl.program_id(2)
    step = ATTN_GROUPS[1][1]
    quarter = SPAN // step
    dilated = {1: (q2_ref, k2_ref, v2_ref), 2: (q3_ref, k3_ref, v3_ref)}

    def dilated_unit(gi, d, m, r, first):
        slot = gi - 1
        q_ref, k_ref, v_ref = dilated[gi]
        loc = BLK * m * d + r
        cur = pl.multiple_of(s_id * (SPAN // d) + BLK * m, BLK)
        prev = pl.multiple_of(jnp.where(first == 1, cur, cur - BLK), BLK)
        q2 = q_ref[0, r, BLK * m:BLK * (m + 1), :] * 0.125
        kk = jnp.concatenate([k_ref[0, r, pl.ds(prev, BLK), :], k_ref[0, r, pl.ds(cur, BLK), :]], axis=0)
        vv = jnp.concatenate([v_ref[0, r, pl.ds(prev, BLK), :], v_ref[0, r, pl.ds(cur, BLK), :]], axis=0)
        parts = _attn_unit(q2, kk, vv, bias_ref[gi, first, 0], bias_ref[gi, first, 1])
        for j, part in enumerate(parts):
            if d == step:
                acc_ref[ACC_PARTS * slot + j, pl.ds(loc, BLK, stride=d), :] = part
            else:
                a, b = divmod(r, step)
                accw_ref[j, pl.ds(b * quarter + a, BLK, stride=step), :] = part

    def reinterleave_pieces():
        pieces = []
        for j in range(ACC_PARTS):
            for b in range(step):
                for j0 in range(0, quarter, 256):
                    def piece(j=j, b=b, j0=j0):
                        acc_ref[ACC_PARTS + j, pl.ds(b + step * j0, 256, stride=step), :] = (
                            accw_ref[j, b * quarter + j0:b * quarter + j0 + 256, :])
                    pieces.append(piece)
        return pieces

    def dense_unit(m):
        loc = m * BLK
        cur = pl.multiple_of(s_id * SPAN + loc, BLK)
        prev = pl.multiple_of(jnp.maximum(cur - BLK, 0), BLK)
        first = jnp.where(cur == 0, 1, 0)
        q2 = q1_ref[loc:loc + BLK, :] * 0.125
        kk = jnp.concatenate([k1_ref[pl.ds(prev, BLK), :], k1_ref[pl.ds(cur, BLK), :]], axis=0)
        vv = jnp.concatenate([v1_ref[pl.ds(prev, BLK), :], v1_ref[pl.ds(cur, BLK), :]], axis=0)
        n1, m1, d1 = _attn_unit(q2, kk, vv, bias_ref[0, first, 0], bias_ref[0, first, 1])
        (n2, m2, d2), (n3, m3, d3) = (
            tuple(acc_ref[g * ACC_PARTS + j, loc:loc + BLK, :] for j in range(ACC_PARTS)) for g in range(2))
        mx = jnp.maximum(jnp.maximum(m1, m2), m3)
        w1, w2, w3 = jnp.exp(m1 - mx), jnp.exp(m2 - mx), jnp.exp(m3 - mx)
        num = w1 * n1 + w2 * n2 + w3 * n3
        den = w1 * d1 + w2 * d2 + w3 * d3
        o_ref[loc:loc + BLK, :] = (num / den).astype(o_ref.dtype)

    first_span = jnp.where(s_id == 0, 1, 0)
    for gi in (2, 1):
        d = ATTN_GROUPS[gi][1]
        for m in range(SPAN // (BLK * d)):
            for r in range(d):
                dilated_unit(gi, d, m, r, first_span if m == 0 else 0)
        if d != step:
            for piece in reinterleave_pieces():
                piece()
    for m in range(SPAN // BLK):
        dense_unit(m)


def _attn_bias():
    slopes = np.exp2(-8.0 * np.arange(1, ATTN_HEADS + 1, dtype=np.float64) / ATTN_HEADS)
    qi = np.arange(BLK)[:, None]
    kj = np.arange(2 * BLK)[None, :]
    rel = qi + BLK - kj
    out = np.zeros((N_GROUPS, 2, ATTN_HEADS, BLK, 2 * BLK), np.float32)
    for gi, (window, d) in enumerate(ATTN_GROUPS):
        n_back = window // d
        assert n_back == BLK
        valid = (rel >= 0) & (rel <= n_back)
        bias = -slopes[:, None, None] * (rel * d)[None].astype(np.float64)
        out[gi, 0] = np.where(valid[None], bias, NEG)
        out[gi, 1] = np.where((valid & (kj >= BLK))[None], bias, NEG)
    return jnp.asarray(out)


def _attention(nat_a, dilated, batch, seq, interpret):
    n = batch * seq
    spans = seq // SPAN
    n_hp = GROUP_W // LANES
    heads = GROUP_W // LANES
    specs = [pl.BlockSpec((SPAN, LANES), lambda b, hp, s: (b * spans + s, hp)),
             pl.BlockSpec((seq, LANES), lambda b, hp, s: (b, heads + hp)),
             pl.BlockSpec((seq, LANES), lambda b, hp, s: (b, 2 * heads + hp))]
    operands = [nat_a, nat_a, nat_a]
    for gi, arr in zip((1, 2), dilated):
        d = ATTN_GROUPS[gi][1]
        specs += [pl.BlockSpec((1, d, SPAN // d, LANES), lambda b, hp, s: (b, 0, s, hp)),
                  pl.BlockSpec((1, d, seq // d, LANES), lambda b, hp, s: (b, 0, 0, heads + hp)),
                  pl.BlockSpec((1, d, seq // d, LANES), lambda b, hp, s: (b, 0, 0, 2 * heads + hp))]
        operands += [arr, arr, arr]
    bias_spec = pl.BlockSpec((N_GROUPS, 2, 2, BLK, 2 * BLK), lambda b, hp, s: (0, 0, hp, 0, 0))
    return pl.pallas_call(
        functools.partial(_attn_kernel, seq=seq),
        grid=(batch, n_hp, spans),
        in_specs=specs + [bias_spec],
        out_specs=pl.BlockSpec((SPAN, LANES), lambda b, hp, s: (b * spans + s, hp)),
        out_shape=jax.ShapeDtypeStruct((n, GROUP_W), BF16),
        scratch_shapes=[
            pltpu.VMEM((2 * ACC_PARTS, SPAN, LANES), F32),
            pltpu.VMEM((ACC_PARTS, SPAN, LANES), F32),
        ],
        compiler_params=_cparams(("arbitrary", "arbitrary", "arbitrary")),
        interpret=interpret,
        name="attn",
    )(*operands, _attn_bias())


def _ret_kernel(*refs):
    heads_per_qk = COL_BLOCK // RET_DK
    n_qk = RET_HEADS // heads_per_qk
    q_refs, k_refs = refs[:n_qk], refs[n_qk:2 * n_qk]
    v_refs = refs[2 * n_qk:2 * n_qk + RET_HEADS]
    g_refs = refs[2 * n_qk + RET_HEADS:2 * n_qk + 2 * RET_HEADS]
    dec_ref, xi_ref, zeta_ref, gch_ref, o_ref, st_ref = refs[2 * n_qk + 2 * RET_HEADS:]
    @pl.when(pl.program_id(1) == 0)
    def _():
        st_ref[...] = jnp.zeros_like(st_ref)

    nt = (((1,), (1,)), ((), ()))
    scale = RET_DK ** -0.5

    for c in range(RET_TS // RET_CHUNK):
        rows = pl.ds(c * RET_CHUNK, RET_CHUNK)
        for h in range(RET_HEADS):
            kcols = slice((h % heads_per_qk) * RET_DK, (h % heads_per_qk + 1) * RET_DK)
            vcols = slice(h * RET_DV, (h + 1) * RET_DV)
            qi = q_refs[h // heads_per_qk][rows, kcols]
            kf = k_refs[h // heads_per_qk][rows, kcols].astype(F32) * scale
            ki = kf.astype(BF16)
            kz_t = jnp.transpose(kf * zeta_ref[h]).astype(BF16)
            vi = v_refs[h][rows, :]
            att = lax.dot_general(qi, ki, nt, preferred_element_type=F32) * dec_ref[h]
            inner = jnp.dot(att.astype(BF16), vi, preferred_element_type=F32)
            st = st_ref[h]
            cross = jnp.dot(qi, st.astype(BF16), preferred_element_type=F32) * xi_ref[h]
            st_ref[h] = gch_ref[h] * st + jnp.dot(kz_t, vi, preferred_element_type=F32)
            y = inner + cross
            mu = jnp.mean(y, axis=-1, keepdims=True)
            yc = y - mu
            var = jnp.mean(yc * yc, axis=-1, keepdims=True)
            yn = yc * lax.rsqrt(var + EPS)
            g = g_refs[h][rows, :].astype(F32)
            o_ref[rows, vcols] = (g * _sigmoid(g) * yn).astype(o_ref.dtype)


def _ret_tables():
    c = RET_CHUNK
    log_g = np.log1p(-np.exp2(-5.0 - np.arange(RET_HEADS, dtype=np.float64)))
    pos = np.arange(c, dtype=np.float64)
    diff = pos[:, None] - pos[None, :]
    dec = np.where(diff >= 0, np.exp(log_g[:, None, None] * np.maximum(diff, 0.0)), 0.0)
    xi = np.exp(log_g[:, None] * (pos + 1.0))[..., None] * np.ones((1, 1, RET_DV))
    zeta = np.exp(log_g[:, None] * (c - 1.0 - pos))[..., None] * np.ones((1, 1, RET_DK))
    gch = np.exp(log_g * c)[:, None, None] * np.ones((1, 1, RET_DV))
    return tuple(jnp.asarray(t, F32) for t in (dec, xi, zeta, gch))


def _retention(nat_a, nat_b, batch, seq, interpret):
    n = batch * seq
    dec, xi, zeta, gch = _ret_tables()
    v_w = RET_HEADS * RET_DV
    nts = seq // RET_TS
    const3 = lambda b, t: (0, 0, 0)

    def block(arr, col):
        base = 0 if arr is nat_a else PROJ_TN
        if arr is nat_a:
            col = _proj_plans()[1] - (PROJ_TN - col)
        idx = (col - base) // COL_BLOCK
        return arr, pl.BlockSpec((RET_TS, COL_BLOCK), lambda b, t: (b * nts + t, idx))

    qk_blocks = RET_HEADS * RET_DK // COL_BLOCK
    picks = [block(nat_a if COL_QR + k * COL_BLOCK < PROJ_TN else nat_b, COL_QR + k * COL_BLOCK)
             for k in range(qk_blocks)]
    picks += [block(nat_b, COL_KR + k * COL_BLOCK) for k in range(qk_blocks)]
    picks += [block(nat_b, COL_VR + h * RET_DV) for h in range(RET_HEADS)]
    picks += [block(nat_b, COL_GR + h * RET_DV) for h in range(RET_HEADS)]
    return pl.pallas_call(
        _ret_kernel,
        grid=(batch, nts),
        in_specs=[spec for _, spec in picks] + [
            pl.BlockSpec((RET_HEADS, RET_CHUNK, RET_CHUNK), const3),
            pl.BlockSpec((RET_HEADS, RET_CHUNK, RET_DV), const3),
            pl.BlockSpec((RET_HEADS, RET_CHUNK, RET_DK), const3),
            pl.BlockSpec((RET_HEADS, 1, RET_DV), const3),
        ],
        out_specs=pl.BlockSpec((RET_TS, v_w), lambda b, t: (b * nts + t, 0)),
        out_shape=jax.ShapeDtypeStruct((n, v_w), BF16),
        scratch_shapes=[pltpu.VMEM((RET_HEADS, RET_DK, RET_DV), F32)],
        compiler_params=_cparams(("arbitrary", "arbitrary")),
        interpret=interpret,
        name="retention",
    )(*[arr for arr, _ in picks], dec, xi, zeta, gch)


ROUTER_OFF = N_EXPERT_GROUPS


def _pack_bf16_pair(a, b):
    hi = lax.bitcast_convert_type(a.astype(BF16).astype(F32), jnp.uint32)
    lo = lax.bitcast_convert_type(b.astype(BF16).astype(F32), jnp.uint32)
    return lax.bitcast_convert_type(hi | (lo >> 16), jnp.int32)


def _unpack_bf16_pair(w):
    u = lax.bitcast_convert_type(w, jnp.uint32)
    a = lax.bitcast_convert_type(u & jnp.uint32(0xFFFF0000), F32).astype(BF16)
    b = lax.bitcast_convert_type(u << 16, F32).astype(BF16)
    return a, b


def _pack_rows(y):
    q = D_MODEL // 4
    return (_pack_bf16_pair(y[:, 0:q], y[:, 2 * q:3 * q]), _pack_bf16_pair(y[:, q:2 * q], y[:, 3 * q:4 * q]))


def _unpack_rows(slab0, slab1):
    q0, q2 = _unpack_bf16_pair(slab0)
    q1, q3 = _unpack_bf16_pair(slab1)
    return jnp.concatenate([q0, q1, q2, q3], axis=1)


def _mix_kernel(oa_ref, or_ref, *refs):
    n_gate = D_MODEL // COL_BLOCK
    ga_refs, gr_refs = refs[:n_gate], refs[n_gate:2 * n_gate]
    (x_ref, pa_ref, pr_ref, wo_ref, gf_ref, wr_ref, br_ref,
     x1_ref, h2_ref, route_ref, route_t_ref, cnt_ref, carry_ref, logit_ref) = refs[2 * n_gate:]
    step = pl.program_id(0)

    @pl.when(step == 0)
    def _():
        carry_ref[...] = jnp.zeros_like(carry_ref)
        logit_ref[...] = jnp.zeros_like(logit_ref)

    routing = iter([functools.partial(_route_rows, pl.ds(c * ROUTE_CHUNK, ROUTE_CHUNK), step > 0, logit_ref,
                                      route_ref, route_t_ref, cnt_ref, carry_ref)
                    for c in range(MIX_TM // ROUTE_CHUNK)])
    for c in range(MIX_TM // MIX_CHUNK):
        for _ in _mix_rows(pl.ds(c * MIX_CHUNK, MIX_CHUNK), oa_ref, or_ref, ga_refs, gr_refs,
                           x_ref, pa_ref, pr_ref, wo_ref, gf_ref, wr_ref, br_ref, x1_ref, h2_ref, logit_ref):
            next(routing, lambda: None)()
    for piece in routing:
        piece()


def _mix_rows(rows, oa_ref, or_ref, ga_refs, gr_refs, x_ref, pa_ref, pr_ref, wo_ref, gf_ref, wr_ref, br_ref,
              x1_ref, h2_ref, logit_ref):
    a = jnp.dot(oa_ref[rows, :], pa_ref[...], preferred_element_type=F32)
    yield
    r = jnp.dot(or_ref[rows, :], pr_ref[...], preferred_element_type=F32)
    yield
    merged = jnp.concatenate(
        [_sigmoid(ga[rows, :].astype(F32)) * a[:, k * COL_BLOCK:(k + 1) * COL_BLOCK]
         + _sigmoid(gr[rows, :].astype(F32)) * r[:, k * COL_BLOCK:(k + 1) * COL_BLOCK]
         for k, (ga, gr) in enumerate(zip(ga_refs, gr_refs))], axis=1)
    x1 = x_ref[rows, :] + jnp.dot(merged.astype(BF16), wo_ref[...], preferred_element_type=F32)
    x1_ref[rows, :] = x1
    ms = jnp.mean(x1 * x1, axis=-1, keepdims=True)
    h2 = x1 * lax.rsqrt(ms + EPS) * gf_ref[...]
    h2_ref[0, rows, :], h2_ref[1, rows, :] = _pack_rows(h2)

    h_hi = h2.astype(BF16)
    h_lo = (h2 - h_hi.astype(F32)).astype(BF16)
    both = jnp.dot(h_hi, wr_ref[...], preferred_element_type=F32)
    logit_ref[rows, :] = (both[:, :LANES] + both[:, LANES:]
                          + jnp.dot(h_lo, wr_ref[:, :LANES], preferred_element_type=F32) + br_ref[...])


def _route_rows(rows, live, logit_ref, route_ref, route_t_ref, cnt_ref, carry_ref):
    logits = logit_ref[rows, :]
    tm = logits.shape[0]
    lane = lax.broadcasted_iota(jnp.int32, (tm, LANES), 1).astype(F32)
    big = jnp.float32(4 * LANES)
    ninf = -jnp.inf
    is_g = lane < N_EXPERT_GROUPS
    gl = jnp.where(is_g, logits, ninf)
    gmax = jnp.max(gl, axis=-1, keepdims=True)
    gsum = jnp.sum(jnp.where(is_g, jnp.exp(gl - gmax), 0.0), axis=-1, keepdims=True)
    g_val = 1.0 / gsum
    g_idx = jnp.min(jnp.where(jnp.logical_and(is_g, gl == gmax), lane, big), axis=-1, keepdims=True)
    lo = ROUTER_OFF + EXPERTS_PER_GROUP * g_idx
    in_grp = jnp.logical_and(lane >= lo, lane < lo + EXPERTS_PER_GROUP)
    el = jnp.where(in_grp, logits, ninf)
    v1 = jnp.max(el, axis=-1, keepdims=True)
    i1 = jnp.min(jnp.where(jnp.logical_and(in_grp, el == v1), lane, big), axis=-1, keepdims=True)
    rest = jnp.logical_and(in_grp, lane != i1)
    el2 = jnp.where(rest, logits, ninf)
    v2 = jnp.max(el2, axis=-1, keepdims=True)
    i2 = jnp.min(jnp.where(jnp.logical_and(rest, el2 == v2), lane, big), axis=-1, keepdims=True)
    t = jnp.exp(v2 - v1)
    w1 = g_val / (1.0 + t)
    w2 = g_val * t / (1.0 + t)

    sel = jnp.logical_or(lane == i1, lane == i2)
    sel_bf = jnp.where(sel, 1.0, 0.0).astype(BF16)
    row = lax.broadcasted_iota(jnp.int32, (tm, tm), 0)
    col = lax.broadcasted_iota(jnp.int32, (tm, tm), 1)
    tri = jnp.where(col < row, 1.0, 0.0).astype(BF16)
    before = jnp.dot(tri, sel_bf, preferred_element_type=F32) + carry_ref[...]
    r1 = jnp.sum(jnp.where(lane == i1, before, 0.0), axis=-1, keepdims=True)
    r2 = jnp.sum(jnp.where(lane == i2, before, 0.0), axis=-1, keepdims=True)
    carry = carry_ref[...] + jnp.where(live, jnp.sum(jnp.where(sel, 1.0, 0.0), axis=0, keepdims=True), 0.0)
    carry_ref[...] = carry
    cnt_ref[...] = carry

    vals = (i1 - ROUTER_OFF, i2 - ROUTER_OFF, w1, w2, r1, r2)
    route = jnp.zeros((tm, LANES), F32)
    for j, v in enumerate(vals):
        route = jnp.where(lane == j, v, route)
    route_ref[rows, :] = route
    route_t_ref[:, rows] = jnp.transpose(route)[:ROUTE_ROWS, :]


def _mix(o_attn, o_ret, nat_b, x2, pa, pr, wo, g_ffn, w_router, b_router, interpret):
    n = x2.shape[0]
    tm = MIX_TM
    last = n // tm - 1
    n_gate = D_MODEL // COL_BLOCK
    const = lambda i: (0, 0)
    cur = lambda i: jnp.minimum(i, last)
    prev = lambda i: jnp.maximum(i - 1, 0)
    return pl.pallas_call(
        _mix_kernel,
        grid=(n // tm + 1,),
        in_specs=[
            pl.BlockSpec((tm, GROUP_W), lambda i: (cur(i), 0)),
            pl.BlockSpec((tm, D_MODEL), lambda i: (cur(i), 0)),
        ] + [
            pl.BlockSpec((tm, COL_BLOCK), functools.partial(lambda i, idx: (cur(i), idx), idx=(col - PROJ_TN) // COL_BLOCK + k))
            for col in (COL_GATE_A, COL_GATE_R) for k in range(n_gate)
        ] + [
            pl.BlockSpec((tm, D_MODEL), lambda i: (cur(i), 0)),
            pl.BlockSpec((GROUP_W, D_MODEL), const),
            pl.BlockSpec((D_MODEL, D_MODEL), const),
            pl.BlockSpec((D_MODEL, D_MODEL), const),
            pl.BlockSpec((1, D_MODEL), const),
            pl.BlockSpec((D_MODEL, 2 * LANES), const),
            pl.BlockSpec((1, LANES), const),
        ],
        out_specs=[
            pl.BlockSpec((tm, D_MODEL), lambda i: (cur(i), 0)),
            pl.BlockSpec((2, tm, SC_ROW_WORDS), lambda i: (0, cur(i), 0)),
            pl.BlockSpec((tm, LANES), lambda i: (prev(i), 0)),
            pl.BlockSpec((ROUTE_ROWS, tm), lambda i: (0, prev(i))),
            pl.BlockSpec((1, LANES), const),
        ],
        out_shape=[
            jax.ShapeDtypeStruct((n, D_MODEL), F32),
            jax.ShapeDtypeStruct((2, n, SC_ROW_WORDS), jnp.int32),
            jax.ShapeDtypeStruct((n, LANES), F32),
            jax.ShapeDtypeStruct((ROUTE_ROWS, n), F32),
            jax.ShapeDtypeStruct((1, LANES), F32),
        ],
        scratch_shapes=[pltpu.VMEM((1, LANES), F32), pltpu.VMEM((tm, LANES), F32)],
        compiler_params=_cparams(("arbitrary",)),
        interpret=interpret,
        name="mix_router",
    )(o_attn, o_ret, *([nat_b] * (2 * n_gate)), x2, pa, pr, wo, g_ffn, w_router, b_router)


def _expert_kernel(te_ref, tr_ref, xs_ref, wg_ref, wu_ref, wd_ref, o_ref):
    i = pl.program_id(0)

    @pl.when(tr_ref[i] == i)
    def _():
        xs = _unpack_rows(xs_ref[0], xs_ref[1])
        a = jnp.dot(xs, wg_ref[0], preferred_element_type=F32)
        u = jnp.dot(xs, wu_ref[0], preferred_element_type=F32)
        hid = (a * _sigmoid(a) * u).astype(BF16)
        y = jnp.dot(hid, wd_ref[0], preferred_element_type=F32)
        o_ref[0], o_ref[1] = _pack_rows(y)


def _experts(xs, tile_expert, tile_row, w_gate, w_up, w_down, interpret):
    p = xs.shape[1]
    n_tiles = p // EXP_TM
    grid_spec = pltpu.PrefetchScalarGridSpec(
        num_scalar_prefetch=2,
        grid=(n_tiles,),
        in_specs=[
            pl.BlockSpec((2, EXP_TM, SC_ROW_WORDS), lambda i, te, tr: (0, tr[i], 0)),
            pl.BlockSpec((1, D_MODEL, EXPERT_FF), lambda i, te, tr: (te[i], 0, 0)),
            pl.BlockSpec((1, D_MODEL, EXPERT_FF), lambda i, te, tr: (te[i], 0, 0)),
            pl.BlockSpec((1, EXPERT_FF, D_MODEL), lambda i, te, tr: (te[i], 0, 0)),
        ],
        out_specs=pl.BlockSpec((2, EXP_TM, SC_ROW_WORDS), lambda i, te, tr: (0, tr[i], 0)),
    )
    return pl.pallas_call(
        _expert_kernel,
        grid_spec=grid_spec,
        out_shape=jax.ShapeDtypeStruct((2, p, SC_ROW_WORDS), jnp.int32),
        compiler_params=_cparams(("arbitrary",)),
        interpret=interpret,
        name="experts",
    )(tile_expert, tile_row, xs, w_gate, w_up, w_down)


def _final_kernel(x1_ref, yab_ref, route_ref, g_ref, o_ref):
    route = route_ref[...]
    w1 = route[:, 2:3]
    w2 = route[:, 3:4]
    ya = _unpack_rows(yab_ref[0], yab_ref[2]).astype(F32)
    yb = _unpack_rows(yab_ref[1], yab_ref[3]).astype(F32)
    x2 = x1_ref[...] + w1 * ya + w2 * yb
    ms = jnp.mean(x2 * x2, axis=-1, keepdims=True)
    o_ref[...] = x2 * lax.rsqrt(ms + EPS) * g_ref[...]


def _final(x1, yab, route, g_final, interpret):
    n = x1.shape[0]
    tm = FIN_TM
    row = lambda i: (i, 0)
    return pl.pallas_call(
        _final_kernel,
        grid=(n // tm,),
        in_specs=[
            pl.BlockSpec((tm, D_MODEL), row),
            pl.BlockSpec((4, tm, SC_ROW_WORDS), lambda i: (0, i, 0)),
            pl.BlockSpec((tm, LANES), row),
            pl.BlockSpec((1, D_MODEL), lambda i: (0, 0)),
        ],
        out_specs=pl.BlockSpec((tm, D_MODEL), row),
        out_shape=jax.ShapeDtypeStruct((n, D_MODEL), F32),
        compiler_params=_cparams(("arbitrary",)),
        interpret=interpret,
        name="combine_final",
    )(x1, yab, route, g_final)


def _dest_kernel(offs_ref, route_t_ref, idx_ref, *, n_rows):
    route_t = route_t_ref[...]
    experts = route_t[0:2, :]
    dest = route_t[4:6, :].astype(jnp.int32)
    for e in range(N_EXPERTS):
        dest = dest + jnp.where(experts == float(e), offs_ref[e], 0)
    idx_ref[0:2, :] = dest
    idx_ref[2:4, :] = dest + n_rows


def _route_plan(route_t, counts, n, interpret):
    cnt = counts[0, ROUTER_OFF:ROUTER_OFF + N_EXPERTS].astype(jnp.int32)
    padded = ((cnt + EXP_TM - 1) // EXP_TM) * EXP_TM
    ends = jnp.cumsum(padded)
    offs = ends - padded
    n_rows = 2 * n + N_EXPERTS * EXP_TM
    idx4 = pl.pallas_call(
        functools.partial(_dest_kernel, n_rows=n_rows),
        grid_spec=pltpu.PrefetchScalarGridSpec(
            num_scalar_prefetch=1, grid=(1,),
            in_specs=[pl.BlockSpec(route_t.shape, lambda i, offs: (0, 0))],
            out_specs=pl.BlockSpec((4, n), lambda i, offs: (0, 0))),
        out_shape=jax.ShapeDtypeStruct((4, n), jnp.int32),
        interpret=interpret,
        name="route_dest",
    )(offs, route_t)
    tile_row = jnp.minimum(jnp.arange(n_rows // EXP_TM, dtype=jnp.int32), ends[-1] // EXP_TM - 1)
    tile_expert = jnp.sum((ends[None, :] <= (tile_row * EXP_TM)[:, None]).astype(jnp.int32), axis=1)
    return idx4, tile_expert, tile_row, n_rows


def _sc_mesh():
    return plsc.VectorSubcoreMesh(core_axis_name="core", subcore_axis_name="subcore")


def _sc_scatter_rows(rows, idx4, n_out):
    n_in, w = rows.shape
    nb = idx4.shape[1] // SC_WINDOW

    @functools.partial(pl.kernel, out_type=jax.ShapeDtypeStruct((n_out, w), rows.dtype), mesh=_sc_mesh(),
                       scratch_types=[], name="sc_scatter_rows")
    def scatter(x_hbm, ia_hbm, ib_hbm, o_hbm):
        def body(x_vmem, ia_vmem, ib_vmem):
            pltpu.sync_copy(x_vmem, o_hbm.at[ia_vmem.at[0]])
            pltpu.sync_copy(x_vmem, o_hbm.at[ib_vmem.at[0]])

        pltpu.emit_pipeline(
            body,
            grid=(n_in // SC_WINDOW,),
            in_specs=[pl.BlockSpec((SC_WINDOW, w), lambda i: (i, 0)),
                      pl.BlockSpec((1, SC_WINDOW), lambda i: (2 * (i // nb), i % nb)),
                      pl.BlockSpec((1, SC_WINDOW), lambda i: (2 * (i // nb) + 1, i % nb))],
            out_specs=[],
            core_axis_name=("core", "subcore"),
            dimension_semantics=(pltpu.PARALLEL,),
        )(x_hbm, ia_hbm, ib_hbm)

    return scatter(rows, idx4, idx4)


def _sc_gather_rows(table, idx4):
    nb = idx4.shape[1] // SC_WINDOW
    n_idx = idx4.shape[0] * idx4.shape[1]
    w = table.shape[1]

    @functools.partial(pl.kernel, out_type=jax.ShapeDtypeStruct((n_idx, w), table.dtype), mesh=_sc_mesh(),
                       scratch_types=[], name="sc_gather_rows")
    def gather(t_hbm, i_hbm, o_hbm):
        def body(i_vmem, o_vmem):
            pltpu.sync_copy(t_hbm.at[i_vmem.at[0]], o_vmem)

        pltpu.emit_pipeline(
            body,
            grid=(n_idx // SC_WINDOW,),
            in_specs=[pl.BlockSpec((1, SC_WINDOW), lambda i: (i // nb, i % nb))],
            out_specs=[pl.BlockSpec((SC_WINDOW, w), lambda i: (i, 0))],
            core_axis_name=("core", "subcore"),
            dimension_semantics=(pltpu.PARALLEL,),
        )(i_hbm, o_hbm)

    return gather(table, idx4)


def _forward(x, g_mix, w_in, w_attn_branch, w_ret_branch, w_out, g_ffn, w_group_router, b_group_router,
             w_expert_router, b_expert_router, w_gate, w_up, w_down, g_final, interpret=False):
    batch, seq, d = x.shape
    n = batch * seq
    x2 = x.reshape(n, d)
    nat_a, nat_b, qkv_dilated, wg_bf, wu_bf, wd_bf = _proj(x2, g_mix[0][None, :], w_in[0].astype(BF16), w_gate[0],
                                                           w_up[0], w_down[0], batch, seq, interpret)
    o_attn = _attention(nat_a, qkv_dilated, batch, seq, interpret)
    o_ret = _retention(nat_a, nat_b, batch, seq, interpret)
    pad = LANES - N_EXPERT_GROUPS - N_EXPERTS
    w_router = jnp.concatenate([w_group_router[0], w_expert_router[0], jnp.zeros((d, pad), F32)], axis=-1)
    w_router_hi = w_router.astype(BF16)
    w_router_lo = (w_router - w_router_hi.astype(F32)).astype(BF16)
    w_router2 = jnp.concatenate([w_router_hi, w_router_lo], axis=-1)
    b_router = jnp.concatenate([b_group_router[0], b_expert_router[0], jnp.zeros((pad,), F32)])[None, :]
    x1, h2p, route, route_t, counts = _mix(o_attn, o_ret, nat_b, x2, w_attn_branch[0].astype(BF16),
                                           w_ret_branch[0].astype(BF16), w_out[0].astype(BF16),
                                           g_ffn[0][None, :], w_router2, b_router, interpret)
    idx4, tile_expert, tile_row, n_rows = _route_plan(route_t, counts, n, interpret)
    xs = _sc_scatter_rows(h2p.reshape(2 * n, SC_ROW_WORDS), idx4, 2 * n_rows)
    ys = _experts(xs.reshape(2, n_rows, SC_ROW_WORDS), tile_expert, tile_row, wg_bf, wu_bf, wd_bf, interpret)
    yab = _sc_gather_rows(ys.reshape(2 * n_rows, SC_ROW_WORDS), idx4)
    out = _final(x1, yab.reshape(4, n, SC_ROW_WORDS), route, g_final[None, :], interpret)
    return out.reshape(batch, seq, d)


def kernel(x, g_mix, w_in, w_attn_branch, w_ret_branch, w_out, g_ffn, w_group_router, b_group_router,
           w_expert_router, b_expert_router, w_gate, w_up, w_down, g_final):
    return _forward(x, g_mix, w_in, w_attn_branch, w_ret_branch, w_out, g_ffn, w_group_router,
                    b_group_router, w_expert_router, b_expert_router, w_gate, w_up, w_down, g_final)
```

```python
import functools

import numpy as np
import jax
import jax.numpy as jnp
from jax import lax
from jax.experimental import pallas as pl
from jax.experimental.pallas import tpu as pltpu
from jax.experimental.pallas import tpu_sc as plsc

F32 = jnp.float32
BF16 = jnp.bfloat16

D_MODEL = 1024
ATTN_GROUPS = ((128, 1), (512, 4), (2048, 16))
N_GROUPS = len(ATTN_GROUPS)
ATTN_HEADS = 8
HEAD_DIM = 64
GROUP_W = ATTN_HEADS * HEAD_DIM
QKV_W = N_GROUPS * GROUP_W
RET_HEADS = 4
RET_DK = 128
RET_DV = 256
RET_CHUNK = 128
RET_TS = 1024
N_EXPERT_GROUPS = 4
EXPERTS_PER_GROUP = 8
N_EXPERTS = N_EXPERT_GROUPS * EXPERTS_PER_GROUP
EXPERT_FF = 512
EPS = 1e-6

LANES = 128
BLK = 128
SPAN = 2048
NEG = -1e30
ACC_PARTS = 3

COL_QA = 0
COL_KA = COL_QA + QKV_W
COL_VA = COL_KA + QKV_W
COL_QR = COL_VA + QKV_W
COL_KR = COL_QR + RET_HEADS * RET_DK
COL_VR = COL_KR + RET_HEADS * RET_DK
COL_GR = COL_VR + RET_HEADS * RET_DV
COL_GATE_A = COL_GR + RET_HEADS * RET_DV
COL_GATE_R = COL_GATE_A + D_MODEL
IN_WIDTH = COL_GATE_R + D_MODEL
COL_BLOCK = 256

PROJ_TM = 512
PROJ_TN = IN_WIDTH // 2
MXU_N = 256
MIX_TM = 1024
MIX_CHUNK = 512
ROUTE_CHUNK = 256
ROUTE_ROWS = 8
EXP_TM = 512
SC_WINDOW = 128
SC_ROW_WORDS = 256
FIN_TM = 1024
VMEM_LIMIT = 56 * 1024 * 1024


def _cparams(sem):
    return pltpu.CompilerParams(dimension_semantics=sem, vmem_limit_bytes=VMEM_LIMIT)


def _sigmoid(x):
    return 0.5 * jnp.tanh(0.5 * x) + 0.5


def _proj_plans():
    plan_a, nat_col = [], 0
    for c in range(PROJ_TN // MXU_N):
        col = c * MXU_N
        section, within = divmod(col, QKV_W)
        gi = within // GROUP_W
        if col < COL_QR and gi > 0:
            plan_a.append((None, (gi, section * GROUP_W + within % GROUP_W)))
        else:
            plan_a.append((nat_col, None))
            nat_col += MXU_N
    plan_b = [(c * MXU_N, None) for c in range((IN_WIDTH - PROJ_TN) // MXU_N)]
    return plan_a, nat_col, plan_b


def _proj_kernel(*refs, plan, n_cast):
    has_dilated = any(d is not None for _, d in plan)
    x_ref, g_ref, w_ref = refs[:3]
    cast_in = refs[3:3 + n_cast]
    o_ref = refs[3 + n_cast]
    rest = refs[4 + n_cast:]
    if has_dilated:
        d4_ref, d16_ref = rest[:2]
        stage_ref, mid_ref = rest[2 + n_cast:]
        rest = rest[2:]
    cast_out = rest[:n_cast]

    x = x_ref[...]
    ms = jnp.mean(x * x, axis=-1, keepdims=True)
    h = (x * lax.rsqrt(ms + EPS) * g_ref[...]).astype(BF16)
    step = ATTN_GROUPS[1][1]
    quarter = PROJ_TM // step
    for c, (nat_col, dil) in enumerate(plan):
        res = jnp.dot(h, w_ref[:, c * MXU_N:(c + 1) * MXU_N], preferred_element_type=F32)
        if nat_col is not None:
            o_ref[:, nat_col:nat_col + MXU_N] = res.astype(o_ref.dtype)
        if dil is None:
            continue
        gi, col = dil
        slot = c % 2
        for lt in range(MXU_N // LANES):
            stage_ref[slot, lt] = res[:, lt * LANES:(lt + 1) * LANES]
        for lt in range(MXU_N // LANES):
            cols = slice(col + lt * LANES, col + (lt + 1) * LANES)
            for b in range(step):
                rows = stage_ref[slot, lt, pl.ds(b, quarter, stride=step), :]
                if gi == 1:
                    d4_ref[0, b, :, cols] = rows.astype(BF16)
                else:
                    mid_ref[slot, lt, b * quarter:(b + 1) * quarter, :] = rows
            if gi == 2:
                for b in range(step):
                    for a in range(step):
                        rows = mid_ref[slot, lt, pl.ds(b * quarter + a, quarter // step, stride=step), :]
                        d16_ref[0, a * step + b, :, cols] = rows.astype(BF16)

    for src, dst in zip(cast_in, cast_out):
        dst[...] = src[...].astype(BF16)


def _proj(x2, g_mix, w_in_bf16, w_down, batch, seq, interpret):
    n = x2.shape[0]
    n_i = n // PROJ_TM
    tiles = seq // PROJ_TM
    plan_a, width_a, plan_b = _proj_plans()
    dils = [ATTN_GROUPS[gi][1] for gi in (1, 2)]
    row = lambda i: (i, 0)

    def call(plan, col_block, nat_width, casts, name):
        flat = [w.reshape(-1, w.shape[-1]) for w in casts]
        w_specs = [pl.BlockSpec((w.shape[0] // n_i, w.shape[1]), row) for w in flat]
        dilated = any(d is not None for _, d in plan)
        d_specs = [pl.BlockSpec((1, d, PROJ_TM // d, 3 * GROUP_W), lambda i: (i // tiles, 0, i % tiles, 0))
                   for d in dils] if dilated else []
        d_shapes = [jax.ShapeDtypeStruct((batch, d, seq // d, 3 * GROUP_W), BF16) for d in dils] if dilated else []
        stage = pltpu.VMEM((2, MXU_N // LANES, PROJ_TM, LANES), F32)
        return pl.pallas_call(
            functools.partial(_proj_kernel, plan=plan, n_cast=len(casts)),
            grid=(n_i,),
            in_specs=[pl.BlockSpec((PROJ_TM, D_MODEL), row),
                      pl.BlockSpec((1, D_MODEL), lambda i: (0, 0)),
                      pl.BlockSpec((D_MODEL, PROJ_TN), lambda i: (0, col_block))] + w_specs,
            out_specs=[pl.BlockSpec((PROJ_TM, nat_width), row)] + d_specs + w_specs,
            out_shape=[jax.ShapeDtypeStruct((n, nat_width), BF16)] + d_shapes
            + [jax.ShapeDtypeStruct(w.shape, BF16) for w in flat],
            scratch_shapes=[stage, stage] if dilated else [],
            compiler_params=_cparams(("arbitrary",)),
            interpret=interpret,
            name=name,
        )(x2, g_mix, w_in_bf16, *flat)

    nat_a, qkv_d4, qkv_d16 = call(plan_a, 0, width_a, (), "proj_attn")
    nat_b, wd_bf = call(plan_b, 1, IN_WIDTH - PROJ_TN, (w_down,), "proj_rest")
    return nat_a, nat_b, (qkv_d4, qkv_d16), wd_bf.reshape(w_down.shape)


def _attn_unit(q2, kk, vv, bias_a, bias_b):
    lane = lax.broadcasted_iota(jnp.int32, (BLK, LANES), 1)
    left = lane < HEAD_DIM
    zero = jnp.zeros_like(q2)
    nt = (((1,), (1,)), ((), ()))
    q_st = jnp.concatenate([jnp.where(left, q2, zero), jnp.where(left, zero, q2)], axis=0)
    s = lax.dot_general(q_st, kk, nt, preferred_element_type=F32) + jnp.concatenate([bias_a, bias_b], axis=0)
    m = jnp.max(s, axis=-1, keepdims=True)
    p = jnp.exp(s - m)
    den = jnp.sum(p, axis=-1, keepdims=True)
    o = jnp.dot(p.astype(BF16), vv, preferred_element_type=F32)
    return (jnp.where(left, o[:BLK], o[BLK:]), jnp.where(left, m[:BLK], m[BLK:]),
            jnp.where(left, den[:BLK], den[BLK:]))


def _attn_kernel(q1_ref, k1_ref, v1_ref, q2_ref, k2_ref, v2_ref, q3_ref, k3_ref, v3_ref,
                 bias_ref, wa_ref, wb_ref, o_ref, wa_o, wb_o, acc_ref, accw_ref, *, seq):
    s_id = pl.program_id(2)
    step = ATTN_GROUPS[1][1]
    quarter = SPAN // step
    dilated = {1: (q2_ref, k2_ref, v2_ref), 2: (q3_ref, k3_ref, v3_ref)}

    def dilated_unit(gi, d, m, r, first):
        slot = gi - 1
        q_ref, k_ref, v_ref = dilated[gi]
        loc = BLK * m * d + r
        cur = pl.multiple_of(s_id * (SPAN // d) + BLK * m, BLK)
        prev = pl.multiple_of(jnp.where(first == 1, cur, cur - BLK), BLK)
        q2 = q_ref[0, r, BLK * m:BLK * (m + 1), :] * 0.125
        kk = jnp.concatenate([k_ref[0, r, pl.ds(prev, BLK), :], k_ref[0, r, pl.ds(cur, BLK), :]], axis=0)
        vv = jnp.concatenate([v_ref[0, r, pl.ds(prev, BLK), :], v_ref[0, r, pl.ds(cur, BLK), :]], axis=0)
        parts = _attn_unit(q2, kk, vv, bias_ref[gi, first, 0], bias_ref[gi, first, 1])
        for j, part in enumerate(parts):
            if d == step:
                acc_ref[ACC_PARTS * slot + j, pl.ds(loc, BLK, stride=d), :] = part
            else:
                a, b = divmod(r, step)
                accw_ref[j, pl.ds(b * quarter + a, BLK, stride=step), :] = part

    def reinterleave_pieces():
        pieces = []
        for j in range(ACC_PARTS):
            for b in range(step):
                for j0 in range(0, quarter, 256):
                    def piece(j=j, b=b, j0=j0):
                        acc_ref[ACC_PARTS + j, pl.ds(b + step * j0, 256, stride=step), :] = (
                            accw_ref[j, b * quarter + j0:b * quarter + j0 + 256, :])
                    pieces.append(piece)
        return pieces

    def dense_unit(m):
        loc = m * BLK
        cur = pl.multiple_of(s_id * SPAN + loc, BLK)
        prev = pl.multiple_of(jnp.maximum(cur - BLK, 0), BLK)
        first = jnp.where(cur == 0, 1, 0)
        q2 = q1_ref[loc:loc + BLK, :] * 0.125
        kk = jnp.concatenate([k1_ref[pl.ds(prev, BLK), :], k1_ref[pl.ds(cur, BLK), :]], axis=0)
        vv = jnp.concatenate([v1_ref[pl.ds(prev, BLK), :], v1_ref[pl.ds(cur, BLK), :]], axis=0)
        n1, m1, d1 = _attn_unit(q2, kk, vv, bias_ref[0, first, 0], bias_ref[0, first, 1])
        (n2, m2, d2), (n3, m3, d3) = (
            tuple(acc_ref[g * ACC_PARTS + j, loc:loc + BLK, :] for j in range(ACC_PARTS)) for g in range(2))
        mx = jnp.maximum(jnp.maximum(m1, m2), m3)
        w1, w2, w3 = jnp.exp(m1 - mx), jnp.exp(m2 - mx), jnp.exp(m3 - mx)
        num = w1 * n1 + w2 * n2 + w3 * n3
        den = w1 * d1 + w2 * d2 + w3 * d3
        o_ref[loc:loc + BLK, :] = (num / den).astype(o_ref.dtype)

    first_span = jnp.where(s_id == 0, 1, 0)
    for gi in (2, 1):
        d = ATTN_GROUPS[gi][1]
        for m in range(SPAN // (BLK * d)):
            for r in range(d):
                dilated_unit(gi, d, m, r, first_span if m == 0 else 0)
        if d != step:
            for piece in reinterleave_pieces():
                piece()
    for m in range(SPAN // BLK):
        dense_unit(m)
    wa_o[...] = wa_ref[...].astype(BF16)
    wb_o[...] = wb_ref[...].astype(BF16)


def _attn_bias():
    slopes = np.exp2(-8.0 * np.arange(1, ATTN_HEADS + 1, dtype=np.float64) / ATTN_HEADS)
    qi = np.arange(BLK)[:, None]
    kj = np.arange(2 * BLK)[None, :]
    rel = qi + BLK - kj
    out = np.zeros((N_GROUPS, 2, ATTN_HEADS, BLK, 2 * BLK), np.float32)
    for gi, (window, d) in enumerate(ATTN_GROUPS):
        n_back = window // d
        assert n_back == BLK
        valid = (rel >= 0) & (rel <= n_back)
        bias = -slopes[:, None, None] * (rel * d)[None].astype(np.float64)
        out[gi, 0] = np.where(valid[None], bias, NEG)
        out[gi, 1] = np.where((valid & (kj >= BLK))[None], bias, NEG)
    return jnp.asarray(out)


def _attention(nat_a, dilated, casts, batch, seq, interpret):
    n = batch * seq
    spans = seq // SPAN
    n_hp = GROUP_W // LANES
    heads = GROUP_W // LANES
    specs = [pl.BlockSpec((SPAN, LANES), lambda b, hp, s: (b * spans + s, hp)),
             pl.BlockSpec((seq, LANES), lambda b, hp, s: (b, heads + hp)),
             pl.BlockSpec((seq, LANES), lambda b, hp, s: (b, 2 * heads + hp))]
    operands = [nat_a, nat_a, nat_a]
    for gi, arr in zip((1, 2), dilated):
        d = ATTN_GROUPS[gi][1]
        specs += [pl.BlockSpec((1, d, SPAN // d, LANES), lambda b, hp, s: (b, 0, s, hp)),
                  pl.BlockSpec((1, d, seq // d, LANES), lambda b, hp, s: (b, 0, 0, heads + hp)),
                  pl.BlockSpec((1, d, seq // d, LANES), lambda b, hp, s: (b, 0, 0, 2 * heads + hp))]
        operands += [arr, arr, arr]
    bias_spec = pl.BlockSpec((N_GROUPS, 2, 2, BLK, 2 * BLK), lambda b, hp, s: (0, 0, hp, 0, 0))
    steps = batch * n_hp * spans
    flat = [w.reshape(-1, w.shape[-1]) for w in casts]
    w_specs = [pl.BlockSpec((w.shape[0] // steps, w.shape[1]), lambda b, hp, s: ((b * n_hp + hp) * spans + s, 0))
               for w in flat]
    outs = pl.pallas_call(
        functools.partial(_attn_kernel, seq=seq),
        grid=(batch, n_hp, spans),
        in_specs=specs + [bias_spec] + w_specs,
        out_specs=[pl.BlockSpec((SPAN, LANES), lambda b, hp, s: (b * spans + s, hp))] + w_specs,
        out_shape=[jax.ShapeDtypeStruct((n, GROUP_W), BF16)] + [jax.ShapeDtypeStruct(w.shape, BF16) for w in flat],
        scratch_shapes=[
            pltpu.VMEM((2 * ACC_PARTS, SPAN, LANES), F32),
            pltpu.VMEM((ACC_PARTS, SPAN, LANES), F32),
        ],
        compiler_params=_cparams(("arbitrary", "arbitrary", "arbitrary")),
        interpret=interpret,
        name="attn",
    )(*operands, _attn_bias(), *flat)
    return (outs[0],) + tuple(o.reshape(w.shape) for o, w in zip(outs[1:], casts))


def _ret_kernel(*refs):
    heads_per_qk = COL_BLOCK // RET_DK
    n_qk = RET_HEADS // heads_per_qk
    q_refs, k_refs = refs[:n_qk], refs[n_qk:2 * n_qk]
    v_refs = refs[2 * n_qk:2 * n_qk + RET_HEADS]
    g_refs = refs[2 * n_qk + RET_HEADS:2 * n_qk + 2 * RET_HEADS]
    dec_ref, xi_ref, zeta_ref, gch_ref, o_ref, st_ref = refs[2 * n_qk + 2 * RET_HEADS:]
    @pl.when(pl.program_id(1) == 0)
    def _():
        st_ref[...] = jnp.zeros_like(st_ref)

    nt = (((1,), (1,)), ((), ()))
    scale = RET_DK ** -0.5

    for c in range(RET_TS // RET_CHUNK):
        rows = pl.ds(c * RET_CHUNK, RET_CHUNK)
        for h in range(RET_HEADS):
            kcols = slice((h % heads_per_qk) * RET_DK, (h % heads_per_qk + 1) * RET_DK)
            vcols = slice(h * RET_DV, (h + 1) * RET_DV)
            qi = q_refs[h // heads_per_qk][rows, kcols]
            kf = k_refs[h // heads_per_qk][rows, kcols].astype(F32) * scale
            ki = kf.astype(BF16)
            kz_t = jnp.transpose(kf * zeta_ref[h]).astype(BF16)
            vi = v_refs[h][rows, :]
            att = lax.dot_general(qi, ki, nt, preferred_element_type=F32) * dec_ref[h]
            inner = jnp.dot(att.astype(BF16), vi, preferred_element_type=F32)
            st = st_ref[h]
            cross = jnp.dot(qi, st.astype(BF16), preferred_element_type=F32) * xi_ref[h]
            st_ref[h] = gch_ref[h] * st + jnp.dot(kz_t, vi, preferred_element_type=F32)
            y = inner + cross
            mu = jnp.mean(y, axis=-1, keepdims=True)
            yc = y - mu
            var = jnp.mean(yc * yc, axis=-1, keepdims=True)
            yn = yc * lax.rsqrt(var + EPS)
            g = g_refs[h][rows, :].astype(F32)
            o_ref[rows, vcols] = (g * _sigmoid(g) * yn).astype(o_ref.dtype)


def _ret_tables():
    c = RET_CHUNK
    log_g = np.log1p(-np.exp2(-5.0 - np.arange(RET_HEADS, dtype=np.float64)))
    pos = np.arange(c, dtype=np.float64)
    diff = pos[:, None] - pos[None, :]
    dec = np.where(diff >= 0, np.exp(log_g[:, None, None] * np.maximum(diff, 0.0)), 0.0)
    xi = np.exp(log_g[:, None] * (pos + 1.0))[..., None] * np.ones((1, 1, RET_DV))
    zeta = np.exp(log_g[:, None] * (c - 1.0 - pos))[..., None] * np.ones((1, 1, RET_DK))
    gch = np.exp(log_g * c)[:, None, None] * np.ones((1, 1, RET_DV))
    return tuple(jnp.asarray(t, F32) for t in (dec, xi, zeta, gch))


def _retention(nat_a, nat_b, batch, seq, interpret):
    n = batch * seq
    dec, xi, zeta, gch = _ret_tables()
    v_w = RET_HEADS * RET_DV
    nts = seq // RET_TS
    const3 = lambda b, t: (0, 0, 0)

    def block(arr, col):
        base = 0 if arr is nat_a else PROJ_TN
        if arr is nat_a:
            col = _proj_plans()[1] - (PROJ_TN - col)
        idx = (col - base) // COL_BLOCK
        return arr, pl.BlockSpec((RET_TS, COL_BLOCK), lambda b, t: (b * nts + t, idx))

    qk_blocks = RET_HEADS * RET_DK // COL_BLOCK
    picks = [block(nat_a if COL_QR + k * COL_BLOCK < PROJ_TN else nat_b, COL_QR + k * COL_BLOCK)
             for k in range(qk_blocks)]
    picks += [block(nat_b, COL_KR + k * COL_BLOCK) for k in range(qk_blocks)]
    picks += [block(nat_b, COL_VR + h * RET_DV) for h in range(RET_HEADS)]
    picks += [block(nat_b, COL_GR + h * RET_DV) for h in range(RET_HEADS)]
    return pl.pallas_call(
        _ret_kernel,
        grid=(batch, nts),
        in_specs=[spec for _, spec in picks] + [
            pl.BlockSpec((RET_HEADS, RET_CHUNK, RET_CHUNK), const3),
            pl.BlockSpec((RET_HEADS, RET_CHUNK, RET_DV), const3),
            pl.BlockSpec((RET_HEADS, RET_CHUNK, RET_DK), const3),
            pl.BlockSpec((RET_HEADS, 1, RET_DV), const3),
        ],
        out_specs=pl.BlockSpec((RET_TS, v_w), lambda b, t: (b * nts + t, 0)),
        out_shape=jax.ShapeDtypeStruct((n, v_w), BF16),
        scratch_shapes=[pltpu.VMEM((RET_HEADS, RET_DK, RET_DV), F32)],
        compiler_params=_cparams(("arbitrary", "arbitrary")),
        interpret=interpret,
        name="retention",
    )(*[arr for arr, _ in picks], dec, xi, zeta, gch)


ROUTER_OFF = N_EXPERT_GROUPS


def _pack_bf16_pair(a, b):
    hi = lax.bitcast_convert_type(a.astype(BF16).astype(F32), jnp.uint32)
    lo = lax.bitcast_convert_type(b.astype(BF16).astype(F32), jnp.uint32)
    return lax.bitcast_convert_type(hi | (lo >> 16), jnp.int32)


def _unpack_bf16_pair(w):
    u = lax.bitcast_convert_type(w, jnp.uint32)
    a = lax.bitcast_convert_type(u & jnp.uint32(0xFFFF0000), F32).astype(BF16)
    b = lax.bitcast_convert_type(u << 16, F32).astype(BF16)
    return a, b


def _pack_rows(y):
    q = D_MODEL // 4
    return (_pack_bf16_pair(y[:, 0:q], y[:, 2 * q:3 * q]), _pack_bf16_pair(y[:, q:2 * q], y[:, 3 * q:4 * q]))


def _unpack_rows(slab0, slab1):
    q0, q2 = _unpack_bf16_pair(slab0)
    q1, q3 = _unpack_bf16_pair(slab1)
    return jnp.concatenate([q0, q1, q2, q3], axis=1)


def _mix_kernel(oa_ref, or_ref, *refs):
    n_gate = D_MODEL // COL_BLOCK
    ga_refs, gr_refs = refs[:n_gate], refs[n_gate:2 * n_gate]
    (x_ref, pa_ref, pr_ref, wo_ref, gf_ref, wr_ref, br_ref,
     x1_ref, h2_ref, route_ref, route_t_ref, cnt_ref, carry_ref, logit_ref) = refs[2 * n_gate:]
    step = pl.program_id(0)

    @pl.when(step == 0)
    def _():
        carry_ref[...] = jnp.zeros_like(carry_ref)
        logit_ref[...] = jnp.zeros_like(logit_ref)

    routing = iter([functools.partial(_route_rows, pl.ds(c * ROUTE_CHUNK, ROUTE_CHUNK), step > 0, logit_ref,
                                      route_ref, route_t_ref, cnt_ref, carry_ref)
                    for c in range(MIX_TM // ROUTE_CHUNK)])
    for c in range(MIX_TM // MIX_CHUNK):
        for _ in _mix_rows(pl.ds(c * MIX_CHUNK, MIX_CHUNK), oa_ref, or_ref, ga_refs, gr_refs,
                           x_ref, pa_ref, pr_ref, wo_ref, gf_ref, wr_ref, br_ref, x1_ref, h2_ref, logit_ref):
            next(routing, lambda: None)()
    for piece in routing:
        piece()


def _mix_rows(rows, oa_ref, or_ref, ga_refs, gr_refs, x_ref, pa_ref, pr_ref, wo_ref, gf_ref, wr_ref, br_ref,
              x1_ref, h2_ref, logit_ref):
    a = jnp.dot(oa_ref[rows, :], pa_ref[...], preferred_element_type=F32)
    yield
    r = jnp.dot(or_ref[rows, :], pr_ref[...], preferred_element_type=F32)
    yield
    merged = jnp.concatenate(
        [_sigmoid(ga[rows, :].astype(F32)) * a[:, k * COL_BLOCK:(k + 1) * COL_BLOCK]
         + _sigmoid(gr[rows, :].astype(F32)) * r[:, k * COL_BLOCK:(k + 1) * COL_BLOCK]
         for k, (ga, gr) in enumerate(zip(ga_refs, gr_refs))], axis=1)
    x1 = x_ref[rows, :] + jnp.dot(merged.astype(BF16), wo_ref[...], preferred_element_type=F32)
    x1_ref[rows, :] = x1
    ms = jnp.mean(x1 * x1, axis=-1, keepdims=True)
    h2 = x1 * lax.rsqrt(ms + EPS) * gf_ref[...]
    h2_ref[0, rows, :], h2_ref[1, rows, :] = _pack_rows(h2)

    h_hi = h2.astype(BF16)
    h_lo = (h2 - h_hi.astype(F32)).astype(BF16)
    both = jnp.dot(h_hi, wr_ref[...], preferred_element_type=F32)
    logit_ref[rows, :] = (both[:, :LANES] + both[:, LANES:]
                          + jnp.dot(h_lo, wr_ref[:, :LANES], preferred_element_type=F32) + br_ref[...])


def _route_rows(rows, live, logit_ref, route_ref, route_t_ref, cnt_ref, carry_ref):
    logits = logit_ref[rows, :]
    tm = logits.shape[0]
    lane = lax.broadcasted_iota(jnp.int32, (tm, LANES), 1).astype(F32)
    big = jnp.float32(4 * LANES)
    ninf = -jnp.inf
    is_g = lane < N_EXPERT_GROUPS
    gl = jnp.where(is_g, logits, ninf)
    gmax = jnp.max(gl, axis=-1, keepdims=True)
    gsum = jnp.sum(jnp.where(is_g, jnp.exp(gl - gmax), 0.0), axis=-1, keepdims=True)
    g_val = 1.0 / gsum
    g_idx = jnp.min(jnp.where(jnp.logical_and(is_g, gl == gmax), lane, big), axis=-1, keepdims=True)
    lo = ROUTER_OFF + EXPERTS_PER_GROUP * g_idx
    in_grp = jnp.logical_and(lane >= lo, lane < lo + EXPERTS_PER_GROUP)
    el = jnp.where(in_grp, logits, ninf)
    v1 = jnp.max(el, axis=-1, keepdims=True)
    i1 = jnp.min(jnp.where(jnp.logical_and(in_grp, el == v1), lane, big), axis=-1, keepdims=True)
    rest = jnp.logical_and(in_grp, lane != i1)
    el2 = jnp.where(rest, logits, ninf)
    v2 = jnp.max(el2, axis=-1, keepdims=True)
    i2 = jnp.min(jnp.where(jnp.logical_and(rest, el2 == v2), lane, big), axis=-1, keepdims=True)
    t = jnp.exp(v2 - v1)
    w1 = g_val / (1.0 + t)
    w2 = g_val * t / (1.0 + t)

    sel = jnp.logical_or(lane == i1, lane == i2)
    sel_bf = jnp.where(sel, 1.0, 0.0).astype(BF16)
    row = lax.broadcasted_iota(jnp.int32, (tm, tm), 0)
    col = lax.broadcasted_iota(jnp.int32, (tm, tm), 1)
    tri = jnp.where(col < row, 1.0, 0.0).astype(BF16)
    before = jnp.dot(tri, sel_bf, preferred_element_type=F32) + carry_ref[...]
    r1 = jnp.sum(jnp.where(lane == i1, before, 0.0), axis=-1, keepdims=True)
    r2 = jnp.sum(jnp.where(lane == i2, before, 0.0), axis=-1, keepdims=True)
    carry = carry_ref[...] + jnp.where(live, jnp.sum(jnp.where(sel, 1.0, 0.0), axis=0, keepdims=True), 0.0)
    carry_ref[...] = carry
    cnt_ref[...] = carry

    vals = (i1 - ROUTER_OFF, i2 - ROUTER_OFF, w1, w2, r1, r2)
    route = jnp.zeros((tm, LANES), F32)
    for j, v in enumerate(vals):
        route = jnp.where(lane == j, v, route)
    route_ref[rows, :] = route
    route_t_ref[:, rows] = jnp.transpose(route)[:ROUTE_ROWS, :]


def _mix(o_attn, o_ret, nat_b, x2, pa, pr, wo, g_ffn, w_router, b_router, interpret):
    n = x2.shape[0]
    tm = MIX_TM
    last = n // tm - 1
    n_gate = D_MODEL // COL_BLOCK
    const = lambda i: (0, 0)
    cur = lambda i: jnp.minimum(i, last)
    prev = lambda i: jnp.maximum(i - 1, 0)
    return pl.pallas_call(
        _mix_kernel,
        grid=(n // tm + 1,),
        in_specs=[
            pl.BlockSpec((tm, GROUP_W), lambda i: (cur(i), 0)),
            pl.BlockSpec((tm, D_MODEL), lambda i: (cur(i), 0)),
        ] + [
            pl.BlockSpec((tm, COL_BLOCK), functools.partial(lambda i, idx: (cur(i), idx), idx=(col - PROJ_TN) // COL_BLOCK + k))
            for col in (COL_GATE_A, COL_GATE_R) for k in range(n_gate)
        ] + [
            pl.BlockSpec((tm, D_MODEL), lambda i: (cur(i), 0)),
            pl.BlockSpec((GROUP_W, D_MODEL), const),
            pl.BlockSpec((D_MODEL, D_MODEL), const),
            pl.BlockSpec((D_MODEL, D_MODEL), const),
            pl.BlockSpec((1, D_MODEL), const),
            pl.BlockSpec((D_MODEL, 2 * LANES), const),
            pl.BlockSpec((1, LANES), const),
        ],
        out_specs=[
            pl.BlockSpec((tm, D_MODEL), lambda i: (cur(i), 0)),
            pl.BlockSpec((2, tm, SC_ROW_WORDS), lambda i: (0, cur(i), 0)),
            pl.BlockSpec((tm, LANES), lambda i: (prev(i), 0)),
            pl.BlockSpec((ROUTE_ROWS, tm), lambda i: (0, prev(i))),
            pl.BlockSpec((1, LANES), const),
        ],
        out_shape=[
            jax.ShapeDtypeStruct((n, D_MODEL), F32),
            jax.ShapeDtypeStruct((2, n, SC_ROW_WORDS), jnp.int32),
            jax.ShapeDtypeStruct((n, LANES), F32),
            jax.ShapeDtypeStruct((ROUTE_ROWS, n), F32),
            jax.ShapeDtypeStruct((1, LANES), F32),
        ],
        scratch_shapes=[pltpu.VMEM((1, LANES), F32), pltpu.VMEM((tm, LANES), F32)],
        compiler_params=_cparams(("arbitrary",)),
        interpret=interpret,
        name="mix_router",
    )(o_attn, o_ret, *([nat_b] * (2 * n_gate)), x2, pa, pr, wo, g_ffn, w_router, b_router)


def _expert_kernel(te_ref, tr_ref, xs_ref, wg_ref, wu_ref, wd_ref, o_ref):
    i = pl.program_id(0)

    @pl.when(tr_ref[i] == i)
    def _():
        xs = _unpack_rows(xs_ref[0], xs_ref[1])
        a = jnp.dot(xs, wg_ref[0], preferred_element_type=F32)
        u = jnp.dot(xs, wu_ref[0], preferred_element_type=F32)
        hid = (a * _sigmoid(a) * u).astype(BF16)
        y = jnp.dot(hid, wd_ref[0], preferred_element_type=F32)
        o_ref[0], o_ref[1] = _pack_rows(y)


def _experts(xs, tile_expert, tile_row, w_gate, w_up, w_down, interpret):
    p = xs.shape[1]
    n_tiles = p // EXP_TM
    grid_spec = pltpu.PrefetchScalarGridSpec(
        num_scalar_prefetch=2,
        grid=(n_tiles,),
        in_specs=[
            pl.BlockSpec((2, EXP_TM, SC_ROW_WORDS), lambda i, te, tr: (0, tr[i], 0)),
            pl.BlockSpec((1, D_MODEL, EXPERT_FF), lambda i, te, tr: (te[i], 0, 0)),
            pl.BlockSpec((1, D_MODEL, EXPERT_FF), lambda i, te, tr: (te[i], 0, 0)),
            pl.BlockSpec((1, EXPERT_FF, D_MODEL), lambda i, te, tr: (te[i], 0, 0)),
        ],
        out_specs=pl.BlockSpec((2, EXP_TM, SC_ROW_WORDS), lambda i, te, tr: (0, tr[i], 0)),
    )
    return pl.pallas_call(
        _expert_kernel,
        grid_spec=grid_spec,
        out_shape=jax.ShapeDtypeStruct((2, p, SC_ROW_WORDS), jnp.int32),
        compiler_params=_cparams(("arbitrary",)),
        interpret=interpret,
        name="experts",
    )(tile_expert, tile_row, xs, w_gate, w_up, w_down)


def _final_kernel(x1_ref, yab_ref, route_ref, g_ref, o_ref):
    route = route_ref[...]
    w1 = route[:, 2:3]
    w2 = route[:, 3:4]
    ya = _unpack_rows(yab_ref[0], yab_ref[2]).astype(F32)
    yb = _unpack_rows(yab_ref[1], yab_ref[3]).astype(F32)
    x2 = x1_ref[...] + w1 * ya + w2 * yb
    ms = jnp.mean(x2 * x2, axis=-1, keepdims=True)
    o_ref[...] = x2 * lax.rsqrt(ms + EPS) * g_ref[...]


def _final(x1, yab, route, g_final, interpret):
    n = x1.shape[0]
    tm = FIN_TM
    row = lambda i: (i, 0)
    return pl.pallas_call(
        _final_kernel,
        grid=(n // tm,),
        in_specs=[
            pl.BlockSpec((tm, D_MODEL), row),
            pl.BlockSpec((4, tm, SC_ROW_WORDS), lambda i: (0, i, 0)),
            pl.BlockSpec((tm, LANES), row),
            pl.BlockSpec((1, D_MODEL), lambda i: (0, 0)),
        ],
        out_specs=pl.BlockSpec((tm, D_MODEL), row),
        out_shape=jax.ShapeDtypeStruct((n, D_MODEL), F32),
        compiler_params=_cparams(("arbitrary",)),
        interpret=interpret,
        name="combine_final",
    )(x1, yab, route, g_final)


def _dest_kernel(offs_ref, route_t_ref, idx_ref, *, n_rows):
    route_t = route_t_ref[...]
    experts = route_t[0:2, :]
    dest = route_t[4:6, :].astype(jnp.int32)
    for e in range(N_EXPERTS):
        dest = dest + jnp.where(experts == float(e), offs_ref[e], 0)
    idx_ref[0:2, :] = dest
    idx_ref[2:4, :] = dest + n_rows


def _route_plan(route_t, counts, n, interpret):
    cnt = counts[0, ROUTER_OFF:ROUTER_OFF + N_EXPERTS].astype(jnp.int32)
    padded = ((cnt + EXP_TM - 1) // EXP_TM) * EXP_TM
    ends = jnp.cumsum(padded)
    offs = ends - padded
    n_rows = 2 * n + N_EXPERTS * EXP_TM
    idx4 = pl.pallas_call(
        functools.partial(_dest_kernel, n_rows=n_rows),
        grid_spec=pltpu.PrefetchScalarGridSpec(
            num_scalar_prefetch=1, grid=(1,),
            in_specs=[pl.BlockSpec(route_t.shape, lambda i, offs: (0, 0))],
            out_specs=pl.BlockSpec((4, n), lambda i, offs: (0, 0))),
        out_shape=jax.ShapeDtypeStruct((4, n), jnp.int32),
        interpret=interpret,
        name="route_dest",
    )(offs, route_t)
    tile_row = jnp.minimum(jnp.arange(n_rows // EXP_TM, dtype=jnp.int32), ends[-1] // EXP_TM - 1)
    tile_expert = jnp.sum((ends[None, :] <= (tile_row * EXP_TM)[:, None]).astype(jnp.int32), axis=1)
    return idx4, tile_expert, tile_row, n_rows


def _sc_mesh():
    return plsc.VectorSubcoreMesh(core_axis_name="core", subcore_axis_name="subcore")


def _sc_scatter_rows(rows, idx4, n_out):
    n_in, w = rows.shape
    nb = idx4.shape[1] // SC_WINDOW

    @functools.partial(pl.kernel, out_type=jax.ShapeDtypeStruct((n_out, w), rows.dtype), mesh=_sc_mesh(),
                       scratch_types=[], name="sc_scatter_rows")
    def scatter(x_hbm, ia_hbm, ib_hbm, o_hbm):
        def body(x_vmem, ia_vmem, ib_vmem):
            pltpu.sync_copy(x_vmem, o_hbm.at[ia_vmem.at[0]])
            pltpu.sync_copy(x_vmem, o_hbm.at[ib_vmem.at[0]])

        pltpu.emit_pipeline(
            body,
            grid=(n_in // SC_WINDOW,),
            in_specs=[pl.BlockSpec((SC_WINDOW, w), lambda i: (i, 0)),
                      pl.BlockSpec((1, SC_WINDOW), lambda i: (2 * (i // nb), i % nb)),
                      pl.BlockSpec((1, SC_WINDOW), lambda i: (2 * (i // nb) + 1, i % nb))],
            out_specs=[],
            core_axis_name=("core", "subcore"),
            dimension_semantics=(pltpu.PARALLEL,),
        )(x_hbm, ia_hbm, ib_hbm)

    return scatter(rows, idx4, idx4)


def _sc_gather_rows(table, idx4):
    nb = idx4.shape[1] // SC_WINDOW
    n_idx = idx4.shape[0] * idx4.shape[1]
    w = table.shape[1]

    @functools.partial(pl.kernel, out_type=jax.ShapeDtypeStruct((n_idx, w), table.dtype), mesh=_sc_mesh(),
                       scratch_types=[], name="sc_gather_rows")
    def gather(t_hbm, i_hbm, o_hbm):
        def body(i_vmem, o_vmem):
            pltpu.sync_copy(t_hbm.at[i_vmem.at[0]], o_vmem)

        pltpu.emit_pipeline(
            body,
            grid=(n_idx // SC_WINDOW,),
            in_specs=[pl.BlockSpec((1, SC_WINDOW), lambda i: (i // nb, i % nb))],
            out_specs=[pl.BlockSpec((SC_WINDOW, w), lambda i: (i, 0))],
            core_axis_name=("core", "subcore"),
            dimension_semantics=(pltpu.PARALLEL,),
        )(i_hbm, o_hbm)

    return gather(table, idx4)


def _forward(x, g_mix, w_in, w_attn_branch, w_ret_branch, w_out, g_ffn, w_group_router, b_group_router,
             w_expert_router, b_expert_router, w_gate, w_up, w_down, g_final, interpret=False):
    batch, seq, d = x.shape
    n = batch * seq
    x2 = x.reshape(n, d)
    nat_a, nat_b, qkv_dilated, wd_bf = _proj(x2, g_mix[0][None, :], w_in[0].astype(BF16), w_down[0], batch, seq,
                                             interpret)
    o_attn, wg_bf, wu_bf = _attention(nat_a, qkv_dilated, (w_gate[0], w_up[0]), batch, seq, interpret)
    o_ret = _retention(nat_a, nat_b, batch, seq, interpret)
    pad = LANES - N_EXPERT_GROUPS - N_EXPERTS
    w_router = jnp.concatenate([w_group_router[0], w_expert_router[0], jnp.zeros((d, pad), F32)], axis=-1)
    w_router_hi = w_router.astype(BF16)
    w_router_lo = (w_router - w_router_hi.astype(F32)).astype(BF16)
    w_router2 = jnp.concatenate([w_router_hi, w_router_lo], axis=-1)
    b_router = jnp.concatenate([b_group_router[0], b_expert_router[0], jnp.zeros((pad,), F32)])[None, :]
    x1, h2p, route, route_t, counts = _mix(o_attn, o_ret, nat_b, x2, w_attn_branch[0].astype(BF16),
                                           w_ret_branch[0].astype(BF16), w_out[0].astype(BF16),
                                           g_ffn[0][None, :], w_router2, b_router, interpret)
    idx4, tile_expert, tile_row, n_rows = _route_plan(route_t, counts, n, interpret)
    xs = _sc_scatter_rows(h2p.reshape(2 * n, SC_ROW_WORDS), idx4, 2 * n_rows)
    ys = _experts(xs.reshape(2, n_rows, SC_ROW_WORDS), tile_expert, tile_row, wg_bf, wu_bf, wd_bf, interpret)
    yab = _sc_gather_rows(ys.reshape(2 * n_rows, SC_ROW_WORDS), idx4)
    out = _final(x1, yab.reshape(4, n, SC_ROW_WORDS), route, g_final[None, :], interpret)
    return out.reshape(batch, seq, d)


def kernel(x, g_mix, w_in, w_attn_branch, w_ret_branch, w_out, g_ffn, w_group_router, b_group_router,
           w_expert_router, b_expert_router, w_gate, w_up, w_down, g_final):
    return _forward(x, g_mix, w_in, w_attn_branch, w_ret_branch, w_out, g_ffn, w_group_router,
                    b_group_router, w_expert_router, b_expert_router, w_gate, w_up, w_down, g_final)
```

```python
import functools

import numpy as np
import jax
import jax.numpy as jnp
from jax import lax
from jax.experimental import pallas as pl
from jax.experimental.pallas import tpu as pltpu
from jax.experimental.pallas import tpu_sc as plsc

F32 = jnp.float32
BF16 = jnp.bfloat16

D_MODEL = 1024
ATTN_GROUPS = ((128, 1), (512, 4), (2048, 16))
N_GROUPS = len(ATTN_GROUPS)
ATTN_HEADS = 8
HEAD_DIM = 64
GROUP_W = ATTN_HEADS * HEAD_DIM
QKV_W = N_GROUPS * GROUP_W
RET_HEADS = 4
RET_DK = 128
RET_DV = 256
RET_CHUNK = 128
RET_TS = 1024
N_EXPERT_GROUPS = 4
EXPERTS_PER_GROUP = 8
N_EXPERTS = N_EXPERT_GROUPS * EXPERTS_PER_GROUP
EXPERT_FF = 512
EPS = 1e-6

LANES = 128
BLK = 128
SPAN = 2048
NEG = -1e30
ACC_PARTS = 3

COL_QA = 0
COL_KA = COL_QA + QKV_W
COL_VA = COL_KA + QKV_W
COL_QR = COL_VA + QKV_W
COL_KR = COL_QR + RET_HEADS * RET_DK
COL_VR = COL_KR + RET_HEADS * RET_DK
COL_GR = COL_VR + RET_HEADS * RET_DV
COL_GATE_A = COL_GR + RET_HEADS * RET_DV
COL_GATE_R = COL_GATE_A + D_MODEL
IN_WIDTH = COL_GATE_R + D_MODEL
COL_BLOCK = 256

PROJ_TM = 512
PROJ_TN = IN_WIDTH // 2
MXU_N = 256
MIX_TM = 1024
MIX_CHUNK = 512
ROUTE_CHUNK = 256
ROUTE_ROWS = 8
EXP_TM = 512
SC_WINDOW = 128
SC_ROW_WORDS = 256
FIN_TM = 1024
VMEM_LIMIT = 56 * 1024 * 1024


def _cparams(sem):
    return pltpu.CompilerParams(dimension_semantics=sem, vmem_limit_bytes=VMEM_LIMIT)


def _sigmoid(x):
    return 0.5 * jnp.tanh(0.5 * x) + 0.5


def _proj_plans():
    plan_a, nat_col = [], 0
    for c in range(PROJ_TN // MXU_N):
        col = c * MXU_N
        section, within = divmod(col, QKV_W)
        gi = within // GROUP_W
        if col < COL_QR and gi > 0:
            plan_a.append((None, (gi, section * GROUP_W + within % GROUP_W)))
        else:
            plan_a.append((nat_col, None))
            nat_col += MXU_N
    plan_b = [(c * MXU_N, None) for c in range((IN_WIDTH - PROJ_TN) // MXU_N)]
    return plan_a, nat_col, plan_b


def _proj_kernel(*refs, plan, n_cast):
    has_dilated = any(d is not None for _, d in plan)
    x_ref, g_ref, w_ref = refs[:3]
    cast_in = refs[3:3 + n_cast]
    o_ref = refs[3 + n_cast]
    rest = refs[4 + n_cast:]
    if has_dilated:
        d4_ref, d16_ref = rest[:2]
        stage_ref, mid_ref = rest[2 + n_cast:]
        rest = rest[2:]
    cast_out = rest[:n_cast]

    x = x_ref[...]
    ms = jnp.mean(x * x, axis=-1, keepdims=True)
    h = (x * lax.rsqrt(ms + EPS) * g_ref[...]).astype(BF16)
    step = ATTN_GROUPS[1][1]
    quarter = PROJ_TM // step
    for c, (nat_col, dil) in enumerate(plan):
        res = jnp.dot(h, w_ref[:, c * MXU_N:(c + 1) * MXU_N], preferred_element_type=F32)
        if nat_col is not None:
            o_ref[:, nat_col:nat_col + MXU_N] = res.astype(o_ref.dtype)
        if dil is None:
            continue
        gi, col = dil
        slot = c % 2
        for lt in range(MXU_N // LANES):
            stage_ref[slot, lt] = res[:, lt * LANES:(lt + 1) * LANES]
        for lt in range(MXU_N // LANES):
            cols = slice(col + lt * LANES, col + (lt + 1) * LANES)
            for b in range(step):
                rows = stage_ref[slot, lt, pl.ds(b, quarter, stride=step), :]
                if gi == 1:
                    d4_ref[0, b, :, cols] = rows.astype(BF16)
                else:
                    mid_ref[slot, lt, b * quarter:(b + 1) * quarter, :] = rows
            if gi == 2:
                for b in range(step):
                    for a in range(step):
                        rows = mid_ref[slot, lt, pl.ds(b * quarter + a, quarter // step, stride=step), :]
                        d16_ref[0, a * step + b, :, cols] = rows.astype(BF16)

    for src, dst in zip(cast_in, cast_out):
        dst[...] = src[...].astype(BF16)


def _proj(x2, g_mix, w_in_bf16, w_gate, w_up, w_down, batch, seq, interpret):
    n = x2.shape[0]
    n_i = n // PROJ_TM
    tiles = seq // PROJ_TM
    plan_a, width_a, plan_b = _proj_plans()
    dils = [ATTN_GROUPS[gi][1] for gi in (1, 2)]
    row = lambda i: (i, 0)

    def call(plan, col_block, nat_width, casts, name):
        flat = [w.reshape(-1, w.shape[-1]) for w in casts]
        w_specs = [pl.BlockSpec((w.shape[0] // n_i, w.shape[1]), row) for w in flat]
        dilated = any(d is not None for _, d in plan)
        d_specs = [pl.BlockSpec((1, d, PROJ_TM // d, 3 * GROUP_W), lambda i: (i // tiles, 0, i % tiles, 0))
                   for d in dils] if dilated else []
        d_shapes = [jax.ShapeDtypeStruct((batch, d, seq // d, 3 * GROUP_W), BF16) for d in dils] if dilated else []
        stage = pltpu.VMEM((2, MXU_N // LANES, PROJ_TM, LANES), F32)
        return pl.pallas_call(
            functools.partial(_proj_kernel, plan=plan, n_cast=len(casts)),
            grid=(n_i,),
            in_specs=[pl.BlockSpec((PROJ_TM, D_MODEL), row),
                      pl.BlockSpec((1, D_MODEL), lambda i: (0, 0)),
                      pl.BlockSpec((D_MODEL, PROJ_TN), lambda i: (0, col_block))] + w_specs,
            out_specs=[pl.BlockSpec((PROJ_TM, nat_width), row)] + d_specs + w_specs,
            out_shape=[jax.ShapeDtypeStruct((n, nat_width), BF16)] + d_shapes
            + [jax.ShapeDtypeStruct(w.shape, BF16) for w in flat],
            scratch_shapes=[stage, stage] if dilated else [],
            compiler_params=_cparams(("arbitrary",)),
            interpret=interpret,
            name=name,
        )(x2, g_mix, w_in_bf16, *flat)

    nat_a, qkv_d4, qkv_d16, wg_bf, wu_bf = call(plan_a, 0, width_a, (w_gate, w_up), "proj_attn")
    nat_b, wd_bf = call(plan_b, 1, IN_WIDTH - PROJ_TN, (w_down,), "proj_rest")
    return (nat_a, nat_b, (qkv_d4, qkv_d16), wg_bf.reshape(w_gate.shape), wu_bf.reshape(w_up.shape),
            wd_bf.reshape(w_down.shape))


def _attn_unit(q2, kk, vv, bias_a, bias_b):
    lane = lax.broadcasted_iota(jnp.int32, (BLK, LANES), 1)
    left = lane < HEAD_DIM
    zero = jnp.zeros_like(q2)
    nt = (((1,), (1,)), ((), ()))
    q_st = jnp.concatenate([jnp.where(left, q2, zero), jnp.where(left, zero, q2)], axis=0)
    s = lax.dot_general(q_st, kk, nt, preferred_element_type=F32) + jnp.concatenate([bias_a, bias_b], axis=0)
    m = jnp.max(s, axis=-1, keepdims=True)
    p = jnp.exp(s - m)
    den = jnp.sum(p, axis=-1, keepdims=True)
    o = jnp.dot(p.astype(BF16), vv, preferred_element_type=F32)
    return (jnp.where(left, o[:BLK], o[BLK:]), jnp.where(left, m[:BLK], m[BLK:]),
            jnp.where(left, den[:BLK], den[BLK:]))


def _attn_kernel(q1_ref, k1_ref, v1_ref, q2_ref, k2_ref, v2_ref, q3_ref, k3_ref, v3_ref,
                 bias_ref, o_ref, acc_ref, accw_ref, *, seq):
    s_id = pl.program_id(2)
    step = ATTN_GROUPS[1][1]
    quarter = SPAN // step
    dilated = {1: (q2_ref, k2_ref, v2_ref), 2: (q3_ref, k3_ref, v3_ref)}

    def dilated_unit(gi, d, m, r, first):
        slot = gi - 1
        q_ref, k_ref, v_ref = dilated[gi]
        loc = BLK * m * d + r
        cur = pl.multiple_of(s_id * (SPAN // d) + BLK * m, BLK)
        prev = pl.multiple_of(jnp.where(first == 1, cur, cur - BLK), BLK)
        q2 = q_ref[0, r, BLK * m:BLK * (m + 1), :] * 0.125
        kk = jnp.concatenate([k_ref[0, r, pl.ds(prev, BLK), :], k_ref[0, r, pl.ds(cur, BLK), :]], axis=0)
        vv = jnp.concatenate([v_ref[0, r, pl.ds(prev, BLK), :], v_ref[0, r, pl.ds(cur, BLK), :]], axis=0)
        parts = _attn_unit(q2, kk, vv, bias_ref[gi, first, 0], bias_ref[gi, first, 1])
        for j, part in enumerate(parts):
            if d == step:
                acc_ref[ACC_PARTS * slot + j, pl.ds(loc, BLK, stride=d), :] = part
            else:
                a, b = divmod(r, step)
                accw_ref[j, pl.ds(b * quarter + a, BLK, stride=step), :] = part

    def reinterleave_pieces():
        pieces = []
        for j in range(ACC_PARTS):
            for b in range(step):
                for j0 in range(0, quarter, 256):
                    def piece(j=j, b=b, j0=j0):
                        acc_ref[ACC_PARTS + j, pl.ds(b + step * j0, 256, stride=step), :] = (
                            accw_ref[j, b * quarter + j0:b * quarter + j0 + 256, :])
                    pieces.append(piece)
        return pieces

    def dense_unit(m):
        loc = m * BLK
        cur = pl.multiple_of(s_id * SPAN + loc, BLK)
        prev = pl.multiple_of(jnp.maximum(cur - BLK, 0), BLK)
        first = jnp.where(cur == 0, 1, 0)
        q2 = q1_ref[loc:loc + BLK, :] * 0.125
        kk = jnp.concatenate([k1_ref[pl.ds(prev, BLK), :], k1_ref[pl.ds(cur, BLK), :]], axis=0)
        vv = jnp.concatenate([v1_ref[pl.ds(prev, BLK), :], v1_ref[pl.ds(cur, BLK), :]], axis=0)
        n1, m1, d1 = _attn_unit(q2, kk, vv, bias_ref[0, first, 0], bias_ref[0, first, 1])
        (n2, m2, d2), (n3, m3, d3) = (
            tuple(acc_ref[g * ACC_PARTS + j, loc:loc + BLK, :] for j in range(ACC_PARTS)) for g in range(2))
        mx = jnp.maximum(jnp.maximum(m1, m2), m3)
        w1, w2, w3 = jnp.exp(m1 - mx), jnp.exp(m2 - mx), jnp.exp(m3 - mx)
        num = w1 * n1 + w2 * n2 + w3 * n3
        den = w1 * d1 + w2 * d2 + w3 * d3
        o_ref[loc:loc + BLK, :] = (num / den).astype(o_ref.dtype)

    first_span = jnp.where(s_id == 0, 1, 0)
    for gi in (2, 1):
        d = ATTN_GROUPS[gi][1]
        for m in range(SPAN // (BLK * d)):
            for r in range(d):
                dilated_unit(gi, d, m, r, first_span if m == 0 else 0)
        if d != step:
            for piece in reinterleave_pieces():
                piece()
    for m in range(SPAN // BLK):
        dense_unit(m)


def _attn_bias():
    slopes = np.exp2(-8.0 * np.arange(1, ATTN_HEADS + 1, dtype=np.float64) / ATTN_HEADS)
    qi = np.arange(BLK)[:, None]
    kj = np.arange(2 * BLK)[None, :]
    rel = qi + BLK - kj
    out = np.zeros((N_GROUPS, 2, ATTN_HEADS, BLK, 2 * BLK), np.float32)
    for gi, (window, d) in enumerate(ATTN_GROUPS):
        n_back = window // d
        assert n_back == BLK
        valid = (rel >= 0) & (rel <= n_back)
        bias = -slopes[:, None, None] * (rel * d)[None].astype(np.float64)
        out[gi, 0] = np.where(valid[None], bias, NEG)
        out[gi, 1] = np.where((valid & (kj >= BLK))[None], bias, NEG)
    return jnp.asarray(out)


def _attention(nat_a, dilated, batch, seq, interpret):
    n = batch * seq
    spans = seq // SPAN
    n_hp = GROUP_W // LANES
    heads = GROUP_W // LANES
    specs = [pl.BlockSpec((SPAN, LANES), lambda b, hp, s: (b * spans + s, hp)),
             pl.BlockSpec((seq, LANES), lambda b, hp, s: (b, heads + hp)),
             pl.BlockSpec((seq, LANES), lambda b, hp, s: (b, 2 * heads + hp))]
    operands = [nat_a, nat_a, nat_a]
    for gi, arr in zip((1, 2), dilated):
        d = ATTN_GROUPS[gi][1]
        specs += [pl.BlockSpec((1, d, SPAN // d, LANES), lambda b, hp, s: (b, 0, s, hp)),
                  pl.BlockSpec((1, d, seq // d, LANES), lambda b, hp, s: (b, 0, 0, heads + hp)),
                  pl.BlockSpec((1, d, seq // d, LANES), lambda b, hp, s: (b, 0, 0, 2 * heads + hp))]
        operands += [arr, arr, arr]
    bias_spec = pl.BlockSpec((N_GROUPS, 2, 2, BLK, 2 * BLK), lambda b, hp, s: (0, 0, hp, 0, 0))
    return pl.pallas_call(
        functools.partial(_attn_kernel, seq=seq),
        grid=(batch, n_hp, spans),
        in_specs=specs + [bias_spec],
        out_specs=pl.BlockSpec((SPAN, LANES), lambda b, hp, s: (b * spans + s, hp)),
        out_shape=jax.ShapeDtypeStruct((n, GROUP_W), BF16),
        scratch_shapes=[
            pltpu.VMEM((2 * ACC_PARTS, SPAN, LANES), F32),
            pltpu.VMEM((ACC_PARTS, SPAN, LANES), F32),
        ],
        compiler_params=_cparams(("arbitrary", "arbitrary", "arbitrary")),
        interpret=interpret,
        name="attn",
    )(*operands, _attn_bias())


def _ret_kernel(*refs):
    heads_per_qk = COL_BLOCK // RET_DK
    n_qk = RET_HEADS // heads_per_qk
    q_refs, k_refs = refs[:n_qk], refs[n_qk:2 * n_qk]
    v_refs = refs[2 * n_qk:2 * n_qk + RET_HEADS]
    g_refs = refs[2 * n_qk + RET_HEADS:2 * n_qk + 2 * RET_HEADS]
    dec_ref, xi_ref, zeta_ref, gch_ref, o_ref, st_ref = refs[2 * n_qk + 2 * RET_HEADS:]
    @pl.when(pl.program_id(1) == 0)
    def _():
        st_ref[...] = jnp.zeros_like(st_ref)

    nt = (((1,), (1,)), ((), ()))
    scale = RET_DK ** -0.5

    for c in range(RET_TS // RET_CHUNK):
        rows = pl.ds(c * RET_CHUNK, RET_CHUNK)
        for h in range(RET_HEADS):
            kcols = slice((h % heads_per_qk) * RET_DK, (h % heads_per_qk + 1) * RET_DK)
            vcols = slice(h * RET_DV, (h + 1) * RET_DV)
            qi = q_refs[h // heads_per_qk][rows, kcols]
            kf = k_refs[h // heads_per_qk][rows, kcols].astype(F32) * scale
            ki = kf.astype(BF16)
            kz_t = jnp.transpose(kf * zeta_ref[h]).astype(BF16)
            vi = v_refs[h][rows, :]
            att = lax.dot_general(qi, ki, nt, preferred_element_type=F32) * dec_ref[h]
            inner = jnp.dot(att.astype(BF16), vi, preferred_element_type=F32)
            st = st_ref[h]
            cross = jnp.dot(qi, st.astype(BF16), preferred_element_type=F32) * xi_ref[h]
            st_ref[h] = gch_ref[h] * st + jnp.dot(kz_t, vi, preferred_element_type=F32)
            y = inner + cross
            mu = jnp.mean(y, axis=-1, keepdims=True)
            yc = y - mu
            var = jnp.mean(yc * yc, axis=-1, keepdims=True)
            yn = yc * lax.rsqrt(var + EPS)
            g = g_refs[h][rows, :].astype(F32)
            o_ref[rows, vcols] = (g * _sigmoid(g) * yn).astype(o_ref.dtype)


def _ret_tables():
    c = RET_CHUNK
    log_g = np.log1p(-np.exp2(-5.0 - np.arange(RET_HEADS, dtype=np.float64)))
    pos = np.arange(c, dtype=np.float64)
    diff = pos[:, None] - pos[None, :]
    dec = np.where(diff >= 0, np.exp(log_g[:, None, None] * np.maximum(diff, 0.0)), 0.0)
    xi = np.exp(log_g[:, None] * (pos + 1.0))[..., None] * np.ones((1, 1, RET_DV))
    zeta = np.exp(log_g[:, None] * (c - 1.0 - pos))[..., None] * np.ones((1, 1, RET_DK))
    gch = np.exp(log_g * c)[:, None, None] * np.ones((1, 1, RET_DV))
    return tuple(jnp.asarray(t, F32) for t in (dec, xi, zeta, gch))


def _retention(nat_a, nat_b, batch, seq, interpret):
    n = batch * seq
    dec, xi, zeta, gch = _ret_tables()
    v_w = RET_HEADS * RET_DV
    nts = seq // RET_TS
    const3 = lambda b, t: (0, 0, 0)

    def block(arr, col):
        base = 0 if arr is nat_a else PROJ_TN
        if arr is nat_a:
            col = _proj_plans()[1] - (PROJ_TN - col)
        idx = (col - base) // COL_BLOCK
        return arr, pl.BlockSpec((RET_TS, COL_BLOCK), lambda b, t: (b * nts + t, idx))

    qk_blocks = RET_HEADS * RET_DK // COL_BLOCK
    picks = [block(nat_a if COL_QR + k * COL_BLOCK < PROJ_TN else nat_b, COL_QR + k * COL_BLOCK)
             for k in range(qk_blocks)]
    picks += [block(nat_b, COL_KR + k * COL_BLOCK) for k in range(qk_blocks)]
    picks += [block(nat_b, COL_VR + h * RET_DV) for h in range(RET_HEADS)]
    picks += [block(nat_b, COL_GR + h * RET_DV) for h in range(RET_HEADS)]
    return pl.pallas_call(
        _ret_kernel,
        grid=(batch, nts),
        in_specs=[spec for _, spec in picks] + [
            pl.BlockSpec((RET_HEADS, RET_CHUNK, RET_CHUNK), const3),
            pl.BlockSpec((RET_HEADS, RET_CHUNK, RET_DV), const3),
            pl.BlockSpec((RET_HEADS, RET_CHUNK, RET_DK), const3),
            pl.BlockSpec((RET_HEADS, 1, RET_DV), const3),
        ],
        out_specs=pl.BlockSpec((RET_TS, v_w), lambda b, t: (b * nts + t, 0)),
        out_shape=jax.ShapeDtypeStruct((n, v_w), BF16),
        scratch_shapes=[pltpu.VMEM((RET_HEADS, RET_DK, RET_DV), F32)],
        compiler_params=_cparams(("arbitrary", "arbitrary")),
        interpret=interpret,
        name="retention",
    )(*[arr for arr, _ in picks], dec, xi, zeta, gch)


ROUTER_OFF = N_EXPERT_GROUPS


def _pack_bf16_pair(a, b):
    hi = lax.bitcast_convert_type(a.astype(BF16).astype(F32), jnp.uint32)
    lo = lax.bitcast_convert_type(b.astype(BF16).astype(F32), jnp.uint32)
    return lax.bitcast_convert_type(hi | (lo >> 16), jnp.int32)


def _unpack_bf16_pair(w):
    u = lax.bitcast_convert_type(w, jnp.uint32)
    a = lax.bitcast_convert_type(u & jnp.uint32(0xFFFF0000), F32).astype(BF16)
    b = lax.bitcast_convert_type(u << 16, F32).astype(BF16)
    return a, b


def _pack_rows(y):
    q = D_MODEL // 4
    return (_pack_bf16_pair(y[:, 0:q], y[:, 2 * q:3 * q]), _pack_bf16_pair(y[:, q:2 * q], y[:, 3 * q:4 * q]))


def _unpack_rows(slab0, slab1):
    q0, q2 = _unpack_bf16_pair(slab0)
    q1, q3 = _unpack_bf16_pair(slab1)
    return jnp.concatenate([q0, q1, q2, q3], axis=1)


def _mix_kernel(oa_ref, or_ref, *refs):
    n_gate = D_MODEL // COL_BLOCK
    ga_refs, gr_refs = refs[:n_gate], refs[n_gate:2 * n_gate]
    (x_ref, pa_ref, pr_ref, wo_ref, gf_ref, wr_ref, br_ref,
     x1_ref, h2_ref, route_ref, route_t_ref, cnt_ref, carry_ref, logit_ref) = refs[2 * n_gate:]
    step = pl.program_id(0)

    @pl.when(step == 0)
    def _():
        carry_ref[...] = jnp.zeros_like(carry_ref)
        logit_ref[...] = jnp.zeros_like(logit_ref)

    routing = iter([functools.partial(_route_rows, pl.ds(c * ROUTE_CHUNK, ROUTE_CHUNK), step > 0, logit_ref,
                                      route_ref, route_t_ref, cnt_ref, carry_ref)
                    for c in range(MIX_TM // ROUTE_CHUNK)])
    for c in range(MIX_TM // MIX_CHUNK):
        for _ in _mix_rows(pl.ds(c * MIX_CHUNK, MIX_CHUNK), oa_ref, or_ref, ga_refs, gr_refs,
                           x_ref, pa_ref, pr_ref, wo_ref, gf_ref, wr_ref, br_ref, x1_ref, h2_ref, logit_ref):
            next(routing, lambda: None)()
    for piece in routing:
        piece()


def _mix_rows(rows, oa_ref, or_ref, ga_refs, gr_refs, x_ref, pa_ref, pr_ref, wo_ref, gf_ref, wr_ref, br_ref,
              x1_ref, h2_ref, logit_ref):
    a = jnp.dot(oa_ref[rows, :], pa_ref[...], preferred_element_type=F32)
    yield
    r = jnp.dot(or_ref[rows, :], pr_ref[...], preferred_element_type=F32)
    yield
    merged = jnp.concatenate(
        [_sigmoid(ga[rows, :].astype(F32)) * a[:, k * COL_BLOCK:(k + 1) * COL_BLOCK]
         + _sigmoid(gr[rows, :].astype(F32)) * r[:, k * COL_BLOCK:(k + 1) * COL_BLOCK]
         for k, (ga, gr) in enumerate(zip(ga_refs, gr_refs))], axis=1)
    x1 = x_ref[rows, :] + jnp.dot(merged.astype(BF16), wo_ref[...], preferred_element_type=F32)
    x1_ref[rows, :] = x1
    ms = jnp.mean(x1 * x1, axis=-1, keepdims=True)
    h2 = x1 * lax.rsqrt(ms + EPS) * gf_ref[...]
    h2_ref[0, rows, :], h2_ref[1, rows, :] = _pack_rows(h2)

    h_hi = h2.astype(BF16)
    h_lo = (h2 - h_hi.astype(F32)).astype(BF16)
    both = jnp.dot(h_hi, wr_ref[...], preferred_element_type=F32)
    logit_ref[rows, :] = (both[:, :LANES] + both[:, LANES:]
                          + jnp.dot(h_lo, wr_ref[:, :LANES], preferred_element_type=F32) + br_ref[...])


def _route_rows(rows, live, logit_ref, route_ref, route_t_ref, cnt_ref, carry_ref):
    logits = logit_ref[rows, :]
    tm = logits.shape[0]
    lane = lax.broadcasted_iota(jnp.int32, (tm, LANES), 1).astype(F32)
    big = jnp.float32(4 * LANES)
    ninf = -jnp.inf
    is_g = lane < N_EXPERT_GROUPS
    gl = jnp.where(is_g, logits, ninf)
    gmax = jnp.max(gl, axis=-1, keepdims=True)
    gsum = jnp.sum(jnp.where(is_g, jnp.exp(gl - gmax), 0.0), axis=-1, keepdims=True)
    g_val = 1.0 / gsum
    g_idx = jnp.min(jnp.where(jnp.logical_and(is_g, gl == gmax), lane, big), axis=-1, keepdims=True)
    lo = ROUTER_OFF + EXPERTS_PER_GROUP * g_idx
    in_grp = jnp.logical_and(lane >= lo, lane < lo + EXPERTS_PER_GROUP)
    el = jnp.where(in_grp, logits, ninf)
    v1 = jnp.max(el, axis=-1, keepdims=True)
    i1 = jnp.min(jnp.where(jnp.logical_and(in_grp, el == v1), lane, big), axis=-1, keepdims=True)
    rest = jnp.logical_and(in_grp, lane != i1)
    el2 = jnp.where(rest, logits, ninf)
    v2 = jnp.max(el2, axis=-1, keepdims=True)
    i2 = jnp.min(jnp.where(jnp.logical_and(rest, el2 == v2), lane, big), axis=-1, keepdims=True)
    t = jnp.exp(v2 - v1)
    w1 = g_val / (1.0 + t)
    w2 = g_val * t / (1.0 + t)

    sel = jnp.logical_or(lane == i1, lane == i2)
    sel_bf = jnp.where(sel, 1.0, 0.0).astype(BF16)
    row = lax.broadcasted_iota(jnp.int32, (tm, tm), 0)
    col = lax.broadcasted_iota(jnp.int32, (tm, tm), 1)
    tri = jnp.where(col < row, 1.0, 0.0).astype(BF16)
    before = jnp.dot(tri, sel_bf, preferred_element_type=F32) + carry_ref[...]
    r1 = jnp.sum(jnp.where(lane == i1, before, 0.0), axis=-1, keepdims=True)
    r2 = jnp.sum(jnp.where(lane == i2, before, 0.0), axis=-1, keepdims=True)
    carry = carry_ref[...] + jnp.where(live, jnp.sum(jnp.where(sel, 1.0, 0.0), axis=0, keepdims=True), 0.0)
    carry_ref[...] = carry
    cnt_ref[...] = carry

    vals = (i1 - ROUTER_OFF, i2 - ROUTER_OFF, w1, w2, r1, r2)
    route = jnp.zeros((tm, LANES), F32)
    for j, v in enumerate(vals):
        route = jnp.where(lane == j, v, route)
    route_ref[rows, :] = route
    route_t_ref[:, rows] = jnp.transpose(route)[:ROUTE_ROWS, :]


def _mix(o_attn, o_ret, nat_b, x2, pa, pr, wo, g_ffn, w_router, b_router, interpret):
    n = x2.shape[0]
    tm = MIX_TM
    last = n // tm - 1
    n_gate = D_MODEL // COL_BLOCK
    const = lambda i: (0, 0)
    cur = lambda i: jnp.minimum(i, last)
    prev = lambda i: jnp.maximum(i - 1, 0)
    return pl.pallas_call(
        _mix_kernel,
        grid=(n // tm + 1,),
        in_specs=[
            pl.BlockSpec((tm, GROUP_W), lambda i: (cur(i), 0)),
            pl.BlockSpec((tm, D_MODEL), lambda i: (cur(i), 0)),
        ] + [
            pl.BlockSpec((tm, COL_BLOCK), functools.partial(lambda i, idx: (cur(i), idx), idx=(col - PROJ_TN) // COL_BLOCK + k))
            for col in (COL_GATE_A, COL_GATE_R) for k in range(n_gate)
        ] + [
            pl.BlockSpec((tm, D_MODEL), lambda i: (cur(i), 0)),
            pl.BlockSpec((GROUP_W, D_MODEL), const),
            pl.BlockSpec((D_MODEL, D_MODEL), const),
            pl.BlockSpec((D_MODEL, D_MODEL), const),
            pl.BlockSpec((1, D_MODEL), const),
            pl.BlockSpec((D_MODEL, 2 * LANES), const),
            pl.BlockSpec((1, LANES), const),
        ],
        out_specs=[
            pl.BlockSpec((tm, D_MODEL), lambda i: (cur(i), 0)),
            pl.BlockSpec((2, tm, SC_ROW_WORDS), lambda i: (0, cur(i), 0)),
            pl.BlockSpec((tm, LANES), lambda i: (prev(i), 0)),
            pl.BlockSpec((ROUTE_ROWS, tm), lambda i: (0, prev(i))),
            pl.BlockSpec((1, LANES), const),
        ],
        out_shape=[
            jax.ShapeDtypeStruct((n, D_MODEL), F32),
            jax.ShapeDtypeStruct((2, n, SC_ROW_WORDS), jnp.int32),
            jax.ShapeDtypeStruct((n, LANES), F32),
            jax.ShapeDtypeStruct((ROUTE_ROWS, n), F32),
            jax.ShapeDtypeStruct((1, LANES), F32),
        ],
        scratch_shapes=[pltpu.VMEM((1, LANES), F32), pltpu.VMEM((tm, LANES), F32)],
        compiler_params=_cparams(("arbitrary",)),
        interpret=interpret,
        name="mix_router",
    )(o_attn, o_ret, *([nat_b] * (2 * n_gate)), x2, pa, pr, wo, g_ffn, w_router, b_router)


def _expert_kernel(te_ref, tr_ref, xs_ref, wg_ref, wu_ref, wd_ref, o_ref):
    i = pl.program_id(0)

    @pl.when(tr_ref[i] == i)
    def _():
        xs = _unpack_rows(xs_ref[0], xs_ref[1])
        a = jnp.dot(xs, wg_ref[0], preferred_element_type=F32)
        u = jnp.dot(xs, wu_ref[0], preferred_element_type=F32)
        hid = (a * _sigmoid(a) * u).astype(BF16)
        y = jnp.dot(hid, wd_ref[0], preferred_element_type=F32)
        o_ref[0], o_ref[1] = _pack_rows(y)


def _experts(xs, tile_expert, tile_row, w_gate, w_up, w_down, interpret):
    p = xs.shape[1]
    n_tiles = p // EXP_TM
    grid_spec = pltpu.PrefetchScalarGridSpec(
        num_scalar_prefetch=2,
        grid=(n_tiles,),
        in_specs=[
            pl.BlockSpec((2, EXP_TM, SC_ROW_WORDS), lambda i, te, tr: (0, tr[i], 0)),
            pl.BlockSpec((1, D_MODEL, EXPERT_FF), lambda i, te, tr: (te[i], 0, 0)),
            pl.BlockSpec((1, D_MODEL, EXPERT_FF), lambda i, te, tr: (te[i], 0, 0)),
            pl.BlockSpec((1, EXPERT_FF, D_MODEL), lambda i, te, tr: (te[i], 0, 0)),
        ],
        out_specs=pl.BlockSpec((2, EXP_TM, SC_ROW_WORDS), lambda i, te, tr: (0, tr[i], 0)),
    )
    return pl.pallas_call(
        _expert_kernel,
        grid_spec=grid_spec,
        out_shape=jax.ShapeDtypeStruct((2, p, SC_ROW_WORDS), jnp.int32),
        compiler_params=_cparams(("arbitrary",)),
        interpret=interpret,
        name="experts",
    )(tile_expert, tile_row, xs, w_gate, w_up, w_down)


def _final_kernel(x1_ref, yab_ref, route_ref, g_ref, o_ref):
    route = route_ref[...]
    w1 = route[:, 2:3]
    w2 = route[:, 3:4]
    ya = _unpack_rows(yab_ref[0], yab_ref[2]).astype(F32)
    yb = _unpack_rows(yab_ref[1], yab_ref[3]).astype(F32)
    x2 = x1_ref[...] + w1 * ya + w2 * yb
    ms = jnp.mean(x2 * x2, axis=-1, keepdims=True)
    o_ref[...] = x2 * lax.rsqrt(ms + EPS) * g_ref[...]


def _final(x1, yab, route, g_final, interpret):
    n = x1.shape[0]
    tm = FIN_TM
    row = lambda i: (i, 0)
    return pl.pallas_call(
        _final_kernel,
        grid=(n // tm,),
        in_specs=[
            pl.BlockSpec((tm, D_MODEL), row),
            pl.BlockSpec((4, tm, SC_ROW_WORDS), lambda i: (0, i, 0)),
            pl.BlockSpec((tm, LANES), row),
            pl.BlockSpec((1, D_MODEL), lambda i: (0, 0)),
        ],
        out_specs=pl.BlockSpec((tm, D_MODEL), row),
        out_shape=jax.ShapeDtypeStruct((n, D_MODEL), F32),
        compiler_params=_cparams(("arbitrary",)),
        interpret=interpret,
        name="combine_final",
    )(x1, yab, route, g_final)


def _dest_kernel(offs_ref, route_t_ref, idx_ref, *, n_rows):
    route_t = route_t_ref[...]
    experts = route_t[0:2, :]
    dest = route_t[4:6, :].astype(jnp.int32)
    for e in range(N_EXPERTS):
        dest = dest + jnp.where(experts == float(e), offs_ref[e], 0)
    idx_ref[0:2, :] = dest
    idx_ref[2:4, :] = dest + n_rows


def _route_plan(route_t, counts, n, interpret):
    cnt = counts[0, ROUTER_OFF:ROUTER_OFF + N_EXPERTS].astype(jnp.int32)
    padded = ((cnt + EXP_TM - 1) // EXP_TM) * EXP_TM
    ends = jnp.cumsum(padded)
    offs = ends - padded
    n_rows = 2 * n + N_EXPERTS * EXP_TM
    idx4 = pl.pallas_call(
        functools.partial(_dest_kernel, n_rows=n_rows),
        grid_spec=pltpu.PrefetchScalarGridSpec(
            num_scalar_prefetch=1, grid=(1,),
            in_specs=[pl.BlockSpec(route_t.shape, lambda i, offs: (0, 0))],
            out_specs=pl.BlockSpec((4, n), lambda i, offs: (0, 0))),
        out_shape=jax.ShapeDtypeStruct((4, n), jnp.int32),
        interpret=interpret,
        name="route_dest",
    )(offs, route_t)
    tile_row = jnp.minimum(jnp.arange(n_rows // EXP_TM, dtype=jnp.int32), ends[-1] // EXP_TM - 1)
    tile_expert = jnp.sum((ends[None, :] <= (tile_row * EXP_TM)[:, None]).astype(jnp.int32), axis=1)
    return idx4, tile_expert, tile_row, n_rows


def _sc_mesh():
    return plsc.VectorSubcoreMesh(core_axis_name="core", subcore_axis_name="subcore")


def _sc_scatter_rows(rows, idx4, n_out):
    n_in, w = rows.shape
    nb = idx4.shape[1] // SC_WINDOW

    @functools.partial(pl.kernel, out_type=jax.ShapeDtypeStruct((n_out, w), rows.dtype), mesh=_sc_mesh(),
                       scratch_types=[], name="sc_scatter_rows")
    def scatter(x_hbm, ia_hbm, ib_hbm, o_hbm):
        def body(x_vmem, ia_vmem, ib_vmem):
            pltpu.sync_copy(x_vmem, o_hbm.at[ia_vmem.at[0]])
            pltpu.sync_copy(x_vmem, o_hbm.at[ib_vmem.at[0]])

        pltpu.emit_pipeline(
            body,
            grid=(n_in // SC_WINDOW,),
            in_specs=[pl.BlockSpec((SC_WINDOW, w), lambda i: (i, 0)),
                      pl.BlockSpec((1, SC_WINDOW), lambda i: (2 * (i // nb), i % nb)),
                      pl.BlockSpec((1, SC_WINDOW), lambda i: (2 * (i // nb) + 1, i % nb))],
            out_specs=[],
            core_axis_name=("core", "subcore"),
            dimension_semantics=(pltpu.PARALLEL,),
        )(x_hbm, ia_hbm, ib_hbm)

    return scatter(rows, idx4, idx4)


def _sc_gather_rows(table, idx4):
    nb = idx4.shape[1] // SC_WINDOW
    n_idx = idx4.shape[0] * idx4.shape[1]
    w = table.shape[1]

    @functools.partial(pl.kernel, out_type=jax.ShapeDtypeStruct((n_idx, w), table.dtype), mesh=_sc_mesh(),
                       scratch_types=[], name="sc_gather_rows")
    def gather(t_hbm, i_hbm, o_hbm):
        def body(i_vmem, o_vmem):
            pltpu.sync_copy(t_hbm.at[i_vmem.at[0]], o_vmem)

        pltpu.emit_pipeline(
            body,
            grid=(n_idx // SC_WINDOW,),
            in_specs=[pl.BlockSpec((1, SC_WINDOW), lambda i: (i // nb, i % nb))],
            out_specs=[pl.BlockSpec((SC_WINDOW, w), lambda i: (i, 0))],
            core_axis_name=("core", "subcore"),
            dimension_semantics=(pltpu.PARALLEL,),
        )(i_hbm, o_hbm)

    return gather(table, idx4)


def _forward(x, g_mix, w_in, w_attn_branch, w_ret_branch, w_out, g_ffn, w_group_router, b_group_router,
             w_expert_router, b_expert_router, w_gate, w_up, w_down, g_final, interpret=False):
    batch, seq, d = x.shape
    n = batch * seq
    x2 = x.reshape(n, d)
    nat_a, nat_b, qkv_dilated, wg_bf, wu_bf, wd_bf = _proj(x2, g_mix[0][None, :], w_in[0].astype(BF16), w_gate[0],
                                                           w_up[0], w_down[0], batch, seq, interpret)
    o_attn = _attention(nat_a, qkv_dilated, batch, seq, interpret)
    o_ret = _retention(nat_a, nat_b, batch, seq, interpret)
    pad = LANES - N_EXPERT_GROUPS - N_EXPERTS
    w_router = jnp.concatenate([w_group_router[0], w_expert_router[0], jnp.zeros((d, pad), F32)], axis=-1)
    w_router_hi = w_router.astype(BF16)
    w_router_lo = (w_router - w_router_hi.astype(F32)).astype(BF16)
    w_router2 = jnp.concatenate([w_router_hi, w_router_lo], axis=-1)
    b_router = jnp.concatenate([b_group_router[0], b_expert_router[0], jnp.zeros((pad,), F32)])[None, :]
    x1, h2p, route, route_t, counts = _mix(o_attn, o_ret, nat_b, x2, w_attn_branch[0].astype(BF16),
                                           w_ret_branch[0].astype(BF16), w_out[0].astype(BF16),
                                           g_ffn[0][None, :], w_router2, b_router, interpret)
    idx4, tile_expert, tile_row, n_rows = _route_plan(route_t, counts, n, interpret)
    xs = _sc_scatter_rows(h2p.reshape(2 * n, SC_ROW_WORDS), idx4, 2 * n_rows)
    ys = _experts(xs.reshape(2, n_rows, SC_ROW_WORDS), tile_expert, tile_row, wg_bf, wu_bf, wd_bf, interpret)
    yab = _sc_gather_rows(ys.reshape(2 * n_rows, SC_ROW_WORDS), idx4)
    out = _final(x1, yab.reshape(4, n, SC_ROW_WORDS), route, g_final[None, :], interpret)
    return out.reshape(batch, seq, d)


def kernel(x, g_mix, w_in, w_attn_branch, w_ret_branch, w_out, g_ffn, w_group_router, b_group_router,
           w_expert_router, b_expert_router, w_gate, w_up, w_down, g_final):
    return _forward(x, g_mix, w_in, w_attn_branch, w_ret_branch, w_out, g_ffn, w_group_router,
                    b_group_router, w_expert_router, b_expert_router, w_gate, w_up, w_down, g_final)
```

```python
import functools

import numpy as np
import jax
import jax.numpy as jnp
from jax import lax
from jax.experimental import pallas as pl
from jax.experimental.pallas import tpu as pltpu
from jax.experimental.pallas import tpu_sc as plsc

F32 = jnp.float32
BF16 = jnp.bfloat16

D_MODEL = 1024
ATTN_GROUPS = ((128, 1), (512, 4), (2048, 16))
N_GROUPS = len(ATTN_GROUPS)
ATTN_HEADS = 8
HEAD_DIM = 64
GROUP_W = ATTN_HEADS * HEAD_DIM
QKV_W = N_GROUPS * GROUP_W
RET_HEADS = 4
RET_DK = 128
RET_DV = 256
RET_CHUNK = 128
RET_TS = 1024
N_EXPERT_GROUPS = 4
EXPERTS_PER_GROUP = 8
N_EXPERTS = N_EXPERT_GROUPS * EXPERTS_PER_GROUP
EXPERT_FF = 512
EPS = 1e-6

LANES = 128
BLK = 128
SPAN = 2048
NEG = -1e30
ACC_PARTS = 3

COL_QA = 0
COL_KA = COL_QA + QKV_W
COL_VA = COL_KA + QKV_W
COL_QR = COL_VA + QKV_W
COL_KR = COL_QR + RET_HEADS * RET_DK
COL_VR = COL_KR + RET_HEADS * RET_DK
COL_GR = COL_VR + RET_HEADS * RET_DV
COL_GATE_A = COL_GR + RET_HEADS * RET_DV
COL_GATE_R = COL_GATE_A + D_MODEL
IN_WIDTH = COL_GATE_R + D_MODEL
COL_BLOCK = 256

PROJ_TM = 512
PROJ_TN = IN_WIDTH // 2
MXU_N = 256
MIX_TM = 1024
MIX_CHUNK = 512
ROUTE_CHUNK = 256
ROUTE_ROWS = 8
EXP_TM = 512
EXP_PAIR = 2
SC_WINDOW = 128
SC_ROW_WORDS = 256
FIN_TM = 1024
VMEM_LIMIT = 56 * 1024 * 1024


def _cparams(sem):
    return pltpu.CompilerParams(dimension_semantics=sem, vmem_limit_bytes=VMEM_LIMIT)


def _sigmoid(x):
    return 0.5 * jnp.tanh(0.5 * x) + 0.5


def _proj_plans():
    plan_a, nat_col = [], 0
    for c in range(PROJ_TN // MXU_N):
        col = c * MXU_N
        section, within = divmod(col, QKV_W)
        gi = within // GROUP_W
        if col < COL_QR and gi > 0:
            plan_a.append((None, (gi, section * GROUP_W + within % GROUP_W)))
        else:
            plan_a.append((nat_col, None))
            nat_col += MXU_N
    plan_b = [(c * MXU_N, None) for c in range((IN_WIDTH - PROJ_TN) // MXU_N)]
    return plan_a, nat_col, plan_b


def _proj_kernel(*refs, plan, n_cast):
    has_dilated = any(d is not None for _, d in plan)
    x_ref, g_ref, w_ref = refs[:3]
    cast_in = refs[3:3 + n_cast]
    o_ref = refs[3 + n_cast]
    rest = refs[4 + n_cast:]
    if has_dilated:
        d4_ref, d16_ref = rest[:2]
        stage_ref, mid_ref = rest[2 + n_cast:]
        rest = rest[2:]
    cast_out = rest[:n_cast]

    x = x_ref[...]
    ms = jnp.mean(x * x, axis=-1, keepdims=True)
    h = (x * lax.rsqrt(ms + EPS) * g_ref[...]).astype(BF16)
    step = ATTN_GROUPS[1][1]
    quarter = PROJ_TM // step
    for c, (nat_col, dil) in enumerate(plan):
        res = jnp.dot(h, w_ref[:, c * MXU_N:(c + 1) * MXU_N], preferred_element_type=F32)
        if nat_col is not None:
            o_ref[:, nat_col:nat_col + MXU_N] = res.astype(o_ref.dtype)
        if dil is None:
            continue
        gi, col = dil
        slot = c % 2
        for lt in range(MXU_N // LANES):
            stage_ref[slot, lt] = res[:, lt * LANES:(lt + 1) * LANES]
        for lt in range(MXU_N // LANES):
            cols = slice(col + lt * LANES, col + (lt + 1) * LANES)
            for b in range(step):
                rows = stage_ref[slot, lt, pl.ds(b, quarter, stride=step), :]
                if gi == 1:
                    d4_ref[0, b, :, cols] = rows.astype(BF16)
                else:
                    mid_ref[slot, lt, b * quarter:(b + 1) * quarter, :] = rows
            if gi == 2:
                for b in range(step):
                    for a in range(step):
                        rows = mid_ref[slot, lt, pl.ds(b * quarter + a, quarter // step, stride=step), :]
                        d16_ref[0, a * step + b, :, cols] = rows.astype(BF16)

    for src, dst in zip(cast_in, cast_out):
        dst[...] = src[...].astype(BF16)


def _proj(x2, g_mix, w_in_bf16, w_gate, w_up, w_down, batch, seq, interpret):
    n = x2.shape[0]
    n_i = n // PROJ_TM
    tiles = seq // PROJ_TM
    plan_a, width_a, plan_b = _proj_plans()
    dils = [ATTN_GROUPS[gi][1] for gi in (1, 2)]
    row = lambda i: (i, 0)

    def call(plan, col_block, nat_width, casts, name):
        flat = [w.reshape(-1, w.shape[-1]) for w in casts]
        w_specs = [pl.BlockSpec((w.shape[0] // n_i, w.shape[1]), row) for w in flat]
        dilated = any(d is not None for _, d in plan)
        d_specs = [pl.BlockSpec((1, d, PROJ_TM // d, 3 * GROUP_W), lambda i: (i // tiles, 0, i % tiles, 0))
                   for d in dils] if dilated else []
        d_shapes = [jax.ShapeDtypeStruct((batch, d, seq // d, 3 * GROUP_W), BF16) for d in dils] if dilated else []
        stage = pltpu.VMEM((2, MXU_N // LANES, PROJ_TM, LANES), F32)
        return pl.pallas_call(
            functools.partial(_proj_kernel, plan=plan, n_cast=len(casts)),
            grid=(n_i,),
            in_specs=[pl.BlockSpec((PROJ_TM, D_MODEL), row),
                      pl.BlockSpec((1, D_MODEL), lambda i: (0, 0)),
                      pl.BlockSpec((D_MODEL, PROJ_TN), lambda i: (0, col_block))] + w_specs,
            out_specs=[pl.BlockSpec((PROJ_TM, nat_width), row)] + d_specs + w_specs,
            out_shape=[jax.ShapeDtypeStruct((n, nat_width), BF16)] + d_shapes
            + [jax.ShapeDtypeStruct(w.shape, BF16) for w in flat],
            scratch_shapes=[stage, stage] if dilated else [],
            compiler_params=_cparams(("arbitrary",)),
            interpret=interpret,
            name=name,
        )(x2, g_mix, w_in_bf16, *flat)

    nat_a, qkv_d4, qkv_d16, wg_bf, wu_bf = call(plan_a, 0, width_a, (w_gate, w_up), "proj_attn")
    nat_b, wd_bf = call(plan_b, 1, IN_WIDTH - PROJ_TN, (w_down,), "proj_rest")
    return (nat_a, nat_b, (qkv_d4, qkv_d16), wg_bf.reshape(w_gate.shape), wu_bf.reshape(w_up.shape),
            wd_bf.reshape(w_down.shape))


def _attn_unit(q2, kk, vv, bias_a, bias_b):
    lane = lax.broadcasted_iota(jnp.int32, (BLK, LANES), 1)
    left = lane < HEAD_DIM
    zero = jnp.zeros_like(q2)
    nt = (((1,), (1,)), ((), ()))
    q_st = jnp.concatenate([jnp.where(left, q2, zero), jnp.where(left, zero, q2)], axis=0)
    s = lax.dot_general(q_st, kk, nt, preferred_element_type=F32) + jnp.concatenate([bias_a, bias_b], axis=0)
    m = jnp.max(s, axis=-1, keepdims=True)
    p = jnp.exp(s - m)
    den = jnp.sum(p, axis=-1, keepdims=True)
    o = jnp.dot(p.astype(BF16), vv, preferred_element_type=F32)
    return (jnp.where(left, o[:BLK], o[BLK:]), jnp.where(left, m[:BLK], m[BLK:]),
            jnp.where(left, den[:BLK], den[BLK:]))


def _attn_kernel(q1_ref, k1_ref, v1_ref, q2_ref, k2_ref, v2_ref, q3_ref, k3_ref, v3_ref,
                 bias_ref, o_ref, acc_ref, accw_ref, *, seq):
    s_id = pl.program_id(2)
    step = ATTN_GROUPS[1][1]
    quarter = SPAN // step
    dilated = {1: (q2_ref, k2_ref, v2_ref), 2: (q3_ref, k3_ref, v3_ref)}

    def dilated_unit(gi, d, m, r, first):
        slot = gi - 1
        q_ref, k_ref, v_ref = dilated[gi]
        loc = BLK * m * d + r
        cur = pl.multiple_of(s_id * (SPAN // d) + BLK * m, BLK)
        prev = pl.multiple_of(jnp.where(first == 1, cur, cur - BLK), BLK)
        q2 = q_ref[0, r, BLK * m:BLK * (m + 1), :] * 0.125
        kk = jnp.concatenate([k_ref[0, r, pl.ds(prev, BLK), :], k_ref[0, r, pl.ds(cur, BLK), :]], axis=0)
        vv = jnp.concatenate([v_ref[0, r, pl.ds(prev, BLK), :], v_ref[0, r, pl.ds(cur, BLK), :]], axis=0)
        parts = _attn_unit(q2, kk, vv, bias_ref[gi, first, 0], bias_ref[gi, first, 1])
        for j, part in enumerate(parts):
            if d == step:
                acc_ref[ACC_PARTS * slot + j, pl.ds(loc, BLK, stride=d), :] = part
            else:
                a, b = divmod(r, step)
                accw_ref[j, pl.ds(b * quarter + a, BLK, stride=step), :] = part

    def reinterleave_pieces():
        pieces = []
        for j in range(ACC_PARTS):
            for b in range(step):
                for j0 in range(0, quarter, 256):
                    def piece(j=j, b=b, j0=j0):
                        acc_ref[ACC_PARTS + j, pl.ds(b + step * j0, 256, stride=step), :] = (
                            accw_ref[j, b * quarter + j0:b * quarter + j0 + 256, :])
                    pieces.append(piece)
        return pieces

    def dense_unit(m):
        loc = m * BLK
        cur = pl.multiple_of(s_id * SPAN + loc, BLK)
        prev = pl.multiple_of(jnp.maximum(cur - BLK, 0), BLK)
        first = jnp.where(cur == 0, 1, 0)
        q2 = q1_ref[loc:loc + BLK, :] * 0.125
        kk = jnp.concatenate([k1_ref[pl.ds(prev, BLK), :], k1_ref[pl.ds(cur, BLK), :]], axis=0)
        vv = jnp.concatenate([v1_ref[pl.ds(prev, BLK), :], v1_ref[pl.ds(cur, BLK), :]], axis=0)
        n1, m1, d1 = _attn_unit(q2, kk, vv, bias_ref[0, first, 0], bias_ref[0, first, 1])
        (n2, m2, d2), (n3, m3, d3) = (
            tuple(acc_ref[g * ACC_PARTS + j, loc:loc + BLK, :] for j in range(ACC_PARTS)) for g in range(2))
        mx = jnp.maximum(jnp.maximum(m1, m2), m3)
        w1, w2, w3 = jnp.exp(m1 - mx), jnp.exp(m2 - mx), jnp.exp(m3 - mx)
        num = w1 * n1 + w2 * n2 + w3 * n3
        den = w1 * d1 + w2 * d2 + w3 * d3
        o_ref[loc:loc + BLK, :] = (num / den).astype(o_ref.dtype)

    first_span = jnp.where(s_id == 0, 1, 0)
    for gi in (2, 1):
        d = ATTN_GROUPS[gi][1]
        for m in range(SPAN // (BLK * d)):
            for r in range(d):
                dilated_unit(gi, d, m, r, first_span if m == 0 else 0)
        if d != step:
            for piece in reinterleave_pieces():
                piece()
    for m in range(SPAN // BLK):
        dense_unit(m)


def _attn_bias():
    slopes = np.exp2(-8.0 * np.arange(1, ATTN_HEADS + 1, dtype=np.float64) / ATTN_HEADS)
    qi = np.arange(BLK)[:, None]
    kj = np.arange(2 * BLK)[None, :]
    rel = qi + BLK - kj
    out = np.zeros((N_GROUPS, 2, ATTN_HEADS, BLK, 2 * BLK), np.float32)
    for gi, (window, d) in enumerate(ATTN_GROUPS):
        n_back = window // d
        assert n_back == BLK
        valid = (rel >= 0) & (rel <= n_back)
        bias = -slopes[:, None, None] * (rel * d)[None].astype(np.float64)
        out[gi, 0] = np.where(valid[None], bias, NEG)
        out[gi, 1] = np.where((valid & (kj >= BLK))[None], bias, NEG)
    return jnp.asarray(out)


def _attention(nat_a, dilated, batch, seq, interpret):
    n = batch * seq
    spans = seq // SPAN
    n_hp = GROUP_W // LANES
    heads = GROUP_W // LANES
    specs = [pl.BlockSpec((SPAN, LANES), lambda b, hp, s: (b * spans + s, hp)),
             pl.BlockSpec((seq, LANES), lambda b, hp, s: (b, heads + hp)),
             pl.BlockSpec((seq, LANES), lambda b, hp, s: (b, 2 * heads + hp))]
    operands = [nat_a, nat_a, nat_a]
    for gi, arr in zip((1, 2), dilated):
        d = ATTN_GROUPS[gi][1]
        specs += [pl.BlockSpec((1, d, SPAN // d, LANES), lambda b, hp, s: (b, 0, s, hp)),
                  pl.BlockSpec((1, d, seq // d, LANES), lambda b, hp, s: (b, 0, 0, heads + hp)),
                  pl.BlockSpec((1, d, seq // d, LANES), lambda b, hp, s: (b, 0, 0, 2 * heads + hp))]
        operands += [arr, arr, arr]
    bias_spec = pl.BlockSpec((N_GROUPS, 2, 2, BLK, 2 * BLK), lambda b, hp, s: (0, 0, hp, 0, 0))
    return pl.pallas_call(
        functools.partial(_attn_kernel, seq=seq),
        grid=(batch, n_hp, spans),
        in_specs=specs + [bias_spec],
        out_specs=pl.BlockSpec((SPAN, LANES), lambda b, hp, s: (b * spans + s, hp)),
        out_shape=jax.ShapeDtypeStruct((n, GROUP_W), BF16),
        scratch_shapes=[
            pltpu.VMEM((2 * ACC_PARTS, SPAN, LANES), F32),
            pltpu.VMEM((ACC_PARTS, SPAN, LANES), F32),
        ],
        compiler_params=_cparams(("arbitrary", "arbitrary", "arbitrary")),
        interpret=interpret,
        name="attn",
    )(*operands, _attn_bias())


def _ret_kernel(*refs):
    heads_per_qk = COL_BLOCK // RET_DK
    n_qk = RET_HEADS // heads_per_qk
    q_refs, k_refs = refs[:n_qk], refs[n_qk:2 * n_qk]
    v_refs = refs[2 * n_qk:2 * n_qk + RET_HEADS]
    g_refs = refs[2 * n_qk + RET_HEADS:2 * n_qk + 2 * RET_HEADS]
    dec_ref, xi_ref, zeta_ref, gch_ref, o_ref, st_ref = refs[2 * n_qk + 2 * RET_HEADS:]
    @pl.when(pl.program_id(1) == 0)
    def _():
        st_ref[...] = jnp.zeros_like(st_ref)

    nt = (((1,), (1,)), ((), ()))
    scale = RET_DK ** -0.5

    for c in range(RET_TS // RET_CHUNK):
        rows = pl.ds(c * RET_CHUNK, RET_CHUNK)
        for h in range(RET_HEADS):
            kcols = slice((h % heads_per_qk) * RET_DK, (h % heads_per_qk + 1) * RET_DK)
            vcols = slice(h * RET_DV, (h + 1) * RET_DV)
            qi = q_refs[h // heads_per_qk][rows, kcols]
            kf = k_refs[h // heads_per_qk][rows, kcols].astype(F32) * scale
            ki = kf.astype(BF16)
            kz_t = jnp.transpose(kf * zeta_ref[h]).astype(BF16)
            vi = v_refs[h][rows, :]
            att = lax.dot_general(qi, ki, nt, preferred_element_type=F32) * dec_ref[h]
            inner = jnp.dot(att.astype(BF16), vi, preferred_element_type=F32)
            st = st_ref[h]
            cross = jnp.dot(qi, st.astype(BF16), preferred_element_type=F32) * xi_ref[h]
            st_ref[h] = gch_ref[h] * st + jnp.dot(kz_t, vi, preferred_element_type=F32)
            y = inner + cross
            mu = jnp.mean(y, axis=-1, keepdims=True)
            yc = y - mu
            var = jnp.mean(yc * yc, axis=-1, keepdims=True)
            yn = yc * lax.rsqrt(var + EPS)
            g = g_refs[h][rows, :].astype(F32)
            o_ref[rows, vcols] = (g * _sigmoid(g) * yn).astype(o_ref.dtype)


def _ret_tables():
    c = RET_CHUNK
    log_g = np.log1p(-np.exp2(-5.0 - np.arange(RET_HEADS, dtype=np.float64)))
    pos = np.arange(c, dtype=np.float64)
    diff = pos[:, None] - pos[None, :]
    dec = np.where(diff >= 0, np.exp(log_g[:, None, None] * np.maximum(diff, 0.0)), 0.0)
    xi = np.exp(log_g[:, None] * (pos + 1.0))[..., None] * np.ones((1, 1, RET_DV))
    zeta = np.exp(log_g[:, None] * (c - 1.0 - pos))[..., None] * np.ones((1, 1, RET_DK))
    gch = np.exp(log_g * c)[:, None, None] * np.ones((1, 1, RET_DV))
    return tuple(jnp.asarray(t, F32) for t in (dec, xi, zeta, gch))


def _retention(nat_a, nat_b, batch, seq, interpret):
    n = batch * seq
    dec, xi, zeta, gch = _ret_tables()
    v_w = RET_HEADS * RET_DV
    nts = seq // RET_TS
    const3 = lambda b, t: (0, 0, 0)

    def block(arr, col):
        base = 0 if arr is nat_a else PROJ_TN
        if arr is nat_a:
            col = _proj_plans()[1] - (PROJ_TN - col)
        idx = (col - base) // COL_BLOCK
        return arr, pl.BlockSpec((RET_TS, COL_BLOCK), lambda b, t: (b * nts + t, idx))

    qk_blocks = RET_HEADS * RET_DK // COL_BLOCK
    picks = [block(nat_a if COL_QR + k * COL_BLOCK < PROJ_TN else nat_b, COL_QR + k * COL_BLOCK)
             for k in range(qk_blocks)]
    picks += [block(nat_b, COL_KR + k * COL_BLOCK) for k in range(qk_blocks)]
    picks += [block(nat_b, COL_VR + h * RET_DV) for h in range(RET_HEADS)]
    picks += [block(nat_b, COL_GR + h * RET_DV) for h in range(RET_HEADS)]
    return pl.pallas_call(
        _ret_kernel,
        grid=(batch, nts),
        in_specs=[spec for _, spec in picks] + [
            pl.BlockSpec((RET_HEADS, RET_CHUNK, RET_CHUNK), const3),
            pl.BlockSpec((RET_HEADS, RET_CHUNK, RET_DV), const3),
            pl.BlockSpec((RET_HEADS, RET_CHUNK, RET_DK), const3),
            pl.BlockSpec((RET_HEADS, 1, RET_DV), const3),
        ],
        out_specs=pl.BlockSpec((RET_TS, v_w), lambda b, t: (b * nts + t, 0)),
        out_shape=jax.ShapeDtypeStruct((n, v_w), BF16),
        scratch_shapes=[pltpu.VMEM((RET_HEADS, RET_DK, RET_DV), F32)],
        compiler_params=_cparams(("arbitrary", "arbitrary")),
        interpret=interpret,
        name="retention",
    )(*[arr for arr, _ in picks], dec, xi, zeta, gch)


ROUTER_OFF = N_EXPERT_GROUPS


def _pack_bf16_pair(a, b):
    hi = lax.bitcast_convert_type(a.astype(BF16).astype(F32), jnp.uint32)
    lo = lax.bitcast_convert_type(b.astype(BF16).astype(F32), jnp.uint32)
    return lax.bitcast_convert_type(hi | (lo >> 16), jnp.int32)


def _unpack_bf16_pair(w):
    u = lax.bitcast_convert_type(w, jnp.uint32)
    a = lax.bitcast_convert_type(u & jnp.uint32(0xFFFF0000), F32).astype(BF16)
    b = lax.bitcast_convert_type(u << 16, F32).astype(BF16)
    return a, b


def _pack_rows(y):
    q = D_MODEL // 4
    return (_pack_bf16_pair(y[:, 0:q], y[:, 2 * q:3 * q]), _pack_bf16_pair(y[:, q:2 * q], y[:, 3 * q:4 * q]))


def _unpack_rows(slab0, slab1):
    q0, q2 = _unpack_bf16_pair(slab0)
    q1, q3 = _unpack_bf16_pair(slab1)
    return jnp.concatenate([q0, q1, q2, q3], axis=1)


def _mix_kernel(oa_ref, or_ref, *refs):
    n_gate = D_MODEL // COL_BLOCK
    ga_refs, gr_refs = refs[:n_gate], refs[n_gate:2 * n_gate]
    (x_ref, pa_ref, pr_ref, wo_ref, gf_ref, wr_ref, br_ref,
     x1_ref, h2_ref, route_ref, route_t_ref, cnt_ref, carry_ref, logit_ref) = refs[2 * n_gate:]
    step = pl.program_id(0)

    @pl.when(step == 0)
    def _():
        carry_ref[...] = jnp.zeros_like(carry_ref)
        logit_ref[...] = jnp.zeros_like(logit_ref)

    routing = iter([functools.partial(_route_rows, pl.ds(c * ROUTE_CHUNK, ROUTE_CHUNK), step > 0, logit_ref,
                                      route_ref, route_t_ref, cnt_ref, carry_ref)
                    for c in range(MIX_TM // ROUTE_CHUNK)])
    for c in range(MIX_TM // MIX_CHUNK):
        for _ in _mix_rows(pl.ds(c * MIX_CHUNK, MIX_CHUNK), oa_ref, or_ref, ga_refs, gr_refs,
                           x_ref, pa_ref, pr_ref, wo_ref, gf_ref, wr_ref, br_ref, x1_ref, h2_ref, logit_ref):
            next(routing, lambda: None)()
    for piece in routing:
        piece()


def _mix_rows(rows, oa_ref, or_ref, ga_refs, gr_refs, x_ref, pa_ref, pr_ref, wo_ref, gf_ref, wr_ref, br_ref,
              x1_ref, h2_ref, logit_ref):
    a = jnp.dot(oa_ref[rows, :], pa_ref[...], preferred_element_type=F32)
    yield
    r = jnp.dot(or_ref[rows, :], pr_ref[...], preferred_element_type=F32)
    yield
    merged = jnp.concatenate(
        [_sigmoid(ga[rows, :].astype(F32)) * a[:, k * COL_BLOCK:(k + 1) * COL_BLOCK]
         + _sigmoid(gr[rows, :].astype(F32)) * r[:, k * COL_BLOCK:(k + 1) * COL_BLOCK]
         for k, (ga, gr) in enumerate(zip(ga_refs, gr_refs))], axis=1)
    x1 = x_ref[rows, :] + jnp.dot(merged.astype(BF16), wo_ref[...], preferred_element_type=F32)
    x1_ref[rows, :] = x1
    ms = jnp.mean(x1 * x1, axis=-1, keepdims=True)
    h2 = x1 * lax.rsqrt(ms + EPS) * gf_ref[...]
    h2_ref[0, rows, :], h2_ref[1, rows, :] = _pack_rows(h2)

    h_hi = h2.astype(BF16)
    h_lo = (h2 - h_hi.astype(F32)).astype(BF16)
    both = jnp.dot(h_hi, wr_ref[...], preferred_element_type=F32)
    logit_ref[rows, :] = (both[:, :LANES] + both[:, LANES:]
                          + jnp.dot(h_lo, wr_ref[:, :LANES], preferred_element_type=F32) + br_ref[...])


def _route_rows(rows, live, logit_ref, route_ref, route_t_ref, cnt_ref, carry_ref):
    logits = logit_ref[rows, :]
    tm = logits.shape[0]
    lane = lax.broadcasted_iota(jnp.int32, (tm, LANES), 1).astype(F32)
    big = jnp.float32(4 * LANES)
    ninf = -jnp.inf
    is_g = lane < N_EXPERT_GROUPS
    gl = jnp.where(is_g, logits, ninf)
    gmax = jnp.max(gl, axis=-1, keepdims=True)
    gsum = jnp.sum(jnp.where(is_g, jnp.exp(gl - gmax), 0.0), axis=-1, keepdims=True)
    g_val = 1.0 / gsum
    g_idx = jnp.min(jnp.where(jnp.logical_and(is_g, gl == gmax), lane, big), axis=-1, keepdims=True)
    lo = ROUTER_OFF + EXPERTS_PER_GROUP * g_idx
    in_grp = jnp.logical_and(lane >= lo, lane < lo + EXPERTS_PER_GROUP)
    el = jnp.where(in_grp, logits, ninf)
    v1 = jnp.max(el, axis=-1, keepdims=True)
    i1 = jnp.min(jnp.where(jnp.logical_and(in_grp, el == v1), lane, big), axis=-1, keepdims=True)
    rest = jnp.logical_and(in_grp, lane != i1)
    el2 = jnp.where(rest, logits, ninf)
    v2 = jnp.max(el2, axis=-1, keepdims=True)
    i2 = jnp.min(jnp.where(jnp.logical_and(rest, el2 == v2), lane, big), axis=-1, keepdims=True)
    t = jnp.exp(v2 - v1)
    w1 = g_val / (1.0 + t)
    w2 = g_val * t / (1.0 + t)

    sel = jnp.logical_or(lane == i1, lane == i2)
    sel_bf = jnp.where(sel, 1.0, 0.0).astype(BF16)
    row = lax.broadcasted_iota(jnp.int32, (tm, tm), 0)
    col = lax.broadcasted_iota(jnp.int32, (tm, tm), 1)
    tri = jnp.where(col < row, 1.0, 0.0).astype(BF16)
    before = jnp.dot(tri, sel_bf, preferred_element_type=F32) + carry_ref[...]
    r1 = jnp.sum(jnp.where(lane == i1, before, 0.0), axis=-1, keepdims=True)
    r2 = jnp.sum(jnp.where(lane == i2, before, 0.0), axis=-1, keepdims=True)
    carry = carry_ref[...] + jnp.where(live, jnp.sum(jnp.where(sel, 1.0, 0.0), axis=0, keepdims=True), 0.0)
    carry_ref[...] = carry
    cnt_ref[...] = carry

    vals = (i1 - ROUTER_OFF, i2 - ROUTER_OFF, w1, w2, r1, r2)
    route = jnp.zeros((tm, LANES), F32)
    for j, v in enumerate(vals):
        route = jnp.where(lane == j, v, route)
    route_ref[rows, :] = route
    route_t_ref[:, rows] = jnp.transpose(route)[:ROUTE_ROWS, :]


def _mix(o_attn, o_ret, nat_b, x2, pa, pr, wo, g_ffn, w_router, b_router, interpret):
    n = x2.shape[0]
    tm = MIX_TM
    last = n // tm - 1
    n_gate = D_MODEL // COL_BLOCK
    const = lambda i: (0, 0)
    cur = lambda i: jnp.minimum(i, last)
    prev = lambda i: jnp.maximum(i - 1, 0)
    return pl.pallas_call(
        _mix_kernel,
        grid=(n // tm + 1,),
        in_specs=[
            pl.BlockSpec((tm, GROUP_W), lambda i: (cur(i), 0)),
            pl.BlockSpec((tm, D_MODEL), lambda i: (cur(i), 0)),
        ] + [
            pl.BlockSpec((tm, COL_BLOCK), functools.partial(lambda i, idx: (cur(i), idx), idx=(col - PROJ_TN) // COL_BLOCK + k))
            for col in (COL_GATE_A, COL_GATE_R) for k in range(n_gate)
        ] + [
            pl.BlockSpec((tm, D_MODEL), lambda i: (cur(i), 0)),
            pl.BlockSpec((GROUP_W, D_MODEL), const),
            pl.BlockSpec((D_MODEL, D_MODEL), const),
            pl.BlockSpec((D_MODEL, D_MODEL), const),
            pl.BlockSpec((1, D_MODEL), const),
            pl.BlockSpec((D_MODEL, 2 * LANES), const),
            pl.BlockSpec((1, LANES), const),
        ],
        out_specs=[
            pl.BlockSpec((tm, D_MODEL), lambda i: (cur(i), 0)),
            pl.BlockSpec((2, tm, SC_ROW_WORDS), lambda i: (0, cur(i), 0)),
            pl.BlockSpec((tm, LANES), lambda i: (prev(i), 0)),
            pl.BlockSpec((ROUTE_ROWS, tm), lambda i: (0, prev(i))),
            pl.BlockSpec((1, LANES), const),
        ],
        out_shape=[
            jax.ShapeDtypeStruct((n, D_MODEL), F32),
            jax.ShapeDtypeStruct((2, n, SC_ROW_WORDS), jnp.int32),
            jax.ShapeDtypeStruct((n, LANES), F32),
            jax.ShapeDtypeStruct((ROUTE_ROWS, n), F32),
            jax.ShapeDtypeStruct((1, LANES), F32),
        ],
        scratch_shapes=[pltpu.VMEM((1, LANES), F32), pltpu.VMEM((tm, LANES), F32)],
        compiler_params=_cparams(("arbitrary",)),
        interpret=interpret,
        name="mix_router",
    )(o_attn, o_ret, *([nat_b] * (2 * n_gate)), x2, pa, pr, wo, g_ffn, w_router, b_router)


def _expert_kernel(te_ref, pr_ref, nu_ref, xs_ref, *refs):
    j = pl.program_id(0)
    o_ref = refs[-1]
    live = pr_ref[j] == j
    for h in range(EXP_PAIR):
        wg_ref, wu_ref, wd_ref = refs[3 * h:3 * h + 3]
        rows = pl.ds(h * EXP_TM, EXP_TM)

        @pl.when(jnp.logical_and(live, EXP_PAIR * j + h < nu_ref[0]))
        def _():
            xs = _unpack_rows(xs_ref[0, rows, :], xs_ref[1, rows, :])
            a = jnp.dot(xs, wg_ref[0], preferred_element_type=F32)
            u = jnp.dot(xs, wu_ref[0], preferred_element_type=F32)
            hid = (a * _sigmoid(a) * u).astype(BF16)
            y = jnp.dot(hid, wd_ref[0], preferred_element_type=F32)
            o_ref[0, rows, :], o_ref[1, rows, :] = _pack_rows(y)


def _experts(xs, tile_expert, pair_row, n_used, w_gate, w_up, w_down, interpret):
    p = xs.shape[1]
    n_steps = p // (EXP_TM * EXP_PAIR)
    tile = lambda j, pr, h: EXP_PAIR * pr[j] + h
    w_specs = []
    for h in range(EXP_PAIR):
        pick = functools.partial(lambda j, te, pr, nu, h: (te[tile(j, pr, h)], 0, 0), h=h)
        w_specs += [pl.BlockSpec((1, D_MODEL, EXPERT_FF), pick), pl.BlockSpec((1, D_MODEL, EXPERT_FF), pick),
                    pl.BlockSpec((1, EXPERT_FF, D_MODEL), pick)]
    rows_spec = pl.BlockSpec((2, EXP_TM * EXP_PAIR, SC_ROW_WORDS), lambda j, te, pr, nu: (0, pr[j], 0))
    grid_spec = pltpu.PrefetchScalarGridSpec(
        num_scalar_prefetch=3,
        grid=(n_steps,),
        in_specs=[rows_spec] + w_specs,
        out_specs=rows_spec,
    )
    return pl.pallas_call(
        _expert_kernel,
        grid_spec=grid_spec,
        out_shape=jax.ShapeDtypeStruct((2, p, SC_ROW_WORDS), jnp.int32),
        compiler_params=_cparams(("arbitrary",)),
        interpret=interpret,
        name="experts",
    )(tile_expert, pair_row, n_used, xs, *([w_gate, w_up, w_down] * EXP_PAIR))


def _final_kernel(x1_ref, yab_ref, route_ref, g_ref, o_ref):
    route = route_ref[...]
    w1 = route[:, 2:3]
    w2 = route[:, 3:4]
    ya = _unpack_rows(yab_ref[0], yab_ref[2]).astype(F32)
    yb = _unpack_rows(yab_ref[1], yab_ref[3]).astype(F32)
    x2 = x1_ref[...] + w1 * ya + w2 * yb
    ms = jnp.mean(x2 * x2, axis=-1, keepdims=True)
    o_ref[...] = x2 * lax.rsqrt(ms + EPS) * g_ref[...]


def _final(x1, yab, route, g_final, interpret):
    n = x1.shape[0]
    tm = FIN_TM
    row = lambda i: (i, 0)
    return pl.pallas_call(
        _final_kernel,
        grid=(n // tm,),
        in_specs=[
            pl.BlockSpec((tm, D_MODEL), row),
            pl.BlockSpec((4, tm, SC_ROW_WORDS), lambda i: (0, i, 0)),
            pl.BlockSpec((tm, LANES), row),
            pl.BlockSpec((1, D_MODEL), lambda i: (0, 0)),
        ],
        out_specs=pl.BlockSpec((tm, D_MODEL), row),
        out_shape=jax.ShapeDtypeStruct((n, D_MODEL), F32),
        compiler_params=_cparams(("arbitrary",)),
        interpret=interpret,
        name="combine_final",
    )(x1, yab, route, g_final)


def _dest_kernel(offs_ref, route_t_ref, idx_ref, *, n_rows):
    route_t = route_t_ref[...]
    experts = route_t[0:2, :]
    dest = route_t[4:6, :].astype(jnp.int32)
    for e in range(N_EXPERTS):
        dest = dest + jnp.where(experts == float(e), offs_ref[e], 0)
    idx_ref[0:2, :] = dest
    idx_ref[2:4, :] = dest + n_rows


def _route_plan(route_t, counts, n, interpret):
    cnt = counts[0, ROUTER_OFF:ROUTER_OFF + N_EXPERTS].astype(jnp.int32)
    padded = ((cnt + EXP_TM - 1) // EXP_TM) * EXP_TM
    ends = jnp.cumsum(padded)
    offs = ends - padded
    n_rows = 2 * n + N_EXPERTS * EXP_TM
    idx4 = pl.pallas_call(
        functools.partial(_dest_kernel, n_rows=n_rows),
        grid_spec=pltpu.PrefetchScalarGridSpec(
            num_scalar_prefetch=1, grid=(1,),
            in_specs=[pl.BlockSpec(route_t.shape, lambda i, offs: (0, 0))],
            out_specs=pl.BlockSpec((4, n), lambda i, offs: (0, 0))),
        out_shape=jax.ShapeDtypeStruct((4, n), jnp.int32),
        interpret=interpret,
        name="route_dest",
    )(offs, route_t)
    n_used = ends[-1] // EXP_TM
    tile_row = jnp.minimum(jnp.arange(n_rows // EXP_TM, dtype=jnp.int32), n_used - 1)
    tile_expert = jnp.sum((ends[None, :] <= (tile_row * EXP_TM)[:, None]).astype(jnp.int32), axis=1)
    pair_row = jnp.minimum(jnp.arange(n_rows // (EXP_TM * EXP_PAIR), dtype=jnp.int32), (n_used - 1) // EXP_PAIR)
    return idx4, tile_expert, pair_row, n_used[None], n_rows


def _sc_mesh():
    return plsc.VectorSubcoreMesh(core_axis_name="core", subcore_axis_name="subcore")


def _sc_scatter_rows(rows, idx4, n_out):
    n_in, w = rows.shape
    nb = idx4.shape[1] // SC_WINDOW

    @functools.partial(pl.kernel, out_type=jax.ShapeDtypeStruct((n_out, w), rows.dtype), mesh=_sc_mesh(),
                       scratch_types=[], name="sc_scatter_rows")
    def scatter(x_hbm, ia_hbm, ib_hbm, o_hbm):
        def body(x_vmem, ia_vmem, ib_vmem):
            pltpu.sync_copy(x_vmem, o_hbm.at[ia_vmem.at[0]])
            pltpu.sync_copy(x_vmem, o_hbm.at[ib_vmem.at[0]])

        pltpu.emit_pipeline(
            body,
            grid=(n_in // SC_WINDOW,),
            in_specs=[pl.BlockSpec((SC_WINDOW, w), lambda i: (i, 0)),
                      pl.BlockSpec((1, SC_WINDOW), lambda i: (2 * (i // nb), i % nb)),
                      pl.BlockSpec((1, SC_WINDOW), lambda i: (2 * (i // nb) + 1, i % nb))],
            out_specs=[],
            core_axis_name=("core", "subcore"),
            dimension_semantics=(pltpu.PARALLEL,),
        )(x_hbm, ia_hbm, ib_hbm)

    return scatter(rows, idx4, idx4)


def _sc_gather_rows(table, idx4):
    nb = idx4.shape[1] // SC_WINDOW
    n_idx = idx4.shape[0] * idx4.shape[1]
    w = table.shape[1]

    @functools.partial(pl.kernel, out_type=jax.ShapeDtypeStruct((n_idx, w), table.dtype), mesh=_sc_mesh(),
                       scratch_types=[], name="sc_gather_rows")
    def gather(t_hbm, i_hbm, o_hbm):
        def body(i_vmem, o_vmem):
            pltpu.sync_copy(t_hbm.at[i_vmem.at[0]], o_vmem)

        pltpu.emit_pipeline(
            body,
            grid=(n_idx // SC_WINDOW,),
            in_specs=[pl.BlockSpec((1, SC_WINDOW), lambda i: (i // nb, i % nb))],
            out_specs=[pl.BlockSpec((SC_WINDOW, w), lambda i: (i, 0))],
            core_axis_name=("core", "subcore"),
            dimension_semantics=(pltpu.PARALLEL,),
        )(i_hbm, o_hbm)

    return gather(table, idx4)


def _forward(x, g_mix, w_in, w_attn_branch, w_ret_branch, w_out, g_ffn, w_group_router, b_group_router,
             w_expert_router, b_expert_router, w_gate, w_up, w_down, g_final, interpret=False):
    batch, seq, d = x.shape
    n = batch * seq
    x2 = x.reshape(n, d)
    nat_a, nat_b, qkv_dilated, wg_bf, wu_bf, wd_bf = _proj(x2, g_mix[0][None, :], w_in[0].astype(BF16), w_gate[0],
                                                           w_up[0], w_down[0], batch, seq, interpret)
    o_attn = _attention(nat_a, qkv_dilated, batch, seq, interpret)
    o_ret = _retention(nat_a, nat_b, batch, seq, interpret)
    pad = LANES - N_EXPERT_GROUPS - N_EXPERTS
    w_router = jnp.concatenate([w_group_router[0], w_expert_router[0], jnp.zeros((d, pad), F32)], axis=-1)
    w_router_hi = w_router.astype(BF16)
    w_router_lo = (w_router - w_router_hi.astype(F32)).astype(BF16)
    w_router2 = jnp.concatenate([w_router_hi, w_router_lo], axis=-1)
    b_router = jnp.concatenate([b_group_router[0], b_expert_router[0], jnp.zeros((pad,), F32)])[None, :]
    x1, h2p, route, route_t, counts = _mix(o_attn, o_ret, nat_b, x2, w_attn_branch[0].astype(BF16),
                                           w_ret_branch[0].astype(BF16), w_out[0].astype(BF16),
                                           g_ffn[0][None, :], w_router2, b_router, interpret)
    idx4, tile_expert, pair_row, n_used, n_rows = _route_plan(route_t, counts, n, interpret)
    xs = _sc_scatter_rows(h2p.reshape(2 * n, SC_ROW_WORDS), idx4, 2 * n_rows)
    ys = _experts(xs.reshape(2, n_rows, SC_ROW_WORDS), tile_expert, pair_row, n_used, wg_bf, wu_bf, wd_bf, interpret)
    yab = _sc_gather_rows(ys.reshape(2 * n_rows, SC_ROW_WORDS), idx4)
    out = _final(x1, yab.reshape(4, n, SC_ROW_WORDS), route, g_final[None, :], interpret)
    return out.reshape(batch, seq, d)


def kernel(x, g_mix, w_in, w_attn_branch, w_ret_branch, w_out, g_ffn, w_group_router, b_group_router,
           w_expert_router, b_expert_router, w_gate, w_up, w_down, g_final):
    return _forward(x, g_mix, w_in, w_attn_branch, w_ret_branch, w_out, g_ffn, w_group_router,
                    b_group_router, w_expert_router, b_expert_router, w_gate, w_up, w_down, g_final)
```

```python
import functools

import numpy as np
import jax
import jax.numpy as jnp
from jax import lax
from jax.experimental import pallas as pl
from jax.experimental.pallas import tpu as pltpu
from jax.experimental.pallas import tpu_sc as plsc

F32 = jnp.float32
BF16 = jnp.bfloat16

D_MODEL = 1024
ATTN_GROUPS = ((128, 1), (512, 4), (2048, 16))
N_GROUPS = len(ATTN_GROUPS)
ATTN_HEADS = 8
HEAD_DIM = 64
GROUP_W = ATTN_HEADS * HEAD_DIM
QKV_W = N_GROUPS * GROUP_W
RET_HEADS = 4
RET_DK = 128
RET_DV = 256
RET_CHUNK = 128
RET_TS = 1024
N_EXPERT_GROUPS = 4
EXPERTS_PER_GROUP = 8
N_EXPERTS = N_EXPERT_GROUPS * EXPERTS_PER_GROUP
EXPERT_FF = 512
EPS = 1e-6

LANES = 128
BLK = 128
SPAN = 2048
NEG = -1e30
ACC_PARTS = 3

COL_QA = 0
COL_KA = COL_QA + QKV_W
COL_VA = COL_KA + QKV_W
COL_QR = COL_VA + QKV_W
COL_KR = COL_QR + RET_HEADS * RET_DK
COL_VR = COL_KR + RET_HEADS * RET_DK
COL_GR = COL_VR + RET_HEADS * RET_DV
COL_GATE_A = COL_GR + RET_HEADS * RET_DV
COL_GATE_R = COL_GATE_A + D_MODEL
IN_WIDTH = COL_GATE_R + D_MODEL
COL_BLOCK = 256

PROJ_TM = 512
PROJ_TN = IN_WIDTH // 2
MXU_N = 256
MIX_TM = 1024
MIX_CHUNK = 512
ROUTE_CHUNK = 256
ROUTE_ROWS = 8
EXP_TM = 512
EXP_PAIR = 4
SC_WINDOW = 128
SC_ROW_WORDS = 256
FIN_TM = 1024
VMEM_LIMIT = 56 * 1024 * 1024


def _cparams(sem):
    return pltpu.CompilerParams(dimension_semantics=sem, vmem_limit_bytes=VMEM_LIMIT)


def _sigmoid(x):
    return 0.5 * jnp.tanh(0.5 * x) + 0.5


def _proj_plans():
    plan_a, nat_col = [], 0
    for c in range(PROJ_TN // MXU_N):
        col = c * MXU_N
        section, within = divmod(col, QKV_W)
        gi = within // GROUP_W
        if col < COL_QR and gi > 0:
            plan_a.append((None, (gi, section * GROUP_W + within % GROUP_W)))
        else:
            plan_a.append((nat_col, None))
            nat_col += MXU_N
    plan_b = [(c * MXU_N, None) for c in range((IN_WIDTH - PROJ_TN) // MXU_N)]
    return plan_a, nat_col, plan_b


def _proj_kernel(*refs, plan, n_cast):
    has_dilated = any(d is not None for _, d in plan)
    x_ref, g_ref, w_ref = refs[:3]
    cast_in = refs[3:3 + n_cast]
    o_ref = refs[3 + n_cast]
    rest = refs[4 + n_cast:]
    if has_dilated:
        d4_ref, d16_ref = rest[:2]
        stage_ref, mid_ref = rest[2 + n_cast:]
        rest = rest[2:]
    cast_out = rest[:n_cast]

    x = x_ref[...]
    ms = jnp.mean(x * x, axis=-1, keepdims=True)
    h = (x * lax.rsqrt(ms + EPS) * g_ref[...]).astype(BF16)
    step = ATTN_GROUPS[1][1]
    quarter = PROJ_TM // step
    for c, (nat_col, dil) in enumerate(plan):
        res = jnp.dot(h, w_ref[:, c * MXU_N:(c + 1) * MXU_N], preferred_element_type=F32)
        if nat_col is not None:
            o_ref[:, nat_col:nat_col + MXU_N] = res.astype(o_ref.dtype)
        if dil is None:
            continue
        gi, col = dil
        slot = c % 2
        for lt in range(MXU_N // LANES):
            stage_ref[slot, lt] = res[:, lt * LANES:(lt + 1) * LANES]
        for lt in range(MXU_N // LANES):
            cols = slice(col + lt * LANES, col + (lt + 1) * LANES)
            for b in range(step):
                rows = stage_ref[slot, lt, pl.ds(b, quarter, stride=step), :]
                if gi == 1:
                    d4_ref[0, b, :, cols] = rows.astype(BF16)
                else:
                    mid_ref[slot, lt, b * quarter:(b + 1) * quarter, :] = rows
            if gi == 2:
                for b in range(step):
                    for a in range(step):
                        rows = mid_ref[slot, lt, pl.ds(b * quarter + a, quarter // step, stride=step), :]
                        d16_ref[0, a * step + b, :, cols] = rows.astype(BF16)

    for src, dst in zip(cast_in, cast_out):
        dst[...] = src[...].astype(BF16)


def _proj(x2, g_mix, w_in_bf16, w_gate, w_up, w_down, batch, seq, interpret):
    n = x2.shape[0]
    n_i = n // PROJ_TM
    tiles = seq // PROJ_TM
    plan_a, width_a, plan_b = _proj_plans()
    dils = [ATTN_GROUPS[gi][1] for gi in (1, 2)]
    row = lambda i: (i, 0)

    def call(plan, col_block, nat_width, casts, name):
        flat = [w.reshape(-1, w.shape[-1]) for w in casts]
        w_specs = [pl.BlockSpec((w.shape[0] // n_i, w.shape[1]), row) for w in flat]
        dilated = any(d is not None for _, d in plan)
        d_specs = [pl.BlockSpec((1, d, PROJ_TM // d, 3 * GROUP_W), lambda i: (i // tiles, 0, i % tiles, 0))
                   for d in dils] if dilated else []
        d_shapes = [jax.ShapeDtypeStruct((batch, d, seq // d, 3 * GROUP_W), BF16) for d in dils] if dilated else []
        stage = pltpu.VMEM((2, MXU_N // LANES, PROJ_TM, LANES), F32)
        return pl.pallas_call(
            functools.partial(_proj_kernel, plan=plan, n_cast=len(casts)),
            grid=(n_i,),
            in_specs=[pl.BlockSpec((PROJ_TM, D_MODEL), row),
                      pl.BlockSpec((1, D_MODEL), lambda i: (0, 0)),
                      pl.BlockSpec((D_MODEL, PROJ_TN), lambda i: (0, col_block))] + w_specs,
            out_specs=[pl.BlockSpec((PROJ_TM, nat_width), row)] + d_specs + w_specs,
            out_shape=[jax.ShapeDtypeStruct((n, nat_width), BF16)] + d_shapes
            + [jax.ShapeDtypeStruct(w.shape, BF16) for w in flat],
            scratch_shapes=[stage, stage] if dilated else [],
            compiler_params=_cparams(("arbitrary",)),
            interpret=interpret,
            name=name,
        )(x2, g_mix, w_in_bf16, *flat)

    nat_a, qkv_d4, qkv_d16, wg_bf, wu_bf = call(plan_a, 0, width_a, (w_gate, w_up), "proj_attn")
    nat_b, wd_bf = call(plan_b, 1, IN_WIDTH - PROJ_TN, (w_down,), "proj_rest")
    return (nat_a, nat_b, (qkv_d4, qkv_d16), wg_bf.reshape(w_gate.shape), wu_bf.reshape(w_up.shape),
            wd_bf.reshape(w_down.shape))


def _attn_unit(q2, kk, vv, bias_a, bias_b):
    lane = lax.broadcasted_iota(jnp.int32, (BLK, LANES), 1)
    left = lane < HEAD_DIM
    zero = jnp.zeros_like(q2)
    nt = (((1,), (1,)), ((), ()))
    q_st = jnp.concatenate([jnp.where(left, q2, zero), jnp.where(left, zero, q2)], axis=0)
    s = lax.dot_general(q_st, kk, nt, preferred_element_type=F32) + jnp.concatenate([bias_a, bias_b], axis=0)
    m = jnp.max(s, axis=-1, keepdims=True)
    p = jnp.exp(s - m)
    den = jnp.sum(p, axis=-1, keepdims=True)
    o = jnp.dot(p.astype(BF16), vv, preferred_element_type=F32)
    return (jnp.where(left, o[:BLK], o[BLK:]), jnp.where(left, m[:BLK], m[BLK:]),
            jnp.where(left, den[:BLK], den[BLK:]))


def _attn_kernel(q1_ref, k1_ref, v1_ref, q2_ref, k2_ref, v2_ref, q3_ref, k3_ref, v3_ref,
                 bias_ref, o_ref, acc_ref, accw_ref, *, seq):
    s_id = pl.program_id(2)
    step = ATTN_GROUPS[1][1]
    quarter = SPAN // step
    dilated = {1: (q2_ref, k2_ref, v2_ref), 2: (q3_ref, k3_ref, v3_ref)}

    def dilated_unit(gi, d, m, r, first):
        slot = gi - 1
        q_ref, k_ref, v_ref = dilated[gi]
        loc = BLK * m * d + r
        cur = pl.multiple_of(s_id * (SPAN // d) + BLK * m, BLK)
        prev = pl.multiple_of(jnp.where(first == 1, cur, cur - BLK), BLK)
        q2 = q_ref[0, r, BLK * m:BLK * (m + 1), :] * 0.125
        kk = jnp.concatenate([k_ref[0, r, pl.ds(prev, BLK), :], k_ref[0, r, pl.ds(cur, BLK), :]], axis=0)
        vv = jnp.concatenate([v_ref[0, r, pl.ds(prev, BLK), :], v_ref[0, r, pl.ds(cur, BLK), :]], axis=0)
        parts = _attn_unit(q2, kk, vv, bias_ref[gi, first, 0], bias_ref[gi, first, 1])
        for j, part in enumerate(parts):
            if d == step:
                acc_ref[ACC_PARTS * slot + j, pl.ds(loc, BLK, stride=d), :] = part
            else:
                a, b = divmod(r, step)
                accw_ref[j, pl.ds(b * quarter + a, BLK, stride=step), :] = part

    def reinterleave_pieces():
        pieces = []
        for j in range(ACC_PARTS):
            for b in range(step):
                for j0 in range(0, quarter, 256):
                    def piece(j=j, b=b, j0=j0):
                        acc_ref[ACC_PARTS + j, pl.ds(b + step * j0, 256, stride=step), :] = (
                            accw_ref[j, b * quarter + j0:b * quarter + j0 + 256, :])
                    pieces.append(piece)
        return pieces

    def dense_unit(m):
        loc = m * BLK
        cur = pl.multiple_of(s_id * SPAN + loc, BLK)
        prev = pl.multiple_of(jnp.maximum(cur - BLK, 0), BLK)
        first = jnp.where(cur == 0, 1, 0)
        q2 = q1_ref[loc:loc + BLK, :] * 0.125
        kk = jnp.concatenate([k1_ref[pl.ds(prev, BLK), :], k1_ref[pl.ds(cur, BLK), :]], axis=0)
        vv = jnp.concatenate([v1_ref[pl.ds(prev, BLK), :], v1_ref[pl.ds(cur, BLK), :]], axis=0)
        n1, m1, d1 = _attn_unit(q2, kk, vv, bias_ref[0, first, 0], bias_ref[0, first, 1])
        (n2, m2, d2), (n3, m3, d3) = (
            tuple(acc_ref[g * ACC_PARTS + j, loc:loc + BLK, :] for j in range(ACC_PARTS)) for g in range(2))
        mx = jnp.maximum(jnp.maximum(m1, m2), m3)
        w1, w2, w3 = jnp.exp(m1 - mx), jnp.exp(m2 - mx), jnp.exp(m3 - mx)
        num = w1 * n1 + w2 * n2 + w3 * n3
        den = w1 * d1 + w2 * d2 + w3 * d3
        o_ref[loc:loc + BLK, :] = (num / den).astype(o_ref.dtype)

    first_span = jnp.where(s_id == 0, 1, 0)
    for gi in (2, 1):
        d = ATTN_GROUPS[gi][1]
        for m in range(SPAN // (BLK * d)):
            for r in range(d):
                dilated_unit(gi, d, m, r, first_span if m == 0 else 0)
        if d != step:
            for piece in reinterleave_pieces():
                piece()
    for m in range(SPAN // BLK):
        dense_unit(m)


def _attn_bias():
    slopes = np.exp2(-8.0 * np.arange(1, ATTN_HEADS + 1, dtype=np.float64) / ATTN_HEADS)
    qi = np.arange(BLK)[:, None]
    kj = np.arange(2 * BLK)[None, :]
    rel = qi + BLK - kj
    out = np.zeros((N_GROUPS, 2, ATTN_HEADS, BLK, 2 * BLK), np.float32)
    for gi, (window, d) in enumerate(ATTN_GROUPS):
        n_back = window // d
        assert n_back == BLK
        valid = (rel >= 0) & (rel <= n_back)
        bias = -slopes[:, None, None] * (rel * d)[None].astype(np.float64)
        out[gi, 0] = np.where(valid[None], bias, NEG)
        out[gi, 1] = np.where((valid & (kj >= BLK))[None], bias, NEG)
    return jnp.asarray(out)


def _attention(nat_a, dilated, batch, seq, interpret):
    n = batch * seq
    spans = seq // SPAN
    n_hp = GROUP_W // LANES
    heads = GROUP_W // LANES
    specs = [pl.BlockSpec((SPAN, LANES), lambda b, hp, s: (b * spans + s, hp)),
             pl.BlockSpec((seq, LANES), lambda b, hp, s: (b, heads + hp)),
             pl.BlockSpec((seq, LANES), lambda b, hp, s: (b, 2 * heads + hp))]
    operands = [nat_a, nat_a, nat_a]
    for gi, arr in zip((1, 2), dilated):
        d = ATTN_GROUPS[gi][1]
        specs += [pl.BlockSpec((1, d, SPAN // d, LANES), lambda b, hp, s: (b, 0, s, hp)),
                  pl.BlockSpec((1, d, seq // d, LANES), lambda b, hp, s: (b, 0, 0, heads + hp)),
                  pl.BlockSpec((1, d, seq // d, LANES), lambda b, hp, s: (b, 0, 0, 2 * heads + hp))]
        operands += [arr, arr, arr]
    bias_spec = pl.BlockSpec((N_GROUPS, 2, 2, BLK, 2 * BLK), lambda b, hp, s: (0, 0, hp, 0, 0))
    return pl.pallas_call(
        functools.partial(_attn_kernel, seq=seq),
        grid=(batch, n_hp, spans),
        in_specs=specs + [bias_spec],
        out_specs=pl.BlockSpec((SPAN, LANES), lambda b, hp, s: (b * spans + s, hp)),
        out_shape=jax.ShapeDtypeStruct((n, GROUP_W), BF16),
        scratch_shapes=[
            pltpu.VMEM((2 * ACC_PARTS, SPAN, LANES), F32),
            pltpu.VMEM((ACC_PARTS, SPAN, LANES), F32),
        ],
        compiler_params=_cparams(("arbitrary", "arbitrary", "arbitrary")),
        interpret=interpret,
        name="attn",
    )(*operands, _attn_bias())


def _ret_kernel(*refs):
    heads_per_qk = COL_BLOCK // RET_DK
    n_qk = RET_HEADS // heads_per_qk
    q_refs, k_refs = refs[:n_qk], refs[n_qk:2 * n_qk]
    v_refs = refs[2 * n_qk:2 * n_qk + RET_HEADS]
    g_refs = refs[2 * n_qk + RET_HEADS:2 * n_qk + 2 * RET_HEADS]
    dec_ref, xi_ref, zeta_ref, gch_ref, o_ref, st_ref = refs[2 * n_qk + 2 * RET_HEADS:]
    @pl.when(pl.program_id(1) == 0)
    def _():
        st_ref[...] = jnp.zeros_like(st_ref)

    nt = (((1,), (1,)), ((), ()))
    scale = RET_DK ** -0.5

    for c in range(RET_TS // RET_CHUNK):
        rows = pl.ds(c * RET_CHUNK, RET_CHUNK)
        for h in range(RET_HEADS):
            kcols = slice((h % heads_per_qk) * RET_DK, (h % heads_per_qk + 1) * RET_DK)
            vcols = slice(h * RET_DV, (h + 1) * RET_DV)
            qi = q_refs[h // heads_per_qk][rows, kcols]
            kf = k_refs[h // heads_per_qk][rows, kcols].astype(F32) * scale
            ki = kf.astype(BF16)
            kz_t = jnp.transpose(kf * zeta_ref[h]).astype(BF16)
            vi = v_refs[h][rows, :]
            att = lax.dot_general(qi, ki, nt, preferred_element_type=F32) * dec_ref[h]
            inner = jnp.dot(att.astype(BF16), vi, preferred_element_type=F32)
            st = st_ref[h]
            cross = jnp.dot(qi, st.astype(BF16), preferred_element_type=F32) * xi_ref[h]
            st_ref[h] = gch_ref[h] * st + jnp.dot(kz_t, vi, preferred_element_type=F32)
            y = inner + cross
            mu = jnp.mean(y, axis=-1, keepdims=True)
            yc = y - mu
            var = jnp.mean(yc * yc, axis=-1, keepdims=True)
            yn = yc * lax.rsqrt(var + EPS)
            g = g_refs[h][rows, :].astype(F32)
            o_ref[rows, vcols] = (g * _sigmoid(g) * yn).astype(o_ref.dtype)


def _ret_tables():
    c = RET_CHUNK
    log_g = np.log1p(-np.exp2(-5.0 - np.arange(RET_HEADS, dtype=np.float64)))
    pos = np.arange(c, dtype=np.float64)
    diff = pos[:, None] - pos[None, :]
    dec = np.where(diff >= 0, np.exp(log_g[:, None, None] * np.maximum(diff, 0.0)), 0.0)
    xi = np.exp(log_g[:, None] * (pos + 1.0))[..., None] * np.ones((1, 1, RET_DV))
    zeta = np.exp(log_g[:, None] * (c - 1.0 - pos))[..., None] * np.ones((1, 1, RET_DK))
    gch = np.exp(log_g * c)[:, None, None] * np.ones((1, 1, RET_DV))
    return tuple(jnp.asarray(t, F32) for t in (dec, xi, zeta, gch))


def _retention(nat_a, nat_b, batch, seq, interpret):
    n = batch * seq
    dec, xi, zeta, gch = _ret_tables()
    v_w = RET_HEADS * RET_DV
    nts = seq // RET_TS
    const3 = lambda b, t: (0, 0, 0)

    def block(arr, col):
        base = 0 if arr is nat_a else PROJ_TN
        if arr is nat_a:
            col = _proj_plans()[1] - (PROJ_TN - col)
        idx = (col - base) // COL_BLOCK
        return arr, pl.BlockSpec((RET_TS, COL_BLOCK), lambda b, t: (b * nts + t, idx))

    qk_blocks = RET_HEADS * RET_DK // COL_BLOCK
    picks = [block(nat_a if COL_QR + k * COL_BLOCK < PROJ_TN else nat_b, COL_QR + k * COL_BLOCK)
             for k in range(qk_blocks)]
    picks += [block(nat_b, COL_KR + k * COL_BLOCK) for k in range(qk_blocks)]
    picks += [block(nat_b, COL_VR + h * RET_DV) for h in range(RET_HEADS)]
    picks += [block(nat_b, COL_GR + h * RET_DV) for h in range(RET_HEADS)]
    return pl.pallas_call(
        _ret_kernel,
        grid=(batch, nts),
        in_specs=[spec for _, spec in picks] + [
            pl.BlockSpec((RET_HEADS, RET_CHUNK, RET_CHUNK), const3),
            pl.BlockSpec((RET_HEADS, RET_CHUNK, RET_DV), const3),
            pl.BlockSpec((RET_HEADS, RET_CHUNK, RET_DK), const3),
            pl.BlockSpec((RET_HEADS, 1, RET_DV), const3),
        ],
        out_specs=pl.BlockSpec((RET_TS, v_w), lambda b, t: (b * nts + t, 0)),
        out_shape=jax.ShapeDtypeStruct((n, v_w), BF16),
        scratch_shapes=[pltpu.VMEM((RET_HEADS, RET_DK, RET_DV), F32)],
        compiler_params=_cparams(("arbitrary", "arbitrary")),
        interpret=interpret,
        name="retention",
    )(*[arr for arr, _ in picks], dec, xi, zeta, gch)


ROUTER_OFF = N_EXPERT_GROUPS


def _pack_bf16_pair(a, b):
    hi = lax.bitcast_convert_type(a.astype(BF16).astype(F32), jnp.uint32)
    lo = lax.bitcast_convert_type(b.astype(BF16).astype(F32), jnp.uint32)
    return lax.bitcast_convert_type(hi | (lo >> 16), jnp.int32)


def _unpack_bf16_pair(w):
    u = lax.bitcast_convert_type(w, jnp.uint32)
    a = lax.bitcast_convert_type(u & jnp.uint32(0xFFFF0000), F32).astype(BF16)
    b = lax.bitcast_convert_type(u << 16, F32).astype(BF16)
    return a, b


def _pack_rows(y):
    q = D_MODEL // 4
    return (_pack_bf16_pair(y[:, 0:q], y[:, 2 * q:3 * q]), _pack_bf16_pair(y[:, q:2 * q], y[:, 3 * q:4 * q]))


def _unpack_rows(slab0, slab1):
    q0, q2 = _unpack_bf16_pair(slab0)
    q1, q3 = _unpack_bf16_pair(slab1)
    return jnp.concatenate([q0, q1, q2, q3], axis=1)


def _mix_kernel(oa_ref, or_ref, *refs):
    n_gate = D_MODEL // COL_BLOCK
    ga_refs, gr_refs = refs[:n_gate], refs[n_gate:2 * n_gate]
    (x_ref, pa_ref, pr_ref, wo_ref, gf_ref, wr_ref, br_ref,
     x1_ref, h2_ref, route_ref, route_t_ref, cnt_ref, carry_ref, logit_ref) = refs[2 * n_gate:]
    step = pl.program_id(0)

    @pl.when(step == 0)
    def _():
        carry_ref[...] = jnp.zeros_like(carry_ref)
        logit_ref[...] = jnp.zeros_like(logit_ref)

    routing = iter([functools.partial(_route_rows, pl.ds(c * ROUTE_CHUNK, ROUTE_CHUNK), step > 0, logit_ref,
                                      route_ref, route_t_ref, cnt_ref, carry_ref)
                    for c in range(MIX_TM // ROUTE_CHUNK)])
    for c in range(MIX_TM // MIX_CHUNK):
        for _ in _mix_rows(pl.ds(c * MIX_CHUNK, MIX_CHUNK), oa_ref, or_ref, ga_refs, gr_refs,
                           x_ref, pa_ref, pr_ref, wo_ref, gf_ref, wr_ref, br_ref, x1_ref, h2_ref, logit_ref):
            next(routing, lambda: None)()
    for piece in routing:
        piece()


def _mix_rows(rows, oa_ref, or_ref, ga_refs, gr_refs, x_ref, pa_ref, pr_ref, wo_ref, gf_ref, wr_ref, br_ref,
              x1_ref, h2_ref, logit_ref):
    a = jnp.dot(oa_ref[rows, :], pa_ref[...], preferred_element_type=F32)
    yield
    r = jnp.dot(or_ref[rows, :], pr_ref[...], preferred_element_type=F32)
    yield
    merged = jnp.concatenate(
        [_sigmoid(ga[rows, :].astype(F32)) * a[:, k * COL_BLOCK:(k + 1) * COL_BLOCK]
         + _sigmoid(gr[rows, :].astype(F32)) * r[:, k * COL_BLOCK:(k + 1) * COL_BLOCK]
         for k, (ga, gr) in enumerate(zip(ga_refs, gr_refs))], axis=1)
    x1 = x_ref[rows, :] + jnp.dot(merged.astype(BF16), wo_ref[...], preferred_element_type=F32)
    x1_ref[rows, :] = x1
    ms = jnp.mean(x1 * x1, axis=-1, keepdims=True)
    h2 = x1 * lax.rsqrt(ms + EPS) * gf_ref[...]
    h2_ref[0, rows, :], h2_ref[1, rows, :] = _pack_rows(h2)

    h_hi = h2.astype(BF16)
    h_lo = (h2 - h_hi.astype(F32)).astype(BF16)
    both = jnp.dot(h_hi, wr_ref[...], preferred_element_type=F32)
    logit_ref[rows, :] = (both[:, :LANES] + both[:, LANES:]
                          + jnp.dot(h_lo, wr_ref[:, :LANES], preferred_element_type=F32) + br_ref[...])


def _route_rows(rows, live, logit_ref, route_ref, route_t_ref, cnt_ref, carry_ref):
    logits = logit_ref[rows, :]
    tm = logits.shape[0]
    lane = lax.broadcasted_iota(jnp.int32, (tm, LANES), 1).astype(F32)
    big = jnp.float32(4 * LANES)
    ninf = -jnp.inf
    is_g = lane < N_EXPERT_GROUPS
    gl = jnp.where(is_g, logits, ninf)
    gmax = jnp.max(gl, axis=-1, keepdims=True)
    gsum = jnp.sum(jnp.where(is_g, jnp.exp(gl - gmax), 0.0), axis=-1, keepdims=True)
    g_val = 1.0 / gsum
    g_idx = jnp.min(jnp.where(jnp.logical_and(is_g, gl == gmax), lane, big), axis=-1, keepdims=True)
    lo = ROUTER_OFF + EXPERTS_PER_GROUP * g_idx
    in_grp = jnp.logical_and(lane >= lo, lane < lo + EXPERTS_PER_GROUP)
    el = jnp.where(in_grp, logits, ninf)
    v1 = jnp.max(el, axis=-1, keepdims=True)
    i1 = jnp.min(jnp.where(jnp.logical_and(in_grp, el == v1), lane, big), axis=-1, keepdims=True)
    rest = jnp.logical_and(in_grp, lane != i1)
    el2 = jnp.where(rest, logits, ninf)
    v2 = jnp.max(el2, axis=-1, keepdims=True)
    i2 = jnp.min(jnp.where(jnp.logical_and(rest, el2 == v2), lane, big), axis=-1, keepdims=True)
    t = jnp.exp(v2 - v1)
    w1 = g_val / (1.0 + t)
    w2 = g_val * t / (1.0 + t)

    sel = jnp.logical_or(lane == i1, lane == i2)
    sel_bf = jnp.where(sel, 1.0, 0.0).astype(BF16)
    row = lax.broadcasted_iota(jnp.int32, (tm, tm), 0)
    col = lax.broadcasted_iota(jnp.int32, (tm, tm), 1)
    tri = jnp.where(col < row, 1.0, 0.0).astype(BF16)
    before = jnp.dot(tri, sel_bf, preferred_element_type=F32) + carry_ref[...]
    r1 = jnp.sum(jnp.where(lane == i1, before, 0.0), axis=-1, keepdims=True)
    r2 = jnp.sum(jnp.where(lane == i2, before, 0.0), axis=-1, keepdims=True)
    carry = carry_ref[...] + jnp.where(live, jnp.sum(jnp.where(sel, 1.0, 0.0), axis=0, keepdims=True), 0.0)
    carry_ref[...] = carry
    cnt_ref[...] = carry

    vals = (i1 - ROUTER_OFF, i2 - ROUTER_OFF, w1, w2, r1, r2)
    route = jnp.zeros((tm, LANES), F32)
    for j, v in enumerate(vals):
        route = jnp.where(lane == j, v, route)
    route_ref[rows, :] = route
    route_t_ref[:, rows] = jnp.transpose(route)[:ROUTE_ROWS, :]


def _mix(o_attn, o_ret, nat_b, x2, pa, pr, wo, g_ffn, w_router, b_router, interpret):
    n = x2.shape[0]
    tm = MIX_TM
    last = n // tm - 1
    n_gate = D_MODEL // COL_BLOCK
    const = lambda i: (0, 0)
    cur = lambda i: jnp.minimum(i, last)
    prev = lambda i: jnp.maximum(i - 1, 0)
    return pl.pallas_call(
        _mix_kernel,
        grid=(n // tm + 1,),
        in_specs=[
            pl.BlockSpec((tm, GROUP_W), lambda i: (cur(i), 0)),
            pl.BlockSpec((tm, D_MODEL), lambda i: (cur(i), 0)),
        ] + [
            pl.BlockSpec((tm, COL_BLOCK), functools.partial(lambda i, idx: (cur(i), idx), idx=(col - PROJ_TN) // COL_BLOCK + k))
            for col in (COL_GATE_A, COL_GATE_R) for k in range(n_gate)
        ] + [
            pl.BlockSpec((tm, D_MODEL), lambda i: (cur(i), 0)),
            pl.BlockSpec((GROUP_W, D_MODEL), const),
            pl.BlockSpec((D_MODEL, D_MODEL), const),
            pl.BlockSpec((D_MODEL, D_MODEL), const),
            pl.BlockSpec((1, D_MODEL), const),
            pl.BlockSpec((D_MODEL, 2 * LANES), const),
            pl.BlockSpec((1, LANES), const),
        ],
        out_specs=[
            pl.BlockSpec((tm, D_MODEL), lambda i: (cur(i), 0)),
            pl.BlockSpec((2, tm, SC_ROW_WORDS), lambda i: (0, cur(i), 0)),
            pl.BlockSpec((tm, LANES), lambda i: (prev(i), 0)),
            pl.BlockSpec((ROUTE_ROWS, tm), lambda i: (0, prev(i))),
            pl.BlockSpec((1, LANES), const),
        ],
        out_shape=[
            jax.ShapeDtypeStruct((n, D_MODEL), F32),
            jax.ShapeDtypeStruct((2, n, SC_ROW_WORDS), jnp.int32),
            jax.ShapeDtypeStruct((n, LANES), F32),
            jax.ShapeDtypeStruct((ROUTE_ROWS, n), F32),
            jax.ShapeDtypeStruct((1, LANES), F32),
        ],
        scratch_shapes=[pltpu.VMEM((1, LANES), F32), pltpu.VMEM((tm, LANES), F32)],
        compiler_params=_cparams(("arbitrary",)),
        interpret=interpret,
        name="mix_router",
    )(o_attn, o_ret, *([nat_b] * (2 * n_gate)), x2, pa, pr, wo, g_ffn, w_router, b_router)


def _expert_kernel(te_ref, pr_ref, nu_ref, xs_ref, *refs):
    j = pl.program_id(0)
    o_ref = refs[-1]
    live = pr_ref[j] == j
    for h in range(EXP_PAIR):
        wg_ref, wu_ref, wd_ref = refs[3 * h:3 * h + 3]
        rows = pl.ds(h * EXP_TM, EXP_TM)

        @pl.when(jnp.logical_and(live, EXP_PAIR * j + h < nu_ref[0]))
        def _():
            xs = _unpack_rows(xs_ref[0, rows, :], xs_ref[1, rows, :])
            a = jnp.dot(xs, wg_ref[0], preferred_element_type=F32)
            u = jnp.dot(xs, wu_ref[0], preferred_element_type=F32)
            hid = (a * _sigmoid(a) * u).astype(BF16)
            y = jnp.dot(hid, wd_ref[0], preferred_element_type=F32)
            o_ref[0, rows, :], o_ref[1, rows, :] = _pack_rows(y)


def _experts(xs, tile_expert, pair_row, n_used, w_gate, w_up, w_down, interpret):
    p = xs.shape[1]
    n_steps = p // (EXP_TM * EXP_PAIR)
    tile = lambda j, pr, h: EXP_PAIR * pr[j] + h
    w_specs = []
    for h in range(EXP_PAIR):
        pick = functools.partial(lambda j, te, pr, nu, h: (te[tile(j, pr, h)], 0, 0), h=h)
        w_specs += [pl.BlockSpec((1, D_MODEL, EXPERT_FF), pick), pl.BlockSpec((1, D_MODEL, EXPERT_FF), pick),
                    pl.BlockSpec((1, EXPERT_FF, D_MODEL), pick)]
    rows_spec = pl.BlockSpec((2, EXP_TM * EXP_PAIR, SC_ROW_WORDS), lambda j, te, pr, nu: (0, pr[j], 0))
    grid_spec = pltpu.PrefetchScalarGridSpec(
        num_scalar_prefetch=3,
        grid=(n_steps,),
        in_specs=[rows_spec] + w_specs,
        out_specs=rows_spec,
    )
    return pl.pallas_call(
        _expert_kernel,
        grid_spec=grid_spec,
        out_shape=jax.ShapeDtypeStruct((2, p, SC_ROW_WORDS), jnp.int32),
        compiler_params=_cparams(("arbitrary",)),
        interpret=interpret,
        name="experts",
    )(tile_expert, pair_row, n_used, xs, *([w_gate, w_up, w_down] * EXP_PAIR))


def _final_kernel(x1_ref, yab_ref, route_ref, g_ref, o_ref):
    route = route_ref[...]
    w1 = route[:, 2:3]
    w2 = route[:, 3:4]
    ya = _unpack_rows(yab_ref[0], yab_ref[2]).astype(F32)
    yb = _unpack_rows(yab_ref[1], yab_ref[3]).astype(F32)
    x2 = x1_ref[...] + w1 * ya + w2 * yb
    ms = jnp.mean(x2 * x2, axis=-1, keepdims=True)
    o_ref[...] = x2 * lax.rsqrt(ms + EPS) * g_ref[...]


def _final(x1, yab, route, g_final, interpret):
    n = x1.shape[0]
    tm = FIN_TM
    row = lambda i: (i, 0)
    return pl.pallas_call(
        _final_kernel,
        grid=(n // tm,),
        in_specs=[
            pl.BlockSpec((tm, D_MODEL), row),
            pl.BlockSpec((4, tm, SC_ROW_WORDS), lambda i: (0, i, 0)),
            pl.BlockSpec((tm, LANES), row),
            pl.BlockSpec((1, D_MODEL), lambda i: (0, 0)),
        ],
        out_specs=pl.BlockSpec((tm, D_MODEL), row),
        out_shape=jax.ShapeDtypeStruct((n, D_MODEL), F32),
        compiler_params=_cparams(("arbitrary",)),
        interpret=interpret,
        name="combine_final",
    )(x1, yab, route, g_final)


def _dest_kernel(offs_ref, route_t_ref, idx_ref, *, n_rows):
    route_t = route_t_ref[...]
    experts = route_t[0:2, :]
    dest = route_t[4:6, :].astype(jnp.int32)
    for e in range(N_EXPERTS):
        dest = dest + jnp.where(experts == float(e), offs_ref[e], 0)
    idx_ref[0:2, :] = dest
    idx_ref[2:4, :] = dest + n_rows


def _route_plan(route_t, counts, n, interpret):
    cnt = counts[0, ROUTER_OFF:ROUTER_OFF + N_EXPERTS].astype(jnp.int32)
    padded = ((cnt + EXP_TM - 1) // EXP_TM) * EXP_TM
    ends = jnp.cumsum(padded)
    offs = ends - padded
    n_rows = 2 * n + N_EXPERTS * EXP_TM
    idx4 = pl.pallas_call(
        functools.partial(_dest_kernel, n_rows=n_rows),
        grid_spec=pltpu.PrefetchScalarGridSpec(
            num_scalar_prefetch=1, grid=(1,),
            in_specs=[pl.BlockSpec(route_t.shape, lambda i, offs: (0, 0))],
            out_specs=pl.BlockSpec((4, n), lambda i, offs: (0, 0))),
        out_shape=jax.ShapeDtypeStruct((4, n), jnp.int32),
        interpret=interpret,
        name="route_dest",
    )(offs, route_t)
    n_used = ends[-1] // EXP_TM
    tile_row = jnp.minimum(jnp.arange(n_rows // EXP_TM, dtype=jnp.int32), n_used - 1)
    tile_expert = jnp.sum((ends[None, :] <= (tile_row * EXP_TM)[:, None]).astype(jnp.int32), axis=1)
    pair_row = jnp.minimum(jnp.arange(n_rows // (EXP_TM * EXP_PAIR), dtype=jnp.int32), (n_used - 1) // EXP_PAIR)
    return idx4, tile_expert, pair_row, n_used[None], n_rows


def _sc_mesh():
    return plsc.VectorSubcoreMesh(core_axis_name="core", subcore_axis_name="subcore")


def _sc_scatter_rows(rows, idx4, n_out):
    n_in, w = rows.shape
    nb = idx4.shape[1] // SC_WINDOW

    @functools.partial(pl.kernel, out_type=jax.ShapeDtypeStruct((n_out, w), rows.dtype), mesh=_sc_mesh(),
                       scratch_types=[], name="sc_scatter_rows")
    def scatter(x_hbm, ia_hbm, ib_hbm, o_hbm):
        def body(x_vmem, ia_vmem, ib_vmem):
            pltpu.sync_copy(x_vmem, o_hbm.at[ia_vmem.at[0]])
            pltpu.sync_copy(x_vmem, o_hbm.at[ib_vmem.at[0]])

        pltpu.emit_pipeline(
            body,
            grid=(n_in // SC_WINDOW,),
            in_specs=[pl.BlockSpec((SC_WINDOW, w), lambda i: (i, 0)),
                      pl.BlockSpec((1, SC_WINDOW), lambda i: (2 * (i // nb), i % nb)),
                      pl.BlockSpec((1, SC_WINDOW), lambda i: (2 * (i // nb) + 1, i % nb))],
            out_specs=[],
            core_axis_name=("core", "subcore"),
            dimension_semantics=(pltpu.PARALLEL,),
        )(x_hbm, ia_hbm, ib_hbm)

    return scatter(rows, idx4, idx4)


def _sc_gather_rows(table, idx4):
    nb = idx4.shape[1] // SC_WINDOW
    n_idx = idx4.shape[0] * idx4.shape[1]
    w = table.shape[1]

    @functools.partial(pl.kernel, out_type=jax.ShapeDtypeStruct((n_idx, w), table.dtype), mesh=_sc_mesh(),
                       scratch_types=[], name="sc_gather_rows")
    def gather(t_hbm, i_hbm, o_hbm):
        def body(i_vmem, o_vmem):
            pltpu.sync_copy(t_hbm.at[i_vmem.at[0]], o_vmem)

        pltpu.emit_pipeline(
            body,
            grid=(n_idx // SC_WINDOW,),
            in_specs=[pl.BlockSpec((1, SC_WINDOW), lambda i: (i // nb, i % nb))],
            out_specs=[pl.BlockSpec((SC_WINDOW, w), lambda i: (i, 0))],
            core_axis_name=("core", "subcore"),
            dimension_semantics=(pltpu.PARALLEL,),
        )(i_hbm, o_hbm)

    return gather(table, idx4)


def _forward(x, g_mix, w_in, w_attn_branch, w_ret_branch, w_out, g_ffn, w_group_router, b_group_router,
             w_expert_router, b_expert_router, w_gate, w_up, w_down, g_final, interpret=False):
    batch, seq, d = x.shape
    n = batch * seq
    x2 = x.reshape(n, d)
    nat_a, nat_b, qkv_dilated, wg_bf, wu_bf, wd_bf = _proj(x2, g_mix[0][None, :], w_in[0].astype(BF16), w_gate[0],
                                                           w_up[0], w_down[0], batch, seq, interpret)
    o_attn = _attention(nat_a, qkv_dilated, batch, seq, interpret)
    o_ret = _retention(nat_a, nat_b, batch, seq, interpret)
    pad = LANES - N_EXPERT_GROUPS - N_EXPERTS
    w_router = jnp.concatenate([w_group_router[0], w_expert_router[0], jnp.zeros((d, pad), F32)], axis=-1)
    w_router_hi = w_router.astype(BF16)
    w_router_lo = (w_router - w_router_hi.astype(F32)).astype(BF16)
    w_router2 = jnp.concatenate([w_router_hi, w_router_lo], axis=-1)
    b_router = jnp.concatenate([b_group_router[0], b_expert_router[0], jnp.zeros((pad,), F32)])[None, :]
    x1, h2p, route, route_t, counts = _mix(o_attn, o_ret, nat_b, x2, w_attn_branch[0].astype(BF16),
                                           w_ret_branch[0].astype(BF16), w_out[0].astype(BF16),
                                           g_ffn[0][None, :], w_router2, b_router, interpret)
    idx4, tile_expert, pair_row, n_used, n_rows = _route_plan(route_t, counts, n, interpret)
    xs = _sc_scatter_rows(h2p.reshape(2 * n, SC_ROW_WORDS), idx4, 2 * n_rows)
    ys = _experts(xs.reshape(2, n_rows, SC_ROW_WORDS), tile_expert, pair_row, n_used, wg_bf, wu_bf, wd_bf, interpret)
    yab = _sc_gather_rows(ys.reshape(2 * n_rows, SC_ROW_WORDS), idx4)
    out = _final(x1, yab.reshape(4, n, SC_ROW_WORDS), route, g_final[None, :], interpret)
    return out.reshape(batch, seq, d)


def kernel(x, g_mix, w_in, w_attn_branch, w_ret_branch, w_out, g_ffn, w_group_router, b_group_router,
           w_expert_router, b_expert_router, w_gate, w_up, w_down, g_final):
    return _forward(x, g_mix, w_in, w_attn_branch, w_ret_branch, w_out, g_ffn, w_group_router,
                    b_group_router, w_expert_router, b_expert_router, w_gate, w_up, w_down, g_final)
```

```python
import functools

import numpy as np
import jax
import jax.numpy as jnp
from jax import lax
from jax.experimental import pallas as pl
from jax.experimental.pallas import tpu as pltpu
from jax.experimental.pallas import tpu_sc as plsc

F32 = jnp.float32
BF16 = jnp.bfloat16

D_MODEL = 1024
ATTN_GROUPS = ((128, 1), (512, 4), (2048, 16))
N_GROUPS = len(ATTN_GROUPS)
ATTN_HEADS = 8
HEAD_DIM = 64
GROUP_W = ATTN_HEADS * HEAD_DIM
QKV_W = N_GROUPS * GROUP_W
RET_HEADS = 4
RET_DK = 128
RET_DV = 256
RET_CHUNK = 128
RET_TS = 1024
N_EXPERT_GROUPS = 4
EXPERTS_PER_GROUP = 8
N_EXPERTS = N_EXPERT_GROUPS * EXPERTS_PER_GROUP
EXPERT_FF = 512
EPS = 1e-6

LANES = 128
BLK = 128
SPAN = 2048
NEG = -1e30
ACC_PARTS = 3

COL_QA = 0
COL_KA = COL_QA + QKV_W
COL_VA = COL_KA + QKV_W
COL_QR = COL_VA + QKV_W
COL_KR = COL_QR + RET_HEADS * RET_DK
COL_VR = COL_KR + RET_HEADS * RET_DK
COL_GR = COL_VR + RET_HEADS * RET_DV
COL_GATE_A = COL_GR + RET_HEADS * RET_DV
COL_GATE_R = COL_GATE_A + D_MODEL
IN_WIDTH = COL_GATE_R + D_MODEL
COL_BLOCK = 256

PROJ_TM = 512
PROJ_REST_TM = 1024
PROJ_TN = IN_WIDTH // 2
MXU_N = 256
MIX_TM = 1024
MIX_CHUNK = 512
ROUTE_CHUNK = 256
ROUTE_ROWS = 8
EXP_TM = 512
EXP_PAIR = 4
SC_WINDOW = 128
SC_ROW_WORDS = 256
FIN_TM = 1024
VMEM_LIMIT = 56 * 1024 * 1024


def _cparams(sem):
    return pltpu.CompilerParams(dimension_semantics=sem, vmem_limit_bytes=VMEM_LIMIT)


def _sigmoid(x):
    return 0.5 * jnp.tanh(0.5 * x) + 0.5


def _proj_plans():
    plan_a, nat_col = [], 0
    for c in range(PROJ_TN // MXU_N):
        col = c * MXU_N
        section, within = divmod(col, QKV_W)
        gi = within // GROUP_W
        if col < COL_QR and gi > 0:
            plan_a.append((None, (gi, section * GROUP_W + within % GROUP_W)))
        else:
            plan_a.append((nat_col, None))
            nat_col += MXU_N
    plan_b = [(c * MXU_N, None) for c in range((IN_WIDTH - PROJ_TN) // MXU_N)]
    return plan_a, nat_col, plan_b


def _proj_kernel(*refs, plan, n_cast):
    has_dilated = any(d is not None for _, d in plan)
    x_ref, g_ref, w_ref = refs[:3]
    cast_in = refs[3:3 + n_cast]
    o_ref = refs[3 + n_cast]
    rest = refs[4 + n_cast:]
    if has_dilated:
        d4_ref, d16_ref = rest[:2]
        stage_ref, mid_ref = rest[2 + n_cast:]
        rest = rest[2:]
    cast_out = rest[:n_cast]

    x = x_ref[...]
    ms = jnp.mean(x * x, axis=-1, keepdims=True)
    h = (x * lax.rsqrt(ms + EPS) * g_ref[...]).astype(BF16)
    step = ATTN_GROUPS[1][1]
    quarter = x_ref.shape[0] // step
    for c, (nat_col, dil) in enumerate(plan):
        res = jnp.dot(h, w_ref[:, c * MXU_N:(c + 1) * MXU_N], preferred_element_type=F32)
        if nat_col is not None:
            o_ref[:, nat_col:nat_col + MXU_N] = res.astype(o_ref.dtype)
        if dil is None:
            continue
        gi, col = dil
        slot = c % 2
        for lt in range(MXU_N // LANES):
            stage_ref[slot, lt] = res[:, lt * LANES:(lt + 1) * LANES]
        for lt in range(MXU_N // LANES):
            cols = slice(col + lt * LANES, col + (lt + 1) * LANES)
            for b in range(step):
                rows = stage_ref[slot, lt, pl.ds(b, quarter, stride=step), :]
                if gi == 1:
                    d4_ref[0, b, :, cols] = rows.astype(BF16)
                else:
                    mid_ref[slot, lt, b * quarter:(b + 1) * quarter, :] = rows
            if gi == 2:
                for b in range(step):
                    for a in range(step):
                        rows = mid_ref[slot, lt, pl.ds(b * quarter + a, quarter // step, stride=step), :]
                        d16_ref[0, a * step + b, :, cols] = rows.astype(BF16)

    for src, dst in zip(cast_in, cast_out):
        dst[...] = src[...].astype(BF16)


def _proj(x2, g_mix, w_in_bf16, w_gate, w_up, w_down, batch, seq, interpret):
    n = x2.shape[0]
    n_i = n // PROJ_TM
    tiles = seq // PROJ_TM
    plan_a, width_a, plan_b = _proj_plans()
    dils = [ATTN_GROUPS[gi][1] for gi in (1, 2)]
    row = lambda i: (i, 0)

    def call(plan, col_block, nat_width, casts, name, tm):
        steps = n // tm
        flat = [w.reshape(-1, w.shape[-1]) for w in casts]
        w_specs = [pl.BlockSpec((w.shape[0] // steps, w.shape[1]), row) for w in flat]
        dilated = any(d is not None for _, d in plan)
        d_specs = [pl.BlockSpec((1, d, tm // d, 3 * GROUP_W), lambda i: (i // tiles, 0, i % tiles, 0))
                   for d in dils] if dilated else []
        d_shapes = [jax.ShapeDtypeStruct((batch, d, seq // d, 3 * GROUP_W), BF16) for d in dils] if dilated else []
        stage = pltpu.VMEM((2, MXU_N // LANES, tm, LANES), F32)
        return pl.pallas_call(
            functools.partial(_proj_kernel, plan=plan, n_cast=len(casts)),
            grid=(steps,),
            in_specs=[pl.BlockSpec((tm, D_MODEL), row),
                      pl.BlockSpec((1, D_MODEL), lambda i: (0, 0)),
                      pl.BlockSpec((D_MODEL, PROJ_TN), lambda i: (0, col_block), pipeline_mode=pl.Buffered(1))]
            + w_specs,
            out_specs=[pl.BlockSpec((tm, nat_width), row)] + d_specs + w_specs,
            out_shape=[jax.ShapeDtypeStruct((n, nat_width), BF16)] + d_shapes
            + [jax.ShapeDtypeStruct(w.shape, BF16) for w in flat],
            scratch_shapes=[stage, stage] if dilated else [],
            compiler_params=_cparams(("arbitrary",)),
            interpret=interpret,
            name=name,
        )(x2, g_mix, w_in_bf16, *flat)

    nat_a, qkv_d4, qkv_d16, wg_bf, wu_bf = call(plan_a, 0, width_a, (w_gate, w_up), "proj_attn", PROJ_TM)
    nat_b, wd_bf = call(plan_b, 1, IN_WIDTH - PROJ_TN, (w_down,), "proj_rest", PROJ_REST_TM)
    return (nat_a, nat_b, (qkv_d4, qkv_d16), wg_bf.reshape(w_gate.shape), wu_bf.reshape(w_up.shape),
            wd_bf.reshape(w_down.shape))


def _attn_unit(q2, kk, vv, bias_a, bias_b):
    lane = lax.broadcasted_iota(jnp.int32, (BLK, LANES), 1)
    left = lane < HEAD_DIM
    zero = jnp.zeros_like(q2)
    nt = (((1,), (1,)), ((), ()))
    q_st = jnp.concatenate([jnp.where(left, q2, zero), jnp.where(left, zero, q2)], axis=0)
    s = lax.dot_general(q_st, kk, nt, preferred_element_type=F32) + jnp.concatenate([bias_a, bias_b], axis=0)
    m = jnp.max(s, axis=-1, keepdims=True)
    p = jnp.exp(s - m)
    den = jnp.sum(p, axis=-1, keepdims=True)
    o = jnp.dot(p.astype(BF16), vv, preferred_element_type=F32)
    return (jnp.where(left, o[:BLK], o[BLK:]), jnp.where(left, m[:BLK], m[BLK:]),
            jnp.where(left, den[:BLK], den[BLK:]))


def _attn_kernel(q1_ref, k1_ref, v1_ref, q2_ref, k2_ref, v2_ref, q3_ref, k3_ref, v3_ref,
                 bias_ref, o_ref, acc_ref, accw_ref, *, seq):
    s_id = pl.program_id(2)
    step = ATTN_GROUPS[1][1]
    quarter = SPAN // step
    dilated = {1: (q2_ref, k2_ref, v2_ref), 2: (q3_ref, k3_ref, v3_ref)}

    def dilated_unit(gi, d, m, r, first):
        slot = gi - 1
        q_ref, k_ref, v_ref = dilated[gi]
        loc = BLK * m * d + r
        cur = pl.multiple_of(s_id * (SPAN // d) + BLK * m, BLK)
        prev = pl.multiple_of(jnp.where(first == 1, cur, cur - BLK), BLK)
        q2 = q_ref[0, r, BLK * m:BLK * (m + 1), :] * 0.125
        kk = jnp.concatenate([k_ref[0, r, pl.ds(prev, BLK), :], k_ref[0, r, pl.ds(cur, BLK), :]], axis=0)
        vv = jnp.concatenate([v_ref[0, r, pl.ds(prev, BLK), :], v_ref[0, r, pl.ds(cur, BLK), :]], axis=0)
        parts = _attn_unit(q2, kk, vv, bias_ref[gi, first, 0], bias_ref[gi, first, 1])
        for j, part in enumerate(parts):
            if d == step:
                acc_ref[ACC_PARTS * slot + j, pl.ds(loc, BLK, stride=d), :] = part
            else:
                a, b = divmod(r, step)
                accw_ref[j, pl.ds(b * quarter + a, BLK, stride=step), :] = part

    def reinterleave_pieces():
        pieces = []
        for j in range(ACC_PARTS):
            for b in range(step):
                for j0 in range(0, quarter, 256):
                    def piece(j=j, b=b, j0=j0):
                        acc_ref[ACC_PARTS + j, pl.ds(b + step * j0, 256, stride=step), :] = (
                            accw_ref[j, b * quarter + j0:b * quarter + j0 + 256, :])
                    pieces.append(piece)
        return pieces

    def dense_unit(m):
        loc = m * BLK
        cur = pl.multiple_of(s_id * SPAN + loc, BLK)
        prev = pl.multiple_of(jnp.maximum(cur - BLK, 0), BLK)
        first = jnp.where(cur == 0, 1, 0)
        q2 = q1_ref[loc:loc + BLK, :] * 0.125
        kk = jnp.concatenate([k1_ref[pl.ds(prev, BLK), :], k1_ref[pl.ds(cur, BLK), :]], axis=0)
        vv = jnp.concatenate([v1_ref[pl.ds(prev, BLK), :], v1_ref[pl.ds(cur, BLK), :]], axis=0)
        n1, m1, d1 = _attn_unit(q2, kk, vv, bias_ref[0, first, 0], bias_ref[0, first, 1])
        (n2, m2, d2), (n3, m3, d3) = (
            tuple(acc_ref[g * ACC_PARTS + j, loc:loc + BLK, :] for j in range(ACC_PARTS)) for g in range(2))
        mx = jnp.maximum(jnp.maximum(m1, m2), m3)
        w1, w2, w3 = jnp.exp(m1 - mx), jnp.exp(m2 - mx), jnp.exp(m3 - mx)
        num = w1 * n1 + w2 * n2 + w3 * n3
        den = w1 * d1 + w2 * d2 + w3 * d3
        o_ref[loc:loc + BLK, :] = (num / den).astype(o_ref.dtype)

    first_span = jnp.where(s_id == 0, 1, 0)
    for gi in (2, 1):
        d = ATTN_GROUPS[gi][1]
        for m in range(SPAN // (BLK * d)):
            for r in range(d):
                dilated_unit(gi, d, m, r, first_span if m == 0 else 0)
        if d != step:
            for piece in reinterleave_pieces():
                piece()
    for m in range(SPAN // BLK):
        dense_unit(m)


def _attn_bias():
    slopes = np.exp2(-8.0 * np.arange(1, ATTN_HEADS + 1, dtype=np.float64) / ATTN_HEADS)
    qi = np.arange(BLK)[:, None]
    kj = np.arange(2 * BLK)[None, :]
    rel = qi + BLK - kj
    out = np.zeros((N_GROUPS, 2, ATTN_HEADS, BLK, 2 * BLK), np.float32)
    for gi, (window, d) in enumerate(ATTN_GROUPS):
        n_back = window // d
        assert n_back == BLK
        valid = (rel >= 0) & (rel <= n_back)
        bias = -slopes[:, None, None] * (rel * d)[None].astype(np.float64)
        out[gi, 0] = np.where(valid[None], bias, NEG)
        out[gi, 1] = np.where((valid & (kj >= BLK))[None], bias, NEG)
    return jnp.asarray(out)


def _attention(nat_a, dilated, batch, seq, interpret):
    n = batch * seq
    spans = seq // SPAN
    n_hp = GROUP_W // LANES
    heads = GROUP_W // LANES
    specs = [pl.BlockSpec((SPAN, LANES), lambda b, hp, s: (b * spans + s, hp)),
             pl.BlockSpec((seq, LANES), lambda b, hp, s: (b, heads + hp)),
             pl.BlockSpec((seq, LANES), lambda b, hp, s: (b, 2 * heads + hp))]
    operands = [nat_a, nat_a, nat_a]
    for gi, arr in zip((1, 2), dilated):
        d = ATTN_GROUPS[gi][1]
        specs += [pl.BlockSpec((1, d, SPAN // d, LANES), lambda b, hp, s: (b, 0, s, hp)),
                  pl.BlockSpec((1, d, seq // d, LANES), lambda b, hp, s: (b, 0, 0, heads + hp)),
                  pl.BlockSpec((1, d, seq // d, LANES), lambda b, hp, s: (b, 0, 0, 2 * heads + hp))]
        operands += [arr, arr, arr]
    bias_spec = pl.BlockSpec((N_GROUPS, 2, 2, BLK, 2 * BLK), lambda b, hp, s: (0, 0, hp, 0, 0))
    return pl.pallas_call(
        functools.partial(_attn_kernel, seq=seq),
        grid=(batch, n_hp, spans),
        in_specs=specs + [bias_spec],
        out_specs=pl.BlockSpec((SPAN, LANES), lambda b, hp, s: (b * spans + s, hp)),
        out_shape=jax.ShapeDtypeStruct((n, GROUP_W), BF16),
        scratch_shapes=[
            pltpu.VMEM((2 * ACC_PARTS, SPAN, LANES), F32),
            pltpu.VMEM((ACC_PARTS, SPAN, LANES), F32),
        ],
        compiler_params=_cparams(("arbitrary", "arbitrary", "arbitrary")),
        interpret=interpret,
        name="attn",
    )(*operands, _attn_bias())


def _ret_kernel(*refs):
    heads_per_qk = COL_BLOCK // RET_DK
    n_qk = RET_HEADS // heads_per_qk
    q_refs, k_refs = refs[:n_qk], refs[n_qk:2 * n_qk]
    v_refs = refs[2 * n_qk:2 * n_qk + RET_HEADS]
    g_refs = refs[2 * n_qk + RET_HEADS:2 * n_qk + 2 * RET_HEADS]
    dec_ref, xi_ref, zeta_ref, gch_ref, o_ref, st_ref = refs[2 * n_qk + 2 * RET_HEADS:]
    @pl.when(pl.program_id(1) == 0)
    def _():
        st_ref[...] = jnp.zeros_like(st_ref)

    nt = (((1,), (1,)), ((), ()))
    scale = RET_DK ** -0.5

    for c in range(RET_TS // RET_CHUNK):
        rows = pl.ds(c * RET_CHUNK, RET_CHUNK)
        for h in range(RET_HEADS):
            kcols = slice((h % heads_per_qk) * RET_DK, (h % heads_per_qk + 1) * RET_DK)
            vcols = slice(h * RET_DV, (h + 1) * RET_DV)
            qi = q_refs[h // heads_per_qk][rows, kcols]
            kf = k_refs[h // heads_per_qk][rows, kcols].astype(F32) * scale
            ki = kf.astype(BF16)
            kz_t = jnp.transpose(kf * zeta_ref[h]).astype(BF16)
            vi = v_refs[h][rows, :]
            att = lax.dot_general(qi, ki, nt, preferred_element_type=F32) * dec_ref[h]
            inner = jnp.dot(att.astype(BF16), vi, preferred_element_type=F32)
            st = st_ref[h]
            cross = jnp.dot(qi, st.astype(BF16), preferred_element_type=F32) * xi_ref[h]
            st_ref[h] = gch_ref[h] * st + jnp.dot(kz_t, vi, preferred_element_type=F32)
            y = inner + cross
            mu = jnp.mean(y, axis=-1, keepdims=True)
            yc = y - mu
            var = jnp.mean(yc * yc, axis=-1, keepdims=True)
            yn = yc * lax.rsqrt(var + EPS)
            g = g_refs[h][rows, :].astype(F32)
            o_ref[rows, vcols] = (g * _sigmoid(g) * yn).astype(o_ref.dtype)


def _ret_tables():
    c = RET_CHUNK
    log_g = np.log1p(-np.exp2(-5.0 - np.arange(RET_HEADS, dtype=np.float64)))
    pos = np.arange(c, dtype=np.float64)
    diff = pos[:, None] - pos[None, :]
    dec = np.where(diff >= 0, np.exp(log_g[:, None, None] * np.maximum(diff, 0.0)), 0.0)
    xi = np.exp(log_g[:, None] * (pos + 1.0))[..., None] * np.ones((1, 1, RET_DV))
    zeta = np.exp(log_g[:, None] * (c - 1.0 - pos))[..., None] * np.ones((1, 1, RET_DK))
    gch = np.exp(log_g * c)[:, None, None] * np.ones((1, 1, RET_DV))
    return tuple(jnp.asarray(t, F32) for t in (dec, xi, zeta, gch))


def _retention(nat_a, nat_b, batch, seq, interpret):
    n = batch * seq
    dec, xi, zeta, gch = _ret_tables()
    v_w = RET_HEADS * RET_DV
    nts = seq // RET_TS
    const3 = lambda b, t: (0, 0, 0)

    def block(arr, col):
        base = 0 if arr is nat_a else PROJ_TN
        if arr is nat_a:
            col = _proj_plans()[1] - (PROJ_TN - col)
        idx = (col - base) // COL_BLOCK
        return arr, pl.BlockSpec((RET_TS, COL_BLOCK), lambda b, t: (b * nts + t, idx))

    qk_blocks = RET_HEADS * RET_DK // COL_BLOCK
    picks = [block(nat_a if COL_QR + k * COL_BLOCK < PROJ_TN else nat_b, COL_QR + k * COL_BLOCK)
             for k in range(qk_blocks)]
    picks += [block(nat_b, COL_KR + k * COL_BLOCK) for k in range(qk_blocks)]
    picks += [block(nat_b, COL_VR + h * RET_DV) for h in range(RET_HEADS)]
    picks += [block(nat_b, COL_GR + h * RET_DV) for h in range(RET_HEADS)]
    return pl.pallas_call(
        _ret_kernel,
        grid=(batch, nts),
        in_specs=[spec for _, spec in picks] + [
            pl.BlockSpec((RET_HEADS, RET_CHUNK, RET_CHUNK), const3),
            pl.BlockSpec((RET_HEADS, RET_CHUNK, RET_DV), const3),
            pl.BlockSpec((RET_HEADS, RET_CHUNK, RET_DK), const3),
            pl.BlockSpec((RET_HEADS, 1, RET_DV), const3),
        ],
        out_specs=pl.BlockSpec((RET_TS, v_w), lambda b, t: (b * nts + t, 0)),
        out_shape=jax.ShapeDtypeStruct((n, v_w), BF16),
        scratch_shapes=[pltpu.VMEM((RET_HEADS, RET_DK, RET_DV), F32)],
        compiler_params=_cparams(("arbitrary", "arbitrary")),
        interpret=interpret,
        name="retention",
    )(*[arr for arr, _ in picks], dec, xi, zeta, gch)


ROUTER_OFF = N_EXPERT_GROUPS


def _pack_bf16_pair(a, b):
    hi = lax.bitcast_convert_type(a.astype(BF16).astype(F32), jnp.uint32)
    lo = lax.bitcast_convert_type(b.astype(BF16).astype(F32), jnp.uint32)
    return lax.bitcast_convert_type(hi | (lo >> 16), jnp.int32)


def _unpack_bf16_pair(w):
    u = lax.bitcast_convert_type(w, jnp.uint32)
    a = lax.bitcast_convert_type(u & jnp.uint32(0xFFFF0000), F32).astype(BF16)
    b = lax.bitcast_convert_type(u << 16, F32).astype(BF16)
    return a, b


def _pack_rows(y):
    q = D_MODEL // 4
    return (_pack_bf16_pair(y[:, 0:q], y[:, 2 * q:3 * q]), _pack_bf16_pair(y[:, q:2 * q], y[:, 3 * q:4 * q]))


def _unpack_rows(slab0, slab1):
    q0, q2 = _unpack_bf16_pair(slab0)
    q1, q3 = _unpack_bf16_pair(slab1)
    return jnp.concatenate([q0, q1, q2, q3], axis=1)


def _mix_kernel(oa_ref, or_ref, *refs):
    n_gate = D_MODEL // COL_BLOCK
    ga_refs, gr_refs = refs[:n_gate], refs[n_gate:2 * n_gate]
    (x_ref, pa_ref, pr_ref, wo_ref, gf_ref, wr_ref, br_ref,
     x1_ref, h2_ref, route_ref, route_t_ref, cnt_ref, carry_ref, logit_ref) = refs[2 * n_gate:]
    step = pl.program_id(0)

    @pl.when(step == 0)
    def _():
        carry_ref[...] = jnp.zeros_like(carry_ref)
        logit_ref[...] = jnp.zeros_like(logit_ref)

    routing = iter([functools.partial(_route_rows, pl.ds(c * ROUTE_CHUNK, ROUTE_CHUNK), step > 0, logit_ref,
                                      route_ref, route_t_ref, cnt_ref, carry_ref)
                    for c in range(MIX_TM // ROUTE_CHUNK)])
    for c in range(MIX_TM // MIX_CHUNK):
        for _ in _mix_rows(pl.ds(c * MIX_CHUNK, MIX_CHUNK), oa_ref, or_ref, ga_refs, gr_refs,
                           x_ref, pa_ref, pr_ref, wo_ref, gf_ref, wr_ref, br_ref, x1_ref, h2_ref, logit_ref):
            next(routing, lambda: None)()
    for piece in routing:
        piece()


def _mix_rows(rows, oa_ref, or_ref, ga_refs, gr_refs, x_ref, pa_ref, pr_ref, wo_ref, gf_ref, wr_ref, br_ref,
              x1_ref, h2_ref, logit_ref):
    a = jnp.dot(oa_ref[rows, :], pa_ref[...], preferred_element_type=F32)
    yield
    r = jnp.dot(or_ref[rows, :], pr_ref[...], preferred_element_type=F32)
    yield
    merged = jnp.concatenate(
        [_sigmoid(ga[rows, :].astype(F32)) * a[:, k * COL_BLOCK:(k + 1) * COL_BLOCK]
         + _sigmoid(gr[rows, :].astype(F32)) * r[:, k * COL_BLOCK:(k + 1) * COL_BLOCK]
         for k, (ga, gr) in enumerate(zip(ga_refs, gr_refs))], axis=1)
    x1 = x_ref[rows, :] + jnp.dot(merged.astype(BF16), wo_ref[...], preferred_element_type=F32)
    x1_ref[rows, :] = x1
    ms = jnp.mean(x1 * x1, axis=-1, keepdims=True)
    h2 = x1 * lax.rsqrt(ms + EPS) * gf_ref[...]
    h2_ref[0, rows, :], h2_ref[1, rows, :] = _pack_rows(h2)

    h_hi = h2.astype(BF16)
    h_lo = (h2 - h_hi.astype(F32)).astype(BF16)
    both = jnp.dot(h_hi, wr_ref[...], preferred_element_type=F32)
    logit_ref[rows, :] = (both[:, :LANES] + both[:, LANES:]
                          + jnp.dot(h_lo, wr_ref[:, :LANES], preferred_element_type=F32) + br_ref[...])


def _route_rows(rows, live, logit_ref, route_ref, route_t_ref, cnt_ref, carry_ref):
    logits = logit_ref[rows, :]
    tm = logits.shape[0]
    lane = lax.broadcasted_iota(jnp.int32, (tm, LANES), 1).astype(F32)
    big = jnp.float32(4 * LANES)
    ninf = -jnp.inf
    is_g = lane < N_EXPERT_GROUPS
    gl = jnp.where(is_g, logits, ninf)
    gmax = jnp.max(gl, axis=-1, keepdims=True)
    gsum = jnp.sum(jnp.where(is_g, jnp.exp(gl - gmax), 0.0), axis=-1, keepdims=True)
    g_val = 1.0 / gsum
    g_idx = jnp.min(jnp.where(jnp.logical_and(is_g, gl == gmax), lane, big), axis=-1, keepdims=True)
    lo = ROUTER_OFF + EXPERTS_PER_GROUP * g_idx
    in_grp = jnp.logical_and(lane >= lo, lane < lo + EXPERTS_PER_GROUP)
    el = jnp.where(in_grp, logits, ninf)
    v1 = jnp.max(el, axis=-1, keepdims=True)
    i1 = jnp.min(jnp.where(jnp.logical_and(in_grp, el == v1), lane, big), axis=-1, keepdims=True)
    rest = jnp.logical_and(in_grp, lane != i1)
    el2 = jnp.where(rest, logits, ninf)
    v2 = jnp.max(el2, axis=-1, keepdims=True)
    i2 = jnp.min(jnp.where(jnp.logical_and(rest, el2 == v2), lane, big), axis=-1, keepdims=True)
    t = jnp.exp(v2 - v1)
    w1 = g_val / (1.0 + t)
    w2 = g_val * t / (1.0 + t)

    sel = jnp.logical_or(lane == i1, lane == i2)
    sel_bf = jnp.where(sel, 1.0, 0.0).astype(BF16)
    row = lax.broadcasted_iota(jnp.int32, (tm, tm), 0)
    col = lax.broadcasted_iota(jnp.int32, (tm, tm), 1)
    tri = jnp.where(col < row, 1.0, 0.0).astype(BF16)
    before = jnp.dot(tri, sel_bf, preferred_element_type=F32) + carry_ref[...]
    r1 = jnp.sum(jnp.where(lane == i1, before, 0.0), axis=-1, keepdims=True)
    r2 = jnp.sum(jnp.where(lane == i2, before, 0.0), axis=-1, keepdims=True)
    carry = carry_ref[...] + jnp.where(live, jnp.sum(jnp.where(sel, 1.0, 0.0), axis=0, keepdims=True), 0.0)
    carry_ref[...] = carry
    cnt_ref[...] = carry

    vals = (i1 - ROUTER_OFF, i2 - ROUTER_OFF, w1, w2, r1, r2)
    route = jnp.zeros((tm, LANES), F32)
    for j, v in enumerate(vals):
        route = jnp.where(lane == j, v, route)
    route_ref[rows, :] = route
    route_t_ref[:, rows] = jnp.transpose(route)[:ROUTE_ROWS, :]


def _mix(o_attn, o_ret, nat_b, x2, pa, pr, wo, g_ffn, w_router, b_router, interpret):
    n = x2.shape[0]
    tm = MIX_TM
    last = n // tm - 1
    n_gate = D_MODEL // COL_BLOCK
    const = lambda i: (0, 0)
    cur = lambda i: jnp.minimum(i, last)
    prev = lambda i: jnp.maximum(i - 1, 0)
    return pl.pallas_call(
        _mix_kernel,
        grid=(n // tm + 1,),
        in_specs=[
            pl.BlockSpec((tm, GROUP_W), lambda i: (cur(i), 0)),
            pl.BlockSpec((tm, D_MODEL), lambda i: (cur(i), 0)),
        ] + [
            pl.BlockSpec((tm, COL_BLOCK), functools.partial(lambda i, idx: (cur(i), idx), idx=(col - PROJ_TN) // COL_BLOCK + k))
            for col in (COL_GATE_A, COL_GATE_R) for k in range(n_gate)
        ] + [
            pl.BlockSpec((tm, D_MODEL), lambda i: (cur(i), 0)),
            pl.BlockSpec((GROUP_W, D_MODEL), const),
            pl.BlockSpec((D_MODEL, D_MODEL), const),
            pl.BlockSpec((D_MODEL, D_MODEL), const),
            pl.BlockSpec((1, D_MODEL), const),
            pl.BlockSpec((D_MODEL, 2 * LANES), const),
            pl.BlockSpec((1, LANES), const),
        ],
        out_specs=[
            pl.BlockSpec((tm, D_MODEL), lambda i: (cur(i), 0)),
            pl.BlockSpec((2, tm, SC_ROW_WORDS), lambda i: (0, cur(i), 0)),
            pl.BlockSpec((tm, LANES), lambda i: (prev(i), 0)),
            pl.BlockSpec((ROUTE_ROWS, tm), lambda i: (0, prev(i))),
            pl.BlockSpec((1, LANES), const),
        ],
        out_shape=[
            jax.ShapeDtypeStruct((n, D_MODEL), F32),
            jax.ShapeDtypeStruct((2, n, SC_ROW_WORDS), jnp.int32),
            jax.ShapeDtypeStruct((n, LANES), F32),
            jax.ShapeDtypeStruct((ROUTE_ROWS, n), F32),
            jax.ShapeDtypeStruct((1, LANES), F32),
        ],
        scratch_shapes=[pltpu.VMEM((1, LANES), F32), pltpu.VMEM((tm, LANES), F32)],
        compiler_params=_cparams(("arbitrary",)),
        interpret=interpret,
        name="mix_router",
    )(o_attn, o_ret, *([nat_b] * (2 * n_gate)), x2, pa, pr, wo, g_ffn, w_router, b_router)


def _expert_kernel(te_ref, pr_ref, nu_ref, xs_ref, *refs):
    j = pl.program_id(0)
    o_ref = refs[-1]
    live = pr_ref[j] == j
    for h in range(EXP_PAIR):
        wg_ref, wu_ref, wd_ref = refs[3 * h:3 * h + 3]
        rows = pl.ds(h * EXP_TM, EXP_TM)

        @pl.when(jnp.logical_and(live, EXP_PAIR * j + h < nu_ref[0]))
        def _():
            xs = _unpack_rows(xs_ref[0, rows, :], xs_ref[1, rows, :])
            a = jnp.dot(xs, wg_ref[0], preferred_element_type=F32)
            u = jnp.dot(xs, wu_ref[0], preferred_element_type=F32)
            hid = (a * _sigmoid(a) * u).astype(BF16)
            y = jnp.dot(hid, wd_ref[0], preferred_element_type=F32)
            o_ref[0, rows, :], o_ref[1, rows, :] = _pack_rows(y)


def _experts(xs, tile_expert, pair_row, n_used, w_gate, w_up, w_down, interpret):
    p = xs.shape[1]
    n_steps = p // (EXP_TM * EXP_PAIR)
    tile = lambda j, pr, h: EXP_PAIR * pr[j] + h
    w_specs = []
    for h in range(EXP_PAIR):
        pick = functools.partial(lambda j, te, pr, nu, h: (te[tile(j, pr, h)], 0, 0), h=h)
        w_specs += [pl.BlockSpec((1, D_MODEL, EXPERT_FF), pick), pl.BlockSpec((1, D_MODEL, EXPERT_FF), pick),
                    pl.BlockSpec((1, EXPERT_FF, D_MODEL), pick)]
    rows_spec = pl.BlockSpec((2, EXP_TM * EXP_PAIR, SC_ROW_WORDS), lambda j, te, pr, nu: (0, pr[j], 0))
    grid_spec = pltpu.PrefetchScalarGridSpec(
        num_scalar_prefetch=3,
        grid=(n_steps,),
        in_specs=[rows_spec] + w_specs,
        out_specs=rows_spec,
    )
    return pl.pallas_call(
        _expert_kernel,
        grid_spec=grid_spec,
        out_shape=jax.ShapeDtypeStruct((2, p, SC_ROW_WORDS), jnp.int32),
        compiler_params=_cparams(("arbitrary",)),
        interpret=interpret,
        name="experts",
    )(tile_expert, pair_row, n_used, xs, *([w_gate, w_up, w_down] * EXP_PAIR))


def _final_kernel(x1_ref, yab_ref, route_ref, g_ref, o_ref):
    route = route_ref[...]
    w1 = route[:, 2:3]
    w2 = route[:, 3:4]
    ya = _unpack_rows(yab_ref[0], yab_ref[2]).astype(F32)
    yb = _unpack_rows(yab_ref[1], yab_ref[3]).astype(F32)
    x2 = x1_ref[...] + w1 * ya + w2 * yb
    ms = jnp.mean(x2 * x2, axis=-1, keepdims=True)
    o_ref[...] = x2 * lax.rsqrt(ms + EPS) * g_ref[...]


def _final(x1, yab, route, g_final, interpret):
    n = x1.shape[0]
    tm = FIN_TM
    row = lambda i: (i, 0)
    return pl.pallas_call(
        _final_kernel,
        grid=(n // tm,),
        in_specs=[
            pl.BlockSpec((tm, D_MODEL), row),
            pl.BlockSpec((4, tm, SC_ROW_WORDS), lambda i: (0, i, 0)),
            pl.BlockSpec((tm, LANES), row),
            pl.BlockSpec((1, D_MODEL), lambda i: (0, 0)),
        ],
        out_specs=pl.BlockSpec((tm, D_MODEL), row),
        out_shape=jax.ShapeDtypeStruct((n, D_MODEL), F32),
        compiler_params=_cparams(("arbitrary",)),
        interpret=interpret,
        name="combine_final",
    )(x1, yab, route, g_final)


def _dest_kernel(offs_ref, route_t_ref, idx_ref, *, n_rows):
    route_t = route_t_ref[...]
    experts = route_t[0:2, :]
    dest = route_t[4:6, :].astype(jnp.int32)
    for e in range(N_EXPERTS):
        dest = dest + jnp.where(experts == float(e), offs_ref[e], 0)
    idx_ref[0:2, :] = dest
    idx_ref[2:4, :] = dest + n_rows


def _route_plan(route_t, counts, n, interpret):
    cnt = counts[0, ROUTER_OFF:ROUTER_OFF + N_EXPERTS].astype(jnp.int32)
    padded = ((cnt + EXP_TM - 1) // EXP_TM) * EXP_TM
    ends = jnp.cumsum(padded)
    offs = ends - padded
    n_rows = 2 * n + N_EXPERTS * EXP_TM
    idx4 = pl.pallas_call(
        functools.partial(_dest_kernel, n_rows=n_rows),
        grid_spec=pltpu.PrefetchScalarGridSpec(
            num_scalar_prefetch=1, grid=(1,),
            in_specs=[pl.BlockSpec(route_t.shape, lambda i, offs: (0, 0))],
            out_specs=pl.BlockSpec((4, n), lambda i, offs: (0, 0))),
        out_shape=jax.ShapeDtypeStruct((4, n), jnp.int32),
        interpret=interpret,
        name="route_dest",
    )(offs, route_t)
    n_used = ends[-1] // EXP_TM
    tile_row = jnp.minimum(jnp.arange(n_rows // EXP_TM, dtype=jnp.int32), n_used - 1)
    tile_expert = jnp.sum((ends[None, :] <= (tile_row * EXP_TM)[:, None]).astype(jnp.int32), axis=1)
    pair_row = jnp.minimum(jnp.arange(n_rows // (EXP_TM * EXP_PAIR), dtype=jnp.int32), (n_used - 1) // EXP_PAIR)
    return idx4, tile_expert, pair_row, n_used[None], n_rows


def _sc_mesh():
    return plsc.VectorSubcoreMesh(core_axis_name="core", subcore_axis_name="subcore")


def _sc_scatter_rows(rows, idx4, n_out):
    n_in, w = rows.shape
    nb = idx4.shape[1] // SC_WINDOW

    @functools.partial(pl.kernel, out_type=jax.ShapeDtypeStruct((n_out, w), rows.dtype), mesh=_sc_mesh(),
                       scratch_types=[], name="sc_scatter_rows")
    def scatter(x_hbm, ia_hbm, ib_hbm, o_hbm):
        def body(x_vmem, ia_vmem, ib_vmem):
            pltpu.sync_copy(x_vmem, o_hbm.at[ia_vmem.at[0]])
            pltpu.sync_copy(x_vmem, o_hbm.at[ib_vmem.at[0]])

        pltpu.emit_pipeline(
            body,
            grid=(n_in // SC_WINDOW,),
            in_specs=[pl.BlockSpec((SC_WINDOW, w), lambda i: (i, 0)),
                      pl.BlockSpec((1, SC_WINDOW), lambda i: (2 * (i // nb), i % nb)),
                      pl.BlockSpec((1, SC_WINDOW), lambda i: (2 * (i // nb) + 1, i % nb))],
            out_specs=[],
            core_axis_name=("core", "subcore"),
            dimension_semantics=(pltpu.PARALLEL,),
        )(x_hbm, ia_hbm, ib_hbm)

    return scatter(rows, idx4, idx4)


def _sc_gather_rows(table, idx4):
    nb = idx4.shape[1] // SC_WINDOW
    n_idx = idx4.shape[0] * idx4.shape[1]
    w = table.shape[1]

    @functools.partial(pl.kernel, out_type=jax.ShapeDtypeStruct((n_idx, w), table.dtype), mesh=_sc_mesh(),
                       scratch_types=[], name="sc_gather_rows")
    def gather(t_hbm, i_hbm, o_hbm):
        def body(i_vmem, o_vmem):
            pltpu.sync_copy(t_hbm.at[i_vmem.at[0]], o_vmem)

        pltpu.emit_pipeline(
            body,
            grid=(n_idx // SC_WINDOW,),
            in_specs=[pl.BlockSpec((1, SC_WINDOW), lambda i: (i // nb, i % nb))],
            out_specs=[pl.BlockSpec((SC_WINDOW, w), lambda i: (i, 0))],
            core_axis_name=("core", "subcore"),
            dimension_semantics=(pltpu.PARALLEL,),
        )(i_hbm, o_hbm)

    return gather(table, idx4)


def _forward(x, g_mix, w_in, w_attn_branch, w_ret_branch, w_out, g_ffn, w_group_router, b_group_router,
             w_expert_router, b_expert_router, w_gate, w_up, w_down, g_final, interpret=False):
    batch, seq, d = x.shape
    n = batch * seq
    x2 = x.reshape(n, d)
    nat_a, nat_b, qkv_dilated, wg_bf, wu_bf, wd_bf = _proj(x2, g_mix[0][None, :], w_in[0].astype(BF16), w_gate[0],
                                                           w_up[0], w_down[0], batch, seq, interpret)
    o_attn = _attention(nat_a, qkv_dilated, batch, seq, interpret)
    o_ret = _retention(nat_a, nat_b, batch, seq, interpret)
    pad = LANES - N_EXPERT_GROUPS - N_EXPERTS
    w_router = jnp.concatenate([w_group_router[0], w_expert_router[0], jnp.zeros((d, pad), F32)], axis=-1)
    w_router_hi = w_router.astype(BF16)
    w_router_lo = (w_router - w_router_hi.astype(F32)).astype(BF16)
    w_router2 = jnp.concatenate([w_router_hi, w_router_lo], axis=-1)
    b_router = jnp.concatenate([b_group_router[0], b_expert_router[0], jnp.zeros((pad,), F32)])[None, :]
    x1, h2p, route, route_t, counts = _mix(o_attn, o_ret, nat_b, x2, w_attn_branch[0].astype(BF16),
                                           w_ret_branch[0].astype(BF16), w_out[0].astype(BF16),
                                           g_ffn[0][None, :], w_router2, b_router, interpret)
    idx4, tile_expert, pair_row, n_used, n_rows = _route_plan(route_t, counts, n, interpret)
    xs = _sc_scatter_rows(h2p.reshape(2 * n, SC_ROW_WORDS), idx4, 2 * n_rows)
    ys = _experts(xs.reshape(2, n_rows, SC_ROW_WORDS), tile_expert, pair_row, n_used, wg_bf, wu_bf, wd_bf, interpret)
    yab = _sc_gather_rows(ys.reshape(2 * n_rows, SC_ROW_WORDS), idx4)
    out = _final(x1, yab.reshape(4, n, SC_ROW_WORDS), route, g_final[None, :], interpret)
    return out.reshape(batch, seq, d)


def kernel(x, g_mix, w_in, w_attn_branch, w_ret_branch, w_out, g_ffn, w_group_router, b_group_router,
           w_expert_router, b_expert_router, w_gate, w_up, w_down, g_final):
    return _forward(x, g_mix, w_in, w_attn_branch, w_ret_branch, w_out, g_ffn, w_group_router,
                    b_group_router, w_expert_router, b_expert_router, w_gate, w_up, w_down, g_final)
```
